```python
import math
import jax, jax.numpy as jnp
from jax import lax
import numpy as np

D_MODEL = 1024
BATCH = 8
SEQ = 4096
DEPTH = 4

HEAD_DIM = 64
N_META = 16
BLOCK = 128
NEG_INF = -1e30
SWA_WINDOW = 128
SWA_Q_HEADS = 8
SWA_KV_HEADS = 2
SWA_GROUP = SWA_Q_HEADS // SWA_KV_HEADS
FOX_HEADS = 8
LRU_WIDTH = D_MODEL // 2
LRU_BLOCKS = 8
LRU_BLOCK_DIM = LRU_WIDTH // LRU_BLOCKS
CONV_WIDTH = 4
LRU_C = 8.0
REL_BUCKETS = 32
REL_MAX_DIST = 128
D_FF = ((-(-8 * D_MODEL // 3)) + 255) // 256 * 256
N_BRANCH = 3
SPLIT_SIZES = (
    SWA_Q_HEADS * HEAD_DIM,
    SWA_KV_HEADS * HEAD_DIM,
    SWA_KV_HEADS * HEAD_DIM,
    FOX_HEADS * HEAD_DIM,
    FOX_HEADS * HEAD_DIM,
    FOX_HEADS * HEAD_DIM,
    FOX_HEADS,
    LRU_WIDTH,
    LRU_WIDTH,
    N_BRANCH * D_MODEL,
)
IN_COLS = sum(SPLIT_SIZES)

kernel_name = "hybrid_swa_fox_rglru_block"


def rms_norm(x, g, eps=1e-6):
    xf = x.astype(jnp.float32)
    y = xf * lax.rsqrt(jnp.mean(xf * xf, axis=-1, keepdims=True) + eps)
    return (y * g.astype(jnp.float32)).astype(x.dtype)


def pad_left(a, n):
    return jnp.pad(a, [(0, 0), (n, 0)] + [(0, 0)] * (a.ndim - 2))


def t5_bucket(dist):
    max_exact = REL_BUCKETS // 2
    d = jnp.maximum(dist, 0)
    scaled = jnp.log(jnp.maximum(d, 1).astype(jnp.float32) / max_exact) / math.log(REL_MAX_DIST / max_exact)
    large = jnp.minimum(max_exact + (scaled * (REL_BUCKETS - max_exact)).astype(jnp.int32), REL_BUCKETS - 1)
    return jnp.where(d < max_exact, d, large)


def swa_sink_attention(q, k, v, sinks, rel_table, n_pad):
    b, tp, _, dh = q.shape
    nb = tp // BLOCK
    qb = q.reshape(b, nb, BLOCK, SWA_KV_HEADS, SWA_GROUP, dh)

    def band(a):
        a = a.reshape(b, nb, BLOCK, SWA_KV_HEADS, dh)
        prev = jnp.pad(a, ((0, 0), (1, 0), (0, 0), (0, 0), (0, 0)))[:, :-1]
        return jnp.concatenate([prev, a], axis=2)

    k_band, v_band = band(k), band(v)
    s = jnp.einsum('bnqhgd,bnkhd->bnhgqk', qb, k_band).astype(jnp.float32) * (dh ** -0.5)
    q_idx = jnp.arange(BLOCK)[:, None]
    k_idx = jnp.arange(2 * BLOCK)[None, :]
    dist = q_idx + BLOCK - k_idx
    bias = rel_table.astype(jnp.float32)[t5_bucket(dist)]
    bias = bias.transpose(2, 0, 1).reshape(SWA_KV_HEADS, SWA_GROUP, BLOCK, 2 * BLOCK)
    key_abs = (jnp.arange(nb)[:, None] - 1) * BLOCK + k_idx
    mask = ((dist >= 0) & (dist < SWA_WINDOW))[None] & (key_abs >= n_pad)[:, None, :]
    s = jnp.where(mask[None, :, None, None], s + bias, NEG_INF)
    sink = sinks.astype(jnp.float32).reshape(SWA_KV_HEADS, SWA_GROUP)[None, None, :, :, None, None]
    m = jnp.maximum(jnp.max(s, axis=-1, keepdims=True), sink)
    p = jnp.exp(s - m)
    denom = jnp.sum(p, axis=-1, keepdims=True) + jnp.exp(sink - m)
    p = (p / denom).astype(v.dtype)
    o = jnp.einsum('bnhgqk,bnkhd->bnqhgd', p, v_band)
    return o.reshape(b, tp, SWA_Q_HEADS * dh)


def forgetting_attention(q, k, v, log_f, n_pad):
    b, tp, h, dh = q.shape
    nb = tp // BLOCK
    cum = jnp.cumsum(log_f, axis=1).transpose(0, 2, 1)
    k_pos = jnp.arange(tp)
    qb = q.reshape(b, nb, BLOCK, h, dh).transpose(1, 0, 2, 3, 4)
    cb = cum.reshape(b, h, nb, BLOCK).transpose(2, 0, 1, 3)

    def one_block(args):
        qi, ci, n = args
        s = jnp.einsum('bqhd,bkhd->bhqk', qi, k).astype(jnp.float32) * (dh ** -0.5)
        s = s + ci[..., :, None] - cum[:, :, None, :]
        q_pos = n * BLOCK + jnp.arange(BLOCK)
        mask = (k_pos[None, :] <= q_pos[:, None]) & (k_pos >= n_pad)[None, :]
        s = jnp.where(mask, s, NEG_INF)
        p = jax.nn.softmax(s, axis=-1).astype(v.dtype)
        return jnp.einsum('bhqk,bkhd->bqhd', p, v)

    o = lax.map(one_block, (qb, cb, jnp.arange(nb)))
    return o.transpose(1, 0, 2, 3, 4).reshape(b, tp, h * dh)


def causal_depthwise_conv(x, w, bias):
    t = x.shape[1]
    xp = jnp.pad(x, ((0, 0), (CONV_WIDTH - 1, 0), (0, 0)))
    out = xp[:, 0:t] * w[0]
    for i in range(1, CONV_WIDTH):
        out = out + xp[:, i:i + t] * w[i]
    return out + bias


def rg_lru(x, w_r, b_r, w_i, b_i, lam):
    b, t, c = x.shape
    xb = x.reshape(b, t, LRU_BLOCKS, LRU_BLOCK_DIM)
    r = jax.nn.sigmoid(jnp.einsum('bthi,hij->bthj', xb, w_r).reshape(b, t, c).astype(jnp.float32) + b_r)
    gi = jax.nn.sigmoid(jnp.einsum('bthi,hij->bthj', xb, w_i).reshape(b, t, c).astype(jnp.float32) + b_i)
    log_a = LRU_C * r * jax.nn.log_sigmoid(lam.astype(jnp.float32))
    a = jnp.exp(log_a)
    inp = jnp.sqrt(-jnp.expm1(2.0 * log_a)) * (gi * x.astype(jnp.float32))

    def combine(left, right):
        a1, b1 = left
        a2, b2 = right
        return a1 * a2, a2 * b1 + b2

    _, h = lax.associative_scan(combine, (a, inp), axis=1)
    return h.astype(x.dtype)


def _fwd_setup_inputs(seed: int = 0) -> dict:
    key = jax.random.key(seed)
    ks = jax.random.split(key, 24)
    f32 = jnp.float32
    nrm = lambda k, shape, scale: jax.random.normal(k, shape, f32) * scale
    u = jax.random.uniform(ks[10], (DEPTH, LRU_WIDTH), f32, 0.9, 0.999)
    a0 = u ** (1.0 / LRU_C)
    return {
        "x": nrm(ks[0], (BATCH, SEQ, D_MODEL), 1.0),
        "meta_tokens": nrm(ks[1], (N_META, D_MODEL), 1.0),
        "rel_bias_table": nrm(ks[2], (REL_BUCKETS, SWA_Q_HEADS), 0.5),
        "norm_mix": 1.0 + nrm(ks[3], (DEPTH, D_MODEL), 0.02),
        "w_in": nrm(ks[4], (DEPTH, D_MODEL, IN_COLS), D_MODEL ** -0.5),
        "swa_sinks": nrm(ks[5], (DEPTH, SWA_Q_HEADS), 0.5),
        "fox_forget_bias": 2.0 + 3.0 * jax.random.uniform(ks[6], (DEPTH, FOX_HEADS), f32),
        "conv_w": nrm(ks[7], (DEPTH, CONV_WIDTH, LRU_WIDTH), CONV_WIDTH ** -0.5),
        "conv_b": nrm(ks[8], (DEPTH, LRU_WIDTH), 0.02),
        "lru_w_r": nrm(ks[9], (DEPTH, LRU_BLOCKS, LRU_BLOCK_DIM, LRU_BLOCK_DIM), LRU_BLOCK_DIM ** -0.5),
        "lru_b_r": nrm(ks[11], (DEPTH, LRU_WIDTH), 0.02),
        "lru_w_i": nrm(ks[12], (DEPTH, LRU_BLOCKS, LRU_BLOCK_DIM, LRU_BLOCK_DIM), LRU_BLOCK_DIM ** -0.5),
        "lru_b_i": nrm(ks[13], (DEPTH, LRU_WIDTH), 0.02),
        "lru_lambda": jnp.log(a0) - jnp.log1p(-a0),
        "w_branch": nrm(ks[14], (DEPTH, N_BRANCH, LRU_WIDTH, D_MODEL), LRU_WIDTH ** -0.5),
        "w_out": nrm(ks[15], (DEPTH, D_MODEL, D_MODEL), D_MODEL ** -0.5),
        "norm_ffn": 1.0 + nrm(ks[16], (DEPTH, D_MODEL), 0.02),
        "w_ffn_in": nrm(ks[17], (DEPTH, D_MODEL, 2 * D_FF), D_MODEL ** -0.5),
        "w_ffn_out": nrm(ks[18], (DEPTH, D_FF, D_MODEL), D_FF ** -0.5),
        "norm_final": 1.0 + nrm(ks[19], (D_MODEL,), 0.02),
    }


def _fwd_reference(x, meta_tokens, rel_bias_table, norm_mix, w_in, swa_sinks, fox_forget_bias, conv_w, conv_b,
              lru_w_r, lru_b_r, lru_w_i, lru_b_i, lru_lambda, w_branch, w_out, norm_ffn, w_ffn_in,
              w_ffn_out, norm_final):
    b = x.shape[0]
    meta = jnp.broadcast_to(meta_tokens.astype(x.dtype)[None], (b, N_META, D_MODEL))
    h = jnp.concatenate([meta, x], axis=1)
    t = h.shape[1]
    n_pad = (-t) % BLOCK
    split_points = [int(p) for p in np.cumsum(SPLIT_SIZES)[:-1]]
    for l in range(DEPTH):
        u = rms_norm(h, norm_mix[l])
        proj = u @ w_in[l]
        qa, ka, va, qf, kf, vf, fl, xc, yc, gates = jnp.split(proj, split_points, axis=-1)
        o_a = swa_sink_attention(
            pad_left(qa.reshape(b, t, SWA_Q_HEADS, HEAD_DIM), n_pad),
            pad_left(ka.reshape(b, t, SWA_KV_HEADS, HEAD_DIM), n_pad),
            pad_left(va.reshape(b, t, SWA_KV_HEADS, HEAD_DIM), n_pad),
            swa_sinks[l], rel_bias_table, n_pad)[:, n_pad:]
        log_f = jax.nn.log_sigmoid(fl.astype(jnp.float32) + fox_forget_bias[l].astype(jnp.float32))
        o_f = forgetting_attention(
            pad_left(qf.reshape(b, t, FOX_HEADS, HEAD_DIM), n_pad),
            pad_left(kf.reshape(b, t, FOX_HEADS, HEAD_DIM), n_pad),
            pad_left(vf.reshape(b, t, FOX_HEADS, HEAD_DIM), n_pad),
            pad_left(log_f, n_pad), n_pad)[:, n_pad:]
        xc = causal_depthwise_conv(xc, conv_w[l], conv_b[l])
        o_c = rg_lru(xc, lru_w_r[l], lru_b_r[l], lru_w_i[l], lru_b_i[l], lru_lambda[l]) * jax.nn.gelu(yc)
        g = jax.nn.sigmoid(gates).reshape(b, t, N_BRANCH, D_MODEL)
        merged = (g[:, :, 0] * (o_a @ w_branch[l, 0])
                  + g[:, :, 1] * (o_f @ w_branch[l, 1])
                  + g[:, :, 2] * (o_c @ w_branch[l, 2]))
        h = h + merged @ w_out[l]
        u = rms_norm(h, norm_ffn[l])
        gate_ff, up_ff = jnp.split(u @ w_ffn_in[l], 2, axis=-1)
        h = h + (jax.nn.silu(gate_ff) * up_ff) @ w_ffn_out[l]
    return rms_norm(h, norm_final)[:, N_META:]


import jax as _jax
import jax.numpy as _jnp

TWIN_FORMAT = 'train_step'
FWD_PARAMS = ['x', 'meta_tokens', 'rel_bias_table', 'norm_mix', 'w_in', 'swa_sinks', 'fox_forget_bias', 'conv_w', 'conv_b', 'lru_w_r', 'lru_b_r', 'lru_w_i', 'lru_b_i', 'lru_lambda', 'w_branch', 'w_out', 'norm_ffn', 'w_ffn_in', 'w_ffn_out', 'norm_final']
TWIN_WEIGHTS = ['meta_tokens', 'rel_bias_table', 'norm_mix', 'w_in', 'swa_sinks', 'fox_forget_bias', 'conv_w', 'conv_b', 'lru_w_r', 'lru_b_r', 'lru_w_i', 'lru_b_i', 'lru_lambda', 'w_branch', 'w_out', 'norm_ffn', 'w_ffn_in', 'w_ffn_out', 'norm_final']
TWIN_DIFF_INPUT = 'x'
TWIN_INPUTS = ['x', 'meta_tokens', 'rel_bias_table', 'norm_mix', 'w_in', 'swa_sinks', 'fox_forget_bias', 'conv_w', 'conv_b', 'lru_w_r', 'lru_b_r', 'lru_w_i', 'lru_b_i', 'lru_lambda', 'w_branch', 'w_out', 'norm_ffn', 'w_ffn_in', 'w_ffn_out', 'norm_final', 'loss_target', 'm_meta_tokens', 'm_rel_bias_table', 'm_norm_mix', 'm_w_in', 'm_swa_sinks', 'm_fox_forget_bias', 'm_conv_w', 'm_conv_b', 'm_lru_w_r', 'm_lru_b_r', 'm_lru_w_i', 'm_lru_b_i', 'm_lru_lambda', 'm_w_branch', 'm_w_out', 'm_norm_ffn', 'm_w_ffn_in', 'm_w_ffn_out', 'm_norm_final', 'v_meta_tokens', 'v_rel_bias_table', 'v_norm_mix', 'v_w_in', 'v_swa_sinks', 'v_fox_forget_bias', 'v_conv_w', 'v_conv_b', 'v_lru_w_r', 'v_lru_b_r', 'v_lru_w_i', 'v_lru_b_i', 'v_lru_lambda', 'v_w_branch', 'v_w_out', 'v_norm_ffn', 'v_w_ffn_in', 'v_w_ffn_out', 'v_norm_final']
TWIN_OUTPUTS = ['loss', 'grad_x', 'grad_meta_tokens', 'grad_rel_bias_table', 'grad_norm_mix', 'grad_w_in', 'grad_swa_sinks', 'grad_fox_forget_bias', 'grad_conv_w', 'grad_conv_b', 'grad_lru_w_r', 'grad_lru_b_r', 'grad_lru_w_i', 'grad_lru_b_i', 'grad_lru_lambda', 'grad_w_branch', 'grad_w_out', 'grad_norm_ffn', 'grad_w_ffn_in', 'grad_w_ffn_out', 'grad_norm_final', 'delta_meta_tokens', 'delta_rel_bias_table', 'delta_norm_mix', 'delta_w_in', 'delta_swa_sinks', 'delta_fox_forget_bias', 'delta_conv_w', 'delta_conv_b', 'delta_lru_w_r', 'delta_lru_b_r', 'delta_lru_w_i', 'delta_lru_b_i', 'delta_lru_lambda', 'delta_w_branch', 'delta_w_out', 'delta_norm_ffn', 'delta_w_ffn_in', 'delta_w_ffn_out', 'delta_norm_final', 'new_m_meta_tokens', 'new_m_rel_bias_table', 'new_m_norm_mix', 'new_m_w_in', 'new_m_swa_sinks', 'new_m_fox_forget_bias', 'new_m_conv_w', 'new_m_conv_b', 'new_m_lru_w_r', 'new_m_lru_b_r', 'new_m_lru_w_i', 'new_m_lru_b_i', 'new_m_lru_lambda', 'new_m_w_branch', 'new_m_w_out', 'new_m_norm_ffn', 'new_m_w_ffn_in', 'new_m_w_ffn_out', 'new_m_norm_final', 'new_v_meta_tokens', 'new_v_rel_bias_table', 'new_v_norm_mix', 'new_v_w_in', 'new_v_swa_sinks', 'new_v_fox_forget_bias', 'new_v_conv_w', 'new_v_conv_b', 'new_v_lru_w_r', 'new_v_lru_b_r', 'new_v_lru_w_i', 'new_v_lru_b_i', 'new_v_lru_lambda', 'new_v_w_branch', 'new_v_w_out', 'new_v_norm_ffn', 'new_v_w_ffn_in', 'new_v_w_ffn_out', 'new_v_norm_final']
TWIN_LEAF_KINDS = {'loss': 'loss', 'grad_x': 'grad_x', 'grad_meta_tokens': 'grad_w', 'grad_rel_bias_table': 'grad_w', 'grad_norm_mix': 'grad_w', 'grad_w_in': 'grad_w', 'grad_swa_sinks': 'grad_w', 'grad_fox_forget_bias': 'grad_w', 'grad_conv_w': 'grad_w', 'grad_conv_b': 'grad_w', 'grad_lru_w_r': 'grad_w', 'grad_lru_b_r': 'grad_w', 'grad_lru_w_i': 'grad_w', 'grad_lru_b_i': 'grad_w', 'grad_lru_lambda': 'grad_w', 'grad_w_branch': 'grad_w', 'grad_w_out': 'grad_w', 'grad_norm_ffn': 'grad_w', 'grad_w_ffn_in': 'grad_w', 'grad_w_ffn_out': 'grad_w', 'grad_norm_final': 'grad_w', 'delta_meta_tokens': 'delta_w', 'delta_rel_bias_table': 'delta_w', 'delta_norm_mix': 'delta_w', 'delta_w_in': 'delta_w', 'delta_swa_sinks': 'delta_w', 'delta_fox_forget_bias': 'delta_w', 'delta_conv_w': 'delta_w', 'delta_conv_b': 'delta_w', 'delta_lru_w_r': 'delta_w', 'delta_lru_b_r': 'delta_w', 'delta_lru_w_i': 'delta_w', 'delta_lru_b_i': 'delta_w', 'delta_lru_lambda': 'delta_w', 'delta_w_branch': 'delta_w', 'delta_w_out': 'delta_w', 'delta_norm_ffn': 'delta_w', 'delta_w_ffn_in': 'delta_w', 'delta_w_ffn_out': 'delta_w', 'delta_norm_final': 'delta_w', 'new_m_meta_tokens': 'new_m', 'new_m_rel_bias_table': 'new_m', 'new_m_norm_mix': 'new_m', 'new_m_w_in': 'new_m', 'new_m_swa_sinks': 'new_m', 'new_m_fox_forget_bias': 'new_m', 'new_m_conv_w': 'new_m', 'new_m_conv_b': 'new_m', 'new_m_lru_w_r': 'new_m', 'new_m_lru_b_r': 'new_m', 'new_m_lru_w_i': 'new_m', 'new_m_lru_b_i': 'new_m', 'new_m_lru_lambda': 'new_m', 'new_m_w_branch': 'new_m', 'new_m_w_out': 'new_m', 'new_m_norm_ffn': 'new_m', 'new_m_w_ffn_in': 'new_m', 'new_m_w_ffn_out': 'new_m', 'new_m_norm_final': 'new_m', 'new_v_meta_tokens': 'new_v', 'new_v_rel_bias_table': 'new_v', 'new_v_norm_mix': 'new_v', 'new_v_w_in': 'new_v', 'new_v_swa_sinks': 'new_v', 'new_v_fox_forget_bias': 'new_v', 'new_v_conv_w': 'new_v', 'new_v_conv_b': 'new_v', 'new_v_lru_w_r': 'new_v', 'new_v_lru_b_r': 'new_v', 'new_v_lru_w_i': 'new_v', 'new_v_lru_b_i': 'new_v', 'new_v_lru_lambda': 'new_v', 'new_v_w_branch': 'new_v', 'new_v_w_out': 'new_v', 'new_v_norm_ffn': 'new_v', 'new_v_w_ffn_in': 'new_v', 'new_v_w_ffn_out': 'new_v', 'new_v_norm_final': 'new_v'}


def _forward(args):
    return _fwd_reference(*[args[k] for k in FWD_PARAMS])


def _output_shape():
    out = _jax.eval_shape(lambda: _forward(_fwd_setup_inputs(0)))
    return out.shape, out.dtype

N_MICROBATCH = 1
ADAM_LR = 0.001
ADAM_B1 = 0.9
ADAM_B2 = 0.999
ADAM_EPS = 1e-08
ADAM_WD = 0.01
ADAM_STEP = 10
PER_EXAMPLE_BATCH_AXIS = {'x': 0, 'loss_target': 0}
SHARED_INPUTS = []
_WEIGHT_DTYPES = {'meta_tokens': _jnp.float32, 'rel_bias_table': _jnp.float32, 'norm_mix': _jnp.float32, 'w_in': _jnp.float32, 'swa_sinks': _jnp.float32, 'fox_forget_bias': _jnp.float32, 'conv_w': _jnp.float32, 'conv_b': _jnp.float32, 'lru_w_r': _jnp.float32, 'lru_b_r': _jnp.float32, 'lru_w_i': _jnp.float32, 'lru_b_i': _jnp.float32, 'lru_lambda': _jnp.float32, 'w_branch': _jnp.float32, 'w_out': _jnp.float32, 'norm_ffn': _jnp.float32, 'w_ffn_in': _jnp.float32, 'w_ffn_out': _jnp.float32, 'norm_final': _jnp.float32}
MOMENT_SCALE = {'meta_tokens': 1.146533e-02, 'rel_bias_table': 7.913985e-02, 'norm_mix': 9.271599e-02, 'w_in': 3.770387e-02, 'swa_sinks': 9.540217e-03, 'fox_forget_bias': 1.860340e-01, 'conv_w': 9.439806e-02, 'conv_b': 9.120233e-01, 'lru_w_r': 3.247930e-02, 'lru_b_r': 2.362310e-02, 'lru_w_i': 5.989788e-02, 'lru_b_i': 3.392444e-02, 'lru_lambda': 4.410222e-02, 'w_branch': 4.132196e-02, 'w_out': 7.081289e-02, 'norm_ffn': 1.308358e-01, 'w_ffn_in': 5.550944e-02, 'w_ffn_out': 9.074215e-02, 'norm_final': 3.194353e+01}


def _to_microbatches(a, axis):
    t = _jnp.moveaxis(a, axis, 0)
    t = t.reshape((N_MICROBATCH, t.shape[0] // N_MICROBATCH) + t.shape[1:])
    return _jnp.moveaxis(t, 1, axis + 1)


def setup_inputs(seed: int = 0) -> dict:
    inp = _fwd_setup_inputs(seed)
    key = _jax.random.fold_in(_jax.random.key(seed), 7919)
    shape, _ = _output_shape()
    out = dict(inp)
    out["loss_target"] = _jax.random.normal(_jax.random.fold_in(key, 0), shape, _jnp.float32)
    for i, name in enumerate(TWIN_WEIGHTS):
        w = inp[name].astype(_jnp.float32)
        if MOMENT_SCALE is None:
            s = _jnp.sqrt(_jnp.mean(_jnp.square(w)) + 1e-30)
        else:
            s = MOMENT_SCALE[name]
        km, kv = _jax.random.split(_jax.random.fold_in(key, i + 1))
        out[name] = w
        out["m_" + name] = s * _jax.random.normal(km, w.shape, _jnp.float32)
        out["v_" + name] = (s * s) * _jax.random.uniform(kv, w.shape, _jnp.float32, 0.5, 1.5)
    if N_MICROBATCH > 1:
        for name, axis in PER_EXAMPLE_BATCH_AXIS.items():
            out[name] = _to_microbatches(out[name], axis)
    return {'x': out['x'], 'meta_tokens': out['meta_tokens'], 'rel_bias_table': out['rel_bias_table'], 'norm_mix': out['norm_mix'], 'w_in': out['w_in'], 'swa_sinks': out['swa_sinks'], 'fox_forget_bias': out['fox_forget_bias'], 'conv_w': out['conv_w'], 'conv_b': out['conv_b'], 'lru_w_r': out['lru_w_r'], 'lru_b_r': out['lru_b_r'], 'lru_w_i': out['lru_w_i'], 'lru_b_i': out['lru_b_i'], 'lru_lambda': out['lru_lambda'], 'w_branch': out['w_branch'], 'w_out': out['w_out'], 'norm_ffn': out['norm_ffn'], 'w_ffn_in': out['w_ffn_in'], 'w_ffn_out': out['w_ffn_out'], 'norm_final': out['norm_final'], 'loss_target': out['loss_target'], 'm_meta_tokens': out['m_meta_tokens'], 'm_rel_bias_table': out['m_rel_bias_table'], 'm_norm_mix': out['m_norm_mix'], 'm_w_in': out['m_w_in'], 'm_swa_sinks': out['m_swa_sinks'], 'm_fox_forget_bias': out['m_fox_forget_bias'], 'm_conv_w': out['m_conv_w'], 'm_conv_b': out['m_conv_b'], 'm_lru_w_r': out['m_lru_w_r'], 'm_lru_b_r': out['m_lru_b_r'], 'm_lru_w_i': out['m_lru_w_i'], 'm_lru_b_i': out['m_lru_b_i'], 'm_lru_lambda': out['m_lru_lambda'], 'm_w_branch': out['m_w_branch'], 'm_w_out': out['m_w_out'], 'm_norm_ffn': out['m_norm_ffn'], 'm_w_ffn_in': out['m_w_ffn_in'], 'm_w_ffn_out': out['m_w_ffn_out'], 'm_norm_final': out['m_norm_final'], 'v_meta_tokens': out['v_meta_tokens'], 'v_rel_bias_table': out['v_rel_bias_table'], 'v_norm_mix': out['v_norm_mix'], 'v_w_in': out['v_w_in'], 'v_swa_sinks': out['v_swa_sinks'], 'v_fox_forget_bias': out['v_fox_forget_bias'], 'v_conv_w': out['v_conv_w'], 'v_conv_b': out['v_conv_b'], 'v_lru_w_r': out['v_lru_w_r'], 'v_lru_b_r': out['v_lru_b_r'], 'v_lru_w_i': out['v_lru_w_i'], 'v_lru_b_i': out['v_lru_b_i'], 'v_lru_lambda': out['v_lru_lambda'], 'v_w_branch': out['v_w_branch'], 'v_w_out': out['v_w_out'], 'v_norm_ffn': out['v_norm_ffn'], 'v_w_ffn_in': out['v_w_ffn_in'], 'v_w_ffn_out': out['v_w_ffn_out'], 'v_norm_final': out['v_norm_final']}


def _loss(weights, diff, rest, loss_target):
    with _jax.named_scope("forward"):
        args = {**rest, TWIN_DIFF_INPUT: diff, **{k: w.astype(_WEIGHT_DTYPES[k]) for k, w in weights.items()}}
        y = _forward(args)
    with _jax.named_scope("loss_head"):
        err = _jnp.square(y.astype(_jnp.float32) - loss_target)
        return 0.5 * _jnp.sum(_jnp.mean(err, axis=-1)) if err.ndim else 0.5 * err


def _adamw(w, g, m, v):
    m = ADAM_B1 * m + (1.0 - ADAM_B1) * g
    v = ADAM_B2 * v + (1.0 - ADAM_B2) * _jnp.square(g)
    m_hat = m / (1.0 - ADAM_B1 ** ADAM_STEP)
    v_hat = v / (1.0 - ADAM_B2 ** ADAM_STEP)
    delta = -ADAM_LR * (m_hat / (_jnp.sqrt(v_hat) + ADAM_EPS) + ADAM_WD * w)
    return delta, m, v


def reference(x, meta_tokens, rel_bias_table, norm_mix, w_in, swa_sinks, fox_forget_bias, conv_w, conv_b, lru_w_r, lru_b_r, lru_w_i, lru_b_i, lru_lambda, w_branch, w_out, norm_ffn, w_ffn_in, w_ffn_out, norm_final, loss_target, m_meta_tokens, m_rel_bias_table, m_norm_mix, m_w_in, m_swa_sinks, m_fox_forget_bias, m_conv_w, m_conv_b, m_lru_w_r, m_lru_b_r, m_lru_w_i, m_lru_b_i, m_lru_lambda, m_w_branch, m_w_out, m_norm_ffn, m_w_ffn_in, m_w_ffn_out, m_norm_final, v_meta_tokens, v_rel_bias_table, v_norm_mix, v_w_in, v_swa_sinks, v_fox_forget_bias, v_conv_w, v_conv_b, v_lru_w_r, v_lru_b_r, v_lru_w_i, v_lru_b_i, v_lru_lambda, v_w_branch, v_w_out, v_norm_ffn, v_w_ffn_in, v_w_ffn_out, v_norm_final):
    given = dict(x=x, meta_tokens=meta_tokens, rel_bias_table=rel_bias_table, norm_mix=norm_mix, w_in=w_in, swa_sinks=swa_sinks, fox_forget_bias=fox_forget_bias, conv_w=conv_w, conv_b=conv_b, lru_w_r=lru_w_r, lru_b_r=lru_b_r, lru_w_i=lru_w_i, lru_b_i=lru_b_i, lru_lambda=lru_lambda, w_branch=w_branch, w_out=w_out, norm_ffn=norm_ffn, w_ffn_in=w_ffn_in, w_ffn_out=w_ffn_out, norm_final=norm_final, loss_target=loss_target, m_meta_tokens=m_meta_tokens, m_rel_bias_table=m_rel_bias_table, m_norm_mix=m_norm_mix, m_w_in=m_w_in, m_swa_sinks=m_swa_sinks, m_fox_forget_bias=m_fox_forget_bias, m_conv_w=m_conv_w, m_conv_b=m_conv_b, m_lru_w_r=m_lru_w_r, m_lru_b_r=m_lru_b_r, m_lru_w_i=m_lru_w_i, m_lru_b_i=m_lru_b_i, m_lru_lambda=m_lru_lambda, m_w_branch=m_w_branch, m_w_out=m_w_out, m_norm_ffn=m_norm_ffn, m_w_ffn_in=m_w_ffn_in, m_w_ffn_out=m_w_ffn_out, m_norm_final=m_norm_final, v_meta_tokens=v_meta_tokens, v_rel_bias_table=v_rel_bias_table, v_norm_mix=v_norm_mix, v_w_in=v_w_in, v_swa_sinks=v_swa_sinks, v_fox_forget_bias=v_fox_forget_bias, v_conv_w=v_conv_w, v_conv_b=v_conv_b, v_lru_w_r=v_lru_w_r, v_lru_b_r=v_lru_b_r, v_lru_w_i=v_lru_w_i, v_lru_b_i=v_lru_b_i, v_lru_lambda=v_lru_lambda, v_w_branch=v_w_branch, v_w_out=v_w_out, v_norm_ffn=v_norm_ffn, v_w_ffn_in=v_w_ffn_in, v_w_ffn_out=v_w_ffn_out, v_norm_final=v_norm_final)
    weights = {n: given[n] for n in TWIN_WEIGHTS}
    shared = {n: given[n] for n in SHARED_INPUTS}
    per_example = {n: given[n] for n in ['x']}
    grad_fn = _jax.value_and_grad(_loss, argnums=(0, 1))

    def one_microbatch(ex, loss_target):
        ex = dict(ex)
        diff = ex.pop(TWIN_DIFF_INPUT)
        return grad_fn(weights, diff, {**shared, **ex}, loss_target)

    if N_MICROBATCH == 1:
        loss, (grad_w, grad_x) = one_microbatch(per_example, given["loss_target"])
    else:
        def body(carry, xs):
            loss_sum, grad_sum = carry
            l_k, (gw_k, gx_k) = one_microbatch(xs[0], xs[1])
            with _jax.named_scope("update"):
                return (loss_sum + l_k, _jax.tree.map(_jnp.add, grad_sum, gw_k)), gx_k

        init = (_jnp.zeros((), _jnp.float32), _jax.tree.map(_jnp.zeros_like, weights))
        (loss, grad_w), grad_x = _jax.lax.scan(body, init, (per_example, given["loss_target"]))
    with _jax.named_scope("update"):
        delta_w, new_m, new_v = {}, {}, {}
        for n in TWIN_WEIGHTS:
            delta_w[n], new_m[n], new_v[n] = _adamw(weights[n], grad_w[n], given["m_" + n], given["v_" + n])
    return (loss, grad_x, *[grad_w[n] for n in TWIN_WEIGHTS], *[delta_w[n] for n in TWIN_WEIGHTS],
            *[new_m[n] for n in TWIN_WEIGHTS], *[new_v[n] for n in TWIN_WEIGHTS])
```

```python
import functools
import math

import numpy as np
import jax
import jax.numpy as jnp
from jax import lax
from jax.experimental import pallas as pl
from jax.experimental.pallas import tpu as pltpu

F32, BF16 = jnp.float32, jnp.bfloat16
MESH = pl.DeviceIdType.MESH
ANY = pl.BlockSpec(memory_space=pl.ANY)
SMEM = pl.BlockSpec(memory_space=pltpu.SMEM)

D = 1024
DEPTH = 4
BLK = 128
N_META = 16
NPAD = 112
NH = 8
LW = 512
DFF = 2816
EPS = 1e-6
NEG = -1e30
SCALE = 0.125
LRU_C = 8.0
REL_BUCKETS = 32
N_SHARD = 4
QA, QF, KF, VF, XC, YC, GT, KA, VA, FL, INP = 0, 512, 1024, 1536, 2048, 2560, 3072, 6144, 6272, 6400, 6656
IN_COLS = 6408
VMEM_LIMIT = 48 * 1024 * 1024

ADAM_LR, ADAM_B1, ADAM_B2, ADAM_EPS, ADAM_WD, ADAM_STEP = 0.001, 0.9, 0.999, 1e-08, 0.01, 10


def _cp(*sem):
    return pltpu.CompilerParams(dimension_semantics=sem or None, vmem_limit_bytes=VMEM_LIMIT)


def _pick(n, prefs):
    for p in prefs:
        if n % p == 0:
            return p
    return n


def _rt(T):
    return _pick(T, (384, 128))


def _sigmoid(z):
    return 1.0 / (1.0 + jnp.exp(-z))


def _log_sigmoid(z):
    return jnp.minimum(z, 0.0) - jnp.log(1.0 + jnp.exp(-jnp.abs(z)))


def _gelu(y):
    c = math.sqrt(2.0 / math.pi)
    return 0.5 * y * (1.0 + jnp.tanh(c * (y + 0.044715 * y * y * y)))


def _gelu_grad(y):
    c = math.sqrt(2.0 / math.pi)
    t = jnp.tanh(c * (y + 0.044715 * y * y * y))
    return 0.5 * (1.0 + t) + 0.5 * y * (1.0 - t * t) * c * (1.0 + 3.0 * 0.044715 * y * y)


def _neg_expm1(z):
    series = -z * (1.0 + z * (0.5 + z * (1.0 / 6.0 + z * (1.0 / 24.0 + z * (1.0 / 120.0)))))
    return jnp.where(z > -0.1, series, 1.0 - jnp.exp(z))


def _dot(a, b, ca, cb):
    return lax.dot_general(a, b, (((ca,), (cb,)), ((), ())), preferred_element_type=F32)


def _mm(a, b, *, ta=False, tb=False, res=None, out_dtype=F32, tm, tn, tk, name):
    M, K = (a.shape[1], a.shape[0]) if ta else a.shape
    N = b.shape[0] if tb else b.shape[1]
    assert (b.shape[1] if tb else b.shape[0]) == K and M % tm == 0 and N % tn == 0 and K % tk == 0, (name, a.shape, b.shape)
    nk = K // tk
    ca, cb = (0 if ta else 1), (1 if tb else 0)

    def body(*refs):
        if res is not None:
            a_ref, b_ref, r_ref, o_ref = refs[:4]
        else:
            a_ref, b_ref, o_ref = refs[:3]
        part = _dot(a_ref[...].astype(BF16), b_ref[...].astype(BF16), ca, cb)

        def fin(acc):
            if res is not None:
                acc = acc + r_ref[...]
            o_ref[...] = acc.astype(out_dtype)

        if nk == 1:
            fin(part)
        else:
            acc_ref = refs[-1]
            k = pl.program_id(2)

            @pl.when(k == 0)
            def _():
                acc_ref[...] = part

            @pl.when(k > 0)
            def _():
                acc_ref[...] += part

            @pl.when(k == nk - 1)
            def _():
                fin(acc_ref[...])

    a_spec = pl.BlockSpec((tk, tm), lambda i, j, k: (k, i)) if ta else pl.BlockSpec((tm, tk), lambda i, j, k: (i, k))
    b_spec = pl.BlockSpec((tn, tk), lambda i, j, k: (j, k)) if tb else pl.BlockSpec((tk, tn), lambda i, j, k: (k, j))
    o_spec = pl.BlockSpec((tm, tn), lambda i, j, k: (i, j))
    in_specs, ops = [a_spec, b_spec], [a, b]
    if res is not None:
        in_specs.append(o_spec)
        ops.append(res)
    return pl.pallas_call(
        body, grid=(M // tm, N // tn, nk), in_specs=in_specs, out_specs=o_spec,
        out_shape=jax.ShapeDtypeStruct((M, N), out_dtype),
        scratch_shapes=[pltpu.VMEM((tm, tn), F32)] if nk > 1 else [],
        compiler_params=_cp("parallel", "parallel", "arbitrary"), name=name)(*ops)


def _rms_fwd(h, g, name):
    T = h.shape[0]
    tr = _rt(T)

    def body(h_ref, g_ref, u_ref):
        x = h_ref[...]
        r = lax.rsqrt(jnp.mean(x * x, axis=-1, keepdims=True) + EPS)
        u_ref[...] = (x * r * g_ref[...]).astype(BF16)

    return pl.pallas_call(
        body, grid=(T // tr,),
        in_specs=[pl.BlockSpec((tr, D), lambda i: (i, 0)), pl.BlockSpec((1, D), lambda i: (0, 0))],
        out_specs=pl.BlockSpec((tr, D), lambda i: (i, 0)), out_shape=jax.ShapeDtypeStruct((T, D), BF16),
        compiler_params=_cp("parallel"), name=name)(h, g.reshape(1, D))


def _rms_bwd(du, h, g, dres, name):
    T = h.shape[0]
    tr = _rt(T)

    def body(du_ref, h_ref, g_ref, dres_ref, dh_ref, dhb_ref, dg_ref):
        x = h_ref[...]
        r = lax.rsqrt(jnp.mean(x * x, axis=-1, keepdims=True) + EPS)
        xh = x * r
        dy = du_ref[...]
        dxh = dy * g_ref[...]
        dx = r * (dxh - xh * jnp.mean(dxh * xh, axis=-1, keepdims=True))
        dh = dres_ref[...] + dx
        dh_ref[...] = dh
        dhb_ref[...] = dh.astype(BF16)
        part = jnp.sum(dy * xh, axis=0, keepdims=True)

        @pl.when(pl.program_id(0) == 0)
        def _():
            dg_ref[...] = part

        @pl.when(pl.program_id(0) > 0)
        def _():
            dg_ref[...] += part

    row = pl.BlockSpec((tr, D), lambda i: (i, 0))
    vec = pl.BlockSpec((1, D), lambda i: (0, 0))
    return pl.pallas_call(
        body, grid=(T // tr,), in_specs=[row, row, vec, row], out_specs=[row, row, vec],
        out_shape=[jax.ShapeDtypeStruct((T, D), F32), jax.ShapeDtypeStruct((T, D), BF16), jax.ShapeDtypeStruct((1, D), F32)],
        compiler_params=_cp("arbitrary"), name=name)(du, h, g.reshape(1, D), dres)


def _loss_head(h, tgt, g, name):
    T = h.shape[0]
    nb = T // BLK

    def body(h_ref, t_ref, g_ref, dh_ref, dhb_ref, dg_ref, loss_ref):
        i = pl.program_id(0)
        x = h_ref[...]
        r = lax.rsqrt(jnp.mean(x * x, axis=-1, keepdims=True) + EPS)
        xh = x * r
        gv = g_ref[...]
        tok = i >= 1
        err = jnp.where(tok, xh * gv - t_ref[...], 0.0)
        dy = err * (1.0 / D)
        dxh = dy * gv
        dx = r * (dxh - xh * jnp.mean(dxh * xh, axis=-1, keepdims=True))
        dh_ref[...] = dx
        dhb_ref[...] = dx.astype(BF16)
        dg = jnp.sum(dy * xh, axis=0, keepdims=True)
        ls = jnp.zeros((1, BLK), F32) + jnp.sum(err * err) * (0.5 / D)

        @pl.when(i == 0)
        def _():
            dg_ref[...] = dg
            loss_ref[...] = ls

        @pl.when(i > 0)
        def _():
            dg_ref[...] += dg
            loss_ref[...] += ls

    row = pl.BlockSpec((BLK, D), lambda i: (i, 0))
    vec = pl.BlockSpec((1, D), lambda i: (0, 0))
    return pl.pallas_call(
        body, grid=(nb,),
        in_specs=[row, pl.BlockSpec((BLK, D), lambda i: (jnp.maximum(i - 1, 0), 0)), vec],
        out_specs=[row, row, vec, pl.BlockSpec((1, BLK), lambda i: (0, 0))],
        out_shape=[jax.ShapeDtypeStruct((T, D), F32), jax.ShapeDtypeStruct((T, D), BF16),
                   jax.ShapeDtypeStruct((1, D), F32), jax.ShapeDtypeStruct((1, BLK), F32)],
        compiler_params=_cp("arbitrary"), name=name)(h, tgt, g.reshape(1, D))


def _bucket_table():
    q = np.arange(BLK)[:, None]
    k = np.arange(2 * BLK)[None, :]
    d = np.maximum(q + BLK - k, 0)
    max_exact = REL_BUCKETS // 2
    scaled = np.log(np.maximum(d, 1).astype(np.float32) / np.float32(max_exact)) / np.float32(math.log(128 / max_exact))
    large = np.minimum(max_exact + (scaled.astype(np.float32) * (REL_BUCKETS - max_exact)).astype(np.int32), REL_BUCKETS - 1)
    return np.where(d < max_exact, d, large).astype(np.int32)


def _bias_build(table, bucket, name):
    def body(t_ref, bk_ref, o_ref):
        bk = bk_ref[...]
        for h in range(NH):
            acc = jnp.zeros((BLK, 2 * BLK), F32)
            for b in range(REL_BUCKETS):
                acc = jnp.where(bk == b, t_ref[b, h], acc)
            o_ref[h] = acc

    return pl.pallas_call(
        body, in_specs=[SMEM, pl.BlockSpec(memory_space=pltpu.VMEM)], out_specs=pl.BlockSpec(memory_space=pltpu.VMEM),
        out_shape=jax.ShapeDtypeStruct((NH, BLK, 2 * BLK), F32), compiler_params=_cp(), name=name)(table, bucket)


def _bias_bwd(dbias, bucket, name):
    def body(d_ref, bk_ref, o_ref):
        bk = bk_ref[...]
        for h in range(NH):
            dh = d_ref[h]
            for b in range(REL_BUCKETS):
                o_ref[b, h] = jnp.sum(jnp.where(bk == b, dh, 0.0))

    return pl.pallas_call(
        body, in_specs=[pl.BlockSpec(memory_space=pltpu.VMEM)] * 2, out_specs=SMEM,
        out_shape=jax.ShapeDtypeStruct((REL_BUCKETS, NH), F32), compiler_params=_cp(), name=name)(dbias, bucket)


def _swa_specs(nq_cols):
    prev = lambda n: jnp.maximum(n - 1, 0)
    return [
        pl.BlockSpec((BLK, nq_cols), lambda n: (n, QA // nq_cols)),
        pl.BlockSpec((BLK, BLK), lambda n: (prev(n), KA // BLK)), pl.BlockSpec((BLK, BLK), lambda n: (n, KA // BLK)),
        pl.BlockSpec((BLK, BLK), lambda n: (prev(n), VA // BLK)), pl.BlockSpec((BLK, BLK), lambda n: (n, VA // BLK)),
    ]


def _swa_mask(n):
    row = lax.broadcasted_iota(jnp.int32, (BLK, 2 * BLK), 0)
    col = lax.broadcasted_iota(jnp.int32, (BLK, 2 * BLK), 1)
    dist = row + BLK - col
    return (dist >= 0) & (dist < BLK) & ((n - 1) * BLK + col >= NPAD)


def _swa_probs(qm, ksel, mask, bias_h, sink):
    s = _dot(qm, ksel, 1, 1) * SCALE
    s = jnp.where(mask, s + bias_h, NEG)
    m = jnp.maximum(jnp.max(s, axis=-1, keepdims=True), sink)
    p = jnp.exp(s - m)
    psink = jnp.exp(sink - m)
    inv = 1.0 / (jnp.sum(p, axis=-1, keepdims=True) + psink)
    return p * inv, psink * inv


def _swa_fwd(proj, bias, sinks, name):
    T = proj.shape[0]
    nb = T // BLK

    def body(sk_ref, q_ref, kp_ref, kc_ref, vp_ref, vc_ref, b_ref, o_ref):
        n = pl.program_id(0)
        lo = lax.broadcasted_iota(jnp.int32, (1, BLK), 1) < 64
        kb = jnp.concatenate([kp_ref[...], kc_ref[...]], axis=0)
        vb = jnp.concatenate([vp_ref[...], vc_ref[...]], axis=0)
        kbs = (kb.astype(BF16), pltpu.roll(kb, 64, 1).astype(BF16))
        vbs = (vb, pltpu.roll(vb, 64, 1))
        mask = _swa_mask(n)
        outs = []
        for pr in range(NH // 2):
            qp = q_ref[:, pr * BLK:(pr + 1) * BLK]
            kv = pr // 2
            acc = jnp.zeros((BLK, BLK), F32)
            for e in range(2):
                lm = lo if e == 0 else jnp.logical_not(lo)
                sw = 0 if kv == e else 1
                qm = jnp.where(lm, qp, 0.0).astype(BF16)
                pn, _ = _swa_probs(qm, kbs[sw], mask, b_ref[2 * pr + e], sk_ref[2 * pr + e])
                acc = acc + _dot(pn.astype(BF16), jnp.where(lm, vbs[sw], 0.0).astype(BF16), 1, 0)
            outs.append(acc)
        o_ref[...] = jnp.concatenate(outs, axis=1).astype(BF16)

    return pl.pallas_call(
        body, grid=(nb,),
        in_specs=[SMEM] + _swa_specs(512) + [pl.BlockSpec((NH, BLK, 2 * BLK), lambda n: (0, 0, 0))],
        out_specs=pl.BlockSpec((BLK, 512), lambda n: (n, 0)), out_shape=jax.ShapeDtypeStruct((T, 512), BF16),
        compiler_params=_cp("parallel"), name=name)(sinks, proj, proj, proj, proj, proj, bias)


def _swa_bwd(proj, bias, sinks, do, dbias_in, name):
    T = proj.shape[0]
    nb = T // BLK

    def body(sk_ref, q_ref, kp_ref, kc_ref, vp_ref, vc_ref, b_ref, do_ref, dbi_ref,
             dq_ref, dk_ref, dv_ref, db_ref, dsk_ref, sk_acc):
        n = pl.program_id(0)
        lane = lax.broadcasted_iota(jnp.int32, (1, BLK), 1)
        lo = lane < 64
        kb = jnp.concatenate([kp_ref[...], kc_ref[...]], axis=0)
        vb = jnp.concatenate([vp_ref[...], vc_ref[...]], axis=0)
        kbs = (kb, pltpu.roll(kb, 64, 1))
        vbs = (vb, pltpu.roll(vb, 64, 1))
        mask = _swa_mask(n)

        @pl.when(n == 0)
        def _():
            db_ref[...] = dbi_ref[...]
            sk_acc[...] = jnp.zeros_like(sk_acc)

        dqs = []
        dk = jnp.zeros((2 * BLK, BLK), F32)
        dv = jnp.zeros((2 * BLK, BLK), F32)
        for pr in range(NH // 2):
            qp = q_ref[:, pr * BLK:(pr + 1) * BLK]
            dop = do_ref[:, pr * BLK:(pr + 1) * BLK].astype(F32)
            kv = pr // 2
            dq = jnp.zeros((BLK, BLK), F32)
            for e in range(2):
                h = 2 * pr + e
                lm = lo if e == 0 else jnp.logical_not(lo)
                sw = 0 if kv == e else 1
                qm = jnp.where(lm, qp, 0.0)
                dom = jnp.where(lm, dop, 0.0)
                pn, ps = _swa_probs(qm.astype(BF16), kbs[sw].astype(BF16), mask, b_ref[h], sk_ref[h])
                dp = _dot(dom.astype(BF16), vbs[sw].astype(BF16), 1, 1)
                delta = jnp.sum(pn * dp, axis=-1, keepdims=True)
                ds = pn * (dp - delta)
                db_ref[h] += ds
                sk_acc[...] += jnp.where(lane == h, -(ps * delta), 0.0)
                dsb = (ds * SCALE).astype(BF16)
                dq = dq + _dot(dsb, jnp.where(lm, kbs[sw], 0.0).astype(BF16), 1, 0)
                qk = qm if sw == 0 else pltpu.roll(qm, 64, 1)
                dok = dom if sw == 0 else pltpu.roll(dom, 64, 1)
                dk = dk + _dot(dsb, qk.astype(BF16), 0, 0)
                dv = dv + _dot(pn.astype(BF16), dok.astype(BF16), 0, 0)
            dqs.append(dq)
        dq_ref[...] = jnp.concatenate(dqs, axis=1).astype(BF16)
        dk_ref[0] = dk
        dv_ref[0] = dv

        @pl.when(n == nb - 1)
        def _():
            dsk_ref[...] = jnp.sum(sk_acc[...], axis=0, keepdims=True)

    full_b = pl.BlockSpec((NH, BLK, 2 * BLK), lambda n: (0, 0, 0))
    band = pl.BlockSpec((1, 2 * BLK, BLK), lambda n: (n, 0, 0))
    return pl.pallas_call(
        body, grid=(nb,),
        in_specs=[SMEM] + _swa_specs(512) + [full_b, pl.BlockSpec((BLK, 512), lambda n: (n, 0)), full_b],
        out_specs=[pl.BlockSpec((BLK, 512), lambda n: (n, 0)), band, band, full_b, pl.BlockSpec((1, BLK), lambda n: (0, 0))],
        out_shape=[jax.ShapeDtypeStruct((T, 512), BF16), jax.ShapeDtypeStruct((nb, 2 * BLK, BLK), F32),
                   jax.ShapeDtypeStruct((nb, 2 * BLK, BLK), F32), jax.ShapeDtypeStruct((NH, BLK, 2 * BLK), F32),
                   jax.ShapeDtypeStruct((1, BLK), F32)],
        scratch_shapes=[pltpu.VMEM((BLK, BLK), F32)],
        compiler_params=_cp("arbitrary"), name=name)(sinks, proj, proj, proj, proj, proj, bias, do, dbias_in)


def _band_fold(dkb, dvb, name):
    nb = dkb.shape[0]

    def body(ko_ref, kn_ref, vo_ref, vn_ref, dk_ref, dv_ref):
        last = pl.program_id(0) == nb - 1
        dk_ref[...] = (ko_ref[0] + jnp.where(last, 0.0, kn_ref[0])).astype(BF16)
        dv_ref[...] = (vo_ref[0] + jnp.where(last, 0.0, vn_ref[0])).astype(BF16)

    own = pl.BlockSpec((1, BLK, BLK), lambda j: (j, 1, 0))
    nxt = pl.BlockSpec((1, BLK, BLK), lambda j: (jnp.minimum(j + 1, nb - 1), 0, 0))
    out = pl.BlockSpec((BLK, BLK), lambda j: (j, 0))
    return pl.pallas_call(
        body, grid=(nb,), in_specs=[own, nxt, own, nxt], out_specs=[out, out],
        out_shape=[jax.ShapeDtypeStruct((nb * BLK, BLK), BF16)] * 2,
        compiler_params=_cp("parallel"), name=name)(dkb, dkb, dvb, dvb)


def _cum_fwd(z3, fb, name):
    nb = z3.shape[0]

    def body(z_ref, fb_ref, c_ref):
        lane = lax.broadcasted_iota(jnp.int32, (NH, BLK), 1)

        def step(b, carry):
            x = jnp.where(b * BLK + lane >= NPAD, _log_sigmoid(z_ref[b] + fb_ref[...]), 0.0)
            s = 1
            while s < BLK:
                x = x + jnp.where(lane >= s, pltpu.roll(x, s, 1), 0.0)
                s *= 2
            x = x + carry
            c_ref[b] = x
            return jnp.sum(jnp.where(lane == BLK - 1, x, 0.0), axis=-1, keepdims=True)

        lax.fori_loop(0, nb, step, jnp.zeros((NH, 1), F32))

    return pl.pallas_call(body, out_shape=jax.ShapeDtypeStruct((nb, NH, BLK), F32), compiler_params=_cp(), name=name)(z3, fb)


def _cum_bwd(dck3, dcq3, z3, fb, name):
    nb = z3.shape[0]

    def body(d_ref, dq_ref, z_ref, fb_ref, dz_ref, db_ref):
        lane = lax.broadcasted_iota(jnp.int32, (NH, BLK), 1)

        def step(k, carry):
            suffix, tot = carry
            b = nb - 1 - k
            x = d_ref[b] + dq_ref[b]
            s = 1
            while s < BLK:
                x = x + jnp.where(lane < BLK - s, pltpu.roll(x, BLK - s, 1), 0.0)
                s *= 2
            x = x + suffix
            dz = jnp.where(b * BLK + lane >= NPAD, x * _sigmoid(-(z_ref[b] + fb_ref[...])), 0.0)
            dz_ref[b] = dz
            return (jnp.sum(jnp.where(lane == 0, x, 0.0), axis=-1, keepdims=True),
                    tot + jnp.sum(dz, axis=-1, keepdims=True))

        z0 = jnp.zeros((NH, 1), F32)
        _, tot = lax.fori_loop(0, nb, step, (z0, z0))
        db_ref[...] = jnp.zeros((NH, BLK), F32) + tot

    return pl.pallas_call(
        body, out_shape=[jax.ShapeDtypeStruct((nb, NH, BLK), F32), jax.ShapeDtypeStruct((NH, BLK), F32)],
        compiler_params=_cp(), name=name)(dck3, dcq3, z3, fb)


def _to_blocks(a):
    return a.reshape(NH, -1, BLK).transpose(1, 0, 2)


def _from_blocks(a):
    return a.transpose(1, 0, 2).reshape(NH, -1)


def _fox_scores(q, kbf, mask, lm, cq, ck):
    s = _dot(jnp.where(lm, q, 0.0).astype(BF16), kbf, 1, 1) * SCALE
    return jnp.where(mask, s + cq - ck, NEG)


def _fox_mask(i, j, t):
    row = i * t + lax.broadcasted_iota(jnp.int32, (t, t), 0)
    col = j * t + lax.broadcasted_iota(jnp.int32, (t, t), 1)
    return (col <= row) & (col >= NPAD)


def _lane_pick(x, lane, idx):
    return jnp.sum(jnp.where(lane == idx, x, 0.0), axis=-1, keepdims=True)


def _fox_fwd(proj, crow, ccol, name):
    T = proj.shape[0]
    t = _rt(T)
    nt = T // t

    def body(q_ref, k_ref, v_ref, cr_ref, cc_ref, o_ref, lse_ref, m_ref, l_ref, acc_ref):
        p_, i, j = pl.program_id(0), pl.program_id(1), pl.program_id(2)
        lane = lax.broadcasted_iota(jnp.int32, (1, BLK), 1)
        lo = lane < 64

        @pl.when(j == 0)
        def _():
            m_ref[...] = jnp.full_like(m_ref, NEG)
            l_ref[...] = jnp.zeros_like(l_ref)
            acc_ref[...] = jnp.zeros_like(acc_ref)

        @pl.when(j <= i)
        def _():
            q = q_ref[...]
            kbf = k_ref[...].astype(BF16)
            v = v_ref[...]
            cc = cc_ref[...]
            mask = _fox_mask(i, j, t)
            alphas, pvs = [], []
            for e in range(2):
                h = 2 * p_ + e
                lm = lo if e == 0 else jnp.logical_not(lo)
                s = _fox_scores(q, kbf, mask, lm, _lane_pick(cc, lane, h), cr_ref[pl.ds(h, 1), :])
                m_old = m_ref[e]
                m_new = jnp.maximum(m_old, jnp.max(s, axis=-1, keepdims=True))
                alpha = jnp.exp(m_old - m_new)
                pe = jnp.exp(s - m_new)
                l_ref[e] = alpha * l_ref[e] + jnp.sum(pe, axis=-1, keepdims=True)
                m_ref[e] = m_new
                alphas.append(alpha)
                pvs.append(_dot(pe.astype(BF16), jnp.where(lm, v, 0.0).astype(BF16), 1, 0))
            acc_ref[...] = acc_ref[...] * jnp.where(lo, alphas[0], alphas[1]) + pvs[0] + pvs[1]

        @pl.when(j == i)
        def _():
            rows = i * t + lax.broadcasted_iota(jnp.int32, (t, 1), 0)
            o = acc_ref[...] / jnp.where(lo, l_ref[0], l_ref[1])
            o_ref[...] = jnp.where(rows >= NPAD, o, 0.0).astype(BF16)
            lse_ref[...] = jnp.where(lo, m_ref[0] + jnp.log(l_ref[0]), m_ref[1] + jnp.log(l_ref[1]))

    kj = lambda i, j: jnp.minimum(j, i)
    return pl.pallas_call(
        body, grid=(NH // 2, nt, nt),
        in_specs=[pl.BlockSpec((t, BLK), lambda p, i, j: (i, QF // BLK + p)),
                  pl.BlockSpec((t, BLK), lambda p, i, j: (kj(i, j), KF // BLK + p)),
                  pl.BlockSpec((t, BLK), lambda p, i, j: (kj(i, j), VF // BLK + p)),
                  pl.BlockSpec((NH, t), lambda p, i, j: (0, kj(i, j))),
                  pl.BlockSpec((t, BLK), lambda p, i, j: (i, 0))],
        out_specs=[pl.BlockSpec((t, BLK), lambda p, i, j: (i, p))] * 2,
        out_shape=[jax.ShapeDtypeStruct((T, 512), BF16), jax.ShapeDtypeStruct((T, 512), F32)],
        scratch_shapes=[pltpu.VMEM((2, t, 1), F32), pltpu.VMEM((2, t, 1), F32), pltpu.VMEM((t, BLK), F32)],
        compiler_params=_cp("parallel", "parallel", "arbitrary"), name=name)(proj, proj, proj, crow, ccol)


def _fox_dq(proj, crow, ccol, o, lse, do, name):
    T = proj.shape[0]
    t = _rt(T)
    nt = T // t

    def body(q_ref, k_ref, v_ref, cr_ref, cc_ref, o_ref, lse_ref, do_ref, dq_ref, rs_ref, acc_ref, dl_ref, rs_acc):
        p_, i, j = pl.program_id(0), pl.program_id(1), pl.program_id(2)
        lane = lax.broadcasted_iota(jnp.int32, (1, BLK), 1)
        lo = lane < 64

        @pl.when(j == 0)
        def _():
            acc_ref[...] = jnp.zeros_like(acc_ref)
            rs_acc[...] = jnp.zeros_like(rs_acc)
            prod = do_ref[...].astype(F32) * o_ref[...].astype(F32)
            dl_ref[0] = jnp.sum(jnp.where(lo, prod, 0.0), axis=-1, keepdims=True)
            dl_ref[1] = jnp.sum(jnp.where(lo, 0.0, prod), axis=-1, keepdims=True)

        @pl.when(j <= i)
        def _():
            q = q_ref[...]
            k = k_ref[...]
            kbf = k.astype(BF16)
            vbf = v_ref[...].astype(BF16)
            cc = cc_ref[...]
            do_ = do_ref[...].astype(F32)
            lse_ = lse_ref[...]
            mask = _fox_mask(i, j, t)
            acc = acc_ref[...]
            for e in range(2):
                h = 2 * p_ + e
                lm = lo if e == 0 else jnp.logical_not(lo)
                s = _fox_scores(q, kbf, mask, lm, _lane_pick(cc, lane, h), cr_ref[pl.ds(h, 1), :])
                pe = jnp.exp(s - _lane_pick(lse_, lane, 64 * e))
                dp = _dot(jnp.where(lm, do_, 0.0).astype(BF16), vbf, 1, 1)
                ds = pe * (dp - dl_ref[e])
                rs_acc[e] += jnp.sum(ds, axis=-1, keepdims=True)
                acc = acc + _dot((ds * SCALE).astype(BF16), jnp.where(lm, k, 0.0).astype(BF16), 1, 0)
            acc_ref[...] = acc

        @pl.when(j == i)
        def _():
            dq_ref[...] = acc_ref[...].astype(BF16)
            rs_ref[...] = jnp.where(lo, rs_acc[0], rs_acc[1])

    kj = lambda i, j: jnp.minimum(j, i)
    qside = pl.BlockSpec((t, BLK), lambda p, i, j: (i, p))
    return pl.pallas_call(
        body, grid=(NH // 2, nt, nt),
        in_specs=[pl.BlockSpec((t, BLK), lambda p, i, j: (i, QF // BLK + p)),
                  pl.BlockSpec((t, BLK), lambda p, i, j: (kj(i, j), KF // BLK + p)),
                  pl.BlockSpec((t, BLK), lambda p, i, j: (kj(i, j), VF // BLK + p)),
                  pl.BlockSpec((NH, t), lambda p, i, j: (0, kj(i, j))),
                  pl.BlockSpec((t, BLK), lambda p, i, j: (i, 0)), qside, qside, qside],
        out_specs=[qside, qside], out_shape=[jax.ShapeDtypeStruct((T, 512), BF16), jax.ShapeDtypeStruct((T, 512), F32)],
        scratch_shapes=[pltpu.VMEM((t, BLK), F32), pltpu.VMEM((2, t, 1), F32), pltpu.VMEM((2, t, 1), F32)],
        compiler_params=_cp("parallel", "parallel", "arbitrary"), name=name)(proj, proj, proj, crow, ccol, o, lse, do)


def _fox_dkv(proj, crow, ccol, o, lse, do, name):
    T = proj.shape[0]
    t = _rt(T)
    nt = T // t

    def body(q_ref, k_ref, v_ref, cr_ref, cc_ref, o_ref, lse_ref, do_ref, dk_ref, dv_ref, dc_ref, dk_acc, dv_acc, dc_acc):
        p_, j, i = pl.program_id(0), pl.program_id(1), pl.program_id(2)
        lane = lax.broadcasted_iota(jnp.int32, (1, BLK), 1)
        lo = lane < 64

        @pl.when(i == 0)
        def _():
            dk_acc[...] = jnp.zeros_like(dk_acc)
            dv_acc[...] = jnp.zeros_like(dv_acc)
            dc_acc[...] = jnp.zeros_like(dc_acc)

        @pl.when(i >= j)
        def _():
            q = q_ref[...]
            kbf = k_ref[...].astype(BF16)
            vbf = v_ref[...].astype(BF16)
            cc = cc_ref[...]
            do_ = do_ref[...].astype(F32)
            lse_ = lse_ref[...]
            prod = do_ * o_ref[...].astype(F32)
            mask = _fox_mask(i, j, t)
            dk = dk_acc[...]
            dv = dv_acc[...]
            for e in range(2):
                h = 2 * p_ + e
                lm = lo if e == 0 else jnp.logical_not(lo)
                s = _fox_scores(q, kbf, mask, lm, _lane_pick(cc, lane, h), cr_ref[pl.ds(h, 1), :])
                pe = jnp.exp(s - _lane_pick(lse_, lane, 64 * e))
                dom = jnp.where(lm, do_, 0.0).astype(BF16)
                dp = _dot(dom, vbf, 1, 1)
                delta = jnp.sum(jnp.where(lm, prod, 0.0), axis=-1, keepdims=True)
                ds = pe * (dp - delta)
                dv = dv + _dot(pe.astype(BF16), dom, 0, 0)
                dk = dk + _dot((ds * SCALE).astype(BF16), jnp.where(lm, q, 0.0).astype(BF16), 0, 0)
                dc_acc[e:e + 1, :] -= jnp.sum(ds, axis=0, keepdims=True)
            dk_acc[...] = dk
            dv_acc[...] = dv

        @pl.when(i == nt - 1)
        def _():
            dk_ref[...] = dk_acc[...].astype(BF16)
            dv_ref[...] = dv_acc[...].astype(BF16)
            dc_ref[0] = dc_acc[...]

    qi = lambda j, i: jnp.maximum(i, j)
    qside = pl.BlockSpec((t, BLK), lambda p, j, i: (qi(j, i), p))
    kside = pl.BlockSpec((t, BLK), lambda p, j, i: (j, p))
    return pl.pallas_call(
        body, grid=(NH // 2, nt, nt),
        in_specs=[pl.BlockSpec((t, BLK), lambda p, j, i: (qi(j, i), QF // BLK + p)),
                  pl.BlockSpec((t, BLK), lambda p, j, i: (j, KF // BLK + p)),
                  pl.BlockSpec((t, BLK), lambda p, j, i: (j, VF // BLK + p)),
                  pl.BlockSpec((NH, t), lambda p, j, i: (0, j)),
                  pl.BlockSpec((t, BLK), lambda p, j, i: (qi(j, i), 0)), qside, qside, qside],
        out_specs=[kside, kside, pl.BlockSpec((1, NH, t), lambda p, j, i: (p, 0, j))],
        out_shape=[jax.ShapeDtypeStruct((T, 512), BF16), jax.ShapeDtypeStruct((T, 512), BF16),
                   jax.ShapeDtypeStruct((NH // 2, NH, T), F32)],
        scratch_shapes=[pltpu.VMEM((t, BLK), F32), pltpu.VMEM((t, BLK), F32), pltpu.VMEM((NH, t), F32)],
        compiler_params=_cp("parallel", "parallel", "arbitrary"), name=name)(proj, proj, proj, crow, ccol, o, lse, do)


def _lru_gates(xc, wr_ref, wi_ref, vec_ref):
    xb = xc.astype(BF16)
    pre_r = jnp.concatenate([_dot(xb[:, p * BLK:(p + 1) * BLK], wr_ref[p], 1, 0) for p in range(LW // BLK)], axis=1)
    pre_i = jnp.concatenate([_dot(xb[:, p * BLK:(p + 1) * BLK], wi_ref[p], 1, 0) for p in range(LW // BLK)], axis=1)
    r = _sigmoid(pre_r + vec_ref[0:1, :])
    gi = _sigmoid(pre_i + vec_ref[1:2, :])
    log_a = LRU_C * r * _log_sigmoid(vec_ref[2:3, :])
    a = jnp.exp(log_a)
    mult = jnp.sqrt(_neg_expm1(2.0 * log_a))
    return r, gi, a, mult


def _conv(xbuf_ref, x, cw_ref, vec_ref, tr):
    return (cw_ref[3:4, :] * x + cw_ref[2:3, :] * xbuf_ref[7:7 + tr, :] + cw_ref[1:2, :] * xbuf_ref[6:6 + tr, :]
            + cw_ref[0:1, :] * xbuf_ref[5:5 + tr, :] + vec_ref[3:4, :])


def _lru_fwd(proj, cw, wr, wi, vec, name):
    T = proj.shape[0]
    tr = _rt(T)

    def body(x_ref, y_ref, cw_ref, wr_ref, wi_ref, vec_ref, oc_ref, hs_ref, xbuf, abuf, bbuf, hcar):
        i = pl.program_id(0)

        @pl.when(i == 0)
        def _():
            xbuf[0:8, :] = jnp.zeros((8, LW), F32)
            hcar[...] = jnp.zeros_like(hcar)

        x = x_ref[...]
        xbuf[8:8 + tr, :] = x
        xc = _conv(xbuf, x, cw_ref, vec_ref, tr)
        xbuf[0:8, :] = x[tr - 8:tr, :]
        _, gi, a, mult = _lru_gates(xc, wr_ref, wi_ref, vec_ref)
        rows = i * tr + lax.broadcasted_iota(jnp.int32, (tr, 1), 0)
        abuf[...] = a
        bbuf[...] = jnp.where(rows >= NPAD, mult * (gi * xc), 0.0)
        sub = lax.broadcasted_iota(jnp.int32, (8, 1), 0)

        def step(k, h):
            sl = pl.ds(pl.multiple_of(k * 8, 8), 8)
            a8, b8 = abuf[sl, :], bbuf[sl, :]
            for s in (1, 2, 4):
                ok = sub >= s
                b8 = jnp.where(ok, a8 * pltpu.roll(b8, s, 0) + b8, b8)
                a8 = jnp.where(ok, a8 * pltpu.roll(a8, s, 0), a8)
            h8 = a8 * h + b8
            bbuf[sl, :] = h8
            return h8[7:8, :]

        hcar[...] = lax.fori_loop(0, tr // 8, step, hcar[...])
        hs = bbuf[...]
        hs_ref[...] = hs
        oc_ref[...] = (hs * _gelu(y_ref[...])).astype(BF16)

    row = pl.BlockSpec((tr, LW), lambda i: (i, 0))
    full = lambda shape: pl.BlockSpec(shape, lambda i: (0,) * len(shape))
    return pl.pallas_call(
        body, grid=(T // tr,),
        in_specs=[pl.BlockSpec((tr, LW), lambda i: (i, XC // LW)), pl.BlockSpec((tr, LW), lambda i: (i, YC // LW)),
                  full((4, LW)), full((4, BLK, BLK)), full((4, BLK, BLK)), full((8, LW))],
        out_specs=[row, row], out_shape=[jax.ShapeDtypeStruct((T, LW), BF16), jax.ShapeDtypeStruct((T, LW), F32)],
        scratch_shapes=[pltpu.VMEM((tr + 8, LW), F32), pltpu.VMEM((tr, LW), F32), pltpu.VMEM((tr, LW), F32),
                        pltpu.VMEM((1, LW), F32)],
        compiler_params=_cp("arbitrary"), name=name)(proj, proj, cw, wr, wi, vec)


def _lru_bwd(proj, hs, doc, cw, wr, wi, vec, name):
    T = proj.shape[0]
    tr = _rt(T)
    nt = T // tr
    r8 = tr // 8

    def body(x_ref, xp_ref, y_ref, hs_ref, hp_ref, do_ref, cw_ref, wr_ref, wi_ref, vec_ref,
             dx_ref, dy_ref, dwr_ref, dwi_ref, dvec_ref, xbuf, abuf, gbuf, hbuf, dbuf, gcar, acar):
        k = pl.program_id(0)
        i = nt - 1 - k

        @pl.when(k == 0)
        def _():
            dwr_ref[...] = jnp.zeros_like(dwr_ref)
            dwi_ref[...] = jnp.zeros_like(dwi_ref)
            dvec_ref[...] = jnp.zeros_like(dvec_ref)
            gcar[...] = jnp.zeros_like(gcar)
            acar[...] = jnp.zeros_like(acar)
            dbuf[tr:tr + 8, :] = jnp.zeros((8, LW), F32)

        first = i == 0
        x = x_ref[...]
        xbuf[0:8, :] = jnp.where(first, 0.0, xp_ref[...])
        xbuf[8:8 + tr, :] = x
        xc = _conv(xbuf, x, cw_ref, vec_ref, tr)
        r, gi, a, mult = _lru_gates(xc, wr_ref, wi_ref, vec_ref)
        y = y_ref[...]
        hs = hs_ref[...]
        do_ = do_ref[...].astype(F32)
        rows = i * tr + lax.broadcasted_iota(jnp.int32, (tr, 1), 0)
        abuf[0:tr, :] = a
        abuf[tr:tr + 8, :] = jnp.zeros((8, LW), F32) + acar[...]
        an = abuf[1:1 + tr, :]
        acar[...] = a[0:1, :]
        abuf[0:tr, :] = an
        gbuf[...] = do_ * _gelu(y)
        sub = lax.broadcasted_iota(jnp.int32, (8, 1), 0)

        def step(kk, g):
            sl = pl.ds(pl.multiple_of((r8 - 1 - kk) * 8, 8), 8)
            a8, b8 = abuf[sl, :], gbuf[sl, :]
            for s in (1, 2, 4):
                ok = sub < 8 - s
                b8 = jnp.where(ok, a8 * pltpu.roll(b8, 8 - s, 0) + b8, b8)
                a8 = jnp.where(ok, a8 * pltpu.roll(a8, 8 - s, 0), a8)
            g8 = a8 * g + b8
            gbuf[sl, :] = g8
            return g8[0:1, :]

        gcar[...] = lax.fori_loop(0, r8, step, gcar[...])
        g = gbuf[...]
        hbuf[0:8, :] = jnp.where(first, 0.0, hp_ref[...])
        hbuf[8:8 + tr, :] = hs
        hprev = hbuf[7:7 + tr, :]
        dinp = jnp.where(rows >= NPAD, g, 0.0)
        da = g * hprev
        dmult = dinp * gi * xc
        dgi = dinp * mult * xc
        dxc = dinp * mult * gi
        dlog_a = da * a - dmult * a * a / mult
        ls = _log_sigmoid(vec_ref[2:3, :])
        dpre_r = dlog_a * (LRU_C * ls) * r * (1.0 - r)
        dpre_i = dgi * gi * (1.0 - gi)
        xb = xc.astype(BF16)
        rb, ib = dpre_r.astype(BF16), dpre_i.astype(BF16)
        back = []
        for p in range(LW // BLK):
            c = slice(p * BLK, (p + 1) * BLK)
            back.append(_dot(rb[:, c], wr_ref[p], 1, 1) + _dot(ib[:, c], wi_ref[p], 1, 1))
            dwr_ref[p] += _dot(xb[:, c], rb[:, c], 0, 0)
            dwi_ref[p] += _dot(xb[:, c], ib[:, c], 0, 0)
        dxc = dxc + jnp.concatenate(back, axis=1)
        col = lambda v: jnp.sum(v, axis=0, keepdims=True)
        dvec_ref[0:1, :] += col(dpre_r)
        dvec_ref[1:2, :] += col(dpre_i)
        dvec_ref[2:3, :] += col(dlog_a * (LRU_C * r)) * _sigmoid(-vec_ref[2:3, :])
        dvec_ref[3:4, :] += col(dxc)
        dvec_ref[4:5, :] += col(dxc * xbuf[5:5 + tr, :])
        dvec_ref[5:6, :] += col(dxc * xbuf[6:6 + tr, :])
        dvec_ref[6:7, :] += col(dxc * xbuf[7:7 + tr, :])
        dvec_ref[7:8, :] += col(dxc * x)
        dbuf[0:tr, :] = dxc
        dxr = (cw_ref[3:4, :] * dxc + cw_ref[2:3, :] * dbuf[1:1 + tr, :] + cw_ref[1:2, :] * dbuf[2:2 + tr, :]
               + cw_ref[0:1, :] * dbuf[3:3 + tr, :])
        dbuf[tr:tr + 8, :] = dxc[0:8, :]
        dx_ref[...] = jnp.where(rows >= NPAD, dxr, 0.0).astype(BF16)
        dy_ref[...] = (do_ * hs * _gelu_grad(y)).astype(BF16)

    rev = lambda k: nt - 1 - k
    row = lambda col0: pl.BlockSpec((tr, LW), lambda k: (rev(k), col0))
    prev8 = lambda col0: pl.BlockSpec((8, LW), lambda k: (jnp.maximum(rev(k) * r8 - 1, 0), col0))
    full = lambda shape: pl.BlockSpec(shape, lambda k: (0,) * len(shape))
    return pl.pallas_call(
        body, grid=(nt,),
        in_specs=[row(XC // LW), prev8(XC // LW), row(YC // LW), row(0), prev8(0), row(0),
                  full((4, LW)), full((4, BLK, BLK)), full((4, BLK, BLK)), full((8, LW))],
        out_specs=[row(0), row(0), full((4, BLK, BLK)), full((4, BLK, BLK)), full((8, LW))],
        out_shape=[jax.ShapeDtypeStruct((T, LW), BF16), jax.ShapeDtypeStruct((T, LW), BF16),
                   jax.ShapeDtypeStruct((4, BLK, BLK), F32), jax.ShapeDtypeStruct((4, BLK, BLK), F32),
                   jax.ShapeDtypeStruct((8, LW), F32)],
        scratch_shapes=[pltpu.VMEM((tr + 8, LW), F32), pltpu.VMEM((tr + 8, LW), F32), pltpu.VMEM((tr, LW), F32),
                        pltpu.VMEM((tr + 8, LW), F32), pltpu.VMEM((tr + 8, LW), F32),
                        pltpu.VMEM((1, LW), F32), pltpu.VMEM((1, LW), F32)],
        compiler_params=_cp("arbitrary"), name=name)(proj, proj, proj, hs, hs, doc, cw, wr, wi, vec)


def _merge_fwd(proj, b0, b1, b2, name):
    T = proj.shape[0]
    tr, tn = _rt(T), 512

    def body(g0, g1, g2, r0, r1, r2, o_ref):
        o_ref[...] = (_sigmoid(g0[...]) * r0[...] + _sigmoid(g1[...]) * r1[...] + _sigmoid(g2[...]) * r2[...]).astype(BF16)

    gate = lambda g: pl.BlockSpec((tr, tn), lambda i, j: (i, (GT + g * D) // tn + j))
    blk = pl.BlockSpec((tr, tn), lambda i, j: (i, j))
    return pl.pallas_call(
        body, grid=(T // tr, D // tn), in_specs=[gate(0), gate(1), gate(2), blk, blk, blk], out_specs=blk,
        out_shape=jax.ShapeDtypeStruct((T, D), BF16), compiler_params=_cp("parallel", "parallel"),
        name=name)(proj, proj, proj, b0, b1, b2)


def _merge_bwd(proj, b0, b1, b2, dm, name):
    T = proj.shape[0]
    tr, tn = _rt(T), 512

    def body(g0, g1, g2, r0, r1, r2, dm_ref, d0, d1, d2, e0, e1, e2):
        dmv = dm_ref[...]
        for g_ref, r_ref, d_ref, e_ref in ((g0, r0, d0, e0), (g1, r1, d1, e1), (g2, r2, d2, e2)):
            sg = _sigmoid(g_ref[...])
            d_ref[...] = (dmv * sg).astype(BF16)
            e_ref[...] = (dmv * r_ref[...] * sg * (1.0 - sg)).astype(BF16)

    gate = lambda g: pl.BlockSpec((tr, tn), lambda i, j: (i, (GT + g * D) // tn + j))
    blk = pl.BlockSpec((tr, tn), lambda i, j: (i, j))
    return pl.pallas_call(
        body, grid=(T // tr, D // tn), in_specs=[gate(0), gate(1), gate(2), blk, blk, blk, blk], out_specs=[blk] * 6,
        out_shape=[jax.ShapeDtypeStruct((T, D), BF16)] * 6, compiler_params=_cp("parallel", "parallel"),
        name=name)(proj, proj, proj, b0, b1, b2, dm)


def _swiglu_fwd(ff, name):
    T = ff.shape[0]
    tr, tn = _rt(T), 256
    nj = DFF // tn

    def body(g_ref, u_ref, o_ref):
        g = g_ref[...]
        o_ref[...] = (g * _sigmoid(g) * u_ref[...]).astype(BF16)

    return pl.pallas_call(
        body, grid=(T // tr, nj),
        in_specs=[pl.BlockSpec((tr, tn), lambda i, j: (i, j)), pl.BlockSpec((tr, tn), lambda i, j: (i, j + nj))],
        out_specs=pl.BlockSpec((tr, tn), lambda i, j: (i, j)), out_shape=jax.ShapeDtypeStruct((T, DFF), BF16),
        compiler_params=_cp("parallel", "parallel"), name=name)(ff, ff)


def _swiglu_bwd(ff, dact, name):
    T = ff.shape[0]
    tr, tn = _rt(T), 256
    nj = DFF // tn

    def body(g_ref, u_ref, d_ref, o_ref):
        g, d = g_ref[...], d_ref[...]
        sg = _sigmoid(g)
        dgate = d * u_ref[...] * (sg + g * sg * (1.0 - sg))
        dup = d * g * sg
        o_ref[...] = jnp.where(pl.program_id(1) < nj, dgate, dup).astype(BF16)

    half = lambda j: j % nj
    return pl.pallas_call(
        body, grid=(T // tr, 2 * nj),
        in_specs=[pl.BlockSpec((tr, tn), lambda i, j: (i, half(j))), pl.BlockSpec((tr, tn), lambda i, j: (i, half(j) + nj)),
                  pl.BlockSpec((tr, tn), lambda i, j: (i, half(j)))],
        out_specs=pl.BlockSpec((tr, tn), lambda i, j: (i, j)), out_shape=jax.ShapeDtypeStruct((T, 2 * DFF), BF16),
        compiler_params=_cp("parallel", "parallel"), name=name)(ff, ff, dact)


def _adamw(w, g, m, v, name):
    R, C = w.shape
    tr = _pick(R, tuple(t for t in (512, 256, 128, 64, 32, 16, 8) if t * C * 4 <= (3 << 19)))
    c1 = 1.0 - ADAM_B1 ** ADAM_STEP
    c2 = 1.0 - ADAM_B2 ** ADAM_STEP

    def body(w_ref, g_ref, m_ref, v_ref, d_ref, mo_ref, vo_ref):
        gv = g_ref[...]
        mn = ADAM_B1 * m_ref[...] + (1.0 - ADAM_B1) * gv
        vn = ADAM_B2 * v_ref[...] + (1.0 - ADAM_B2) * (gv * gv)
        d_ref[...] = -ADAM_LR * ((mn / c1) / (jnp.sqrt(vn / c2) + ADAM_EPS) + ADAM_WD * w_ref[...])
        mo_ref[...] = mn
        vo_ref[...] = vn

    blk = pl.BlockSpec((tr, C), lambda i: (i, 0))
    return pl.pallas_call(
        body, grid=(R // tr,), in_specs=[blk] * 4, out_specs=[blk] * 3,
        out_shape=[jax.ShapeDtypeStruct((R, C), F32)] * 3, compiler_params=_cp("parallel"), name=name)(w, g, m, v)


def _sum_lead(x, name):
    n, R, C = x.shape
    tr = _pick(R, (512, 256, 128, 64, 32, 16, 8))

    def body(x_ref, o_ref):
        acc = x_ref[0]
        for d in range(1, n):
            acc = acc + x_ref[d]
        o_ref[...] = acc

    return pl.pallas_call(
        body, grid=(R // tr,), in_specs=[pl.BlockSpec((n, tr, C), lambda i: (0, i, 0))],
        out_specs=pl.BlockSpec((tr, C), lambda i: (i, 0)), out_shape=jax.ShapeDtypeStruct((R, C), F32),
        compiler_params=_cp("parallel"), name=name)(x)


def _here():
    return lax.axis_index("x"), lax.axis_index("y"), lax.axis_index("c")


def _rcopy(src, dst, send_sems, recv_sems, k, to):
    return pltpu.make_async_remote_copy(src_ref=src, dst_ref=dst, send_sem=send_sems.at[k], recv_sem=recv_sems.at[k],
                                        device_id=to, device_id_type=MESH)


def _all_gather_shards(wpk, name):
    R, C = wpk.shape
    H = R // 2

    def body(w_ref, out_ref, send_sems, recv_sems, local_sem):
        x, y, c = _here()
        sib = (x, y, 1 - c)
        chips = [(1 - x, y), (x, 1 - y), (1 - x, 1 - y)]
        mine = 2 * x + y

        def half(chip, hc):
            return out_ref.at[2 * chip[0] + chip[1], pl.ds(hc * H, H), :]

        own = pltpu.make_async_copy(w_ref, out_ref.at[mine], local_sem)
        own.start()
        src = w_ref.at[pl.ds(c * H, H), :]
        first = [_rcopy(src, half((x, y), c), send_sems, recv_sems, k, (*chip, c)) for k, chip in enumerate(chips)]
        for cp in first:
            cp.start()
        passed = [_rcopy(half(chip, c), half(chip, c), send_sems, recv_sems, 3 + k, sib) for k, chip in enumerate(chips)]
        for k, chip in enumerate(chips):
            _rcopy(src, half(chip, c), send_sems, recv_sems, k, (*chip, c)).wait_recv()
            passed[k].start()
        for k, chip in enumerate(chips):
            _rcopy(src, half(chip, 1 - c), send_sems, recv_sems, 3 + k, sib).wait_recv()
        for cp in first + passed:
            cp.wait_send()
        own.wait()

    return pl.pallas_call(
        body, in_specs=[ANY], out_specs=ANY, out_shape=jax.ShapeDtypeStruct((N_SHARD, R, C), wpk.dtype),
        scratch_shapes=[pltpu.SemaphoreType.DMA((6,)), pltpu.SemaphoreType.DMA((6,)), pltpu.SemaphoreType.DMA(())],
        compiler_params=pltpu.CompilerParams(has_side_effects=True), name=name)(wpk)


def _swap_halves(g, name):
    n, _, H, C = g.shape

    def body(g_ref, out_ref, send_sems, recv_sems):
        x, y, c = _here()
        sib = (x, y, 1 - c)
        cps = [_rcopy(g_ref.at[j, 1 - c], out_ref.at[j], send_sems, recv_sems, j, sib) for j in range(n)]
        for cp in cps:
            cp.start()
        for cp in cps:
            cp.wait()

    return pl.pallas_call(
        body, in_specs=[ANY], out_specs=ANY, out_shape=jax.ShapeDtypeStruct((n, H, C), g.dtype),
        scratch_shapes=[pltpu.SemaphoreType.DMA((n,)), pltpu.SemaphoreType.DMA((n,))],
        compiler_params=pltpu.CompilerParams(has_side_effects=True), name=name)(g)


def _scatter_to_chips(s, name):
    n, H, C = s.shape

    def body(s_ref, out_ref, send_sems, recv_sems):
        x, y, c = _here()
        chips = [(1 - x, y), (x, 1 - y), (1 - x, 1 - y)]
        cps = [_rcopy(s_ref.at[2 * chip[0] + chip[1]], out_ref.at[k], send_sems, recv_sems, k, (*chip, c))
               for k, chip in enumerate(chips)]
        for cp in cps:
            cp.start()
        for cp in cps:
            cp.wait()

    return pl.pallas_call(
        body, in_specs=[ANY], out_specs=ANY, out_shape=jax.ShapeDtypeStruct((3, H, C), s.dtype),
        scratch_shapes=[pltpu.SemaphoreType.DMA((3,)), pltpu.SemaphoreType.DMA((3,))],
        compiler_params=pltpu.CompilerParams(has_side_effects=True), name=name)(s)


def _join_halves(tot, name):
    H, C = tot.shape

    def body(t_ref, out_ref, send_sems, recv_sems, local_sem):
        x, y, c = _here()
        own = pltpu.make_async_copy(t_ref, out_ref.at[c], local_sem)
        own.start()
        cp = _rcopy(t_ref, out_ref.at[c], send_sems, recv_sems, 0, (x, y, 1 - c))
        cp.start()
        _rcopy(t_ref, out_ref.at[1 - c], send_sems, recv_sems, 0, (x, y, 1 - c)).wait_recv()
        cp.wait_send()
        own.wait()

    return pl.pallas_call(
        body, in_specs=[ANY], out_specs=ANY, out_shape=jax.ShapeDtypeStruct((2, H, C), tot.dtype),
        scratch_shapes=[pltpu.SemaphoreType.DMA((1,)), pltpu.SemaphoreType.DMA((1,)), pltpu.SemaphoreType.DMA(())],
        compiler_params=pltpu.CompilerParams(has_side_effects=True), name=name)(tot)


def _all_gather_small(v, name):
    R, C = v.shape

    def body(v_ref, out_ref, send_sems, recv_sems, local_sem):
        x, y, c = _here()
        me = 4 * x + 2 * y + c
        own = pltpu.make_async_copy(v_ref, out_ref.at[me], local_sem)
        own.start()
        cps = []
        for k in range(1, 8):
            fx, fy, fc = (k >> 2) & 1, (k >> 1) & 1, k & 1
            to = (x ^ fx, y ^ fy, c ^ fc)
            cps.append((_rcopy(v_ref, out_ref.at[me], send_sems, recv_sems, k - 1, to),
                        _rcopy(v_ref, out_ref.at[4 * to[0] + 2 * to[1] + to[2]], send_sems, recv_sems, k - 1, to)))
        for snd, _ in cps:
            snd.start()
        for snd, rcv in cps:
            rcv.wait_recv()
        for snd, _ in cps:
            snd.wait_send()
        own.wait()

    return pl.pallas_call(
        body, in_specs=[ANY], out_specs=ANY, out_shape=jax.ShapeDtypeStruct((8, R, C), v.dtype),
        scratch_shapes=[pltpu.SemaphoreType.DMA((7,)), pltpu.SemaphoreType.DMA((7,)), pltpu.SemaphoreType.DMA(())],
        compiler_params=pltpu.CompilerParams(has_side_effects=True), name=name)(v)


def _add_half(g, other, cidx, name):
    n, _, H, C = g.shape
    tr = _pick(H, (1024, 512, 256, 128, 64, 32, 16, 8))

    def body(c_ref, g_ref, o_ref, out_ref):
        out_ref[0] = g_ref[0, 0] + o_ref[0]

    return pl.pallas_call(
        body,
        grid_spec=pltpu.PrefetchScalarGridSpec(
            num_scalar_prefetch=1, grid=(n, H // tr),
            in_specs=[pl.BlockSpec((1, 1, tr, C), lambda j, i, c: (j, c[0], i, 0)),
                      pl.BlockSpec((1, tr, C), lambda j, i, c: (j, i, 0))],
            out_specs=pl.BlockSpec((1, tr, C), lambda j, i, c: (j, i, 0))),
        out_shape=jax.ShapeDtypeStruct((n, H, C), F32), compiler_params=_cp("parallel", "parallel"), name=name)(cidx, g, other)


def _add_chips(s, recv, chip_idx, name):
    n, H, C = s.shape
    tr = _pick(H, (1024, 512, 256, 128, 64, 32, 16, 8))

    def body(c_ref, s_ref, r_ref, out_ref):
        out_ref[...] = ((s_ref[0] + r_ref[0]) + r_ref[1]) + r_ref[2]

    return pl.pallas_call(
        body,
        grid_spec=pltpu.PrefetchScalarGridSpec(
            num_scalar_prefetch=1, grid=(H // tr,),
            in_specs=[pl.BlockSpec((1, tr, C), lambda i, c: (c[0], i, 0)), pl.BlockSpec((3, tr, C), lambda i, c: (0, i, 0))],
            out_specs=pl.BlockSpec((tr, C), lambda i, c: (i, 0))),
        out_shape=jax.ShapeDtypeStruct((H, C), F32), compiler_params=_cp("parallel"), name=name)(chip_idx, s, recv)


def _permute_in_cols(w):
    pad = jnp.zeros(w.shape[:-1] + (INP - IN_COLS,), w.dtype)
    return jnp.concatenate([w[..., 0:512], w[..., 768:2304], w[..., 2312:6408], w[..., 512:768], w[..., 2304:2312], pad], axis=-1)


def _unpermute_in_cols(w):
    return jnp.concatenate([w[..., 0:512], w[..., KA:FL], w[..., 512:2048], w[..., FL:FL + 8], w[..., 2048:KA]], axis=-1)


def _pair_blocks(w):
    z = jnp.zeros((4, 64, 64), w.dtype)
    w = w.reshape(4, 2, 64, 64)
    top = jnp.concatenate([w[:, 0], z], axis=2)
    bot = jnp.concatenate([z, w[:, 1]], axis=2)
    return jnp.concatenate([top, bot], axis=1)


def _unpair_blocks(w):
    return jnp.stack([w[:, :64, :64], w[:, 64:, 64:]], axis=1).reshape(8, 64, 64)


BIG = ("w_in", "w_branch", "w_out", "w_ffn_in", "w_ffn_out")
TINY = ("conv_w", "meta_tokens")
SMALL = ("rel_bias_table", "norm_mix", "swa_sinks", "fox_forget_bias", "conv_b", "lru_w_r", "lru_b_r", "lru_w_i",
         "lru_b_i", "lru_lambda", "norm_ffn", "norm_final")
SHARD_AXIS = {"w_in": 2, "w_branch": 3, "w_out": 1, "w_ffn_in": 2, "w_ffn_out": 1, "conv_w": 2, "meta_tokens": 1}


def _rows128(a):
    return a.reshape(-1, 128)


def _pack_small(d):
    flat = jnp.concatenate([d[n].reshape(-1) for n in SMALL])
    pad = (-flat.shape[0]) % (256 * 128)
    return _rows128(jnp.concatenate([flat, jnp.zeros((pad,), F32)]))


def _unpack_small(buf, shapes):
    flat, out, off = buf.reshape(-1), {}, 0
    for n in SMALL:
        sz = int(np.prod(shapes[n]))
        out[n] = flat[off:off + sz].reshape(shapes[n])
        off += sz
    return out


def _local_step(x, tgt, W):
    S = x.shape[0]
    T = S + BLK
    tm = _rt(T)
    bucket = jnp.asarray(_bucket_table())
    bias = _bias_build(W["rel_bias_table"], bucket, "bias_build")
    h = jnp.concatenate([jnp.zeros((NPAD, D), F32), W["meta_tokens"], x], axis=0)

    saved = []
    for l in range(DEPTH):
        sv = {"h0": h}
        u = _rms_fwd(h, W["norm_mix"][l], f"rms_mix_fwd")
        proj = _mm(u, W["w_in"][l], tm=tm, tn=512, tk=D, name="mm_in_fwd")
        oa = _swa_fwd(proj, bias, W["swa_sinks"][l], "swa_fwd")
        z3 = _to_blocks(proj[:, FL:FL + NH].T)
        fb = W["fox_forget_bias"][l].reshape(NH, 1)
        crow = _from_blocks(_cum_fwd(z3, fb, "cum_fwd"))
        ccol = jnp.pad(crow.T, ((0, 0), (0, BLK - NH)))
        of, lse = _fox_fwd(proj, crow, ccol, "fox_fwd")
        lru_vec = jnp.concatenate([W["lru_b_r"][l][None], W["lru_b_i"][l][None], W["lru_lambda"][l][None],
                                   W["conv_b"][l][None], jnp.zeros((4, LW), F32)], axis=0)
        oc, hs = _lru_fwd(proj, W["conv_w"][l], W["lru_w_r"][l], W["lru_w_i"][l], lru_vec, "lru_fwd")
        bs = [_mm(o, W["w_branch"][l, g], tm=tm, tn=512, tk=LW, name="mm_branch_fwd") for g, o in enumerate((oa, of, oc))]
        merged = _merge_fwd(proj, *bs, "merge_fwd")
        h2 = _mm(merged, W["w_out"][l], res=h, tm=tm, tn=512, tk=D, name="mm_out_fwd")
        u2 = _rms_fwd(h2, W["norm_ffn"][l], "rms_ffn_fwd")
        ff = _mm(u2, W["w_ffn_in"][l], tm=tm, tn=512, tk=D, name="mm_ffn_in_fwd")
        act = _swiglu_fwd(ff, "swiglu_fwd")
        h = _mm(act, W["w_ffn_out"][l], res=h2, tm=tm, tn=512, tk=_pick(DFF, (1408, 256)), name="mm_ffn_out_fwd")
        sv.update(u=u, proj=proj, oa=oa, of=of, oc=oc, lse=lse, hs=hs, z3=z3, fb=fb, crow=crow, ccol=ccol, lru_vec=lru_vec,
                  bs=bs, merged=merged, h2=h2, u2=u2, ff=ff, act=act)
        saved.append(sv)

    tgt_pad = tgt
    dh, dhb, dg_final, loss_vec = _loss_head(h, tgt_pad, W["norm_final"], "loss_head")
    loss = loss_vec[0, 0]

    G = {n: [None] * DEPTH for n in ("norm_mix", "w_in", "swa_sinks", "fox_forget_bias", "conv_w", "conv_b", "lru_w_r",
                                     "lru_b_r", "lru_w_i", "lru_b_i", "lru_lambda", "w_branch", "w_out", "norm_ffn",
                                     "w_ffn_in", "w_ffn_out")}
    G["norm_final"] = dg_final.reshape(D)
    dbias = jnp.zeros((NH, BLK, 2 * BLK), F32)
    tkT = _rt(T)
    for l in reversed(range(DEPTH)):
        sv = saved[l]
        G["w_ffn_out"][l] = _mm(sv["act"], dhb, ta=True, tm=_pick(DFF, (1408, 256)), tn=512, tk=tkT, name="mm_ffn_out_dw")
        dact = _mm(dhb, W["w_ffn_out"][l], tb=True, tm=tm, tn=_pick(DFF, (1408, 256)), tk=D, name="mm_ffn_out_dx")
        dff = _swiglu_bwd(sv["ff"], dact, "swiglu_bwd")
        G["w_ffn_in"][l] = _mm(sv["u2"], dff, ta=True, tm=512, tn=512, tk=tkT, name="mm_ffn_in_dw")
        du2 = _mm(dff, W["w_ffn_in"][l], tb=True, tm=tm, tn=512, tk=_pick(2 * DFF, (1408, 512)), name="mm_ffn_in_dx")
        dh, dhb, dgn = _rms_bwd(du2, sv["h2"], W["norm_ffn"][l], dh, "rms_ffn_bwd")
        G["norm_ffn"][l] = dgn.reshape(D)
        G["w_out"][l] = _mm(sv["merged"], dhb, ta=True, tm=512, tn=512, tk=tkT, name="mm_out_dw")
        dm = _mm(dhb, W["w_out"][l], tb=True, tm=tm, tn=512, tk=D, name="mm_out_dx")
        db0, db1, db2, dg0, dg1, dg2 = _merge_bwd(sv["proj"], *sv["bs"], dm, "merge_bwd")
        dos, dwb = [], []
        for g, (o, db) in enumerate(zip((sv["oa"], sv["of"], sv["oc"]), (db0, db1, db2))):
            dwb.append(_mm(o, db, ta=True, tm=LW, tn=512, tk=tkT, name="mm_branch_dw"))
            dos.append(_mm(db, W["w_branch"][l, g], tb=True, out_dtype=BF16, tm=tm, tn=LW, tk=D, name="mm_branch_dx"))
        G["w_branch"][l] = jnp.stack(dwb)
        dqa, dkb, dvb, dbias, dsk = _swa_bwd(sv["proj"], bias, W["swa_sinks"][l], dos[0], dbias, "swa_bwd")
        dka, dva = _band_fold(dkb, dvb, "swa_band_fold")
        G["swa_sinks"][l] = dsk[0, :NH]
        dqf, dcq = _fox_dq(sv["proj"], sv["crow"], sv["ccol"], sv["of"], sv["lse"], dos[1], "fox_dq")
        dkf, dvf, dcp = _fox_dkv(sv["proj"], sv["crow"], sv["ccol"], sv["of"], sv["lse"], dos[1], "fox_dkv")
        dck3 = _to_blocks(dcp[:, :2, :].reshape(NH, T))
        dcq3 = _to_blocks(dcq[:, ::64].T)
        dz3, dfb = _cum_bwd(dck3, dcq3, sv["z3"], sv["fb"], "cum_bwd")
        G["fox_forget_bias"][l] = dfb[:, 0]
        dfl = jnp.pad(_from_blocks(dz3).T, ((0, 0), (0, INP - FL - NH))).astype(BF16)
        dxc, dyc, dwr, dwi, dvec = _lru_bwd(sv["proj"], sv["hs"], dos[2], W["conv_w"][l], W["lru_w_r"][l], W["lru_w_i"][l],
                                            sv["lru_vec"], "lru_bwd")
        G["lru_w_r"][l], G["lru_w_i"][l] = _unpair_blocks(dwr), _unpair_blocks(dwi)
        G["lru_b_r"][l], G["lru_b_i"][l], G["lru_lambda"][l], G["conv_b"][l] = dvec[0], dvec[1], dvec[2], dvec[3]
        G["conv_w"][l] = dvec[4:8]
        dproj = jnp.concatenate([dqa, dqf, dkf, dvf, dxc, dyc, dg0, dg1, dg2, dka, dva, dfl], axis=1)
        G["w_in"][l] = _mm(sv["u"], dproj, ta=True, tm=512, tn=512, tk=tkT, name="mm_in_dw")
        du = _mm(dproj, W["w_in"][l], tb=True, tm=tm, tn=512, tk=_pick(INP, (1664, 512)), name="mm_in_dx")
        dh, dhb, dgn = _rms_bwd(du, sv["h0"], W["norm_mix"][l], dh, "rms_mix_bwd")
        G["norm_mix"][l] = dgn.reshape(D)

    grads = {n: (jnp.stack(v) if isinstance(v, list) else v) for n, v in G.items()}
    grads["w_in"] = _unpermute_in_cols(grads["w_in"])
    grads["rel_bias_table"] = _bias_bwd(dbias, bucket, "bias_bwd")
    grads["meta_tokens"] = dh[NPAD:BLK]
    return loss, dh[BLK:], grads


NAMES = ("meta_tokens", "rel_bias_table", "norm_mix", "w_in", "swa_sinks", "fox_forget_bias", "conv_w", "conv_b",
         "lru_w_r", "lru_b_r", "lru_w_i", "lru_b_i", "lru_lambda", "w_branch", "w_out", "norm_ffn", "w_ffn_in",
         "w_ffn_out", "norm_final")


def _gather_weights(P):
    tiny = jnp.concatenate([lax.bitcast_convert_type(P[n], BF16).reshape(-1) for n in TINY])
    wpk = _rows128(jnp.concatenate([P[n].astype(BF16).reshape(-1) for n in BIG] + [tiny]))
    got = _all_gather_shards(wpk, "ag_weights")
    full = {}
    parts = {n: [] for n in BIG + TINY}
    for j in range(N_SHARD):
        flat, off = got[j].reshape(-1), 0
        for n in BIG:
            sz = int(np.prod(P[n].shape))
            parts[n].append(flat[off:off + sz].reshape(P[n].shape))
            off += sz
        for n in TINY:
            sz = 2 * int(np.prod(P[n].shape))
            parts[n].append(lax.bitcast_convert_type(flat[off:off + sz].reshape(P[n].shape + (2,)), F32))
            off += sz
    for n in BIG + TINY:
        full[n] = jnp.concatenate(parts[n], axis=SHARD_AXIS[n])
    full["w_in"] = _permute_in_cols(full["w_in"])
    for n in SMALL:
        full[n] = P[n]
    full["lru_w_r"] = jnp.stack([_pair_blocks(P["lru_w_r"][l]) for l in range(DEPTH)]).astype(BF16)
    full["lru_w_i"] = jnp.stack([_pair_blocks(P["lru_w_i"][l]) for l in range(DEPTH)]).astype(BF16)
    return full


def _reduce_grads(grads, P):
    x, y, c = _here()
    order = BIG + TINY
    slots = []
    for j in range(N_SHARD):
        pieces = []
        for n in order:
            ax, w = SHARD_AXIS[n], P[n].shape[SHARD_AXIS[n]]
            pieces.append(lax.slice_in_dim(grads[n], j * w, (j + 1) * w, axis=ax).reshape(-1))
        n_el = sum(int(p.shape[0]) for p in pieces)
        pieces.append(jnp.zeros(((-n_el) % (2 * 1024 * 128),), F32))
        slots.append(jnp.concatenate(pieces))
    g = jnp.stack(slots)
    n_el = g.shape[1]
    H = n_el // (2 * 128)
    g = g.reshape(N_SHARD, 2, H, 128)
    cidx = jnp.reshape(c, (1,)).astype(jnp.int32)
    chip = jnp.reshape(2 * x + y, (1,)).astype(jnp.int32)
    s = _add_half(g, _swap_halves(g, "rs_swap_halves"), cidx, "rs_add_half")
    tot = _add_chips(s, _scatter_to_chips(s, "rs_scatter"), chip, "rs_add_chips")
    flat = _join_halves(tot, "rs_join_halves").reshape(-1)
    out, off = {}, 0
    for n in order:
        sz = int(np.prod(P[n].shape))
        out[n] = flat[off:off + sz].reshape(P[n].shape)
        off += sz
    small = _sum_lead(_all_gather_small(_pack_small(grads), "ag_small_grads"), "sum_small_grads")
    out.update(_unpack_small(small, {n: P[n].shape for n in SMALL}))
    return out


def _update(P, Gd, M, V):
    delta, new_m, new_v = {}, {}, {}
    for n in BIG + TINY:
        shp = P[n].shape
        two = (int(np.prod(shp[:-1])), shp[-1])
        d, m, v = _adamw(P[n].reshape(two), Gd[n].reshape(two), M[n].reshape(two), V[n].reshape(two), "adamw_" + n)
        delta[n], new_m[n], new_v[n] = d.reshape(shp), m.reshape(shp), v.reshape(shp)
    shapes = {n: P[n].shape for n in SMALL}
    d, m, v = _adamw(_pack_small(P), _pack_small(Gd), _pack_small(M), _pack_small(V), "adamw_small")
    for dst, buf in ((delta, d), (new_m, m), (new_v, v)):
        dst.update(_unpack_small(buf, shapes))
    return delta, new_m, new_v


def kernel(x, meta_tokens, rel_bias_table, norm_mix, w_in, swa_sinks, fox_forget_bias, conv_w, conv_b, lru_w_r, lru_b_r, lru_w_i, lru_b_i, lru_lambda, w_branch, w_out, norm_ffn, w_ffn_in, w_ffn_out, norm_final, loss_target, m_meta_tokens, m_rel_bias_table, m_norm_mix, m_w_in, m_swa_sinks, m_fox_forget_bias, m_conv_w, m_conv_b, m_lru_w_r, m_lru_b_r, m_lru_w_i, m_lru_b_i, m_lru_lambda, m_w_branch, m_w_out, m_norm_ffn, m_w_ffn_in, m_w_ffn_out, m_norm_final, v_meta_tokens, v_rel_bias_table, v_norm_mix, v_w_in, v_swa_sinks, v_fox_forget_bias, v_conv_w, v_conv_b, v_lru_w_r, v_lru_b_r, v_lru_w_i, v_lru_b_i, v_lru_lambda, v_w_branch, v_w_out, v_norm_ffn, v_w_ffn_in, v_w_ffn_out, v_norm_final):
    P = dict(zip(NAMES, (meta_tokens, rel_bias_table, norm_mix, w_in, swa_sinks, fox_forget_bias, conv_w, conv_b, lru_w_r,
                         lru_b_r, lru_w_i, lru_b_i, lru_lambda, w_branch, w_out, norm_ffn, w_ffn_in, w_ffn_out, norm_final)))
    M = dict(zip(NAMES, (m_meta_tokens, m_rel_bias_table, m_norm_mix, m_w_in, m_swa_sinks, m_fox_forget_bias, m_conv_w,
                         m_conv_b, m_lru_w_r, m_lru_b_r, m_lru_w_i, m_lru_b_i, m_lru_lambda, m_w_branch, m_w_out, m_norm_ffn,
                         m_w_ffn_in, m_w_ffn_out, m_norm_final)))
    V = dict(zip(NAMES, (v_meta_tokens, v_rel_bias_table, v_norm_mix, v_w_in, v_swa_sinks, v_fox_forget_bias, v_conv_w,
                         v_conv_b, v_lru_w_r, v_lru_b_r, v_lru_w_i, v_lru_b_i, v_lru_lambda, v_w_branch, v_w_out, v_norm_ffn,
                         v_w_ffn_in, v_w_ffn_out, v_norm_final)))
    W = _gather_weights(P)
    loss_local, grad_x, grads = _local_step(x[0], loss_target[0], W)
    loss = lax.psum(loss_local, ("x", "y", "c"))
    Gd = _reduce_grads(grads, P)
    delta, new_m, new_v = _update(P, Gd, M, V)
    return (loss, grad_x[None], *[Gd[n] for n in NAMES], *[delta[n] for n in NAMES],
            *[new_m[n] for n in NAMES], *[new_v[n] for n in NAMES])
```

```python
import functools
import math

import numpy as np
import jax
import jax.numpy as jnp
from jax import lax
from jax.experimental import pallas as pl
from jax.experimental.pallas import tpu as pltpu

F32, BF16 = jnp.float32, jnp.bfloat16
MESH = pl.DeviceIdType.MESH
ANY = pl.BlockSpec(memory_space=pl.ANY)
SMEM = pl.BlockSpec(memory_space=pltpu.SMEM)

D = 1024
DEPTH = 4
BLK = 128
N_META = 16
NPAD = 112
NH = 8
LW = 512
DFF = 2816
EPS = 1e-6
NEG = -1e30
SCALE = 0.125
LRU_C = 8.0
REL_BUCKETS = 32
N_SHARD = 4
QA, QF, KF, VF, XC, YC, GT, KA, VA, FL, INP = 0, 512, 1024, 1536, 2048, 2560, 3072, 6144, 6272, 6400, 6656
IN_COLS = 6408
VMEM_LIMIT = 48 * 1024 * 1024

ADAM_LR, ADAM_B1, ADAM_B2, ADAM_EPS, ADAM_WD, ADAM_STEP = 0.001, 0.9, 0.999, 1e-08, 0.01, 10


def _cp(*sem):
    return pltpu.CompilerParams(dimension_semantics=sem or None, vmem_limit_bytes=VMEM_LIMIT)


def _pick(n, prefs):
    for p in prefs:
        if n % p == 0:
            return p
    return n


def _rt(T):
    return _pick(T, (384, 128))


def _sigmoid(z):
    return 1.0 / (1.0 + jnp.exp(-z))


def _log_sigmoid(z):
    return jnp.minimum(z, 0.0) - jnp.log(1.0 + jnp.exp(-jnp.abs(z)))


def _gelu(y):
    c = math.sqrt(2.0 / math.pi)
    return 0.5 * y * (1.0 + jnp.tanh(c * (y + 0.044715 * y * y * y)))


def _gelu_grad(y):
    c = math.sqrt(2.0 / math.pi)
    t = jnp.tanh(c * (y + 0.044715 * y * y * y))
    return 0.5 * (1.0 + t) + 0.5 * y * (1.0 - t * t) * c * (1.0 + 3.0 * 0.044715 * y * y)


def _neg_expm1(z):
    series = -z * (1.0 + z * (0.5 + z * (1.0 / 6.0 + z * (1.0 / 24.0 + z * (1.0 / 120.0)))))
    return jnp.where(z > -0.1, series, 1.0 - jnp.exp(z))


def _dot(a, b, ca, cb):
    return lax.dot_general(a, b, (((ca,), (cb,)), ((), ())), preferred_element_type=F32)


def _mm(a, b, *, ta=False, tb=False, res=None, out_dtype=F32, tm, tn, tk, name, slab=None):
    M, K = (a.shape[1], a.shape[0]) if ta else a.shape
    N = b.shape[0] if tb else b.shape[1]
    assert (b.shape[1] if tb else b.shape[0]) == K and M % tm == 0 and N % tn == 0 and K % tk == 0, (name, a.shape, b.shape)
    nk = K // tk
    ca, cb = (0 if ta else 1), (1 if tb else 0)
    n_in = 2 + (res is not None) + (slab is not None and slab[0] is not None)

    def body(*refs):
        a_ref, b_ref = refs[:2]
        r_ref = refs[2] if res is not None else None
        o_ref = refs[n_in]
        part = _dot(a_ref[...].astype(BF16), b_ref[...].astype(BF16), ca, cb)

        def fin(acc):
            if res is not None:
                acc = acc + r_ref[...]
            o_ref[...] = acc.astype(out_dtype)

        if nk == 1:
            fin(part)
        else:
            acc_ref = refs[-1]
            k = pl.program_id(2)

            @pl.when(k == 0)
            def _():
                acc_ref[...] = part

            @pl.when(k > 0)
            def _():
                acc_ref[...] += part

            @pl.when(k == nk - 1)
            def _():
                fin(acc_ref[...])

    a_spec = pl.BlockSpec((tk, tm), lambda i, j, k: (k, i)) if ta else pl.BlockSpec((tm, tk), lambda i, j, k: (i, k))
    b_spec = pl.BlockSpec((tn, tk), lambda i, j, k: (j, k)) if tb else pl.BlockSpec((tk, tn), lambda i, j, k: (k, j))
    o_spec = pl.BlockSpec((tm, tn), lambda i, j, k: (i, j))
    in_specs, ops = [a_spec, b_spec], [a, b]
    if res is not None:
        in_specs.append(o_spec)
        ops.append(res)
    out_shape, aliases = jax.ShapeDtypeStruct((M, N), out_dtype), {}
    if slab is not None:
        buf, idx, n = slab
        o_spec = pl.BlockSpec((None, tm, tn), lambda i, j, k: (idx, i, j))
        out_shape = jax.ShapeDtypeStruct((n, M, N), out_dtype)
        if buf is not None:
            aliases = {len(ops): 0}
            in_specs.append(ANY)
            ops.append(buf)
    return pl.pallas_call(
        body, grid=(M // tm, N // tn, nk), in_specs=in_specs, out_specs=o_spec, out_shape=out_shape,
        input_output_aliases=aliases, scratch_shapes=[pltpu.VMEM((tm, tn), F32)] if nk > 1 else [],
        compiler_params=_cp("parallel", "parallel", "arbitrary"), name=name)(*ops)


def _transpose(x, name):
    T, C = x.shape
    tr, tc = _rt(T), _pick(C, (512, 256, 128))

    def body(x_ref, o_ref):
        o_ref[...] = x_ref[...].T

    return pl.pallas_call(
        body, grid=(T // tr, C // tc), in_specs=[pl.BlockSpec((tr, tc), lambda i, j: (i, j))],
        out_specs=pl.BlockSpec((tc, tr), lambda i, j: (j, i)), out_shape=jax.ShapeDtypeStruct((C, T), x.dtype),
        compiler_params=_cp("parallel", "parallel"), name=name)(x)


def _rms_fwd(h, g, name):
    T = h.shape[0]
    tr = _rt(T)

    def body(h_ref, g_ref, u_ref):
        x = h_ref[...]
        r = lax.rsqrt(jnp.mean(x * x, axis=-1, keepdims=True) + EPS)
        u_ref[...] = (x * r * g_ref[...]).astype(BF16)

    return pl.pallas_call(
        body, grid=(T // tr,),
        in_specs=[pl.BlockSpec((tr, D), lambda i: (i, 0)), pl.BlockSpec((1, D), lambda i: (0, 0))],
        out_specs=pl.BlockSpec((tr, D), lambda i: (i, 0)), out_shape=jax.ShapeDtypeStruct((T, D), BF16),
        compiler_params=_cp("parallel"), name=name)(h, g.reshape(1, D))


def _rms_bwd(du, h, g, dres, name):
    T = h.shape[0]
    tr = _rt(T)

    def body(du_ref, h_ref, g_ref, dres_ref, dh_ref, dhb_ref, dg_ref):
        x = h_ref[...]
        r = lax.rsqrt(jnp.mean(x * x, axis=-1, keepdims=True) + EPS)
        xh = x * r
        dy = du_ref[...]
        dxh = dy * g_ref[...]
        dx = r * (dxh - xh * jnp.mean(dxh * xh, axis=-1, keepdims=True))
        dh = dres_ref[...] + dx
        dh_ref[...] = dh
        dhb_ref[...] = dh.astype(BF16)
        part = jnp.sum(dy * xh, axis=0, keepdims=True)

        @pl.when(pl.program_id(0) == 0)
        def _():
            dg_ref[...] = part

        @pl.when(pl.program_id(0) > 0)
        def _():
            dg_ref[...] += part

    row = pl.BlockSpec((tr, D), lambda i: (i, 0))
    vec = pl.BlockSpec((1, D), lambda i: (0, 0))
    return pl.pallas_call(
        body, grid=(T // tr,), in_specs=[row, row, vec, row], out_specs=[row, row, vec],
        out_shape=[jax.ShapeDtypeStruct((T, D), F32), jax.ShapeDtypeStruct((T, D), BF16), jax.ShapeDtypeStruct((1, D), F32)],
        compiler_params=_cp("arbitrary"), name=name)(du, h, g.reshape(1, D), dres)


def _loss_head(h, tgt, g, name):
    T = h.shape[0]
    nb = T // BLK

    def body(h_ref, t_ref, g_ref, dh_ref, dhb_ref, dg_ref, loss_ref):
        i = pl.program_id(0)
        x = h_ref[...]
        r = lax.rsqrt(jnp.mean(x * x, axis=-1, keepdims=True) + EPS)
        xh = x * r
        gv = g_ref[...]
        tok = i >= 1
        err = jnp.where(tok, xh * gv - t_ref[...], 0.0)
        dy = err * (1.0 / D)
        dxh = dy * gv
        dx = r * (dxh - xh * jnp.mean(dxh * xh, axis=-1, keepdims=True))
        dh_ref[...] = dx
        dhb_ref[...] = dx.astype(BF16)
        dg = jnp.sum(dy * xh, axis=0, keepdims=True)
        ls = jnp.zeros((1, BLK), F32) + jnp.sum(err * err) * (0.5 / D)

        @pl.when(i == 0)
        def _():
            dg_ref[...] = dg
            loss_ref[...] = ls

        @pl.when(i > 0)
        def _():
            dg_ref[...] += dg
            loss_ref[...] += ls

    row = pl.BlockSpec((BLK, D), lambda i: (i, 0))
    vec = pl.BlockSpec((1, D), lambda i: (0, 0))
    return pl.pallas_call(
        body, grid=(nb,),
        in_specs=[row, pl.BlockSpec((BLK, D), lambda i: (jnp.maximum(i - 1, 0), 0)), vec],
        out_specs=[row, row, vec, pl.BlockSpec((1, BLK), lambda i: (0, 0))],
        out_shape=[jax.ShapeDtypeStruct((T, D), F32), jax.ShapeDtypeStruct((T, D), BF16),
                   jax.ShapeDtypeStruct((1, D), F32), jax.ShapeDtypeStruct((1, BLK), F32)],
        compiler_params=_cp("arbitrary"), name=name)(h, tgt, g.reshape(1, D))


def _bucket_table():
    q = np.arange(BLK)[:, None]
    k = np.arange(2 * BLK)[None, :]
    d = np.maximum(q + BLK - k, 0)
    max_exact = REL_BUCKETS // 2
    scaled = np.log(np.maximum(d, 1).astype(np.float32) / np.float32(max_exact)) / np.float32(math.log(128 / max_exact))
    large = np.minimum(max_exact + (scaled.astype(np.float32) * (REL_BUCKETS - max_exact)).astype(np.int32), REL_BUCKETS - 1)
    return np.where(d < max_exact, d, large).astype(np.int32)


def _bias_build(table, bucket, name):
    def body(t_ref, bk_ref, o_ref):
        bk = bk_ref[...]
        for h in range(NH):
            acc = jnp.zeros((BLK, 2 * BLK), F32)
            for b in range(REL_BUCKETS):
                acc = jnp.where(bk == b, t_ref[b, h], acc)
            o_ref[h] = acc

    return pl.pallas_call(
        body, in_specs=[SMEM, pl.BlockSpec(memory_space=pltpu.VMEM)], out_specs=pl.BlockSpec(memory_space=pltpu.VMEM),
        out_shape=jax.ShapeDtypeStruct((NH, BLK, 2 * BLK), F32), compiler_params=_cp(), name=name)(table, bucket)


def _bias_bwd(dbias, bucket, name):
    def body(d_ref, bk_ref, o_ref):
        bk = bk_ref[...]
        for h in range(NH):
            dh = d_ref[h]
            for b in range(REL_BUCKETS):
                o_ref[b, h] = jnp.sum(jnp.where(bk == b, dh, 0.0))

    return pl.pallas_call(
        body, in_specs=[pl.BlockSpec(memory_space=pltpu.VMEM)] * 2, out_specs=SMEM,
        out_shape=jax.ShapeDtypeStruct((REL_BUCKETS, NH), F32), compiler_params=_cp(), name=name)(dbias, bucket)


def _swa_specs(nq_cols):
    prev = lambda n: jnp.maximum(n - 1, 0)
    return [
        pl.BlockSpec((BLK, nq_cols), lambda n: (n, QA // nq_cols)),
        pl.BlockSpec((BLK, BLK), lambda n: (prev(n), KA // BLK)), pl.BlockSpec((BLK, BLK), lambda n: (n, KA // BLK)),
        pl.BlockSpec((BLK, BLK), lambda n: (prev(n), VA // BLK)), pl.BlockSpec((BLK, BLK), lambda n: (n, VA // BLK)),
    ]


def _swa_mask(n):
    row = lax.broadcasted_iota(jnp.int32, (BLK, 2 * BLK), 0)
    col = lax.broadcasted_iota(jnp.int32, (BLK, 2 * BLK), 1)
    dist = row + BLK - col
    return (dist >= 0) & (dist < BLK) & ((n - 1) * BLK + col >= NPAD)


def _swa_probs(qm, ksel, mask, bias_h, sink):
    s = _dot(qm, ksel, 1, 1) * SCALE
    s = jnp.where(mask, s + bias_h, NEG)
    m = jnp.maximum(jnp.max(s, axis=-1, keepdims=True), sink)
    p = jnp.exp(s - m)
    psink = jnp.exp(sink - m)
    inv = 1.0 / (jnp.sum(p, axis=-1, keepdims=True) + psink)
    return p * inv, psink * inv


def _swa_fwd(proj, bias, sinks, name):
    T = proj.shape[0]
    nb = T // BLK

    def body(sk_ref, q_ref, kp_ref, kc_ref, vp_ref, vc_ref, b_ref, o_ref):
        n = pl.program_id(0)
        lo = lax.broadcasted_iota(jnp.int32, (1, BLK), 1) < 64
        kb = jnp.concatenate([kp_ref[...], kc_ref[...]], axis=0)
        vb = jnp.concatenate([vp_ref[...], vc_ref[...]], axis=0)
        kbs = (kb.astype(BF16), pltpu.roll(kb, 64, 1).astype(BF16))
        vbs = (vb, pltpu.roll(vb, 64, 1))
        mask = _swa_mask(n)
        outs = []
        for pr in range(NH // 2):
            qp = q_ref[:, pr * BLK:(pr + 1) * BLK]
            kv = pr // 2
            acc = jnp.zeros((BLK, BLK), F32)
            for e in range(2):
                lm = lo if e == 0 else jnp.logical_not(lo)
                sw = 0 if kv == e else 1
                qm = jnp.where(lm, qp, 0.0).astype(BF16)
                pn, _ = _swa_probs(qm, kbs[sw], mask, b_ref[2 * pr + e], sk_ref[2 * pr + e])
                acc = acc + _dot(pn.astype(BF16), jnp.where(lm, vbs[sw], 0.0).astype(BF16), 1, 0)
            outs.append(acc)
        o_ref[...] = jnp.concatenate(outs, axis=1).astype(BF16)

    return pl.pallas_call(
        body, grid=(nb,),
        in_specs=[SMEM] + _swa_specs(512) + [pl.BlockSpec((NH, BLK, 2 * BLK), lambda n: (0, 0, 0))],
        out_specs=pl.BlockSpec((BLK, 512), lambda n: (n, 0)), out_shape=jax.ShapeDtypeStruct((T, 512), BF16),
        compiler_params=_cp("parallel"), name=name)(sinks, proj, proj, proj, proj, proj, bias)


def _swa_bwd(proj, bias, sinks, do, dbias_in, name):
    T = proj.shape[0]
    nb = T // BLK

    def body(sk_ref, q_ref, kp_ref, kc_ref, vp_ref, vc_ref, b_ref, do_ref, dbi_ref,
             dq_ref, dk_ref, dv_ref, db_ref, dsk_ref, sk_acc):
        n = pl.program_id(0)
        lane = lax.broadcasted_iota(jnp.int32, (1, BLK), 1)
        lo = lane < 64
        kb = jnp.concatenate([kp_ref[...], kc_ref[...]], axis=0)
        vb = jnp.concatenate([vp_ref[...], vc_ref[...]], axis=0)
        kbs = (kb, pltpu.roll(kb, 64, 1))
        vbs = (vb, pltpu.roll(vb, 64, 1))
        mask = _swa_mask(n)

        @pl.when(n == 0)
        def _():
            db_ref[...] = dbi_ref[...]
            sk_acc[...] = jnp.zeros_like(sk_acc)

        dqs = []
        dk = jnp.zeros((2 * BLK, BLK), F32)
        dv = jnp.zeros((2 * BLK, BLK), F32)
        for pr in range(NH // 2):
            qp = q_ref[:, pr * BLK:(pr + 1) * BLK]
            dop = do_ref[:, pr * BLK:(pr + 1) * BLK].astype(F32)
            kv = pr // 2
            dq = jnp.zeros((BLK, BLK), F32)
            for e in range(2):
                h = 2 * pr + e
                lm = lo if e == 0 else jnp.logical_not(lo)
                sw = 0 if kv == e else 1
                qm = jnp.where(lm, qp, 0.0)
                dom = jnp.where(lm, dop, 0.0)
                pn, ps = _swa_probs(qm.astype(BF16), kbs[sw].astype(BF16), mask, b_ref[h], sk_ref[h])
                dp = _dot(dom.astype(BF16), vbs[sw].astype(BF16), 1, 1)
                delta = jnp.sum(pn * dp, axis=-1, keepdims=True)
                ds = pn * (dp - delta)
                db_ref[h] += ds
                sk_acc[...] += jnp.where(lane == h, -(ps * delta), 0.0)
                dsb = (ds * SCALE).astype(BF16)
                dq = dq + _dot(dsb, jnp.where(lm, kbs[sw], 0.0).astype(BF16), 1, 0)
                qk = qm if sw == 0 else pltpu.roll(qm, 64, 1)
                dok = dom if sw == 0 else pltpu.roll(dom, 64, 1)
                dk = dk + _dot(dsb, qk.astype(BF16), 0, 0)
                dv = dv + _dot(pn.astype(BF16), dok.astype(BF16), 0, 0)
            dqs.append(dq)
        dq_ref[...] = jnp.concatenate(dqs, axis=1).astype(BF16)
        dk_ref[0] = dk
        dv_ref[0] = dv

        @pl.when(n == nb - 1)
        def _():
            dsk_ref[...] = jnp.sum(sk_acc[...], axis=0, keepdims=True)

    full_b = pl.BlockSpec((NH, BLK, 2 * BLK), lambda n: (0, 0, 0))
    band = pl.BlockSpec((1, 2 * BLK, BLK), lambda n: (n, 0, 0))
    return pl.pallas_call(
        body, grid=(nb,),
        in_specs=[SMEM] + _swa_specs(512) + [full_b, pl.BlockSpec((BLK, 512), lambda n: (n, 0)), full_b],
        out_specs=[pl.BlockSpec((BLK, 512), lambda n: (n, 0)), band, band, full_b, pl.BlockSpec((1, BLK), lambda n: (0, 0))],
        out_shape=[jax.ShapeDtypeStruct((T, 512), BF16), jax.ShapeDtypeStruct((nb, 2 * BLK, BLK), F32),
                   jax.ShapeDtypeStruct((nb, 2 * BLK, BLK), F32), jax.ShapeDtypeStruct((NH, BLK, 2 * BLK), F32),
                   jax.ShapeDtypeStruct((1, BLK), F32)],
        scratch_shapes=[pltpu.VMEM((BLK, BLK), F32)],
        compiler_params=_cp("arbitrary"), name=name)(sinks, proj, proj, proj, proj, proj, bias, do, dbias_in)


def _band_fold(dkb, dvb, name):
    nb = dkb.shape[0]

    def body(ko_ref, kn_ref, vo_ref, vn_ref, dk_ref, dv_ref):
        last = pl.program_id(0) == nb - 1
        dk_ref[...] = (ko_ref[0] + jnp.where(last, 0.0, kn_ref[0])).astype(BF16)
        dv_ref[...] = (vo_ref[0] + jnp.where(last, 0.0, vn_ref[0])).astype(BF16)

    own = pl.BlockSpec((1, BLK, BLK), lambda j: (j, 1, 0))
    nxt = pl.BlockSpec((1, BLK, BLK), lambda j: (jnp.minimum(j + 1, nb - 1), 0, 0))
    out = pl.BlockSpec((BLK, BLK), lambda j: (j, 0))
    return pl.pallas_call(
        body, grid=(nb,), in_specs=[own, nxt, own, nxt], out_specs=[out, out],
        out_shape=[jax.ShapeDtypeStruct((nb * BLK, BLK), BF16)] * 2,
        compiler_params=_cp("parallel"), name=name)(dkb, dkb, dvb, dvb)


def _cum_fwd(z3, fb, name):
    nb = z3.shape[0]

    def body(z_ref, fb_ref, c_ref):
        lane = lax.broadcasted_iota(jnp.int32, (NH, BLK), 1)

        def step(b, carry):
            x = jnp.where(b * BLK + lane >= NPAD, _log_sigmoid(z_ref[b] + fb_ref[...]), 0.0)
            s = 1
            while s < BLK:
                x = x + jnp.where(lane >= s, pltpu.roll(x, s, 1), 0.0)
                s *= 2
            x = x + carry
            c_ref[b] = x
            return jnp.sum(jnp.where(lane == BLK - 1, x, 0.0), axis=-1, keepdims=True)

        lax.fori_loop(0, nb, step, jnp.zeros((NH, 1), F32))

    return pl.pallas_call(body, out_shape=jax.ShapeDtypeStruct((nb, NH, BLK), F32), compiler_params=_cp(), name=name)(z3, fb)


def _cum_bwd(dck3, dcq3, z3, fb, name):
    nb = z3.shape[0]

    def body(d_ref, dq_ref, z_ref, fb_ref, dz_ref, db_ref):
        lane = lax.broadcasted_iota(jnp.int32, (NH, BLK), 1)

        def step(k, carry):
            suffix, tot = carry
            b = nb - 1 - k
            x = d_ref[b] + dq_ref[b]
            s = 1
            while s < BLK:
                x = x + jnp.where(lane < BLK - s, pltpu.roll(x, BLK - s, 1), 0.0)
                s *= 2
            x = x + suffix
            dz = jnp.where(b * BLK + lane >= NPAD, x * _sigmoid(-(z_ref[b] + fb_ref[...])), 0.0)
            dz_ref[b] = dz
            return (jnp.sum(jnp.where(lane == 0, x, 0.0), axis=-1, keepdims=True),
                    tot + jnp.sum(dz, axis=-1, keepdims=True))

        z0 = jnp.zeros((NH, 1), F32)
        _, tot = lax.fori_loop(0, nb, step, (z0, z0))
        db_ref[...] = jnp.zeros((NH, BLK), F32) + tot

    return pl.pallas_call(
        body, out_shape=[jax.ShapeDtypeStruct((nb, NH, BLK), F32), jax.ShapeDtypeStruct((NH, BLK), F32)],
        compiler_params=_cp(), name=name)(dck3, dcq3, z3, fb)


def _to_blocks(a):
    return a.reshape(NH, -1, BLK).transpose(1, 0, 2)


def _from_blocks(a):
    return a.transpose(1, 0, 2).reshape(NH, -1)


def _fox_scores(q, kbf, mask, lm, cq, ck):
    s = _dot(jnp.where(lm, q, 0.0).astype(BF16), kbf, 1, 1) * SCALE
    return jnp.where(mask, s + cq - ck, NEG)


def _fox_mask(i, j, t):
    row = i * t + lax.broadcasted_iota(jnp.int32, (t, t), 0)
    col = j * t + lax.broadcasted_iota(jnp.int32, (t, t), 1)
    return (col <= row) & (col >= NPAD)


def _lane_pick(x, lane, idx):
    return jnp.sum(jnp.where(lane == idx, x, 0.0), axis=-1, keepdims=True)


def _fox_fwd(proj, crow, ccol, name):
    T = proj.shape[0]
    t = _rt(T)
    nt = T // t

    def body(q_ref, k_ref, v_ref, cr_ref, cc_ref, o_ref, lse_ref, m_ref, l_ref, acc_ref):
        p_, i, j = pl.program_id(0), pl.program_id(1), pl.program_id(2)
        lane = lax.broadcasted_iota(jnp.int32, (1, BLK), 1)
        lo = lane < 64

        @pl.when(j == 0)
        def _():
            m_ref[...] = jnp.full_like(m_ref, NEG)
            l_ref[...] = jnp.zeros_like(l_ref)
            acc_ref[...] = jnp.zeros_like(acc_ref)

        @pl.when(j <= i)
        def _():
            q = q_ref[...]
            kbf = k_ref[...].astype(BF16)
            v = v_ref[...]
            cc = cc_ref[...]
            mask = _fox_mask(i, j, t)
            alphas, pvs = [], []
            for e in range(2):
                h = 2 * p_ + e
                lm = lo if e == 0 else jnp.logical_not(lo)
                s = _fox_scores(q, kbf, mask, lm, _lane_pick(cc, lane, h), cr_ref[pl.ds(h, 1), :])
                m_old = m_ref[e]
                m_new = jnp.maximum(m_old, jnp.max(s, axis=-1, keepdims=True))
                alpha = jnp.exp(m_old - m_new)
                pe = jnp.exp(s - m_new)
                l_ref[e] = alpha * l_ref[e] + jnp.sum(pe, axis=-1, keepdims=True)
                m_ref[e] = m_new
                alphas.append(alpha)
                pvs.append(_dot(pe.astype(BF16), jnp.where(lm, v, 0.0).astype(BF16), 1, 0))
            acc_ref[...] = acc_ref[...] * jnp.where(lo, alphas[0], alphas[1]) + pvs[0] + pvs[1]

        @pl.when(j == i)
        def _():
            rows = i * t + lax.broadcasted_iota(jnp.int32, (t, 1), 0)
            o = acc_ref[...] / jnp.where(lo, l_ref[0], l_ref[1])
            o_ref[...] = jnp.where(rows >= NPAD, o, 0.0).astype(BF16)
            lse_ref[...] = jnp.where(lo, m_ref[0] + jnp.log(l_ref[0]), m_ref[1] + jnp.log(l_ref[1]))

    kj = lambda i, j: jnp.minimum(j, i)
    return pl.pallas_call(
        body, grid=(NH // 2, nt, nt),
        in_specs=[pl.BlockSpec((t, BLK), lambda p, i, j: (i, QF // BLK + p)),
                  pl.BlockSpec((t, BLK), lambda p, i, j: (kj(i, j), KF // BLK + p)),
                  pl.BlockSpec((t, BLK), lambda p, i, j: (kj(i, j), VF // BLK + p)),
                  pl.BlockSpec((NH, t), lambda p, i, j: (0, kj(i, j))),
                  pl.BlockSpec((t, BLK), lambda p, i, j: (i, 0))],
        out_specs=[pl.BlockSpec((t, BLK), lambda p, i, j: (i, p))] * 2,
        out_shape=[jax.ShapeDtypeStruct((T, 512), BF16), jax.ShapeDtypeStruct((T, 512), F32)],
        scratch_shapes=[pltpu.VMEM((2, t, 1), F32), pltpu.VMEM((2, t, 1), F32), pltpu.VMEM((t, BLK), F32)],
        compiler_params=_cp("parallel", "parallel", "arbitrary"), name=name)(proj, proj, proj, crow, ccol)


def _fox_dq(proj, crow, ccol, o, lse, do, name):
    T = proj.shape[0]
    t = _rt(T)
    nt = T // t

    def body(q_ref, k_ref, v_ref, cr_ref, cc_ref, o_ref, lse_ref, do_ref, dq_ref, rs_ref, acc_ref, dl_ref, rs_acc):
        p_, i, j = pl.program_id(0), pl.program_id(1), pl.program_id(2)
        lane = lax.broadcasted_iota(jnp.int32, (1, BLK), 1)
        lo = lane < 64

        @pl.when(j == 0)
        def _():
            acc_ref[...] = jnp.zeros_like(acc_ref)
            rs_acc[...] = jnp.zeros_like(rs_acc)
            prod = do_ref[...].astype(F32) * o_ref[...].astype(F32)
            dl_ref[0] = jnp.sum(jnp.where(lo, prod, 0.0), axis=-1, keepdims=True)
            dl_ref[1] = jnp.sum(jnp.where(lo, 0.0, prod), axis=-1, keepdims=True)

        @pl.when(j <= i)
        def _():
            q = q_ref[...]
            k = k_ref[...]
            kbf = k.astype(BF16)
            vbf = v_ref[...].astype(BF16)
            cc = cc_ref[...]
            do_ = do_ref[...].astype(F32)
            lse_ = lse_ref[...]
            mask = _fox_mask(i, j, t)
            acc = acc_ref[...]
            for e in range(2):
                h = 2 * p_ + e
                lm = lo if e == 0 else jnp.logical_not(lo)
                s = _fox_scores(q, kbf, mask, lm, _lane_pick(cc, lane, h), cr_ref[pl.ds(h, 1), :])
                pe = jnp.exp(s - _lane_pick(lse_, lane, 64 * e))
                dp = _dot(jnp.where(lm, do_, 0.0).astype(BF16), vbf, 1, 1)
                ds = pe * (dp - dl_ref[e])
                rs_acc[e] += jnp.sum(ds, axis=-1, keepdims=True)
                acc = acc + _dot((ds * SCALE).astype(BF16), jnp.where(lm, k, 0.0).astype(BF16), 1, 0)
            acc_ref[...] = acc

        @pl.when(j == i)
        def _():
            dq_ref[...] = acc_ref[...].astype(BF16)
            rs_ref[...] = jnp.where(lo, rs_acc[0], rs_acc[1])

    kj = lambda i, j: jnp.minimum(j, i)
    qside = pl.BlockSpec((t, BLK), lambda p, i, j: (i, p))
    return pl.pallas_call(
        body, grid=(NH // 2, nt, nt),
        in_specs=[pl.BlockSpec((t, BLK), lambda p, i, j: (i, QF // BLK + p)),
                  pl.BlockSpec((t, BLK), lambda p, i, j: (kj(i, j), KF // BLK + p)),
                  pl.BlockSpec((t, BLK), lambda p, i, j: (kj(i, j), VF // BLK + p)),
                  pl.BlockSpec((NH, t), lambda p, i, j: (0, kj(i, j))),
                  pl.BlockSpec((t, BLK), lambda p, i, j: (i, 0)), qside, qside, qside],
        out_specs=[qside, qside], out_shape=[jax.ShapeDtypeStruct((T, 512), BF16), jax.ShapeDtypeStruct((T, 512), F32)],
        scratch_shapes=[pltpu.VMEM((t, BLK), F32), pltpu.VMEM((2, t, 1), F32), pltpu.VMEM((2, t, 1), F32)],
        compiler_params=_cp("parallel", "parallel", "arbitrary"), name=name)(proj, proj, proj, crow, ccol, o, lse, do)


def _fox_dkv(proj, crow, ccol, o, lse, do, name):
    T = proj.shape[0]
    t = _rt(T)
    nt = T // t

    def body(q_ref, k_ref, v_ref, cr_ref, cc_ref, o_ref, lse_ref, do_ref, dk_ref, dv_ref, dc_ref, dk_acc, dv_acc, dc_acc):
        p_, j, i = pl.program_id(0), pl.program_id(1), pl.program_id(2)
        lane = lax.broadcasted_iota(jnp.int32, (1, BLK), 1)
        lo = lane < 64

        @pl.when(i == 0)
        def _():
            dk_acc[...] = jnp.zeros_like(dk_acc)
            dv_acc[...] = jnp.zeros_like(dv_acc)
            dc_acc[...] = jnp.zeros_like(dc_acc)

        @pl.when(i >= j)
        def _():
            q = q_ref[...]
            kbf = k_ref[...].astype(BF16)
            vbf = v_ref[...].astype(BF16)
            cc = cc_ref[...]
            do_ = do_ref[...].astype(F32)
            lse_ = lse_ref[...]
            prod = do_ * o_ref[...].astype(F32)
            mask = _fox_mask(i, j, t)
            dk = dk_acc[...]
            dv = dv_acc[...]
            for e in range(2):
                h = 2 * p_ + e
                lm = lo if e == 0 else jnp.logical_not(lo)
                s = _fox_scores(q, kbf, mask, lm, _lane_pick(cc, lane, h), cr_ref[pl.ds(h, 1), :])
                pe = jnp.exp(s - _lane_pick(lse_, lane, 64 * e))
                dom = jnp.where(lm, do_, 0.0).astype(BF16)
                dp = _dot(dom, vbf, 1, 1)
                delta = jnp.sum(jnp.where(lm, prod, 0.0), axis=-1, keepdims=True)
                ds = pe * (dp - delta)
                dv = dv + _dot(pe.astype(BF16), dom, 0, 0)
                dk = dk + _dot((ds * SCALE).astype(BF16), jnp.where(lm, q, 0.0).astype(BF16), 0, 0)
                dc_acc[e:e + 1, :] -= jnp.sum(ds, axis=0, keepdims=True)
            dk_acc[...] = dk
            dv_acc[...] = dv

        @pl.when(i == nt - 1)
        def _():
            dk_ref[...] = dk_acc[...].astype(BF16)
            dv_ref[...] = dv_acc[...].astype(BF16)
            dc_ref[0] = dc_acc[...]

    qi = lambda j, i: jnp.maximum(i, j)
    qside = pl.BlockSpec((t, BLK), lambda p, j, i: (qi(j, i), p))
    kside = pl.BlockSpec((t, BLK), lambda p, j, i: (j, p))
    return pl.pallas_call(
        body, grid=(NH // 2, nt, nt),
        in_specs=[pl.BlockSpec((t, BLK), lambda p, j, i: (qi(j, i), QF // BLK + p)),
                  pl.BlockSpec((t, BLK), lambda p, j, i: (j, KF // BLK + p)),
                  pl.BlockSpec((t, BLK), lambda p, j, i: (j, VF // BLK + p)),
                  pl.BlockSpec((NH, t), lambda p, j, i: (0, j)),
                  pl.BlockSpec((t, BLK), lambda p, j, i: (qi(j, i), 0)), qside, qside, qside],
        out_specs=[kside, kside, pl.BlockSpec((1, NH, t), lambda p, j, i: (p, 0, j))],
        out_shape=[jax.ShapeDtypeStruct((T, 512), BF16), jax.ShapeDtypeStruct((T, 512), BF16),
                   jax.ShapeDtypeStruct((NH // 2, NH, T), F32)],
        scratch_shapes=[pltpu.VMEM((t, BLK), F32), pltpu.VMEM((t, BLK), F32), pltpu.VMEM((NH, t), F32)],
        compiler_params=_cp("parallel", "parallel", "arbitrary"), name=name)(proj, proj, proj, crow, ccol, o, lse, do)


def _lru_gates(xc, wr_ref, wi_ref, vec_ref):
    xb = xc.astype(BF16)
    pre_r = jnp.concatenate([_dot(xb[:, p * BLK:(p + 1) * BLK], wr_ref[p], 1, 0) for p in range(LW // BLK)], axis=1)
    pre_i = jnp.concatenate([_dot(xb[:, p * BLK:(p + 1) * BLK], wi_ref[p], 1, 0) for p in range(LW // BLK)], axis=1)
    r = _sigmoid(pre_r + vec_ref[0:1, :])
    gi = _sigmoid(pre_i + vec_ref[1:2, :])
    log_a = LRU_C * r * _log_sigmoid(vec_ref[2:3, :])
    a = jnp.exp(log_a)
    mult = jnp.sqrt(_neg_expm1(2.0 * log_a))
    return r, gi, a, mult


def _conv(xbuf_ref, x, cw_ref, vec_ref, tr):
    return (cw_ref[3:4, :] * x + cw_ref[2:3, :] * xbuf_ref[7:7 + tr, :] + cw_ref[1:2, :] * xbuf_ref[6:6 + tr, :]
            + cw_ref[0:1, :] * xbuf_ref[5:5 + tr, :] + vec_ref[3:4, :])


def _lru_fwd(proj, cw, wr, wi, vec, name):
    T = proj.shape[0]
    tr = _rt(T)

    def body(x_ref, y_ref, cw_ref, wr_ref, wi_ref, vec_ref, oc_ref, hs_ref, xbuf, abuf, bbuf, hcar):
        i = pl.program_id(0)

        @pl.when(i == 0)
        def _():
            xbuf[0:8, :] = jnp.zeros((8, LW), F32)
            hcar[...] = jnp.zeros_like(hcar)

        x = x_ref[...]
        xbuf[8:8 + tr, :] = x
        xc = _conv(xbuf, x, cw_ref, vec_ref, tr)
        xbuf[0:8, :] = x[tr - 8:tr, :]
        _, gi, a, mult = _lru_gates(xc, wr_ref, wi_ref, vec_ref)
        rows = i * tr + lax.broadcasted_iota(jnp.int32, (tr, 1), 0)
        abuf[...] = a
        bbuf[...] = jnp.where(rows >= NPAD, mult * (gi * xc), 0.0)
        sub = lax.broadcasted_iota(jnp.int32, (8, 1), 0)

        def step(k, h):
            sl = pl.ds(pl.multiple_of(k * 8, 8), 8)
            a8, b8 = abuf[sl, :], bbuf[sl, :]
            for s in (1, 2, 4):
                ok = sub >= s
                b8 = jnp.where(ok, a8 * pltpu.roll(b8, s, 0) + b8, b8)
                a8 = jnp.where(ok, a8 * pltpu.roll(a8, s, 0), a8)
            h8 = a8 * h + b8
            bbuf[sl, :] = h8
            return h8[7:8, :]

        hcar[...] = lax.fori_loop(0, tr // 8, step, hcar[...])
        hs = bbuf[...]
        hs_ref[...] = hs
        oc_ref[...] = (hs * _gelu(y_ref[...])).astype(BF16)

    row = pl.BlockSpec((tr, LW), lambda i: (i, 0))
    full = lambda shape: pl.BlockSpec(shape, lambda i: (0,) * len(shape))
    return pl.pallas_call(
        body, grid=(T // tr,),
        in_specs=[pl.BlockSpec((tr, LW), lambda i: (i, XC // LW)), pl.BlockSpec((tr, LW), lambda i: (i, YC // LW)),
                  full((4, LW)), full((4, BLK, BLK)), full((4, BLK, BLK)), full((8, LW))],
        out_specs=[row, row], out_shape=[jax.ShapeDtypeStruct((T, LW), BF16), jax.ShapeDtypeStruct((T, LW), F32)],
        scratch_shapes=[pltpu.VMEM((tr + 8, LW), F32), pltpu.VMEM((tr, LW), F32), pltpu.VMEM((tr, LW), F32),
                        pltpu.VMEM((1, LW), F32)],
        compiler_params=_cp("arbitrary"), name=name)(proj, proj, cw, wr, wi, vec)


def _lru_bwd(proj, hs, doc, cw, wr, wi, vec, name):
    T = proj.shape[0]
    tr = _rt(T)
    nt = T // tr
    r8 = tr // 8

    def body(x_ref, xp_ref, y_ref, hs_ref, hp_ref, do_ref, cw_ref, wr_ref, wi_ref, vec_ref,
             dx_ref, dy_ref, dwr_ref, dwi_ref, dvec_ref, xbuf, abuf, gbuf, hbuf, dbuf, gcar, acar):
        k = pl.program_id(0)
        i = nt - 1 - k

        @pl.when(k == 0)
        def _():
            dwr_ref[...] = jnp.zeros_like(dwr_ref)
            dwi_ref[...] = jnp.zeros_like(dwi_ref)
            dvec_ref[...] = jnp.zeros_like(dvec_ref)
            gcar[...] = jnp.zeros_like(gcar)
            acar[...] = jnp.zeros_like(acar)
            dbuf[tr:tr + 8, :] = jnp.zeros((8, LW), F32)

        first = i == 0
        x = x_ref[...]
        xbuf[0:8, :] = jnp.where(first, 0.0, xp_ref[...])
        xbuf[8:8 + tr, :] = x
        xc = _conv(xbuf, x, cw_ref, vec_ref, tr)
        r, gi, a, mult = _lru_gates(xc, wr_ref, wi_ref, vec_ref)
        y = y_ref[...]
        hs = hs_ref[...]
        do_ = do_ref[...].astype(F32)
        rows = i * tr + lax.broadcasted_iota(jnp.int32, (tr, 1), 0)
        abuf[0:tr, :] = a
        abuf[tr:tr + 8, :] = jnp.zeros((8, LW), F32) + acar[...]
        an = abuf[1:1 + tr, :]
        acar[...] = a[0:1, :]
        abuf[0:tr, :] = an
        gbuf[...] = do_ * _gelu(y)
        sub = lax.broadcasted_iota(jnp.int32, (8, 1), 0)

        def step(kk, g):
            sl = pl.ds(pl.multiple_of((r8 - 1 - kk) * 8, 8), 8)
            a8, b8 = abuf[sl, :], gbuf[sl, :]
            for s in (1, 2, 4):
                ok = sub < 8 - s
                b8 = jnp.where(ok, a8 * pltpu.roll(b8, 8 - s, 0) + b8, b8)
                a8 = jnp.where(ok, a8 * pltpu.roll(a8, 8 - s, 0), a8)
            g8 = a8 * g + b8
            gbuf[sl, :] = g8
            return g8[0:1, :]

        gcar[...] = lax.fori_loop(0, r8, step, gcar[...])
        g = gbuf[...]
        hbuf[0:8, :] = jnp.where(first, 0.0, hp_ref[...])
        hbuf[8:8 + tr, :] = hs
        hprev = hbuf[7:7 + tr, :]
        dinp = jnp.where(rows >= NPAD, g, 0.0)
        da = g * hprev
        dmult = dinp * gi * xc
        dgi = dinp * mult * xc
        dxc = dinp * mult * gi
        dlog_a = da * a - dmult * a * a / mult
        ls = _log_sigmoid(vec_ref[2:3, :])
        dpre_r = dlog_a * (LRU_C * ls) * r * (1.0 - r)
        dpre_i = dgi * gi * (1.0 - gi)
        xb = xc.astype(BF16)
        rb, ib = dpre_r.astype(BF16), dpre_i.astype(BF16)
        back = []
        for p in range(LW // BLK):
            c = slice(p * BLK, (p + 1) * BLK)
            back.append(_dot(rb[:, c], wr_ref[p], 1, 1) + _dot(ib[:, c], wi_ref[p], 1, 1))
            dwr_ref[p] += _dot(xb[:, c], rb[:, c], 0, 0)
            dwi_ref[p] += _dot(xb[:, c], ib[:, c], 0, 0)
        dxc = dxc + jnp.concatenate(back, axis=1)
        col = lambda v: jnp.sum(v, axis=0, keepdims=True)
        dvec_ref[0:1, :] += col(dpre_r)
        dvec_ref[1:2, :] += col(dpre_i)
        dvec_ref[2:3, :] += col(dlog_a * (LRU_C * r)) * _sigmoid(-vec_ref[2:3, :])
        dvec_ref[3:4, :] += col(dxc)
        dvec_ref[4:5, :] += col(dxc * xbuf[5:5 + tr, :])
        dvec_ref[5:6, :] += col(dxc * xbuf[6:6 + tr, :])
        dvec_ref[6:7, :] += col(dxc * xbuf[7:7 + tr, :])
        dvec_ref[7:8, :] += col(dxc * x)
        dbuf[0:tr, :] = dxc
        dxr = (cw_ref[3:4, :] * dxc + cw_ref[2:3, :] * dbuf[1:1 + tr, :] + cw_ref[1:2, :] * dbuf[2:2 + tr, :]
               + cw_ref[0:1, :] * dbuf[3:3 + tr, :])
        dbuf[tr:tr + 8, :] = dxc[0:8, :]
        dx_ref[...] = jnp.where(rows >= NPAD, dxr, 0.0).astype(BF16)
        dy_ref[...] = (do_ * hs * _gelu_grad(y)).astype(BF16)

    rev = lambda k: nt - 1 - k
    row = lambda col0: pl.BlockSpec((tr, LW), lambda k: (rev(k), col0))
    prev8 = lambda col0: pl.BlockSpec((8, LW), lambda k: (jnp.maximum(rev(k) * r8 - 1, 0), col0))
    full = lambda shape: pl.BlockSpec(shape, lambda k: (0,) * len(shape))
    return pl.pallas_call(
        body, grid=(nt,),
        in_specs=[row(XC // LW), prev8(XC // LW), row(YC // LW), row(0), prev8(0), row(0),
                  full((4, LW)), full((4, BLK, BLK)), full((4, BLK, BLK)), full((8, LW))],
        out_specs=[row(0), row(0), full((4, BLK, BLK)), full((4, BLK, BLK)), full((8, LW))],
        out_shape=[jax.ShapeDtypeStruct((T, LW), BF16), jax.ShapeDtypeStruct((T, LW), BF16),
                   jax.ShapeDtypeStruct((4, BLK, BLK), F32), jax.ShapeDtypeStruct((4, BLK, BLK), F32),
                   jax.ShapeDtypeStruct((8, LW), F32)],
        scratch_shapes=[pltpu.VMEM((tr + 8, LW), F32), pltpu.VMEM((tr + 8, LW), F32), pltpu.VMEM((tr, LW), F32),
                        pltpu.VMEM((tr + 8, LW), F32), pltpu.VMEM((tr + 8, LW), F32),
                        pltpu.VMEM((1, LW), F32), pltpu.VMEM((1, LW), F32)],
        compiler_params=_cp("arbitrary"), name=name)(proj, proj, proj, hs, hs, doc, cw, wr, wi, vec)


def _merge_fwd(proj, b0, b1, b2, name):
    T = proj.shape[0]
    tr, tn = _rt(T), 512

    def body(g0, g1, g2, r0, r1, r2, o_ref):
        o_ref[...] = (_sigmoid(g0[...]) * r0[...] + _sigmoid(g1[...]) * r1[...] + _sigmoid(g2[...]) * r2[...]).astype(BF16)

    gate = lambda g: pl.BlockSpec((tr, tn), lambda i, j: (i, (GT + g * D) // tn + j))
    blk = pl.BlockSpec((tr, tn), lambda i, j: (i, j))
    return pl.pallas_call(
        body, grid=(T // tr, D // tn), in_specs=[gate(0), gate(1), gate(2), blk, blk, blk], out_specs=blk,
        out_shape=jax.ShapeDtypeStruct((T, D), BF16), compiler_params=_cp("parallel", "parallel"),
        name=name)(proj, proj, proj, b0, b1, b2)


def _merge_bwd(proj, b0, b1, b2, dm, name):
    T = proj.shape[0]
    tr, tn = _rt(T), 512

    def body(g0, g1, g2, r0, r1, r2, dm_ref, d0, d1, d2, e0, e1, e2):
        dmv = dm_ref[...]
        for g_ref, r_ref, d_ref, e_ref in ((g0, r0, d0, e0), (g1, r1, d1, e1), (g2, r2, d2, e2)):
            sg = _sigmoid(g_ref[...])
            d_ref[...] = (dmv * sg).astype(BF16)
            e_ref[...] = (dmv * r_ref[...] * sg * (1.0 - sg)).astype(BF16)

    gate = lambda g: pl.BlockSpec((tr, tn), lambda i, j: (i, (GT + g * D) // tn + j))
    blk = pl.BlockSpec((tr, tn), lambda i, j: (i, j))
    return pl.pallas_call(
        body, grid=(T // tr, D // tn), in_specs=[gate(0), gate(1), gate(2), blk, blk, blk, blk], out_specs=[blk] * 6,
        out_shape=[jax.ShapeDtypeStruct((T, D), BF16)] * 6, compiler_params=_cp("parallel", "parallel"),
        name=name)(proj, proj, proj, b0, b1, b2, dm)


def _swiglu_fwd(ff, name):
    T = ff.shape[0]
    tr, tn = _rt(T), 256
    nj = DFF // tn

    def body(g_ref, u_ref, o_ref):
        g = g_ref[...]
        o_ref[...] = (g * _sigmoid(g) * u_ref[...]).astype(BF16)

    return pl.pallas_call(
        body, grid=(T // tr, nj),
        in_specs=[pl.BlockSpec((tr, tn), lambda i, j: (i, j)), pl.BlockSpec((tr, tn), lambda i, j: (i, j + nj))],
        out_specs=pl.BlockSpec((tr, tn), lambda i, j: (i, j)), out_shape=jax.ShapeDtypeStruct((T, DFF), BF16),
        compiler_params=_cp("parallel", "parallel"), name=name)(ff, ff)


def _swiglu_bwd(ff, dact, name):
    T = ff.shape[0]
    tr, tn = _rt(T), 256
    nj = DFF // tn

    def body(g_ref, u_ref, d_ref, o_ref):
        g, d = g_ref[...], d_ref[...]
        sg = _sigmoid(g)
        dgate = d * u_ref[...] * (sg + g * sg * (1.0 - sg))
        dup = d * g * sg
        o_ref[...] = jnp.where(pl.program_id(1) < nj, dgate, dup).astype(BF16)

    half = lambda j: j % nj
    return pl.pallas_call(
        body, grid=(T // tr, 2 * nj),
        in_specs=[pl.BlockSpec((tr, tn), lambda i, j: (i, half(j))), pl.BlockSpec((tr, tn), lambda i, j: (i, half(j) + nj)),
                  pl.BlockSpec((tr, tn), lambda i, j: (i, half(j)))],
        out_specs=pl.BlockSpec((tr, tn), lambda i, j: (i, j)), out_shape=jax.ShapeDtypeStruct((T, 2 * DFF), BF16),
        compiler_params=_cp("parallel", "parallel"), name=name)(ff, ff, dact)


def _adamw(w, g, m, v, name):
    R, C = w.shape
    tr = _pick(R, tuple(t for t in (512, 256, 128, 64, 32, 16, 8) if t * C * 4 <= (3 << 19)))
    c1 = 1.0 - ADAM_B1 ** ADAM_STEP
    c2 = 1.0 - ADAM_B2 ** ADAM_STEP

    def body(w_ref, g_ref, m_ref, v_ref, d_ref, mo_ref, vo_ref):
        gv = g_ref[...]
        mn = ADAM_B1 * m_ref[...] + (1.0 - ADAM_B1) * gv
        vn = ADAM_B2 * v_ref[...] + (1.0 - ADAM_B2) * (gv * gv)
        d_ref[...] = -ADAM_LR * ((mn / c1) / (jnp.sqrt(vn / c2) + ADAM_EPS) + ADAM_WD * w_ref[...])
        mo_ref[...] = mn
        vo_ref[...] = vn

    blk = pl.BlockSpec((tr, C), lambda i: (i, 0))
    return pl.pallas_call(
        body, grid=(R // tr,), in_specs=[blk] * 4, out_specs=[blk] * 3,
        out_shape=[jax.ShapeDtypeStruct((R, C), F32)] * 3, compiler_params=_cp("parallel"), name=name)(w, g, m, v)


def _sum_lead(x, name):
    n, R, C = x.shape
    tr = _pick(R, (512, 256, 128, 64, 32, 16, 8))

    def body(x_ref, o_ref):
        acc = x_ref[0]
        for d in range(1, n):
            acc = acc + x_ref[d]
        o_ref[...] = acc

    return pl.pallas_call(
        body, grid=(R // tr,), in_specs=[pl.BlockSpec((n, tr, C), lambda i: (0, i, 0))],
        out_specs=pl.BlockSpec((tr, C), lambda i: (i, 0)), out_shape=jax.ShapeDtypeStruct((R, C), F32),
        compiler_params=_cp("parallel"), name=name)(x)


def _here():
    return lax.axis_index("x"), lax.axis_index("y"), lax.axis_index("c")


def _rcopy(src, dst, send_sems, recv_sems, k, to):
    return pltpu.make_async_remote_copy(src_ref=src, dst_ref=dst, send_sem=send_sems.at[k], recv_sem=recv_sems.at[k],
                                        device_id=to, device_id_type=MESH)


def _window(ref, lead, axis, start, width):
    idx = [lead] + [slice(None)] * (len(ref.shape) - 1)
    if axis is not None:
        idx[axis] = pl.ds(start, width)
    return ref.at[tuple(idx)]


def _hbm_calls(body, args, out_shapes, n_sems, aliases, name):
    return pl.pallas_call(
        body, in_specs=[ANY] * len(args), out_specs=[ANY] * len(out_shapes), out_shape=out_shapes,
        input_output_aliases=aliases,
        scratch_shapes=[pltpu.SemaphoreType.DMA((n_sems,)), pltpu.SemaphoreType.DMA((n_sems,))],
        compiler_params=pltpu.CompilerParams(has_side_effects=True), name=name)(*args)


def _all_gather_weights(fulls, axes, name):
    nt = len(fulls)

    def body(*refs):
        outs, (send_sems, recv_sems) = refs[nt:2 * nt], refs[2 * nt:]
        x, y, c = _here()
        sib = (x, y, 1 - c)
        chips = [(1 - x, y), (x, 1 - y), (1 - x, 1 - y)]

        def win(t, chip, hc):
            w = outs[t].shape[axes[t]] // N_SHARD
            return _window(outs[t], pl.ds(2 * hc, 2), axes[t], pl.multiple_of((2 * chip[0] + chip[1]) * w, 8), w)

        def copy(t, k, chip, hc, to):
            return _rcopy(win(t, chip, hc), win(t, chip, hc), send_sems, recv_sems, 6 * t + k, to)

        sends = [copy(t, k, (x, y), c, (*chip, c)) for t in range(nt) for k, chip in enumerate(chips)]
        for cp in sends:
            cp.start()
        for t in range(nt):
            for k, chip in enumerate(chips):
                copy(t, k, chip, c, (*chip, c)).wait_recv()
                fwd = copy(t, 3 + k, chip, c, sib)
                fwd.start()
                sends.append(fwd)
        for t in range(nt):
            for k, chip in enumerate(chips):
                copy(t, 3 + k, chip, 1 - c, sib).wait_recv()
        for cp in sends:
            cp.wait_send()

    return _hbm_calls(body, fulls, [jax.ShapeDtypeStruct(f.shape, f.dtype) for f in fulls], 6 * nt,
                      {t: t for t in range(nt)}, name)


def _swap_halves(gs, name):
    nt = len(gs)

    def body(*refs):
        ins, outs, (send_sems, recv_sems) = refs[:nt], refs[nt:2 * nt], refs[2 * nt:]
        x, y, c = _here()
        cps = [_rcopy(g.at[pl.ds(2 * (1 - c), 2)], o, send_sems, recv_sems, t, (x, y, 1 - c))
               for t, (g, o) in enumerate(zip(ins, outs))]
        for cp in cps:
            cp.start()
        for cp in cps:
            cp.wait()

    return _hbm_calls(body, gs, [jax.ShapeDtypeStruct((2,) + g.shape[1:], g.dtype) for g in gs], nt, {}, name)


def _scatter_to_chips(ss, axes, name):
    nt = len(ss)

    def shard_shape(s, ax):
        shp = list(s.shape)
        shp[ax] //= N_SHARD
        return tuple(shp)

    def body(*refs):
        ins, outs, (send_sems, recv_sems) = refs[:nt], refs[nt:2 * nt], refs[2 * nt:]
        x, y, c = _here()
        chips = [(1 - x, y), (x, 1 - y), (1 - x, 1 - y)]
        cps = []
        for t, (s, o, ax) in enumerate(zip(ins, outs, axes)):
            w = s.shape[ax] // N_SHARD
            for k, chip in enumerate(chips):
                src = _window(s, slice(None), ax, pl.multiple_of((2 * chip[0] + chip[1]) * w, 8), w)
                cps.append(_rcopy(src, o.at[k], send_sems, recv_sems, 3 * t + k, (*chip, c)))
        for cp in cps:
            cp.start()
        for cp in cps:
            cp.wait()

    return _hbm_calls(body, ss, [jax.ShapeDtypeStruct((3,) + shard_shape(s, ax), s.dtype) for s, ax in zip(ss, axes)],
                      3 * nt, {}, name)


def _join_halves(fins, name):
    nt = len(fins)

    def body(*refs):
        outs, (send_sems, recv_sems) = refs[nt:2 * nt], refs[2 * nt:]
        x, y, c = _here()
        cps = [_rcopy(o.at[pl.ds(2 * c, 2)], o.at[pl.ds(2 * c, 2)], send_sems, recv_sems, t, (x, y, 1 - c))
               for t, o in enumerate(outs)]
        for cp in cps:
            cp.start()
        for t, o in enumerate(outs):
            _rcopy(o.at[pl.ds(2 * (1 - c), 2)], o.at[pl.ds(2 * (1 - c), 2)], send_sems, recv_sems, t, (x, y, 1 - c)).wait_recv()
        for cp in cps:
            cp.wait_send()

    return _hbm_calls(body, fins, [jax.ShapeDtypeStruct(f.shape, f.dtype) for f in fins], nt, {t: t for t in range(nt)}, name)


def _all_gather_small(buf, name):
    def body(_, out_ref, send_sems, recv_sems):
        x, y, c = _here()
        me = 4 * x + 2 * y + c
        cps = []
        for k in range(1, 8):
            to = (x ^ ((k >> 2) & 1), y ^ ((k >> 1) & 1), c ^ (k & 1))
            peer = 4 * to[0] + 2 * to[1] + to[2]
            cps.append((_rcopy(out_ref.at[me], out_ref.at[me], send_sems, recv_sems, k - 1, to),
                        _rcopy(out_ref.at[peer], out_ref.at[peer], send_sems, recv_sems, k - 1, to)))
        for snd, _ in cps:
            snd.start()
        for _, rcv in cps:
            rcv.wait_recv()
        for snd, _ in cps:
            snd.wait_send()

    return _hbm_calls(body, [buf], [jax.ShapeDtypeStruct(buf.shape, buf.dtype)], 7, {0: 0}, name)[0]


def _place(block, n, index):
    buf = jnp.zeros((n,) + block.shape[1:], block.dtype)
    return lax.dynamic_update_slice_in_dim(buf, block, index, axis=0)


def _add_half(g, other, cidx, name):
    _, R, C = g.shape
    tr = _pick(R, tuple(t for t in (512, 256, 128, 64, 32, 16, 8) if t * C * 4 <= (1 << 21)))

    def body(c_ref, g_ref, o_ref, s_ref, sb_ref):
        s = g_ref[...] + o_ref[...]
        s_ref[...] = s
        sb_ref[...] = s.astype(BF16)

    blk = pl.BlockSpec((None, tr, C), lambda l, i, c: (l, i, 0))
    return pl.pallas_call(
        body,
        grid_spec=pltpu.PrefetchScalarGridSpec(
            num_scalar_prefetch=1, grid=(2, R // tr),
            in_specs=[pl.BlockSpec((None, tr, C), lambda l, i, c: (2 * c[0] + l, i, 0)), blk], out_specs=[blk, blk]),
        out_shape=[jax.ShapeDtypeStruct((2, R, C), F32), jax.ShapeDtypeStruct((2, R, C), BF16)],
        compiler_params=_cp("parallel", "parallel"), name=name)(cidx, g, other)


def _add_chips(s, recv, axis, chip_idx, name):
    _, _, r, cw = recv.shape
    tr = _pick(r, tuple(t for t in (704, 512, 256, 128, 64, 32, 16, 8) if t * cw * 4 <= (1 << 21)))
    nr = r // tr

    def body(c_ref, s_ref, r_ref, out_ref):
        out_ref[...] = ((s_ref[...] + r_ref[0].astype(F32)) + r_ref[1].astype(F32)) + r_ref[2].astype(F32)

    if axis == 2:
        s_map = lambda l, i, c: (l, i, c[0])
    else:
        s_map = lambda l, i, c: (l, c[0] * nr + i, 0)
    return pl.pallas_call(
        body,
        grid_spec=pltpu.PrefetchScalarGridSpec(
            num_scalar_prefetch=1, grid=(2, nr),
            in_specs=[pl.BlockSpec((None, tr, cw), s_map), pl.BlockSpec((3, None, tr, cw), lambda l, i, c: (0, l, i, 0))],
            out_specs=pl.BlockSpec((None, tr, cw), lambda l, i, c: (l, i, 0))),
        out_shape=jax.ShapeDtypeStruct((2, r, cw), F32), compiler_params=_cp("parallel", "parallel"), name=name)(chip_idx, s, recv)


IN_SHARD = IN_COLS // N_SHARD
IN_SLOT = INP // N_SHARD
IN_PIECES = ((0, 512, QA), (512, 640, KA), (640, 768, VA), (768, 1280, QF), (1280, 1792, KF), (1792, 2304, VF),
             (2304, 2312, FL), (2312, 2824, XC), (2824, 3336, YC), (3336, 6408, GT))


def _gathered_to_kernel_cols(w):
    parts, pos = [], 0
    for a, b, k in sorted(IN_PIECES, key=lambda p: p[2]):
        assert k == pos
        while a < b:
            j = a // IN_SHARD
            e = min(b, (j + 1) * IN_SHARD)
            g = j * IN_SLOT + a - j * IN_SHARD
            parts.append(w[..., g:g + e - a])
            pos += e - a
            a = e
    parts.append(jnp.zeros(w.shape[:-1] + (INP - pos,), w.dtype))
    return jnp.concatenate(parts, axis=-1)


def _kernel_to_gathered_cols(w):
    parts = []
    for j in range(N_SHARD):
        lo, hi = j * IN_SHARD, (j + 1) * IN_SHARD
        for a, b, k in IN_PIECES:
            s, e = max(a, lo), min(b, hi)
            if s < e:
                parts.append(w[..., k + s - a:k + e - a])
        parts.append(jnp.zeros(w.shape[:-1] + (IN_SLOT - IN_SHARD,), w.dtype))
    return jnp.concatenate(parts, axis=-1)


def _pair_blocks(w):
    z = jnp.zeros((4, 64, 64), w.dtype)
    w = w.reshape(4, 2, 64, 64)
    top = jnp.concatenate([w[:, 0], z], axis=2)
    bot = jnp.concatenate([z, w[:, 1]], axis=2)
    return jnp.concatenate([top, bot], axis=1)


def _unpair_blocks(w):
    return jnp.stack([w[:, :64, :64], w[:, 64:, 64:]], axis=1).reshape(8, 64, 64)


BIG = ("w_in", "w_branch", "w_out", "w_ffn_in", "w_ffn_out")
TINY = ("conv_w", "meta_tokens")
SMALL = ("rel_bias_table", "norm_mix", "swa_sinks", "fox_forget_bias", "conv_b", "lru_w_r", "lru_b_r", "lru_w_i",
         "lru_b_i", "lru_lambda", "norm_ffn", "norm_final")
SHARD_AXIS = {"conv_w": 2, "meta_tokens": 1}
BIG_AXIS = {"w_in": 2, "w_branch": 2, "w_out": 1, "w_ffn_in": 2, "w_ffn_out": 1}


def _pack(d, names):
    flat = jnp.concatenate([d[n].reshape(-1) for n in names])
    pad = (-flat.shape[0]) % (256 * 128)
    return jnp.concatenate([flat, jnp.zeros((pad,), F32)]).reshape(-1, 128)


def _unpack(buf, names, shapes):
    flat, out, off = buf.reshape(-1), {}, 0
    for n in names:
        sz = int(np.prod(shapes[n]))
        out[n] = flat[off:off + sz].reshape(shapes[n])
        off += sz
    return out


def _local_step(x, tgt, W):
    S = x.shape[0]
    T = S + BLK
    tm = _rt(T)
    bucket = jnp.asarray(_bucket_table())
    bias = _bias_build(W["rel_bias_table"], bucket, "bias_build")
    h = jnp.concatenate([jnp.zeros((NPAD, D), F32), W["meta_tokens"], x], axis=0)

    saved = []
    for l in range(DEPTH):
        sv = {"h0": h}
        u = _rms_fwd(h, W["norm_mix"][l], f"rms_mix_fwd")
        proj = _mm(u, W["w_in"][l], tm=tm, tn=512, tk=D, name="mm_in_fwd")
        oa = _swa_fwd(proj, bias, W["swa_sinks"][l], "swa_fwd")
        z3 = _to_blocks(proj[:, FL:FL + NH].T)
        fb = W["fox_forget_bias"][l].reshape(NH, 1)
        crow = _from_blocks(_cum_fwd(z3, fb, "cum_fwd"))
        ccol = jnp.pad(crow.T, ((0, 0), (0, BLK - NH)))
        of, lse = _fox_fwd(proj, crow, ccol, "fox_fwd")
        lru_vec = jnp.concatenate([W["lru_b_r"][l][None], W["lru_b_i"][l][None], W["lru_lambda"][l][None],
                                   W["conv_b"][l][None], jnp.zeros((4, LW), F32)], axis=0)
        oc, hs = _lru_fwd(proj, W["conv_w"][l], W["lru_w_r"][l], W["lru_w_i"][l], lru_vec, "lru_fwd")
        bs = [_mm(o, W["w_branch"][l, g], tm=tm, tn=512, tk=LW, name="mm_branch_fwd") for g, o in enumerate((oa, of, oc))]
        merged = _merge_fwd(proj, *bs, "merge_fwd")
        h2 = _mm(merged, W["w_out"][l], res=h, tm=tm, tn=512, tk=D, name="mm_out_fwd")
        u2 = _rms_fwd(h2, W["norm_ffn"][l], "rms_ffn_fwd")
        ff = _mm(u2, W["w_ffn_in"][l], tm=tm, tn=512, tk=D, name="mm_ffn_in_fwd")
        act = _swiglu_fwd(ff, "swiglu_fwd")
        h = _mm(act, W["w_ffn_out"][l], res=h2, tm=tm, tn=512, tk=_pick(DFF, (1408, 256)), name="mm_ffn_out_fwd")
        sv.update(u=u, proj=proj, oa=oa, of=of, oc=oc, lse=lse, hs=hs, z3=z3, fb=fb, crow=crow, ccol=ccol, lru_vec=lru_vec,
                  bs=bs, merged=merged, h2=h2, u2=u2, ff=ff, act=act)
        saved.append(sv)

    tgt_pad = tgt
    dh, dhb, dg_final, loss_vec = _loss_head(h, tgt_pad, W["norm_final"], "loss_head")
    loss = loss_vec[0, 0]

    small = ("norm_mix", "swa_sinks", "fox_forget_bias", "conv_w", "conv_b", "lru_w_r", "lru_b_r", "lru_w_i", "lru_b_i",
             "lru_lambda", "norm_ffn")
    G = {n: [None] * DEPTH for n in small}
    G["norm_final"] = dg_final.reshape(D)
    GW = {n: None for n in BIG}
    dbias = jnp.zeros((NH, BLK, 2 * BLK), F32)
    tkT = _rt(T)
    for l in reversed(range(DEPTH)):
        sv = saved[l]
        GW["w_ffn_out"] = _mm(_transpose(sv["act"], "tr_act"), dhb, tm=_pick(DFF, (1408, 256)), tn=D, tk=tkT,
                              slab=(GW["w_ffn_out"], l, DEPTH), name="mm_ffn_out_dw")
        dact = _mm(dhb, W["w_ffn_out"][l], tb=True, tm=tm, tn=_pick(DFF, (1408, 256)), tk=D, name="mm_ffn_out_dx")
        dff = _swiglu_bwd(sv["ff"], dact, "swiglu_bwd")
        GW["w_ffn_in"] = _mm(_transpose(sv["u2"], "tr_u2"), dff, tm=D, tn=1408, tk=tkT,
                             slab=(GW["w_ffn_in"], l, DEPTH), name="mm_ffn_in_dw")
        du2 = _mm(dff, W["w_ffn_in"][l], tb=True, tm=tm, tn=512, tk=_pick(2 * DFF, (1408, 512)), name="mm_ffn_in_dx")
        dh, dhb, dgn = _rms_bwd(du2, sv["h2"], W["norm_ffn"][l], dh, "rms_ffn_bwd")
        G["norm_ffn"][l] = dgn.reshape(D)
        GW["w_out"] = _mm(_transpose(sv["merged"], "tr_merged"), dhb, tm=D, tn=D, tk=tkT,
                          slab=(GW["w_out"], l, DEPTH), name="mm_out_dw")
        dm = _mm(dhb, W["w_out"][l], tb=True, tm=tm, tn=512, tk=D, name="mm_out_dx")
        db0, db1, db2, dg0, dg1, dg2 = _merge_bwd(sv["proj"], *sv["bs"], dm, "merge_bwd")
        dos = []
        for g, (o, db) in enumerate(zip((sv["oa"], sv["of"], sv["oc"]), (db0, db1, db2))):
            GW["w_branch"] = _mm(_transpose(o, "tr_branch"), db, tm=LW, tn=D, tk=tkT,
                                 slab=(GW["w_branch"], 3 * l + g, 3 * DEPTH), name="mm_branch_dw")
            dos.append(_mm(db, W["w_branch"][l, g], tb=True, out_dtype=BF16, tm=tm, tn=LW, tk=D, name="mm_branch_dx"))
        dqa, dkb, dvb, dbias, dsk = _swa_bwd(sv["proj"], bias, W["swa_sinks"][l], dos[0], dbias, "swa_bwd")
        dka, dva = _band_fold(dkb, dvb, "swa_band_fold")
        G["swa_sinks"][l] = dsk[0, :NH]
        dqf, dcq = _fox_dq(sv["proj"], sv["crow"], sv["ccol"], sv["of"], sv["lse"], dos[1], "fox_dq")
        dkf, dvf, dcp = _fox_dkv(sv["proj"], sv["crow"], sv["ccol"], sv["of"], sv["lse"], dos[1], "fox_dkv")
        dck3 = _to_blocks(dcp[:, :2, :].reshape(NH, T))
        dcq3 = _to_blocks(dcq[:, ::64].T)
        dz3, dfb = _cum_bwd(dck3, dcq3, sv["z3"], sv["fb"], "cum_bwd")
        G["fox_forget_bias"][l] = dfb[:, 0]
        dfl = jnp.pad(_from_blocks(dz3).T, ((0, 0), (0, INP - FL - NH))).astype(BF16)
        dxc, dyc, dwr, dwi, dvec = _lru_bwd(sv["proj"], sv["hs"], dos[2], W["conv_w"][l], W["lru_w_r"][l], W["lru_w_i"][l],
                                            sv["lru_vec"], "lru_bwd")
        G["lru_w_r"][l], G["lru_w_i"][l] = _unpair_blocks(dwr), _unpair_blocks(dwi)
        G["lru_b_r"][l], G["lru_b_i"][l], G["lru_lambda"][l], G["conv_b"][l] = dvec[0], dvec[1], dvec[2], dvec[3]
        G["conv_w"][l] = dvec[4:8]
        dproj = jnp.concatenate([dqa, dqf, dkf, dvf, dxc, dyc, dg0, dg1, dg2, dka, dva, dfl], axis=1)
        GW["w_in"] = _mm(_transpose(sv["u"], "tr_u"), dproj, tm=D, tn=IN_SLOT, tk=tkT,
                         slab=(GW["w_in"], l, DEPTH), name="mm_in_dw")
        du = _mm(dproj, W["w_in"][l], tb=True, tm=tm, tn=512, tk=_pick(INP, (1664, 512)), name="mm_in_dx")
        dh, dhb, dgn = _rms_bwd(du, sv["h0"], W["norm_mix"][l], dh, "rms_mix_bwd")
        G["norm_mix"][l] = dgn.reshape(D)

    grads = {n: (jnp.stack(v) if isinstance(v, list) else v) for n, v in G.items()}
    grads.update(GW)
    grads["w_branch"] = GW["w_branch"].reshape(DEPTH, 3 * LW, D)
    grads["rel_bias_table"] = _bias_bwd(dbias, bucket, "bias_bwd")
    grads["meta_tokens"] = dh[NPAD:BLK]
    return loss, dh[BLK:], grads


NAMES = ("meta_tokens", "rel_bias_table", "norm_mix", "w_in", "swa_sinks", "fox_forget_bias", "conv_w", "conv_b",
         "lru_w_r", "lru_b_r", "lru_w_i", "lru_b_i", "lru_lambda", "w_branch", "w_out", "norm_ffn", "w_ffn_in",
         "w_ffn_out", "norm_final")


def _three_d(n, a):
    return a.reshape(DEPTH, 3 * LW, -1) if n == "w_branch" else a


def _gather_weights(P):
    x, y, c = _here()
    mine, me = 2 * x + y, 4 * x + 2 * y + c
    fulls = []
    for n in BIG:
        shard = _three_d(n, P[n].astype(BF16))
        if n == "w_in":
            shard = jnp.pad(shard, ((0, 0), (0, 0), (0, IN_SLOT - IN_SHARD)))
        ax = BIG_AXIS[n]
        shape = list(shard.shape)
        shape[ax] *= N_SHARD
        fulls.append(lax.dynamic_update_slice_in_dim(jnp.zeros(shape, BF16), shard, mine * shard.shape[ax], axis=ax))
    full = dict(zip(BIG, _all_gather_weights(fulls, [BIG_AXIS[n] for n in BIG], "ag_weights")))
    full["w_in"] = _gathered_to_kernel_cols(full["w_in"])
    full["w_branch"] = full["w_branch"].reshape(DEPTH, 3, LW, D)
    tiny = _all_gather_small(_place(_pack(P, TINY)[None], 8, me), "ag_tiny_weights")
    parts = [_unpack(tiny[2 * j], TINY, {n: P[n].shape for n in TINY}) for j in range(N_SHARD)]
    for n in TINY:
        full[n] = jnp.concatenate([p[n] for p in parts], axis=SHARD_AXIS[n])
    for n in SMALL:
        full[n] = P[n]
    full["lru_w_r"] = jnp.stack([_pair_blocks(P["lru_w_r"][l]) for l in range(DEPTH)]).astype(BF16)
    full["lru_w_i"] = jnp.stack([_pair_blocks(P["lru_w_i"][l]) for l in range(DEPTH)]).astype(BF16)
    return full


def _reduce_grads(grads, P):
    x, y, c = _here()
    mine, me = 2 * x + y, 4 * x + 2 * y + c
    cidx = jnp.reshape(c, (1,)).astype(jnp.int32)
    chip = jnp.reshape(mine, (1,)).astype(jnp.int32)
    axes = [BIG_AXIS[n] for n in BIG]
    gs = [grads[n] for n in BIG]
    pairs = [_add_half(g, r, cidx, "rs_add_half_" + n) for n, g, r in zip(BIG, gs, _swap_halves(gs, "rs_swap_halves"))]
    ss, sbs = [list(t) for t in zip(*pairs)]
    ss[0], sbs[0] = _kernel_to_gathered_cols(ss[0]), _kernel_to_gathered_cols(sbs[0])
    recv = _scatter_to_chips(sbs, axes, "rs_scatter")
    tots = [_add_chips(s, r, ax, chip, "rs_add_chips_" + n) for n, s, r, ax in zip(BIG, ss, recv, axes)]
    fins = dict(zip(BIG, _join_halves([_place(t, DEPTH, 2 * c) for t in tots], "rs_join_halves")))
    out = {n: fins[n].reshape(P[n].shape) for n in BIG if n != "w_in"}
    out["w_in"] = fins["w_in"][:, :, :IN_SHARD]
    names = SMALL + TINY
    gathered = _all_gather_small(_place(_pack(grads, names)[None], 8, me), "ag_small_grads")
    small = _unpack(_sum_lead(gathered, "sum_small_grads"), names, {n: grads[n].shape for n in names})
    for n in SMALL:
        out[n] = small[n]
    for n in TINY:
        w = P[n].shape[SHARD_AXIS[n]]
        out[n] = lax.dynamic_slice_in_dim(small[n], mine * w, w, axis=SHARD_AXIS[n])
    return out


def _update(P, Gd, M, V):
    delta, new_m, new_v = {}, {}, {}
    for n in BIG + TINY:
        shp = P[n].shape
        two = (int(np.prod(shp[:-1])), shp[-1])
        d, m, v = _adamw(P[n].reshape(two), Gd[n].reshape(two), M[n].reshape(two), V[n].reshape(two), "adamw_" + n)
        delta[n], new_m[n], new_v[n] = d.reshape(shp), m.reshape(shp), v.reshape(shp)
    shapes = {n: P[n].shape for n in SMALL}
    d, m, v = _adamw(_pack(P, SMALL), _pack(Gd, SMALL), _pack(M, SMALL), _pack(V, SMALL), "adamw_small")
    for dst, buf in ((delta, d), (new_m, m), (new_v, v)):
        dst.update(_unpack(buf, SMALL, shapes))
    return delta, new_m, new_v


def kernel(x, meta_tokens, rel_bias_table, norm_mix, w_in, swa_sinks, fox_forget_bias, conv_w, conv_b, lru_w_r, lru_b_r, lru_w_i, lru_b_i, lru_lambda, w_branch, w_out, norm_ffn, w_ffn_in, w_ffn_out, norm_final, loss_target, m_meta_tokens, m_rel_bias_table, m_norm_mix, m_w_in, m_swa_sinks, m_fox_forget_bias, m_conv_w, m_conv_b, m_lru_w_r, m_lru_b_r, m_lru_w_i, m_lru_b_i, m_lru_lambda, m_w_branch, m_w_out, m_norm_ffn, m_w_ffn_in, m_w_ffn_out, m_norm_final, v_meta_tokens, v_rel_bias_table, v_norm_mix, v_w_in, v_swa_sinks, v_fox_forget_bias, v_conv_w, v_conv_b, v_lru_w_r, v_lru_b_r, v_lru_w_i, v_lru_b_i, v_lru_lambda, v_w_branch, v_w_out, v_norm_ffn, v_w_ffn_in, v_w_ffn_out, v_norm_final):
    P = dict(zip(NAMES, (meta_tokens, rel_bias_table, norm_mix, w_in, swa_sinks, fox_forget_bias, conv_w, conv_b, lru_w_r,
                         lru_b_r, lru_w_i, lru_b_i, lru_lambda, w_branch, w_out, norm_ffn, w_ffn_in, w_ffn_out, norm_final)))
    M = dict(zip(NAMES, (m_meta_tokens, m_rel_bias_table, m_norm_mix, m_w_in, m_swa_sinks, m_fox_forget_bias, m_conv_w,
                         m_conv_b, m_lru_w_r, m_lru_b_r, m_lru_w_i, m_lru_b_i, m_lru_lambda, m_w_branch, m_w_out, m_norm_ffn,
                         m_w_ffn_in, m_w_ffn_out, m_norm_final)))
    V = dict(zip(NAMES, (v_meta_tokens, v_rel_bias_table, v_norm_mix, v_w_in, v_swa_sinks, v_fox_forget_bias, v_conv_w,
                         v_conv_b, v_lru_w_r, v_lru_b_r, v_lru_w_i, v_lru_b_i, v_lru_lambda, v_w_branch, v_w_out, v_norm_ffn,
                         v_w_ffn_in, v_w_ffn_out, v_norm_final)))
    W = _gather_weights(P)
    loss_local, grad_x, grads = _local_step(x[0], loss_target[0], W)
    loss = lax.psum(loss_local, ("x", "y", "c"))
    Gd = _reduce_grads(grads, P)
    delta, new_m, new_v = _update(P, Gd, M, V)
    return (loss, grad_x[None], *[Gd[n] for n in NAMES], *[delta[n] for n in NAMES],
            *[new_m[n] for n in NAMES], *[new_v[n] for n in NAMES])
```

```python
import functools
import math

import numpy as np
import jax
import jax.numpy as jnp
from jax import lax
from jax.experimental import pallas as pl
from jax.experimental.pallas import tpu as pltpu

F32, BF16 = jnp.float32, jnp.bfloat16
MESH = pl.DeviceIdType.MESH
ANY = pl.BlockSpec(memory_space=pl.ANY)
SMEM = pl.BlockSpec(memory_space=pltpu.SMEM)

D = 1024
DEPTH = 4
BLK = 128
N_META = 16
NPAD = 112
NH = 8
LW = 512
DFF = 2816
EPS = 1e-6
NEG = -1e30
SCALE = 0.125
LRU_C = 8.0
REL_BUCKETS = 32
N_SHARD = 4
QA, QF, KF, VF, XC, YC, GT, KA, VA, FL, INP = 0, 512, 1024, 1536, 2048, 2560, 3072, 6144, 6272, 6400, 6656
IN_COLS = 6408
VMEM_LIMIT = 48 * 1024 * 1024

ADAM_LR, ADAM_B1, ADAM_B2, ADAM_EPS, ADAM_WD, ADAM_STEP = 0.001, 0.9, 0.999, 1e-08, 0.01, 10


def _cp(*sem):
    return pltpu.CompilerParams(dimension_semantics=sem or None, vmem_limit_bytes=VMEM_LIMIT)


def _pick(n, prefs):
    for p in prefs:
        if n % p == 0:
            return p
    return n


def _rt(T):
    return _pick(T, (384, 128))


def _sigmoid(z):
    return 1.0 / (1.0 + jnp.exp(-z))


def _log_sigmoid(z):
    return jnp.minimum(z, 0.0) - jnp.log(1.0 + jnp.exp(-jnp.abs(z)))


def _gelu(y):
    c = math.sqrt(2.0 / math.pi)
    return 0.5 * y * (1.0 + jnp.tanh(c * (y + 0.044715 * y * y * y)))


def _gelu_grad(y):
    c = math.sqrt(2.0 / math.pi)
    t = jnp.tanh(c * (y + 0.044715 * y * y * y))
    return 0.5 * (1.0 + t) + 0.5 * y * (1.0 - t * t) * c * (1.0 + 3.0 * 0.044715 * y * y)


def _neg_expm1(z):
    series = -z * (1.0 + z * (0.5 + z * (1.0 / 6.0 + z * (1.0 / 24.0 + z * (1.0 / 120.0)))))
    return jnp.where(z > -0.1, series, 1.0 - jnp.exp(z))


def _dot(a, b, ca, cb):
    return lax.dot_general(a, b, (((ca,), (cb,)), ((), ())), preferred_element_type=F32)


def _mm(a, b, *, ta=False, tb=False, res=None, out_dtype=F32, tm, tn, tk, name, slab=None, b_k0=0, col0=0, cols=None):
    M, K = (a.shape[1], a.shape[0]) if ta else a.shape
    N = b.shape[0] if tb else b.shape[1]
    assert (b.shape[1] if tb else b.shape[0]) >= K + b_k0 and M % tm == 0 and N % tn == 0 and K % tk == 0, (name, a.shape, b.shape)
    assert b_k0 % tk == 0 and col0 % tn == 0
    nk, kb, jb = K // tk, b_k0 // tk, col0 // tn
    ca, cb = (0 if ta else 1), (1 if tb else 0)
    n_in = 2 + (res is not None) + (slab is not None and slab[0] is not None)

    def body(*refs):
        a_ref, b_ref = refs[:2]
        r_ref = refs[2] if res is not None else None
        o_ref = refs[n_in]
        part = _dot(a_ref[...].astype(BF16), b_ref[...].astype(BF16), ca, cb)

        def fin(acc):
            if res is not None:
                acc = acc + r_ref[...]
            o_ref[...] = acc.astype(out_dtype)

        if nk == 1:
            fin(part)
        else:
            acc_ref = refs[-1]
            k = pl.program_id(2)

            @pl.when(k == 0)
            def _():
                acc_ref[...] = part

            @pl.when(k > 0)
            def _():
                acc_ref[...] += part

            @pl.when(k == nk - 1)
            def _():
                fin(acc_ref[...])

    a_spec = pl.BlockSpec((tk, tm), lambda i, j, k: (k, i)) if ta else pl.BlockSpec((tm, tk), lambda i, j, k: (i, k))
    b_spec = (pl.BlockSpec((tn, tk), lambda i, j, k: (j, k + kb)) if tb
              else pl.BlockSpec((tk, tn), lambda i, j, k: (k + kb, j)))
    o_spec = pl.BlockSpec((tm, tn), lambda i, j, k: (i, j))
    in_specs, ops = [a_spec, b_spec], [a, b]
    if res is not None:
        in_specs.append(o_spec)
        ops.append(res)
    out_shape, aliases = jax.ShapeDtypeStruct((M, N), out_dtype), {}
    if slab is not None:
        buf, idx, n = slab
        o_spec = pl.BlockSpec((None, tm, tn), lambda i, j, k: (idx, i, j + jb))
        out_shape = jax.ShapeDtypeStruct((n, M, cols or N), out_dtype)
        if buf is not None:
            aliases = {len(ops): 0}
            in_specs.append(ANY)
            ops.append(buf)
    return pl.pallas_call(
        body, grid=(M // tm, N // tn, nk), in_specs=in_specs, out_specs=o_spec, out_shape=out_shape,
        input_output_aliases=aliases, scratch_shapes=[pltpu.VMEM((tm, tn), F32)] if nk > 1 else [],
        compiler_params=_cp("parallel", "parallel", "arbitrary"), name=name)(*ops)


def _transpose(x, name):
    T, C = x.shape
    tr, tc = _rt(T), _pick(C, (512, 256, 128))

    def body(x_ref, o_ref):
        o_ref[...] = x_ref[...].T

    return pl.pallas_call(
        body, grid=(T // tr, C // tc), in_specs=[pl.BlockSpec((tr, tc), lambda i, j: (i, j))],
        out_specs=pl.BlockSpec((tc, tr), lambda i, j: (j, i)), out_shape=jax.ShapeDtypeStruct((C, T), x.dtype),
        compiler_params=_cp("parallel", "parallel"), name=name)(x)


def _rms_fwd(h, g, name):
    T = h.shape[0]
    tr = _rt(T)

    def body(h_ref, g_ref, u_ref):
        x = h_ref[...]
        r = lax.rsqrt(jnp.mean(x * x, axis=-1, keepdims=True) + EPS)
        u_ref[...] = (x * r * g_ref[...]).astype(BF16)

    return pl.pallas_call(
        body, grid=(T // tr,),
        in_specs=[pl.BlockSpec((tr, D), lambda i: (i, 0)), pl.BlockSpec((1, D), lambda i: (0, 0))],
        out_specs=pl.BlockSpec((tr, D), lambda i: (i, 0)), out_shape=jax.ShapeDtypeStruct((T, D), BF16),
        compiler_params=_cp("parallel"), name=name)(h, g.reshape(1, D))


def _rms_bwd(du, h, g, dres, name):
    T = h.shape[0]
    tr = _rt(T)

    def body(du_ref, h_ref, g_ref, dres_ref, dh_ref, dhb_ref, dg_ref):
        x = h_ref[...]
        r = lax.rsqrt(jnp.mean(x * x, axis=-1, keepdims=True) + EPS)
        xh = x * r
        dy = du_ref[...]
        dxh = dy * g_ref[...]
        dx = r * (dxh - xh * jnp.mean(dxh * xh, axis=-1, keepdims=True))
        dh = dres_ref[...] + dx
        dh_ref[...] = dh
        dhb_ref[...] = dh.astype(BF16)
        part = jnp.sum(dy * xh, axis=0, keepdims=True)

        @pl.when(pl.program_id(0) == 0)
        def _():
            dg_ref[...] = part

        @pl.when(pl.program_id(0) > 0)
        def _():
            dg_ref[...] += part

    row = pl.BlockSpec((tr, D), lambda i: (i, 0))
    vec = pl.BlockSpec((1, D), lambda i: (0, 0))
    return pl.pallas_call(
        body, grid=(T // tr,), in_specs=[row, row, vec, row], out_specs=[row, row, vec],
        out_shape=[jax.ShapeDtypeStruct((T, D), F32), jax.ShapeDtypeStruct((T, D), BF16), jax.ShapeDtypeStruct((1, D), F32)],
        compiler_params=_cp("arbitrary"), name=name)(du, h, g.reshape(1, D), dres)


def _loss_head(h, tgt, g, name):
    T = h.shape[0]
    nb = T // BLK

    def body(h_ref, t_ref, g_ref, dh_ref, dhb_ref, dg_ref, loss_ref):
        i = pl.program_id(0)
        x = h_ref[...]
        r = lax.rsqrt(jnp.mean(x * x, axis=-1, keepdims=True) + EPS)
        xh = x * r
        gv = g_ref[...]
        tok = i >= 1
        err = jnp.where(tok, xh * gv - t_ref[...], 0.0)
        dy = err * (1.0 / D)
        dxh = dy * gv
        dx = r * (dxh - xh * jnp.mean(dxh * xh, axis=-1, keepdims=True))
        dh_ref[...] = dx
        dhb_ref[...] = dx.astype(BF16)
        dg = jnp.sum(dy * xh, axis=0, keepdims=True)
        ls = jnp.zeros((1, BLK), F32) + jnp.sum(err * err) * (0.5 / D)

        @pl.when(i == 0)
        def _():
            dg_ref[...] = dg
            loss_ref[...] = ls

        @pl.when(i > 0)
        def _():
            dg_ref[...] += dg
            loss_ref[...] += ls

    row = pl.BlockSpec((BLK, D), lambda i: (i, 0))
    vec = pl.BlockSpec((1, D), lambda i: (0, 0))
    return pl.pallas_call(
        body, grid=(nb,),
        in_specs=[row, pl.BlockSpec((BLK, D), lambda i: (jnp.maximum(i - 1, 0), 0)), vec],
        out_specs=[row, row, vec, pl.BlockSpec((1, BLK), lambda i: (0, 0))],
        out_shape=[jax.ShapeDtypeStruct((T, D), F32), jax.ShapeDtypeStruct((T, D), BF16),
                   jax.ShapeDtypeStruct((1, D), F32), jax.ShapeDtypeStruct((1, BLK), F32)],
        compiler_params=_cp("arbitrary"), name=name)(h, tgt, g.reshape(1, D))


def _bucket_table():
    q = np.arange(BLK)[:, None]
    k = np.arange(2 * BLK)[None, :]
    d = np.maximum(q + BLK - k, 0)
    max_exact = REL_BUCKETS // 2
    scaled = np.log(np.maximum(d, 1).astype(np.float32) / np.float32(max_exact)) / np.float32(math.log(128 / max_exact))
    large = np.minimum(max_exact + (scaled.astype(np.float32) * (REL_BUCKETS - max_exact)).astype(np.int32), REL_BUCKETS - 1)
    return np.where(d < max_exact, d, large).astype(np.int32)


def _bias_build(table, bucket, name):
    def body(t_ref, bk_ref, o_ref):
        bk = bk_ref[...]
        for h in range(NH):
            acc = jnp.zeros((BLK, 2 * BLK), F32)
            for b in range(REL_BUCKETS):
                acc = jnp.where(bk == b, t_ref[b, h], acc)
            o_ref[h] = acc

    return pl.pallas_call(
        body, in_specs=[SMEM, pl.BlockSpec(memory_space=pltpu.VMEM)], out_specs=pl.BlockSpec(memory_space=pltpu.VMEM),
        out_shape=jax.ShapeDtypeStruct((NH, BLK, 2 * BLK), F32), compiler_params=_cp(), name=name)(table, bucket)


def _bias_bwd(dbias, bucket, name):
    def body(d_ref, bk_ref, o_ref):
        bk = bk_ref[...]
        for h in range(NH):
            dh = d_ref[h]
            for b in range(REL_BUCKETS):
                o_ref[b, h] = jnp.sum(jnp.where(bk == b, dh, 0.0))

    return pl.pallas_call(
        body, in_specs=[pl.BlockSpec(memory_space=pltpu.VMEM)] * 2, out_specs=SMEM,
        out_shape=jax.ShapeDtypeStruct((REL_BUCKETS, NH), F32), compiler_params=_cp(), name=name)(dbias, bucket)


def _swa_specs(nq_cols):
    prev = lambda n: jnp.maximum(n - 1, 0)
    return [
        pl.BlockSpec((BLK, nq_cols), lambda n: (n, QA // nq_cols)),
        pl.BlockSpec((BLK, BLK), lambda n: (prev(n), KA // BLK)), pl.BlockSpec((BLK, BLK), lambda n: (n, KA // BLK)),
        pl.BlockSpec((BLK, BLK), lambda n: (prev(n), VA // BLK)), pl.BlockSpec((BLK, BLK), lambda n: (n, VA // BLK)),
    ]


def _swa_mask(n):
    row = lax.broadcasted_iota(jnp.int32, (BLK, 2 * BLK), 0)
    col = lax.broadcasted_iota(jnp.int32, (BLK, 2 * BLK), 1)
    dist = row + BLK - col
    return (dist >= 0) & (dist < BLK) & ((n - 1) * BLK + col >= NPAD)


def _swa_probs(qm, ksel, mask, bias_h, sink):
    s = _dot(qm, ksel, 1, 1) * SCALE
    s = jnp.where(mask, s + bias_h, NEG)
    m = jnp.maximum(jnp.max(s, axis=-1, keepdims=True), sink)
    p = jnp.exp(s - m)
    psink = jnp.exp(sink - m)
    inv = 1.0 / (jnp.sum(p, axis=-1, keepdims=True) + psink)
    return p * inv, psink * inv


def _swa_fwd(proj, bias, sinks, name):
    T = proj.shape[0]
    nb = T // BLK

    def body(sk_ref, q_ref, kp_ref, kc_ref, vp_ref, vc_ref, b_ref, o_ref):
        n = pl.program_id(0)
        lo = lax.broadcasted_iota(jnp.int32, (1, BLK), 1) < 64
        kb = jnp.concatenate([kp_ref[...], kc_ref[...]], axis=0)
        vb = jnp.concatenate([vp_ref[...], vc_ref[...]], axis=0)
        kbs = (kb.astype(BF16), pltpu.roll(kb, 64, 1).astype(BF16))
        vbs = (vb, pltpu.roll(vb, 64, 1))
        mask = _swa_mask(n)
        outs = []
        for pr in range(NH // 2):
            qp = q_ref[:, pr * BLK:(pr + 1) * BLK]
            kv = pr // 2
            acc = jnp.zeros((BLK, BLK), F32)
            for e in range(2):
                lm = lo if e == 0 else jnp.logical_not(lo)
                sw = 0 if kv == e else 1
                qm = jnp.where(lm, qp, 0.0).astype(BF16)
                pn, _ = _swa_probs(qm, kbs[sw], mask, b_ref[2 * pr + e], sk_ref[2 * pr + e])
                acc = acc + _dot(pn.astype(BF16), jnp.where(lm, vbs[sw], 0.0).astype(BF16), 1, 0)
            outs.append(acc)
        o_ref[...] = jnp.concatenate(outs, axis=1).astype(BF16)

    return pl.pallas_call(
        body, grid=(nb,),
        in_specs=[SMEM] + _swa_specs(512) + [pl.BlockSpec((NH, BLK, 2 * BLK), lambda n: (0, 0, 0))],
        out_specs=pl.BlockSpec((BLK, 512), lambda n: (n, 0)), out_shape=jax.ShapeDtypeStruct((T, 512), BF16),
        compiler_params=_cp("parallel"), name=name)(sinks, proj, proj, proj, proj, proj, bias)


def _swa_bwd(proj, bias, sinks, do, dbias_in, name):
    T = proj.shape[0]
    nb = T // BLK

    def body(sk_ref, q_ref, kp_ref, kc_ref, vp_ref, vc_ref, b_ref, do_ref, dbi_ref,
             dq_ref, dk_ref, dv_ref, db_ref, dsk_ref, sk_acc):
        n = pl.program_id(0)
        lane = lax.broadcasted_iota(jnp.int32, (1, BLK), 1)
        lo = lane < 64
        kb = jnp.concatenate([kp_ref[...], kc_ref[...]], axis=0)
        vb = jnp.concatenate([vp_ref[...], vc_ref[...]], axis=0)
        kbs = (kb, pltpu.roll(kb, 64, 1))
        vbs = (vb, pltpu.roll(vb, 64, 1))
        mask = _swa_mask(n)

        @pl.when(n == 0)
        def _():
            db_ref[...] = dbi_ref[...]
            sk_acc[...] = jnp.zeros_like(sk_acc)

        dqs = []
        dk = jnp.zeros((2 * BLK, BLK), F32)
        dv = jnp.zeros((2 * BLK, BLK), F32)
        for pr in range(NH // 2):
            qp = q_ref[:, pr * BLK:(pr + 1) * BLK]
            dop = do_ref[:, pr * BLK:(pr + 1) * BLK].astype(F32)
            kv = pr // 2
            dq = jnp.zeros((BLK, BLK), F32)
            for e in range(2):
                h = 2 * pr + e
                lm = lo if e == 0 else jnp.logical_not(lo)
                sw = 0 if kv == e else 1
                qm = jnp.where(lm, qp, 0.0)
                dom = jnp.where(lm, dop, 0.0)
                pn, ps = _swa_probs(qm.astype(BF16), kbs[sw].astype(BF16), mask, b_ref[h], sk_ref[h])
                dp = _dot(dom.astype(BF16), vbs[sw].astype(BF16), 1, 1)
                delta = jnp.sum(pn * dp, axis=-1, keepdims=True)
                ds = pn * (dp - delta)
                db_ref[h] += ds
                sk_acc[...] += jnp.where(lane == h, -(ps * delta), 0.0)
                dsb = (ds * SCALE).astype(BF16)
                dq = dq + _dot(dsb, jnp.where(lm, kbs[sw], 0.0).astype(BF16), 1, 0)
                qk = qm if sw == 0 else pltpu.roll(qm, 64, 1)
                dok = dom if sw == 0 else pltpu.roll(dom, 64, 1)
                dk = dk + _dot(dsb, qk.astype(BF16), 0, 0)
                dv = dv + _dot(pn.astype(BF16), dok.astype(BF16), 0, 0)
            dqs.append(dq)
        dq_ref[...] = jnp.concatenate(dqs, axis=1).astype(BF16)
        dk_ref[0] = dk
        dv_ref[0] = dv

        @pl.when(n == nb - 1)
        def _():
            dsk_ref[...] = jnp.sum(sk_acc[...], axis=0, keepdims=True)

    full_b = pl.BlockSpec((NH, BLK, 2 * BLK), lambda n: (0, 0, 0))
    band = pl.BlockSpec((1, 2 * BLK, BLK), lambda n: (n, 0, 0))
    return pl.pallas_call(
        body, grid=(nb,),
        in_specs=[SMEM] + _swa_specs(512) + [full_b, pl.BlockSpec((BLK, 512), lambda n: (n, 0)), full_b],
        out_specs=[pl.BlockSpec((BLK, 512), lambda n: (n, 0)), band, band, full_b, pl.BlockSpec((1, BLK), lambda n: (0, 0))],
        out_shape=[jax.ShapeDtypeStruct((T, 512), BF16), jax.ShapeDtypeStruct((nb, 2 * BLK, BLK), F32),
                   jax.ShapeDtypeStruct((nb, 2 * BLK, BLK), F32), jax.ShapeDtypeStruct((NH, BLK, 2 * BLK), F32),
                   jax.ShapeDtypeStruct((1, BLK), F32)],
        scratch_shapes=[pltpu.VMEM((BLK, BLK), F32)],
        compiler_params=_cp("arbitrary"), name=name)(sinks, proj, proj, proj, proj, proj, bias, do, dbias_in)


def _band_fold(dkb, dvb, name):
    nb = dkb.shape[0]

    def body(ko_ref, kn_ref, vo_ref, vn_ref, dk_ref, dv_ref):
        last = pl.program_id(0) == nb - 1
        dk_ref[...] = (ko_ref[0] + jnp.where(last, 0.0, kn_ref[0])).astype(BF16)
        dv_ref[...] = (vo_ref[0] + jnp.where(last, 0.0, vn_ref[0])).astype(BF16)

    own = pl.BlockSpec((1, BLK, BLK), lambda j: (j, 1, 0))
    nxt = pl.BlockSpec((1, BLK, BLK), lambda j: (jnp.minimum(j + 1, nb - 1), 0, 0))
    out = pl.BlockSpec((BLK, BLK), lambda j: (j, 0))
    return pl.pallas_call(
        body, grid=(nb,), in_specs=[own, nxt, own, nxt], out_specs=[out, out],
        out_shape=[jax.ShapeDtypeStruct((nb * BLK, BLK), BF16)] * 2,
        compiler_params=_cp("parallel"), name=name)(dkb, dkb, dvb, dvb)


def _cum_fwd(z3, fb, name):
    nb = z3.shape[0]

    def body(z_ref, fb_ref, c_ref):
        lane = lax.broadcasted_iota(jnp.int32, (NH, BLK), 1)

        def step(b, carry):
            x = jnp.where(b * BLK + lane >= NPAD, _log_sigmoid(z_ref[b] + fb_ref[...]), 0.0)
            s = 1
            while s < BLK:
                x = x + jnp.where(lane >= s, pltpu.roll(x, s, 1), 0.0)
                s *= 2
            x = x + carry
            c_ref[b] = x
            return jnp.sum(jnp.where(lane == BLK - 1, x, 0.0), axis=-1, keepdims=True)

        lax.fori_loop(0, nb, step, jnp.zeros((NH, 1), F32))

    return pl.pallas_call(body, out_shape=jax.ShapeDtypeStruct((nb, NH, BLK), F32), compiler_params=_cp(), name=name)(z3, fb)


def _cum_bwd(dck3, dcq3, z3, fb, name):
    nb = z3.shape[0]

    def body(d_ref, dq_ref, z_ref, fb_ref, dz_ref, db_ref):
        lane = lax.broadcasted_iota(jnp.int32, (NH, BLK), 1)

        def step(k, carry):
            suffix, tot = carry
            b = nb - 1 - k
            x = d_ref[b] + dq_ref[b]
            s = 1
            while s < BLK:
                x = x + jnp.where(lane < BLK - s, pltpu.roll(x, BLK - s, 1), 0.0)
                s *= 2
            x = x + suffix
            dz = jnp.where(b * BLK + lane >= NPAD, x * _sigmoid(-(z_ref[b] + fb_ref[...])), 0.0)
            dz_ref[b] = dz
            return (jnp.sum(jnp.where(lane == 0, x, 0.0), axis=-1, keepdims=True),
                    tot + jnp.sum(dz, axis=-1, keepdims=True))

        z0 = jnp.zeros((NH, 1), F32)
        _, tot = lax.fori_loop(0, nb, step, (z0, z0))
        db_ref[...] = jnp.zeros((NH, BLK), F32) + tot

    return pl.pallas_call(
        body, out_shape=[jax.ShapeDtypeStruct((nb, NH, BLK), F32), jax.ShapeDtypeStruct((NH, BLK), F32)],
        compiler_params=_cp(), name=name)(dck3, dcq3, z3, fb)


def _to_blocks(a):
    return a.reshape(NH, -1, BLK).transpose(1, 0, 2)


def _from_blocks(a):
    return a.transpose(1, 0, 2).reshape(NH, -1)


def _fox_scores(q, kbf, mask, lm, cq, ck):
    s = _dot(jnp.where(lm, q, 0.0).astype(BF16), kbf, 1, 1) * SCALE
    return jnp.where(mask, s + cq - ck, NEG)


def _fox_mask(i, j, t):
    row = i * t + lax.broadcasted_iota(jnp.int32, (t, t), 0)
    col = j * t + lax.broadcasted_iota(jnp.int32, (t, t), 1)
    return (col <= row) & (col >= NPAD)


def _lane_pick(x, lane, idx):
    return jnp.sum(jnp.where(lane == idx, x, 0.0), axis=-1, keepdims=True)


def _fox_fwd(proj, crow, ccol, name):
    T = proj.shape[0]
    t = _rt(T)
    nt = T // t

    def body(q_ref, k_ref, v_ref, cr_ref, cc_ref, o_ref, lse_ref, m_ref, l_ref, acc_ref):
        p_, i, j = pl.program_id(0), pl.program_id(1), pl.program_id(2)
        lane = lax.broadcasted_iota(jnp.int32, (1, BLK), 1)
        lo = lane < 64

        @pl.when(j == 0)
        def _():
            m_ref[...] = jnp.full_like(m_ref, NEG)
            l_ref[...] = jnp.zeros_like(l_ref)
            acc_ref[...] = jnp.zeros_like(acc_ref)

        @pl.when(j <= i)
        def _():
            q = q_ref[...]
            kbf = k_ref[...].astype(BF16)
            v = v_ref[...]
            cc = cc_ref[...]
            mask = _fox_mask(i, j, t)
            alphas, pvs = [], []
            for e in range(2):
                h = 2 * p_ + e
                lm = lo if e == 0 else jnp.logical_not(lo)
                s = _fox_scores(q, kbf, mask, lm, _lane_pick(cc, lane, h), cr_ref[pl.ds(h, 1), :])
                m_old = m_ref[e]
                m_new = jnp.maximum(m_old, jnp.max(s, axis=-1, keepdims=True))
                alpha = jnp.exp(m_old - m_new)
                pe = jnp.exp(s - m_new)
                l_ref[e] = alpha * l_ref[e] + jnp.sum(pe, axis=-1, keepdims=True)
                m_ref[e] = m_new
                alphas.append(alpha)
                pvs.append(_dot(pe.astype(BF16), jnp.where(lm, v, 0.0).astype(BF16), 1, 0))
            acc_ref[...] = acc_ref[...] * jnp.where(lo, alphas[0], alphas[1]) + pvs[0] + pvs[1]

        @pl.when(j == i)
        def _():
            rows = i * t + lax.broadcasted_iota(jnp.int32, (t, 1), 0)
            o = acc_ref[...] / jnp.where(lo, l_ref[0], l_ref[1])
            o_ref[...] = jnp.where(rows >= NPAD, o, 0.0).astype(BF16)
            lse_ref[...] = jnp.where(lo, m_ref[0] + jnp.log(l_ref[0]), m_ref[1] + jnp.log(l_ref[1]))

    kj = lambda i, j: jnp.minimum(j, i)
    return pl.pallas_call(
        body, grid=(NH // 2, nt, nt),
        in_specs=[pl.BlockSpec((t, BLK), lambda p, i, j: (i, QF // BLK + p)),
                  pl.BlockSpec((t, BLK), lambda p, i, j: (kj(i, j), KF // BLK + p)),
                  pl.BlockSpec((t, BLK), lambda p, i, j: (kj(i, j), VF // BLK + p)),
                  pl.BlockSpec((NH, t), lambda p, i, j: (0, kj(i, j))),
                  pl.BlockSpec((t, BLK), lambda p, i, j: (i, 0))],
        out_specs=[pl.BlockSpec((t, BLK), lambda p, i, j: (i, p))] * 2,
        out_shape=[jax.ShapeDtypeStruct((T, 512), BF16), jax.ShapeDtypeStruct((T, 512), F32)],
        scratch_shapes=[pltpu.VMEM((2, t, 1), F32), pltpu.VMEM((2, t, 1), F32), pltpu.VMEM((t, BLK), F32)],
        compiler_params=_cp("parallel", "parallel", "arbitrary"), name=name)(proj, proj, proj, crow, ccol)


def _fox_dq(proj, crow, ccol, o, lse, do, name):
    T = proj.shape[0]
    t = _rt(T)
    nt = T // t

    def body(q_ref, k_ref, v_ref, cr_ref, cc_ref, o_ref, lse_ref, do_ref, dq_ref, rs_ref, acc_ref, dl_ref, rs_acc):
        p_, i, j = pl.program_id(0), pl.program_id(1), pl.program_id(2)
        lane = lax.broadcasted_iota(jnp.int32, (1, BLK), 1)
        lo = lane < 64

        @pl.when(j == 0)
        def _():
            acc_ref[...] = jnp.zeros_like(acc_ref)
            rs_acc[...] = jnp.zeros_like(rs_acc)
            prod = do_ref[...].astype(F32) * o_ref[...].astype(F32)
            dl_ref[0] = jnp.sum(jnp.where(lo, prod, 0.0), axis=-1, keepdims=True)
            dl_ref[1] = jnp.sum(jnp.where(lo, 0.0, prod), axis=-1, keepdims=True)

        @pl.when(j <= i)
        def _():
            q = q_ref[...]
            k = k_ref[...]
            kbf = k.astype(BF16)
            vbf = v_ref[...].astype(BF16)
            cc = cc_ref[...]
            do_ = do_ref[...].astype(F32)
            lse_ = lse_ref[...]
            mask = _fox_mask(i, j, t)
            acc = acc_ref[...]
            for e in range(2):
                h = 2 * p_ + e
                lm = lo if e == 0 else jnp.logical_not(lo)
                s = _fox_scores(q, kbf, mask, lm, _lane_pick(cc, lane, h), cr_ref[pl.ds(h, 1), :])
                pe = jnp.exp(s - _lane_pick(lse_, lane, 64 * e))
                dp = _dot(jnp.where(lm, do_, 0.0).astype(BF16), vbf, 1, 1)
                ds = pe * (dp - dl_ref[e])
                rs_acc[e] += jnp.sum(ds, axis=-1, keepdims=True)
                acc = acc + _dot((ds * SCALE).astype(BF16), jnp.where(lm, k, 0.0).astype(BF16), 1, 0)
            acc_ref[...] = acc

        @pl.when(j == i)
        def _():
            dq_ref[...] = acc_ref[...].astype(BF16)
            rs_ref[...] = jnp.where(lo, rs_acc[0], rs_acc[1])

    kj = lambda i, j: jnp.minimum(j, i)
    qside = pl.BlockSpec((t, BLK), lambda p, i, j: (i, p))
    return pl.pallas_call(
        body, grid=(NH // 2, nt, nt),
        in_specs=[pl.BlockSpec((t, BLK), lambda p, i, j: (i, QF // BLK + p)),
                  pl.BlockSpec((t, BLK), lambda p, i, j: (kj(i, j), KF // BLK + p)),
                  pl.BlockSpec((t, BLK), lambda p, i, j: (kj(i, j), VF // BLK + p)),
                  pl.BlockSpec((NH, t), lambda p, i, j: (0, kj(i, j))),
                  pl.BlockSpec((t, BLK), lambda p, i, j: (i, 0)), qside, qside, qside],
        out_specs=[qside, qside], out_shape=[jax.ShapeDtypeStruct((T, 512), BF16), jax.ShapeDtypeStruct((T, 512), F32)],
        scratch_shapes=[pltpu.VMEM((t, BLK), F32), pltpu.VMEM((2, t, 1), F32), pltpu.VMEM((2, t, 1), F32)],
        compiler_params=_cp("parallel", "parallel", "arbitrary"), name=name)(proj, proj, proj, crow, ccol, o, lse, do)


def _fox_dkv(proj, crow, ccol, o, lse, do, name):
    T = proj.shape[0]
    t = _rt(T)
    nt = T // t

    def body(q_ref, k_ref, v_ref, cr_ref, cc_ref, o_ref, lse_ref, do_ref, dk_ref, dv_ref, dc_ref, dk_acc, dv_acc, dc_acc):
        p_, j, i = pl.program_id(0), pl.program_id(1), pl.program_id(2)
        lane = lax.broadcasted_iota(jnp.int32, (1, BLK), 1)
        lo = lane < 64

        @pl.when(i == 0)
        def _():
            dk_acc[...] = jnp.zeros_like(dk_acc)
            dv_acc[...] = jnp.zeros_like(dv_acc)
            dc_acc[...] = jnp.zeros_like(dc_acc)

        @pl.when(i >= j)
        def _():
            q = q_ref[...]
            kbf = k_ref[...].astype(BF16)
            vbf = v_ref[...].astype(BF16)
            cc = cc_ref[...]
            do_ = do_ref[...].astype(F32)
            lse_ = lse_ref[...]
            prod = do_ * o_ref[...].astype(F32)
            mask = _fox_mask(i, j, t)
            dk = dk_acc[...]
            dv = dv_acc[...]
            for e in range(2):
                h = 2 * p_ + e
                lm = lo if e == 0 else jnp.logical_not(lo)
                s = _fox_scores(q, kbf, mask, lm, _lane_pick(cc, lane, h), cr_ref[pl.ds(h, 1), :])
                pe = jnp.exp(s - _lane_pick(lse_, lane, 64 * e))
                dom = jnp.where(lm, do_, 0.0).astype(BF16)
                dp = _dot(dom, vbf, 1, 1)
                delta = jnp.sum(jnp.where(lm, prod, 0.0), axis=-1, keepdims=True)
                ds = pe * (dp - delta)
                dv = dv + _dot(pe.astype(BF16), dom, 0, 0)
                dk = dk + _dot((ds * SCALE).astype(BF16), jnp.where(lm, q, 0.0).astype(BF16), 0, 0)
                dc_acc[e:e + 1, :] -= jnp.sum(ds, axis=0, keepdims=True)
            dk_acc[...] = dk
            dv_acc[...] = dv

        @pl.when(i == nt - 1)
        def _():
            dk_ref[...] = dk_acc[...].astype(BF16)
            dv_ref[...] = dv_acc[...].astype(BF16)
            dc_ref[0] = dc_acc[...]

    qi = lambda j, i: jnp.maximum(i, j)
    qside = pl.BlockSpec((t, BLK), lambda p, j, i: (qi(j, i), p))
    kside = pl.BlockSpec((t, BLK), lambda p, j, i: (j, p))
    return pl.pallas_call(
        body, grid=(NH // 2, nt, nt),
        in_specs=[pl.BlockSpec((t, BLK), lambda p, j, i: (qi(j, i), QF // BLK + p)),
                  pl.BlockSpec((t, BLK), lambda p, j, i: (j, KF // BLK + p)),
                  pl.BlockSpec((t, BLK), lambda p, j, i: (j, VF // BLK + p)),
                  pl.BlockSpec((NH, t), lambda p, j, i: (0, j)),
                  pl.BlockSpec((t, BLK), lambda p, j, i: (qi(j, i), 0)), qside, qside, qside],
        out_specs=[kside, kside, pl.BlockSpec((1, NH, t), lambda p, j, i: (p, 0, j))],
        out_shape=[jax.ShapeDtypeStruct((T, 512), BF16), jax.ShapeDtypeStruct((T, 512), BF16),
                   jax.ShapeDtypeStruct((NH // 2, NH, T), F32)],
        scratch_shapes=[pltpu.VMEM((t, BLK), F32), pltpu.VMEM((t, BLK), F32), pltpu.VMEM((NH, t), F32)],
        compiler_params=_cp("parallel", "parallel", "arbitrary"), name=name)(proj, proj, proj, crow, ccol, o, lse, do)


def _lru_gates(xc, wr_ref, wi_ref, vec_ref):
    xb = xc.astype(BF16)
    pre_r = jnp.concatenate([_dot(xb[:, p * BLK:(p + 1) * BLK], wr_ref[p], 1, 0) for p in range(LW // BLK)], axis=1)
    pre_i = jnp.concatenate([_dot(xb[:, p * BLK:(p + 1) * BLK], wi_ref[p], 1, 0) for p in range(LW // BLK)], axis=1)
    r = _sigmoid(pre_r + vec_ref[0:1, :])
    gi = _sigmoid(pre_i + vec_ref[1:2, :])
    log_a = LRU_C * r * _log_sigmoid(vec_ref[2:3, :])
    a = jnp.exp(log_a)
    mult = jnp.sqrt(_neg_expm1(2.0 * log_a))
    return r, gi, a, mult


def _conv(xbuf_ref, x, cw_ref, vec_ref, tr):
    return (cw_ref[3:4, :] * x + cw_ref[2:3, :] * xbuf_ref[7:7 + tr, :] + cw_ref[1:2, :] * xbuf_ref[6:6 + tr, :]
            + cw_ref[0:1, :] * xbuf_ref[5:5 + tr, :] + vec_ref[3:4, :])


def _lru_fwd(proj, cw, wr, wi, vec, name):
    T = proj.shape[0]
    tr = _rt(T)

    def body(x_ref, y_ref, cw_ref, wr_ref, wi_ref, vec_ref, oc_ref, hs_ref, xbuf, abuf, bbuf, hcar):
        i = pl.program_id(0)

        @pl.when(i == 0)
        def _():
            xbuf[0:8, :] = jnp.zeros((8, LW), F32)
            hcar[...] = jnp.zeros_like(hcar)

        x = x_ref[...]
        xbuf[8:8 + tr, :] = x
        xc = _conv(xbuf, x, cw_ref, vec_ref, tr)
        xbuf[0:8, :] = x[tr - 8:tr, :]
        _, gi, a, mult = _lru_gates(xc, wr_ref, wi_ref, vec_ref)
        rows = i * tr + lax.broadcasted_iota(jnp.int32, (tr, 1), 0)
        abuf[...] = a
        bbuf[...] = jnp.where(rows >= NPAD, mult * (gi * xc), 0.0)
        sub = lax.broadcasted_iota(jnp.int32, (8, 1), 0)

        def step(k, h):
            sl = pl.ds(pl.multiple_of(k * 8, 8), 8)
            a8, b8 = abuf[sl, :], bbuf[sl, :]
            for s in (1, 2, 4):
                ok = sub >= s
                b8 = jnp.where(ok, a8 * pltpu.roll(b8, s, 0) + b8, b8)
                a8 = jnp.where(ok, a8 * pltpu.roll(a8, s, 0), a8)
            h8 = a8 * h + b8
            bbuf[sl, :] = h8
            return h8[7:8, :]

        hcar[...] = lax.fori_loop(0, tr // 8, step, hcar[...])
        hs = bbuf[...]
        hs_ref[...] = hs
        oc_ref[...] = (hs * _gelu(y_ref[...])).astype(BF16)

    row = pl.BlockSpec((tr, LW), lambda i: (i, 0))
    full = lambda shape: pl.BlockSpec(shape, lambda i: (0,) * len(shape))
    return pl.pallas_call(
        body, grid=(T // tr,),
        in_specs=[pl.BlockSpec((tr, LW), lambda i: (i, XC // LW)), pl.BlockSpec((tr, LW), lambda i: (i, YC // LW)),
                  full((4, LW)), full((4, BLK, BLK)), full((4, BLK, BLK)), full((8, LW))],
        out_specs=[row, row], out_shape=[jax.ShapeDtypeStruct((T, LW), BF16), jax.ShapeDtypeStruct((T, LW), F32)],
        scratch_shapes=[pltpu.VMEM((tr + 8, LW), F32), pltpu.VMEM((tr, LW), F32), pltpu.VMEM((tr, LW), F32),
                        pltpu.VMEM((1, LW), F32)],
        compiler_params=_cp("arbitrary"), name=name)(proj, proj, cw, wr, wi, vec)


def _lru_bwd(proj, hs, doc, cw, wr, wi, vec, name):
    T = proj.shape[0]
    tr = _rt(T)
    nt = T // tr
    r8 = tr // 8

    def body(x_ref, xp_ref, y_ref, hs_ref, hp_ref, do_ref, cw_ref, wr_ref, wi_ref, vec_ref,
             dx_ref, dy_ref, dwr_ref, dwi_ref, dvec_ref, xbuf, abuf, gbuf, hbuf, dbuf, gcar, acar):
        k = pl.program_id(0)
        i = nt - 1 - k

        @pl.when(k == 0)
        def _():
            dwr_ref[...] = jnp.zeros_like(dwr_ref)
            dwi_ref[...] = jnp.zeros_like(dwi_ref)
            dvec_ref[...] = jnp.zeros_like(dvec_ref)
            gcar[...] = jnp.zeros_like(gcar)
            acar[...] = jnp.zeros_like(acar)
            dbuf[tr:tr + 8, :] = jnp.zeros((8, LW), F32)

        first = i == 0
        x = x_ref[...]
        xbuf[0:8, :] = jnp.where(first, 0.0, xp_ref[...])
        xbuf[8:8 + tr, :] = x
        xc = _conv(xbuf, x, cw_ref, vec_ref, tr)
        r, gi, a, mult = _lru_gates(xc, wr_ref, wi_ref, vec_ref)
        y = y_ref[...]
        hs = hs_ref[...]
        do_ = do_ref[...].astype(F32)
        rows = i * tr + lax.broadcasted_iota(jnp.int32, (tr, 1), 0)
        abuf[0:tr, :] = a
        abuf[tr:tr + 8, :] = jnp.zeros((8, LW), F32) + acar[...]
        an = abuf[1:1 + tr, :]
        acar[...] = a[0:1, :]
        abuf[0:tr, :] = an
        gbuf[...] = do_ * _gelu(y)
        sub = lax.broadcasted_iota(jnp.int32, (8, 1), 0)

        def step(kk, g):
            sl = pl.ds(pl.multiple_of((r8 - 1 - kk) * 8, 8), 8)
            a8, b8 = abuf[sl, :], gbuf[sl, :]
            for s in (1, 2, 4):
                ok = sub < 8 - s
                b8 = jnp.where(ok, a8 * pltpu.roll(b8, 8 - s, 0) + b8, b8)
                a8 = jnp.where(ok, a8 * pltpu.roll(a8, 8 - s, 0), a8)
            g8 = a8 * g + b8
            gbuf[sl, :] = g8
            return g8[0:1, :]

        gcar[...] = lax.fori_loop(0, r8, step, gcar[...])
        g = gbuf[...]
        hbuf[0:8, :] = jnp.where(first, 0.0, hp_ref[...])
        hbuf[8:8 + tr, :] = hs
        hprev = hbuf[7:7 + tr, :]
        dinp = jnp.where(rows >= NPAD, g, 0.0)
        da = g * hprev
        dmult = dinp * gi * xc
        dgi = dinp * mult * xc
        dxc = dinp * mult * gi
        dlog_a = da * a - dmult * a * a / mult
        ls = _log_sigmoid(vec_ref[2:3, :])
        dpre_r = dlog_a * (LRU_C * ls) * r * (1.0 - r)
        dpre_i = dgi * gi * (1.0 - gi)
        xb = xc.astype(BF16)
        rb, ib = dpre_r.astype(BF16), dpre_i.astype(BF16)
        back = []
        for p in range(LW // BLK):
            c = slice(p * BLK, (p + 1) * BLK)
            back.append(_dot(rb[:, c], wr_ref[p], 1, 1) + _dot(ib[:, c], wi_ref[p], 1, 1))
            dwr_ref[p] += _dot(xb[:, c], rb[:, c], 0, 0)
            dwi_ref[p] += _dot(xb[:, c], ib[:, c], 0, 0)
        dxc = dxc + jnp.concatenate(back, axis=1)
        col = lambda v: jnp.sum(v, axis=0, keepdims=True)
        dvec_ref[0:1, :] += col(dpre_r)
        dvec_ref[1:2, :] += col(dpre_i)
        dvec_ref[2:3, :] += col(dlog_a * (LRU_C * r)) * _sigmoid(-vec_ref[2:3, :])
        dvec_ref[3:4, :] += col(dxc)
        dvec_ref[4:5, :] += col(dxc * xbuf[5:5 + tr, :])
        dvec_ref[5:6, :] += col(dxc * xbuf[6:6 + tr, :])
        dvec_ref[6:7, :] += col(dxc * xbuf[7:7 + tr, :])
        dvec_ref[7:8, :] += col(dxc * x)
        dbuf[0:tr, :] = dxc
        dxr = (cw_ref[3:4, :] * dxc + cw_ref[2:3, :] * dbuf[1:1 + tr, :] + cw_ref[1:2, :] * dbuf[2:2 + tr, :]
               + cw_ref[0:1, :] * dbuf[3:3 + tr, :])
        dbuf[tr:tr + 8, :] = dxc[0:8, :]
        dx_ref[...] = jnp.where(rows >= NPAD, dxr, 0.0).astype(BF16)
        dy_ref[...] = (do_ * hs * _gelu_grad(y)).astype(BF16)

    rev = lambda k: nt - 1 - k
    row = lambda col0: pl.BlockSpec((tr, LW), lambda k: (rev(k), col0))
    prev8 = lambda col0: pl.BlockSpec((8, LW), lambda k: (jnp.maximum(rev(k) * r8 - 1, 0), col0))
    full = lambda shape: pl.BlockSpec(shape, lambda k: (0,) * len(shape))
    return pl.pallas_call(
        body, grid=(nt,),
        in_specs=[row(XC // LW), prev8(XC // LW), row(YC // LW), row(0), prev8(0), row(0),
                  full((4, LW)), full((4, BLK, BLK)), full((4, BLK, BLK)), full((8, LW))],
        out_specs=[row(0), row(0), full((4, BLK, BLK)), full((4, BLK, BLK)), full((8, LW))],
        out_shape=[jax.ShapeDtypeStruct((T, LW), BF16), jax.ShapeDtypeStruct((T, LW), BF16),
                   jax.ShapeDtypeStruct((4, BLK, BLK), F32), jax.ShapeDtypeStruct((4, BLK, BLK), F32),
                   jax.ShapeDtypeStruct((8, LW), F32)],
        scratch_shapes=[pltpu.VMEM((tr + 8, LW), F32), pltpu.VMEM((tr + 8, LW), F32), pltpu.VMEM((tr, LW), F32),
                        pltpu.VMEM((tr + 8, LW), F32), pltpu.VMEM((tr + 8, LW), F32),
                        pltpu.VMEM((1, LW), F32), pltpu.VMEM((1, LW), F32)],
        compiler_params=_cp("arbitrary"), name=name)(proj, proj, proj, hs, hs, doc, cw, wr, wi, vec)


def _branch_merge_fwd(oa, of, oc, wb, proj, name):
    T = proj.shape[0]
    tm, tn = _rt(T), 512

    def body(a0, a1, a2, w_ref, g0, g1, g2, r0, r1, r2, m_ref):
        acc = None
        for g, (a_ref, g_ref, r_ref) in enumerate(((a0, g0, r0), (a1, g1, r1), (a2, g2, r2))):
            b = _dot(a_ref[...], w_ref[g], 1, 0)
            r_ref[...] = b
            term = _sigmoid(g_ref[...]) * b
            acc = term if acc is None else acc + term
        m_ref[...] = acc.astype(BF16)

    act = pl.BlockSpec((tm, LW), lambda j, i: (i, 0))
    gate = lambda g: pl.BlockSpec((tm, tn), lambda j, i: (i, (GT + g * D) // tn + j))
    blk = pl.BlockSpec((tm, tn), lambda j, i: (i, j))
    return pl.pallas_call(
        body, grid=(D // tn, T // tm),
        in_specs=[act, act, act, pl.BlockSpec((3, LW, tn), lambda j, i: (0, 0, j)), gate(0), gate(1), gate(2)],
        out_specs=[blk] * 4,
        out_shape=[jax.ShapeDtypeStruct((T, D), F32)] * 3 + [jax.ShapeDtypeStruct((T, D), BF16)],
        compiler_params=_cp("parallel", "parallel"), name=name)(oa, of, oc, wb, proj, proj, proj)


def _out_dx_merge_bwd(dhb, w_out, proj, b0, b1, b2, name):
    T = proj.shape[0]
    tm, tn = _rt(T), 512

    def body(dh_ref, w_ref, g0, g1, g2, r0, r1, r2, d0, d1, d2, e0, e1, e2):
        dmv = _dot(dh_ref[...], w_ref[...], 1, 1)
        for g_ref, r_ref, d_ref, e_ref in ((g0, r0, d0, e0), (g1, r1, d1, e1), (g2, r2, d2, e2)):
            sg = _sigmoid(g_ref[...])
            d_ref[...] = (dmv * sg).astype(BF16)
            e_ref[...] = (dmv * r_ref[...] * sg * (1.0 - sg)).astype(BF16)

    gate = lambda g: pl.BlockSpec((tm, tn), lambda j, i: (i, (GT + g * D) // tn + j))
    blk = pl.BlockSpec((tm, tn), lambda j, i: (i, j))
    return pl.pallas_call(
        body, grid=(D // tn, T // tm),
        in_specs=[pl.BlockSpec((tm, D), lambda j, i: (i, 0)), pl.BlockSpec((tn, D), lambda j, i: (j, 0)),
                  gate(0), gate(1), gate(2), blk, blk, blk],
        out_specs=[blk] * 6, out_shape=[jax.ShapeDtypeStruct((T, D), BF16)] * 6,
        compiler_params=_cp("parallel", "parallel"), name=name)(dhb, w_out, proj, proj, proj, b0, b1, b2)


def _ffn_in_swiglu_fwd(u, w, name):
    T = u.shape[0]
    tm, tn = _rt(T), _pick(DFF, (1408, 256))
    nj = DFF // tn

    def body(u_ref, wg_ref, wu_ref, g_ref, up_ref, a_ref):
        ub = u_ref[...]
        g = _dot(ub, wg_ref[...], 1, 0)
        up = _dot(ub, wu_ref[...], 1, 0)
        g_ref[...] = g
        up_ref[...] = up
        a_ref[...] = (g * _sigmoid(g) * up).astype(BF16)

    blk = pl.BlockSpec((tm, tn), lambda j, i: (i, j))
    return pl.pallas_call(
        body, grid=(nj, T // tm),
        in_specs=[pl.BlockSpec((tm, D), lambda j, i: (i, 0)), pl.BlockSpec((D, tn), lambda j, i: (0, j)),
                  pl.BlockSpec((D, tn), lambda j, i: (0, j + nj))],
        out_specs=[blk] * 3,
        out_shape=[jax.ShapeDtypeStruct((T, DFF), F32)] * 2 + [jax.ShapeDtypeStruct((T, DFF), BF16)],
        compiler_params=_cp("parallel", "parallel"), name=name)(u, w, w)


def _ffn_out_dx_swiglu_bwd(dhb, w, gate, up, name):
    T = dhb.shape[0]
    tm, tn = _rt(T), _pick(DFF, (1408, 256))

    def body(dh_ref, w_ref, g_ref, up_ref, dg_ref, du_ref):
        d = _dot(dh_ref[...], w_ref[...], 1, 1)
        g = g_ref[...]
        sg = _sigmoid(g)
        dg_ref[...] = (d * up_ref[...] * (sg + g * sg * (1.0 - sg))).astype(BF16)
        du_ref[...] = (d * g * sg).astype(BF16)

    blk = pl.BlockSpec((tm, tn), lambda j, i: (i, j))
    return pl.pallas_call(
        body, grid=(DFF // tn, T // tm),
        in_specs=[pl.BlockSpec((tm, D), lambda j, i: (i, 0)), pl.BlockSpec((tn, D), lambda j, i: (j, 0)), blk, blk],
        out_specs=[blk] * 2, out_shape=[jax.ShapeDtypeStruct((T, DFF), BF16)] * 2,
        compiler_params=_cp("parallel", "parallel"), name=name)(dhb, w, gate, up)


def _adamw(w, g, m, v, name):
    R, C = w.shape
    tr = _pick(R, tuple(t for t in (512, 256, 128, 64, 32, 16, 8) if t * C * 4 <= (3 << 19)))
    c1 = 1.0 - ADAM_B1 ** ADAM_STEP
    c2 = 1.0 - ADAM_B2 ** ADAM_STEP

    def body(w_ref, g_ref, m_ref, v_ref, d_ref, mo_ref, vo_ref):
        gv = g_ref[...]
        mn = ADAM_B1 * m_ref[...] + (1.0 - ADAM_B1) * gv
        vn = ADAM_B2 * v_ref[...] + (1.0 - ADAM_B2) * (gv * gv)
        d_ref[...] = -ADAM_LR * ((mn / c1) / (jnp.sqrt(vn / c2) + ADAM_EPS) + ADAM_WD * w_ref[...])
        mo_ref[...] = mn
        vo_ref[...] = vn

    blk = pl.BlockSpec((tr, C), lambda i: (i, 0))
    return pl.pallas_call(
        body, grid=(R // tr,), in_specs=[blk] * 4, out_specs=[blk] * 3,
        out_shape=[jax.ShapeDtypeStruct((R, C), F32)] * 3, compiler_params=_cp("parallel"), name=name)(w, g, m, v)


def _sum_lead(x, name):
    n, R, C = x.shape
    tr = _pick(R, (512, 256, 128, 64, 32, 16, 8))

    def body(x_ref, o_ref):
        acc = x_ref[0]
        for d in range(1, n):
            acc = acc + x_ref[d]
        o_ref[...] = acc

    return pl.pallas_call(
        body, grid=(R // tr,), in_specs=[pl.BlockSpec((n, tr, C), lambda i: (0, i, 0))],
        out_specs=pl.BlockSpec((tr, C), lambda i: (i, 0)), out_shape=jax.ShapeDtypeStruct((R, C), F32),
        compiler_params=_cp("parallel"), name=name)(x)


def _here():
    return lax.axis_index("x"), lax.axis_index("y"), lax.axis_index("c")


def _rcopy(src, dst, send_sems, recv_sems, k, to):
    return pltpu.make_async_remote_copy(src_ref=src, dst_ref=dst, send_sem=send_sems.at[k], recv_sem=recv_sems.at[k],
                                        device_id=to, device_id_type=MESH)


def _window(ref, lead, axis, start, width):
    idx = [lead] + [slice(None)] * (len(ref.shape) - 1)
    if axis is not None:
        idx[axis] = pl.ds(start, width)
    return ref.at[tuple(idx)]


def _hbm_calls(body, args, out_shapes, n_sems, aliases, name):
    return pl.pallas_call(
        body, in_specs=[ANY] * len(args), out_specs=[ANY] * len(out_shapes), out_shape=out_shapes,
        input_output_aliases=aliases,
        scratch_shapes=[pltpu.SemaphoreType.DMA((n_sems,)), pltpu.SemaphoreType.DMA((n_sems,))],
        compiler_params=pltpu.CompilerParams(has_side_effects=True), name=name)(*args)


def _all_gather_weights(fulls, axes, name):
    nt = len(fulls)

    def body(*refs):
        outs, (send_sems, recv_sems) = refs[nt:2 * nt], refs[2 * nt:]
        x, y, c = _here()
        sib = (x, y, 1 - c)
        chips = [(1 - x, y), (x, 1 - y), (1 - x, 1 - y)]

        def win(t, chip, hc):
            w = outs[t].shape[axes[t]] // N_SHARD
            return _window(outs[t], pl.ds(2 * hc, 2), axes[t], pl.multiple_of((2 * chip[0] + chip[1]) * w, 8), w)

        def copy(t, k, chip, hc, to):
            return _rcopy(win(t, chip, hc), win(t, chip, hc), send_sems, recv_sems, 6 * t + k, to)

        sends = [copy(t, k, (x, y), c, (*chip, c)) for t in range(nt) for k, chip in enumerate(chips)]
        for cp in sends:
            cp.start()
        for t in range(nt):
            for k, chip in enumerate(chips):
                copy(t, k, chip, c, (*chip, c)).wait_recv()
                fwd = copy(t, 3 + k, chip, c, sib)
                fwd.start()
                sends.append(fwd)
        for t in range(nt):
            for k, chip in enumerate(chips):
                copy(t, 3 + k, chip, 1 - c, sib).wait_recv()
        for cp in sends:
            cp.wait_send()

    return _hbm_calls(body, fulls, [jax.ShapeDtypeStruct(f.shape, f.dtype) for f in fulls], 6 * nt,
                      {t: t for t in range(nt)}, name)


def _swap_halves(gs, name):
    nt = len(gs)

    def body(*refs):
        ins, outs, (send_sems, recv_sems) = refs[:nt], refs[nt:2 * nt], refs[2 * nt:]
        x, y, c = _here()
        cps = [_rcopy(g.at[pl.ds(2 * (1 - c), 2)], o, send_sems, recv_sems, t, (x, y, 1 - c))
               for t, (g, o) in enumerate(zip(ins, outs))]
        for cp in cps:
            cp.start()
        for cp in cps:
            cp.wait()

    return _hbm_calls(body, gs, [jax.ShapeDtypeStruct((2,) + g.shape[1:], g.dtype) for g in gs], nt, {}, name)


def _scatter_to_chips(ss, axes, name):
    nt = len(ss)

    def shard_shape(s, ax):
        shp = list(s.shape)
        shp[ax] //= N_SHARD
        return tuple(shp)

    def body(*refs):
        ins, outs, (send_sems, recv_sems) = refs[:nt], refs[nt:2 * nt], refs[2 * nt:]
        x, y, c = _here()
        chips = [(1 - x, y), (x, 1 - y), (1 - x, 1 - y)]
        cps = []
        for t, (s, o, ax) in enumerate(zip(ins, outs, axes)):
            w = s.shape[ax] // N_SHARD
            for k, chip in enumerate(chips):
                src = _window(s, slice(None), ax, pl.multiple_of((2 * chip[0] + chip[1]) * w, 8), w)
                cps.append(_rcopy(src, o.at[k], send_sems, recv_sems, 3 * t + k, (*chip, c)))
        for cp in cps:
            cp.start()
        for cp in cps:
            cp.wait()

    return _hbm_calls(body, ss, [jax.ShapeDtypeStruct((3,) + shard_shape(s, ax), s.dtype) for s, ax in zip(ss, axes)],
                      3 * nt, {}, name)


def _join_halves(fins, name):
    nt = len(fins)

    def body(*refs):
        outs, (send_sems, recv_sems) = refs[nt:2 * nt], refs[2 * nt:]
        x, y, c = _here()
        cps = [_rcopy(o.at[pl.ds(2 * c, 2)], o.at[pl.ds(2 * c, 2)], send_sems, recv_sems, t, (x, y, 1 - c))
               for t, o in enumerate(outs)]
        for cp in cps:
            cp.start()
        for t, o in enumerate(outs):
            _rcopy(o.at[pl.ds(2 * (1 - c), 2)], o.at[pl.ds(2 * (1 - c), 2)], send_sems, recv_sems, t, (x, y, 1 - c)).wait_recv()
        for cp in cps:
            cp.wait_send()

    return _hbm_calls(body, fins, [jax.ShapeDtypeStruct(f.shape, f.dtype) for f in fins], nt, {t: t for t in range(nt)}, name)


def _all_gather_small(buf, name):
    def body(_, out_ref, send_sems, recv_sems):
        x, y, c = _here()
        me = 4 * x + 2 * y + c
        cps = []
        for k in range(1, 8):
            to = (x ^ ((k >> 2) & 1), y ^ ((k >> 1) & 1), c ^ (k & 1))
            peer = 4 * to[0] + 2 * to[1] + to[2]
            cps.append((_rcopy(out_ref.at[me], out_ref.at[me], send_sems, recv_sems, k - 1, to),
                        _rcopy(out_ref.at[peer], out_ref.at[peer], send_sems, recv_sems, k - 1, to)))
        for snd, _ in cps:
            snd.start()
        for _, rcv in cps:
            rcv.wait_recv()
        for snd, _ in cps:
            snd.wait_send()

    return _hbm_calls(body, [buf], [jax.ShapeDtypeStruct(buf.shape, buf.dtype)], 7, {0: 0}, name)[0]


def _place(block, n, index):
    buf = jnp.zeros((n,) + block.shape[1:], block.dtype)
    return lax.dynamic_update_slice_in_dim(buf, block, index, axis=0)


def _add_half(g, other, cidx, name):
    _, R, C = g.shape
    tr = _pick(R, tuple(t for t in (512, 256, 128, 64, 32, 16, 8) if t * C * 4 <= (1 << 21)))

    def body(c_ref, g_ref, o_ref, s_ref, sb_ref):
        s = g_ref[...] + o_ref[...]
        s_ref[...] = s
        sb_ref[...] = s.astype(BF16)

    blk = pl.BlockSpec((None, tr, C), lambda l, i, c: (l, i, 0))
    return pl.pallas_call(
        body,
        grid_spec=pltpu.PrefetchScalarGridSpec(
            num_scalar_prefetch=1, grid=(2, R // tr),
            in_specs=[pl.BlockSpec((None, tr, C), lambda l, i, c: (2 * c[0] + l, i, 0)), blk], out_specs=[blk, blk]),
        out_shape=[jax.ShapeDtypeStruct((2, R, C), F32), jax.ShapeDtypeStruct((2, R, C), BF16)],
        compiler_params=_cp("parallel", "parallel"), name=name)(cidx, g, other)


def _add_chips(s, recv, axis, chip_idx, name):
    _, _, r, cw = recv.shape
    tr = _pick(r, tuple(t for t in (704, 512, 256, 128, 64, 32, 16, 8) if t * cw * 4 <= (1 << 21)))
    nr = r // tr

    def body(c_ref, s_ref, r_ref, out_ref):
        out_ref[...] = ((s_ref[...] + r_ref[0].astype(F32)) + r_ref[1].astype(F32)) + r_ref[2].astype(F32)

    if axis == 2:
        s_map = lambda l, i, c: (l, i, c[0])
    else:
        s_map = lambda l, i, c: (l, c[0] * nr + i, 0)
    return pl.pallas_call(
        body,
        grid_spec=pltpu.PrefetchScalarGridSpec(
            num_scalar_prefetch=1, grid=(2, nr),
            in_specs=[pl.BlockSpec((None, tr, cw), s_map), pl.BlockSpec((3, None, tr, cw), lambda l, i, c: (0, l, i, 0))],
            out_specs=pl.BlockSpec((None, tr, cw), lambda l, i, c: (l, i, 0))),
        out_shape=jax.ShapeDtypeStruct((2, r, cw), F32), compiler_params=_cp("parallel", "parallel"), name=name)(chip_idx, s, recv)


IN_SHARD = IN_COLS // N_SHARD
IN_SLOT = INP // N_SHARD
IN_PIECES = ((0, 512, QA), (512, 640, KA), (640, 768, VA), (768, 1280, QF), (1280, 1792, KF), (1792, 2304, VF),
             (2304, 2312, FL), (2312, 2824, XC), (2824, 3336, YC), (3336, 6408, GT))


def _gathered_to_kernel_cols(w):
    parts, pos = [], 0
    for a, b, k in sorted(IN_PIECES, key=lambda p: p[2]):
        assert k == pos
        while a < b:
            j = a // IN_SHARD
            e = min(b, (j + 1) * IN_SHARD)
            g = j * IN_SLOT + a - j * IN_SHARD
            parts.append(w[..., g:g + e - a])
            pos += e - a
            a = e
    parts.append(jnp.zeros(w.shape[:-1] + (INP - pos,), w.dtype))
    return jnp.concatenate(parts, axis=-1)


def _kernel_to_gathered_cols(w):
    parts = []
    for j in range(N_SHARD):
        lo, hi = j * IN_SHARD, (j + 1) * IN_SHARD
        for a, b, k in IN_PIECES:
            s, e = max(a, lo), min(b, hi)
            if s < e:
                parts.append(w[..., k + s - a:k + e - a])
        parts.append(jnp.zeros(w.shape[:-1] + (IN_SLOT - IN_SHARD,), w.dtype))
    return jnp.concatenate(parts, axis=-1)


def _pair_blocks(w):
    z = jnp.zeros((4, 64, 64), w.dtype)
    w = w.reshape(4, 2, 64, 64)
    top = jnp.concatenate([w[:, 0], z], axis=2)
    bot = jnp.concatenate([z, w[:, 1]], axis=2)
    return jnp.concatenate([top, bot], axis=1)


def _unpair_blocks(w):
    return jnp.stack([w[:, :64, :64], w[:, 64:, 64:]], axis=1).reshape(8, 64, 64)


BIG = ("w_in", "w_branch", "w_out", "w_ffn_in", "w_ffn_out")
TINY = ("conv_w", "meta_tokens")
SMALL = ("rel_bias_table", "norm_mix", "swa_sinks", "fox_forget_bias", "conv_b", "lru_w_r", "lru_b_r", "lru_w_i",
         "lru_b_i", "lru_lambda", "norm_ffn", "norm_final")
SHARD_AXIS = {"conv_w": 2, "meta_tokens": 1}
BIG_AXIS = {"w_in": 2, "w_branch": 2, "w_out": 1, "w_ffn_in": 2, "w_ffn_out": 1}


def _pack(d, names):
    flat = jnp.concatenate([d[n].reshape(-1) for n in names])
    pad = (-flat.shape[0]) % (256 * 128)
    return jnp.concatenate([flat, jnp.zeros((pad,), F32)]).reshape(-1, 128)


def _unpack(buf, names, shapes):
    flat, out, off = buf.reshape(-1), {}, 0
    for n in names:
        sz = int(np.prod(shapes[n]))
        out[n] = flat[off:off + sz].reshape(shapes[n])
        off += sz
    return out


def _local_step(x, tgt, W):
    S = x.shape[0]
    T = S + BLK
    tm = _pick(T, (1408, 384, 128))
    bucket = jnp.asarray(_bucket_table())
    bias = _bias_build(W["rel_bias_table"], bucket, "bias_build")
    h = jnp.concatenate([jnp.zeros((NPAD, D), F32), W["meta_tokens"], x], axis=0)

    saved = []
    for l in range(DEPTH):
        sv = {"h0": h}
        u = _rms_fwd(h, W["norm_mix"][l], f"rms_mix_fwd")
        proj = _mm(u, W["w_in"][l], tm=tm, tn=512, tk=D, name="mm_in_fwd")
        oa = _swa_fwd(proj, bias, W["swa_sinks"][l], "swa_fwd")
        z3 = _to_blocks(proj[:, FL:FL + NH].T)
        fb = W["fox_forget_bias"][l].reshape(NH, 1)
        crow = _from_blocks(_cum_fwd(z3, fb, "cum_fwd"))
        ccol = jnp.pad(crow.T, ((0, 0), (0, BLK - NH)))
        of, lse = _fox_fwd(proj, crow, ccol, "fox_fwd")
        lru_vec = jnp.concatenate([W["lru_b_r"][l][None], W["lru_b_i"][l][None], W["lru_lambda"][l][None],
                                   W["conv_b"][l][None], jnp.zeros((4, LW), F32)], axis=0)
        oc, hs = _lru_fwd(proj, W["conv_w"][l], W["lru_w_r"][l], W["lru_w_i"][l], lru_vec, "lru_fwd")
        *bs, merged = _branch_merge_fwd(oa, of, oc, W["w_branch"][l], proj, "branch_merge_fwd")
        h2 = _mm(merged, W["w_out"][l], res=h, tm=tm, tn=512, tk=D, name="mm_out_fwd")
        u2 = _rms_fwd(h2, W["norm_ffn"][l], "rms_ffn_fwd")
        gate, up, act = _ffn_in_swiglu_fwd(u2, W["w_ffn_in"][l], "ffn_in_swiglu_fwd")
        h = _mm(act, W["w_ffn_out"][l], res=h2, tm=tm, tn=512, tk=_pick(DFF, (1408, 256)), name="mm_ffn_out_fwd")
        sv.update(u=u, proj=proj, oa=oa, of=of, oc=oc, lse=lse, hs=hs, z3=z3, fb=fb, crow=crow, ccol=ccol, lru_vec=lru_vec,
                  bs=bs, merged=merged, h2=h2, u2=u2, gate=gate, up=up, act=act)
        saved.append(sv)

    tgt_pad = tgt
    dh, dhb, dg_final, loss_vec = _loss_head(h, tgt_pad, W["norm_final"], "loss_head")
    loss = loss_vec[0, 0]

    small = ("norm_mix", "swa_sinks", "fox_forget_bias", "conv_w", "conv_b", "lru_w_r", "lru_b_r", "lru_w_i", "lru_b_i",
             "lru_lambda", "norm_ffn")
    G = {n: [None] * DEPTH for n in small}
    G["norm_final"] = dg_final.reshape(D)
    GW = {n: None for n in BIG}
    dbias = jnp.zeros((NH, BLK, 2 * BLK), F32)
    tkT = _rt(T)
    for l in reversed(range(DEPTH)):
        sv = saved[l]
        GW["w_ffn_out"] = _mm(_transpose(sv["act"], "tr_act"), dhb, tm=_pick(DFF, (1408, 256)), tn=D, tk=tkT,
                              slab=(GW["w_ffn_out"], l, DEPTH), name="mm_ffn_out_dw")
        dgate, dup = _ffn_out_dx_swiglu_bwd(dhb, W["w_ffn_out"][l], sv["gate"], sv["up"], "ffn_out_dx_swiglu_bwd")
        u2t = _transpose(sv["u2"], "tr_u2")
        du2 = None
        for half, dpart in enumerate((dgate, dup)):
            GW["w_ffn_in"] = _mm(u2t, dpart, tm=D, tn=_pick(DFF, (1408, 256)), tk=tkT, slab=(GW["w_ffn_in"], l, DEPTH),
                                 col0=half * DFF, cols=2 * DFF, name="mm_ffn_in_dw")
            du2 = _mm(dpart, W["w_ffn_in"][l], tb=True, res=du2, b_k0=half * DFF, tm=tm, tn=512,
                      tk=_pick(DFF, (1408, 256)), name="mm_ffn_in_dx")
        dh, dhb, dgn = _rms_bwd(du2, sv["h2"], W["norm_ffn"][l], dh, "rms_ffn_bwd")
        G["norm_ffn"][l] = dgn.reshape(D)
        GW["w_out"] = _mm(_transpose(sv["merged"], "tr_merged"), dhb, tm=D, tn=D, tk=tkT,
                          slab=(GW["w_out"], l, DEPTH), name="mm_out_dw")
        db0, db1, db2, dg0, dg1, dg2 = _out_dx_merge_bwd(dhb, W["w_out"][l], sv["proj"], *sv["bs"], "out_dx_merge_bwd")
        dos = []
        for g, (o, db) in enumerate(zip((sv["oa"], sv["of"], sv["oc"]), (db0, db1, db2))):
            GW["w_branch"] = _mm(_transpose(o, "tr_branch"), db, tm=LW, tn=D, tk=tkT,
                                 slab=(GW["w_branch"], 3 * l + g, 3 * DEPTH), name="mm_branch_dw")
            dos.append(_mm(db, W["w_branch"][l, g], tb=True, out_dtype=BF16, tm=tm, tn=LW, tk=D, name="mm_branch_dx"))
        dqa, dkb, dvb, dbias, dsk = _swa_bwd(sv["proj"], bias, W["swa_sinks"][l], dos[0], dbias, "swa_bwd")
        dka, dva = _band_fold(dkb, dvb, "swa_band_fold")
        G["swa_sinks"][l] = dsk[0, :NH]
        dqf, dcq = _fox_dq(sv["proj"], sv["crow"], sv["ccol"], sv["of"], sv["lse"], dos[1], "fox_dq")
        dkf, dvf, dcp = _fox_dkv(sv["proj"], sv["crow"], sv["ccol"], sv["of"], sv["lse"], dos[1], "fox_dkv")
        dck3 = _to_blocks(dcp[:, :2, :].reshape(NH, T))
        dcq3 = _to_blocks(dcq[:, ::64].T)
        dz3, dfb = _cum_bwd(dck3, dcq3, sv["z3"], sv["fb"], "cum_bwd")
        G["fox_forget_bias"][l] = dfb[:, 0]
        dfl = jnp.pad(_from_blocks(dz3).T, ((0, 0), (0, INP - FL - NH))).astype(BF16)
        dxc, dyc, dwr, dwi, dvec = _lru_bwd(sv["proj"], sv["hs"], dos[2], W["conv_w"][l], W["lru_w_r"][l], W["lru_w_i"][l],
                                            sv["lru_vec"], "lru_bwd")
        G["lru_w_r"][l], G["lru_w_i"][l] = _unpair_blocks(dwr), _unpair_blocks(dwi)
        G["lru_b_r"][l], G["lru_b_i"][l], G["lru_lambda"][l], G["conv_b"][l] = dvec[0], dvec[1], dvec[2], dvec[3]
        G["conv_w"][l] = dvec[4:8]
        dproj = jnp.concatenate([dqa, dqf, dkf, dvf, dxc, dyc, dg0, dg1, dg2, dka, dva, dfl], axis=1)
        GW["w_in"] = _mm(_transpose(sv["u"], "tr_u"), dproj, tm=D, tn=IN_SLOT, tk=tkT,
                         slab=(GW["w_in"], l, DEPTH), name="mm_in_dw")
        du = _mm(dproj, W["w_in"][l], tb=True, tm=tm, tn=512, tk=_pick(INP, (1664, 512)), name="mm_in_dx")
        dh, dhb, dgn = _rms_bwd(du, sv["h0"], W["norm_mix"][l], dh, "rms_mix_bwd")
        G["norm_mix"][l] = dgn.reshape(D)

    grads = {n: (jnp.stack(v) if isinstance(v, list) else v) for n, v in G.items()}
    grads.update(GW)
    grads["w_branch"] = GW["w_branch"].reshape(DEPTH, 3 * LW, D)
    grads["rel_bias_table"] = _bias_bwd(dbias, bucket, "bias_bwd")
    grads["meta_tokens"] = dh[NPAD:BLK]
    return loss, dh[BLK:], grads


NAMES = ("meta_tokens", "rel_bias_table", "norm_mix", "w_in", "swa_sinks", "fox_forget_bias", "conv_w", "conv_b",
         "lru_w_r", "lru_b_r", "lru_w_i", "lru_b_i", "lru_lambda", "w_branch", "w_out", "norm_ffn", "w_ffn_in",
         "w_ffn_out", "norm_final")


def _three_d(n, a):
    return a.reshape(DEPTH, 3 * LW, -1) if n == "w_branch" else a


def _gather_weights(P):
    x, y, c = _here()
    mine, me = 2 * x + y, 4 * x + 2 * y + c
    fulls = []
    for n in BIG:
        shard = _three_d(n, P[n].astype(BF16))
        if n == "w_in":
            shard = jnp.pad(shard, ((0, 0), (0, 0), (0, IN_SLOT - IN_SHARD)))
        ax = BIG_AXIS[n]
        shape = list(shard.shape)
        shape[ax] *= N_SHARD
        fulls.append(lax.dynamic_update_slice_in_dim(jnp.zeros(shape, BF16), shard, mine * shard.shape[ax], axis=ax))
    full = dict(zip(BIG, _all_gather_weights(fulls, [BIG_AXIS[n] for n in BIG], "ag_weights")))
    full["w_in"] = _gathered_to_kernel_cols(full["w_in"])
    full["w_branch"] = full["w_branch"].reshape(DEPTH, 3, LW, D)
    tiny = _all_gather_small(_place(_pack(P, TINY)[None], 8, me), "ag_tiny_weights")
    parts = [_unpack(tiny[2 * j], TINY, {n: P[n].shape for n in TINY}) for j in range(N_SHARD)]
    for n in TINY:
        full[n] = jnp.concatenate([p[n] for p in parts], axis=SHARD_AXIS[n])
    for n in SMALL:
        full[n] = P[n]
    full["lru_w_r"] = jnp.stack([_pair_blocks(P["lru_w_r"][l]) for l in range(DEPTH)]).astype(BF16)
    full["lru_w_i"] = jnp.stack([_pair_blocks(P["lru_w_i"][l]) for l in range(DEPTH)]).astype(BF16)
    return full


def _reduce_grads(grads, P):
    x, y, c = _here()
    mine, me = 2 * x + y, 4 * x + 2 * y + c
    cidx = jnp.reshape(c, (1,)).astype(jnp.int32)
    chip = jnp.reshape(mine, (1,)).astype(jnp.int32)
    axes = [BIG_AXIS[n] for n in BIG]
    gs = [grads[n] for n in BIG]
    pairs = [_add_half(g, r, cidx, "rs_add_half_" + n) for n, g, r in zip(BIG, gs, _swap_halves(gs, "rs_swap_halves"))]
    ss, sbs = [list(t) for t in zip(*pairs)]
    ss[0], sbs[0] = _kernel_to_gathered_cols(ss[0]), _kernel_to_gathered_cols(sbs[0])
    recv = _scatter_to_chips(sbs, axes, "rs_scatter")
    tots = [_add_chips(s, r, ax, chip, "rs_add_chips_" + n) for n, s, r, ax in zip(BIG, ss, recv, axes)]
    fins = dict(zip(BIG, _join_halves([_place(t, DEPTH, 2 * c) for t in tots], "rs_join_halves")))
    out = {n: fins[n].reshape(P[n].shape) for n in BIG if n != "w_in"}
    out["w_in"] = fins["w_in"][:, :, :IN_SHARD]
    names = SMALL + TINY
    gathered = _all_gather_small(_place(_pack(grads, names)[None], 8, me), "ag_small_grads")
    small = _unpack(_sum_lead(gathered, "sum_small_grads"), names, {n: grads[n].shape for n in names})
    for n in SMALL:
        out[n] = small[n]
    for n in TINY:
        w = P[n].shape[SHARD_AXIS[n]]
        out[n] = lax.dynamic_slice_in_dim(small[n], mine * w, w, axis=SHARD_AXIS[n])
    return out


def _update(P, Gd, M, V):
    delta, new_m, new_v = {}, {}, {}
    for n in BIG + TINY:
        shp = P[n].shape
        two = (int(np.prod(shp[:-1])), shp[-1])
        d, m, v = _adamw(P[n].reshape(two), Gd[n].reshape(two), M[n].reshape(two), V[n].reshape(two), "adamw_" + n)
        delta[n], new_m[n], new_v[n] = d.reshape(shp), m.reshape(shp), v.reshape(shp)
    shapes = {n: P[n].shape for n in SMALL}
    d, m, v = _adamw(_pack(P, SMALL), _pack(Gd, SMALL), _pack(M, SMALL), _pack(V, SMALL), "adamw_small")
    for dst, buf in ((delta, d), (new_m, m), (new_v, v)):
        dst.update(_unpack(buf, SMALL, shapes))
    return delta, new_m, new_v


def kernel(x, meta_tokens, rel_bias_table, norm_mix, w_in, swa_sinks, fox_forget_bias, conv_w, conv_b, lru_w_r, lru_b_r, lru_w_i, lru_b_i, lru_lambda, w_branch, w_out, norm_ffn, w_ffn_in, w_ffn_out, norm_final, loss_target, m_meta_tokens, m_rel_bias_table, m_norm_mix, m_w_in, m_swa_sinks, m_fox_forget_bias, m_conv_w, m_conv_b, m_lru_w_r, m_lru_b_r, m_lru_w_i, m_lru_b_i, m_lru_lambda, m_w_branch, m_w_out, m_norm_ffn, m_w_ffn_in, m_w_ffn_out, m_norm_final, v_meta_tokens, v_rel_bias_table, v_norm_mix, v_w_in, v_swa_sinks, v_fox_forget_bias, v_conv_w, v_conv_b, v_lru_w_r, v_lru_b_r, v_lru_w_i, v_lru_b_i, v_lru_lambda, v_w_branch, v_w_out, v_norm_ffn, v_w_ffn_in, v_w_ffn_out, v_norm_final):
    P = dict(zip(NAMES, (meta_tokens, rel_bias_table, norm_mix, w_in, swa_sinks, fox_forget_bias, conv_w, conv_b, lru_w_r,
                         lru_b_r, lru_w_i, lru_b_i, lru_lambda, w_branch, w_out, norm_ffn, w_ffn_in, w_ffn_out, norm_final)))
    M = dict(zip(NAMES, (m_meta_tokens, m_rel_bias_table, m_norm_mix, m_w_in, m_swa_sinks, m_fox_forget_bias, m_conv_w,
                         m_conv_b, m_lru_w_r, m_lru_b_r, m_lru_w_i, m_lru_b_i, m_lru_lambda, m_w_branch, m_w_out, m_norm_ffn,
                         m_w_ffn_in, m_w_ffn_out, m_norm_final)))
    V = dict(zip(NAMES, (v_meta_tokens, v_rel_bias_table, v_norm_mix, v_w_in, v_swa_sinks, v_fox_forget_bias, v_conv_w,
                         v_conv_b, v_lru_w_r, v_lru_b_r, v_lru_w_i, v_lru_b_i, v_lru_lambda, v_w_branch, v_w_out, v_norm_ffn,
                         v_w_ffn_in, v_w_ffn_out, v_norm_final)))
    W = _gather_weights(P)
    loss_local, grad_x, grads = _local_step(x[0], loss_target[0], W)
    loss = lax.psum(loss_local, ("x", "y", "c"))
    Gd = _reduce_grads(grads, P)
    delta, new_m, new_v = _update(P, Gd, M, V)
    return (loss, grad_x[None], *[Gd[n] for n in NAMES], *[delta[n] for n in NAMES],
            *[new_m[n] for n in NAMES], *[new_v[n] for n in NAMES])
```

```python
import functools
import math

import numpy as np
import jax
import jax.numpy as jnp
from jax import lax
from jax.experimental import pallas as pl
from jax.experimental.pallas import tpu as pltpu

F32, BF16 = jnp.float32, jnp.bfloat16
MESH = pl.DeviceIdType.MESH
ANY = pl.BlockSpec(memory_space=pl.ANY)
SMEM = pl.BlockSpec(memory_space=pltpu.SMEM)

D = 1024
DEPTH = 4
BLK = 128
N_META = 16
NPAD = 112
NH = 8
LW = 512
DFF = 2816
EPS = 1e-6
NEG = -1e30
SCALE = 0.125
LRU_C = 8.0
REL_BUCKETS = 32
N_SHARD = 4
QA, QF, KF, VF, XC, YC, GT, KA, VA, FL, INP = 0, 512, 1024, 1536, 2048, 2560, 3072, 6144, 6272, 6400, 6656
IN_COLS = 6408
VMEM_LIMIT = 48 * 1024 * 1024

ADAM_LR, ADAM_B1, ADAM_B2, ADAM_EPS, ADAM_WD, ADAM_STEP = 0.001, 0.9, 0.999, 1e-08, 0.01, 10


def _cp(*sem):
    return pltpu.CompilerParams(dimension_semantics=sem or None, vmem_limit_bytes=VMEM_LIMIT)


def _pick(n, prefs):
    for p in prefs:
        if n % p == 0:
            return p
    return n


def _rt(T):
    return _pick(T, (384, 128))


def _sigmoid(z):
    return 1.0 / (1.0 + jnp.exp(-z))


def _log_sigmoid(z):
    return jnp.minimum(z, 0.0) - jnp.log(1.0 + jnp.exp(-jnp.abs(z)))


def _gelu(y):
    c = math.sqrt(2.0 / math.pi)
    return 0.5 * y * (1.0 + jnp.tanh(c * (y + 0.044715 * y * y * y)))


def _gelu_grad(y):
    c = math.sqrt(2.0 / math.pi)
    t = jnp.tanh(c * (y + 0.044715 * y * y * y))
    return 0.5 * (1.0 + t) + 0.5 * y * (1.0 - t * t) * c * (1.0 + 3.0 * 0.044715 * y * y)


def _neg_expm1(z):
    series = -z * (1.0 + z * (0.5 + z * (1.0 / 6.0 + z * (1.0 / 24.0 + z * (1.0 / 120.0)))))
    return jnp.where(z > -0.1, series, 1.0 - jnp.exp(z))


def _dot(a, b, ca, cb):
    return lax.dot_general(a, b, (((ca,), (cb,)), ((), ())), preferred_element_type=F32)


def _mm(a, b, *, ta=False, tb=False, res=None, out_dtype=F32, tm, tn, tk, name, slab=None, b_k0=0, col0=0, cols=None):
    M, K = (a.shape[1], a.shape[0]) if ta else a.shape
    N = b.shape[0] if tb else b.shape[1]
    assert (b.shape[1] if tb else b.shape[0]) >= K + b_k0 and M % tm == 0 and N % tn == 0 and K % tk == 0, (name, a.shape, b.shape)
    assert b_k0 % tk == 0 and col0 % tn == 0
    nk, kb, jb = K // tk, b_k0 // tk, col0 // tn
    ca, cb = (0 if ta else 1), (1 if tb else 0)
    n_in = 2 + (res is not None) + (slab is not None and slab[0] is not None)

    def body(*refs):
        a_ref, b_ref = refs[:2]
        r_ref = refs[2] if res is not None else None
        o_ref = refs[n_in]
        part = _dot(a_ref[...].astype(BF16), b_ref[...].astype(BF16), ca, cb)

        def fin(acc):
            if res is not None:
                acc = acc + r_ref[...]
            o_ref[...] = acc.astype(out_dtype)

        if nk == 1:
            fin(part)
        else:
            acc_ref = refs[-1]
            k = pl.program_id(2)

            @pl.when(k == 0)
            def _():
                acc_ref[...] = part

            @pl.when(k > 0)
            def _():
                acc_ref[...] += part

            @pl.when(k == nk - 1)
            def _():
                fin(acc_ref[...])

    a_spec = pl.BlockSpec((tk, tm), lambda i, j, k: (k, i)) if ta else pl.BlockSpec((tm, tk), lambda i, j, k: (i, k))
    b_spec = (pl.BlockSpec((tn, tk), lambda i, j, k: (j, k + kb)) if tb
              else pl.BlockSpec((tk, tn), lambda i, j, k: (k + kb, j)))
    o_spec = pl.BlockSpec((tm, tn), lambda i, j, k: (i, j))
    in_specs, ops = [a_spec, b_spec], [a, b]
    if res is not None:
        in_specs.append(o_spec)
        ops.append(res)
    out_shape, aliases = jax.ShapeDtypeStruct((M, N), out_dtype), {}
    if slab is not None:
        buf, idx, n = slab
        o_spec = pl.BlockSpec((None, tm, tn), lambda i, j, k: (idx, i, j + jb))
        out_shape = jax.ShapeDtypeStruct((n, M, cols or N), out_dtype)
        if buf is not None:
            aliases = {len(ops): 0}
            in_specs.append(ANY)
            ops.append(buf)
    return pl.pallas_call(
        body, grid=(M // tm, N // tn, nk), in_specs=in_specs, out_specs=o_spec, out_shape=out_shape,
        input_output_aliases=aliases, scratch_shapes=[pltpu.VMEM((tm, tn), F32)] if nk > 1 else [],
        compiler_params=_cp("parallel", "parallel", "arbitrary"), name=name)(*ops)


def _transpose(x, name):
    T, C = x.shape
    tr, tc = _rt(T), _pick(C, (512, 256, 128))

    def body(x_ref, o_ref):
        o_ref[...] = x_ref[...].T

    return pl.pallas_call(
        body, grid=(T // tr, C // tc), in_specs=[pl.BlockSpec((tr, tc), lambda i, j: (i, j))],
        out_specs=pl.BlockSpec((tc, tr), lambda i, j: (j, i)), out_shape=jax.ShapeDtypeStruct((C, T), x.dtype),
        compiler_params=_cp("parallel", "parallel"), name=name)(x)


def _rms_fwd(h, g, name):
    T = h.shape[0]
    tr = _rt(T)

    def body(h_ref, g_ref, u_ref):
        x = h_ref[...]
        r = lax.rsqrt(jnp.mean(x * x, axis=-1, keepdims=True) + EPS)
        u_ref[...] = (x * r * g_ref[...]).astype(BF16)

    return pl.pallas_call(
        body, grid=(T // tr,),
        in_specs=[pl.BlockSpec((tr, D), lambda i: (i, 0)), pl.BlockSpec((1, D), lambda i: (0, 0))],
        out_specs=pl.BlockSpec((tr, D), lambda i: (i, 0)), out_shape=jax.ShapeDtypeStruct((T, D), BF16),
        compiler_params=_cp("parallel"), name=name)(h, g.reshape(1, D))


def _rms_bwd(du, h, g, dres, name):
    T = h.shape[0]
    tr = _rt(T)

    def body(du_ref, h_ref, g_ref, dres_ref, dh_ref, dhb_ref, dg_ref):
        x = h_ref[...]
        r = lax.rsqrt(jnp.mean(x * x, axis=-1, keepdims=True) + EPS)
        xh = x * r
        dy = du_ref[...]
        dxh = dy * g_ref[...]
        dx = r * (dxh - xh * jnp.mean(dxh * xh, axis=-1, keepdims=True))
        dh = dres_ref[...] + dx
        dh_ref[...] = dh
        dhb_ref[...] = dh.astype(BF16)
        part = jnp.sum(dy * xh, axis=0, keepdims=True)

        @pl.when(pl.program_id(0) == 0)
        def _():
            dg_ref[...] = part

        @pl.when(pl.program_id(0) > 0)
        def _():
            dg_ref[...] += part

    row = pl.BlockSpec((tr, D), lambda i: (i, 0))
    vec = pl.BlockSpec((1, D), lambda i: (0, 0))
    return pl.pallas_call(
        body, grid=(T // tr,), in_specs=[row, row, vec, row], out_specs=[row, row, vec],
        out_shape=[jax.ShapeDtypeStruct((T, D), F32), jax.ShapeDtypeStruct((T, D), BF16), jax.ShapeDtypeStruct((1, D), F32)],
        compiler_params=_cp("arbitrary"), name=name)(du, h, g.reshape(1, D), dres)


def _loss_head(h, tgt, g, name):
    T = h.shape[0]
    nb = T // BLK

    def body(h_ref, t_ref, g_ref, dh_ref, dhb_ref, dg_ref, loss_ref):
        i = pl.program_id(0)
        x = h_ref[...]
        r = lax.rsqrt(jnp.mean(x * x, axis=-1, keepdims=True) + EPS)
        xh = x * r
        gv = g_ref[...]
        tok = i >= 1
        err = jnp.where(tok, xh * gv - t_ref[...], 0.0)
        dy = err * (1.0 / D)
        dxh = dy * gv
        dx = r * (dxh - xh * jnp.mean(dxh * xh, axis=-1, keepdims=True))
        dh_ref[...] = dx
        dhb_ref[...] = dx.astype(BF16)
        dg = jnp.sum(dy * xh, axis=0, keepdims=True)
        ls = jnp.zeros((1, BLK), F32) + jnp.sum(err * err) * (0.5 / D)

        @pl.when(i == 0)
        def _():
            dg_ref[...] = dg
            loss_ref[...] = ls

        @pl.when(i > 0)
        def _():
            dg_ref[...] += dg
            loss_ref[...] += ls

    row = pl.BlockSpec((BLK, D), lambda i: (i, 0))
    vec = pl.BlockSpec((1, D), lambda i: (0, 0))
    return pl.pallas_call(
        body, grid=(nb,),
        in_specs=[row, pl.BlockSpec((BLK, D), lambda i: (jnp.maximum(i - 1, 0), 0)), vec],
        out_specs=[row, row, vec, pl.BlockSpec((1, BLK), lambda i: (0, 0))],
        out_shape=[jax.ShapeDtypeStruct((T, D), F32), jax.ShapeDtypeStruct((T, D), BF16),
                   jax.ShapeDtypeStruct((1, D), F32), jax.ShapeDtypeStruct((1, BLK), F32)],
        compiler_params=_cp("arbitrary"), name=name)(h, tgt, g.reshape(1, D))


def _bucket_table():
    q = np.arange(BLK)[:, None]
    k = np.arange(2 * BLK)[None, :]
    d = np.maximum(q + BLK - k, 0)
    max_exact = REL_BUCKETS // 2
    scaled = np.log(np.maximum(d, 1).astype(np.float32) / np.float32(max_exact)) / np.float32(math.log(128 / max_exact))
    large = np.minimum(max_exact + (scaled.astype(np.float32) * (REL_BUCKETS - max_exact)).astype(np.int32), REL_BUCKETS - 1)
    return np.where(d < max_exact, d, large).astype(np.int32)


def _bias_build(table, bucket, name):
    def body(t_ref, bk_ref, o_ref):
        bk = bk_ref[...]
        for h in range(NH):
            acc = jnp.zeros((BLK, 2 * BLK), F32)
            for b in range(REL_BUCKETS):
                acc = jnp.where(bk == b, t_ref[b, h], acc)
            o_ref[h] = acc

    return pl.pallas_call(
        body, in_specs=[SMEM, pl.BlockSpec(memory_space=pltpu.VMEM)], out_specs=pl.BlockSpec(memory_space=pltpu.VMEM),
        out_shape=jax.ShapeDtypeStruct((NH, BLK, 2 * BLK), F32), compiler_params=_cp(), name=name)(table, bucket)


def _bias_bwd(dbias, bucket, name):
    def body(d_ref, bk_ref, o_ref):
        bk = bk_ref[...]
        for h in range(NH):
            dh = d_ref[h]
            for b in range(REL_BUCKETS):
                o_ref[b, h] = jnp.sum(jnp.where(bk == b, dh, 0.0))

    return pl.pallas_call(
        body, in_specs=[pl.BlockSpec(memory_space=pltpu.VMEM)] * 2, out_specs=SMEM,
        out_shape=jax.ShapeDtypeStruct((REL_BUCKETS, NH), F32), compiler_params=_cp(), name=name)(dbias, bucket)


def _swa_specs(nq_cols):
    prev = lambda n: jnp.maximum(n - 1, 0)
    return [
        pl.BlockSpec((BLK, nq_cols), lambda n: (n, QA // nq_cols)),
        pl.BlockSpec((BLK, BLK), lambda n: (prev(n), KA // BLK)), pl.BlockSpec((BLK, BLK), lambda n: (n, KA // BLK)),
        pl.BlockSpec((BLK, BLK), lambda n: (prev(n), VA // BLK)), pl.BlockSpec((BLK, BLK), lambda n: (n, VA // BLK)),
    ]


def _swa_mask(n):
    row = lax.broadcasted_iota(jnp.int32, (BLK, 2 * BLK), 0)
    col = lax.broadcasted_iota(jnp.int32, (BLK, 2 * BLK), 1)
    dist = row + BLK - col
    return (dist >= 0) & (dist < BLK) & ((n - 1) * BLK + col >= NPAD)


def _swa_probs(qm, ksel, mask, bias_h, sink):
    s = _dot(qm, ksel, 1, 1) * SCALE
    s = jnp.where(mask, s + bias_h, NEG)
    m = jnp.maximum(jnp.max(s, axis=-1, keepdims=True), sink)
    p = jnp.exp(s - m)
    psink = jnp.exp(sink - m)
    inv = 1.0 / (jnp.sum(p, axis=-1, keepdims=True) + psink)
    return p * inv, psink * inv


def _swa_fwd(proj, bias, sinks, name):
    T = proj.shape[0]
    nb = T // BLK

    def body(sk_ref, q_ref, kp_ref, kc_ref, vp_ref, vc_ref, b_ref, o_ref):
        n = pl.program_id(0)
        lo = lax.broadcasted_iota(jnp.int32, (1, BLK), 1) < 64
        kb = jnp.concatenate([kp_ref[...], kc_ref[...]], axis=0)
        vb = jnp.concatenate([vp_ref[...], vc_ref[...]], axis=0)
        kbs = (kb.astype(BF16), pltpu.roll(kb, 64, 1).astype(BF16))
        vbs = (vb, pltpu.roll(vb, 64, 1))
        mask = _swa_mask(n)
        outs = []
        for pr in range(NH // 2):
            qp = q_ref[:, pr * BLK:(pr + 1) * BLK]
            kv = pr // 2
            acc = jnp.zeros((BLK, BLK), F32)
            for e in range(2):
                lm = lo if e == 0 else jnp.logical_not(lo)
                sw = 0 if kv == e else 1
                qm = jnp.where(lm, qp, 0.0).astype(BF16)
                pn, _ = _swa_probs(qm, kbs[sw], mask, b_ref[2 * pr + e], sk_ref[2 * pr + e])
                acc = acc + _dot(pn.astype(BF16), jnp.where(lm, vbs[sw], 0.0).astype(BF16), 1, 0)
            outs.append(acc)
        o_ref[...] = jnp.concatenate(outs, axis=1).astype(BF16)

    return pl.pallas_call(
        body, grid=(nb,),
        in_specs=[SMEM] + _swa_specs(512) + [pl.BlockSpec((NH, BLK, 2 * BLK), lambda n: (0, 0, 0))],
        out_specs=pl.BlockSpec((BLK, 512), lambda n: (n, 0)), out_shape=jax.ShapeDtypeStruct((T, 512), BF16),
        compiler_params=_cp("parallel"), name=name)(sinks, proj, proj, proj, proj, proj, bias)


def _swa_bwd(proj, bias, sinks, do, dbias_in, name):
    T = proj.shape[0]
    nb = T // BLK

    def body(sk_ref, q_ref, kp_ref, kc_ref, vp_ref, vc_ref, b_ref, do_ref, dbi_ref,
             dq_ref, dk_ref, dv_ref, db_ref, dsk_ref, sk_acc):
        n = pl.program_id(0)
        lane = lax.broadcasted_iota(jnp.int32, (1, BLK), 1)
        lo = lane < 64
        kb = jnp.concatenate([kp_ref[...], kc_ref[...]], axis=0)
        vb = jnp.concatenate([vp_ref[...], vc_ref[...]], axis=0)
        kbs = (kb, pltpu.roll(kb, 64, 1))
        vbs = (vb, pltpu.roll(vb, 64, 1))
        mask = _swa_mask(n)

        @pl.when(n == 0)
        def _():
            db_ref[...] = dbi_ref[...]
            sk_acc[...] = jnp.zeros_like(sk_acc)

        dqs = []
        dk = jnp.zeros((2 * BLK, BLK), F32)
        dv = jnp.zeros((2 * BLK, BLK), F32)
        for pr in range(NH // 2):
            qp = q_ref[:, pr * BLK:(pr + 1) * BLK]
            dop = do_ref[:, pr * BLK:(pr + 1) * BLK].astype(F32)
            kv = pr // 2
            dq = jnp.zeros((BLK, BLK), F32)
            for e in range(2):
                h = 2 * pr + e
                lm = lo if e == 0 else jnp.logical_not(lo)
                sw = 0 if kv == e else 1
                qm = jnp.where(lm, qp, 0.0)
                dom = jnp.where(lm, dop, 0.0)
                pn, ps = _swa_probs(qm.astype(BF16), kbs[sw].astype(BF16), mask, b_ref[h], sk_ref[h])
                dp = _dot(dom.astype(BF16), vbs[sw].astype(BF16), 1, 1)
                delta = jnp.sum(pn * dp, axis=-1, keepdims=True)
                ds = pn * (dp - delta)
                db_ref[h] += ds
                sk_acc[...] += jnp.where(lane == h, -(ps * delta), 0.0)
                dsb = (ds * SCALE).astype(BF16)
                dq = dq + _dot(dsb, jnp.where(lm, kbs[sw], 0.0).astype(BF16), 1, 0)
                qk = qm if sw == 0 else pltpu.roll(qm, 64, 1)
                dok = dom if sw == 0 else pltpu.roll(dom, 64, 1)
                dk = dk + _dot(dsb, qk.astype(BF16), 0, 0)
                dv = dv + _dot(pn.astype(BF16), dok.astype(BF16), 0, 0)
            dqs.append(dq)
        dq_ref[...] = jnp.concatenate(dqs, axis=1).astype(BF16)
        dk_ref[0] = dk
        dv_ref[0] = dv

        @pl.when(n == nb - 1)
        def _():
            dsk_ref[...] = jnp.sum(sk_acc[...], axis=0, keepdims=True)

    full_b = pl.BlockSpec((NH, BLK, 2 * BLK), lambda n: (0, 0, 0))
    band = pl.BlockSpec((1, 2 * BLK, BLK), lambda n: (n, 0, 0))
    return pl.pallas_call(
        body, grid=(nb,),
        in_specs=[SMEM] + _swa_specs(512) + [full_b, pl.BlockSpec((BLK, 512), lambda n: (n, 0)), full_b],
        out_specs=[pl.BlockSpec((BLK, 512), lambda n: (n, 0)), band, band, full_b, pl.BlockSpec((1, BLK), lambda n: (0, 0))],
        out_shape=[jax.ShapeDtypeStruct((T, 512), BF16), jax.ShapeDtypeStruct((nb, 2 * BLK, BLK), F32),
                   jax.ShapeDtypeStruct((nb, 2 * BLK, BLK), F32), jax.ShapeDtypeStruct((NH, BLK, 2 * BLK), F32),
                   jax.ShapeDtypeStruct((1, BLK), F32)],
        scratch_shapes=[pltpu.VMEM((BLK, BLK), F32)],
        compiler_params=_cp("arbitrary"), name=name)(sinks, proj, proj, proj, proj, proj, bias, do, dbias_in)


def _band_fold(dkb, dvb, name):
    nb = dkb.shape[0]

    def body(ko_ref, kn_ref, vo_ref, vn_ref, dk_ref, dv_ref):
        last = pl.program_id(0) == nb - 1
        dk_ref[...] = (ko_ref[0] + jnp.where(last, 0.0, kn_ref[0])).astype(BF16)
        dv_ref[...] = (vo_ref[0] + jnp.where(last, 0.0, vn_ref[0])).astype(BF16)

    own = pl.BlockSpec((1, BLK, BLK), lambda j: (j, 1, 0))
    nxt = pl.BlockSpec((1, BLK, BLK), lambda j: (jnp.minimum(j + 1, nb - 1), 0, 0))
    out = pl.BlockSpec((BLK, BLK), lambda j: (j, 0))
    return pl.pallas_call(
        body, grid=(nb,), in_specs=[own, nxt, own, nxt], out_specs=[out, out],
        out_shape=[jax.ShapeDtypeStruct((nb * BLK, BLK), BF16)] * 2,
        compiler_params=_cp("parallel"), name=name)(dkb, dkb, dvb, dvb)


def _cum_fwd(z3, fb, name):
    nb = z3.shape[0]

    def body(z_ref, fb_ref, c_ref):
        lane = lax.broadcasted_iota(jnp.int32, (NH, BLK), 1)

        def step(b, carry):
            x = jnp.where(b * BLK + lane >= NPAD, _log_sigmoid(z_ref[b] + fb_ref[...]), 0.0)
            s = 1
            while s < BLK:
                x = x + jnp.where(lane >= s, pltpu.roll(x, s, 1), 0.0)
                s *= 2
            x = x + carry
            c_ref[b] = x
            return jnp.sum(jnp.where(lane == BLK - 1, x, 0.0), axis=-1, keepdims=True)

        lax.fori_loop(0, nb, step, jnp.zeros((NH, 1), F32))

    return pl.pallas_call(body, out_shape=jax.ShapeDtypeStruct((nb, NH, BLK), F32), compiler_params=_cp(), name=name)(z3, fb)


def _cum_bwd(dck3, dcq3, z3, fb, name):
    nb = z3.shape[0]

    def body(d_ref, dq_ref, z_ref, fb_ref, dz_ref, db_ref):
        lane = lax.broadcasted_iota(jnp.int32, (NH, BLK), 1)

        def step(k, carry):
            suffix, tot = carry
            b = nb - 1 - k
            x = dq_ref[b] - d_ref[b]
            s = 1
            while s < BLK:
                x = x + jnp.where(lane < BLK - s, pltpu.roll(x, BLK - s, 1), 0.0)
                s *= 2
            x = x + suffix
            dz = jnp.where(b * BLK + lane >= NPAD, x * _sigmoid(-(z_ref[b] + fb_ref[...])), 0.0)
            dz_ref[b] = dz
            return (jnp.sum(jnp.where(lane == 0, x, 0.0), axis=-1, keepdims=True),
                    tot + jnp.sum(dz, axis=-1, keepdims=True))

        z0 = jnp.zeros((NH, 1), F32)
        _, tot = lax.fori_loop(0, nb, step, (z0, z0))
        db_ref[...] = jnp.zeros((NH, BLK), F32) + tot

    return pl.pallas_call(
        body, out_shape=[jax.ShapeDtypeStruct((nb, NH, BLK), F32), jax.ShapeDtypeStruct((NH, BLK), F32)],
        compiler_params=_cp(), name=name)(dck3, dcq3, z3, fb)


def _to_blocks(a):
    return a.reshape(NH, -1, BLK).transpose(1, 0, 2)


def _from_blocks(a):
    return a.transpose(1, 0, 2).reshape(NH, -1)


def _fox_prep(proj, ccol, name):
    T = proj.shape[0]
    tr = _rt(T)

    def body(q_ref, k_ref, v_ref, cc_ref, qa_ref, ka_ref, vm_ref):
        h = pl.program_id(1)
        lane = lax.broadcasted_iota(jnp.int32, (1, BLK), 1)
        own = (lane >> 6) == (h & 1)
        a0 = 64 * (1 - (h & 1))
        c = _lane_pick(cc_ref[...], lane, h)
        hi = c.astype(BF16).astype(F32)
        mid = (c - hi).astype(BF16).astype(F32)
        lo = (c - hi - mid).astype(BF16).astype(F32)
        ones = (lane >= a0 + 3) & (lane < a0 + 6)
        qa = jnp.where(own, q_ref[...] * SCALE, jnp.where(ones, 1.0, 0.0))
        qa = jnp.where(lane == a0, hi, jnp.where(lane == a0 + 1, mid, jnp.where(lane == a0 + 2, lo, qa)))
        ones = (lane >= a0) & (lane < a0 + 3)
        ka = jnp.where(own, k_ref[...], jnp.where(ones, 1.0, 0.0))
        ka = jnp.where(lane == a0 + 3, -hi, jnp.where(lane == a0 + 4, -mid, jnp.where(lane == a0 + 5, -lo, ka)))
        qa_ref[...] = qa.astype(BF16)
        ka_ref[...] = ka.astype(BF16)
        vm_ref[...] = jnp.where(own, v_ref[...], 0.0).astype(BF16)

    pair = lambda col0: pl.BlockSpec((tr, BLK), lambda i, h: (i, col0 // BLK + (h >> 1)))
    out = pl.BlockSpec((None, tr, BLK), lambda i, h: (h, i, 0))
    return pl.pallas_call(
        body, grid=(T // tr, NH), in_specs=[pair(QF), pair(KF), pair(VF), pl.BlockSpec((tr, BLK), lambda i, h: (i, 0))],
        out_specs=[out] * 3, out_shape=[jax.ShapeDtypeStruct((NH, T, BLK), BF16)] * 3,
        compiler_params=_cp("parallel", "arbitrary"), name=name)(proj, proj, proj, ccol)


def _fox_fwd(qaug, kaug, vm, name):
    T = qaug.shape[1]
    t = _rt(T)
    nt = T // t

    def body(q0, q1, k0, k1, v0, v1, o_ref, lse_ref, m_ref, l_ref, acc_ref):
        i, j = pl.program_id(1), pl.program_id(2)
        lo = lax.broadcasted_iota(jnp.int32, (1, BLK), 1) < 64

        @pl.when(j == 0)
        def _():
            m_ref[...] = jnp.full_like(m_ref, NEG)
            l_ref[...] = jnp.zeros_like(l_ref)
            acc_ref[...] = jnp.zeros_like(acc_ref)

        def step(masked):
            for e, (q_ref, k_ref, v_ref) in enumerate(((q0, k0, v0), (q1, k1, v1))):
                s = _dot(q_ref[...], k_ref[...], 1, 1)
                if masked:
                    s = jnp.where(_fox_mask(i, j, t), s, NEG)
                m_old = m_ref[e]
                m_new = jnp.maximum(m_old, jnp.max(s, axis=-1, keepdims=True))
                alpha = jnp.exp(m_old - m_new)
                pe = jnp.exp(s - m_new)
                l_ref[e] = alpha * l_ref[e] + jnp.sum(pe, axis=-1, keepdims=True)
                m_ref[e] = m_new
                acc_ref[e] = alpha * acc_ref[e] + _dot(pe.astype(BF16), v_ref[...], 1, 0)

        pl.when((j < i) & (j > 0))(lambda: step(False))
        pl.when((j == i) | ((j == 0) & (i > 0)))(lambda: step(True))

        @pl.when(j == i)
        def _():
            rows = i * t + lax.broadcasted_iota(jnp.int32, (t, 1), 0)
            o = jnp.where(lo, acc_ref[0] / l_ref[0], acc_ref[1] / l_ref[1])
            o_ref[...] = jnp.where(rows >= NPAD, o, 0.0).astype(BF16)
            lse_ref[...] = jnp.where(lo, m_ref[0] + jnp.log(l_ref[0]), m_ref[1] + jnp.log(l_ref[1]))

    kj = lambda i, j: jnp.minimum(j, i)
    qs = lambda e: pl.BlockSpec((None, t, BLK), lambda p, i, j: (2 * p + e, i, 0))
    ks = lambda e: pl.BlockSpec((None, t, BLK), lambda p, i, j: (2 * p + e, kj(i, j), 0))
    return pl.pallas_call(
        body, grid=(NH // 2, nt, nt), in_specs=[qs(0), qs(1), ks(0), ks(1), ks(0), ks(1)],
        out_specs=[pl.BlockSpec((t, BLK), lambda p, i, j: (i, p))] * 2,
        out_shape=[jax.ShapeDtypeStruct((T, 512), BF16), jax.ShapeDtypeStruct((T, 512), F32)],
        scratch_shapes=[pltpu.VMEM((2, t, 1), F32), pltpu.VMEM((2, t, 1), F32), pltpu.VMEM((2, t, BLK), F32)],
        compiler_params=_cp("parallel", "parallel", "arbitrary"), name=name)(qaug, qaug, kaug, kaug, vm, vm)


def _fox_delta(do, o, name):
    T = do.shape[0]
    tr = _rt(T)

    def body(do_ref, o_ref, d_ref):
        lo = lax.broadcasted_iota(jnp.int32, (1, BLK), 1) < 64
        prod = do_ref[...].astype(F32) * o_ref[...].astype(F32)
        d0 = jnp.sum(jnp.where(lo, prod, 0.0), axis=-1, keepdims=True)
        d1 = jnp.sum(jnp.where(lo, 0.0, prod), axis=-1, keepdims=True)
        d_ref[...] = jnp.where(lo, d0, d1)

    blk = pl.BlockSpec((tr, BLK), lambda i, p: (i, p))
    return pl.pallas_call(
        body, grid=(T // tr, NH // 2), in_specs=[blk, blk], out_specs=blk, out_shape=jax.ShapeDtypeStruct((T, 512), F32),
        compiler_params=_cp("parallel", "parallel"), name=name)(do, o)


def _fox_bwd(qaug, kaug, vm, do, lse, delta, name):
    T = qaug.shape[1]
    t = _rt(T)
    nt = T // t

    def body(q0, q1, k0, k1, v0, v1, do_ref, lse_ref, dl_ref, dq_ref, dqx_ref, dk_ref, dv_ref, dkx_ref, dq_acc, dk_acc, dv_acc):
        j, i = pl.program_id(1), pl.program_id(2)
        lane = lax.broadcasted_iota(jnp.int32, (1, BLK), 1)
        lo = lane < 64

        @pl.when((j == 0) & (i == 0))
        def _():
            dq_acc[...] = jnp.zeros_like(dq_acc)

        @pl.when(i == 0)
        def _():
            dk_acc[...] = jnp.zeros_like(dk_acc)
            dv_acc[...] = jnp.zeros_like(dv_acc)

        def step(masked):
            dob = do_ref[...]
            lse_ = lse_ref[...]
            dl_ = dl_ref[...]
            rows = pl.ds(pl.multiple_of(i * t, t), t)
            for e, (q_ref, k_ref, v_ref) in enumerate(((q0, k0, v0), (q1, k1, v1))):
                s = _dot(q_ref[...], k_ref[...], 1, 1)
                if masked:
                    s = jnp.where(_fox_mask(i, j, t), s, NEG)
                pe = jnp.exp(s - _lane_pick(lse_, lane, 64 * e))
                dp = _dot(dob, v_ref[...], 1, 1)
                ds = (pe * (dp - _lane_pick(dl_, lane, 64 * e))).astype(BF16)
                dq_acc[e, rows, :] += _dot(ds, k_ref[...], 1, 0)
                dk_acc[e] += _dot(ds, q_ref[...], 0, 0)
                dv_acc[e] += _dot(pe.astype(BF16), dob, 0, 0)

        pl.when((i > j) & (j > 0))(lambda: step(False))
        pl.when((i == j) | ((j == 0) & (i > 0)))(lambda: step(True))

        @pl.when(i == nt - 1)
        def _():
            dk_ref[...] = jnp.where(lo, dk_acc[0], dk_acc[1]).astype(BF16)
            dv_ref[...] = jnp.where(lo, dv_acc[0], dv_acc[1]).astype(BF16)
            dkx_ref[...] = jnp.where(lo, dk_acc[1], dk_acc[0])

        @pl.when((i == nt - 1) & (j == nt - 1))
        def _():
            dq_ref[...] = (jnp.where(lo, dq_acc[0], dq_acc[1]) * SCALE).astype(BF16)
            dqx_ref[...] = jnp.where(lo, dq_acc[1], dq_acc[0])

    qi = lambda j, i: jnp.maximum(i, j)
    qs = lambda e: pl.BlockSpec((None, t, BLK), lambda p, j, i: (2 * p + e, qi(j, i), 0))
    ks = lambda e: pl.BlockSpec((None, t, BLK), lambda p, j, i: (2 * p + e, j, 0))
    qside = pl.BlockSpec((t, BLK), lambda p, j, i: (qi(j, i), p))
    kside = pl.BlockSpec((t, BLK), lambda p, j, i: (j, p))
    whole = pl.BlockSpec((T, BLK), lambda p, j, i: (0, p))
    return pl.pallas_call(
        body, grid=(NH // 2, nt, nt),
        in_specs=[qs(0), qs(1), ks(0), ks(1), ks(0), ks(1), qside, qside, qside],
        out_specs=[whole, whole, kside, kside, kside],
        out_shape=[jax.ShapeDtypeStruct((T, 512), BF16), jax.ShapeDtypeStruct((T, 512), F32),
                   jax.ShapeDtypeStruct((T, 512), BF16), jax.ShapeDtypeStruct((T, 512), BF16),
                   jax.ShapeDtypeStruct((T, 512), F32)],
        scratch_shapes=[pltpu.VMEM((2, T, BLK), F32), pltpu.VMEM((2, t, BLK), F32), pltpu.VMEM((2, t, BLK), F32)],
        compiler_params=_cp("parallel", "arbitrary", "arbitrary"), name=name)(qaug, qaug, kaug, kaug, vm, vm, do, lse, delta)


def _fox_mask(i, j, t):
    row = i * t + lax.broadcasted_iota(jnp.int32, (t, t), 0)
    col = j * t + lax.broadcasted_iota(jnp.int32, (t, t), 1)
    return (col <= row) & (col >= NPAD)


def _lane_pick(x, lane, idx):
    return jnp.sum(jnp.where(lane == idx, x, 0.0), axis=-1, keepdims=True)


def _lru_gates(xc, wr_ref, wi_ref, vec_ref):
    xb = xc.astype(BF16)
    pre_r = jnp.concatenate([_dot(xb[:, p * BLK:(p + 1) * BLK], wr_ref[p], 1, 0) for p in range(LW // BLK)], axis=1)
    pre_i = jnp.concatenate([_dot(xb[:, p * BLK:(p + 1) * BLK], wi_ref[p], 1, 0) for p in range(LW // BLK)], axis=1)
    r = _sigmoid(pre_r + vec_ref[0:1, :])
    gi = _sigmoid(pre_i + vec_ref[1:2, :])
    log_a = LRU_C * r * _log_sigmoid(vec_ref[2:3, :])
    a = jnp.exp(log_a)
    mult = jnp.sqrt(_neg_expm1(2.0 * log_a))
    return r, gi, a, mult


def _conv(xbuf_ref, x, cw_ref, vec_ref, tr):
    return (cw_ref[3:4, :] * x + cw_ref[2:3, :] * xbuf_ref[7:7 + tr, :] + cw_ref[1:2, :] * xbuf_ref[6:6 + tr, :]
            + cw_ref[0:1, :] * xbuf_ref[5:5 + tr, :] + vec_ref[3:4, :])


def _lru_fwd(proj, cw, wr, wi, vec, name):
    T = proj.shape[0]
    tr = _rt(T)

    def body(x_ref, y_ref, cw_ref, wr_ref, wi_ref, vec_ref, oc_ref, hs_ref, xbuf, abuf, bbuf, hcar):
        i = pl.program_id(0)

        @pl.when(i == 0)
        def _():
            xbuf[0:8, :] = jnp.zeros((8, LW), F32)
            hcar[...] = jnp.zeros_like(hcar)

        x = x_ref[...]
        xbuf[8:8 + tr, :] = x
        xc = _conv(xbuf, x, cw_ref, vec_ref, tr)
        xbuf[0:8, :] = x[tr - 8:tr, :]
        _, gi, a, mult = _lru_gates(xc, wr_ref, wi_ref, vec_ref)
        rows = i * tr + lax.broadcasted_iota(jnp.int32, (tr, 1), 0)
        abuf[...] = a
        bbuf[...] = jnp.where(rows >= NPAD, mult * (gi * xc), 0.0)
        sub = lax.broadcasted_iota(jnp.int32, (8, 1), 0)

        def step(k, h):
            sl = pl.ds(pl.multiple_of(k * 8, 8), 8)
            a8, b8 = abuf[sl, :], bbuf[sl, :]
            for s in (1, 2, 4):
                ok = sub >= s
                b8 = jnp.where(ok, a8 * pltpu.roll(b8, s, 0) + b8, b8)
                a8 = jnp.where(ok, a8 * pltpu.roll(a8, s, 0), a8)
            h8 = a8 * h + b8
            bbuf[sl, :] = h8
            return h8[7:8, :]

        hcar[...] = lax.fori_loop(0, tr // 8, step, hcar[...])
        hs = bbuf[...]
        hs_ref[...] = hs
        oc_ref[...] = (hs * _gelu(y_ref[...])).astype(BF16)

    row = pl.BlockSpec((tr, LW), lambda i: (i, 0))
    full = lambda shape: pl.BlockSpec(shape, lambda i: (0,) * len(shape))
    return pl.pallas_call(
        body, grid=(T // tr,),
        in_specs=[pl.BlockSpec((tr, LW), lambda i: (i, XC // LW)), pl.BlockSpec((tr, LW), lambda i: (i, YC // LW)),
                  full((4, LW)), full((4, BLK, BLK)), full((4, BLK, BLK)), full((8, LW))],
        out_specs=[row, row], out_shape=[jax.ShapeDtypeStruct((T, LW), BF16), jax.ShapeDtypeStruct((T, LW), F32)],
        scratch_shapes=[pltpu.VMEM((tr + 8, LW), F32), pltpu.VMEM((tr, LW), F32), pltpu.VMEM((tr, LW), F32),
                        pltpu.VMEM((1, LW), F32)],
        compiler_params=_cp("arbitrary"), name=name)(proj, proj, cw, wr, wi, vec)


def _lru_bwd(proj, hs, doc, cw, wr, wi, vec, name):
    T = proj.shape[0]
    tr = _rt(T)
    nt = T // tr
    r8 = tr // 8

    def body(x_ref, xp_ref, y_ref, hs_ref, hp_ref, do_ref, cw_ref, wr_ref, wi_ref, vec_ref,
             dx_ref, dy_ref, dwr_ref, dwi_ref, dvec_ref, xbuf, abuf, gbuf, hbuf, dbuf, gcar, acar):
        k = pl.program_id(0)
        i = nt - 1 - k

        @pl.when(k == 0)
        def _():
            dwr_ref[...] = jnp.zeros_like(dwr_ref)
            dwi_ref[...] = jnp.zeros_like(dwi_ref)
            dvec_ref[...] = jnp.zeros_like(dvec_ref)
            gcar[...] = jnp.zeros_like(gcar)
            acar[...] = jnp.zeros_like(acar)
            dbuf[tr:tr + 8, :] = jnp.zeros((8, LW), F32)

        first = i == 0
        x = x_ref[...]
        xbuf[0:8, :] = jnp.where(first, 0.0, xp_ref[...])
        xbuf[8:8 + tr, :] = x
        xc = _conv(xbuf, x, cw_ref, vec_ref, tr)
        r, gi, a, mult = _lru_gates(xc, wr_ref, wi_ref, vec_ref)
        y = y_ref[...]
        hs = hs_ref[...]
        do_ = do_ref[...].astype(F32)
        rows = i * tr + lax.broadcasted_iota(jnp.int32, (tr, 1), 0)
        abuf[0:tr, :] = a
        abuf[tr:tr + 8, :] = jnp.zeros((8, LW), F32) + acar[...]
        an = abuf[1:1 + tr, :]
        acar[...] = a[0:1, :]
        abuf[0:tr, :] = an
        gbuf[...] = do_ * _gelu(y)
        sub = lax.broadcasted_iota(jnp.int32, (8, 1), 0)

        def step(kk, g):
            sl = pl.ds(pl.multiple_of((r8 - 1 - kk) * 8, 8), 8)
            a8, b8 = abuf[sl, :], gbuf[sl, :]
            for s in (1, 2, 4):
                ok = sub < 8 - s
                b8 = jnp.where(ok, a8 * pltpu.roll(b8, 8 - s, 0) + b8, b8)
                a8 = jnp.where(ok, a8 * pltpu.roll(a8, 8 - s, 0), a8)
            g8 = a8 * g + b8
            gbuf[sl, :] = g8
            return g8[0:1, :]

        gcar[...] = lax.fori_loop(0, r8, step, gcar[...])
        g = gbuf[...]
        hbuf[0:8, :] = jnp.where(first, 0.0, hp_ref[...])
        hbuf[8:8 + tr, :] = hs
        hprev = hbuf[7:7 + tr, :]
        dinp = jnp.where(rows >= NPAD, g, 0.0)
        da = g * hprev
        dmult = dinp * gi * xc
        dgi = dinp * mult * xc
        dxc = dinp * mult * gi
        dlog_a = da * a - dmult * a * a / mult
        ls = _log_sigmoid(vec_ref[2:3, :])
        dpre_r = dlog_a * (LRU_C * ls) * r * (1.0 - r)
        dpre_i = dgi * gi * (1.0 - gi)
        xb = xc.astype(BF16)
        rb, ib = dpre_r.astype(BF16), dpre_i.astype(BF16)
        back = []
        for p in range(LW // BLK):
            c = slice(p * BLK, (p + 1) * BLK)
            back.append(_dot(rb[:, c], wr_ref[p], 1, 1) + _dot(ib[:, c], wi_ref[p], 1, 1))
            dwr_ref[p] += _dot(xb[:, c], rb[:, c], 0, 0)
            dwi_ref[p] += _dot(xb[:, c], ib[:, c], 0, 0)
        dxc = dxc + jnp.concatenate(back, axis=1)
        col = lambda v: jnp.sum(v, axis=0, keepdims=True)
        dvec_ref[0:1, :] += col(dpre_r)
        dvec_ref[1:2, :] += col(dpre_i)
        dvec_ref[2:3, :] += col(dlog_a * (LRU_C * r)) * _sigmoid(-vec_ref[2:3, :])
        dvec_ref[3:4, :] += col(dxc)
        dvec_ref[4:5, :] += col(dxc * xbuf[5:5 + tr, :])
        dvec_ref[5:6, :] += col(dxc * xbuf[6:6 + tr, :])
        dvec_ref[6:7, :] += col(dxc * xbuf[7:7 + tr, :])
        dvec_ref[7:8, :] += col(dxc * x)
        dbuf[0:tr, :] = dxc
        dxr = (cw_ref[3:4, :] * dxc + cw_ref[2:3, :] * dbuf[1:1 + tr, :] + cw_ref[1:2, :] * dbuf[2:2 + tr, :]
               + cw_ref[0:1, :] * dbuf[3:3 + tr, :])
        dbuf[tr:tr + 8, :] = dxc[0:8, :]
        dx_ref[...] = jnp.where(rows >= NPAD, dxr, 0.0).astype(BF16)
        dy_ref[...] = (do_ * hs * _gelu_grad(y)).astype(BF16)

    rev = lambda k: nt - 1 - k
    row = lambda col0: pl.BlockSpec((tr, LW), lambda k: (rev(k), col0))
    prev8 = lambda col0: pl.BlockSpec((8, LW), lambda k: (jnp.maximum(rev(k) * r8 - 1, 0), col0))
    full = lambda shape: pl.BlockSpec(shape, lambda k: (0,) * len(shape))
    return pl.pallas_call(
        body, grid=(nt,),
        in_specs=[row(XC // LW), prev8(XC // LW), row(YC // LW), row(0), prev8(0), row(0),
                  full((4, LW)), full((4, BLK, BLK)), full((4, BLK, BLK)), full((8, LW))],
        out_specs=[row(0), row(0), full((4, BLK, BLK)), full((4, BLK, BLK)), full((8, LW))],
        out_shape=[jax.ShapeDtypeStruct((T, LW), BF16), jax.ShapeDtypeStruct((T, LW), BF16),
                   jax.ShapeDtypeStruct((4, BLK, BLK), F32), jax.ShapeDtypeStruct((4, BLK, BLK), F32),
                   jax.ShapeDtypeStruct((8, LW), F32)],
        scratch_shapes=[pltpu.VMEM((tr + 8, LW), F32), pltpu.VMEM((tr + 8, LW), F32), pltpu.VMEM((tr, LW), F32),
                        pltpu.VMEM((tr + 8, LW), F32), pltpu.VMEM((tr + 8, LW), F32),
                        pltpu.VMEM((1, LW), F32), pltpu.VMEM((1, LW), F32)],
        compiler_params=_cp("arbitrary"), name=name)(proj, proj, proj, hs, hs, doc, cw, wr, wi, vec)


def _branch_merge_fwd(oa, of, oc, wb, proj, name):
    T = proj.shape[0]
    tm, tn = _rt(T), 512

    def body(a0, a1, a2, w_ref, g0, g1, g2, r0, r1, r2, m_ref):
        acc = None
        for g, (a_ref, g_ref, r_ref) in enumerate(((a0, g0, r0), (a1, g1, r1), (a2, g2, r2))):
            b = _dot(a_ref[...], w_ref[g], 1, 0)
            r_ref[...] = b
            term = _sigmoid(g_ref[...]) * b
            acc = term if acc is None else acc + term
        m_ref[...] = acc.astype(BF16)

    act = pl.BlockSpec((tm, LW), lambda j, i: (i, 0))
    gate = lambda g: pl.BlockSpec((tm, tn), lambda j, i: (i, (GT + g * D) // tn + j))
    blk = pl.BlockSpec((tm, tn), lambda j, i: (i, j))
    return pl.pallas_call(
        body, grid=(D // tn, T // tm),
        in_specs=[act, act, act, pl.BlockSpec((3, LW, tn), lambda j, i: (0, 0, j)), gate(0), gate(1), gate(2)],
        out_specs=[blk] * 4,
        out_shape=[jax.ShapeDtypeStruct((T, D), F32)] * 3 + [jax.ShapeDtypeStruct((T, D), BF16)],
        compiler_params=_cp("parallel", "parallel"), name=name)(oa, of, oc, wb, proj, proj, proj)


def _out_dx_merge_bwd(dhb, w_out, proj, b0, b1, b2, name):
    T = proj.shape[0]
    tm, tn = _rt(T), 512

    def body(dh_ref, w_ref, g0, g1, g2, r0, r1, r2, d0, d1, d2, e0, e1, e2):
        dmv = _dot(dh_ref[...], w_ref[...], 1, 1)
        for g_ref, r_ref, d_ref, e_ref in ((g0, r0, d0, e0), (g1, r1, d1, e1), (g2, r2, d2, e2)):
            sg = _sigmoid(g_ref[...])
            d_ref[...] = (dmv * sg).astype(BF16)
            e_ref[...] = (dmv * r_ref[...] * sg * (1.0 - sg)).astype(BF16)

    gate = lambda g: pl.BlockSpec((tm, tn), lambda j, i: (i, (GT + g * D) // tn + j))
    blk = pl.BlockSpec((tm, tn), lambda j, i: (i, j))
    return pl.pallas_call(
        body, grid=(D // tn, T // tm),
        in_specs=[pl.BlockSpec((tm, D), lambda j, i: (i, 0)), pl.BlockSpec((tn, D), lambda j, i: (j, 0)),
                  gate(0), gate(1), gate(2), blk, blk, blk],
        out_specs=[blk] * 6, out_shape=[jax.ShapeDtypeStruct((T, D), BF16)] * 6,
        compiler_params=_cp("parallel", "parallel"), name=name)(dhb, w_out, proj, proj, proj, b0, b1, b2)


def _ffn_in_swiglu_fwd(u, w, name):
    T = u.shape[0]
    tm, tn = _rt(T), _pick(DFF, (1408, 256))
    nj = DFF // tn

    def body(u_ref, wg_ref, wu_ref, g_ref, up_ref, a_ref):
        ub = u_ref[...]
        g = _dot(ub, wg_ref[...], 1, 0)
        up = _dot(ub, wu_ref[...], 1, 0)
        g_ref[...] = g
        up_ref[...] = up
        a_ref[...] = (g * _sigmoid(g) * up).astype(BF16)

    blk = pl.BlockSpec((tm, tn), lambda j, i: (i, j))
    return pl.pallas_call(
        body, grid=(nj, T // tm),
        in_specs=[pl.BlockSpec((tm, D), lambda j, i: (i, 0)), pl.BlockSpec((D, tn), lambda j, i: (0, j)),
                  pl.BlockSpec((D, tn), lambda j, i: (0, j + nj))],
        out_specs=[blk] * 3,
        out_shape=[jax.ShapeDtypeStruct((T, DFF), F32)] * 2 + [jax.ShapeDtypeStruct((T, DFF), BF16)],
        compiler_params=_cp("parallel", "parallel"), name=name)(u, w, w)


def _ffn_out_dx_swiglu_bwd(dhb, w, gate, up, name):
    T = dhb.shape[0]
    tm, tn = _rt(T), _pick(DFF, (1408, 256))

    def body(dh_ref, w_ref, g_ref, up_ref, dg_ref, du_ref):
        d = _dot(dh_ref[...], w_ref[...], 1, 1)
        g = g_ref[...]
        sg = _sigmoid(g)
        dg_ref[...] = (d * up_ref[...] * (sg + g * sg * (1.0 - sg))).astype(BF16)
        du_ref[...] = (d * g * sg).astype(BF16)

    blk = pl.BlockSpec((tm, tn), lambda j, i: (i, j))
    return pl.pallas_call(
        body, grid=(DFF // tn, T // tm),
        in_specs=[pl.BlockSpec((tm, D), lambda j, i: (i, 0)), pl.BlockSpec((tn, D), lambda j, i: (j, 0)), blk, blk],
        out_specs=[blk] * 2, out_shape=[jax.ShapeDtypeStruct((T, DFF), BF16)] * 2,
        compiler_params=_cp("parallel", "parallel"), name=name)(dhb, w, gate, up)


def _adamw(w, g, m, v, name):
    R, C = w.shape
    tr = _pick(R, tuple(t for t in (512, 256, 128, 64, 32, 16, 8) if t * C * 4 <= (3 << 19)))
    c1 = 1.0 - ADAM_B1 ** ADAM_STEP
    c2 = 1.0 - ADAM_B2 ** ADAM_STEP

    def body(w_ref, g_ref, m_ref, v_ref, d_ref, mo_ref, vo_ref):
        gv = g_ref[...]
        mn = ADAM_B1 * m_ref[...] + (1.0 - ADAM_B1) * gv
        vn = ADAM_B2 * v_ref[...] + (1.0 - ADAM_B2) * (gv * gv)
        d_ref[...] = -ADAM_LR * ((mn / c1) / (jnp.sqrt(vn / c2) + ADAM_EPS) + ADAM_WD * w_ref[...])
        mo_ref[...] = mn
        vo_ref[...] = vn

    blk = pl.BlockSpec((tr, C), lambda i: (i, 0))
    return pl.pallas_call(
        body, grid=(R // tr,), in_specs=[blk] * 4, out_specs=[blk] * 3,
        out_shape=[jax.ShapeDtypeStruct((R, C), F32)] * 3, compiler_params=_cp("parallel"), name=name)(w, g, m, v)


def _sum_lead(x, name):
    n, R, C = x.shape
    tr = _pick(R, (512, 256, 128, 64, 32, 16, 8))

    def body(x_ref, o_ref):
        acc = x_ref[0]
        for d in range(1, n):
            acc = acc + x_ref[d]
        o_ref[...] = acc

    return pl.pallas_call(
        body, grid=(R // tr,), in_specs=[pl.BlockSpec((n, tr, C), lambda i: (0, i, 0))],
        out_specs=pl.BlockSpec((tr, C), lambda i: (i, 0)), out_shape=jax.ShapeDtypeStruct((R, C), F32),
        compiler_params=_cp("parallel"), name=name)(x)


def _here():
    return lax.axis_index("x"), lax.axis_index("y"), lax.axis_index("c")


def _rcopy(src, dst, send_sems, recv_sems, k, to):
    return pltpu.make_async_remote_copy(src_ref=src, dst_ref=dst, send_sem=send_sems.at[k], recv_sem=recv_sems.at[k],
                                        device_id=to, device_id_type=MESH)


def _window(ref, lead, axis, start, width):
    idx = [lead] + [slice(None)] * (len(ref.shape) - 1)
    if axis is not None:
        idx[axis] = pl.ds(start, width)
    return ref.at[tuple(idx)]


def _hbm_calls(body, args, out_shapes, n_sems, aliases, name):
    return pl.pallas_call(
        body, in_specs=[ANY] * len(args), out_specs=[ANY] * len(out_shapes), out_shape=out_shapes,
        input_output_aliases=aliases,
        scratch_shapes=[pltpu.SemaphoreType.DMA((n_sems,)), pltpu.SemaphoreType.DMA((n_sems,))],
        compiler_params=pltpu.CompilerParams(has_side_effects=True), name=name)(*args)


def _all_gather_weights(fulls, axes, name):
    nt = len(fulls)

    def body(*refs):
        outs, (send_sems, recv_sems) = refs[nt:2 * nt], refs[2 * nt:]
        x, y, c = _here()
        sib = (x, y, 1 - c)
        chips = [(1 - x, y), (x, 1 - y), (1 - x, 1 - y)]

        def win(t, chip, hc):
            w = outs[t].shape[axes[t]] // N_SHARD
            return _window(outs[t], pl.ds(2 * hc, 2), axes[t], pl.multiple_of((2 * chip[0] + chip[1]) * w, 8), w)

        def copy(t, k, chip, hc, to):
            return _rcopy(win(t, chip, hc), win(t, chip, hc), send_sems, recv_sems, 6 * t + k, to)

        sends = [copy(t, k, (x, y), c, (*chip, c)) for t in range(nt) for k, chip in enumerate(chips)]
        for cp in sends:
            cp.start()
        for t in range(nt):
            for k, chip in enumerate(chips):
                copy(t, k, chip, c, (*chip, c)).wait_recv()
                fwd = copy(t, 3 + k, chip, c, sib)
                fwd.start()
                sends.append(fwd)
        for t in range(nt):
            for k, chip in enumerate(chips):
                copy(t, 3 + k, chip, 1 - c, sib).wait_recv()
        for cp in sends:
            cp.wait_send()

    return _hbm_calls(body, fulls, [jax.ShapeDtypeStruct(f.shape, f.dtype) for f in fulls], 6 * nt,
                      {t: t for t in range(nt)}, name)


def _swap_halves(gs, name):
    nt = len(gs)

    def body(*refs):
        ins, outs, (send_sems, recv_sems) = refs[:nt], refs[nt:2 * nt], refs[2 * nt:]
        x, y, c = _here()
        cps = [_rcopy(g.at[pl.ds(2 * (1 - c), 2)], o, send_sems, recv_sems, t, (x, y, 1 - c))
               for t, (g, o) in enumerate(zip(ins, outs))]
        for cp in cps:
            cp.start()
        for cp in cps:
            cp.wait()

    return _hbm_calls(body, gs, [jax.ShapeDtypeStruct((2,) + g.shape[1:], g.dtype) for g in gs], nt, {}, name)


def _scatter_to_chips(ss, axes, name):
    nt = len(ss)

    def shard_shape(s, ax):
        shp = list(s.shape)
        shp[ax] //= N_SHARD
        return tuple(shp)

    def body(*refs):
        ins, outs, (send_sems, recv_sems) = refs[:nt], refs[nt:2 * nt], refs[2 * nt:]
        x, y, c = _here()
        chips = [(1 - x, y), (x, 1 - y), (1 - x, 1 - y)]
        cps = []
        for t, (s, o, ax) in enumerate(zip(ins, outs, axes)):
            w = s.shape[ax] // N_SHARD
            for k, chip in enumerate(chips):
                src = _window(s, slice(None), ax, pl.multiple_of((2 * chip[0] + chip[1]) * w, 8), w)
                cps.append(_rcopy(src, o.at[k], send_sems, recv_sems, 3 * t + k, (*chip, c)))
        for cp in cps:
            cp.start()
        for cp in cps:
            cp.wait()

    return _hbm_calls(body, ss, [jax.ShapeDtypeStruct((3,) + shard_shape(s, ax), s.dtype) for s, ax in zip(ss, axes)],
                      3 * nt, {}, name)


def _join_halves(fins, name):
    nt = len(fins)

    def body(*refs):
        outs, (send_sems, recv_sems) = refs[nt:2 * nt], refs[2 * nt:]
        x, y, c = _here()
        cps = [_rcopy(o.at[pl.ds(2 * c, 2)], o.at[pl.ds(2 * c, 2)], send_sems, recv_sems, t, (x, y, 1 - c))
               for t, o in enumerate(outs)]
        for cp in cps:
            cp.start()
        for t, o in enumerate(outs):
            _rcopy(o.at[pl.ds(2 * (1 - c), 2)], o.at[pl.ds(2 * (1 - c), 2)], send_sems, recv_sems, t, (x, y, 1 - c)).wait_recv()
        for cp in cps:
            cp.wait_send()

    return _hbm_calls(body, fins, [jax.ShapeDtypeStruct(f.shape, f.dtype) for f in fins], nt, {t: t for t in range(nt)}, name)


def _all_gather_small(buf, name):
    def body(_, out_ref, send_sems, recv_sems):
        x, y, c = _here()
        me = 4 * x + 2 * y + c
        cps = []
        for k in range(1, 8):
            to = (x ^ ((k >> 2) & 1), y ^ ((k >> 1) & 1), c ^ (k & 1))
            peer = 4 * to[0] + 2 * to[1] + to[2]
            cps.append((_rcopy(out_ref.at[me], out_ref.at[me], send_sems, recv_sems, k - 1, to),
                        _rcopy(out_ref.at[peer], out_ref.at[peer], send_sems, recv_sems, k - 1, to)))
        for snd, _ in cps:
            snd.start()
        for _, rcv in cps:
            rcv.wait_recv()
        for snd, _ in cps:
            snd.wait_send()

    return _hbm_calls(body, [buf], [jax.ShapeDtypeStruct(buf.shape, buf.dtype)], 7, {0: 0}, name)[0]


def _place(block, n, index):
    buf = jnp.zeros((n,) + block.shape[1:], block.dtype)
    return lax.dynamic_update_slice_in_dim(buf, block, index, axis=0)


def _add_half(g, other, cidx, name):
    _, R, C = g.shape
    tr = _pick(R, tuple(t for t in (512, 256, 128, 64, 32, 16, 8) if t * C * 4 <= (1 << 21)))

    def body(c_ref, g_ref, o_ref, s_ref, sb_ref):
        s = g_ref[...] + o_ref[...]
        s_ref[...] = s
        sb_ref[...] = s.astype(BF16)

    blk = pl.BlockSpec((None, tr, C), lambda l, i, c: (l, i, 0))
    return pl.pallas_call(
        body,
        grid_spec=pltpu.PrefetchScalarGridSpec(
            num_scalar_prefetch=1, grid=(2, R // tr),
            in_specs=[pl.BlockSpec((None, tr, C), lambda l, i, c: (2 * c[0] + l, i, 0)), blk], out_specs=[blk, blk]),
        out_shape=[jax.ShapeDtypeStruct((2, R, C), F32), jax.ShapeDtypeStruct((2, R, C), BF16)],
        compiler_params=_cp("parallel", "parallel"), name=name)(cidx, g, other)


def _add_chips(s, recv, axis, chip_idx, name):
    _, _, r, cw = recv.shape
    tr = _pick(r, tuple(t for t in (704, 512, 256, 128, 64, 32, 16, 8) if t * cw * 4 <= (1 << 21)))
    nr = r // tr

    def body(c_ref, s_ref, r_ref, out_ref):
        out_ref[...] = ((s_ref[...] + r_ref[0].astype(F32)) + r_ref[1].astype(F32)) + r_ref[2].astype(F32)

    if axis == 2:
        s_map = lambda l, i, c: (l, i, c[0])
    else:
        s_map = lambda l, i, c: (l, c[0] * nr + i, 0)
    return pl.pallas_call(
        body,
        grid_spec=pltpu.PrefetchScalarGridSpec(
            num_scalar_prefetch=1, grid=(2, nr),
            in_specs=[pl.BlockSpec((None, tr, cw), s_map), pl.BlockSpec((3, None, tr, cw), lambda l, i, c: (0, l, i, 0))],
            out_specs=pl.BlockSpec((None, tr, cw), lambda l, i, c: (l, i, 0))),
        out_shape=jax.ShapeDtypeStruct((2, r, cw), F32), compiler_params=_cp("parallel", "parallel"), name=name)(chip_idx, s, recv)


IN_SHARD = IN_COLS // N_SHARD
IN_SLOT = INP // N_SHARD
IN_PIECES = ((0, 512, QA), (512, 640, KA), (640, 768, VA), (768, 1280, QF), (1280, 1792, KF), (1792, 2304, VF),
             (2304, 2312, FL), (2312, 2824, XC), (2824, 3336, YC), (3336, 6408, GT))


def _gathered_to_kernel_cols(w):
    parts, pos = [], 0
    for a, b, k in sorted(IN_PIECES, key=lambda p: p[2]):
        assert k == pos
        while a < b:
            j = a // IN_SHARD
            e = min(b, (j + 1) * IN_SHARD)
            g = j * IN_SLOT + a - j * IN_SHARD
            parts.append(w[..., g:g + e - a])
            pos += e - a
            a = e
    parts.append(jnp.zeros(w.shape[:-1] + (INP - pos,), w.dtype))
    return jnp.concatenate(parts, axis=-1)


def _kernel_to_gathered_cols(w):
    parts = []
    for j in range(N_SHARD):
        lo, hi = j * IN_SHARD, (j + 1) * IN_SHARD
        for a, b, k in IN_PIECES:
            s, e = max(a, lo), min(b, hi)
            if s < e:
                parts.append(w[..., k + s - a:k + e - a])
        parts.append(jnp.zeros(w.shape[:-1] + (IN_SLOT - IN_SHARD,), w.dtype))
    return jnp.concatenate(parts, axis=-1)


def _pair_blocks(w):
    z = jnp.zeros((4, 64, 64), w.dtype)
    w = w.reshape(4, 2, 64, 64)
    top = jnp.concatenate([w[:, 0], z], axis=2)
    bot = jnp.concatenate([z, w[:, 1]], axis=2)
    return jnp.concatenate([top, bot], axis=1)


def _unpair_blocks(w):
    return jnp.stack([w[:, :64, :64], w[:, 64:, 64:]], axis=1).reshape(8, 64, 64)


BIG = ("w_in", "w_branch", "w_out", "w_ffn_in", "w_ffn_out")
TINY = ("conv_w", "meta_tokens")
SMALL = ("rel_bias_table", "norm_mix", "swa_sinks", "fox_forget_bias", "conv_b", "lru_w_r", "lru_b_r", "lru_w_i",
         "lru_b_i", "lru_lambda", "norm_ffn", "norm_final")
SHARD_AXIS = {"conv_w": 2, "meta_tokens": 1}
BIG_AXIS = {"w_in": 2, "w_branch": 2, "w_out": 1, "w_ffn_in": 2, "w_ffn_out": 1}


def _pack(d, names):
    flat = jnp.concatenate([d[n].reshape(-1) for n in names])
    pad = (-flat.shape[0]) % (256 * 128)
    return jnp.concatenate([flat, jnp.zeros((pad,), F32)]).reshape(-1, 128)


def _unpack(buf, names, shapes):
    flat, out, off = buf.reshape(-1), {}, 0
    for n in names:
        sz = int(np.prod(shapes[n]))
        out[n] = flat[off:off + sz].reshape(shapes[n])
        off += sz
    return out


def _local_step(x, tgt, W):
    S = x.shape[0]
    T = S + BLK
    tm = _pick(T, (1408, 384, 128))
    bucket = jnp.asarray(_bucket_table())
    bias = _bias_build(W["rel_bias_table"], bucket, "bias_build")
    h = jnp.concatenate([jnp.zeros((NPAD, D), F32), W["meta_tokens"], x], axis=0)

    saved = []
    for l in range(DEPTH):
        sv = {"h0": h}
        u = _rms_fwd(h, W["norm_mix"][l], f"rms_mix_fwd")
        proj = _mm(u, W["w_in"][l], tm=tm, tn=512, tk=D, name="mm_in_fwd")
        oa = _swa_fwd(proj, bias, W["swa_sinks"][l], "swa_fwd")
        z3 = _to_blocks(proj[:, FL:FL + NH].T)
        fb = W["fox_forget_bias"][l].reshape(NH, 1)
        crow = _from_blocks(_cum_fwd(z3, fb, "cum_fwd"))
        ccol = jnp.pad(crow.T, ((0, 0), (0, BLK - NH)))
        qaug, kaug, vm = _fox_prep(proj, ccol, "fox_prep")
        of, lse = _fox_fwd(qaug, kaug, vm, "fox_fwd")
        lru_vec = jnp.concatenate([W["lru_b_r"][l][None], W["lru_b_i"][l][None], W["lru_lambda"][l][None],
                                   W["conv_b"][l][None], jnp.zeros((4, LW), F32)], axis=0)
        oc, hs = _lru_fwd(proj, W["conv_w"][l], W["lru_w_r"][l], W["lru_w_i"][l], lru_vec, "lru_fwd")
        *bs, merged = _branch_merge_fwd(oa, of, oc, W["w_branch"][l], proj, "branch_merge_fwd")
        h2 = _mm(merged, W["w_out"][l], res=h, tm=tm, tn=512, tk=D, name="mm_out_fwd")
        u2 = _rms_fwd(h2, W["norm_ffn"][l], "rms_ffn_fwd")
        gate, up, act = _ffn_in_swiglu_fwd(u2, W["w_ffn_in"][l], "ffn_in_swiglu_fwd")
        h = _mm(act, W["w_ffn_out"][l], res=h2, tm=tm, tn=512, tk=_pick(DFF, (1408, 256)), name="mm_ffn_out_fwd")
        sv.update(u=u, proj=proj, oa=oa, of=of, oc=oc, lse=lse, hs=hs, z3=z3, fb=fb, qaug=qaug, kaug=kaug, vm=vm, lru_vec=lru_vec,
                  bs=bs, merged=merged, h2=h2, u2=u2, gate=gate, up=up, act=act)
        saved.append(sv)

    tgt_pad = tgt
    dh, dhb, dg_final, loss_vec = _loss_head(h, tgt_pad, W["norm_final"], "loss_head")
    loss = loss_vec[0, 0]

    small = ("norm_mix", "swa_sinks", "fox_forget_bias", "conv_w", "conv_b", "lru_w_r", "lru_b_r", "lru_w_i", "lru_b_i",
             "lru_lambda", "norm_ffn")
    G = {n: [None] * DEPTH for n in small}
    G["norm_final"] = dg_final.reshape(D)
    GW = {n: None for n in BIG}
    dbias = jnp.zeros((NH, BLK, 2 * BLK), F32)
    tkT = _rt(T)
    for l in reversed(range(DEPTH)):
        sv = saved[l]
        GW["w_ffn_out"] = _mm(_transpose(sv["act"], "tr_act"), dhb, tm=_pick(DFF, (1408, 256)), tn=D, tk=tkT,
                              slab=(GW["w_ffn_out"], l, DEPTH), name="mm_ffn_out_dw")
        dgate, dup = _ffn_out_dx_swiglu_bwd(dhb, W["w_ffn_out"][l], sv["gate"], sv["up"], "ffn_out_dx_swiglu_bwd")
        u2t = _transpose(sv["u2"], "tr_u2")
        du2 = None
        for half, dpart in enumerate((dgate, dup)):
            GW["w_ffn_in"] = _mm(u2t, dpart, tm=D, tn=_pick(DFF, (1408, 256)), tk=tkT, slab=(GW["w_ffn_in"], l, DEPTH),
                                 col0=half * DFF, cols=2 * DFF, name="mm_ffn_in_dw")
            du2 = _mm(dpart, W["w_ffn_in"][l], tb=True, res=du2, b_k0=half * DFF, tm=tm, tn=512,
                      tk=_pick(DFF, (1408, 256)), name="mm_ffn_in_dx")
        dh, dhb, dgn = _rms_bwd(du2, sv["h2"], W["norm_ffn"][l], dh, "rms_ffn_bwd")
        G["norm_ffn"][l] = dgn.reshape(D)
        GW["w_out"] = _mm(_transpose(sv["merged"], "tr_merged"), dhb, tm=D, tn=D, tk=tkT,
                          slab=(GW["w_out"], l, DEPTH), name="mm_out_dw")
        db0, db1, db2, dg0, dg1, dg2 = _out_dx_merge_bwd(dhb, W["w_out"][l], sv["proj"], *sv["bs"], "out_dx_merge_bwd")
        dos = []
        for g, (o, db) in enumerate(zip((sv["oa"], sv["of"], sv["oc"]), (db0, db1, db2))):
            GW["w_branch"] = _mm(_transpose(o, "tr_branch"), db, tm=LW, tn=D, tk=tkT,
                                 slab=(GW["w_branch"], 3 * l + g, 3 * DEPTH), name="mm_branch_dw")
            dos.append(_mm(db, W["w_branch"][l, g], tb=True, out_dtype=BF16, tm=tm, tn=LW, tk=D, name="mm_branch_dx"))
        dqa, dkb, dvb, dbias, dsk = _swa_bwd(sv["proj"], bias, W["swa_sinks"][l], dos[0], dbias, "swa_bwd")
        dka, dva = _band_fold(dkb, dvb, "swa_band_fold")
        G["swa_sinks"][l] = dsk[0, :NH]
        delta = _fox_delta(dos[1], sv["of"], "fox_delta")
        dqf, dqx, dkf, dvf, dkx = _fox_bwd(sv["qaug"], sv["kaug"], sv["vm"], dos[1], sv["lse"], delta, "fox_bwd")
        aug0 = [BLK * (h // 2) + (64 if h % 2 == 0 else 0) for h in range(NH)]
        cols = lambda a, off: jnp.concatenate([a[:, c + off:c + off + 1] for c in aug0], axis=1)
        dcq3 = _to_blocks(cols(dqx, 0).T)
        dck3 = _to_blocks(cols(dkx, 3).T)
        dz3, dfb = _cum_bwd(dck3, dcq3, sv["z3"], sv["fb"], "cum_bwd")
        G["fox_forget_bias"][l] = dfb[:, 0]
        dfl = jnp.pad(_from_blocks(dz3).T, ((0, 0), (0, INP - FL - NH))).astype(BF16)
        dxc, dyc, dwr, dwi, dvec = _lru_bwd(sv["proj"], sv["hs"], dos[2], W["conv_w"][l], W["lru_w_r"][l], W["lru_w_i"][l],
                                            sv["lru_vec"], "lru_bwd")
        G["lru_w_r"][l], G["lru_w_i"][l] = _unpair_blocks(dwr), _unpair_blocks(dwi)
        G["lru_b_r"][l], G["lru_b_i"][l], G["lru_lambda"][l], G["conv_b"][l] = dvec[0], dvec[1], dvec[2], dvec[3]
        G["conv_w"][l] = dvec[4:8]
        dproj = jnp.concatenate([dqa, dqf, dkf, dvf, dxc, dyc, dg0, dg1, dg2, dka, dva, dfl], axis=1)
        GW["w_in"] = _mm(_transpose(sv["u"], "tr_u"), dproj, tm=D, tn=IN_SLOT, tk=tkT,
                         slab=(GW["w_in"], l, DEPTH), name="mm_in_dw")
        du = _mm(dproj, W["w_in"][l], tb=True, tm=tm, tn=512, tk=_pick(INP, (1664, 512)), name="mm_in_dx")
        dh, dhb, dgn = _rms_bwd(du, sv["h0"], W["norm_mix"][l], dh, "rms_mix_bwd")
        G["norm_mix"][l] = dgn.reshape(D)

    grads = {n: (jnp.stack(v) if isinstance(v, list) else v) for n, v in G.items()}
    grads.update(GW)
    grads["w_branch"] = GW["w_branch"].reshape(DEPTH, 3 * LW, D)
    grads["rel_bias_table"] = _bias_bwd(dbias, bucket, "bias_bwd")
    grads["meta_tokens"] = dh[NPAD:BLK]
    return loss, dh[BLK:], grads


NAMES = ("meta_tokens", "rel_bias_table", "norm_mix", "w_in", "swa_sinks", "fox_forget_bias", "conv_w", "conv_b",
         "lru_w_r", "lru_b_r", "lru_w_i", "lru_b_i", "lru_lambda", "w_branch", "w_out", "norm_ffn", "w_ffn_in",
         "w_ffn_out", "norm_final")


def _three_d(n, a):
    return a.reshape(DEPTH, 3 * LW, -1) if n == "w_branch" else a


def _gather_weights(P):
    x, y, c = _here()
    mine, me = 2 * x + y, 4 * x + 2 * y + c
    fulls = []
    for n in BIG:
        shard = _three_d(n, P[n].astype(BF16))
        if n == "w_in":
            shard = jnp.pad(shard, ((0, 0), (0, 0), (0, IN_SLOT - IN_SHARD)))
        ax = BIG_AXIS[n]
        shape = list(shard.shape)
        shape[ax] *= N_SHARD
        fulls.append(lax.dynamic_update_slice_in_dim(jnp.zeros(shape, BF16), shard, mine * shard.shape[ax], axis=ax))
    full = dict(zip(BIG, _all_gather_weights(fulls, [BIG_AXIS[n] for n in BIG], "ag_weights")))
    full["w_in"] = _gathered_to_kernel_cols(full["w_in"])
    full["w_branch"] = full["w_branch"].reshape(DEPTH, 3, LW, D)
    tiny = _all_gather_small(_place(_pack(P, TINY)[None], 8, me), "ag_tiny_weights")
    parts = [_unpack(tiny[2 * j], TINY, {n: P[n].shape for n in TINY}) for j in range(N_SHARD)]
    for n in TINY:
        full[n] = jnp.concatenate([p[n] for p in parts], axis=SHARD_AXIS[n])
    for n in SMALL:
        full[n] = P[n]
    full["lru_w_r"] = jnp.stack([_pair_blocks(P["lru_w_r"][l]) for l in range(DEPTH)]).astype(BF16)
    full["lru_w_i"] = jnp.stack([_pair_blocks(P["lru_w_i"][l]) for l in range(DEPTH)]).astype(BF16)
    return full


def _reduce_grads(grads, P):
    x, y, c = _here()
    mine, me = 2 * x + y, 4 * x + 2 * y + c
    cidx = jnp.reshape(c, (1,)).astype(jnp.int32)
    chip = jnp.reshape(mine, (1,)).astype(jnp.int32)
    axes = [BIG_AXIS[n] for n in BIG]
    gs = [grads[n] for n in BIG]
    pairs = [_add_half(g, r, cidx, "rs_add_half_" + n) for n, g, r in zip(BIG, gs, _swap_halves(gs, "rs_swap_halves"))]
    ss, sbs = [list(t) for t in zip(*pairs)]
    ss[0], sbs[0] = _kernel_to_gathered_cols(ss[0]), _kernel_to_gathered_cols(sbs[0])
    recv = _scatter_to_chips(sbs, axes, "rs_scatter")
    tots = [_add_chips(s, r, ax, chip, "rs_add_chips_" + n) for n, s, r, ax in zip(BIG, ss, recv, axes)]
    fins = dict(zip(BIG, _join_halves([_place(t, DEPTH, 2 * c) for t in tots], "rs_join_halves")))
    out = {n: fins[n].reshape(P[n].shape) for n in BIG if n != "w_in"}
    out["w_in"] = fins["w_in"][:, :, :IN_SHARD]
    names = SMALL + TINY
    gathered = _all_gather_small(_place(_pack(grads, names)[None], 8, me), "ag_small_grads")
    small = _unpack(_sum_lead(gathered, "sum_small_grads"), names, {n: grads[n].shape for n in names})
    for n in SMALL:
        out[n] = small[n]
    for n in TINY:
        w = P[n].shape[SHARD_AXIS[n]]
        out[n] = lax.dynamic_slice_in_dim(small[n], mine * w, w, axis=SHARD_AXIS[n])
    return out


def _update(P, Gd, M, V):
    delta, new_m, new_v = {}, {}, {}
    for n in BIG + TINY:
        shp = P[n].shape
        two = (int(np.prod(shp[:-1])), shp[-1])
        d, m, v = _adamw(P[n].reshape(two), Gd[n].reshape(two), M[n].reshape(two), V[n].reshape(two), "adamw_" + n)
        delta[n], new_m[n], new_v[n] = d.reshape(shp), m.reshape(shp), v.reshape(shp)
    shapes = {n: P[n].shape for n in SMALL}
    d, m, v = _adamw(_pack(P, SMALL), _pack(Gd, SMALL), _pack(M, SMALL), _pack(V, SMALL), "adamw_small")
    for dst, buf in ((delta, d), (new_m, m), (new_v, v)):
        dst.update(_unpack(buf, SMALL, shapes))
    return delta, new_m, new_v


def kernel(x, meta_tokens, rel_bias_table, norm_mix, w_in, swa_sinks, fox_forget_bias, conv_w, conv_b, lru_w_r, lru_b_r, lru_w_i, lru_b_i, lru_lambda, w_branch, w_out, norm_ffn, w_ffn_in, w_ffn_out, norm_final, loss_target, m_meta_tokens, m_rel_bias_table, m_norm_mix, m_w_in, m_swa_sinks, m_fox_forget_bias, m_conv_w, m_conv_b, m_lru_w_r, m_lru_b_r, m_lru_w_i, m_lru_b_i, m_lru_lambda, m_w_branch, m_w_out, m_norm_ffn, m_w_ffn_in, m_w_ffn_out, m_norm_final, v_meta_tokens, v_rel_bias_table, v_norm_mix, v_w_in, v_swa_sinks, v_fox_forget_bias, v_conv_w, v_conv_b, v_lru_w_r, v_lru_b_r, v_lru_w_i, v_lru_b_i, v_lru_lambda, v_w_branch, v_w_out, v_norm_ffn, v_w_ffn_in, v_w_ffn_out, v_norm_final):
    P = dict(zip(NAMES, (meta_tokens, rel_bias_table, norm_mix, w_in, swa_sinks, fox_forget_bias, conv_w, conv_b, lru_w_r,
                         lru_b_r, lru_w_i, lru_b_i, lru_lambda, w_branch, w_out, norm_ffn, w_ffn_in, w_ffn_out, norm_final)))
    M = dict(zip(NAMES, (m_meta_tokens, m_rel_bias_table, m_norm_mix, m_w_in, m_swa_sinks, m_fox_forget_bias, m_conv_w,
                         m_conv_b, m_lru_w_r, m_lru_b_r, m_lru_w_i, m_lru_b_i, m_lru_lambda, m_w_branch, m_w_out, m_norm_ffn,
                         m_w_ffn_in, m_w_ffn_out, m_norm_final)))
    V = dict(zip(NAMES, (v_meta_tokens, v_rel_bias_table, v_norm_mix, v_w_in, v_swa_sinks, v_fox_forget_bias, v_conv_w,
                         v_conv_b, v_lru_w_r, v_lru_b_r, v_lru_w_i, v_lru_b_i, v_lru_lambda, v_w_branch, v_w_out, v_norm_ffn,
                         v_w_ffn_in, v_w_ffn_out, v_norm_final)))
    W = _gather_weights(P)
    loss_local, grad_x, grads = _local_step(x[0], loss_target[0], W)
    loss = lax.psum(loss_local, ("x", "y", "c"))
    Gd = _reduce_grads(grads, P)
    delta, new_m, new_v = _update(P, Gd, M, V)
    return (loss, grad_x[None], *[Gd[n] for n in NAMES], *[delta[n] for n in NAMES],
            *[new_m[n] for n in NAMES], *[new_v[n] for n in NAMES])
```

```python
import functools
import math

import numpy as np
import jax
import jax.numpy as jnp
from jax import lax
from jax.experimental import pallas as pl
from jax.experimental.pallas import tpu as pltpu

F32, BF16 = jnp.float32, jnp.bfloat16
MESH = pl.DeviceIdType.MESH
ANY = pl.BlockSpec(memory_space=pl.ANY)
SMEM = pl.BlockSpec(memory_space=pltpu.SMEM)

D = 1024
DEPTH = 4
BLK = 128
N_META = 16
NPAD = 112
NH = 8
LW = 512
DFF = 2816
EPS = 1e-6
NEG = -1e30
SCALE = 0.125
LRU_C = 8.0
REL_BUCKETS = 32
N_SHARD = 4
QA, QF, KF, VF, XC, YC, GT, KA, VA, FL, INP = 0, 512, 1024, 1536, 2048, 2560, 3072, 6144, 6272, 6400, 6656
IN_COLS = 6408
VMEM_LIMIT = 48 * 1024 * 1024

ADAM_LR, ADAM_B1, ADAM_B2, ADAM_EPS, ADAM_WD, ADAM_STEP = 0.001, 0.9, 0.999, 1e-08, 0.01, 10


def _cp(*sem):
    return pltpu.CompilerParams(dimension_semantics=sem or None, vmem_limit_bytes=VMEM_LIMIT)


def _pick(n, prefs):
    for p in prefs:
        if n % p == 0:
            return p
    return n


def _rt(T):
    return _pick(T, (384, 128))


def _sigmoid(z):
    return 1.0 / (1.0 + jnp.exp(-z))


def _log_sigmoid(z):
    return jnp.minimum(z, 0.0) - jnp.log(1.0 + jnp.exp(-jnp.abs(z)))


def _gelu(y):
    c = math.sqrt(2.0 / math.pi)
    return 0.5 * y * (1.0 + jnp.tanh(c * (y + 0.044715 * y * y * y)))


def _gelu_grad(y):
    c = math.sqrt(2.0 / math.pi)
    t = jnp.tanh(c * (y + 0.044715 * y * y * y))
    return 0.5 * (1.0 + t) + 0.5 * y * (1.0 - t * t) * c * (1.0 + 3.0 * 0.044715 * y * y)


def _neg_expm1(z):
    series = -z * (1.0 + z * (0.5 + z * (1.0 / 6.0 + z * (1.0 / 24.0 + z * (1.0 / 120.0)))))
    return jnp.where(z > -0.1, series, 1.0 - jnp.exp(z))


def _dot(a, b, ca, cb):
    return lax.dot_general(a, b, (((ca,), (cb,)), ((), ())), preferred_element_type=F32)


def _mm(a, b, *, ta=False, tb=False, res=None, out_dtype=F32, tm, tn, tk, name, slab=None, b_k0=0, col0=0, cols=None):
    M, K = (a.shape[1], a.shape[0]) if ta else a.shape
    N = b.shape[0] if tb else b.shape[1]
    assert (b.shape[1] if tb else b.shape[0]) >= K + b_k0 and M % tm == 0 and N % tn == 0 and K % tk == 0, (name, a.shape, b.shape)
    assert b_k0 % tk == 0 and col0 % tn == 0
    nk, kb, jb = K // tk, b_k0 // tk, col0 // tn
    ca, cb = (0 if ta else 1), (1 if tb else 0)
    n_in = 2 + (res is not None) + (slab is not None and slab[0] is not None)

    def body(*refs):
        a_ref, b_ref = refs[:2]
        r_ref = refs[2] if res is not None else None
        o_ref = refs[n_in]
        part = _dot(a_ref[...].astype(BF16), b_ref[...].astype(BF16), ca, cb)

        def fin(acc):
            if res is not None:
                acc = acc + r_ref[...]
            o_ref[...] = acc.astype(out_dtype)

        if nk == 1:
            fin(part)
        else:
            acc_ref = refs[-1]
            k = pl.program_id(2)

            @pl.when(k == 0)
            def _():
                acc_ref[...] = part

            @pl.when(k > 0)
            def _():
                acc_ref[...] += part

            @pl.when(k == nk - 1)
            def _():
                fin(acc_ref[...])

    a_spec = pl.BlockSpec((tk, tm), lambda i, j, k: (k, i)) if ta else pl.BlockSpec((tm, tk), lambda i, j, k: (i, k))
    b_spec = (pl.BlockSpec((tn, tk), lambda i, j, k: (j, k + kb)) if tb
              else pl.BlockSpec((tk, tn), lambda i, j, k: (k + kb, j)))
    o_spec = pl.BlockSpec((tm, tn), lambda i, j, k: (i, j))
    in_specs, ops = [a_spec, b_spec], [a, b]
    if res is not None:
        in_specs.append(o_spec)
        ops.append(res)
    out_shape, aliases = jax.ShapeDtypeStruct((M, N), out_dtype), {}
    if slab is not None:
        buf, idx, n = slab
        o_spec = pl.BlockSpec((None, tm, tn), lambda i, j, k: (idx, i, j + jb))
        out_shape = jax.ShapeDtypeStruct((n, M, cols or N), out_dtype)
        if buf is not None:
            aliases = {len(ops): 0}
            in_specs.append(ANY)
            ops.append(buf)
    return pl.pallas_call(
        body, grid=(M // tm, N // tn, nk), in_specs=in_specs, out_specs=o_spec, out_shape=out_shape,
        input_output_aliases=aliases, scratch_shapes=[pltpu.VMEM((tm, tn), F32)] if nk > 1 else [],
        compiler_params=_cp("parallel", "parallel", "arbitrary"), name=name)(*ops)


def _transpose(x, name):
    T, C = x.shape
    tr, tc = _rt(T), _pick(C, (1408, 1024, 512, 256, 128))

    def body(x_ref, o_ref):
        o_ref[...] = x_ref[...].T

    return pl.pallas_call(
        body, grid=(T // tr, C // tc), in_specs=[pl.BlockSpec((tr, tc), lambda i, j: (i, j))],
        out_specs=pl.BlockSpec((tc, tr), lambda i, j: (j, i)), out_shape=jax.ShapeDtypeStruct((C, T), x.dtype),
        compiler_params=_cp("parallel", "parallel"), name=name)(x)


def _rms_fwd(h, g, name):
    T = h.shape[0]
    tr = _rt(T)

    def body(h_ref, g_ref, u_ref):
        x = h_ref[...]
        r = lax.rsqrt(jnp.mean(x * x, axis=-1, keepdims=True) + EPS)
        u_ref[...] = (x * r * g_ref[...]).astype(BF16)

    return pl.pallas_call(
        body, grid=(T // tr,),
        in_specs=[pl.BlockSpec((tr, D), lambda i: (i, 0)), pl.BlockSpec((1, D), lambda i: (0, 0))],
        out_specs=pl.BlockSpec((tr, D), lambda i: (i, 0)), out_shape=jax.ShapeDtypeStruct((T, D), BF16),
        compiler_params=_cp("parallel"), name=name)(h, g.reshape(1, D))


def _rms_bwd(du, h, g, dres, name):
    T = h.shape[0]
    tr = _rt(T)

    def body(du_ref, h_ref, g_ref, dres_ref, dh_ref, dhb_ref, dg_ref):
        x = h_ref[...]
        r = lax.rsqrt(jnp.mean(x * x, axis=-1, keepdims=True) + EPS)
        xh = x * r
        dy = du_ref[...]
        dxh = dy * g_ref[...]
        dx = r * (dxh - xh * jnp.mean(dxh * xh, axis=-1, keepdims=True))
        dh = dres_ref[...] + dx
        dh_ref[...] = dh
        dhb_ref[...] = dh.astype(BF16)
        part = jnp.sum(dy * xh, axis=0, keepdims=True)

        @pl.when(pl.program_id(0) == 0)
        def _():
            dg_ref[...] = part

        @pl.when(pl.program_id(0) > 0)
        def _():
            dg_ref[...] += part

    row = pl.BlockSpec((tr, D), lambda i: (i, 0))
    vec = pl.BlockSpec((1, D), lambda i: (0, 0))
    return pl.pallas_call(
        body, grid=(T // tr,), in_specs=[row, row, vec, row], out_specs=[row, row, vec],
        out_shape=[jax.ShapeDtypeStruct((T, D), F32), jax.ShapeDtypeStruct((T, D), BF16), jax.ShapeDtypeStruct((1, D), F32)],
        compiler_params=_cp("arbitrary"), name=name)(du, h, g.reshape(1, D), dres)


def _loss_head(h, tgt, g, name):
    T = h.shape[0]
    nb = T // BLK

    def body(h_ref, t_ref, g_ref, dh_ref, dhb_ref, dg_ref, loss_ref):
        i = pl.program_id(0)
        x = h_ref[...]
        r = lax.rsqrt(jnp.mean(x * x, axis=-1, keepdims=True) + EPS)
        xh = x * r
        gv = g_ref[...]
        tok = i >= 1
        err = jnp.where(tok, xh * gv - t_ref[...], 0.0)
        dy = err * (1.0 / D)
        dxh = dy * gv
        dx = r * (dxh - xh * jnp.mean(dxh * xh, axis=-1, keepdims=True))
        dh_ref[...] = dx
        dhb_ref[...] = dx.astype(BF16)
        dg = jnp.sum(dy * xh, axis=0, keepdims=True)
        ls = jnp.zeros((1, BLK), F32) + jnp.sum(err * err) * (0.5 / D)

        @pl.when(i == 0)
        def _():
            dg_ref[...] = dg
            loss_ref[...] = ls

        @pl.when(i > 0)
        def _():
            dg_ref[...] += dg
            loss_ref[...] += ls

    row = pl.BlockSpec((BLK, D), lambda i: (i, 0))
    vec = pl.BlockSpec((1, D), lambda i: (0, 0))
    return pl.pallas_call(
        body, grid=(nb,),
        in_specs=[row, pl.BlockSpec((BLK, D), lambda i: (jnp.maximum(i - 1, 0), 0)), vec],
        out_specs=[row, row, vec, pl.BlockSpec((1, BLK), lambda i: (0, 0))],
        out_shape=[jax.ShapeDtypeStruct((T, D), F32), jax.ShapeDtypeStruct((T, D), BF16),
                   jax.ShapeDtypeStruct((1, D), F32), jax.ShapeDtypeStruct((1, BLK), F32)],
        compiler_params=_cp("arbitrary"), name=name)(h, tgt, g.reshape(1, D))


def _bucket_table():
    q = np.arange(BLK)[:, None]
    k = np.arange(2 * BLK)[None, :]
    d = np.maximum(q + BLK - k, 0)
    max_exact = REL_BUCKETS // 2
    scaled = np.log(np.maximum(d, 1).astype(np.float32) / np.float32(max_exact)) / np.float32(math.log(128 / max_exact))
    large = np.minimum(max_exact + (scaled.astype(np.float32) * (REL_BUCKETS - max_exact)).astype(np.int32), REL_BUCKETS - 1)
    return np.where(d < max_exact, d, large).astype(np.int32)


def _bias_build(table, bucket, name):
    def body(t_ref, bk_ref, o_ref):
        bk = bk_ref[...]
        for h in range(NH):
            acc = jnp.zeros((BLK, 2 * BLK), F32)
            for b in range(REL_BUCKETS):
                acc = jnp.where(bk == b, t_ref[b, h], acc)
            o_ref[h] = acc

    return pl.pallas_call(
        body, in_specs=[SMEM, pl.BlockSpec(memory_space=pltpu.VMEM)], out_specs=pl.BlockSpec(memory_space=pltpu.VMEM),
        out_shape=jax.ShapeDtypeStruct((NH, BLK, 2 * BLK), F32), compiler_params=_cp(), name=name)(table, bucket)


def _bias_bwd(dbias, bucket, name):
    def body(d_ref, bk_ref, o_ref):
        bk = bk_ref[...]
        for h in range(NH):
            dh = d_ref[h]
            for b in range(REL_BUCKETS):
                o_ref[b, h] = jnp.sum(jnp.where(bk == b, dh, 0.0))

    return pl.pallas_call(
        body, in_specs=[pl.BlockSpec(memory_space=pltpu.VMEM)] * 2, out_specs=SMEM,
        out_shape=jax.ShapeDtypeStruct((REL_BUCKETS, NH), F32), compiler_params=_cp(), name=name)(dbias, bucket)


def _swa_specs(nq_cols):
    prev = lambda n: jnp.maximum(n - 1, 0)
    return [
        pl.BlockSpec((BLK, nq_cols), lambda n: (n, QA // nq_cols)),
        pl.BlockSpec((BLK, BLK), lambda n: (prev(n), KA // BLK)), pl.BlockSpec((BLK, BLK), lambda n: (n, KA // BLK)),
        pl.BlockSpec((BLK, BLK), lambda n: (prev(n), VA // BLK)), pl.BlockSpec((BLK, BLK), lambda n: (n, VA // BLK)),
    ]


def _swa_mask(n):
    row = lax.broadcasted_iota(jnp.int32, (BLK, 2 * BLK), 0)
    col = lax.broadcasted_iota(jnp.int32, (BLK, 2 * BLK), 1)
    dist = row + BLK - col
    return (dist >= 0) & (dist < BLK) & ((n - 1) * BLK + col >= NPAD)


def _swa_probs(qm, ksel, mask, bias_h, sink):
    s = _dot(qm, ksel, 1, 1) * SCALE
    s = jnp.where(mask, s + bias_h, NEG)
    m = jnp.maximum(jnp.max(s, axis=-1, keepdims=True), sink)
    p = jnp.exp(s - m)
    psink = jnp.exp(sink - m)
    inv = 1.0 / (jnp.sum(p, axis=-1, keepdims=True) + psink)
    return p * inv, psink * inv


def _swa_fwd(proj, bias, sinks, name):
    T = proj.shape[0]
    nb = T // BLK

    def body(sk_ref, q_ref, kp_ref, kc_ref, vp_ref, vc_ref, b_ref, o_ref):
        n = pl.program_id(0)
        lo = lax.broadcasted_iota(jnp.int32, (1, BLK), 1) < 64
        kb = jnp.concatenate([kp_ref[...], kc_ref[...]], axis=0)
        vb = jnp.concatenate([vp_ref[...], vc_ref[...]], axis=0)
        kbs = (kb.astype(BF16), pltpu.roll(kb, 64, 1).astype(BF16))
        vbs = (vb, pltpu.roll(vb, 64, 1))
        mask = _swa_mask(n)
        outs = []
        for pr in range(NH // 2):
            qp = q_ref[:, pr * BLK:(pr + 1) * BLK]
            kv = pr // 2
            acc = jnp.zeros((BLK, BLK), F32)
            for e in range(2):
                lm = lo if e == 0 else jnp.logical_not(lo)
                sw = 0 if kv == e else 1
                qm = jnp.where(lm, qp, 0.0).astype(BF16)
                pn, _ = _swa_probs(qm, kbs[sw], mask, b_ref[2 * pr + e], sk_ref[2 * pr + e])
                acc = acc + _dot(pn.astype(BF16), jnp.where(lm, vbs[sw], 0.0).astype(BF16), 1, 0)
            outs.append(acc)
        o_ref[...] = jnp.concatenate(outs, axis=1).astype(BF16)

    return pl.pallas_call(
        body, grid=(nb,),
        in_specs=[SMEM] + _swa_specs(512) + [pl.BlockSpec((NH, BLK, 2 * BLK), lambda n: (0, 0, 0))],
        out_specs=pl.BlockSpec((BLK, 512), lambda n: (n, 0)), out_shape=jax.ShapeDtypeStruct((T, 512), BF16),
        compiler_params=_cp("parallel"), name=name)(sinks, proj, proj, proj, proj, proj, bias)


def _swa_bwd(proj, bias, sinks, do, dbias_in, name):
    T = proj.shape[0]
    nb = T // BLK

    def body(sk_ref, q_ref, kp_ref, kc_ref, vp_ref, vc_ref, b_ref, do_ref, dbi_ref,
             dq_ref, dk_ref, dv_ref, db_ref, dsk_ref, sk_acc):
        n = pl.program_id(0)
        lane = lax.broadcasted_iota(jnp.int32, (1, BLK), 1)
        lo = lane < 64
        kb = jnp.concatenate([kp_ref[...], kc_ref[...]], axis=0)
        vb = jnp.concatenate([vp_ref[...], vc_ref[...]], axis=0)
        kbs = (kb, pltpu.roll(kb, 64, 1))
        vbs = (vb, pltpu.roll(vb, 64, 1))
        mask = _swa_mask(n)

        @pl.when(n == 0)
        def _():
            db_ref[...] = dbi_ref[...]
            sk_acc[...] = jnp.zeros_like(sk_acc)

        dqs = []
        dk = jnp.zeros((2 * BLK, BLK), F32)
        dv = jnp.zeros((2 * BLK, BLK), F32)
        for pr in range(NH // 2):
            qp = q_ref[:, pr * BLK:(pr + 1) * BLK]
            dop = do_ref[:, pr * BLK:(pr + 1) * BLK].astype(F32)
            kv = pr // 2
            dq = jnp.zeros((BLK, BLK), F32)
            for e in range(2):
                h = 2 * pr + e
                lm = lo if e == 0 else jnp.logical_not(lo)
                sw = 0 if kv == e else 1
                qm = jnp.where(lm, qp, 0.0)
                dom = jnp.where(lm, dop, 0.0)
                pn, ps = _swa_probs(qm.astype(BF16), kbs[sw].astype(BF16), mask, b_ref[h], sk_ref[h])
                dp = _dot(dom.astype(BF16), vbs[sw].astype(BF16), 1, 1)
                delta = jnp.sum(pn * dp, axis=-1, keepdims=True)
                ds = pn * (dp - delta)
                db_ref[h] += ds
                sk_acc[...] += jnp.where(lane == h, -(ps * delta), 0.0)
                dsb = (ds * SCALE).astype(BF16)
                dq = dq + _dot(dsb, jnp.where(lm, kbs[sw], 0.0).astype(BF16), 1, 0)
                qk = qm if sw == 0 else pltpu.roll(qm, 64, 1)
                dok = dom if sw == 0 else pltpu.roll(dom, 64, 1)
                dk = dk + _dot(dsb, qk.astype(BF16), 0, 0)
                dv = dv + _dot(pn.astype(BF16), dok.astype(BF16), 0, 0)
            dqs.append(dq)
        dq_ref[...] = jnp.concatenate(dqs, axis=1).astype(BF16)
        dk_ref[0] = dk
        dv_ref[0] = dv

        @pl.when(n == nb - 1)
        def _():
            dsk_ref[...] = jnp.sum(sk_acc[...], axis=0, keepdims=True)

    full_b = pl.BlockSpec((NH, BLK, 2 * BLK), lambda n: (0, 0, 0))
    band = pl.BlockSpec((1, 2 * BLK, BLK), lambda n: (n, 0, 0))
    return pl.pallas_call(
        body, grid=(nb,),
        in_specs=[SMEM] + _swa_specs(512) + [full_b, pl.BlockSpec((BLK, 512), lambda n: (n, 0)), full_b],
        out_specs=[pl.BlockSpec((BLK, 512), lambda n: (n, 0)), band, band, full_b, pl.BlockSpec((1, BLK), lambda n: (0, 0))],
        out_shape=[jax.ShapeDtypeStruct((T, 512), BF16), jax.ShapeDtypeStruct((nb, 2 * BLK, BLK), F32),
                   jax.ShapeDtypeStruct((nb, 2 * BLK, BLK), F32), jax.ShapeDtypeStruct((NH, BLK, 2 * BLK), F32),
                   jax.ShapeDtypeStruct((1, BLK), F32)],
        scratch_shapes=[pltpu.VMEM((BLK, BLK), F32)],
        compiler_params=_cp("arbitrary"), name=name)(sinks, proj, proj, proj, proj, proj, bias, do, dbias_in)


def _band_fold(dkb, dvb, name):
    nb = dkb.shape[0]

    def body(ko_ref, kn_ref, vo_ref, vn_ref, dk_ref, dv_ref):
        last = pl.program_id(0) == nb - 1
        dk_ref[...] = (ko_ref[0] + jnp.where(last, 0.0, kn_ref[0])).astype(BF16)
        dv_ref[...] = (vo_ref[0] + jnp.where(last, 0.0, vn_ref[0])).astype(BF16)

    own = pl.BlockSpec((1, BLK, BLK), lambda j: (j, 1, 0))
    nxt = pl.BlockSpec((1, BLK, BLK), lambda j: (jnp.minimum(j + 1, nb - 1), 0, 0))
    out = pl.BlockSpec((BLK, BLK), lambda j: (j, 0))
    return pl.pallas_call(
        body, grid=(nb,), in_specs=[own, nxt, own, nxt], out_specs=[out, out],
        out_shape=[jax.ShapeDtypeStruct((nb * BLK, BLK), BF16)] * 2,
        compiler_params=_cp("parallel"), name=name)(dkb, dkb, dvb, dvb)


def _cum_fwd(z3, fb, name):
    nb = z3.shape[0]

    def body(z_ref, fb_ref, c_ref):
        lane = lax.broadcasted_iota(jnp.int32, (NH, BLK), 1)

        def step(b, carry):
            x = jnp.where(b * BLK + lane >= NPAD, _log_sigmoid(z_ref[b] + fb_ref[...]), 0.0)
            s = 1
            while s < BLK:
                x = x + jnp.where(lane >= s, pltpu.roll(x, s, 1), 0.0)
                s *= 2
            x = x + carry
            c_ref[b] = x
            return jnp.sum(jnp.where(lane == BLK - 1, x, 0.0), axis=-1, keepdims=True)

        lax.fori_loop(0, nb, step, jnp.zeros((NH, 1), F32))

    return pl.pallas_call(body, out_shape=jax.ShapeDtypeStruct((nb, NH, BLK), F32), compiler_params=_cp(), name=name)(z3, fb)


def _cum_bwd(dck3, dcq3, z3, fb, name):
    nb = z3.shape[0]

    def body(d_ref, dq_ref, z_ref, fb_ref, dz_ref, db_ref):
        lane = lax.broadcasted_iota(jnp.int32, (NH, BLK), 1)

        def step(k, carry):
            suffix, tot = carry
            b = nb - 1 - k
            x = dq_ref[b] - d_ref[b]
            s = 1
            while s < BLK:
                x = x + jnp.where(lane < BLK - s, pltpu.roll(x, BLK - s, 1), 0.0)
                s *= 2
            x = x + suffix
            dz = jnp.where(b * BLK + lane >= NPAD, x * _sigmoid(-(z_ref[b] + fb_ref[...])), 0.0)
            dz_ref[b] = dz
            return (jnp.sum(jnp.where(lane == 0, x, 0.0), axis=-1, keepdims=True),
                    tot + jnp.sum(dz, axis=-1, keepdims=True))

        z0 = jnp.zeros((NH, 1), F32)
        _, tot = lax.fori_loop(0, nb, step, (z0, z0))
        db_ref[...] = jnp.zeros((NH, BLK), F32) + tot

    return pl.pallas_call(
        body, out_shape=[jax.ShapeDtypeStruct((nb, NH, BLK), F32), jax.ShapeDtypeStruct((NH, BLK), F32)],
        compiler_params=_cp(), name=name)(dck3, dcq3, z3, fb)


def _to_blocks(a):
    return a.reshape(NH, -1, BLK).transpose(1, 0, 2)


def _from_blocks(a):
    return a.transpose(1, 0, 2).reshape(NH, -1)


def _fox_prep(proj, ccol, name):
    T = proj.shape[0]
    tr = _rt(T)

    def body(q_ref, k_ref, v_ref, cc_ref, qa_ref, ka_ref, kt_ref, vm_ref, vo_ref):
        h = pl.program_id(1)
        lane = lax.broadcasted_iota(jnp.int32, (1, BLK), 1)
        own = (lane >> 6) == (h & 1)
        a0 = 64 * (1 - (h & 1))
        c = _lane_pick(cc_ref[...], lane, h)
        hi = c.astype(BF16).astype(F32)
        mid = (c - hi).astype(BF16).astype(F32)
        lo = (c - hi - mid).astype(BF16).astype(F32)
        ones = (lane >= a0 + 3) & (lane < a0 + 6)
        qa = jnp.where(own, q_ref[...] * SCALE, jnp.where(ones, 1.0, 0.0))
        qa = jnp.where(lane == a0, hi, jnp.where(lane == a0 + 1, mid, jnp.where(lane == a0 + 2, lo, qa)))
        ones = (lane >= a0) & (lane < a0 + 3)
        ka = jnp.where(own, k_ref[...], jnp.where(ones, 1.0, 0.0))
        ka = jnp.where(lane == a0 + 3, -hi, jnp.where(lane == a0 + 4, -mid, jnp.where(lane == a0 + 5, -lo, ka)))
        qa_ref[...] = qa.astype(BF16)
        kab = ka.astype(BF16)
        ka_ref[...] = kab
        kt_ref[...] = kab.T
        vm = jnp.where(own, v_ref[...], 0.0)
        vm_ref[...] = vm.astype(BF16)
        vo_ref[...] = jnp.where(lane == a0, 1.0, vm).astype(BF16)

    pair = lambda col0: pl.BlockSpec((tr, BLK), lambda i, h: (i, col0 // BLK + (h >> 1)))
    out = pl.BlockSpec((None, tr, BLK), lambda i, h: (h, i, 0))
    out_t = pl.BlockSpec((None, BLK, tr), lambda i, h: (h, 0, i))
    tok = jax.ShapeDtypeStruct((NH, T, BLK), BF16)
    return pl.pallas_call(
        body, grid=(T // tr, NH), in_specs=[pair(QF), pair(KF), pair(VF), pl.BlockSpec((tr, BLK), lambda i, h: (i, 0))],
        out_specs=[out, out, out_t, out, out], out_shape=[tok, tok, jax.ShapeDtypeStruct((NH, BLK, T), BF16), tok, tok],
        compiler_params=_cp("parallel", "arbitrary"), name=name)(proj, proj, proj, ccol)


def _fox_fwd(qaug, kaug_t, vo, name):
    T = qaug.shape[1]
    t = _rt(T)
    nt = T // t

    def body(q0, q1, k0, k1, v0, v1, o_ref, lse0_ref, lse1_ref, m_ref, acc_ref):
        i, j = pl.program_id(1), pl.program_id(2)
        lane = lax.broadcasted_iota(jnp.int32, (1, BLK), 1)
        lo = lane < 64

        @pl.when(j == 0)
        def _():
            m_ref[...] = jnp.full_like(m_ref, NEG)
            acc_ref[...] = jnp.zeros_like(acc_ref)

        def step(masked):
            for e, (q_ref, k_ref, v_ref) in enumerate(((q0, k0, v0), (q1, k1, v1))):
                s = _dot(q_ref[...], k_ref[...], 1, 0)
                if masked:
                    s = jnp.where(_fox_mask(i, j, t), s, NEG)
                m_old = m_ref[e]
                m_new = jnp.maximum(m_old, jnp.max(s, axis=-1, keepdims=True))
                m_ref[e] = m_new
                pe = jnp.exp(s - jnp.concatenate([m_new] * (t // BLK), axis=1))
                acc_ref[e] = jnp.exp(m_old - m_new) * acc_ref[e] + _dot(pe.astype(BF16), v_ref[...], 1, 0)

        pl.when((j < i) & (j > 0))(lambda: step(False))
        pl.when((j == i) | ((j == 0) & (i > 0)))(lambda: step(True))

        @pl.when(j == i)
        def _():
            rows = i * t + lax.broadcasted_iota(jnp.int32, (t, 1), 0)
            l0, l1 = _lane_pick(acc_ref[0], lane, 64), _lane_pick(acc_ref[1], lane, 0)
            o = jnp.where(lo, acc_ref[0] / l0, acc_ref[1] / l1)
            o_ref[...] = jnp.where(rows >= NPAD, o, 0.0).astype(BF16)
            lse0_ref[...] = m_ref[0] + jnp.log(l0)
            lse1_ref[...] = m_ref[1] + jnp.log(l1)

    kj = lambda i, j: jnp.minimum(j, i)
    qs = lambda e: pl.BlockSpec((None, t, BLK), lambda p, i, j: (2 * p + e, i, 0))
    ks = lambda e: pl.BlockSpec((None, t, BLK), lambda p, i, j: (2 * p + e, kj(i, j), 0))
    kts = lambda e: pl.BlockSpec((None, BLK, t), lambda p, i, j: (2 * p + e, 0, kj(i, j)))
    rep = pl.BlockSpec((None, t, BLK), lambda p, i, j: (p, i, 0))
    return pl.pallas_call(
        body, grid=(NH // 2, nt, nt), in_specs=[qs(0), qs(1), kts(0), kts(1), ks(0), ks(1)],
        out_specs=[pl.BlockSpec((t, BLK), lambda p, i, j: (i, p)), rep, rep],
        out_shape=[jax.ShapeDtypeStruct((T, 512), BF16)] + [jax.ShapeDtypeStruct((NH // 2, T, BLK), F32)] * 2,
        scratch_shapes=[pltpu.VMEM((2, t, BLK), F32), pltpu.VMEM((2, t, BLK), F32)],
        compiler_params=_cp("parallel", "parallel", "arbitrary"), name=name)(qaug, qaug, kaug_t, kaug_t, vo, vo)


def _fox_delta(do, o, name):
    T = do.shape[0]
    tr = _rt(T)

    def body(do_ref, o_ref, d0_ref, d1_ref):
        lo = lax.broadcasted_iota(jnp.int32, (1, BLK), 1) < 64
        prod = do_ref[...].astype(F32) * o_ref[...].astype(F32)
        d0_ref[...] = jnp.zeros((tr, BLK), F32) + jnp.sum(jnp.where(lo, prod, 0.0), axis=-1, keepdims=True)
        d1_ref[...] = jnp.zeros((tr, BLK), F32) + jnp.sum(jnp.where(lo, 0.0, prod), axis=-1, keepdims=True)

    blk = pl.BlockSpec((tr, BLK), lambda i, p: (i, p))
    rep = pl.BlockSpec((None, tr, BLK), lambda i, p: (p, i, 0))
    return pl.pallas_call(
        body, grid=(T // tr, NH // 2), in_specs=[blk, blk], out_specs=[rep, rep],
        out_shape=[jax.ShapeDtypeStruct((NH // 2, T, BLK), F32)] * 2,
        compiler_params=_cp("parallel", "parallel"), name=name)(do, o)


def _fox_bwd(qaug, kaug, kaug_t, vm, do, lses, deltas, name):
    T = qaug.shape[1]
    t = _rt(T)
    nt = T // t

    def body(q0, q1, k0, k1, kt0, kt1, v0, v1, do_ref, lse0, lse1, dl0, dl1,
             dq_ref, dqx_ref, dk_ref, dv_ref, dkx_ref, dq_acc, dk_acc, dv_acc):
        j, i = pl.program_id(1), pl.program_id(2)
        lane = lax.broadcasted_iota(jnp.int32, (1, BLK), 1)
        lo = lane < 64

        @pl.when((j == 0) & (i == 0))
        def _():
            dq_acc[...] = jnp.zeros_like(dq_acc)

        @pl.when(i == 0)
        def _():
            dk_acc[...] = jnp.zeros_like(dk_acc)
            dv_acc[...] = jnp.zeros_like(dv_acc)

        def step(masked):
            dob = do_ref[...]
            rows = pl.ds(pl.multiple_of(i * t, t), t)
            wide = lambda ref: jnp.concatenate([ref[...]] * (t // BLK), axis=1)
            for e, (q_ref, k_ref, kt_ref, v_ref, lse_ref, dl_ref) in enumerate(
                    ((q0, k0, kt0, v0, lse0, dl0), (q1, k1, kt1, v1, lse1, dl1))):
                s = _dot(q_ref[...], kt_ref[...], 1, 0)
                if masked:
                    s = jnp.where(_fox_mask(i, j, t), s, NEG)
                pe = jnp.exp(s - wide(lse_ref))
                dp = _dot(dob, v_ref[...], 1, 1)
                ds = (pe * (dp - wide(dl_ref))).astype(BF16)
                dq_acc[e, rows, :] += _dot(ds, k_ref[...], 1, 0)
                dk_acc[e] += _dot(ds, q_ref[...], 0, 0)
                dv_acc[e] += _dot(pe.astype(BF16), dob, 0, 0)

        pl.when((i > j) & (j > 0))(lambda: step(False))
        pl.when((i == j) | ((j == 0) & (i > 0)))(lambda: step(True))

        @pl.when(i == nt - 1)
        def _():
            dk_ref[...] = jnp.where(lo, dk_acc[0], dk_acc[1]).astype(BF16)
            dv_ref[...] = jnp.where(lo, dv_acc[0], dv_acc[1]).astype(BF16)
            dkx_ref[...] = jnp.where(lo, dk_acc[1], dk_acc[0])

        @pl.when((i == nt - 1) & (j == nt - 1))
        def _():
            dq_ref[...] = (jnp.where(lo, dq_acc[0], dq_acc[1]) * SCALE).astype(BF16)
            dqx_ref[...] = jnp.where(lo, dq_acc[1], dq_acc[0])

    qi = lambda j, i: jnp.maximum(i, j)
    qs = lambda e: pl.BlockSpec((None, t, BLK), lambda p, j, i: (2 * p + e, qi(j, i), 0))
    ks = lambda e: pl.BlockSpec((None, t, BLK), lambda p, j, i: (2 * p + e, j, 0))
    kts = lambda e: pl.BlockSpec((None, BLK, t), lambda p, j, i: (2 * p + e, 0, j))
    qside = pl.BlockSpec((t, BLK), lambda p, j, i: (qi(j, i), p))
    kside = pl.BlockSpec((t, BLK), lambda p, j, i: (j, p))
    rep = pl.BlockSpec((None, t, BLK), lambda p, j, i: (p, qi(j, i), 0))
    whole = pl.BlockSpec((T, BLK), lambda p, j, i: (0, p))
    return pl.pallas_call(
        body, grid=(NH // 2, nt, nt),
        in_specs=[qs(0), qs(1), ks(0), ks(1), kts(0), kts(1), ks(0), ks(1), qside, rep, rep, rep, rep],
        out_specs=[whole, whole, kside, kside, kside],
        out_shape=[jax.ShapeDtypeStruct((T, 512), BF16), jax.ShapeDtypeStruct((T, 512), F32),
                   jax.ShapeDtypeStruct((T, 512), BF16), jax.ShapeDtypeStruct((T, 512), BF16),
                   jax.ShapeDtypeStruct((T, 512), F32)],
        scratch_shapes=[pltpu.VMEM((2, T, BLK), F32), pltpu.VMEM((2, t, BLK), F32), pltpu.VMEM((2, t, BLK), F32)],
        compiler_params=_cp("parallel", "arbitrary", "arbitrary"), name=name)(
            qaug, qaug, kaug, kaug, kaug_t, kaug_t, vm, vm, do, *lses, *deltas)


def _fox_mask(i, j, t):
    row = i * t + lax.broadcasted_iota(jnp.int32, (t, t), 0)
    col = j * t + lax.broadcasted_iota(jnp.int32, (t, t), 1)
    return (col <= row) & (col >= NPAD)


def _lane_pick(x, lane, idx):
    return jnp.sum(jnp.where(lane == idx, x, 0.0), axis=-1, keepdims=True)


def _lru_gates(xc, wr_ref, wi_ref, vec_ref):
    xb = xc.astype(BF16)
    pre_r = jnp.concatenate([_dot(xb[:, p * BLK:(p + 1) * BLK], wr_ref[p], 1, 0) for p in range(LW // BLK)], axis=1)
    pre_i = jnp.concatenate([_dot(xb[:, p * BLK:(p + 1) * BLK], wi_ref[p], 1, 0) for p in range(LW // BLK)], axis=1)
    r = _sigmoid(pre_r + vec_ref[0:1, :])
    gi = _sigmoid(pre_i + vec_ref[1:2, :])
    log_a = LRU_C * r * _log_sigmoid(vec_ref[2:3, :])
    a = jnp.exp(log_a)
    mult = jnp.sqrt(_neg_expm1(2.0 * log_a))
    return r, gi, a, mult


def _conv(xbuf_ref, x, cw_ref, vec_ref, tr):
    return (cw_ref[3:4, :] * x + cw_ref[2:3, :] * xbuf_ref[7:7 + tr, :] + cw_ref[1:2, :] * xbuf_ref[6:6 + tr, :]
            + cw_ref[0:1, :] * xbuf_ref[5:5 + tr, :] + vec_ref[3:4, :])


def _lru_fwd(proj, cw, wr, wi, vec, name):
    T = proj.shape[0]
    tr = _rt(T)

    def body(x_ref, y_ref, cw_ref, wr_ref, wi_ref, vec_ref, oc_ref, hs_ref, xbuf, abuf, bbuf, hcar):
        i = pl.program_id(0)

        @pl.when(i == 0)
        def _():
            xbuf[0:8, :] = jnp.zeros((8, LW), F32)
            hcar[...] = jnp.zeros_like(hcar)

        x = x_ref[...]
        xbuf[8:8 + tr, :] = x
        xc = _conv(xbuf, x, cw_ref, vec_ref, tr)
        xbuf[0:8, :] = x[tr - 8:tr, :]
        _, gi, a, mult = _lru_gates(xc, wr_ref, wi_ref, vec_ref)
        rows = i * tr + lax.broadcasted_iota(jnp.int32, (tr, 1), 0)
        abuf[...] = a
        bbuf[...] = jnp.where(rows >= NPAD, mult * (gi * xc), 0.0)
        sub = lax.broadcasted_iota(jnp.int32, (8, 1), 0)

        def step(k, h):
            sl = pl.ds(pl.multiple_of(k * 8, 8), 8)
            a8, b8 = abuf[sl, :], bbuf[sl, :]
            for s in (1, 2, 4):
                ok = sub >= s
                b8 = jnp.where(ok, a8 * pltpu.roll(b8, s, 0) + b8, b8)
                a8 = jnp.where(ok, a8 * pltpu.roll(a8, s, 0), a8)
            h8 = a8 * h + b8
            bbuf[sl, :] = h8
            return h8[7:8, :]

        hcar[...] = lax.fori_loop(0, tr // 8, step, hcar[...])
        hs = bbuf[...]
        hs_ref[...] = hs
        oc_ref[...] = (hs * _gelu(y_ref[...])).astype(BF16)

    row = pl.BlockSpec((tr, LW), lambda i: (i, 0))
    full = lambda shape: pl.BlockSpec(shape, lambda i: (0,) * len(shape))
    return pl.pallas_call(
        body, grid=(T // tr,),
        in_specs=[pl.BlockSpec((tr, LW), lambda i: (i, XC // LW)), pl.BlockSpec((tr, LW), lambda i: (i, YC // LW)),
                  full((4, LW)), full((4, BLK, BLK)), full((4, BLK, BLK)), full((8, LW))],
        out_specs=[row, row], out_shape=[jax.ShapeDtypeStruct((T, LW), BF16), jax.ShapeDtypeStruct((T, LW), F32)],
        scratch_shapes=[pltpu.VMEM((tr + 8, LW), F32), pltpu.VMEM((tr, LW), F32), pltpu.VMEM((tr, LW), F32),
                        pltpu.VMEM((1, LW), F32)],
        compiler_params=_cp("arbitrary"), name=name)(proj, proj, cw, wr, wi, vec)


def _lru_bwd(proj, hs, doc, cw, wr, wi, vec, name):
    T = proj.shape[0]
    tr = _rt(T)
    nt = T // tr
    r8 = tr // 8

    def body(x_ref, xp_ref, y_ref, hs_ref, hp_ref, do_ref, cw_ref, wr_ref, wi_ref, vec_ref,
             dx_ref, dy_ref, dwr_ref, dwi_ref, dvec_ref, xbuf, abuf, gbuf, hbuf, dbuf, gcar, acar):
        k = pl.program_id(0)
        i = nt - 1 - k

        @pl.when(k == 0)
        def _():
            dwr_ref[...] = jnp.zeros_like(dwr_ref)
            dwi_ref[...] = jnp.zeros_like(dwi_ref)
            dvec_ref[...] = jnp.zeros_like(dvec_ref)
            gcar[...] = jnp.zeros_like(gcar)
            acar[...] = jnp.zeros_like(acar)
            dbuf[tr:tr + 8, :] = jnp.zeros((8, LW), F32)

        first = i == 0
        x = x_ref[...]
        xbuf[0:8, :] = jnp.where(first, 0.0, xp_ref[...])
        xbuf[8:8 + tr, :] = x
        xc = _conv(xbuf, x, cw_ref, vec_ref, tr)
        r, gi, a, mult = _lru_gates(xc, wr_ref, wi_ref, vec_ref)
        y = y_ref[...]
        hs = hs_ref[...]
        do_ = do_ref[...].astype(F32)
        rows = i * tr + lax.broadcasted_iota(jnp.int32, (tr, 1), 0)
        abuf[0:tr, :] = a
        abuf[tr:tr + 8, :] = jnp.zeros((8, LW), F32) + acar[...]
        an = abuf[1:1 + tr, :]
        acar[...] = a[0:1, :]
        abuf[0:tr, :] = an
        gbuf[...] = do_ * _gelu(y)
        sub = lax.broadcasted_iota(jnp.int32, (8, 1), 0)

        def step(kk, g):
            sl = pl.ds(pl.multiple_of((r8 - 1 - kk) * 8, 8), 8)
            a8, b8 = abuf[sl, :], gbuf[sl, :]
            for s in (1, 2, 4):
                ok = sub < 8 - s
                b8 = jnp.where(ok, a8 * pltpu.roll(b8, 8 - s, 0) + b8, b8)
                a8 = jnp.where(ok, a8 * pltpu.roll(a8, 8 - s, 0), a8)
            g8 = a8 * g + b8
            gbuf[sl, :] = g8
            return g8[0:1, :]

        gcar[...] = lax.fori_loop(0, r8, step, gcar[...])
        g = gbuf[...]
        hbuf[0:8, :] = jnp.where(first, 0.0, hp_ref[...])
        hbuf[8:8 + tr, :] = hs
        hprev = hbuf[7:7 + tr, :]
        dinp = jnp.where(rows >= NPAD, g, 0.0)
        da = g * hprev
        dmult = dinp * gi * xc
        dgi = dinp * mult * xc
        dxc = dinp * mult * gi
        dlog_a = da * a - dmult * a * a / mult
        ls = _log_sigmoid(vec_ref[2:3, :])
        dpre_r = dlog_a * (LRU_C * ls) * r * (1.0 - r)
        dpre_i = dgi * gi * (1.0 - gi)
        xb = xc.astype(BF16)
        rb, ib = dpre_r.astype(BF16), dpre_i.astype(BF16)
        back = []
        for p in range(LW // BLK):
            c = slice(p * BLK, (p + 1) * BLK)
            back.append(_dot(rb[:, c], wr_ref[p], 1, 1) + _dot(ib[:, c], wi_ref[p], 1, 1))
            dwr_ref[p] += _dot(xb[:, c], rb[:, c], 0, 0)
            dwi_ref[p] += _dot(xb[:, c], ib[:, c], 0, 0)
        dxc = dxc + jnp.concatenate(back, axis=1)
        col = lambda v: jnp.sum(v, axis=0, keepdims=True)
        dvec_ref[0:1, :] += col(dpre_r)
        dvec_ref[1:2, :] += col(dpre_i)
        dvec_ref[2:3, :] += col(dlog_a * (LRU_C * r)) * _sigmoid(-vec_ref[2:3, :])
        dvec_ref[3:4, :] += col(dxc)
        dvec_ref[4:5, :] += col(dxc * xbuf[5:5 + tr, :])
        dvec_ref[5:6, :] += col(dxc * xbuf[6:6 + tr, :])
        dvec_ref[6:7, :] += col(dxc * xbuf[7:7 + tr, :])
        dvec_ref[7:8, :] += col(dxc * x)
        dbuf[0:tr, :] = dxc
        dxr = (cw_ref[3:4, :] * dxc + cw_ref[2:3, :] * dbuf[1:1 + tr, :] + cw_ref[1:2, :] * dbuf[2:2 + tr, :]
               + cw_ref[0:1, :] * dbuf[3:3 + tr, :])
        dbuf[tr:tr + 8, :] = dxc[0:8, :]
        dx_ref[...] = jnp.where(rows >= NPAD, dxr, 0.0).astype(BF16)
        dy_ref[...] = (do_ * hs * _gelu_grad(y)).astype(BF16)

    rev = lambda k: nt - 1 - k
    row = lambda col0: pl.BlockSpec((tr, LW), lambda k: (rev(k), col0))
    prev8 = lambda col0: pl.BlockSpec((8, LW), lambda k: (jnp.maximum(rev(k) * r8 - 1, 0), col0))
    full = lambda shape: pl.BlockSpec(shape, lambda k: (0,) * len(shape))
    return pl.pallas_call(
        body, grid=(nt,),
        in_specs=[row(XC // LW), prev8(XC // LW), row(YC // LW), row(0), prev8(0), row(0),
                  full((4, LW)), full((4, BLK, BLK)), full((4, BLK, BLK)), full((8, LW))],
        out_specs=[row(0), row(0), full((4, BLK, BLK)), full((4, BLK, BLK)), full((8, LW))],
        out_shape=[jax.ShapeDtypeStruct((T, LW), BF16), jax.ShapeDtypeStruct((T, LW), BF16),
                   jax.ShapeDtypeStruct((4, BLK, BLK), F32), jax.ShapeDtypeStruct((4, BLK, BLK), F32),
                   jax.ShapeDtypeStruct((8, LW), F32)],
        scratch_shapes=[pltpu.VMEM((tr + 8, LW), F32), pltpu.VMEM((tr + 8, LW), F32), pltpu.VMEM((tr, LW), F32),
                        pltpu.VMEM((tr + 8, LW), F32), pltpu.VMEM((tr + 8, LW), F32),
                        pltpu.VMEM((1, LW), F32), pltpu.VMEM((1, LW), F32)],
        compiler_params=_cp("arbitrary"), name=name)(proj, proj, proj, hs, hs, doc, cw, wr, wi, vec)


def _branch_merge_fwd(oa, of, oc, wb, proj, name):
    T = proj.shape[0]
    tm, tn = _rt(T), 512

    def body(a0, a1, a2, w_ref, g0, g1, g2, r0, r1, r2, m_ref):
        acc = None
        for g, (a_ref, g_ref, r_ref) in enumerate(((a0, g0, r0), (a1, g1, r1), (a2, g2, r2))):
            b = _dot(a_ref[...], w_ref[g], 1, 0)
            r_ref[...] = b
            term = _sigmoid(g_ref[...]) * b
            acc = term if acc is None else acc + term
        m_ref[...] = acc.astype(BF16)

    act = pl.BlockSpec((tm, LW), lambda j, i: (i, 0))
    gate = lambda g: pl.BlockSpec((tm, tn), lambda j, i: (i, (GT + g * D) // tn + j))
    blk = pl.BlockSpec((tm, tn), lambda j, i: (i, j))
    return pl.pallas_call(
        body, grid=(D // tn, T // tm),
        in_specs=[act, act, act, pl.BlockSpec((3, LW, tn), lambda j, i: (0, 0, j)), gate(0), gate(1), gate(2)],
        out_specs=[blk] * 4,
        out_shape=[jax.ShapeDtypeStruct((T, D), F32)] * 3 + [jax.ShapeDtypeStruct((T, D), BF16)],
        compiler_params=_cp("parallel", "parallel"), name=name)(oa, of, oc, wb, proj, proj, proj)


def _out_dx_merge_bwd(dhb, w_out, proj, b0, b1, b2, name):
    T = proj.shape[0]
    tm, tn = _rt(T), 512

    def body(dh_ref, w_ref, g0, g1, g2, r0, r1, r2, d0, d1, d2, e0, e1, e2):
        dmv = _dot(dh_ref[...], w_ref[...], 1, 1)
        for g_ref, r_ref, d_ref, e_ref in ((g0, r0, d0, e0), (g1, r1, d1, e1), (g2, r2, d2, e2)):
            sg = _sigmoid(g_ref[...])
            d_ref[...] = (dmv * sg).astype(BF16)
            e_ref[...] = (dmv * r_ref[...] * sg * (1.0 - sg)).astype(BF16)

    gate = lambda g: pl.BlockSpec((tm, tn), lambda j, i: (i, (GT + g * D) // tn + j))
    blk = pl.BlockSpec((tm, tn), lambda j, i: (i, j))
    return pl.pallas_call(
        body, grid=(D // tn, T // tm),
        in_specs=[pl.BlockSpec((tm, D), lambda j, i: (i, 0)), pl.BlockSpec((tn, D), lambda j, i: (j, 0)),
                  gate(0), gate(1), gate(2), blk, blk, blk],
        out_specs=[blk] * 6, out_shape=[jax.ShapeDtypeStruct((T, D), BF16)] * 6,
        compiler_params=_cp("parallel", "parallel"), name=name)(dhb, w_out, proj, proj, proj, b0, b1, b2)


def _ffn_in_swiglu_fwd(u, w, name):
    T = u.shape[0]
    tm, tn = _rt(T), _pick(DFF, (1408, 256))
    nj = DFF // tn

    def body(u_ref, wg_ref, wu_ref, g_ref, up_ref, a_ref):
        ub = u_ref[...]
        g = _dot(ub, wg_ref[...], 1, 0)
        up = _dot(ub, wu_ref[...], 1, 0)
        g_ref[...] = g
        up_ref[...] = up
        a_ref[...] = (g * _sigmoid(g) * up).astype(BF16)

    blk = pl.BlockSpec((tm, tn), lambda j, i: (i, j))
    return pl.pallas_call(
        body, grid=(nj, T // tm),
        in_specs=[pl.BlockSpec((tm, D), lambda j, i: (i, 0)), pl.BlockSpec((D, tn), lambda j, i: (0, j)),
                  pl.BlockSpec((D, tn), lambda j, i: (0, j + nj))],
        out_specs=[blk] * 3,
        out_shape=[jax.ShapeDtypeStruct((T, DFF), F32)] * 2 + [jax.ShapeDtypeStruct((T, DFF), BF16)],
        compiler_params=_cp("parallel", "parallel"), name=name)(u, w, w)


def _ffn_out_dx_swiglu_bwd(dhb, w, gate, up, name):
    T = dhb.shape[0]
    tm, tn = _rt(T), _pick(DFF, (1408, 256))

    def body(dh_ref, w_ref, g_ref, up_ref, dg_ref, du_ref):
        d = _dot(dh_ref[...], w_ref[...], 1, 1)
        g = g_ref[...]
        sg = _sigmoid(g)
        dg_ref[...] = (d * up_ref[...] * (sg + g * sg * (1.0 - sg))).astype(BF16)
        du_ref[...] = (d * g * sg).astype(BF16)

    blk = pl.BlockSpec((tm, tn), lambda j, i: (i, j))
    return pl.pallas_call(
        body, grid=(DFF // tn, T // tm),
        in_specs=[pl.BlockSpec((tm, D), lambda j, i: (i, 0)), pl.BlockSpec((tn, D), lambda j, i: (j, 0)), blk, blk],
        out_specs=[blk] * 2, out_shape=[jax.ShapeDtypeStruct((T, DFF), BF16)] * 2,
        compiler_params=_cp("parallel", "parallel"), name=name)(dhb, w, gate, up)


def _adamw(w, g, m, v, name):
    R, C = w.shape
    tr = _pick(R, tuple(t for t in (512, 256, 128, 64, 32, 16, 8) if t * C * 4 <= (3 << 19)))
    c1 = 1.0 - ADAM_B1 ** ADAM_STEP
    c2 = 1.0 - ADAM_B2 ** ADAM_STEP

    def body(w_ref, g_ref, m_ref, v_ref, d_ref, mo_ref, vo_ref):
        gv = g_ref[...]
        mn = ADAM_B1 * m_ref[...] + (1.0 - ADAM_B1) * gv
        vn = ADAM_B2 * v_ref[...] + (1.0 - ADAM_B2) * (gv * gv)
        d_ref[...] = -ADAM_LR * ((mn / c1) / (jnp.sqrt(vn / c2) + ADAM_EPS) + ADAM_WD * w_ref[...])
        mo_ref[...] = mn
        vo_ref[...] = vn

    blk = pl.BlockSpec((tr, C), lambda i: (i, 0))
    return pl.pallas_call(
        body, grid=(R // tr,), in_specs=[blk] * 4, out_specs=[blk] * 3,
        out_shape=[jax.ShapeDtypeStruct((R, C), F32)] * 3, compiler_params=_cp("parallel"), name=name)(w, g, m, v)


def _sum_lead(x, name):
    n, R, C = x.shape
    tr = _pick(R, (512, 256, 128, 64, 32, 16, 8))

    def body(x_ref, o_ref):
        acc = x_ref[0]
        for d in range(1, n):
            acc = acc + x_ref[d]
        o_ref[...] = acc

    return pl.pallas_call(
        body, grid=(R // tr,), in_specs=[pl.BlockSpec((n, tr, C), lambda i: (0, i, 0))],
        out_specs=pl.BlockSpec((tr, C), lambda i: (i, 0)), out_shape=jax.ShapeDtypeStruct((R, C), F32),
        compiler_params=_cp("parallel"), name=name)(x)


def _here():
    return lax.axis_index("x"), lax.axis_index("y"), lax.axis_index("c")


def _rcopy(src, dst, send_sems, recv_sems, k, to):
    return pltpu.make_async_remote_copy(src_ref=src, dst_ref=dst, send_sem=send_sems.at[k], recv_sem=recv_sems.at[k],
                                        device_id=to, device_id_type=MESH)


def _window(ref, lead, axis, start, width):
    idx = [lead] + [slice(None)] * (len(ref.shape) - 1)
    if axis is not None:
        idx[axis] = pl.ds(start, width)
    return ref.at[tuple(idx)]


def _hbm_calls(body, args, out_shapes, n_sems, aliases, name):
    return pl.pallas_call(
        body, in_specs=[ANY] * len(args), out_specs=[ANY] * len(out_shapes), out_shape=out_shapes,
        input_output_aliases=aliases,
        scratch_shapes=[pltpu.SemaphoreType.DMA((n_sems,)), pltpu.SemaphoreType.DMA((n_sems,))],
        compiler_params=pltpu.CompilerParams(has_side_effects=True), name=name)(*args)


def _all_gather_weights(fulls, axes, name):
    nt = len(fulls)

    def body(*refs):
        outs, (send_sems, recv_sems) = refs[nt:2 * nt], refs[2 * nt:]
        x, y, c = _here()
        sib = (x, y, 1 - c)
        chips = [(1 - x, y), (x, 1 - y), (1 - x, 1 - y)]

        def win(t, chip, hc):
            w = outs[t].shape[axes[t]] // N_SHARD
            return _window(outs[t], pl.ds(2 * hc, 2), axes[t], pl.multiple_of((2 * chip[0] + chip[1]) * w, 8), w)

        def copy(t, k, chip, hc, to):
            return _rcopy(win(t, chip, hc), win(t, chip, hc), send_sems, recv_sems, 6 * t + k, to)

        sends = [copy(t, k, (x, y), c, (*chip, c)) for t in range(nt) for k, chip in enumerate(chips)]
        for cp in sends:
            cp.start()
        for t in range(nt):
            for k, chip in enumerate(chips):
                copy(t, k, chip, c, (*chip, c)).wait_recv()
                fwd = copy(t, 3 + k, chip, c, sib)
                fwd.start()
                sends.append(fwd)
        for t in range(nt):
            for k, chip in enumerate(chips):
                copy(t, 3 + k, chip, 1 - c, sib).wait_recv()
        for cp in sends:
            cp.wait_send()

    return _hbm_calls(body, fulls, [jax.ShapeDtypeStruct(f.shape, f.dtype) for f in fulls], 6 * nt,
                      {t: t for t in range(nt)}, name)


def _swap_halves(gs, name):
    nt = len(gs)

    def body(*refs):
        ins, outs, (send_sems, recv_sems) = refs[:nt], refs[nt:2 * nt], refs[2 * nt:]
        x, y, c = _here()
        cps = [_rcopy(g.at[pl.ds(2 * (1 - c), 2)], o, send_sems, recv_sems, t, (x, y, 1 - c))
               for t, (g, o) in enumerate(zip(ins, outs))]
        for cp in cps:
            cp.start()
        for cp in cps:
            cp.wait()

    return _hbm_calls(body, gs, [jax.ShapeDtypeStruct((2,) + g.shape[1:], g.dtype) for g in gs], nt, {}, name)


def _scatter_to_chips(ss, axes, name):
    nt = len(ss)

    def shard_shape(s, ax):
        shp = list(s.shape)
        shp[ax] //= N_SHARD
        return tuple(shp)

    def body(*refs):
        ins, outs, (send_sems, recv_sems) = refs[:nt], refs[nt:2 * nt], refs[2 * nt:]
        x, y, c = _here()
        chips = [(1 - x, y), (x, 1 - y), (1 - x, 1 - y)]
        cps = []
        for t, (s, o, ax) in enumerate(zip(ins, outs, axes)):
            w = s.shape[ax] // N_SHARD
            for k, chip in enumerate(chips):
                src = _window(s, slice(None), ax, pl.multiple_of((2 * chip[0] + chip[1]) * w, 8), w)
                cps.append(_rcopy(src, o.at[k], send_sems, recv_sems, 3 * t + k, (*chip, c)))
        for cp in cps:
            cp.start()
        for cp in cps:
            cp.wait()

    return _hbm_calls(body, ss, [jax.ShapeDtypeStruct((3,) + shard_shape(s, ax), s.dtype) for s, ax in zip(ss, axes)],
                      3 * nt, {}, name)


def _join_halves(fins, name):
    nt = len(fins)

    def body(*refs):
        outs, (send_sems, recv_sems) = refs[nt:2 * nt], refs[2 * nt:]
        x, y, c = _here()
        cps = [_rcopy(o.at[pl.ds(2 * c, 2)], o.at[pl.ds(2 * c, 2)], send_sems, recv_sems, t, (x, y, 1 - c))
               for t, o in enumerate(outs)]
        for cp in cps:
            cp.start()
        for t, o in enumerate(outs):
            _rcopy(o.at[pl.ds(2 * (1 - c), 2)], o.at[pl.ds(2 * (1 - c), 2)], send_sems, recv_sems, t, (x, y, 1 - c)).wait_recv()
        for cp in cps:
            cp.wait_send()

    return _hbm_calls(body, fins, [jax.ShapeDtypeStruct(f.shape, f.dtype) for f in fins], nt, {t: t for t in range(nt)}, name)


def _all_gather_small(buf, name):
    def body(_, out_ref, send_sems, recv_sems):
        x, y, c = _here()
        me = 4 * x + 2 * y + c
        cps = []
        for k in range(1, 8):
            to = (x ^ ((k >> 2) & 1), y ^ ((k >> 1) & 1), c ^ (k & 1))
            peer = 4 * to[0] + 2 * to[1] + to[2]
            cps.append((_rcopy(out_ref.at[me], out_ref.at[me], send_sems, recv_sems, k - 1, to),
                        _rcopy(out_ref.at[peer], out_ref.at[peer], send_sems, recv_sems, k - 1, to)))
        for snd, _ in cps:
            snd.start()
        for _, rcv in cps:
            rcv.wait_recv()
        for snd, _ in cps:
            snd.wait_send()

    return _hbm_calls(body, [buf], [jax.ShapeDtypeStruct(buf.shape, buf.dtype)], 7, {0: 0}, name)[0]


def _place(block, n, index):
    buf = jnp.zeros((n,) + block.shape[1:], block.dtype)
    return lax.dynamic_update_slice_in_dim(buf, block, index, axis=0)


def _add_half(g, other, cidx, name):
    _, R, C = g.shape
    tr = _pick(R, tuple(t for t in (512, 256, 128, 64, 32, 16, 8) if t * C * 4 <= (1 << 21)))

    def body(c_ref, g_ref, o_ref, s_ref, sb_ref):
        s = g_ref[...] + o_ref[...]
        s_ref[...] = s
        sb_ref[...] = s.astype(BF16)

    blk = pl.BlockSpec((None, tr, C), lambda l, i, c: (l, i, 0))
    return pl.pallas_call(
        body,
        grid_spec=pltpu.PrefetchScalarGridSpec(
            num_scalar_prefetch=1, grid=(2, R // tr),
            in_specs=[pl.BlockSpec((None, tr, C), lambda l, i, c: (2 * c[0] + l, i, 0)), blk], out_specs=[blk, blk]),
        out_shape=[jax.ShapeDtypeStruct((2, R, C), F32), jax.ShapeDtypeStruct((2, R, C), BF16)],
        compiler_params=_cp("parallel", "parallel"), name=name)(cidx, g, other)


def _add_chips(s, recv, axis, chip_idx, name):
    _, _, r, cw = recv.shape
    tr = _pick(r, tuple(t for t in (704, 512, 256, 128, 64, 32, 16, 8) if t * cw * 4 <= (1 << 21)))
    nr = r // tr

    def body(c_ref, s_ref, r_ref, out_ref):
        out_ref[...] = ((s_ref[...] + r_ref[0].astype(F32)) + r_ref[1].astype(F32)) + r_ref[2].astype(F32)

    if axis == 2:
        s_map = lambda l, i, c: (l, i, c[0])
    else:
        s_map = lambda l, i, c: (l, c[0] * nr + i, 0)
    return pl.pallas_call(
        body,
        grid_spec=pltpu.PrefetchScalarGridSpec(
            num_scalar_prefetch=1, grid=(2, nr),
            in_specs=[pl.BlockSpec((None, tr, cw), s_map), pl.BlockSpec((3, None, tr, cw), lambda l, i, c: (0, l, i, 0))],
            out_specs=pl.BlockSpec((None, tr, cw), lambda l, i, c: (l, i, 0))),
        out_shape=jax.ShapeDtypeStruct((2, r, cw), F32), compiler_params=_cp("parallel", "parallel"), name=name)(chip_idx, s, recv)


IN_SHARD = IN_COLS // N_SHARD
IN_SLOT = INP // N_SHARD
IN_PIECES = ((0, 512, QA), (512, 640, KA), (640, 768, VA), (768, 1280, QF), (1280, 1792, KF), (1792, 2304, VF),
             (2304, 2312, FL), (2312, 2824, XC), (2824, 3336, YC), (3336, 6408, GT))


def _gathered_to_kernel_cols(w):
    parts, pos = [], 0
    for a, b, k in sorted(IN_PIECES, key=lambda p: p[2]):
        assert k == pos
        while a < b:
            j = a // IN_SHARD
            e = min(b, (j + 1) * IN_SHARD)
            g = j * IN_SLOT + a - j * IN_SHARD
            parts.append(w[..., g:g + e - a])
            pos += e - a
            a = e
    parts.append(jnp.zeros(w.shape[:-1] + (INP - pos,), w.dtype))
    return jnp.concatenate(parts, axis=-1)


def _kernel_to_gathered_cols(w):
    parts = []
    for j in range(N_SHARD):
        lo, hi = j * IN_SHARD, (j + 1) * IN_SHARD
        for a, b, k in IN_PIECES:
            s, e = max(a, lo), min(b, hi)
            if s < e:
                parts.append(w[..., k + s - a:k + e - a])
        parts.append(jnp.zeros(w.shape[:-1] + (IN_SLOT - IN_SHARD,), w.dtype))
    return jnp.concatenate(parts, axis=-1)


def _pair_blocks(w):
    z = jnp.zeros((4, 64, 64), w.dtype)
    w = w.reshape(4, 2, 64, 64)
    top = jnp.concatenate([w[:, 0], z], axis=2)
    bot = jnp.concatenate([z, w[:, 1]], axis=2)
    return jnp.concatenate([top, bot], axis=1)


def _unpair_blocks(w):
    return jnp.stack([w[:, :64, :64], w[:, 64:, 64:]], axis=1).reshape(8, 64, 64)


BIG = ("w_in", "w_branch", "w_out", "w_ffn_in", "w_ffn_out")
TINY = ("conv_w", "meta_tokens")
SMALL = ("rel_bias_table", "norm_mix", "swa_sinks", "fox_forget_bias", "conv_b", "lru_w_r", "lru_b_r", "lru_w_i",
         "lru_b_i", "lru_lambda", "norm_ffn", "norm_final")
SHARD_AXIS = {"conv_w": 2, "meta_tokens": 1}
BIG_AXIS = {"w_in": 2, "w_branch": 2, "w_out": 1, "w_ffn_in": 2, "w_ffn_out": 1}


def _pack(d, names):
    flat = jnp.concatenate([d[n].reshape(-1) for n in names])
    pad = (-flat.shape[0]) % (256 * 128)
    return jnp.concatenate([flat, jnp.zeros((pad,), F32)]).reshape(-1, 128)


def _unpack(buf, names, shapes):
    flat, out, off = buf.reshape(-1), {}, 0
    for n in names:
        sz = int(np.prod(shapes[n]))
        out[n] = flat[off:off + sz].reshape(shapes[n])
        off += sz
    return out


def _local_step(x, tgt, W):
    S = x.shape[0]
    T = S + BLK
    tm = _pick(T, (1408, 384, 128))
    bucket = jnp.asarray(_bucket_table())
    bias = _bias_build(W["rel_bias_table"], bucket, "bias_build")
    h = jnp.concatenate([jnp.zeros((NPAD, D), F32), W["meta_tokens"], x], axis=0)

    saved = []
    for l in range(DEPTH):
        sv = {"h0": h}
        u = _rms_fwd(h, W["norm_mix"][l], f"rms_mix_fwd")
        proj = _mm(u, W["w_in"][l], tm=tm, tn=512, tk=D, name="mm_in_fwd")
        oa = _swa_fwd(proj, bias, W["swa_sinks"][l], "swa_fwd")
        z3 = _to_blocks(proj[:, FL:FL + NH].T)
        fb = W["fox_forget_bias"][l].reshape(NH, 1)
        crow = _from_blocks(_cum_fwd(z3, fb, "cum_fwd"))
        ccol = jnp.pad(crow.T, ((0, 0), (0, BLK - NH)))
        qaug, kaug, kaug_t, vm, vo = _fox_prep(proj, ccol, "fox_prep")
        of, *lse = _fox_fwd(qaug, kaug_t, vo, "fox_fwd")
        lru_vec = jnp.concatenate([W["lru_b_r"][l][None], W["lru_b_i"][l][None], W["lru_lambda"][l][None],
                                   W["conv_b"][l][None], jnp.zeros((4, LW), F32)], axis=0)
        oc, hs = _lru_fwd(proj, W["conv_w"][l], W["lru_w_r"][l], W["lru_w_i"][l], lru_vec, "lru_fwd")
        *bs, merged = _branch_merge_fwd(oa, of, oc, W["w_branch"][l], proj, "branch_merge_fwd")
        h2 = _mm(merged, W["w_out"][l], res=h, tm=tm, tn=512, tk=D, name="mm_out_fwd")
        u2 = _rms_fwd(h2, W["norm_ffn"][l], "rms_ffn_fwd")
        gate, up, act = _ffn_in_swiglu_fwd(u2, W["w_ffn_in"][l], "ffn_in_swiglu_fwd")
        h = _mm(act, W["w_ffn_out"][l], res=h2, tm=tm, tn=512, tk=_pick(DFF, (1408, 256)), name="mm_ffn_out_fwd")
        sv.update(u=u, proj=proj, oa=oa, of=of, oc=oc, lse=lse, hs=hs, z3=z3, fb=fb, qaug=qaug, kaug=kaug, kaug_t=kaug_t, vm=vm, lru_vec=lru_vec,
                  bs=bs, merged=merged, h2=h2, u2=u2, gate=gate, up=up, act=act)
        saved.append(sv)

    tgt_pad = tgt
    dh, dhb, dg_final, loss_vec = _loss_head(h, tgt_pad, W["norm_final"], "loss_head")
    loss = loss_vec[0, 0]

    small = ("norm_mix", "swa_sinks", "fox_forget_bias", "conv_w", "conv_b", "lru_w_r", "lru_b_r", "lru_w_i", "lru_b_i",
             "lru_lambda", "norm_ffn")
    G = {n: [None] * DEPTH for n in small}
    G["norm_final"] = dg_final.reshape(D)
    GW = {n: None for n in BIG}
    dbias = jnp.zeros((NH, BLK, 2 * BLK), F32)
    tkT = _rt(T)
    for l in reversed(range(DEPTH)):
        sv = saved[l]
        GW["w_ffn_out"] = _mm(_transpose(sv["act"], "tr_act"), dhb, tm=_pick(DFF, (1408, 256)), tn=D, tk=tkT,
                              slab=(GW["w_ffn_out"], l, DEPTH), name="mm_ffn_out_dw")
        dgate, dup = _ffn_out_dx_swiglu_bwd(dhb, W["w_ffn_out"][l], sv["gate"], sv["up"], "ffn_out_dx_swiglu_bwd")
        u2t = _transpose(sv["u2"], "tr_u2")
        du2 = None
        for half, dpart in enumerate((dgate, dup)):
            GW["w_ffn_in"] = _mm(u2t, dpart, tm=D, tn=_pick(DFF, (1408, 256)), tk=tkT, slab=(GW["w_ffn_in"], l, DEPTH),
                                 col0=half * DFF, cols=2 * DFF, name="mm_ffn_in_dw")
            du2 = _mm(dpart, W["w_ffn_in"][l], tb=True, res=du2, b_k0=half * DFF, tm=tm, tn=512,
                      tk=_pick(DFF, (1408, 256)), name="mm_ffn_in_dx")
        dh, dhb, dgn = _rms_bwd(du2, sv["h2"], W["norm_ffn"][l], dh, "rms_ffn_bwd")
        G["norm_ffn"][l] = dgn.reshape(D)
        GW["w_out"] = _mm(_transpose(sv["merged"], "tr_merged"), dhb, tm=D, tn=D, tk=tkT,
                          slab=(GW["w_out"], l, DEPTH), name="mm_out_dw")
        db0, db1, db2, dg0, dg1, dg2 = _out_dx_merge_bwd(dhb, W["w_out"][l], sv["proj"], *sv["bs"], "out_dx_merge_bwd")
        dos = []
        for g, (o, db) in enumerate(zip((sv["oa"], sv["of"], sv["oc"]), (db0, db1, db2))):
            GW["w_branch"] = _mm(_transpose(o, "tr_branch"), db, tm=LW, tn=D, tk=tkT,
                                 slab=(GW["w_branch"], 3 * l + g, 3 * DEPTH), name="mm_branch_dw")
            dos.append(_mm(db, W["w_branch"][l, g], tb=True, out_dtype=BF16, tm=tm, tn=LW, tk=D, name="mm_branch_dx"))
        dqa, dkb, dvb, dbias, dsk = _swa_bwd(sv["proj"], bias, W["swa_sinks"][l], dos[0], dbias, "swa_bwd")
        dka, dva = _band_fold(dkb, dvb, "swa_band_fold")
        G["swa_sinks"][l] = dsk[0, :NH]
        delta = _fox_delta(dos[1], sv["of"], "fox_delta")
        dqf, dqx, dkf, dvf, dkx = _fox_bwd(sv["qaug"], sv["kaug"], sv["kaug_t"], sv["vm"], dos[1], sv["lse"], delta, "fox_bwd")
        aug0 = [BLK * (h // 2) + (64 if h % 2 == 0 else 0) for h in range(NH)]
        cols = lambda a, off: jnp.concatenate([a[:, c + off:c + off + 1] for c in aug0], axis=1)
        dcq3 = _to_blocks(cols(dqx, 0).T)
        dck3 = _to_blocks(cols(dkx, 3).T)
        dz3, dfb = _cum_bwd(dck3, dcq3, sv["z3"], sv["fb"], "cum_bwd")
        G["fox_forget_bias"][l] = dfb[:, 0]
        dfl = jnp.pad(_from_blocks(dz3).T, ((0, 0), (0, INP - FL - NH))).astype(BF16)
        dxc, dyc, dwr, dwi, dvec = _lru_bwd(sv["proj"], sv["hs"], dos[2], W["conv_w"][l], W["lru_w_r"][l], W["lru_w_i"][l],
                                            sv["lru_vec"], "lru_bwd")
        G["lru_w_r"][l], G["lru_w_i"][l] = _unpair_blocks(dwr), _unpair_blocks(dwi)
        G["lru_b_r"][l], G["lru_b_i"][l], G["lru_lambda"][l], G["conv_b"][l] = dvec[0], dvec[1], dvec[2], dvec[3]
        G["conv_w"][l] = dvec[4:8]
        dproj = jnp.concatenate([dqa, dqf, dkf, dvf, dxc, dyc, dg0, dg1, dg2, dka, dva, dfl], axis=1)
        GW["w_in"] = _mm(_transpose(sv["u"], "tr_u"), dproj, tm=D, tn=IN_SLOT, tk=tkT,
                         slab=(GW["w_in"], l, DEPTH), name="mm_in_dw")
        du = _mm(dproj, W["w_in"][l], tb=True, tm=tm, tn=512, tk=_pick(INP, (1664, 512)), name="mm_in_dx")
        dh, dhb, dgn = _rms_bwd(du, sv["h0"], W["norm_mix"][l], dh, "rms_mix_bwd")
        G["norm_mix"][l] = dgn.reshape(D)

    grads = {n: (jnp.stack(v) if isinstance(v, list) else v) for n, v in G.items()}
    grads.update(GW)
    grads["w_branch"] = GW["w_branch"].reshape(DEPTH, 3 * LW, D)
    grads["rel_bias_table"] = _bias_bwd(dbias, bucket, "bias_bwd")
    grads["meta_tokens"] = dh[NPAD:BLK]
    return loss, dh[BLK:], grads


NAMES = ("meta_tokens", "rel_bias_table", "norm_mix", "w_in", "swa_sinks", "fox_forget_bias", "conv_w", "conv_b",
         "lru_w_r", "lru_b_r", "lru_w_i", "lru_b_i", "lru_lambda", "w_branch", "w_out", "norm_ffn", "w_ffn_in",
         "w_ffn_out", "norm_final")


def _three_d(n, a):
    return a.reshape(DEPTH, 3 * LW, -1) if n == "w_branch" else a


def _gather_weights(P):
    x, y, c = _here()
    mine, me = 2 * x + y, 4 * x + 2 * y + c
    fulls = []
    for n in BIG:
        shard = _three_d(n, P[n].astype(BF16))
        if n == "w_in":
            shard = jnp.pad(shard, ((0, 0), (0, 0), (0, IN_SLOT - IN_SHARD)))
        ax = BIG_AXIS[n]
        shape = list(shard.shape)
        shape[ax] *= N_SHARD
        fulls.append(lax.dynamic_update_slice_in_dim(jnp.zeros(shape, BF16), shard, mine * shard.shape[ax], axis=ax))
    full = dict(zip(BIG, _all_gather_weights(fulls, [BIG_AXIS[n] for n in BIG], "ag_weights")))
    full["w_in"] = _gathered_to_kernel_cols(full["w_in"])
    full["w_branch"] = full["w_branch"].reshape(DEPTH, 3, LW, D)
    tiny = _all_gather_small(_place(_pack(P, TINY)[None], 8, me), "ag_tiny_weights")
    parts = [_unpack(tiny[2 * j], TINY, {n: P[n].shape for n in TINY}) for j in range(N_SHARD)]
    for n in TINY:
        full[n] = jnp.concatenate([p[n] for p in parts], axis=SHARD_AXIS[n])
    for n in SMALL:
        full[n] = P[n]
    full["lru_w_r"] = jnp.stack([_pair_blocks(P["lru_w_r"][l]) for l in range(DEPTH)]).astype(BF16)
    full["lru_w_i"] = jnp.stack([_pair_blocks(P["lru_w_i"][l]) for l in range(DEPTH)]).astype(BF16)
    return full


def _reduce_grads(grads, P):
    x, y, c = _here()
    mine, me = 2 * x + y, 4 * x + 2 * y + c
    cidx = jnp.reshape(c, (1,)).astype(jnp.int32)
    chip = jnp.reshape(mine, (1,)).astype(jnp.int32)
    axes = [BIG_AXIS[n] for n in BIG]
    gs = [grads[n] for n in BIG]
    pairs = [_add_half(g, r, cidx, "rs_add_half_" + n) for n, g, r in zip(BIG, gs, _swap_halves(gs, "rs_swap_halves"))]
    ss, sbs = [list(t) for t in zip(*pairs)]
    ss[0], sbs[0] = _kernel_to_gathered_cols(ss[0]), _kernel_to_gathered_cols(sbs[0])
    recv = _scatter_to_chips(sbs, axes, "rs_scatter")
    tots = [_add_chips(s, r, ax, chip, "rs_add_chips_" + n) for n, s, r, ax in zip(BIG, ss, recv, axes)]
    fins = dict(zip(BIG, _join_halves([_place(t, DEPTH, 2 * c) for t in tots], "rs_join_halves")))
    out = {n: fins[n].reshape(P[n].shape) for n in BIG if n != "w_in"}
    out["w_in"] = fins["w_in"][:, :, :IN_SHARD]
    names = SMALL + TINY
    gathered = _all_gather_small(_place(_pack(grads, names)[None], 8, me), "ag_small_grads")
    small = _unpack(_sum_lead(gathered, "sum_small_grads"), names, {n: grads[n].shape for n in names})
    for n in SMALL:
        out[n] = small[n]
    for n in TINY:
        w = P[n].shape[SHARD_AXIS[n]]
        out[n] = lax.dynamic_slice_in_dim(small[n], mine * w, w, axis=SHARD_AXIS[n])
    return out


def _update(P, Gd, M, V):
    delta, new_m, new_v = {}, {}, {}
    for n in BIG + TINY:
        shp = P[n].shape
        two = (int(np.prod(shp[:-1])), shp[-1])
        d, m, v = _adamw(P[n].reshape(two), Gd[n].reshape(two), M[n].reshape(two), V[n].reshape(two), "adamw_" + n)
        delta[n], new_m[n], new_v[n] = d.reshape(shp), m.reshape(shp), v.reshape(shp)
    shapes = {n: P[n].shape for n in SMALL}
    d, m, v = _adamw(_pack(P, SMALL), _pack(Gd, SMALL), _pack(M, SMALL), _pack(V, SMALL), "adamw_small")
    for dst, buf in ((delta, d), (new_m, m), (new_v, v)):
        dst.update(_unpack(buf, SMALL, shapes))
    return delta, new_m, new_v


def kernel(x, meta_tokens, rel_bias_table, norm_mix, w_in, swa_sinks, fox_forget_bias, conv_w, conv_b, lru_w_r, lru_b_r, lru_w_i, lru_b_i, lru_lambda, w_branch, w_out, norm_ffn, w_ffn_in, w_ffn_out, norm_final, loss_target, m_meta_tokens, m_rel_bias_table, m_norm_mix, m_w_in, m_swa_sinks, m_fox_forget_bias, m_conv_w, m_conv_b, m_lru_w_r, m_lru_b_r, m_lru_w_i, m_lru_b_i, m_lru_lambda, m_w_branch, m_w_out, m_norm_ffn, m_w_ffn_in, m_w_ffn_out, m_norm_final, v_meta_tokens, v_rel_bias_table, v_norm_mix, v_w_in, v_swa_sinks, v_fox_forget_bias, v_conv_w, v_conv_b, v_lru_w_r, v_lru_b_r, v_lru_w_i, v_lru_b_i, v_lru_lambda, v_w_branch, v_w_out, v_norm_ffn, v_w_ffn_in, v_w_ffn_out, v_norm_final):
    P = dict(zip(NAMES, (meta_tokens, rel_bias_table, norm_mix, w_in, swa_sinks, fox_forget_bias, conv_w, conv_b, lru_w_r,
                         lru_b_r, lru_w_i, lru_b_i, lru_lambda, w_branch, w_out, norm_ffn, w_ffn_in, w_ffn_out, norm_final)))
    M = dict(zip(NAMES, (m_meta_tokens, m_rel_bias_table, m_norm_mix, m_w_in, m_swa_sinks, m_fox_forget_bias, m_conv_w,
                         m_conv_b, m_lru_w_r, m_lru_b_r, m_lru_w_i, m_lru_b_i, m_lru_lambda, m_w_branch, m_w_out, m_norm_ffn,
                         m_w_ffn_in, m_w_ffn_out, m_norm_final)))
    V = dict(zip(NAMES, (v_meta_tokens, v_rel_bias_table, v_norm_mix, v_w_in, v_swa_sinks, v_fox_forget_bias, v_conv_w,
                         v_conv_b, v_lru_w_r, v_lru_b_r, v_lru_w_i, v_lru_b_i, v_lru_lambda, v_w_branch, v_w_out, v_norm_ffn,
                         v_w_ffn_in, v_w_ffn_out, v_norm_final)))
    W = _gather_weights(P)
    loss_local, grad_x, grads = _local_step(x[0], loss_target[0], W)
    loss = lax.psum(loss_local, ("x", "y", "c"))
    Gd = _reduce_grads(grads, P)
    delta, new_m, new_v = _update(P, Gd, M, V)
    return (loss, grad_x[None], *[Gd[n] for n in NAMES], *[delta[n] for n in NAMES],
            *[new_m[n] for n in NAMES], *[new_v[n] for n in NAMES])
```

```python
import functools
import math

import numpy as np
import jax
import jax.numpy as jnp
from jax import lax
from jax.experimental import pallas as pl
from jax.experimental.pallas import tpu as pltpu

F32, BF16 = jnp.float32, jnp.bfloat16
MESH = pl.DeviceIdType.MESH
ANY = pl.BlockSpec(memory_space=pl.ANY)
SMEM = pl.BlockSpec(memory_space=pltpu.SMEM)

D = 1024
DEPTH = 4
BLK = 128
N_META = 16
NPAD = 112
NH = 8
LW = 512
DFF = 2816
EPS = 1e-6
NEG = -1e30
SCALE = 0.125
LRU_C = 8.0
REL_BUCKETS = 32
N_SHARD = 4
QA, QF, KF, VF, XC, YC, GT, KA, VA, FL, INP = 0, 512, 1024, 1536, 2048, 2560, 3072, 6144, 6272, 6400, 6656
IN_COLS = 6408
VMEM_LIMIT = 48 * 1024 * 1024

ADAM_LR, ADAM_B1, ADAM_B2, ADAM_EPS, ADAM_WD, ADAM_STEP = 0.001, 0.9, 0.999, 1e-08, 0.01, 10


def _cp(*sem):
    return pltpu.CompilerParams(dimension_semantics=sem or None, vmem_limit_bytes=VMEM_LIMIT)


def _pick(n, prefs):
    for p in prefs:
        if n % p == 0:
            return p
    return n


def _rt(T):
    return _pick(T, (384, 128))


def _sigmoid(z):
    return 1.0 / (1.0 + jnp.exp(-z))


def _log_sigmoid(z):
    return jnp.minimum(z, 0.0) - jnp.log(1.0 + jnp.exp(-jnp.abs(z)))


def _gelu(y):
    c = math.sqrt(2.0 / math.pi)
    return 0.5 * y * (1.0 + jnp.tanh(c * (y + 0.044715 * y * y * y)))


def _gelu_grad(y):
    c = math.sqrt(2.0 / math.pi)
    t = jnp.tanh(c * (y + 0.044715 * y * y * y))
    return 0.5 * (1.0 + t) + 0.5 * y * (1.0 - t * t) * c * (1.0 + 3.0 * 0.044715 * y * y)


def _neg_expm1(z):
    series = -z * (1.0 + z * (0.5 + z * (1.0 / 6.0 + z * (1.0 / 24.0 + z * (1.0 / 120.0)))))
    return jnp.where(z > -0.1, series, 1.0 - jnp.exp(z))


def _dot(a, b, ca, cb):
    return lax.dot_general(a, b, (((ca,), (cb,)), ((), ())), preferred_element_type=F32)


def _mm(a, b, *, ta=False, tb=False, res=None, out_dtype=F32, tm, tn, tk, name, slab=None, b_k0=0, col0=0, cols=None):
    M, K = (a.shape[1], a.shape[0]) if ta else a.shape
    N = b.shape[0] if tb else b.shape[1]
    assert (b.shape[1] if tb else b.shape[0]) >= K + b_k0 and M % tm == 0 and N % tn == 0 and K % tk == 0, (name, a.shape, b.shape)
    assert b_k0 % tk == 0 and col0 % tn == 0
    nk, kb, jb = K // tk, b_k0 // tk, col0 // tn
    ca, cb = (0 if ta else 1), (1 if tb else 0)
    n_in = 2 + (res is not None) + (slab is not None and slab[0] is not None)

    def body(*refs):
        a_ref, b_ref = refs[:2]
        r_ref = refs[2] if res is not None else None
        o_ref = refs[n_in]
        part = _dot(a_ref[...].astype(BF16), b_ref[...].astype(BF16), ca, cb)

        def fin(acc):
            if res is not None:
                acc = acc + r_ref[...]
            o_ref[...] = acc.astype(out_dtype)

        if nk == 1:
            fin(part)
        else:
            acc_ref = refs[-1]
            k = pl.program_id(2)

            @pl.when(k == 0)
            def _():
                acc_ref[...] = part

            @pl.when(k > 0)
            def _():
                acc_ref[...] += part

            @pl.when(k == nk - 1)
            def _():
                fin(acc_ref[...])

    a_spec = pl.BlockSpec((tk, tm), lambda i, j, k: (k, i)) if ta else pl.BlockSpec((tm, tk), lambda i, j, k: (i, k))
    b_spec = (pl.BlockSpec((tn, tk), lambda i, j, k: (j, k + kb)) if tb
              else pl.BlockSpec((tk, tn), lambda i, j, k: (k + kb, j)))
    o_spec = pl.BlockSpec((tm, tn), lambda i, j, k: (i, j))
    in_specs, ops = [a_spec, b_spec], [a, b]
    if res is not None:
        in_specs.append(o_spec)
        ops.append(res)
    out_shape, aliases = jax.ShapeDtypeStruct((M, N), out_dtype), {}
    if slab is not None:
        buf, idx, n = slab
        o_spec = pl.BlockSpec((None, tm, tn), lambda i, j, k: (idx, i, j + jb))
        out_shape = jax.ShapeDtypeStruct((n, M, cols or N), out_dtype)
        if buf is not None:
            aliases = {len(ops): 0}
            in_specs.append(ANY)
            ops.append(buf)
    return pl.pallas_call(
        body, grid=(M // tm, N // tn, nk), in_specs=in_specs, out_specs=o_spec, out_shape=out_shape,
        input_output_aliases=aliases, scratch_shapes=[pltpu.VMEM((tm, tn), F32)] if nk > 1 else [],
        compiler_params=_cp("parallel", "parallel", "arbitrary"), name=name)(*ops)


def _transpose(x, name):
    T, C = x.shape
    tr, tc = _rt(T), _pick(C, (1408, 1024, 512, 256, 128))

    def body(x_ref, o_ref):
        o_ref[...] = x_ref[...].T

    return pl.pallas_call(
        body, grid=(T // tr, C // tc), in_specs=[pl.BlockSpec((tr, tc), lambda i, j: (i, j))],
        out_specs=pl.BlockSpec((tc, tr), lambda i, j: (j, i)), out_shape=jax.ShapeDtypeStruct((C, T), x.dtype),
        compiler_params=_cp("parallel", "parallel"), name=name)(x)


def _rms_fwd(h, g, name):
    T = h.shape[0]
    tr = _rt(T)

    def body(h_ref, g_ref, u_ref):
        x = h_ref[...]
        r = lax.rsqrt(jnp.mean(x * x, axis=-1, keepdims=True) + EPS)
        u_ref[...] = (x * r * g_ref[...]).astype(BF16)

    return pl.pallas_call(
        body, grid=(T // tr,),
        in_specs=[pl.BlockSpec((tr, D), lambda i: (i, 0)), pl.BlockSpec((1, D), lambda i: (0, 0))],
        out_specs=pl.BlockSpec((tr, D), lambda i: (i, 0)), out_shape=jax.ShapeDtypeStruct((T, D), BF16),
        compiler_params=_cp("parallel"), name=name)(h, g.reshape(1, D))


def _rms_bwd(du, h, g, dres, name):
    T = h.shape[0]
    tr = _rt(T)

    def body(du_ref, h_ref, g_ref, dres_ref, dh_ref, dhb_ref, dg_ref):
        x = h_ref[...]
        r = lax.rsqrt(jnp.mean(x * x, axis=-1, keepdims=True) + EPS)
        xh = x * r
        dy = du_ref[...]
        dxh = dy * g_ref[...]
        dx = r * (dxh - xh * jnp.mean(dxh * xh, axis=-1, keepdims=True))
        dh = dres_ref[...] + dx
        dh_ref[...] = dh
        dhb_ref[...] = dh.astype(BF16)
        part = jnp.sum(dy * xh, axis=0, keepdims=True)

        @pl.when(pl.program_id(0) == 0)
        def _():
            dg_ref[...] = part

        @pl.when(pl.program_id(0) > 0)
        def _():
            dg_ref[...] += part

    row = pl.BlockSpec((tr, D), lambda i: (i, 0))
    vec = pl.BlockSpec((1, D), lambda i: (0, 0))
    return pl.pallas_call(
        body, grid=(T // tr,), in_specs=[row, row, vec, row], out_specs=[row, row, vec],
        out_shape=[jax.ShapeDtypeStruct((T, D), F32), jax.ShapeDtypeStruct((T, D), BF16), jax.ShapeDtypeStruct((1, D), F32)],
        compiler_params=_cp("arbitrary"), name=name)(du, h, g.reshape(1, D), dres)


def _loss_head(h, tgt, g, name):
    T = h.shape[0]
    nb = T // BLK

    def body(h_ref, t_ref, g_ref, dh_ref, dhb_ref, dg_ref, loss_ref):
        i = pl.program_id(0)
        x = h_ref[...]
        r = lax.rsqrt(jnp.mean(x * x, axis=-1, keepdims=True) + EPS)
        xh = x * r
        gv = g_ref[...]
        tok = i >= 1
        err = jnp.where(tok, xh * gv - t_ref[...], 0.0)
        dy = err * (1.0 / D)
        dxh = dy * gv
        dx = r * (dxh - xh * jnp.mean(dxh * xh, axis=-1, keepdims=True))
        dh_ref[...] = dx
        dhb_ref[...] = dx.astype(BF16)
        dg = jnp.sum(dy * xh, axis=0, keepdims=True)
        ls = jnp.zeros((1, BLK), F32) + jnp.sum(err * err) * (0.5 / D)

        @pl.when(i == 0)
        def _():
            dg_ref[...] = dg
            loss_ref[...] = ls

        @pl.when(i > 0)
        def _():
            dg_ref[...] += dg
            loss_ref[...] += ls

    row = pl.BlockSpec((BLK, D), lambda i: (i, 0))
    vec = pl.BlockSpec((1, D), lambda i: (0, 0))
    return pl.pallas_call(
        body, grid=(nb,),
        in_specs=[row, pl.BlockSpec((BLK, D), lambda i: (jnp.maximum(i - 1, 0), 0)), vec],
        out_specs=[row, row, vec, pl.BlockSpec((1, BLK), lambda i: (0, 0))],
        out_shape=[jax.ShapeDtypeStruct((T, D), F32), jax.ShapeDtypeStruct((T, D), BF16),
                   jax.ShapeDtypeStruct((1, D), F32), jax.ShapeDtypeStruct((1, BLK), F32)],
        compiler_params=_cp("arbitrary"), name=name)(h, tgt, g.reshape(1, D))


def _bucket_table():
    q = np.arange(BLK)[:, None]
    k = np.arange(2 * BLK)[None, :]
    d = np.maximum(q + BLK - k, 0)
    max_exact = REL_BUCKETS // 2
    scaled = np.log(np.maximum(d, 1).astype(np.float32) / np.float32(max_exact)) / np.float32(math.log(128 / max_exact))
    large = np.minimum(max_exact + (scaled.astype(np.float32) * (REL_BUCKETS - max_exact)).astype(np.int32), REL_BUCKETS - 1)
    return np.where(d < max_exact, d, large).astype(np.int32)


def _bias_build(table, bucket, name):
    def body(t_ref, bk_ref, o_ref):
        bk = bk_ref[...]
        for h in range(NH):
            acc = jnp.zeros((BLK, 2 * BLK), F32)
            for b in range(REL_BUCKETS):
                acc = jnp.where(bk == b, t_ref[b, h], acc)
            o_ref[h] = acc

    return pl.pallas_call(
        body, in_specs=[SMEM, pl.BlockSpec(memory_space=pltpu.VMEM)], out_specs=pl.BlockSpec(memory_space=pltpu.VMEM),
        out_shape=jax.ShapeDtypeStruct((NH, BLK, 2 * BLK), F32), compiler_params=_cp(), name=name)(table, bucket)


def _bias_bwd(dbias, bucket, name):
    def body(d_ref, bk_ref, o_ref):
        bk = bk_ref[...]
        for h in range(NH):
            dh = d_ref[h]
            for b in range(REL_BUCKETS):
                o_ref[b, h] = jnp.sum(jnp.where(bk == b, dh, 0.0))

    return pl.pallas_call(
        body, in_specs=[pl.BlockSpec(memory_space=pltpu.VMEM)] * 2, out_specs=SMEM,
        out_shape=jax.ShapeDtypeStruct((REL_BUCKETS, NH), F32), compiler_params=_cp(), name=name)(dbias, bucket)


def _swa_specs(nq_cols):
    prev = lambda n: jnp.maximum(n - 1, 0)
    return [
        pl.BlockSpec((BLK, nq_cols), lambda n: (n, QA // nq_cols)),
        pl.BlockSpec((BLK, BLK), lambda n: (prev(n), KA // BLK)), pl.BlockSpec((BLK, BLK), lambda n: (n, KA // BLK)),
        pl.BlockSpec((BLK, BLK), lambda n: (prev(n), VA // BLK)), pl.BlockSpec((BLK, BLK), lambda n: (n, VA // BLK)),
    ]


def _swa_mask(n):
    row = lax.broadcasted_iota(jnp.int32, (BLK, 2 * BLK), 0)
    col = lax.broadcasted_iota(jnp.int32, (BLK, 2 * BLK), 1)
    dist = row + BLK - col
    return (dist >= 0) & (dist < BLK) & ((n - 1) * BLK + col >= NPAD)


def _swa_probs(qm, ksel, mask, bias_h, sink):
    s = _dot(qm, ksel, 1, 1) * SCALE
    s = jnp.where(mask, s + bias_h, NEG)
    m = jnp.maximum(jnp.max(s, axis=-1, keepdims=True), sink)
    p = jnp.exp(s - m)
    psink = jnp.exp(sink - m)
    inv = 1.0 / (jnp.sum(p, axis=-1, keepdims=True) + psink)
    return p * inv, psink * inv


def _swa_fwd(proj, bias, sinks, name):
    T = proj.shape[0]
    nb = T // BLK

    def body(sk_ref, q_ref, kp_ref, kc_ref, vp_ref, vc_ref, b_ref, o_ref):
        n = pl.program_id(0)
        lo = lax.broadcasted_iota(jnp.int32, (1, BLK), 1) < 64
        kb = jnp.concatenate([kp_ref[...], kc_ref[...]], axis=0)
        vb = jnp.concatenate([vp_ref[...], vc_ref[...]], axis=0)
        kbs = (kb.astype(BF16), pltpu.roll(kb, 64, 1).astype(BF16))
        vbs = (vb, pltpu.roll(vb, 64, 1))
        mask = _swa_mask(n)
        outs = []
        for pr in range(NH // 2):
            qp = q_ref[:, pr * BLK:(pr + 1) * BLK]
            kv = pr // 2
            acc = jnp.zeros((BLK, BLK), F32)
            for e in range(2):
                lm = lo if e == 0 else jnp.logical_not(lo)
                sw = 0 if kv == e else 1
                qm = jnp.where(lm, qp, 0.0).astype(BF16)
                pn, _ = _swa_probs(qm, kbs[sw], mask, b_ref[2 * pr + e], sk_ref[2 * pr + e])
                acc = acc + _dot(pn.astype(BF16), jnp.where(lm, vbs[sw], 0.0).astype(BF16), 1, 0)
            outs.append(acc)
        o_ref[...] = jnp.concatenate(outs, axis=1).astype(BF16)

    return pl.pallas_call(
        body, grid=(nb,),
        in_specs=[SMEM] + _swa_specs(512) + [pl.BlockSpec((NH, BLK, 2 * BLK), lambda n: (0, 0, 0))],
        out_specs=pl.BlockSpec((BLK, 512), lambda n: (n, 0)), out_shape=jax.ShapeDtypeStruct((T, 512), BF16),
        compiler_params=_cp("parallel"), name=name)(sinks, proj, proj, proj, proj, proj, bias)


def _swa_bwd(proj, bias, sinks, do, dbias_in, name):
    T = proj.shape[0]
    nb = T // BLK

    def body(sk_ref, q_ref, kp_ref, kc_ref, vp_ref, vc_ref, b_ref, do_ref, dbi_ref,
             dq_ref, dk_ref, dv_ref, db_ref, dsk_ref, sk_acc):
        n = pl.program_id(0)
        lane = lax.broadcasted_iota(jnp.int32, (1, BLK), 1)
        lo = lane < 64
        kb = jnp.concatenate([kp_ref[...], kc_ref[...]], axis=0)
        vb = jnp.concatenate([vp_ref[...], vc_ref[...]], axis=0)
        kbs = (kb, pltpu.roll(kb, 64, 1))
        vbs = (vb, pltpu.roll(vb, 64, 1))
        mask = _swa_mask(n)

        @pl.when(n == 0)
        def _():
            db_ref[...] = dbi_ref[...]
            sk_acc[...] = jnp.zeros_like(sk_acc)

        dqs = []
        dk = jnp.zeros((2 * BLK, BLK), F32)
        dv = jnp.zeros((2 * BLK, BLK), F32)
        for pr in range(NH // 2):
            qp = q_ref[:, pr * BLK:(pr + 1) * BLK]
            dop = do_ref[:, pr * BLK:(pr + 1) * BLK].astype(F32)
            kv = pr // 2
            dq = jnp.zeros((BLK, BLK), F32)
            for e in range(2):
                h = 2 * pr + e
                lm = lo if e == 0 else jnp.logical_not(lo)
                sw = 0 if kv == e else 1
                qm = jnp.where(lm, qp, 0.0)
                dom = jnp.where(lm, dop, 0.0)
                pn, ps = _swa_probs(qm.astype(BF16), kbs[sw].astype(BF16), mask, b_ref[h], sk_ref[h])
                dp = _dot(dom.astype(BF16), vbs[sw].astype(BF16), 1, 1)
                delta = jnp.sum(pn * dp, axis=-1, keepdims=True)
                ds = pn * (dp - delta)
                db_ref[h] += ds
                sk_acc[...] += jnp.where(lane == h, -(ps * delta), 0.0)
                dsb = (ds * SCALE).astype(BF16)
                dq = dq + _dot(dsb, jnp.where(lm, kbs[sw], 0.0).astype(BF16), 1, 0)
                qk = qm if sw == 0 else pltpu.roll(qm, 64, 1)
                dok = dom if sw == 0 else pltpu.roll(dom, 64, 1)
                dk = dk + _dot(dsb, qk.astype(BF16), 0, 0)
                dv = dv + _dot(pn.astype(BF16), dok.astype(BF16), 0, 0)
            dqs.append(dq)
        dq_ref[...] = jnp.concatenate(dqs, axis=1).astype(BF16)
        dk_ref[0] = dk
        dv_ref[0] = dv

        @pl.when(n == nb - 1)
        def _():
            dsk_ref[...] = jnp.sum(sk_acc[...], axis=0, keepdims=True)

    full_b = pl.BlockSpec((NH, BLK, 2 * BLK), lambda n: (0, 0, 0))
    band = pl.BlockSpec((1, 2 * BLK, BLK), lambda n: (n, 0, 0))
    return pl.pallas_call(
        body, grid=(nb,),
        in_specs=[SMEM] + _swa_specs(512) + [full_b, pl.BlockSpec((BLK, 512), lambda n: (n, 0)), full_b],
        out_specs=[pl.BlockSpec((BLK, 512), lambda n: (n, 0)), band, band, full_b, pl.BlockSpec((1, BLK), lambda n: (0, 0))],
        out_shape=[jax.ShapeDtypeStruct((T, 512), BF16), jax.ShapeDtypeStruct((nb, 2 * BLK, BLK), F32),
                   jax.ShapeDtypeStruct((nb, 2 * BLK, BLK), F32), jax.ShapeDtypeStruct((NH, BLK, 2 * BLK), F32),
                   jax.ShapeDtypeStruct((1, BLK), F32)],
        scratch_shapes=[pltpu.VMEM((BLK, BLK), F32)],
        compiler_params=_cp("arbitrary"), name=name)(sinks, proj, proj, proj, proj, proj, bias, do, dbias_in)


def _band_fold(dkb, dvb, name):
    nb = dkb.shape[0]

    def body(ko_ref, kn_ref, vo_ref, vn_ref, dk_ref, dv_ref):
        last = pl.program_id(0) == nb - 1
        dk_ref[...] = (ko_ref[0] + jnp.where(last, 0.0, kn_ref[0])).astype(BF16)
        dv_ref[...] = (vo_ref[0] + jnp.where(last, 0.0, vn_ref[0])).astype(BF16)

    own = pl.BlockSpec((1, BLK, BLK), lambda j: (j, 1, 0))
    nxt = pl.BlockSpec((1, BLK, BLK), lambda j: (jnp.minimum(j + 1, nb - 1), 0, 0))
    out = pl.BlockSpec((BLK, BLK), lambda j: (j, 0))
    return pl.pallas_call(
        body, grid=(nb,), in_specs=[own, nxt, own, nxt], out_specs=[out, out],
        out_shape=[jax.ShapeDtypeStruct((nb * BLK, BLK), BF16)] * 2,
        compiler_params=_cp("parallel"), name=name)(dkb, dkb, dvb, dvb)


def _token_major(x, width):
    full = jnp.concatenate([x, jnp.zeros((BLK - NH, BLK), F32)], axis=0).T
    return full if width == BLK else jnp.concatenate([full, jnp.zeros((BLK, width - BLK), F32)], axis=1)


def _cum_fwd(proj, fb, name):
    T = proj.shape[0]

    def body(z_ref, fb_ref, c_ref, carry):
        b = pl.program_id(0)
        lane = lax.broadcasted_iota(jnp.int32, (NH, BLK), 1)

        @pl.when(b == 0)
        def _():
            carry[...] = jnp.zeros_like(carry)

        z = z_ref[...].T[0:NH, :] + fb_ref[...]
        x = jnp.where(b * BLK + lane >= NPAD, _log_sigmoid(z), 0.0)
        s = 1
        while s < BLK:
            x = x + jnp.where(lane >= s, pltpu.roll(x, s, 1), 0.0)
            s *= 2
        x = x + carry[...]
        carry[...] = jnp.zeros((NH, BLK), F32) + jnp.sum(jnp.where(lane == BLK - 1, x, 0.0), axis=-1, keepdims=True)
        c_ref[...] = _token_major(x, BLK)

    return pl.pallas_call(
        body, grid=(T // BLK,),
        in_specs=[pl.BlockSpec((BLK, BLK), lambda b: (b, FL // BLK)), pl.BlockSpec((NH, 1), lambda b: (0, 0))],
        out_specs=pl.BlockSpec((BLK, BLK), lambda b: (b, 0)), out_shape=jax.ShapeDtypeStruct((T, BLK), F32),
        scratch_shapes=[pltpu.VMEM((NH, BLK), F32)], compiler_params=_cp("arbitrary"), name=name)(proj, fb)


def _cum_bwd(dqx, dkx, proj, fb, name):
    T = proj.shape[0]
    nb = T // BLK

    def body(dq_ref, dk_ref, z_ref, fb_ref, dz_ref, db_ref, carry):
        k = pl.program_id(0)
        b = nb - 1 - k
        lane = lax.broadcasted_iota(jnp.int32, (NH, BLK), 1)

        @pl.when(k == 0)
        def _():
            carry[...] = jnp.zeros_like(carry)
            db_ref[...] = jnp.zeros_like(db_ref)

        def picked(ref, r_first, r_second):
            rows = []
            for p in range(NH // 2):
                t_ = ref[:, p * BLK:(p + 1) * BLK].T
                rows += [t_[r_first:r_first + 1, :], t_[r_second:r_second + 1, :]]
            return jnp.concatenate(rows, axis=0)

        x = picked(dq_ref, 64, 0) - picked(dk_ref, 67, 3)
        s = 1
        while s < BLK:
            x = x + jnp.where(lane < BLK - s, pltpu.roll(x, BLK - s, 1), 0.0)
            s *= 2
        x = x + carry[...]
        carry[...] = jnp.zeros((NH, BLK), F32) + jnp.sum(jnp.where(lane == 0, x, 0.0), axis=-1, keepdims=True)
        z = z_ref[...].T[0:NH, :] + fb_ref[...]
        dz = jnp.where(b * BLK + lane >= NPAD, x * _sigmoid(-z), 0.0)
        db_ref[...] += jnp.sum(dz, axis=-1, keepdims=True)
        dz_ref[...] = _token_major(dz, 2 * BLK).astype(BF16)

    rev = lambda k: nb - 1 - k
    wide = pl.BlockSpec((BLK, 512), lambda k: (rev(k), 0))
    return pl.pallas_call(
        body, grid=(nb,),
        in_specs=[wide, wide, pl.BlockSpec((BLK, BLK), lambda k: (rev(k), FL // BLK)), pl.BlockSpec((NH, 1), lambda k: (0, 0))],
        out_specs=[pl.BlockSpec((BLK, 2 * BLK), lambda k: (rev(k), 0)), pl.BlockSpec((NH, BLK), lambda k: (0, 0))],
        out_shape=[jax.ShapeDtypeStruct((T, 2 * BLK), BF16), jax.ShapeDtypeStruct((NH, BLK), F32)],
        scratch_shapes=[pltpu.VMEM((NH, BLK), F32)], compiler_params=_cp("arbitrary"), name=name)(dqx, dkx, proj, fb)


def _fox_prep(proj, ccol, name):
    T = proj.shape[0]
    tr = _pick(T, (1408, 384, 128))

    def body(q_ref, k_ref, v_ref, cc_ref, qa_ref, ka_ref, kt_ref, vm_ref, vo_ref):
        h = pl.program_id(1)
        lane = lax.broadcasted_iota(jnp.int32, (1, BLK), 1)
        own = (lane >> 6) == (h & 1)
        a0 = 64 * (1 - (h & 1))
        c = _lane_pick(cc_ref[...], lane, h)
        hi = c.astype(BF16).astype(F32)
        mid = (c - hi).astype(BF16).astype(F32)
        lo = (c - hi - mid).astype(BF16).astype(F32)
        ones = (lane >= a0 + 3) & (lane < a0 + 6)
        qa = jnp.where(own, q_ref[...] * SCALE, jnp.where(ones, 1.0, 0.0))
        qa = jnp.where(lane == a0, hi, jnp.where(lane == a0 + 1, mid, jnp.where(lane == a0 + 2, lo, qa)))
        ones = (lane >= a0) & (lane < a0 + 3)
        ka = jnp.where(own, k_ref[...], jnp.where(ones, 1.0, 0.0))
        ka = jnp.where(lane == a0 + 3, -hi, jnp.where(lane == a0 + 4, -mid, jnp.where(lane == a0 + 5, -lo, ka)))
        qa_ref[...] = qa.astype(BF16)
        kab = ka.astype(BF16)
        ka_ref[...] = kab
        kt_ref[...] = kab.T
        vm = jnp.where(own, v_ref[...], 0.0)
        vm_ref[...] = vm.astype(BF16)
        vo_ref[...] = jnp.where(lane == a0, 1.0, vm).astype(BF16)

    pair = lambda col0: pl.BlockSpec((tr, BLK), lambda i, h: (i, col0 // BLK + (h >> 1)))
    out = pl.BlockSpec((None, tr, BLK), lambda i, h: (h, i, 0))
    out_t = pl.BlockSpec((None, BLK, tr), lambda i, h: (h, 0, i))
    tok = jax.ShapeDtypeStruct((NH, T, BLK), BF16)
    return pl.pallas_call(
        body, grid=(T // tr, NH), in_specs=[pair(QF), pair(KF), pair(VF), pl.BlockSpec((tr, BLK), lambda i, h: (i, 0))],
        out_specs=[out, out, out_t, out, out], out_shape=[tok, tok, jax.ShapeDtypeStruct((NH, BLK, T), BF16), tok, tok],
        compiler_params=_cp("parallel", "arbitrary"), name=name)(proj, proj, proj, ccol)


def _fox_fwd(qaug, kaug_t, vo, name):
    T = qaug.shape[1]
    t = _rt(T)
    nt = T // t

    def body(q0, q1, k0, k1, v0, v1, o_ref, lse0_ref, lse1_ref, m_ref, acc_ref):
        i, j = pl.program_id(1), pl.program_id(2)
        lane = lax.broadcasted_iota(jnp.int32, (1, BLK), 1)
        lo = lane < 64

        @pl.when(j == 0)
        def _():
            m_ref[...] = jnp.full_like(m_ref, NEG)
            acc_ref[...] = jnp.zeros_like(acc_ref)

        def step(masked):
            for e, (q_ref, k_ref, v_ref) in enumerate(((q0, k0, v0), (q1, k1, v1))):
                s = _dot(q_ref[...], k_ref[...], 1, 0)
                if masked:
                    s = jnp.where(_fox_mask(i, j, t), s, NEG)
                m_old = m_ref[e]
                m_new = jnp.maximum(m_old, jnp.max(s, axis=-1, keepdims=True))
                m_ref[e] = m_new
                pe = jnp.exp(s - jnp.concatenate([m_new] * (t // BLK), axis=1))
                acc_ref[e] = jnp.exp(m_old - m_new) * acc_ref[e] + _dot(pe.astype(BF16), v_ref[...], 1, 0)

        pl.when((j < i) & (j > 0))(lambda: step(False))
        pl.when((j == i) | ((j == 0) & (i > 0)))(lambda: step(True))

        @pl.when(j == i)
        def _():
            rows = i * t + lax.broadcasted_iota(jnp.int32, (t, 1), 0)
            l0, l1 = _lane_pick(acc_ref[0], lane, 64), _lane_pick(acc_ref[1], lane, 0)
            o = jnp.where(lo, acc_ref[0] / l0, acc_ref[1] / l1)
            o_ref[...] = jnp.where(rows >= NPAD, o, 0.0).astype(BF16)
            lse0_ref[...] = m_ref[0] + jnp.log(l0)
            lse1_ref[...] = m_ref[1] + jnp.log(l1)

    kj = lambda i, j: jnp.minimum(j, i)
    qs = lambda e: pl.BlockSpec((None, t, BLK), lambda p, i, j: (2 * p + e, i, 0))
    ks = lambda e: pl.BlockSpec((None, t, BLK), lambda p, i, j: (2 * p + e, kj(i, j), 0))
    kts = lambda e: pl.BlockSpec((None, BLK, t), lambda p, i, j: (2 * p + e, 0, kj(i, j)))
    rep = pl.BlockSpec((None, t, BLK), lambda p, i, j: (p, i, 0))
    return pl.pallas_call(
        body, grid=(NH // 2, nt, nt), in_specs=[qs(0), qs(1), kts(0), kts(1), ks(0), ks(1)],
        out_specs=[pl.BlockSpec((t, BLK), lambda p, i, j: (i, p)), rep, rep],
        out_shape=[jax.ShapeDtypeStruct((T, 512), BF16)] + [jax.ShapeDtypeStruct((NH // 2, T, BLK), F32)] * 2,
        scratch_shapes=[pltpu.VMEM((2, t, BLK), F32), pltpu.VMEM((2, t, BLK), F32)],
        compiler_params=_cp("parallel", "parallel", "arbitrary"), name=name)(qaug, qaug, kaug_t, kaug_t, vo, vo)


def _fox_delta(do, o, name):
    T = do.shape[0]
    tr = _rt(T)

    def body(do_ref, o_ref, d0_ref, d1_ref):
        lo = lax.broadcasted_iota(jnp.int32, (1, BLK), 1) < 64
        prod = do_ref[...].astype(F32) * o_ref[...].astype(F32)
        d0_ref[...] = jnp.zeros((tr, BLK), F32) + jnp.sum(jnp.where(lo, prod, 0.0), axis=-1, keepdims=True)
        d1_ref[...] = jnp.zeros((tr, BLK), F32) + jnp.sum(jnp.where(lo, 0.0, prod), axis=-1, keepdims=True)

    blk = pl.BlockSpec((tr, BLK), lambda i, p: (i, p))
    rep = pl.BlockSpec((None, tr, BLK), lambda i, p: (p, i, 0))
    return pl.pallas_call(
        body, grid=(T // tr, NH // 2), in_specs=[blk, blk], out_specs=[rep, rep],
        out_shape=[jax.ShapeDtypeStruct((NH // 2, T, BLK), F32)] * 2,
        compiler_params=_cp("parallel", "parallel"), name=name)(do, o)


def _fox_bwd(qaug, kaug, kaug_t, vm, do, lses, deltas, name):
    T = qaug.shape[1]
    t = _rt(T)
    nt = T // t

    def body(q0, q1, k0, k1, kt0, kt1, v0, v1, do_ref, lse0, lse1, dl0, dl1,
             dq_ref, dqx_ref, dk_ref, dv_ref, dkx_ref, dq_acc, dk_acc, dv_acc):
        j, i = pl.program_id(1), pl.program_id(2)
        lane = lax.broadcasted_iota(jnp.int32, (1, BLK), 1)
        lo = lane < 64

        @pl.when((j == 0) & (i == 0))
        def _():
            dq_acc[...] = jnp.zeros_like(dq_acc)

        @pl.when(i == 0)
        def _():
            dk_acc[...] = jnp.zeros_like(dk_acc)
            dv_acc[...] = jnp.zeros_like(dv_acc)

        def step(masked):
            dob = do_ref[...]
            rows = pl.ds(pl.multiple_of(i * t, t), t)
            wide = lambda ref: jnp.concatenate([ref[...]] * (t // BLK), axis=1)
            for e, (q_ref, k_ref, kt_ref, v_ref, lse_ref, dl_ref) in enumerate(
                    ((q0, k0, kt0, v0, lse0, dl0), (q1, k1, kt1, v1, lse1, dl1))):
                s = _dot(q_ref[...], kt_ref[...], 1, 0)
                if masked:
                    s = jnp.where(_fox_mask(i, j, t), s, NEG)
                pe = jnp.exp(s - wide(lse_ref))
                dp = _dot(dob, v_ref[...], 1, 1)
                ds = (pe * (dp - wide(dl_ref))).astype(BF16)
                dq_acc[e, rows, :] += _dot(ds, k_ref[...], 1, 0)
                dk_acc[e] += _dot(ds, q_ref[...], 0, 0)
                dv_acc[e] += _dot(pe.astype(BF16), dob, 0, 0)

        pl.when((i > j) & (j > 0))(lambda: step(False))
        pl.when((i == j) | ((j == 0) & (i > 0)))(lambda: step(True))

        @pl.when(i == nt - 1)
        def _():
            dk_ref[...] = jnp.where(lo, dk_acc[0], dk_acc[1]).astype(BF16)
            dv_ref[...] = jnp.where(lo, dv_acc[0], dv_acc[1]).astype(BF16)
            dkx_ref[...] = jnp.where(lo, dk_acc[1], dk_acc[0])

        @pl.when((i == nt - 1) & (j == nt - 1))
        def _():
            dq_ref[...] = (jnp.where(lo, dq_acc[0], dq_acc[1]) * SCALE).astype(BF16)
            dqx_ref[...] = jnp.where(lo, dq_acc[1], dq_acc[0])

    qi = lambda j, i: jnp.maximum(i, j)
    qs = lambda e: pl.BlockSpec((None, t, BLK), lambda p, j, i: (2 * p + e, qi(j, i), 0))
    ks = lambda e: pl.BlockSpec((None, t, BLK), lambda p, j, i: (2 * p + e, j, 0))
    kts = lambda e: pl.BlockSpec((None, BLK, t), lambda p, j, i: (2 * p + e, 0, j))
    qside = pl.BlockSpec((t, BLK), lambda p, j, i: (qi(j, i), p))
    kside = pl.BlockSpec((t, BLK), lambda p, j, i: (j, p))
    rep = pl.BlockSpec((None, t, BLK), lambda p, j, i: (p, qi(j, i), 0))
    whole = pl.BlockSpec((T, BLK), lambda p, j, i: (0, p))
    return pl.pallas_call(
        body, grid=(NH // 2, nt, nt),
        in_specs=[qs(0), qs(1), ks(0), ks(1), kts(0), kts(1), ks(0), ks(1), qside, rep, rep, rep, rep],
        out_specs=[whole, whole, kside, kside, kside],
        out_shape=[jax.ShapeDtypeStruct((T, 512), BF16), jax.ShapeDtypeStruct((T, 512), F32),
                   jax.ShapeDtypeStruct((T, 512), BF16), jax.ShapeDtypeStruct((T, 512), BF16),
                   jax.ShapeDtypeStruct((T, 512), F32)],
        scratch_shapes=[pltpu.VMEM((2, T, BLK), F32), pltpu.VMEM((2, t, BLK), F32), pltpu.VMEM((2, t, BLK), F32)],
        compiler_params=_cp("parallel", "arbitrary", "arbitrary"), name=name)(
            qaug, qaug, kaug, kaug, kaug_t, kaug_t, vm, vm, do, *lses, *deltas)


def _fox_mask(i, j, t):
    row = i * t + lax.broadcasted_iota(jnp.int32, (t, t), 0)
    col = j * t + lax.broadcasted_iota(jnp.int32, (t, t), 1)
    return (col <= row) & (col >= NPAD)


def _lane_pick(x, lane, idx):
    return jnp.sum(jnp.where(lane == idx, x, 0.0), axis=-1, keepdims=True)


def _lru_gates(xc, wr_ref, wi_ref, vec_ref):
    xb = xc.astype(BF16)
    pre_r = jnp.concatenate([_dot(xb[:, p * BLK:(p + 1) * BLK], wr_ref[p], 1, 0) for p in range(LW // BLK)], axis=1)
    pre_i = jnp.concatenate([_dot(xb[:, p * BLK:(p + 1) * BLK], wi_ref[p], 1, 0) for p in range(LW // BLK)], axis=1)
    r = _sigmoid(pre_r + vec_ref[0:1, :])
    gi = _sigmoid(pre_i + vec_ref[1:2, :])
    log_a = LRU_C * r * _log_sigmoid(vec_ref[2:3, :])
    a = jnp.exp(log_a)
    mult = jnp.sqrt(_neg_expm1(2.0 * log_a))
    return r, gi, a, mult


def _conv(xbuf_ref, x, cw_ref, vec_ref, tr):
    return (cw_ref[3:4, :] * x + cw_ref[2:3, :] * xbuf_ref[7:7 + tr, :] + cw_ref[1:2, :] * xbuf_ref[6:6 + tr, :]
            + cw_ref[0:1, :] * xbuf_ref[5:5 + tr, :] + vec_ref[3:4, :])


def _lru_fwd(proj, cw, wr, wi, vec, name):
    T = proj.shape[0]
    tr = _rt(T)

    def body(x_ref, y_ref, cw_ref, wr_ref, wi_ref, vec_ref, oc_ref, hs_ref, xbuf, abuf, bbuf, hcar):
        i = pl.program_id(0)

        @pl.when(i == 0)
        def _():
            xbuf[0:8, :] = jnp.zeros((8, LW), F32)
            hcar[...] = jnp.zeros_like(hcar)

        x = x_ref[...]
        xbuf[8:8 + tr, :] = x
        xc = _conv(xbuf, x, cw_ref, vec_ref, tr)
        xbuf[0:8, :] = x[tr - 8:tr, :]
        _, gi, a, mult = _lru_gates(xc, wr_ref, wi_ref, vec_ref)
        rows = i * tr + lax.broadcasted_iota(jnp.int32, (tr, 1), 0)
        abuf[...] = a
        bbuf[...] = jnp.where(rows >= NPAD, mult * (gi * xc), 0.0)
        sub = lax.broadcasted_iota(jnp.int32, (8, 1), 0)

        def step(k, h):
            sl = pl.ds(pl.multiple_of(k * 8, 8), 8)
            a8, b8 = abuf[sl, :], bbuf[sl, :]
            for s in (1, 2, 4):
                ok = sub >= s
                b8 = jnp.where(ok, a8 * pltpu.roll(b8, s, 0) + b8, b8)
                a8 = jnp.where(ok, a8 * pltpu.roll(a8, s, 0), a8)
            h8 = a8 * h + b8
            bbuf[sl, :] = h8
            return h8[7:8, :]

        hcar[...] = lax.fori_loop(0, tr // 8, step, hcar[...])
        hs = bbuf[...]
        hs_ref[...] = hs
        oc_ref[...] = (hs * _gelu(y_ref[...])).astype(BF16)

    row = pl.BlockSpec((tr, LW), lambda i: (i, 0))
    full = lambda shape: pl.BlockSpec(shape, lambda i: (0,) * len(shape))
    return pl.pallas_call(
        body, grid=(T // tr,),
        in_specs=[pl.BlockSpec((tr, LW), lambda i: (i, XC // LW)), pl.BlockSpec((tr, LW), lambda i: (i, YC // LW)),
                  full((4, LW)), full((4, BLK, BLK)), full((4, BLK, BLK)), full((8, LW))],
        out_specs=[row, row], out_shape=[jax.ShapeDtypeStruct((T, LW), BF16), jax.ShapeDtypeStruct((T, LW), F32)],
        scratch_shapes=[pltpu.VMEM((tr + 8, LW), F32), pltpu.VMEM((tr, LW), F32), pltpu.VMEM((tr, LW), F32),
                        pltpu.VMEM((1, LW), F32)],
        compiler_params=_cp("arbitrary"), name=name)(proj, proj, cw, wr, wi, vec)


def _lru_bwd(proj, hs, doc, cw, wr, wi, vec, name):
    T = proj.shape[0]
    tr = _rt(T)
    nt = T // tr
    r8 = tr // 8

    def body(x_ref, xp_ref, y_ref, hs_ref, hp_ref, do_ref, cw_ref, wr_ref, wi_ref, vec_ref,
             dx_ref, dy_ref, dwr_ref, dwi_ref, dvec_ref, xbuf, abuf, gbuf, hbuf, dbuf, gcar, acar):
        k = pl.program_id(0)
        i = nt - 1 - k

        @pl.when(k == 0)
        def _():
            dwr_ref[...] = jnp.zeros_like(dwr_ref)
            dwi_ref[...] = jnp.zeros_like(dwi_ref)
            dvec_ref[...] = jnp.zeros_like(dvec_ref)
            gcar[...] = jnp.zeros_like(gcar)
            acar[...] = jnp.zeros_like(acar)
            dbuf[tr:tr + 8, :] = jnp.zeros((8, LW), F32)

        first = i == 0
        x = x_ref[...]
        xbuf[0:8, :] = jnp.where(first, 0.0, xp_ref[...])
        xbuf[8:8 + tr, :] = x
        xc = _conv(xbuf, x, cw_ref, vec_ref, tr)
        r, gi, a, mult = _lru_gates(xc, wr_ref, wi_ref, vec_ref)
        y = y_ref[...]
        hs = hs_ref[...]
        do_ = do_ref[...].astype(F32)
        rows = i * tr + lax.broadcasted_iota(jnp.int32, (tr, 1), 0)
        abuf[0:tr, :] = a
        abuf[tr:tr + 8, :] = jnp.zeros((8, LW), F32) + acar[...]
        an = abuf[1:1 + tr, :]
        acar[...] = a[0:1, :]
        abuf[0:tr, :] = an
        gbuf[...] = do_ * _gelu(y)
        sub = lax.broadcasted_iota(jnp.int32, (8, 1), 0)

        def step(kk, g):
            sl = pl.ds(pl.multiple_of((r8 - 1 - kk) * 8, 8), 8)
            a8, b8 = abuf[sl, :], gbuf[sl, :]
            for s in (1, 2, 4):
                ok = sub < 8 - s
                b8 = jnp.where(ok, a8 * pltpu.roll(b8, 8 - s, 0) + b8, b8)
                a8 = jnp.where(ok, a8 * pltpu.roll(a8, 8 - s, 0), a8)
            g8 = a8 * g + b8
            gbuf[sl, :] = g8
            return g8[0:1, :]

        gcar[...] = lax.fori_loop(0, r8, step, gcar[...])
        g = gbuf[...]
        hbuf[0:8, :] = jnp.where(first, 0.0, hp_ref[...])
        hbuf[8:8 + tr, :] = hs
        hprev = hbuf[7:7 + tr, :]
        dinp = jnp.where(rows >= NPAD, g, 0.0)
        da = g * hprev
        dmult = dinp * gi * xc
        dgi = dinp * mult * xc
        dxc = dinp * mult * gi
        dlog_a = da * a - dmult * a * a / mult
        ls = _log_sigmoid(vec_ref[2:3, :])
        dpre_r = dlog_a * (LRU_C * ls) * r * (1.0 - r)
        dpre_i = dgi * gi * (1.0 - gi)
        xb = xc.astype(BF16)
        rb, ib = dpre_r.astype(BF16), dpre_i.astype(BF16)
        back = []
        for p in range(LW // BLK):
            c = slice(p * BLK, (p + 1) * BLK)
            back.append(_dot(rb[:, c], wr_ref[p], 1, 1) + _dot(ib[:, c], wi_ref[p], 1, 1))
            dwr_ref[p] += _dot(xb[:, c], rb[:, c], 0, 0)
            dwi_ref[p] += _dot(xb[:, c], ib[:, c], 0, 0)
        dxc = dxc + jnp.concatenate(back, axis=1)
        col = lambda v: jnp.sum(v, axis=0, keepdims=True)
        dvec_ref[0:1, :] += col(dpre_r)
        dvec_ref[1:2, :] += col(dpre_i)
        dvec_ref[2:3, :] += col(dlog_a * (LRU_C * r)) * _sigmoid(-vec_ref[2:3, :])
        dvec_ref[3:4, :] += col(dxc)
        dvec_ref[4:5, :] += col(dxc * xbuf[5:5 + tr, :])
        dvec_ref[5:6, :] += col(dxc * xbuf[6:6 + tr, :])
        dvec_ref[6:7, :] += col(dxc * xbuf[7:7 + tr, :])
        dvec_ref[7:8, :] += col(dxc * x)
        dbuf[0:tr, :] = dxc
        dxr = (cw_ref[3:4, :] * dxc + cw_ref[2:3, :] * dbuf[1:1 + tr, :] + cw_ref[1:2, :] * dbuf[2:2 + tr, :]
               + cw_ref[0:1, :] * dbuf[3:3 + tr, :])
        dbuf[tr:tr + 8, :] = dxc[0:8, :]
        dx_ref[...] = jnp.where(rows >= NPAD, dxr, 0.0).astype(BF16)
        dy_ref[...] = (do_ * hs * _gelu_grad(y)).astype(BF16)

    rev = lambda k: nt - 1 - k
    row = lambda col0: pl.BlockSpec((tr, LW), lambda k: (rev(k), col0))
    prev8 = lambda col0: pl.BlockSpec((8, LW), lambda k: (jnp.maximum(rev(k) * r8 - 1, 0), col0))
    full = lambda shape: pl.BlockSpec(shape, lambda k: (0,) * len(shape))
    return pl.pallas_call(
        body, grid=(nt,),
        in_specs=[row(XC // LW), prev8(XC // LW), row(YC // LW), row(0), prev8(0), row(0),
                  full((4, LW)), full((4, BLK, BLK)), full((4, BLK, BLK)), full((8, LW))],
        out_specs=[row(0), row(0), full((4, BLK, BLK)), full((4, BLK, BLK)), full((8, LW))],
        out_shape=[jax.ShapeDtypeStruct((T, LW), BF16), jax.ShapeDtypeStruct((T, LW), BF16),
                   jax.ShapeDtypeStruct((4, BLK, BLK), F32), jax.ShapeDtypeStruct((4, BLK, BLK), F32),
                   jax.ShapeDtypeStruct((8, LW), F32)],
        scratch_shapes=[pltpu.VMEM((tr + 8, LW), F32), pltpu.VMEM((tr + 8, LW), F32), pltpu.VMEM((tr, LW), F32),
                        pltpu.VMEM((tr + 8, LW), F32), pltpu.VMEM((tr + 8, LW), F32),
                        pltpu.VMEM((1, LW), F32), pltpu.VMEM((1, LW), F32)],
        compiler_params=_cp("arbitrary"), name=name)(proj, proj, proj, hs, hs, doc, cw, wr, wi, vec)


def _branch_merge_fwd(oa, of, oc, wb, proj, name):
    T = proj.shape[0]
    tm, tn = _rt(T), 512

    def body(a0, a1, a2, w_ref, g0, g1, g2, r0, r1, r2, m_ref):
        acc = None
        for g, (a_ref, g_ref, r_ref) in enumerate(((a0, g0, r0), (a1, g1, r1), (a2, g2, r2))):
            b = _dot(a_ref[...], w_ref[g], 1, 0)
            r_ref[...] = b
            term = _sigmoid(g_ref[...]) * b
            acc = term if acc is None else acc + term
        m_ref[...] = acc.astype(BF16)

    act = pl.BlockSpec((tm, LW), lambda j, i: (i, 0))
    gate = lambda g: pl.BlockSpec((tm, tn), lambda j, i: (i, (GT + g * D) // tn + j))
    blk = pl.BlockSpec((tm, tn), lambda j, i: (i, j))
    return pl.pallas_call(
        body, grid=(D // tn, T // tm),
        in_specs=[act, act, act, pl.BlockSpec((3, LW, tn), lambda j, i: (0, 0, j)), gate(0), gate(1), gate(2)],
        out_specs=[blk] * 4,
        out_shape=[jax.ShapeDtypeStruct((T, D), F32)] * 3 + [jax.ShapeDtypeStruct((T, D), BF16)],
        compiler_params=_cp("parallel", "parallel"), name=name)(oa, of, oc, wb, proj, proj, proj)


def _out_dx_merge_bwd(dhb, w_out, proj, b0, b1, b2, name):
    T = proj.shape[0]
    tm, tn = _rt(T), 512

    def body(dh_ref, w_ref, g0, g1, g2, r0, r1, r2, d0, d1, d2, e0, e1, e2):
        dmv = _dot(dh_ref[...], w_ref[...], 1, 1)
        for g_ref, r_ref, d_ref, e_ref in ((g0, r0, d0, e0), (g1, r1, d1, e1), (g2, r2, d2, e2)):
            sg = _sigmoid(g_ref[...])
            d_ref[...] = (dmv * sg).astype(BF16)
            e_ref[...] = (dmv * r_ref[...] * sg * (1.0 - sg)).astype(BF16)

    gate = lambda g: pl.BlockSpec((tm, tn), lambda j, i: (i, (GT + g * D) // tn + j))
    blk = pl.BlockSpec((tm, tn), lambda j, i: (i, j))
    return pl.pallas_call(
        body, grid=(D // tn, T // tm),
        in_specs=[pl.BlockSpec((tm, D), lambda j, i: (i, 0)), pl.BlockSpec((tn, D), lambda j, i: (j, 0)),
                  gate(0), gate(1), gate(2), blk, blk, blk],
        out_specs=[blk] * 6, out_shape=[jax.ShapeDtypeStruct((T, D), BF16)] * 6,
        compiler_params=_cp("parallel", "parallel"), name=name)(dhb, w_out, proj, proj, proj, b0, b1, b2)


def _ffn_in_swiglu_fwd(u, w, name):
    T = u.shape[0]
    tm, tn = _rt(T), _pick(DFF, (1408, 256))
    nj = DFF // tn

    def body(u_ref, wg_ref, wu_ref, g_ref, up_ref, a_ref):
        ub = u_ref[...]
        g = _dot(ub, wg_ref[...], 1, 0)
        up = _dot(ub, wu_ref[...], 1, 0)
        g_ref[...] = g
        up_ref[...] = up
        a_ref[...] = (g * _sigmoid(g) * up).astype(BF16)

    blk = pl.BlockSpec((tm, tn), lambda j, i: (i, j))
    return pl.pallas_call(
        body, grid=(nj, T // tm),
        in_specs=[pl.BlockSpec((tm, D), lambda j, i: (i, 0)), pl.BlockSpec((D, tn), lambda j, i: (0, j)),
                  pl.BlockSpec((D, tn), lambda j, i: (0, j + nj))],
        out_specs=[blk] * 3,
        out_shape=[jax.ShapeDtypeStruct((T, DFF), F32)] * 2 + [jax.ShapeDtypeStruct((T, DFF), BF16)],
        compiler_params=_cp("parallel", "parallel"), name=name)(u, w, w)


def _ffn_out_dx_swiglu_bwd(dhb, w, gate, up, name):
    T = dhb.shape[0]
    tm, tn = _rt(T), _pick(DFF, (1408, 256))

    def body(dh_ref, w_ref, g_ref, up_ref, dg_ref, du_ref):
        d = _dot(dh_ref[...], w_ref[...], 1, 1)
        g = g_ref[...]
        sg = _sigmoid(g)
        dg_ref[...] = (d * up_ref[...] * (sg + g * sg * (1.0 - sg))).astype(BF16)
        du_ref[...] = (d * g * sg).astype(BF16)

    blk = pl.BlockSpec((tm, tn), lambda j, i: (i, j))
    return pl.pallas_call(
        body, grid=(DFF // tn, T // tm),
        in_specs=[pl.BlockSpec((tm, D), lambda j, i: (i, 0)), pl.BlockSpec((tn, D), lambda j, i: (j, 0)), blk, blk],
        out_specs=[blk] * 2, out_shape=[jax.ShapeDtypeStruct((T, DFF), BF16)] * 2,
        compiler_params=_cp("parallel", "parallel"), name=name)(dhb, w, gate, up)


def _adamw(w, g, m, v, name):
    R, C = w.shape
    tr = _pick(R, tuple(t for t in (512, 256, 128, 64, 32, 16, 8) if t * C * 4 <= (3 << 19)))
    c1 = 1.0 - ADAM_B1 ** ADAM_STEP
    c2 = 1.0 - ADAM_B2 ** ADAM_STEP

    def body(w_ref, g_ref, m_ref, v_ref, d_ref, mo_ref, vo_ref):
        gv = g_ref[...]
        mn = ADAM_B1 * m_ref[...] + (1.0 - ADAM_B1) * gv
        vn = ADAM_B2 * v_ref[...] + (1.0 - ADAM_B2) * (gv * gv)
        d_ref[...] = -ADAM_LR * ((mn / c1) / (jnp.sqrt(vn / c2) + ADAM_EPS) + ADAM_WD * w_ref[...])
        mo_ref[...] = mn
        vo_ref[...] = vn

    blk = pl.BlockSpec((tr, C), lambda i: (i, 0))
    return pl.pallas_call(
        body, grid=(R // tr,), in_specs=[blk] * 4, out_specs=[blk] * 3,
        out_shape=[jax.ShapeDtypeStruct((R, C), F32)] * 3, compiler_params=_cp("parallel"), name=name)(w, g, m, v)


def _sum_lead(x, name):
    n, R, C = x.shape
    tr = _pick(R, (512, 256, 128, 64, 32, 16, 8))

    def body(x_ref, o_ref):
        acc = x_ref[0]
        for d in range(1, n):
            acc = acc + x_ref[d]
        o_ref[...] = acc

    return pl.pallas_call(
        body, grid=(R // tr,), in_specs=[pl.BlockSpec((n, tr, C), lambda i: (0, i, 0))],
        out_specs=pl.BlockSpec((tr, C), lambda i: (i, 0)), out_shape=jax.ShapeDtypeStruct((R, C), F32),
        compiler_params=_cp("parallel"), name=name)(x)


def _here():
    return lax.axis_index("x"), lax.axis_index("y"), lax.axis_index("c")


def _rcopy(src, dst, send_sems, recv_sems, k, to):
    return pltpu.make_async_remote_copy(src_ref=src, dst_ref=dst, send_sem=send_sems.at[k], recv_sem=recv_sems.at[k],
                                        device_id=to, device_id_type=MESH)


def _window(ref, lead, axis, start, width):
    idx = [lead] + [slice(None)] * (len(ref.shape) - 1)
    if axis is not None:
        idx[axis] = pl.ds(start, width)
    return ref.at[tuple(idx)]


def _hbm_calls(body, args, out_shapes, n_sems, aliases, name):
    return pl.pallas_call(
        body, in_specs=[ANY] * len(args), out_specs=[ANY] * len(out_shapes), out_shape=out_shapes,
        input_output_aliases=aliases,
        scratch_shapes=[pltpu.SemaphoreType.DMA((n_sems,)), pltpu.SemaphoreType.DMA((n_sems,))],
        compiler_params=pltpu.CompilerParams(has_side_effects=True), name=name)(*args)


def _all_gather_weights(fulls, axes, name):
    nt = len(fulls)

    def body(*refs):
        outs, (send_sems, recv_sems) = refs[nt:2 * nt], refs[2 * nt:]
        x, y, c = _here()
        sib = (x, y, 1 - c)
        chips = [(1 - x, y), (x, 1 - y), (1 - x, 1 - y)]

        def win(t, chip, hc):
            w = outs[t].shape[axes[t]] // N_SHARD
            return _window(outs[t], pl.ds(2 * hc, 2), axes[t], pl.multiple_of((2 * chip[0] + chip[1]) * w, 8), w)

        def copy(t, k, chip, hc, to):
            return _rcopy(win(t, chip, hc), win(t, chip, hc), send_sems, recv_sems, 6 * t + k, to)

        sends = [copy(t, k, (x, y), c, (*chip, c)) for t in range(nt) for k, chip in enumerate(chips)]
        for cp in sends:
            cp.start()
        for t in range(nt):
            for k, chip in enumerate(chips):
                copy(t, k, chip, c, (*chip, c)).wait_recv()
                fwd = copy(t, 3 + k, chip, c, sib)
                fwd.start()
                sends.append(fwd)
        for t in range(nt):
            for k, chip in enumerate(chips):
                copy(t, 3 + k, chip, 1 - c, sib).wait_recv()
        for cp in sends:
            cp.wait_send()

    return _hbm_calls(body, fulls, [jax.ShapeDtypeStruct(f.shape, f.dtype) for f in fulls], 6 * nt,
                      {t: t for t in range(nt)}, name)


def _swap_halves(gs, name):
    nt = len(gs)

    def body(*refs):
        ins, outs, (send_sems, recv_sems) = refs[:nt], refs[nt:2 * nt], refs[2 * nt:]
        x, y, c = _here()
        cps = [_rcopy(g.at[pl.ds(2 * (1 - c), 2)], o, send_sems, recv_sems, t, (x, y, 1 - c))
               for t, (g, o) in enumerate(zip(ins, outs))]
        for cp in cps:
            cp.start()
        for cp in cps:
            cp.wait()

    return _hbm_calls(body, gs, [jax.ShapeDtypeStruct((2,) + g.shape[1:], g.dtype) for g in gs], nt, {}, name)


def _scatter_to_chips(ss, axes, name):
    nt = len(ss)

    def shard_shape(s, ax):
        shp = list(s.shape)
        shp[ax] //= N_SHARD
        return tuple(shp)

    def body(*refs):
        ins, outs, (send_sems, recv_sems) = refs[:nt], refs[nt:2 * nt], refs[2 * nt:]
        x, y, c = _here()
        chips = [(1 - x, y), (x, 1 - y), (1 - x, 1 - y)]
        cps = []
        for t, (s, o, ax) in enumerate(zip(ins, outs, axes)):
            w = s.shape[ax] // N_SHARD
            for k, chip in enumerate(chips):
                src = _window(s, slice(None), ax, pl.multiple_of((2 * chip[0] + chip[1]) * w, 8), w)
                cps.append(_rcopy(src, o.at[k], send_sems, recv_sems, 3 * t + k, (*chip, c)))
        for cp in cps:
            cp.start()
        for cp in cps:
            cp.wait()

    return _hbm_calls(body, ss, [jax.ShapeDtypeStruct((3,) + shard_shape(s, ax), s.dtype) for s, ax in zip(ss, axes)],
                      3 * nt, {}, name)


def _join_halves(fins, name):
    nt = len(fins)

    def body(*refs):
        outs, (send_sems, recv_sems) = refs[nt:2 * nt], refs[2 * nt:]
        x, y, c = _here()
        cps = [_rcopy(o.at[pl.ds(2 * c, 2)], o.at[pl.ds(2 * c, 2)], send_sems, recv_sems, t, (x, y, 1 - c))
               for t, o in enumerate(outs)]
        for cp in cps:
            cp.start()
        for t, o in enumerate(outs):
            _rcopy(o.at[pl.ds(2 * (1 - c), 2)], o.at[pl.ds(2 * (1 - c), 2)], send_sems, recv_sems, t, (x, y, 1 - c)).wait_recv()
        for cp in cps:
            cp.wait_send()

    return _hbm_calls(body, fins, [jax.ShapeDtypeStruct(f.shape, f.dtype) for f in fins], nt, {t: t for t in range(nt)}, name)


def _all_gather_small(buf, name):
    def body(_, out_ref, send_sems, recv_sems):
        x, y, c = _here()
        me = 4 * x + 2 * y + c
        cps = []
        for k in range(1, 8):
            to = (x ^ ((k >> 2) & 1), y ^ ((k >> 1) & 1), c ^ (k & 1))
            peer = 4 * to[0] + 2 * to[1] + to[2]
            cps.append((_rcopy(out_ref.at[me], out_ref.at[me], send_sems, recv_sems, k - 1, to),
                        _rcopy(out_ref.at[peer], out_ref.at[peer], send_sems, recv_sems, k - 1, to)))
        for snd, _ in cps:
            snd.start()
        for _, rcv in cps:
            rcv.wait_recv()
        for snd, _ in cps:
            snd.wait_send()

    return _hbm_calls(body, [buf], [jax.ShapeDtypeStruct(buf.shape, buf.dtype)], 7, {0: 0}, name)[0]


def _place(block, n, index):
    buf = jnp.zeros((n,) + block.shape[1:], block.dtype)
    return lax.dynamic_update_slice_in_dim(buf, block, index, axis=0)


def _add_half(g, other, cidx, name):
    _, R, C = g.shape
    tr = _pick(R, tuple(t for t in (512, 256, 128, 64, 32, 16, 8) if t * C * 4 <= (1 << 21)))

    def body(c_ref, g_ref, o_ref, s_ref, sb_ref):
        s = g_ref[...] + o_ref[...]
        s_ref[...] = s
        sb_ref[...] = s.astype(BF16)

    blk = pl.BlockSpec((None, tr, C), lambda l, i, c: (l, i, 0))
    return pl.pallas_call(
        body,
        grid_spec=pltpu.PrefetchScalarGridSpec(
            num_scalar_prefetch=1, grid=(2, R // tr),
            in_specs=[pl.BlockSpec((None, tr, C), lambda l, i, c: (2 * c[0] + l, i, 0)), blk], out_specs=[blk, blk]),
        out_shape=[jax.ShapeDtypeStruct((2, R, C), F32), jax.ShapeDtypeStruct((2, R, C), BF16)],
        compiler_params=_cp("parallel", "parallel"), name=name)(cidx, g, other)


def _add_chips(s, recv, axis, chip_idx, name):
    _, _, r, cw = recv.shape
    tr = _pick(r, tuple(t for t in (704, 512, 256, 128, 64, 32, 16, 8) if t * cw * 4 <= (1 << 21)))
    nr = r // tr

    def body(c_ref, s_ref, r_ref, out_ref):
        out_ref[...] = ((s_ref[...] + r_ref[0].astype(F32)) + r_ref[1].astype(F32)) + r_ref[2].astype(F32)

    if axis == 2:
        s_map = lambda l, i, c: (l, i, c[0])
    else:
        s_map = lambda l, i, c: (l, c[0] * nr + i, 0)
    return pl.pallas_call(
        body,
        grid_spec=pltpu.PrefetchScalarGridSpec(
            num_scalar_prefetch=1, grid=(2, nr),
            in_specs=[pl.BlockSpec((None, tr, cw), s_map), pl.BlockSpec((3, None, tr, cw), lambda l, i, c: (0, l, i, 0))],
            out_specs=pl.BlockSpec((None, tr, cw), lambda l, i, c: (l, i, 0))),
        out_shape=jax.ShapeDtypeStruct((2, r, cw), F32), compiler_params=_cp("parallel", "parallel"), name=name)(chip_idx, s, recv)


IN_SHARD = IN_COLS // N_SHARD
IN_SLOT = INP // N_SHARD
IN_PIECES = ((0, 512, QA), (512, 640, KA), (640, 768, VA), (768, 1280, QF), (1280, 1792, KF), (1792, 2304, VF),
             (2304, 2312, FL), (2312, 2824, XC), (2824, 3336, YC), (3336, 6408, GT))


def _gathered_to_kernel_cols(w):
    parts, pos = [], 0
    for a, b, k in sorted(IN_PIECES, key=lambda p: p[2]):
        assert k == pos
        while a < b:
            j = a // IN_SHARD
            e = min(b, (j + 1) * IN_SHARD)
            g = j * IN_SLOT + a - j * IN_SHARD
            parts.append(w[..., g:g + e - a])
            pos += e - a
            a = e
    parts.append(jnp.zeros(w.shape[:-1] + (INP - pos,), w.dtype))
    return jnp.concatenate(parts, axis=-1)


def _kernel_to_gathered_cols(w):
    parts = []
    for j in range(N_SHARD):
        lo, hi = j * IN_SHARD, (j + 1) * IN_SHARD
        for a, b, k in IN_PIECES:
            s, e = max(a, lo), min(b, hi)
            if s < e:
                parts.append(w[..., k + s - a:k + e - a])
        parts.append(jnp.zeros(w.shape[:-1] + (IN_SLOT - IN_SHARD,), w.dtype))
    return jnp.concatenate(parts, axis=-1)


def _pair_blocks(w):
    z = jnp.zeros((4, 64, 64), w.dtype)
    w = w.reshape(4, 2, 64, 64)
    top = jnp.concatenate([w[:, 0], z], axis=2)
    bot = jnp.concatenate([z, w[:, 1]], axis=2)
    return jnp.concatenate([top, bot], axis=1)


def _unpair_blocks(w):
    return jnp.stack([w[:, :64, :64], w[:, 64:, 64:]], axis=1).reshape(8, 64, 64)


BIG = ("w_in", "w_branch", "w_out", "w_ffn_in", "w_ffn_out")
TINY = ("conv_w", "meta_tokens")
SMALL = ("rel_bias_table", "norm_mix", "swa_sinks", "fox_forget_bias", "conv_b", "lru_w_r", "lru_b_r", "lru_w_i",
         "lru_b_i", "lru_lambda", "norm_ffn", "norm_final")
SHARD_AXIS = {"conv_w": 2, "meta_tokens": 1}
BIG_AXIS = {"w_in": 2, "w_branch": 2, "w_out": 1, "w_ffn_in": 2, "w_ffn_out": 1}


def _pack(d, names):
    flat = jnp.concatenate([d[n].reshape(-1) for n in names])
    pad = (-flat.shape[0]) % (256 * 128)
    return jnp.concatenate([flat, jnp.zeros((pad,), F32)]).reshape(-1, 128)


def _unpack(buf, names, shapes):
    flat, out, off = buf.reshape(-1), {}, 0
    for n in names:
        sz = int(np.prod(shapes[n]))
        out[n] = flat[off:off + sz].reshape(shapes[n])
        off += sz
    return out


def _local_step(x, tgt, W):
    S = x.shape[0]
    T = S + BLK
    tm = _pick(T, (1408, 384, 128))
    bucket = jnp.asarray(_bucket_table())
    bias = _bias_build(W["rel_bias_table"], bucket, "bias_build")
    h = jnp.concatenate([jnp.zeros((NPAD, D), F32), W["meta_tokens"], x], axis=0)

    saved = []
    for l in range(DEPTH):
        sv = {"h0": h}
        u = _rms_fwd(h, W["norm_mix"][l], f"rms_mix_fwd")
        proj = _mm(u, W["w_in"][l], tm=tm, tn=512, tk=D, name="mm_in_fwd")
        oa = _swa_fwd(proj, bias, W["swa_sinks"][l], "swa_fwd")
        fb = W["fox_forget_bias"][l].reshape(NH, 1)
        qaug, kaug, kaug_t, vm, vo = _fox_prep(proj, _cum_fwd(proj, fb, "cum_fwd"), "fox_prep")
        of, *lse = _fox_fwd(qaug, kaug_t, vo, "fox_fwd")
        lru_vec = jnp.concatenate([W["lru_b_r"][l][None], W["lru_b_i"][l][None], W["lru_lambda"][l][None],
                                   W["conv_b"][l][None], jnp.zeros((4, LW), F32)], axis=0)
        oc, hs = _lru_fwd(proj, W["conv_w"][l], W["lru_w_r"][l], W["lru_w_i"][l], lru_vec, "lru_fwd")
        *bs, merged = _branch_merge_fwd(oa, of, oc, W["w_branch"][l], proj, "branch_merge_fwd")
        h2 = _mm(merged, W["w_out"][l], res=h, tm=tm, tn=512, tk=D, name="mm_out_fwd")
        u2 = _rms_fwd(h2, W["norm_ffn"][l], "rms_ffn_fwd")
        gate, up, act = _ffn_in_swiglu_fwd(u2, W["w_ffn_in"][l], "ffn_in_swiglu_fwd")
        h = _mm(act, W["w_ffn_out"][l], res=h2, tm=tm, tn=512, tk=_pick(DFF, (1408, 256)), name="mm_ffn_out_fwd")
        sv.update(u=u, proj=proj, oa=oa, of=of, oc=oc, lse=lse, hs=hs, fb=fb, qaug=qaug, kaug=kaug, kaug_t=kaug_t, vm=vm, lru_vec=lru_vec,
                  bs=bs, merged=merged, h2=h2, u2=u2, gate=gate, up=up, act=act)
        saved.append(sv)

    tgt_pad = tgt
    dh, dhb, dg_final, loss_vec = _loss_head(h, tgt_pad, W["norm_final"], "loss_head")
    loss = loss_vec[0, 0]

    small = ("norm_mix", "swa_sinks", "fox_forget_bias", "conv_w", "conv_b", "lru_w_r", "lru_b_r", "lru_w_i", "lru_b_i",
             "lru_lambda", "norm_ffn")
    G = {n: [None] * DEPTH for n in small}
    G["norm_final"] = dg_final.reshape(D)
    GW = {n: None for n in BIG}
    dbias = jnp.zeros((NH, BLK, 2 * BLK), F32)
    tkT = tm
    for l in reversed(range(DEPTH)):
        sv = saved[l]
        GW["w_ffn_out"] = _mm(_transpose(sv["act"], "tr_act"), dhb, tm=_pick(DFF, (1408, 256)), tn=D, tk=tkT,
                              slab=(GW["w_ffn_out"], l, DEPTH), name="mm_ffn_out_dw")
        dgate, dup = _ffn_out_dx_swiglu_bwd(dhb, W["w_ffn_out"][l], sv["gate"], sv["up"], "ffn_out_dx_swiglu_bwd")
        u2t = _transpose(sv["u2"], "tr_u2")
        du2 = None
        for half, dpart in enumerate((dgate, dup)):
            GW["w_ffn_in"] = _mm(u2t, dpart, tm=D, tn=_pick(DFF, (1408, 256)), tk=tkT, slab=(GW["w_ffn_in"], l, DEPTH),
                                 col0=half * DFF, cols=2 * DFF, name="mm_ffn_in_dw")
            du2 = _mm(dpart, W["w_ffn_in"][l], tb=True, res=du2, b_k0=half * DFF, tm=tm, tn=512,
                      tk=_pick(DFF, (1408, 256)), name="mm_ffn_in_dx")
        dh, dhb, dgn = _rms_bwd(du2, sv["h2"], W["norm_ffn"][l], dh, "rms_ffn_bwd")
        G["norm_ffn"][l] = dgn.reshape(D)
        GW["w_out"] = _mm(_transpose(sv["merged"], "tr_merged"), dhb, tm=D, tn=D, tk=tkT,
                          slab=(GW["w_out"], l, DEPTH), name="mm_out_dw")
        db0, db1, db2, dg0, dg1, dg2 = _out_dx_merge_bwd(dhb, W["w_out"][l], sv["proj"], *sv["bs"], "out_dx_merge_bwd")
        dos = []
        for g, (o, db) in enumerate(zip((sv["oa"], sv["of"], sv["oc"]), (db0, db1, db2))):
            GW["w_branch"] = _mm(_transpose(o, "tr_branch"), db, tm=LW, tn=D, tk=tkT,
                                 slab=(GW["w_branch"], 3 * l + g, 3 * DEPTH), name="mm_branch_dw")
            dos.append(_mm(db, W["w_branch"][l, g], tb=True, out_dtype=BF16, tm=tm, tn=LW, tk=D, name="mm_branch_dx"))
        dqa, dkb, dvb, dbias, dsk = _swa_bwd(sv["proj"], bias, W["swa_sinks"][l], dos[0], dbias, "swa_bwd")
        dka, dva = _band_fold(dkb, dvb, "swa_band_fold")
        G["swa_sinks"][l] = dsk[0, :NH]
        delta = _fox_delta(dos[1], sv["of"], "fox_delta")
        dqf, dqx, dkf, dvf, dkx = _fox_bwd(sv["qaug"], sv["kaug"], sv["kaug_t"], sv["vm"], dos[1], sv["lse"], delta, "fox_bwd")
        dfl, dfb = _cum_bwd(dqx, dkx, sv["proj"], sv["fb"], "cum_bwd")
        G["fox_forget_bias"][l] = dfb[:, 0]
        dxc, dyc, dwr, dwi, dvec = _lru_bwd(sv["proj"], sv["hs"], dos[2], W["conv_w"][l], W["lru_w_r"][l], W["lru_w_i"][l],
                                            sv["lru_vec"], "lru_bwd")
        G["lru_w_r"][l], G["lru_w_i"][l] = _unpair_blocks(dwr), _unpair_blocks(dwi)
        G["lru_b_r"][l], G["lru_b_i"][l], G["lru_lambda"][l], G["conv_b"][l] = dvec[0], dvec[1], dvec[2], dvec[3]
        G["conv_w"][l] = dvec[4:8]
        dproj = jnp.concatenate([dqa, dqf, dkf, dvf, dxc, dyc, dg0, dg1, dg2, dka, dva, dfl], axis=1)
        GW["w_in"] = _mm(_transpose(sv["u"], "tr_u"), dproj, tm=D, tn=IN_SLOT, tk=tkT,
                         slab=(GW["w_in"], l, DEPTH), name="mm_in_dw")
        du = _mm(dproj, W["w_in"][l], tb=True, tm=tm, tn=512, tk=_pick(INP, (1664, 512)), name="mm_in_dx")
        dh, dhb, dgn = _rms_bwd(du, sv["h0"], W["norm_mix"][l], dh, "rms_mix_bwd")
        G["norm_mix"][l] = dgn.reshape(D)

    grads = {n: (jnp.stack(v) if isinstance(v, list) else v) for n, v in G.items()}
    grads.update(GW)
    grads["w_branch"] = GW["w_branch"].reshape(DEPTH, 3 * LW, D)
    grads["rel_bias_table"] = _bias_bwd(dbias, bucket, "bias_bwd")
    grads["meta_tokens"] = dh[NPAD:BLK]
    return loss, dh[BLK:], grads


NAMES = ("meta_tokens", "rel_bias_table", "norm_mix", "w_in", "swa_sinks", "fox_forget_bias", "conv_w", "conv_b",
         "lru_w_r", "lru_b_r", "lru_w_i", "lru_b_i", "lru_lambda", "w_branch", "w_out", "norm_ffn", "w_ffn_in",
         "w_ffn_out", "norm_final")


def _three_d(n, a):
    return a.reshape(DEPTH, 3 * LW, -1) if n == "w_branch" else a


def _gather_weights(P):
    x, y, c = _here()
    mine, me = 2 * x + y, 4 * x + 2 * y + c
    fulls = []
    for n in BIG:
        shard = _three_d(n, P[n].astype(BF16))
        if n == "w_in":
            shard = jnp.pad(shard, ((0, 0), (0, 0), (0, IN_SLOT - IN_SHARD)))
        zero = jnp.zeros_like(shard)
        fulls.append(jnp.concatenate([jnp.where(mine == j, shard, zero) for j in range(N_SHARD)], axis=BIG_AXIS[n]))
    full = dict(zip(BIG, _all_gather_weights(fulls, [BIG_AXIS[n] for n in BIG], "ag_weights")))
    full["w_in"] = _gathered_to_kernel_cols(full["w_in"])
    full["w_branch"] = full["w_branch"].reshape(DEPTH, 3, LW, D)
    tiny = _all_gather_small(_place(_pack(P, TINY)[None], 8, me), "ag_tiny_weights")
    parts = [_unpack(tiny[2 * j], TINY, {n: P[n].shape for n in TINY}) for j in range(N_SHARD)]
    for n in TINY:
        full[n] = jnp.concatenate([p[n] for p in parts], axis=SHARD_AXIS[n])
    for n in SMALL:
        full[n] = P[n]
    full["lru_w_r"] = jnp.stack([_pair_blocks(P["lru_w_r"][l]) for l in range(DEPTH)]).astype(BF16)
    full["lru_w_i"] = jnp.stack([_pair_blocks(P["lru_w_i"][l]) for l in range(DEPTH)]).astype(BF16)
    return full


def _reduce_grads(grads, P):
    x, y, c = _here()
    mine, me = 2 * x + y, 4 * x + 2 * y + c
    cidx = jnp.reshape(c, (1,)).astype(jnp.int32)
    chip = jnp.reshape(mine, (1,)).astype(jnp.int32)
    axes = [BIG_AXIS[n] for n in BIG]
    gs = [grads[n] for n in BIG]
    pairs = [_add_half(g, r, cidx, "rs_add_half_" + n) for n, g, r in zip(BIG, gs, _swap_halves(gs, "rs_swap_halves"))]
    ss, sbs = [list(t) for t in zip(*pairs)]
    ss[0], sbs[0] = _kernel_to_gathered_cols(ss[0]), _kernel_to_gathered_cols(sbs[0])
    recv = _scatter_to_chips(sbs, axes, "rs_scatter")
    tots = [_add_chips(s, r, ax, chip, "rs_add_chips_" + n) for n, s, r, ax in zip(BIG, ss, recv, axes)]
    fins = dict(zip(BIG, _join_halves([_place(t, DEPTH, 2 * c) for t in tots], "rs_join_halves")))
    out = {n: fins[n].reshape(P[n].shape) for n in BIG if n != "w_in"}
    out["w_in"] = fins["w_in"][:, :, :IN_SHARD]
    names = SMALL + TINY
    gathered = _all_gather_small(_place(_pack(grads, names)[None], 8, me), "ag_small_grads")
    small = _unpack(_sum_lead(gathered, "sum_small_grads"), names, {n: grads[n].shape for n in names})
    for n in SMALL:
        out[n] = small[n]
    for n in TINY:
        w = P[n].shape[SHARD_AXIS[n]]
        out[n] = lax.dynamic_slice_in_dim(small[n], mine * w, w, axis=SHARD_AXIS[n])
    return out


def _update(P, Gd, M, V):
    delta, new_m, new_v = {}, {}, {}
    for n in BIG + TINY:
        shp = P[n].shape
        two = (int(np.prod(shp[:-1])), shp[-1])
        d, m, v = _adamw(P[n].reshape(two), Gd[n].reshape(two), M[n].reshape(two), V[n].reshape(two), "adamw_" + n)
        delta[n], new_m[n], new_v[n] = d.reshape(shp), m.reshape(shp), v.reshape(shp)
    shapes = {n: P[n].shape for n in SMALL}
    d, m, v = _adamw(_pack(P, SMALL), _pack(Gd, SMALL), _pack(M, SMALL), _pack(V, SMALL), "adamw_small")
    for dst, buf in ((delta, d), (new_m, m), (new_v, v)):
        dst.update(_unpack(buf, SMALL, shapes))
    return delta, new_m, new_v


def kernel(x, meta_tokens, rel_bias_table, norm_mix, w_in, swa_sinks, fox_forget_bias, conv_w, conv_b, lru_w_r, lru_b_r, lru_w_i, lru_b_i, lru_lambda, w_branch, w_out, norm_ffn, w_ffn_in, w_ffn_out, norm_final, loss_target, m_meta_tokens, m_rel_bias_table, m_norm_mix, m_w_in, m_swa_sinks, m_fox_forget_bias, m_conv_w, m_conv_b, m_lru_w_r, m_lru_b_r, m_lru_w_i, m_lru_b_i, m_lru_lambda, m_w_branch, m_w_out, m_norm_ffn, m_w_ffn_in, m_w_ffn_out, m_norm_final, v_meta_tokens, v_rel_bias_table, v_norm_mix, v_w_in, v_swa_sinks, v_fox_forget_bias, v_conv_w, v_conv_b, v_lru_w_r, v_lru_b_r, v_lru_w_i, v_lru_b_i, v_lru_lambda, v_w_branch, v_w_out, v_norm_ffn, v_w_ffn_in, v_w_ffn_out, v_norm_final):
    P = dict(zip(NAMES, (meta_tokens, rel_bias_table, norm_mix, w_in, swa_sinks, fox_forget_bias, conv_w, conv_b, lru_w_r,
                         lru_b_r, lru_w_i, lru_b_i, lru_lambda, w_branch, w_out, norm_ffn, w_ffn_in, w_ffn_out, norm_final)))
    M = dict(zip(NAMES, (m_meta_tokens, m_rel_bias_table, m_norm_mix, m_w_in, m_swa_sinks, m_fox_forget_bias, m_conv_w,
                         m_conv_b, m_lru_w_r, m_lru_b_r, m_lru_w_i, m_lru_b_i, m_lru_lambda, m_w_branch, m_w_out, m_norm_ffn,
                         m_w_ffn_in, m_w_ffn_out, m_norm_final)))
    V = dict(zip(NAMES, (v_meta_tokens, v_rel_bias_table, v_norm_mix, v_w_in, v_swa_sinks, v_fox_forget_bias, v_conv_w,
                         v_conv_b, v_lru_w_r, v_lru_b_r, v_lru_w_i, v_lru_b_i, v_lru_lambda, v_w_branch, v_w_out, v_norm_ffn,
                         v_w_ffn_in, v_w_ffn_out, v_norm_final)))
    W = _gather_weights(P)
    loss_local, grad_x, grads = _local_step(x[0], loss_target[0], W)
    loss = lax.psum(loss_local, ("x", "y", "c"))
    Gd = _reduce_grads(grads, P)
    delta, new_m, new_v = _update(P, Gd, M, V)
    return (loss, grad_x[None], *[Gd[n] for n in NAMES], *[delta[n] for n in NAMES],
            *[new_m[n] for n in NAMES], *[new_v[n] for n in NAMES])
```

```python
import functools
import math

import numpy as np
import jax
import jax.numpy as jnp
from jax import lax
from jax.experimental import pallas as pl
from jax.experimental.pallas import tpu as pltpu

F32, BF16 = jnp.float32, jnp.bfloat16
MESH = pl.DeviceIdType.MESH
ANY = pl.BlockSpec(memory_space=pl.ANY)
SMEM = pl.BlockSpec(memory_space=pltpu.SMEM)

D = 1024
DEPTH = 4
BLK = 128
N_META = 16
NPAD = 112
NH = 8
LW = 512
DFF = 2816
EPS = 1e-6
NEG = -1e30
SCALE = 0.125
LRU_C = 8.0
REL_BUCKETS = 32
N_SHARD = 4
QA, QF, KF, VF, XC, YC, GT, KA, VA, FL, INP = 0, 512, 1024, 1536, 2048, 2560, 3072, 6144, 6272, 6400, 6656
IN_COLS = 6408
VMEM_LIMIT = 48 * 1024 * 1024

ADAM_LR, ADAM_B1, ADAM_B2, ADAM_EPS, ADAM_WD, ADAM_STEP = 0.001, 0.9, 0.999, 1e-08, 0.01, 10


def _cp(*sem):
    return pltpu.CompilerParams(dimension_semantics=sem or None, vmem_limit_bytes=VMEM_LIMIT)


def _pick(n, prefs):
    for p in prefs:
        if n % p == 0:
            return p
    return n


def _rt(T):
    return _pick(T, (384, 128))


def _sigmoid(z):
    return 1.0 / (1.0 + jnp.exp(-z))


def _log_sigmoid(z):
    return jnp.minimum(z, 0.0) - jnp.log(1.0 + jnp.exp(-jnp.abs(z)))


def _gelu(y):
    c = math.sqrt(2.0 / math.pi)
    return 0.5 * y * (1.0 + jnp.tanh(c * (y + 0.044715 * y * y * y)))


def _gelu_grad(y):
    c = math.sqrt(2.0 / math.pi)
    t = jnp.tanh(c * (y + 0.044715 * y * y * y))
    return 0.5 * (1.0 + t) + 0.5 * y * (1.0 - t * t) * c * (1.0 + 3.0 * 0.044715 * y * y)


def _neg_expm1(z):
    series = -z * (1.0 + z * (0.5 + z * (1.0 / 6.0 + z * (1.0 / 24.0 + z * (1.0 / 120.0)))))
    return jnp.where(z > -0.1, series, 1.0 - jnp.exp(z))


def _dot(a, b, ca, cb):
    return lax.dot_general(a, b, (((ca,), (cb,)), ((), ())), preferred_element_type=F32)


def _mm(a, b, *, ta=False, tb=False, res=None, out_dtype=F32, tm, tn, tk, name, slab=None, b_k0=0, col0=0, cols=None):
    M, K = (a.shape[1], a.shape[0]) if ta else a.shape
    N = b.shape[0] if tb else b.shape[1]
    assert (b.shape[1] if tb else b.shape[0]) >= K + b_k0 and M % tm == 0 and N % tn == 0 and K % tk == 0, (name, a.shape, b.shape)
    assert b_k0 % tk == 0 and col0 % tn == 0
    nk, kb, jb = K // tk, b_k0 // tk, col0 // tn
    ca, cb = (0 if ta else 1), (1 if tb else 0)
    n_in = 2 + (res is not None) + (slab is not None and slab[0] is not None)

    def body(*refs):
        a_ref, b_ref = refs[:2]
        r_ref = refs[2] if res is not None else None
        o_ref = refs[n_in]
        part = _dot(a_ref[...].astype(BF16), b_ref[...].astype(BF16), ca, cb)

        def fin(acc):
            if res is not None:
                acc = acc + r_ref[...]
            o_ref[...] = acc.astype(out_dtype)

        if nk == 1:
            fin(part)
        else:
            acc_ref = refs[-1]
            k = pl.program_id(2)

            @pl.when(k == 0)
            def _():
                acc_ref[...] = part

            @pl.when(k > 0)
            def _():
                acc_ref[...] += part

            @pl.when(k == nk - 1)
            def _():
                fin(acc_ref[...])

    a_spec = pl.BlockSpec((tk, tm), lambda i, j, k: (k, i)) if ta else pl.BlockSpec((tm, tk), lambda i, j, k: (i, k))
    b_spec = (pl.BlockSpec((tn, tk), lambda i, j, k: (j, k + kb)) if tb
              else pl.BlockSpec((tk, tn), lambda i, j, k: (k + kb, j)))
    o_spec = pl.BlockSpec((tm, tn), lambda i, j, k: (i, j))
    in_specs, ops = [a_spec, b_spec], [a, b]
    if res is not None:
        in_specs.append(o_spec)
        ops.append(res)
    out_shape, aliases = jax.ShapeDtypeStruct((M, N), out_dtype), {}
    if slab is not None:
        buf, idx, n = slab
        o_spec = pl.BlockSpec((None, tm, tn), lambda i, j, k: (idx, i, j + jb))
        out_shape = jax.ShapeDtypeStruct((n, M, cols or N), out_dtype)
        if buf is not None:
            aliases = {len(ops): 0}
            in_specs.append(ANY)
            ops.append(buf)
    return pl.pallas_call(
        body, grid=(M // tm, N // tn, nk), in_specs=in_specs, out_specs=o_spec, out_shape=out_shape,
        input_output_aliases=aliases, scratch_shapes=[pltpu.VMEM((tm, tn), F32)] if nk > 1 else [],
        compiler_params=_cp("parallel", "parallel", "arbitrary"), name=name)(*ops)


def _transpose(x, name):
    T, C = x.shape
    tr, tc = _rt(T), _pick(C, (1408, 1024, 512, 256, 128))

    def body(x_ref, o_ref):
        o_ref[...] = x_ref[...].T

    return pl.pallas_call(
        body, grid=(T // tr, C // tc), in_specs=[pl.BlockSpec((tr, tc), lambda i, j: (i, j))],
        out_specs=pl.BlockSpec((tc, tr), lambda i, j: (j, i)), out_shape=jax.ShapeDtypeStruct((C, T), x.dtype),
        compiler_params=_cp("parallel", "parallel"), name=name)(x)


def _rms_fwd(h, g, name):
    T = h.shape[0]
    tr = _rt(T)

    def body(h_ref, g_ref, u_ref):
        x = h_ref[...]
        r = lax.rsqrt(jnp.mean(x * x, axis=-1, keepdims=True) + EPS)
        u_ref[...] = (x * r * g_ref[...]).astype(BF16)

    return pl.pallas_call(
        body, grid=(T // tr,),
        in_specs=[pl.BlockSpec((tr, D), lambda i: (i, 0)), pl.BlockSpec((1, D), lambda i: (0, 0))],
        out_specs=pl.BlockSpec((tr, D), lambda i: (i, 0)), out_shape=jax.ShapeDtypeStruct((T, D), BF16),
        compiler_params=_cp("parallel"), name=name)(h, g.reshape(1, D))


def _rms_bwd(du, h, g, dres, name):
    T = h.shape[0]
    tr = _rt(T)

    def body(du_ref, h_ref, g_ref, dres_ref, dh_ref, dhb_ref, dg_ref):
        x = h_ref[...]
        r = lax.rsqrt(jnp.mean(x * x, axis=-1, keepdims=True) + EPS)
        xh = x * r
        dy = du_ref[...]
        dxh = dy * g_ref[...]
        dx = r * (dxh - xh * jnp.mean(dxh * xh, axis=-1, keepdims=True))
        dh = dres_ref[...] + dx
        dh_ref[...] = dh
        dhb_ref[...] = dh.astype(BF16)
        part = jnp.sum(dy * xh, axis=0, keepdims=True)

        @pl.when(pl.program_id(0) == 0)
        def _():
            dg_ref[...] = part

        @pl.when(pl.program_id(0) > 0)
        def _():
            dg_ref[...] += part

    row = pl.BlockSpec((tr, D), lambda i: (i, 0))
    vec = pl.BlockSpec((1, D), lambda i: (0, 0))
    return pl.pallas_call(
        body, grid=(T // tr,), in_specs=[row, row, vec, row], out_specs=[row, row, vec],
        out_shape=[jax.ShapeDtypeStruct((T, D), F32), jax.ShapeDtypeStruct((T, D), BF16), jax.ShapeDtypeStruct((1, D), F32)],
        compiler_params=_cp("arbitrary"), name=name)(du, h, g.reshape(1, D), dres)


def _loss_head(h, tgt, g, name):
    T = h.shape[0]
    nb = T // BLK

    def body(h_ref, t_ref, g_ref, dh_ref, dhb_ref, dg_ref, loss_ref):
        i = pl.program_id(0)
        x = h_ref[...]
        r = lax.rsqrt(jnp.mean(x * x, axis=-1, keepdims=True) + EPS)
        xh = x * r
        gv = g_ref[...]
        tok = i >= 1
        err = jnp.where(tok, xh * gv - t_ref[...], 0.0)
        dy = err * (1.0 / D)
        dxh = dy * gv
        dx = r * (dxh - xh * jnp.mean(dxh * xh, axis=-1, keepdims=True))
        dh_ref[...] = dx
        dhb_ref[...] = dx.astype(BF16)
        dg = jnp.sum(dy * xh, axis=0, keepdims=True)
        ls = jnp.zeros((1, BLK), F32) + jnp.sum(err * err) * (0.5 / D)

        @pl.when(i == 0)
        def _():
            dg_ref[...] = dg
            loss_ref[...] = ls

        @pl.when(i > 0)
        def _():
            dg_ref[...] += dg
            loss_ref[...] += ls

    row = pl.BlockSpec((BLK, D), lambda i: (i, 0))
    vec = pl.BlockSpec((1, D), lambda i: (0, 0))
    return pl.pallas_call(
        body, grid=(nb,),
        in_specs=[row, pl.BlockSpec((BLK, D), lambda i: (jnp.maximum(i - 1, 0), 0)), vec],
        out_specs=[row, row, vec, pl.BlockSpec((1, BLK), lambda i: (0, 0))],
        out_shape=[jax.ShapeDtypeStruct((T, D), F32), jax.ShapeDtypeStruct((T, D), BF16),
                   jax.ShapeDtypeStruct((1, D), F32), jax.ShapeDtypeStruct((1, BLK), F32)],
        compiler_params=_cp("arbitrary"), name=name)(h, tgt, g.reshape(1, D))


def _bucket_table():
    q = np.arange(BLK)[:, None]
    k = np.arange(2 * BLK)[None, :]
    d = np.maximum(q + BLK - k, 0)
    max_exact = REL_BUCKETS // 2
    scaled = np.log(np.maximum(d, 1).astype(np.float32) / np.float32(max_exact)) / np.float32(math.log(128 / max_exact))
    large = np.minimum(max_exact + (scaled.astype(np.float32) * (REL_BUCKETS - max_exact)).astype(np.int32), REL_BUCKETS - 1)
    return np.where(d < max_exact, d, large).astype(np.int32)


def _bias_build(table, bucket, name):
    def body(t_ref, bk_ref, o_ref):
        bk = bk_ref[...]
        for h in range(NH):
            acc = jnp.zeros((BLK, 2 * BLK), F32)
            for b in range(REL_BUCKETS):
                acc = jnp.where(bk == b, t_ref[b, h], acc)
            o_ref[h] = acc

    return pl.pallas_call(
        body, in_specs=[SMEM, pl.BlockSpec(memory_space=pltpu.VMEM)], out_specs=pl.BlockSpec(memory_space=pltpu.VMEM),
        out_shape=jax.ShapeDtypeStruct((NH, BLK, 2 * BLK), F32), compiler_params=_cp(), name=name)(table, bucket)


def _bias_bwd(dbias, bucket, name):
    def body(d_ref, bk_ref, o_ref):
        bk = bk_ref[...]
        for h in range(NH):
            dh = d_ref[h]
            for b in range(REL_BUCKETS):
                o_ref[b, h] = jnp.sum(jnp.where(bk == b, dh, 0.0))

    return pl.pallas_call(
        body, in_specs=[pl.BlockSpec(memory_space=pltpu.VMEM)] * 2, out_specs=SMEM,
        out_shape=jax.ShapeDtypeStruct((REL_BUCKETS, NH), F32), compiler_params=_cp(), name=name)(dbias, bucket)


def _swa_specs(nq_cols):
    prev = lambda n: jnp.maximum(n - 1, 0)
    return [
        pl.BlockSpec((BLK, nq_cols), lambda n: (n, QA // nq_cols)),
        pl.BlockSpec((BLK, BLK), lambda n: (prev(n), KA // BLK)), pl.BlockSpec((BLK, BLK), lambda n: (n, KA // BLK)),
        pl.BlockSpec((BLK, BLK), lambda n: (prev(n), VA // BLK)), pl.BlockSpec((BLK, BLK), lambda n: (n, VA // BLK)),
    ]


def _swa_mask(n):
    row = lax.broadcasted_iota(jnp.int32, (BLK, 2 * BLK), 0)
    col = lax.broadcasted_iota(jnp.int32, (BLK, 2 * BLK), 1)
    dist = row + BLK - col
    return (dist >= 0) & (dist < BLK) & ((n - 1) * BLK + col >= NPAD)


def _swa_probs(qm, ksel, mask, bias_h, sink):
    s = _dot(qm, ksel, 1, 1) * SCALE
    s = jnp.where(mask, s + bias_h, NEG)
    m = jnp.maximum(jnp.max(s, axis=-1, keepdims=True), sink)
    p = jnp.exp(s - m)
    psink = jnp.exp(sink - m)
    inv = 1.0 / (jnp.sum(p, axis=-1, keepdims=True) + psink)
    return p * inv, psink * inv


def _swa_fwd(proj, bias, sinks, name):
    T = proj.shape[0]
    nb = T // BLK

    def body(sk_ref, q_ref, kp_ref, kc_ref, vp_ref, vc_ref, b_ref, o_ref):
        n = pl.program_id(0)
        lo = lax.broadcasted_iota(jnp.int32, (1, BLK), 1) < 64
        kb = jnp.concatenate([kp_ref[...], kc_ref[...]], axis=0)
        vb = jnp.concatenate([vp_ref[...], vc_ref[...]], axis=0)
        kbs = (kb.astype(BF16), pltpu.roll(kb, 64, 1).astype(BF16))
        vbs = (vb, pltpu.roll(vb, 64, 1))
        mask = _swa_mask(n)
        outs = []
        for pr in range(NH // 2):
            qp = q_ref[:, pr * BLK:(pr + 1) * BLK]
            kv = pr // 2
            acc = jnp.zeros((BLK, BLK), F32)
            for e in range(2):
                lm = lo if e == 0 else jnp.logical_not(lo)
                sw = 0 if kv == e else 1
                qm = jnp.where(lm, qp, 0.0).astype(BF16)
                pn, _ = _swa_probs(qm, kbs[sw], mask, b_ref[2 * pr + e], sk_ref[2 * pr + e])
                acc = acc + _dot(pn.astype(BF16), jnp.where(lm, vbs[sw], 0.0).astype(BF16), 1, 0)
            outs.append(acc)
        o_ref[...] = jnp.concatenate(outs, axis=1).astype(BF16)

    return pl.pallas_call(
        body, grid=(nb,),
        in_specs=[SMEM] + _swa_specs(512) + [pl.BlockSpec((NH, BLK, 2 * BLK), lambda n: (0, 0, 0))],
        out_specs=pl.BlockSpec((BLK, 512), lambda n: (n, 0)), out_shape=jax.ShapeDtypeStruct((T, 512), BF16),
        compiler_params=_cp("parallel"), name=name)(sinks, proj, proj, proj, proj, proj, bias)


def _swa_bwd(proj, bias, sinks, do, dbias_in, name):
    T = proj.shape[0]
    nb = T // BLK

    def body(sk_ref, q_ref, kp_ref, kc_ref, vp_ref, vc_ref, b_ref, do_ref, dbi_ref,
             dq_ref, dk_ref, dv_ref, db_ref, dsk_ref, sk_acc):
        n = pl.program_id(0)
        lane = lax.broadcasted_iota(jnp.int32, (1, BLK), 1)
        lo = lane < 64
        kb = jnp.concatenate([kp_ref[...], kc_ref[...]], axis=0)
        vb = jnp.concatenate([vp_ref[...], vc_ref[...]], axis=0)
        kbs = (kb, pltpu.roll(kb, 64, 1))
        vbs = (vb, pltpu.roll(vb, 64, 1))
        mask = _swa_mask(n)

        @pl.when(n == 0)
        def _():
            db_ref[...] = dbi_ref[...]
            sk_acc[...] = jnp.zeros_like(sk_acc)

        dqs = []
        dk = jnp.zeros((2 * BLK, BLK), F32)
        dv = jnp.zeros((2 * BLK, BLK), F32)
        for pr in range(NH // 2):
            qp = q_ref[:, pr * BLK:(pr + 1) * BLK]
            dop = do_ref[:, pr * BLK:(pr + 1) * BLK].astype(F32)
            kv = pr // 2
            dq = jnp.zeros((BLK, BLK), F32)
            for e in range(2):
                h = 2 * pr + e
                lm = lo if e == 0 else jnp.logical_not(lo)
                sw = 0 if kv == e else 1
                qm = jnp.where(lm, qp, 0.0)
                dom = jnp.where(lm, dop, 0.0)
                pn, ps = _swa_probs(qm.astype(BF16), kbs[sw].astype(BF16), mask, b_ref[h], sk_ref[h])
                dp = _dot(dom.astype(BF16), vbs[sw].astype(BF16), 1, 1)
                delta = jnp.sum(pn * dp, axis=-1, keepdims=True)
                ds = pn * (dp - delta)
                db_ref[h] += ds
                sk_acc[...] += jnp.where(lane == h, -(ps * delta), 0.0)
                dsb = (ds * SCALE).astype(BF16)
                dq = dq + _dot(dsb, jnp.where(lm, kbs[sw], 0.0).astype(BF16), 1, 0)
                qk = qm if sw == 0 else pltpu.roll(qm, 64, 1)
                dok = dom if sw == 0 else pltpu.roll(dom, 64, 1)
                dk = dk + _dot(dsb, qk.astype(BF16), 0, 0)
                dv = dv + _dot(pn.astype(BF16), dok.astype(BF16), 0, 0)
            dqs.append(dq)
        dq_ref[...] = jnp.concatenate(dqs, axis=1).astype(BF16)
        dk_ref[0] = dk
        dv_ref[0] = dv

        @pl.when(n == nb - 1)
        def _():
            dsk_ref[...] = jnp.sum(sk_acc[...], axis=0, keepdims=True)

    full_b = pl.BlockSpec((NH, BLK, 2 * BLK), lambda n: (0, 0, 0))
    band = pl.BlockSpec((1, 2 * BLK, BLK), lambda n: (n, 0, 0))
    return pl.pallas_call(
        body, grid=(nb,),
        in_specs=[SMEM] + _swa_specs(512) + [full_b, pl.BlockSpec((BLK, 512), lambda n: (n, 0)), full_b],
        out_specs=[pl.BlockSpec((BLK, 512), lambda n: (n, 0)), band, band, full_b, pl.BlockSpec((1, BLK), lambda n: (0, 0))],
        out_shape=[jax.ShapeDtypeStruct((T, 512), BF16), jax.ShapeDtypeStruct((nb, 2 * BLK, BLK), F32),
                   jax.ShapeDtypeStruct((nb, 2 * BLK, BLK), F32), jax.ShapeDtypeStruct((NH, BLK, 2 * BLK), F32),
                   jax.ShapeDtypeStruct((1, BLK), F32)],
        scratch_shapes=[pltpu.VMEM((BLK, BLK), F32)],
        compiler_params=_cp("arbitrary"), name=name)(sinks, proj, proj, proj, proj, proj, bias, do, dbias_in)


def _band_fold(dkb, dvb, name):
    nb = dkb.shape[0]

    def body(ko_ref, kn_ref, vo_ref, vn_ref, dk_ref, dv_ref):
        last = pl.program_id(0) == nb - 1
        dk_ref[...] = (ko_ref[0] + jnp.where(last, 0.0, kn_ref[0])).astype(BF16)
        dv_ref[...] = (vo_ref[0] + jnp.where(last, 0.0, vn_ref[0])).astype(BF16)

    own = pl.BlockSpec((1, BLK, BLK), lambda j: (j, 1, 0))
    nxt = pl.BlockSpec((1, BLK, BLK), lambda j: (jnp.minimum(j + 1, nb - 1), 0, 0))
    out = pl.BlockSpec((BLK, BLK), lambda j: (j, 0))
    return pl.pallas_call(
        body, grid=(nb,), in_specs=[own, nxt, own, nxt], out_specs=[out, out],
        out_shape=[jax.ShapeDtypeStruct((nb * BLK, BLK), BF16)] * 2,
        compiler_params=_cp("parallel"), name=name)(dkb, dkb, dvb, dvb)


def _token_major(x, width):
    full = jnp.concatenate([x, jnp.zeros((BLK - NH, BLK), F32)], axis=0).T
    return full if width == BLK else jnp.concatenate([full, jnp.zeros((BLK, width - BLK), F32)], axis=1)


def _cum_fwd(proj, fb, name):
    T = proj.shape[0]

    def body(z_ref, fb_ref, c_ref, carry):
        b = pl.program_id(0)
        lane = lax.broadcasted_iota(jnp.int32, (NH, BLK), 1)

        @pl.when(b == 0)
        def _():
            carry[...] = jnp.zeros_like(carry)

        z = z_ref[...].T[0:NH, :] + fb_ref[...]
        x = jnp.where(b * BLK + lane >= NPAD, _log_sigmoid(z), 0.0)
        s = 1
        while s < BLK:
            x = x + jnp.where(lane >= s, pltpu.roll(x, s, 1), 0.0)
            s *= 2
        x = x + carry[...]
        carry[...] = jnp.zeros((NH, BLK), F32) + jnp.sum(jnp.where(lane == BLK - 1, x, 0.0), axis=-1, keepdims=True)
        c_ref[...] = _token_major(x, BLK)

    return pl.pallas_call(
        body, grid=(T // BLK,),
        in_specs=[pl.BlockSpec((BLK, BLK), lambda b: (b, FL // BLK)), pl.BlockSpec((NH, 1), lambda b: (0, 0))],
        out_specs=pl.BlockSpec((BLK, BLK), lambda b: (b, 0)), out_shape=jax.ShapeDtypeStruct((T, BLK), F32),
        scratch_shapes=[pltpu.VMEM((NH, BLK), F32)], compiler_params=_cp("arbitrary"), name=name)(proj, fb)


def _cum_bwd(dqx, dkx, proj, fb, name):
    T = proj.shape[0]
    nb = T // BLK

    def body(dq_ref, dk_ref, z_ref, fb_ref, dz_ref, db_ref, carry):
        k = pl.program_id(0)
        b = nb - 1 - k
        lane = lax.broadcasted_iota(jnp.int32, (NH, BLK), 1)

        @pl.when(k == 0)
        def _():
            carry[...] = jnp.zeros_like(carry)
            db_ref[...] = jnp.zeros_like(db_ref)

        def picked(ref, r_first, r_second):
            rows = []
            for p in range(NH // 2):
                t_ = ref[:, p * BLK:(p + 1) * BLK].T
                rows += [t_[r_first:r_first + 1, :], t_[r_second:r_second + 1, :]]
            return jnp.concatenate(rows, axis=0)

        x = picked(dq_ref, 64, 0) - picked(dk_ref, 67, 3)
        s = 1
        while s < BLK:
            x = x + jnp.where(lane < BLK - s, pltpu.roll(x, BLK - s, 1), 0.0)
            s *= 2
        x = x + carry[...]
        carry[...] = jnp.zeros((NH, BLK), F32) + jnp.sum(jnp.where(lane == 0, x, 0.0), axis=-1, keepdims=True)
        z = z_ref[...].T[0:NH, :] + fb_ref[...]
        dz = jnp.where(b * BLK + lane >= NPAD, x * _sigmoid(-z), 0.0)
        db_ref[...] += jnp.sum(dz, axis=-1, keepdims=True)
        dz_ref[...] = _token_major(dz, 2 * BLK).astype(BF16)

    rev = lambda k: nb - 1 - k
    wide = pl.BlockSpec((BLK, 512), lambda k: (rev(k), 0))
    return pl.pallas_call(
        body, grid=(nb,),
        in_specs=[wide, wide, pl.BlockSpec((BLK, BLK), lambda k: (rev(k), FL // BLK)), pl.BlockSpec((NH, 1), lambda k: (0, 0))],
        out_specs=[pl.BlockSpec((BLK, 2 * BLK), lambda k: (rev(k), 0)), pl.BlockSpec((NH, BLK), lambda k: (0, 0))],
        out_shape=[jax.ShapeDtypeStruct((T, 2 * BLK), BF16), jax.ShapeDtypeStruct((NH, BLK), F32)],
        scratch_shapes=[pltpu.VMEM((NH, BLK), F32)], compiler_params=_cp("arbitrary"), name=name)(dqx, dkx, proj, fb)


def _fox_prep(proj, ccol, name):
    T = proj.shape[0]
    tr = _pick(T, (1408, 384, 128))

    def body(q_ref, k_ref, v_ref, cc_ref, qa_ref, ka_ref, kt_ref, vm_ref, vo_ref):
        h = pl.program_id(1)
        lane = lax.broadcasted_iota(jnp.int32, (1, BLK), 1)
        own = (lane >> 6) == (h & 1)
        a0 = 64 * (1 - (h & 1))
        c = _lane_pick(cc_ref[...], lane, h)
        hi = c.astype(BF16).astype(F32)
        mid = (c - hi).astype(BF16).astype(F32)
        lo = (c - hi - mid).astype(BF16).astype(F32)
        ones = (lane >= a0 + 3) & (lane < a0 + 6)
        qa = jnp.where(own, q_ref[...] * SCALE, jnp.where(ones, 1.0, 0.0))
        qa = jnp.where(lane == a0, hi, jnp.where(lane == a0 + 1, mid, jnp.where(lane == a0 + 2, lo, qa)))
        ones = (lane >= a0) & (lane < a0 + 3)
        ka = jnp.where(own, k_ref[...], jnp.where(ones, 1.0, 0.0))
        ka = jnp.where(lane == a0 + 3, -hi, jnp.where(lane == a0 + 4, -mid, jnp.where(lane == a0 + 5, -lo, ka)))
        qa_ref[...] = qa.astype(BF16)
        kab = ka.astype(BF16)
        ka_ref[...] = kab
        kt_ref[...] = kab.T
        vm = jnp.where(own, v_ref[...], 0.0)
        vm_ref[...] = vm.astype(BF16)
        vo_ref[...] = jnp.where(lane == a0, 1.0, vm).astype(BF16)

    pair = lambda col0: pl.BlockSpec((tr, BLK), lambda i, h: (i, col0 // BLK + (h >> 1)))
    out = pl.BlockSpec((None, tr, BLK), lambda i, h: (h, i, 0))
    out_t = pl.BlockSpec((None, BLK, tr), lambda i, h: (h, 0, i))
    tok = jax.ShapeDtypeStruct((NH, T, BLK), BF16)
    return pl.pallas_call(
        body, grid=(T // tr, NH), in_specs=[pair(QF), pair(KF), pair(VF), pl.BlockSpec((tr, BLK), lambda i, h: (i, 0))],
        out_specs=[out, out, out_t, out, out], out_shape=[tok, tok, jax.ShapeDtypeStruct((NH, BLK, T), BF16), tok, tok],
        compiler_params=_cp("parallel", "arbitrary"), name=name)(proj, proj, proj, ccol)


def _fox_fwd(qaug, kaug_t, vo, name, gather=None):
    T = qaug.shape[1]
    t = _rt(T)
    nt = T // t
    ng = len(gather[0]) if gather else 0

    def body(q0, q1, k0, k1, v0, v1, *rest):
        o_ref, lse0_ref, lse1_ref = rest[ng:ng + 3]
        m_ref, acc_ref = rest[2 * ng + 3:2 * ng + 5]
        p_, i, j = pl.program_id(0), pl.program_id(1), pl.program_id(2)
        lane = lax.broadcasted_iota(jnp.int32, (1, BLK), 1)
        lo = lane < 64
        if gather:
            start, finish = _gather_plan(rest[ng + 3:2 * ng + 3], gather[1], *rest[2 * ng + 5:])
            pl.when((p_ == 0) & (i == 0) & (j == 0))(start)

        @pl.when(j == 0)
        def _():
            m_ref[...] = jnp.full_like(m_ref, NEG)
            acc_ref[...] = jnp.zeros_like(acc_ref)

        def step(masked):
            for e, (q_ref, k_ref, v_ref) in enumerate(((q0, k0, v0), (q1, k1, v1))):
                s = _dot(q_ref[...], k_ref[...], 1, 0)
                if masked:
                    s = jnp.where(_fox_mask(i, j, t), s, NEG)
                m_old = m_ref[e]
                m_new = jnp.maximum(m_old, jnp.max(s, axis=-1, keepdims=True))
                m_ref[e] = m_new
                pe = jnp.exp(s - jnp.concatenate([m_new] * (t // BLK), axis=1))
                acc_ref[e] = jnp.exp(m_old - m_new) * acc_ref[e] + _dot(pe.astype(BF16), v_ref[...], 1, 0)

        pl.when((j < i) & (j > 0))(lambda: step(False))
        pl.when((j == i) | ((j == 0) & (i > 0)))(lambda: step(True))

        @pl.when(j == i)
        def _():
            rows = i * t + lax.broadcasted_iota(jnp.int32, (t, 1), 0)
            l0, l1 = _lane_pick(acc_ref[0], lane, 64), _lane_pick(acc_ref[1], lane, 0)
            o = jnp.where(lo, acc_ref[0] / l0, acc_ref[1] / l1)
            o_ref[...] = jnp.where(rows >= NPAD, o, 0.0).astype(BF16)
            lse0_ref[...] = m_ref[0] + jnp.log(l0)
            lse1_ref[...] = m_ref[1] + jnp.log(l1)

        if gather:
            pl.when((p_ == NH // 2 - 1) & (i == nt - 1) & (j == nt - 1))(finish)

    kj = lambda i, j: jnp.minimum(j, i)
    qs = lambda e: pl.BlockSpec((None, t, BLK), lambda p, i, j: (2 * p + e, i, 0))
    ks = lambda e: pl.BlockSpec((None, t, BLK), lambda p, i, j: (2 * p + e, kj(i, j), 0))
    kts = lambda e: pl.BlockSpec((None, BLK, t), lambda p, i, j: (2 * p + e, 0, kj(i, j)))
    rep = pl.BlockSpec((None, t, BLK), lambda p, i, j: (p, i, 0))
    bufs = list(gather[0]) if gather else []
    return pl.pallas_call(
        body, grid=(NH // 2, nt, nt), in_specs=[qs(0), qs(1), kts(0), kts(1), ks(0), ks(1)] + [ANY] * ng,
        out_specs=[pl.BlockSpec((t, BLK), lambda p, i, j: (i, p)), rep, rep] + [ANY] * ng,
        out_shape=[jax.ShapeDtypeStruct((T, 512), BF16)] + [jax.ShapeDtypeStruct((NH // 2, T, BLK), F32)] * 2
        + [jax.ShapeDtypeStruct(b.shape, b.dtype) for b in bufs],
        input_output_aliases={6 + g: 3 + g for g in range(ng)},
        scratch_shapes=[pltpu.VMEM((2, t, BLK), F32), pltpu.VMEM((2, t, BLK), F32)]
        + ([pltpu.SemaphoreType.DMA((6 * ng,)), pltpu.SemaphoreType.DMA((6 * ng,))] if gather else []),
        compiler_params=(pltpu.CompilerParams(dimension_semantics=("arbitrary",) * 3, vmem_limit_bytes=VMEM_LIMIT,
                                              has_side_effects=True) if gather
                         else _cp("parallel", "parallel", "arbitrary")), name=name)(qaug, qaug, kaug_t, kaug_t, vo, vo, *bufs)


def _fox_delta(do, o, name):
    T = do.shape[0]
    tr = _rt(T)

    def body(do_ref, o_ref, d0_ref, d1_ref):
        lo = lax.broadcasted_iota(jnp.int32, (1, BLK), 1) < 64
        prod = do_ref[...].astype(F32) * o_ref[...].astype(F32)
        d0_ref[...] = jnp.zeros((tr, BLK), F32) + jnp.sum(jnp.where(lo, prod, 0.0), axis=-1, keepdims=True)
        d1_ref[...] = jnp.zeros((tr, BLK), F32) + jnp.sum(jnp.where(lo, 0.0, prod), axis=-1, keepdims=True)

    blk = pl.BlockSpec((tr, BLK), lambda i, p: (i, p))
    rep = pl.BlockSpec((None, tr, BLK), lambda i, p: (p, i, 0))
    return pl.pallas_call(
        body, grid=(T // tr, NH // 2), in_specs=[blk, blk], out_specs=[rep, rep],
        out_shape=[jax.ShapeDtypeStruct((NH // 2, T, BLK), F32)] * 2,
        compiler_params=_cp("parallel", "parallel"), name=name)(do, o)


def _fox_bwd(qaug, kaug, kaug_t, vm, do, lses, deltas, name):
    T = qaug.shape[1]
    t = _rt(T)
    nt = T // t

    def body(q0, q1, k0, k1, kt0, kt1, v0, v1, do_ref, lse0, lse1, dl0, dl1,
             dq_ref, dqx_ref, dk_ref, dv_ref, dkx_ref, dq_acc, dk_acc, dv_acc):
        j, i = pl.program_id(1), pl.program_id(2)
        lane = lax.broadcasted_iota(jnp.int32, (1, BLK), 1)
        lo = lane < 64

        @pl.when((j == 0) & (i == 0))
        def _():
            dq_acc[...] = jnp.zeros_like(dq_acc)

        @pl.when(i == 0)
        def _():
            dk_acc[...] = jnp.zeros_like(dk_acc)
            dv_acc[...] = jnp.zeros_like(dv_acc)

        def step(masked):
            dob = do_ref[...]
            rows = pl.ds(pl.multiple_of(i * t, t), t)
            wide = lambda ref: jnp.concatenate([ref[...]] * (t // BLK), axis=1)
            for e, (q_ref, k_ref, kt_ref, v_ref, lse_ref, dl_ref) in enumerate(
                    ((q0, k0, kt0, v0, lse0, dl0), (q1, k1, kt1, v1, lse1, dl1))):
                s = _dot(q_ref[...], kt_ref[...], 1, 0)
                if masked:
                    s = jnp.where(_fox_mask(i, j, t), s, NEG)
                pe = jnp.exp(s - wide(lse_ref))
                dp = _dot(dob, v_ref[...], 1, 1)
                ds = (pe * (dp - wide(dl_ref))).astype(BF16)
                dq_acc[e, rows, :] += _dot(ds, k_ref[...], 1, 0)
                dk_acc[e] += _dot(ds, q_ref[...], 0, 0)
                dv_acc[e] += _dot(pe.astype(BF16), dob, 0, 0)

        pl.when((i > j) & (j > 0))(lambda: step(False))
        pl.when((i == j) | ((j == 0) & (i > 0)))(lambda: step(True))

        @pl.when(i == nt - 1)
        def _():
            dk_ref[...] = jnp.where(lo, dk_acc[0], dk_acc[1]).astype(BF16)
            dv_ref[...] = jnp.where(lo, dv_acc[0], dv_acc[1]).astype(BF16)
            dkx_ref[...] = jnp.where(lo, dk_acc[1], dk_acc[0])

        @pl.when((i == nt - 1) & (j == nt - 1))
        def _():
            dq_ref[...] = (jnp.where(lo, dq_acc[0], dq_acc[1]) * SCALE).astype(BF16)
            dqx_ref[...] = jnp.where(lo, dq_acc[1], dq_acc[0])

    qi = lambda j, i: jnp.maximum(i, j)
    qs = lambda e: pl.BlockSpec((None, t, BLK), lambda p, j, i: (2 * p + e, qi(j, i), 0))
    ks = lambda e: pl.BlockSpec((None, t, BLK), lambda p, j, i: (2 * p + e, j, 0))
    kts = lambda e: pl.BlockSpec((None, BLK, t), lambda p, j, i: (2 * p + e, 0, j))
    qside = pl.BlockSpec((t, BLK), lambda p, j, i: (qi(j, i), p))
    kside = pl.BlockSpec((t, BLK), lambda p, j, i: (j, p))
    rep = pl.BlockSpec((None, t, BLK), lambda p, j, i: (p, qi(j, i), 0))
    whole = pl.BlockSpec((T, BLK), lambda p, j, i: (0, p))
    return pl.pallas_call(
        body, grid=(NH // 2, nt, nt),
        in_specs=[qs(0), qs(1), ks(0), ks(1), kts(0), kts(1), ks(0), ks(1), qside, rep, rep, rep, rep],
        out_specs=[whole, whole, kside, kside, kside],
        out_shape=[jax.ShapeDtypeStruct((T, 512), BF16), jax.ShapeDtypeStruct((T, 512), F32),
                   jax.ShapeDtypeStruct((T, 512), BF16), jax.ShapeDtypeStruct((T, 512), BF16),
                   jax.ShapeDtypeStruct((T, 512), F32)],
        scratch_shapes=[pltpu.VMEM((2, T, BLK), F32), pltpu.VMEM((2, t, BLK), F32), pltpu.VMEM((2, t, BLK), F32)],
        compiler_params=_cp("parallel", "arbitrary", "arbitrary"), name=name)(
            qaug, qaug, kaug, kaug, kaug_t, kaug_t, vm, vm, do, *lses, *deltas)


def _fox_mask(i, j, t):
    row = i * t + lax.broadcasted_iota(jnp.int32, (t, t), 0)
    col = j * t + lax.broadcasted_iota(jnp.int32, (t, t), 1)
    return (col <= row) & (col >= NPAD)


def _lane_pick(x, lane, idx):
    return jnp.sum(jnp.where(lane == idx, x, 0.0), axis=-1, keepdims=True)


def _lru_gates(xc, wr_ref, wi_ref, vec_ref):
    xb = xc.astype(BF16)
    pre_r = jnp.concatenate([_dot(xb[:, p * BLK:(p + 1) * BLK], wr_ref[p], 1, 0) for p in range(LW // BLK)], axis=1)
    pre_i = jnp.concatenate([_dot(xb[:, p * BLK:(p + 1) * BLK], wi_ref[p], 1, 0) for p in range(LW // BLK)], axis=1)
    r = _sigmoid(pre_r + vec_ref[0:1, :])
    gi = _sigmoid(pre_i + vec_ref[1:2, :])
    log_a = LRU_C * r * _log_sigmoid(vec_ref[2:3, :])
    a = jnp.exp(log_a)
    mult = jnp.sqrt(_neg_expm1(2.0 * log_a))
    return r, gi, a, mult


def _conv(xbuf_ref, x, cw_ref, vec_ref, tr):
    return (cw_ref[3:4, :] * x + cw_ref[2:3, :] * xbuf_ref[7:7 + tr, :] + cw_ref[1:2, :] * xbuf_ref[6:6 + tr, :]
            + cw_ref[0:1, :] * xbuf_ref[5:5 + tr, :] + vec_ref[3:4, :])


def _lru_fwd(proj, cw, wr, wi, vec, name):
    T = proj.shape[0]
    tr = _rt(T)

    def body(x_ref, y_ref, cw_ref, wr_ref, wi_ref, vec_ref, oc_ref, hs_ref, xbuf, abuf, bbuf, hcar):
        i = pl.program_id(0)

        @pl.when(i == 0)
        def _():
            xbuf[0:8, :] = jnp.zeros((8, LW), F32)
            hcar[...] = jnp.zeros_like(hcar)

        x = x_ref[...]
        xbuf[8:8 + tr, :] = x
        xc = _conv(xbuf, x, cw_ref, vec_ref, tr)
        xbuf[0:8, :] = x[tr - 8:tr, :]
        _, gi, a, mult = _lru_gates(xc, wr_ref, wi_ref, vec_ref)
        rows = i * tr + lax.broadcasted_iota(jnp.int32, (tr, 1), 0)
        abuf[...] = a
        bbuf[...] = jnp.where(rows >= NPAD, mult * (gi * xc), 0.0)
        sub = lax.broadcasted_iota(jnp.int32, (8, 1), 0)

        def step(k, h):
            sl = pl.ds(pl.multiple_of(k * 8, 8), 8)
            a8, b8 = abuf[sl, :], bbuf[sl, :]
            for s in (1, 2, 4):
                ok = sub >= s
                b8 = jnp.where(ok, a8 * pltpu.roll(b8, s, 0) + b8, b8)
                a8 = jnp.where(ok, a8 * pltpu.roll(a8, s, 0), a8)
            h8 = a8 * h + b8
            bbuf[sl, :] = h8
            return h8[7:8, :]

        hcar[...] = lax.fori_loop(0, tr // 8, step, hcar[...])
        hs = bbuf[...]
        hs_ref[...] = hs
        oc_ref[...] = (hs * _gelu(y_ref[...])).astype(BF16)

    row = pl.BlockSpec((tr, LW), lambda i: (i, 0))
    full = lambda shape: pl.BlockSpec(shape, lambda i: (0,) * len(shape))
    return pl.pallas_call(
        body, grid=(T // tr,),
        in_specs=[pl.BlockSpec((tr, LW), lambda i: (i, XC // LW)), pl.BlockSpec((tr, LW), lambda i: (i, YC // LW)),
                  full((4, LW)), full((4, BLK, BLK)), full((4, BLK, BLK)), full((8, LW))],
        out_specs=[row, row], out_shape=[jax.ShapeDtypeStruct((T, LW), BF16), jax.ShapeDtypeStruct((T, LW), F32)],
        scratch_shapes=[pltpu.VMEM((tr + 8, LW), F32), pltpu.VMEM((tr, LW), F32), pltpu.VMEM((tr, LW), F32),
                        pltpu.VMEM((1, LW), F32)],
        compiler_params=_cp("arbitrary"), name=name)(proj, proj, cw, wr, wi, vec)


def _lru_bwd(proj, hs, doc, cw, wr, wi, vec, name):
    T = proj.shape[0]
    tr = _rt(T)
    nt = T // tr
    r8 = tr // 8

    def body(x_ref, xp_ref, y_ref, hs_ref, hp_ref, do_ref, cw_ref, wr_ref, wi_ref, vec_ref,
             dx_ref, dy_ref, dwr_ref, dwi_ref, dvec_ref, xbuf, abuf, gbuf, hbuf, dbuf, gcar, acar):
        k = pl.program_id(0)
        i = nt - 1 - k

        @pl.when(k == 0)
        def _():
            dwr_ref[...] = jnp.zeros_like(dwr_ref)
            dwi_ref[...] = jnp.zeros_like(dwi_ref)
            dvec_ref[...] = jnp.zeros_like(dvec_ref)
            gcar[...] = jnp.zeros_like(gcar)
            acar[...] = jnp.zeros_like(acar)
            dbuf[tr:tr + 8, :] = jnp.zeros((8, LW), F32)

        first = i == 0
        x = x_ref[...]
        xbuf[0:8, :] = jnp.where(first, 0.0, xp_ref[...])
        xbuf[8:8 + tr, :] = x
        xc = _conv(xbuf, x, cw_ref, vec_ref, tr)
        r, gi, a, mult = _lru_gates(xc, wr_ref, wi_ref, vec_ref)
        y = y_ref[...]
        hs = hs_ref[...]
        do_ = do_ref[...].astype(F32)
        rows = i * tr + lax.broadcasted_iota(jnp.int32, (tr, 1), 0)
        abuf[0:tr, :] = a
        abuf[tr:tr + 8, :] = jnp.zeros((8, LW), F32) + acar[...]
        an = abuf[1:1 + tr, :]
        acar[...] = a[0:1, :]
        abuf[0:tr, :] = an
        gbuf[...] = do_ * _gelu(y)
        sub = lax.broadcasted_iota(jnp.int32, (8, 1), 0)

        def step(kk, g):
            sl = pl.ds(pl.multiple_of((r8 - 1 - kk) * 8, 8), 8)
            a8, b8 = abuf[sl, :], gbuf[sl, :]
            for s in (1, 2, 4):
                ok = sub < 8 - s
                b8 = jnp.where(ok, a8 * pltpu.roll(b8, 8 - s, 0) + b8, b8)
                a8 = jnp.where(ok, a8 * pltpu.roll(a8, 8 - s, 0), a8)
            g8 = a8 * g + b8
            gbuf[sl, :] = g8
            return g8[0:1, :]

        gcar[...] = lax.fori_loop(0, r8, step, gcar[...])
        g = gbuf[...]
        hbuf[0:8, :] = jnp.where(first, 0.0, hp_ref[...])
        hbuf[8:8 + tr, :] = hs
        hprev = hbuf[7:7 + tr, :]
        dinp = jnp.where(rows >= NPAD, g, 0.0)
        da = g * hprev
        dmult = dinp * gi * xc
        dgi = dinp * mult * xc
        dxc = dinp * mult * gi
        dlog_a = da * a - dmult * a * a / mult
        ls = _log_sigmoid(vec_ref[2:3, :])
        dpre_r = dlog_a * (LRU_C * ls) * r * (1.0 - r)
        dpre_i = dgi * gi * (1.0 - gi)
        xb = xc.astype(BF16)
        rb, ib = dpre_r.astype(BF16), dpre_i.astype(BF16)
        back = []
        for p in range(LW // BLK):
            c = slice(p * BLK, (p + 1) * BLK)
            back.append(_dot(rb[:, c], wr_ref[p], 1, 1) + _dot(ib[:, c], wi_ref[p], 1, 1))
            dwr_ref[p] += _dot(xb[:, c], rb[:, c], 0, 0)
            dwi_ref[p] += _dot(xb[:, c], ib[:, c], 0, 0)
        dxc = dxc + jnp.concatenate(back, axis=1)
        col = lambda v: jnp.sum(v, axis=0, keepdims=True)
        dvec_ref[0:1, :] += col(dpre_r)
        dvec_ref[1:2, :] += col(dpre_i)
        dvec_ref[2:3, :] += col(dlog_a * (LRU_C * r)) * _sigmoid(-vec_ref[2:3, :])
        dvec_ref[3:4, :] += col(dxc)
        dvec_ref[4:5, :] += col(dxc * xbuf[5:5 + tr, :])
        dvec_ref[5:6, :] += col(dxc * xbuf[6:6 + tr, :])
        dvec_ref[6:7, :] += col(dxc * xbuf[7:7 + tr, :])
        dvec_ref[7:8, :] += col(dxc * x)
        dbuf[0:tr, :] = dxc
        dxr = (cw_ref[3:4, :] * dxc + cw_ref[2:3, :] * dbuf[1:1 + tr, :] + cw_ref[1:2, :] * dbuf[2:2 + tr, :]
               + cw_ref[0:1, :] * dbuf[3:3 + tr, :])
        dbuf[tr:tr + 8, :] = dxc[0:8, :]
        dx_ref[...] = jnp.where(rows >= NPAD, dxr, 0.0).astype(BF16)
        dy_ref[...] = (do_ * hs * _gelu_grad(y)).astype(BF16)

    rev = lambda k: nt - 1 - k
    row = lambda col0: pl.BlockSpec((tr, LW), lambda k: (rev(k), col0))
    prev8 = lambda col0: pl.BlockSpec((8, LW), lambda k: (jnp.maximum(rev(k) * r8 - 1, 0), col0))
    full = lambda shape: pl.BlockSpec(shape, lambda k: (0,) * len(shape))
    return pl.pallas_call(
        body, grid=(nt,),
        in_specs=[row(XC // LW), prev8(XC // LW), row(YC // LW), row(0), prev8(0), row(0),
                  full((4, LW)), full((4, BLK, BLK)), full((4, BLK, BLK)), full((8, LW))],
        out_specs=[row(0), row(0), full((4, BLK, BLK)), full((4, BLK, BLK)), full((8, LW))],
        out_shape=[jax.ShapeDtypeStruct((T, LW), BF16), jax.ShapeDtypeStruct((T, LW), BF16),
                   jax.ShapeDtypeStruct((4, BLK, BLK), F32), jax.ShapeDtypeStruct((4, BLK, BLK), F32),
                   jax.ShapeDtypeStruct((8, LW), F32)],
        scratch_shapes=[pltpu.VMEM((tr + 8, LW), F32), pltpu.VMEM((tr + 8, LW), F32), pltpu.VMEM((tr, LW), F32),
                        pltpu.VMEM((tr + 8, LW), F32), pltpu.VMEM((tr + 8, LW), F32),
                        pltpu.VMEM((1, LW), F32), pltpu.VMEM((1, LW), F32)],
        compiler_params=_cp("arbitrary"), name=name)(proj, proj, proj, hs, hs, doc, cw, wr, wi, vec)


def _branch_merge_fwd(oa, of, oc, wb, proj, name):
    T = proj.shape[0]
    tm, tn = _rt(T), 512

    def body(a0, a1, a2, w_ref, g0, g1, g2, r0, r1, r2, m_ref):
        acc = None
        for g, (a_ref, g_ref, r_ref) in enumerate(((a0, g0, r0), (a1, g1, r1), (a2, g2, r2))):
            b = _dot(a_ref[...], w_ref[g], 1, 0)
            r_ref[...] = b
            term = _sigmoid(g_ref[...]) * b
            acc = term if acc is None else acc + term
        m_ref[...] = acc.astype(BF16)

    act = pl.BlockSpec((tm, LW), lambda j, i: (i, 0))
    gate = lambda g: pl.BlockSpec((tm, tn), lambda j, i: (i, (GT + g * D) // tn + j))
    blk = pl.BlockSpec((tm, tn), lambda j, i: (i, j))
    return pl.pallas_call(
        body, grid=(D // tn, T // tm),
        in_specs=[act, act, act, pl.BlockSpec((3, LW, tn), lambda j, i: (0, 0, j)), gate(0), gate(1), gate(2)],
        out_specs=[blk] * 4,
        out_shape=[jax.ShapeDtypeStruct((T, D), F32)] * 3 + [jax.ShapeDtypeStruct((T, D), BF16)],
        compiler_params=_cp("parallel", "parallel"), name=name)(oa, of, oc, wb, proj, proj, proj)


def _out_dx_merge_bwd(dhb, w_out, proj, b0, b1, b2, name):
    T = proj.shape[0]
    tm, tn = _rt(T), 512

    def body(dh_ref, w_ref, g0, g1, g2, r0, r1, r2, d0, d1, d2, e0, e1, e2):
        dmv = _dot(dh_ref[...], w_ref[...], 1, 1)
        for g_ref, r_ref, d_ref, e_ref in ((g0, r0, d0, e0), (g1, r1, d1, e1), (g2, r2, d2, e2)):
            sg = _sigmoid(g_ref[...])
            d_ref[...] = (dmv * sg).astype(BF16)
            e_ref[...] = (dmv * r_ref[...] * sg * (1.0 - sg)).astype(BF16)

    gate = lambda g: pl.BlockSpec((tm, tn), lambda j, i: (i, (GT + g * D) // tn + j))
    blk = pl.BlockSpec((tm, tn), lambda j, i: (i, j))
    return pl.pallas_call(
        body, grid=(D // tn, T // tm),
        in_specs=[pl.BlockSpec((tm, D), lambda j, i: (i, 0)), pl.BlockSpec((tn, D), lambda j, i: (j, 0)),
                  gate(0), gate(1), gate(2), blk, blk, blk],
        out_specs=[blk] * 6, out_shape=[jax.ShapeDtypeStruct((T, D), BF16)] * 6,
        compiler_params=_cp("parallel", "parallel"), name=name)(dhb, w_out, proj, proj, proj, b0, b1, b2)


def _ffn_in_swiglu_fwd(u, w, name):
    T = u.shape[0]
    tm, tn = _rt(T), _pick(DFF, (1408, 256))
    nj = DFF // tn

    def body(u_ref, wg_ref, wu_ref, g_ref, up_ref, a_ref):
        ub = u_ref[...]
        g = _dot(ub, wg_ref[...], 1, 0)
        up = _dot(ub, wu_ref[...], 1, 0)
        g_ref[...] = g
        up_ref[...] = up
        a_ref[...] = (g * _sigmoid(g) * up).astype(BF16)

    blk = pl.BlockSpec((tm, tn), lambda j, i: (i, j))
    return pl.pallas_call(
        body, grid=(nj, T // tm),
        in_specs=[pl.BlockSpec((tm, D), lambda j, i: (i, 0)), pl.BlockSpec((D, tn), lambda j, i: (0, j)),
                  pl.BlockSpec((D, tn), lambda j, i: (0, j + nj))],
        out_specs=[blk] * 3,
        out_shape=[jax.ShapeDtypeStruct((T, DFF), F32)] * 2 + [jax.ShapeDtypeStruct((T, DFF), BF16)],
        compiler_params=_cp("parallel", "parallel"), name=name)(u, w, w)


def _ffn_out_dx_swiglu_bwd(dhb, w, gate, up, name):
    T = dhb.shape[0]
    tm, tn = _rt(T), _pick(DFF, (1408, 256))

    def body(dh_ref, w_ref, g_ref, up_ref, dg_ref, du_ref):
        d = _dot(dh_ref[...], w_ref[...], 1, 1)
        g = g_ref[...]
        sg = _sigmoid(g)
        dg_ref[...] = (d * up_ref[...] * (sg + g * sg * (1.0 - sg))).astype(BF16)
        du_ref[...] = (d * g * sg).astype(BF16)

    blk = pl.BlockSpec((tm, tn), lambda j, i: (i, j))
    return pl.pallas_call(
        body, grid=(DFF // tn, T // tm),
        in_specs=[pl.BlockSpec((tm, D), lambda j, i: (i, 0)), pl.BlockSpec((tn, D), lambda j, i: (j, 0)), blk, blk],
        out_specs=[blk] * 2, out_shape=[jax.ShapeDtypeStruct((T, DFF), BF16)] * 2,
        compiler_params=_cp("parallel", "parallel"), name=name)(dhb, w, gate, up)


def _adamw(w, g, m, v, name):
    R, C = w.shape
    tr = _pick(R, tuple(t for t in (512, 256, 128, 64, 32, 16, 8) if t * C * 4 <= (3 << 19)))
    c1 = 1.0 - ADAM_B1 ** ADAM_STEP
    c2 = 1.0 - ADAM_B2 ** ADAM_STEP

    def body(w_ref, g_ref, m_ref, v_ref, d_ref, mo_ref, vo_ref):
        gv = g_ref[...]
        mn = ADAM_B1 * m_ref[...] + (1.0 - ADAM_B1) * gv
        vn = ADAM_B2 * v_ref[...] + (1.0 - ADAM_B2) * (gv * gv)
        d_ref[...] = -ADAM_LR * ((mn / c1) / (jnp.sqrt(vn / c2) + ADAM_EPS) + ADAM_WD * w_ref[...])
        mo_ref[...] = mn
        vo_ref[...] = vn

    blk = pl.BlockSpec((tr, C), lambda i: (i, 0))
    return pl.pallas_call(
        body, grid=(R // tr,), in_specs=[blk] * 4, out_specs=[blk] * 3,
        out_shape=[jax.ShapeDtypeStruct((R, C), F32)] * 3, compiler_params=_cp("parallel"), name=name)(w, g, m, v)


def _sum_lead(x, name):
    n, R, C = x.shape
    tr = _pick(R, (512, 256, 128, 64, 32, 16, 8))

    def body(x_ref, o_ref):
        acc = x_ref[0]
        for d in range(1, n):
            acc = acc + x_ref[d]
        o_ref[...] = acc

    return pl.pallas_call(
        body, grid=(R // tr,), in_specs=[pl.BlockSpec((n, tr, C), lambda i: (0, i, 0))],
        out_specs=pl.BlockSpec((tr, C), lambda i: (i, 0)), out_shape=jax.ShapeDtypeStruct((R, C), F32),
        compiler_params=_cp("parallel"), name=name)(x)


def _here():
    return lax.axis_index("x"), lax.axis_index("y"), lax.axis_index("c")


def _rcopy(src, dst, send_sems, recv_sems, k, to):
    return pltpu.make_async_remote_copy(src_ref=src, dst_ref=dst, send_sem=send_sems.at[k], recv_sem=recv_sems.at[k],
                                        device_id=to, device_id_type=MESH)


def _window(ref, lead, axis, start, width):
    idx = [lead] + [slice(None)] * (len(ref.shape) - 1)
    if axis is not None:
        idx[axis] = pl.ds(start, width)
    return ref.at[tuple(idx)]


def _hbm_calls(body, args, out_shapes, n_sems, aliases, name):
    return pl.pallas_call(
        body, in_specs=[ANY] * len(args), out_specs=[ANY] * len(out_shapes), out_shape=out_shapes,
        input_output_aliases=aliases,
        scratch_shapes=[pltpu.SemaphoreType.DMA((n_sems,)), pltpu.SemaphoreType.DMA((n_sems,))],
        compiler_params=pltpu.CompilerParams(has_side_effects=True), name=name)(*args)


def _gather_plan(outs, axes, send_sems, recv_sems):
    x, y, c = _here()
    sib = (x, y, 1 - c)
    chips = [(1 - x, y), (x, 1 - y), (1 - x, 1 - y)]
    todo = [(t, k, chip) for t in range(len(outs)) for k, chip in enumerate(chips)]

    def win(t, chip, hc):
        o, ax = outs[t], axes[t]
        w = o.shape[ax] // N_SHARD
        first = (2 * chip[0] + chip[1]) * w
        if ax == 0:
            return o.at[pl.ds(pl.multiple_of(first + hc * (w // 2), 16), w // 2), :]
        rows = o.shape[0] // 2
        return o.at[pl.ds(pl.multiple_of(hc * rows, 16), rows), pl.ds(pl.multiple_of(first, BLK), w)]

    def copy(t, k, chip, hc, to):
        return _rcopy(win(t, chip, hc), win(t, chip, hc), send_sems, recv_sems, 6 * t + k, to)

    def start():
        for t, k, chip in todo:
            copy(t, k, (x, y), c, (*chip, c)).start()

    def finish():
        for t, k, chip in todo:
            copy(t, k, chip, c, (*chip, c)).wait_recv()
            copy(t, 3 + k, chip, c, sib).start()
        for t, k, chip in todo:
            copy(t, 3 + k, chip, 1 - c, sib).wait_recv()
        for t, k, chip in todo:
            copy(t, k, (x, y), c, (*chip, c)).wait_send()
            copy(t, 3 + k, chip, c, sib).wait_send()

    return start, finish


def _all_gather_weights(fulls, axes, name):
    nt = len(fulls)

    def body(*refs):
        start, finish = _gather_plan(refs[nt:2 * nt], axes, *refs[2 * nt:])
        start()
        finish()

    return _hbm_calls(body, fulls, [jax.ShapeDtypeStruct(f.shape, f.dtype) for f in fulls], 6 * nt,
                      {t: t for t in range(nt)}, name)


def _swap_halves(gs, name):
    nt = len(gs)

    def body(*refs):
        ins, outs, (send_sems, recv_sems) = refs[:nt], refs[nt:2 * nt], refs[2 * nt:]
        x, y, c = _here()
        cps = [_rcopy(g.at[pl.ds(2 * (1 - c), 2)], o, send_sems, recv_sems, t, (x, y, 1 - c))
               for t, (g, o) in enumerate(zip(ins, outs))]
        for cp in cps:
            cp.start()
        for cp in cps:
            cp.wait()

    return _hbm_calls(body, gs, [jax.ShapeDtypeStruct((2,) + g.shape[1:], g.dtype) for g in gs], nt, {}, name)


def _scatter_to_chips(ss, axes, name):
    nt = len(ss)

    def shard_shape(s, ax):
        shp = list(s.shape)
        shp[ax] //= N_SHARD
        return tuple(shp)

    def body(*refs):
        ins, outs, (send_sems, recv_sems) = refs[:nt], refs[nt:2 * nt], refs[2 * nt:]
        x, y, c = _here()
        chips = [(1 - x, y), (x, 1 - y), (1 - x, 1 - y)]
        cps = []
        for t, (s, o, ax) in enumerate(zip(ins, outs, axes)):
            w = s.shape[ax] // N_SHARD
            for k, chip in enumerate(chips):
                src = _window(s, slice(None), ax, pl.multiple_of((2 * chip[0] + chip[1]) * w, 8), w)
                cps.append(_rcopy(src, o.at[k], send_sems, recv_sems, 3 * t + k, (*chip, c)))
        for cp in cps:
            cp.start()
        for cp in cps:
            cp.wait()

    return _hbm_calls(body, ss, [jax.ShapeDtypeStruct((3,) + shard_shape(s, ax), s.dtype) for s, ax in zip(ss, axes)],
                      3 * nt, {}, name)


def _join_halves(fins, name):
    nt = len(fins)

    def body(*refs):
        outs, (send_sems, recv_sems) = refs[nt:2 * nt], refs[2 * nt:]
        x, y, c = _here()
        cps = [_rcopy(o.at[pl.ds(2 * c, 2)], o.at[pl.ds(2 * c, 2)], send_sems, recv_sems, t, (x, y, 1 - c))
               for t, o in enumerate(outs)]
        for cp in cps:
            cp.start()
        for t, o in enumerate(outs):
            _rcopy(o.at[pl.ds(2 * (1 - c), 2)], o.at[pl.ds(2 * (1 - c), 2)], send_sems, recv_sems, t, (x, y, 1 - c)).wait_recv()
        for cp in cps:
            cp.wait_send()

    return _hbm_calls(body, fins, [jax.ShapeDtypeStruct(f.shape, f.dtype) for f in fins], nt, {t: t for t in range(nt)}, name)


def _all_gather_small(buf, name):
    def body(_, out_ref, send_sems, recv_sems):
        x, y, c = _here()
        me = 4 * x + 2 * y + c
        cps = []
        for k in range(1, 8):
            to = (x ^ ((k >> 2) & 1), y ^ ((k >> 1) & 1), c ^ (k & 1))
            peer = 4 * to[0] + 2 * to[1] + to[2]
            cps.append((_rcopy(out_ref.at[me], out_ref.at[me], send_sems, recv_sems, k - 1, to),
                        _rcopy(out_ref.at[peer], out_ref.at[peer], send_sems, recv_sems, k - 1, to)))
        for snd, _ in cps:
            snd.start()
        for _, rcv in cps:
            rcv.wait_recv()
        for snd, _ in cps:
            snd.wait_send()

    return _hbm_calls(body, [buf], [jax.ShapeDtypeStruct(buf.shape, buf.dtype)], 7, {0: 0}, name)[0]


def _place(block, n, index):
    buf = jnp.zeros((n,) + block.shape[1:], block.dtype)
    return lax.dynamic_update_slice_in_dim(buf, block, index, axis=0)


def _add_half(g, other, cidx, name):
    _, R, C = g.shape
    tr = _pick(R, tuple(t for t in (512, 256, 128, 64, 32, 16, 8) if t * C * 4 <= (1 << 21)))

    def body(c_ref, g_ref, o_ref, s_ref, sb_ref):
        s = g_ref[...] + o_ref[...]
        s_ref[...] = s
        sb_ref[...] = s.astype(BF16)

    blk = pl.BlockSpec((None, tr, C), lambda l, i, c: (l, i, 0))
    return pl.pallas_call(
        body,
        grid_spec=pltpu.PrefetchScalarGridSpec(
            num_scalar_prefetch=1, grid=(2, R // tr),
            in_specs=[pl.BlockSpec((None, tr, C), lambda l, i, c: (2 * c[0] + l, i, 0)), blk], out_specs=[blk, blk]),
        out_shape=[jax.ShapeDtypeStruct((2, R, C), F32), jax.ShapeDtypeStruct((2, R, C), BF16)],
        compiler_params=_cp("parallel", "parallel"), name=name)(cidx, g, other)


def _add_chips(s, recv, axis, chip_idx, name):
    _, _, r, cw = recv.shape
    tr = _pick(r, tuple(t for t in (704, 512, 256, 128, 64, 32, 16, 8) if t * cw * 4 <= (1 << 21)))
    nr = r // tr

    def body(c_ref, s_ref, r_ref, out_ref):
        out_ref[...] = ((s_ref[...] + r_ref[0].astype(F32)) + r_ref[1].astype(F32)) + r_ref[2].astype(F32)

    if axis == 2:
        s_map = lambda l, i, c: (l, i, c[0])
    else:
        s_map = lambda l, i, c: (l, c[0] * nr + i, 0)
    return pl.pallas_call(
        body,
        grid_spec=pltpu.PrefetchScalarGridSpec(
            num_scalar_prefetch=1, grid=(2, nr),
            in_specs=[pl.BlockSpec((None, tr, cw), s_map), pl.BlockSpec((3, None, tr, cw), lambda l, i, c: (0, l, i, 0))],
            out_specs=pl.BlockSpec((None, tr, cw), lambda l, i, c: (l, i, 0))),
        out_shape=jax.ShapeDtypeStruct((2, r, cw), F32), compiler_params=_cp("parallel", "parallel"), name=name)(chip_idx, s, recv)


IN_SHARD = IN_COLS // N_SHARD
IN_SLOT = INP // N_SHARD
IN_PIECES = ((0, 512, QA), (512, 640, KA), (640, 768, VA), (768, 1280, QF), (1280, 1792, KF), (1792, 2304, VF),
             (2304, 2312, FL), (2312, 2824, XC), (2824, 3336, YC), (3336, 6408, GT))


def _gathered_to_kernel_cols(w):
    parts, pos = [], 0
    for a, b, k in sorted(IN_PIECES, key=lambda p: p[2]):
        assert k == pos
        while a < b:
            j = a // IN_SHARD
            e = min(b, (j + 1) * IN_SHARD)
            g = j * IN_SLOT + a - j * IN_SHARD
            parts.append(w[..., g:g + e - a])
            pos += e - a
            a = e
    parts.append(jnp.zeros(w.shape[:-1] + (INP - pos,), w.dtype))
    return jnp.concatenate(parts, axis=-1)


def _kernel_to_gathered_cols(w):
    parts = []
    for j in range(N_SHARD):
        lo, hi = j * IN_SHARD, (j + 1) * IN_SHARD
        for a, b, k in IN_PIECES:
            s, e = max(a, lo), min(b, hi)
            if s < e:
                parts.append(w[..., k + s - a:k + e - a])
        parts.append(jnp.zeros(w.shape[:-1] + (IN_SLOT - IN_SHARD,), w.dtype))
    return jnp.concatenate(parts, axis=-1)


def _pair_blocks(w):
    z = jnp.zeros((4, 64, 64), w.dtype)
    w = w.reshape(4, 2, 64, 64)
    top = jnp.concatenate([w[:, 0], z], axis=2)
    bot = jnp.concatenate([z, w[:, 1]], axis=2)
    return jnp.concatenate([top, bot], axis=1)


def _unpair_blocks(w):
    return jnp.stack([w[:, :64, :64], w[:, 64:, 64:]], axis=1).reshape(8, 64, 64)


BIG = ("w_in", "w_branch", "w_out", "w_ffn_in", "w_ffn_out")
TINY = ("conv_w", "meta_tokens")
SMALL = ("rel_bias_table", "norm_mix", "swa_sinks", "fox_forget_bias", "conv_b", "lru_w_r", "lru_b_r", "lru_w_i",
         "lru_b_i", "lru_lambda", "norm_ffn", "norm_final")
SHARD_AXIS = {"conv_w": 2, "meta_tokens": 1}
BIG_AXIS = {"w_in": 2, "w_branch": 2, "w_out": 1, "w_ffn_in": 2, "w_ffn_out": 1}


def _pack(d, names):
    flat = jnp.concatenate([d[n].reshape(-1) for n in names])
    pad = (-flat.shape[0]) % (256 * 128)
    return jnp.concatenate([flat, jnp.zeros((pad,), F32)]).reshape(-1, 128)


def _unpack(buf, names, shapes):
    flat, out, off = buf.reshape(-1), {}, 0
    for n in names:
        sz = int(np.prod(shapes[n]))
        out[n] = flat[off:off + sz].reshape(shapes[n])
        off += sz
    return out


def _layer_layout(n, a):
    if n == "w_in":
        return _gathered_to_kernel_cols(a)
    return a.reshape(3, LW, D) if n == "w_branch" else a


def _local_step(x, tgt, W, placed=None):
    S = x.shape[0]
    T = S + BLK
    tm = _pick(T, (1408, 384, 128))
    bucket = jnp.asarray(_bucket_table())
    bias = _bias_build(W["rel_bias_table"], bucket, "bias_build")
    h = jnp.concatenate([jnp.zeros((NPAD, D), F32), W["meta_tokens"], x], axis=0)
    if placed is None:
        WL = {n: [W[n][l] for l in range(DEPTH)] for n in BIG}
    else:
        WL = {n: [W[n]] + [None] * (DEPTH - 1) for n in BIG}

    saved = []
    for l in range(DEPTH):
        sv = {"h0": h}
        u = _rms_fwd(h, W["norm_mix"][l], f"rms_mix_fwd")
        proj = _mm(u, WL["w_in"][l], tm=tm, tn=512, tk=D, name="mm_in_fwd")
        oa = _swa_fwd(proj, bias, W["swa_sinks"][l], "swa_fwd")
        fb = W["fox_forget_bias"][l].reshape(NH, 1)
        qaug, kaug, kaug_t, vm, vo = _fox_prep(proj, _cum_fwd(proj, fb, "cum_fwd"), "fox_prep")
        if placed is not None and l + 1 < DEPTH:
            of, lse0, lse1, *got = _fox_fwd(qaug, kaug_t, vo, "fox_fwd_gather", gather=(placed[l + 1], GATHER_AXES))
            for n, a in zip(BIG, got):
                WL[n][l + 1] = _layer_layout(n, a)
            lse = [lse0, lse1]
        else:
            of, *lse = _fox_fwd(qaug, kaug_t, vo, "fox_fwd")
        lru_vec = jnp.concatenate([W["lru_b_r"][l][None], W["lru_b_i"][l][None], W["lru_lambda"][l][None],
                                   W["conv_b"][l][None], jnp.zeros((4, LW), F32)], axis=0)
        oc, hs = _lru_fwd(proj, W["conv_w"][l], W["lru_w_r"][l], W["lru_w_i"][l], lru_vec, "lru_fwd")
        *bs, merged = _branch_merge_fwd(oa, of, oc, WL["w_branch"][l], proj, "branch_merge_fwd")
        h2 = _mm(merged, WL["w_out"][l], res=h, tm=tm, tn=512, tk=D, name="mm_out_fwd")
        u2 = _rms_fwd(h2, W["norm_ffn"][l], "rms_ffn_fwd")
        gate, up, act = _ffn_in_swiglu_fwd(u2, WL["w_ffn_in"][l], "ffn_in_swiglu_fwd")
        h = _mm(act, WL["w_ffn_out"][l], res=h2, tm=tm, tn=512, tk=_pick(DFF, (1408, 256)), name="mm_ffn_out_fwd")
        sv.update(u=u, proj=proj, oa=oa, of=of, oc=oc, lse=lse, hs=hs, fb=fb, qaug=qaug, kaug=kaug, kaug_t=kaug_t, vm=vm, lru_vec=lru_vec,
                  bs=bs, merged=merged, h2=h2, u2=u2, gate=gate, up=up, act=act)
        saved.append(sv)

    tgt_pad = tgt
    dh, dhb, dg_final, loss_vec = _loss_head(h, tgt_pad, W["norm_final"], "loss_head")
    loss = loss_vec[0, 0]

    small = ("norm_mix", "swa_sinks", "fox_forget_bias", "conv_w", "conv_b", "lru_w_r", "lru_b_r", "lru_w_i", "lru_b_i",
             "lru_lambda", "norm_ffn")
    G = {n: [None] * DEPTH for n in small}
    G["norm_final"] = dg_final.reshape(D)
    GW = {n: None for n in BIG}
    dbias = jnp.zeros((NH, BLK, 2 * BLK), F32)
    tkT = tm
    for l in reversed(range(DEPTH)):
        sv = saved[l]
        GW["w_ffn_out"] = _mm(_transpose(sv["act"], "tr_act"), dhb, tm=_pick(DFF, (1408, 256)), tn=D, tk=tkT,
                              slab=(GW["w_ffn_out"], l, DEPTH), name="mm_ffn_out_dw")
        dgate, dup = _ffn_out_dx_swiglu_bwd(dhb, WL["w_ffn_out"][l], sv["gate"], sv["up"], "ffn_out_dx_swiglu_bwd")
        u2t = _transpose(sv["u2"], "tr_u2")
        du2 = None
        for half, dpart in enumerate((dgate, dup)):
            GW["w_ffn_in"] = _mm(u2t, dpart, tm=D, tn=_pick(DFF, (1408, 256)), tk=tkT, slab=(GW["w_ffn_in"], l, DEPTH),
                                 col0=half * DFF, cols=2 * DFF, name="mm_ffn_in_dw")
            du2 = _mm(dpart, WL["w_ffn_in"][l], tb=True, res=du2, b_k0=half * DFF, tm=tm, tn=512,
                      tk=_pick(DFF, (1408, 256)), name="mm_ffn_in_dx")
        dh, dhb, dgn = _rms_bwd(du2, sv["h2"], W["norm_ffn"][l], dh, "rms_ffn_bwd")
        G["norm_ffn"][l] = dgn.reshape(D)
        GW["w_out"] = _mm(_transpose(sv["merged"], "tr_merged"), dhb, tm=D, tn=D, tk=tkT,
                          slab=(GW["w_out"], l, DEPTH), name="mm_out_dw")
        db0, db1, db2, dg0, dg1, dg2 = _out_dx_merge_bwd(dhb, WL["w_out"][l], sv["proj"], *sv["bs"], "out_dx_merge_bwd")
        dos = []
        for g, (o, db) in enumerate(zip((sv["oa"], sv["of"], sv["oc"]), (db0, db1, db2))):
            GW["w_branch"] = _mm(_transpose(o, "tr_branch"), db, tm=LW, tn=D, tk=tkT,
                                 slab=(GW["w_branch"], 3 * l + g, 3 * DEPTH), name="mm_branch_dw")
            dos.append(_mm(db, WL["w_branch"][l][g], tb=True, out_dtype=BF16, tm=tm, tn=LW, tk=D, name="mm_branch_dx"))
        dqa, dkb, dvb, dbias, dsk = _swa_bwd(sv["proj"], bias, W["swa_sinks"][l], dos[0], dbias, "swa_bwd")
        dka, dva = _band_fold(dkb, dvb, "swa_band_fold")
        G["swa_sinks"][l] = dsk[0, :NH]
        delta = _fox_delta(dos[1], sv["of"], "fox_delta")
        dqf, dqx, dkf, dvf, dkx = _fox_bwd(sv["qaug"], sv["kaug"], sv["kaug_t"], sv["vm"], dos[1], sv["lse"], delta, "fox_bwd")
        dfl, dfb = _cum_bwd(dqx, dkx, sv["proj"], sv["fb"], "cum_bwd")
        G["fox_forget_bias"][l] = dfb[:, 0]
        dxc, dyc, dwr, dwi, dvec = _lru_bwd(sv["proj"], sv["hs"], dos[2], W["conv_w"][l], W["lru_w_r"][l], W["lru_w_i"][l],
                                            sv["lru_vec"], "lru_bwd")
        G["lru_w_r"][l], G["lru_w_i"][l] = _unpair_blocks(dwr), _unpair_blocks(dwi)
        G["lru_b_r"][l], G["lru_b_i"][l], G["lru_lambda"][l], G["conv_b"][l] = dvec[0], dvec[1], dvec[2], dvec[3]
        G["conv_w"][l] = dvec[4:8]
        dproj = jnp.concatenate([dqa, dqf, dkf, dvf, dxc, dyc, dg0, dg1, dg2, dka, dva, dfl], axis=1)
        GW["w_in"] = _mm(_transpose(sv["u"], "tr_u"), dproj, tm=D, tn=IN_SLOT, tk=tkT,
                         slab=(GW["w_in"], l, DEPTH), name="mm_in_dw")
        du = _mm(dproj, WL["w_in"][l], tb=True, tm=tm, tn=512, tk=_pick(INP, (1664, 512)), name="mm_in_dx")
        dh, dhb, dgn = _rms_bwd(du, sv["h0"], W["norm_mix"][l], dh, "rms_mix_bwd")
        G["norm_mix"][l] = dgn.reshape(D)

    grads = {n: (jnp.stack(v) if isinstance(v, list) else v) for n, v in G.items()}
    grads.update(GW)
    grads["w_branch"] = GW["w_branch"].reshape(DEPTH, 3 * LW, D)
    grads["rel_bias_table"] = _bias_bwd(dbias, bucket, "bias_bwd")
    grads["meta_tokens"] = dh[NPAD:BLK]
    return loss, dh[BLK:], grads


NAMES = ("meta_tokens", "rel_bias_table", "norm_mix", "w_in", "swa_sinks", "fox_forget_bias", "conv_w", "conv_b",
         "lru_w_r", "lru_b_r", "lru_w_i", "lru_b_i", "lru_lambda", "w_branch", "w_out", "norm_ffn", "w_ffn_in",
         "w_ffn_out", "norm_final")


def _three_d(n, a):
    return a.reshape(DEPTH, 3 * LW, -1) if n == "w_branch" else a


GATHER_AXES = [BIG_AXIS[n] - 1 for n in BIG]


def _gather_weights(P):
    x, y, c = _here()
    mine, me = 2 * x + y, 4 * x + 2 * y + c
    placed = []
    for l in range(DEPTH):
        bufs = []
        for n in BIG:
            shard = _three_d(n, P[n])[l].astype(BF16)
            if n == "w_in":
                shard = jnp.pad(shard, ((0, 0), (0, IN_SLOT - IN_SHARD)))
            zero = jnp.zeros_like(shard)
            bufs.append(jnp.concatenate([jnp.where(mine == j, shard, zero) for j in range(N_SHARD)], axis=BIG_AXIS[n] - 1))
        placed.append(bufs)
    full = {n: _layer_layout(n, a) for n, a in zip(BIG, _all_gather_weights(placed[0], GATHER_AXES, "ag_weights"))}
    tiny = _all_gather_small(_place(_pack(P, TINY)[None], 8, me), "ag_tiny_weights")
    parts = [_unpack(tiny[2 * j], TINY, {n: P[n].shape for n in TINY}) for j in range(N_SHARD)]
    for n in TINY:
        full[n] = jnp.concatenate([p[n] for p in parts], axis=SHARD_AXIS[n])
    for n in SMALL:
        full[n] = P[n]
    full["lru_w_r"] = jnp.stack([_pair_blocks(P["lru_w_r"][l]) for l in range(DEPTH)]).astype(BF16)
    full["lru_w_i"] = jnp.stack([_pair_blocks(P["lru_w_i"][l]) for l in range(DEPTH)]).astype(BF16)
    return full, placed


def _reduce_grads(grads, P):
    x, y, c = _here()
    mine, me = 2 * x + y, 4 * x + 2 * y + c
    cidx = jnp.reshape(c, (1,)).astype(jnp.int32)
    chip = jnp.reshape(mine, (1,)).astype(jnp.int32)
    axes = [BIG_AXIS[n] for n in BIG]
    gs = [grads[n] for n in BIG]
    pairs = [_add_half(g, r, cidx, "rs_add_half_" + n) for n, g, r in zip(BIG, gs, _swap_halves(gs, "rs_swap_halves"))]
    ss, sbs = [list(t) for t in zip(*pairs)]
    ss[0], sbs[0] = _kernel_to_gathered_cols(ss[0]), _kernel_to_gathered_cols(sbs[0])
    recv = _scatter_to_chips(sbs, axes, "rs_scatter")
    tots = [_add_chips(s, r, ax, chip, "rs_add_chips_" + n) for n, s, r, ax in zip(BIG, ss, recv, axes)]
    fins = dict(zip(BIG, _join_halves([_place(t, DEPTH, 2 * c) for t in tots], "rs_join_halves")))
    out = {n: fins[n].reshape(P[n].shape) for n in BIG if n != "w_in"}
    out["w_in"] = fins["w_in"][:, :, :IN_SHARD]
    names = SMALL + TINY
    gathered = _all_gather_small(_place(_pack(grads, names)[None], 8, me), "ag_small_grads")
    small = _unpack(_sum_lead(gathered, "sum_small_grads"), names, {n: grads[n].shape for n in names})
    for n in SMALL:
        out[n] = small[n]
    for n in TINY:
        w = P[n].shape[SHARD_AXIS[n]]
        out[n] = lax.dynamic_slice_in_dim(small[n], mine * w, w, axis=SHARD_AXIS[n])
    return out


def _update(P, Gd, M, V):
    delta, new_m, new_v = {}, {}, {}
    for n in BIG + TINY:
        shp = P[n].shape
        two = (int(np.prod(shp[:-1])), shp[-1])
        d, m, v = _adamw(P[n].reshape(two), Gd[n].reshape(two), M[n].reshape(two), V[n].reshape(two), "adamw_" + n)
        delta[n], new_m[n], new_v[n] = d.reshape(shp), m.reshape(shp), v.reshape(shp)
    shapes = {n: P[n].shape for n in SMALL}
    d, m, v = _adamw(_pack(P, SMALL), _pack(Gd, SMALL), _pack(M, SMALL), _pack(V, SMALL), "adamw_small")
    for dst, buf in ((delta, d), (new_m, m), (new_v, v)):
        dst.update(_unpack(buf, SMALL, shapes))
    return delta, new_m, new_v


def kernel(x, meta_tokens, rel_bias_table, norm_mix, w_in, swa_sinks, fox_forget_bias, conv_w, conv_b, lru_w_r, lru_b_r, lru_w_i, lru_b_i, lru_lambda, w_branch, w_out, norm_ffn, w_ffn_in, w_ffn_out, norm_final, loss_target, m_meta_tokens, m_rel_bias_table, m_norm_mix, m_w_in, m_swa_sinks, m_fox_forget_bias, m_conv_w, m_conv_b, m_lru_w_r, m_lru_b_r, m_lru_w_i, m_lru_b_i, m_lru_lambda, m_w_branch, m_w_out, m_norm_ffn, m_w_ffn_in, m_w_ffn_out, m_norm_final, v_meta_tokens, v_rel_bias_table, v_norm_mix, v_w_in, v_swa_sinks, v_fox_forget_bias, v_conv_w, v_conv_b, v_lru_w_r, v_lru_b_r, v_lru_w_i, v_lru_b_i, v_lru_lambda, v_w_branch, v_w_out, v_norm_ffn, v_w_ffn_in, v_w_ffn_out, v_norm_final):
    P = dict(zip(NAMES, (meta_tokens, rel_bias_table, norm_mix, w_in, swa_sinks, fox_forget_bias, conv_w, conv_b, lru_w_r,
                         lru_b_r, lru_w_i, lru_b_i, lru_lambda, w_branch, w_out, norm_ffn, w_ffn_in, w_ffn_out, norm_final)))
    M = dict(zip(NAMES, (m_meta_tokens, m_rel_bias_table, m_norm_mix, m_w_in, m_swa_sinks, m_fox_forget_bias, m_conv_w,
                         m_conv_b, m_lru_w_r, m_lru_b_r, m_lru_w_i, m_lru_b_i, m_lru_lambda, m_w_branch, m_w_out, m_norm_ffn,
                         m_w_ffn_in, m_w_ffn_out, m_norm_final)))
    V = dict(zip(NAMES, (v_meta_tokens, v_rel_bias_table, v_norm_mix, v_w_in, v_swa_sinks, v_fox_forget_bias, v_conv_w,
                         v_conv_b, v_lru_w_r, v_lru_b_r, v_lru_w_i, v_lru_b_i, v_lru_lambda, v_w_branch, v_w_out, v_norm_ffn,
                         v_w_ffn_in, v_w_ffn_out, v_norm_final)))
    W, placed = _gather_weights(P)
    loss_local, grad_x, grads = _local_step(x[0], loss_target[0], W, placed)
    loss = lax.psum(loss_local, ("x", "y", "c"))
    Gd = _reduce_grads(grads, P)
    delta, new_m, new_v = _update(P, Gd, M, V)
    return (loss, grad_x[None], *[Gd[n] for n in NAMES], *[delta[n] for n in NAMES],
            *[new_m[n] for n in NAMES], *[new_v[n] for n in NAMES])
```

```python
import functools
import math

import numpy as np
import jax
import jax.numpy as jnp
from jax import lax
from jax.experimental import pallas as pl
from jax.experimental.pallas import tpu as pltpu

F32, BF16 = jnp.float32, jnp.bfloat16
MESH = pl.DeviceIdType.MESH
ANY = pl.BlockSpec(memory_space=pl.ANY)
SMEM = pl.BlockSpec(memory_space=pltpu.SMEM)

D = 1024
DEPTH = 4
BLK = 128
N_META = 16
NPAD = 112
NH = 8
LW = 512
DFF = 2816
EPS = 1e-6
NEG = -1e30
SCALE = 0.125
LRU_C = 8.0
REL_BUCKETS = 32
N_SHARD = 4
QA, QF, KF, VF, XC, YC, GT, KA, VA, FL, INP = 0, 512, 1024, 1536, 2048, 2560, 3072, 6144, 6272, 6400, 6656
IN_COLS = 6408
VMEM_LIMIT = 48 * 1024 * 1024

ADAM_LR, ADAM_B1, ADAM_B2, ADAM_EPS, ADAM_WD, ADAM_STEP = 0.001, 0.9, 0.999, 1e-08, 0.01, 10


def _cp(*sem):
    return pltpu.CompilerParams(dimension_semantics=sem or None, vmem_limit_bytes=VMEM_LIMIT)


def _pick(n, prefs):
    for p in prefs:
        if n % p == 0:
            return p
    return n


def _rt(T):
    return _pick(T, (384, 128))


def _sigmoid(z):
    return 1.0 / (1.0 + jnp.exp(-z))


def _log_sigmoid(z):
    return jnp.minimum(z, 0.0) - jnp.log(1.0 + jnp.exp(-jnp.abs(z)))


def _gelu(y):
    c = math.sqrt(2.0 / math.pi)
    return 0.5 * y * (1.0 + jnp.tanh(c * (y + 0.044715 * y * y * y)))


def _gelu_grad(y):
    c = math.sqrt(2.0 / math.pi)
    t = jnp.tanh(c * (y + 0.044715 * y * y * y))
    return 0.5 * (1.0 + t) + 0.5 * y * (1.0 - t * t) * c * (1.0 + 3.0 * 0.044715 * y * y)


def _neg_expm1(z):
    series = -z * (1.0 + z * (0.5 + z * (1.0 / 6.0 + z * (1.0 / 24.0 + z * (1.0 / 120.0)))))
    return jnp.where(z > -0.1, series, 1.0 - jnp.exp(z))


def _dot(a, b, ca, cb):
    return lax.dot_general(a, b, (((ca,), (cb,)), ((), ())), preferred_element_type=F32)


def _mm(a, b, *, ta=False, tb=False, res=None, out_dtype=F32, tm, tn, tk, name, slab=None, b_k0=0, col0=0, cols=None):
    M, K = (a.shape[1], a.shape[0]) if ta else a.shape
    N = b.shape[0] if tb else b.shape[1]
    assert (b.shape[1] if tb else b.shape[0]) >= K + b_k0 and M % tm == 0 and N % tn == 0 and K % tk == 0, (name, a.shape, b.shape)
    assert b_k0 % tk == 0 and col0 % tn == 0
    nk, kb, jb = K // tk, b_k0 // tk, col0 // tn
    ca, cb = (0 if ta else 1), (1 if tb else 0)
    n_in = 2 + (res is not None) + (slab is not None and slab[0] is not None)

    def body(*refs):
        a_ref, b_ref = refs[:2]
        r_ref = refs[2] if res is not None else None
        o_ref = refs[n_in]
        part = _dot(a_ref[...].astype(BF16), b_ref[...].astype(BF16), ca, cb)

        def fin(acc):
            if res is not None:
                acc = acc + r_ref[...]
            o_ref[...] = acc.astype(out_dtype)

        if nk == 1:
            fin(part)
        else:
            acc_ref = refs[-1]
            k = pl.program_id(2)

            @pl.when(k == 0)
            def _():
                acc_ref[...] = part

            @pl.when(k > 0)
            def _():
                acc_ref[...] += part

            @pl.when(k == nk - 1)
            def _():
                fin(acc_ref[...])

    a_spec = pl.BlockSpec((tk, tm), lambda i, j, k: (k, i)) if ta else pl.BlockSpec((tm, tk), lambda i, j, k: (i, k))
    b_spec = (pl.BlockSpec((tn, tk), lambda i, j, k: (j, k + kb)) if tb
              else pl.BlockSpec((tk, tn), lambda i, j, k: (k + kb, j)))
    o_spec = pl.BlockSpec((tm, tn), lambda i, j, k: (i, j))
    in_specs, ops = [a_spec, b_spec], [a, b]
    if res is not None:
        in_specs.append(o_spec)
        ops.append(res)
    out_shape, aliases = jax.ShapeDtypeStruct((M, N), out_dtype), {}
    if slab is not None:
        buf, idx, n = slab
        o_spec = pl.BlockSpec((None, tm, tn), lambda i, j, k: (idx, i, j + jb))
        out_shape = jax.ShapeDtypeStruct((n, M, cols or N), out_dtype)
        if buf is not None:
            aliases = {len(ops): 0}
            in_specs.append(ANY)
            ops.append(buf)
    return pl.pallas_call(
        body, grid=(M // tm, N // tn, nk), in_specs=in_specs, out_specs=o_spec, out_shape=out_shape,
        input_output_aliases=aliases, scratch_shapes=[pltpu.VMEM((tm, tn), F32)] if nk > 1 else [],
        compiler_params=_cp("parallel", "parallel", "arbitrary"), name=name)(*ops)


def _transpose(x, name):
    T, C = x.shape
    tr, tc = _rt(T), _pick(C, (1408, 1024, 512, 256, 128))

    def body(x_ref, o_ref):
        o_ref[...] = x_ref[...].T

    return pl.pallas_call(
        body, grid=(T // tr, C // tc), in_specs=[pl.BlockSpec((tr, tc), lambda i, j: (i, j))],
        out_specs=pl.BlockSpec((tc, tr), lambda i, j: (j, i)), out_shape=jax.ShapeDtypeStruct((C, T), x.dtype),
        compiler_params=_cp("parallel", "parallel"), name=name)(x)


def _rms_fwd(h, g, name):
    T = h.shape[0]
    tr = _rt(T)

    def body(h_ref, g_ref, u_ref):
        x = h_ref[...]
        r = lax.rsqrt(jnp.mean(x * x, axis=-1, keepdims=True) + EPS)
        u_ref[...] = (x * r * g_ref[...]).astype(BF16)

    return pl.pallas_call(
        body, grid=(T // tr,),
        in_specs=[pl.BlockSpec((tr, D), lambda i: (i, 0)), pl.BlockSpec((1, D), lambda i: (0, 0))],
        out_specs=pl.BlockSpec((tr, D), lambda i: (i, 0)), out_shape=jax.ShapeDtypeStruct((T, D), BF16),
        compiler_params=_cp("parallel"), name=name)(h, g.reshape(1, D))


def _rms_bwd(du, h, g, dres, name):
    T = h.shape[0]
    tr = _rt(T)

    def body(du_ref, h_ref, g_ref, dres_ref, dh_ref, dhb_ref, dg_ref):
        x = h_ref[...]
        r = lax.rsqrt(jnp.mean(x * x, axis=-1, keepdims=True) + EPS)
        xh = x * r
        dy = du_ref[...]
        dxh = dy * g_ref[...]
        dx = r * (dxh - xh * jnp.mean(dxh * xh, axis=-1, keepdims=True))
        dh = dres_ref[...] + dx
        dh_ref[...] = dh
        dhb_ref[...] = dh.astype(BF16)
        part = jnp.sum(dy * xh, axis=0, keepdims=True)

        @pl.when(pl.program_id(0) == 0)
        def _():
            dg_ref[...] = part

        @pl.when(pl.program_id(0) > 0)
        def _():
            dg_ref[...] += part

    row = pl.BlockSpec((tr, D), lambda i: (i, 0))
    vec = pl.BlockSpec((1, D), lambda i: (0, 0))
    return pl.pallas_call(
        body, grid=(T // tr,), in_specs=[row, row, vec, row], out_specs=[row, row, vec],
        out_shape=[jax.ShapeDtypeStruct((T, D), F32), jax.ShapeDtypeStruct((T, D), BF16), jax.ShapeDtypeStruct((1, D), F32)],
        compiler_params=_cp("arbitrary"), name=name)(du, h, g.reshape(1, D), dres)


def _loss_head(h, tgt, g, name):
    T = h.shape[0]
    nb = T // BLK

    def body(h_ref, t_ref, g_ref, dh_ref, dhb_ref, dg_ref, loss_ref):
        i = pl.program_id(0)
        x = h_ref[...]
        r = lax.rsqrt(jnp.mean(x * x, axis=-1, keepdims=True) + EPS)
        xh = x * r
        gv = g_ref[...]
        tok = i >= 1
        err = jnp.where(tok, xh * gv - t_ref[...], 0.0)
        dy = err * (1.0 / D)
        dxh = dy * gv
        dx = r * (dxh - xh * jnp.mean(dxh * xh, axis=-1, keepdims=True))
        dh_ref[...] = dx
        dhb_ref[...] = dx.astype(BF16)
        dg = jnp.sum(dy * xh, axis=0, keepdims=True)
        ls = jnp.zeros((1, BLK), F32) + jnp.sum(err * err) * (0.5 / D)

        @pl.when(i == 0)
        def _():
            dg_ref[...] = dg
            loss_ref[...] = ls

        @pl.when(i > 0)
        def _():
            dg_ref[...] += dg
            loss_ref[...] += ls

    row = pl.BlockSpec((BLK, D), lambda i: (i, 0))
    vec = pl.BlockSpec((1, D), lambda i: (0, 0))
    return pl.pallas_call(
        body, grid=(nb,),
        in_specs=[row, pl.BlockSpec((BLK, D), lambda i: (jnp.maximum(i - 1, 0), 0)), vec],
        out_specs=[row, row, vec, pl.BlockSpec((1, BLK), lambda i: (0, 0))],
        out_shape=[jax.ShapeDtypeStruct((T, D), F32), jax.ShapeDtypeStruct((T, D), BF16),
                   jax.ShapeDtypeStruct((1, D), F32), jax.ShapeDtypeStruct((1, BLK), F32)],
        compiler_params=_cp("arbitrary"), name=name)(h, tgt, g.reshape(1, D))


def _bucket_table():
    q = np.arange(BLK)[:, None]
    k = np.arange(2 * BLK)[None, :]
    d = np.maximum(q + BLK - k, 0)
    max_exact = REL_BUCKETS // 2
    scaled = np.log(np.maximum(d, 1).astype(np.float32) / np.float32(max_exact)) / np.float32(math.log(128 / max_exact))
    large = np.minimum(max_exact + (scaled.astype(np.float32) * (REL_BUCKETS - max_exact)).astype(np.int32), REL_BUCKETS - 1)
    return np.where(d < max_exact, d, large).astype(np.int32)


def _bias_build(table, bucket, name):
    def body(t_ref, bk_ref, o_ref):
        bk = bk_ref[...]
        for h in range(NH):
            acc = jnp.zeros((BLK, 2 * BLK), F32)
            for b in range(REL_BUCKETS):
                acc = jnp.where(bk == b, t_ref[b, h], acc)
            o_ref[h] = acc

    return pl.pallas_call(
        body, in_specs=[SMEM, pl.BlockSpec(memory_space=pltpu.VMEM)], out_specs=pl.BlockSpec(memory_space=pltpu.VMEM),
        out_shape=jax.ShapeDtypeStruct((NH, BLK, 2 * BLK), F32), compiler_params=_cp(), name=name)(table, bucket)


def _bias_bwd(dbias, bucket, name):
    def body(d_ref, bk_ref, o_ref):
        bk = bk_ref[...]
        for h in range(NH):
            dh = d_ref[h]
            for b in range(REL_BUCKETS):
                o_ref[b, h] = jnp.sum(jnp.where(bk == b, dh, 0.0))

    return pl.pallas_call(
        body, in_specs=[pl.BlockSpec(memory_space=pltpu.VMEM)] * 2, out_specs=SMEM,
        out_shape=jax.ShapeDtypeStruct((REL_BUCKETS, NH), F32), compiler_params=_cp(), name=name)(dbias, bucket)


def _swa_specs(nq_cols):
    prev = lambda n: jnp.maximum(n - 1, 0)
    return [
        pl.BlockSpec((BLK, nq_cols), lambda n: (n, QA // nq_cols)),
        pl.BlockSpec((BLK, BLK), lambda n: (prev(n), KA // BLK)), pl.BlockSpec((BLK, BLK), lambda n: (n, KA // BLK)),
        pl.BlockSpec((BLK, BLK), lambda n: (prev(n), VA // BLK)), pl.BlockSpec((BLK, BLK), lambda n: (n, VA // BLK)),
    ]


def _swa_mask(n):
    row = lax.broadcasted_iota(jnp.int32, (BLK, 2 * BLK), 0)
    col = lax.broadcasted_iota(jnp.int32, (BLK, 2 * BLK), 1)
    dist = row + BLK - col
    return (dist >= 0) & (dist < BLK) & ((n - 1) * BLK + col >= NPAD)


def _swa_probs(qm, ksel, mask, bias_h, sink):
    s = _dot(qm, ksel, 1, 1) * SCALE
    s = jnp.where(mask, s + bias_h, NEG)
    m = jnp.maximum(jnp.max(s, axis=-1, keepdims=True), sink)
    p = jnp.exp(s - m)
    psink = jnp.exp(sink - m)
    inv = 1.0 / (jnp.sum(p, axis=-1, keepdims=True) + psink)
    return p * inv, psink * inv


def _swa_fwd(proj, bias, sinks, name):
    T = proj.shape[0]
    nb = T // BLK

    def body(sk_ref, q_ref, kp_ref, kc_ref, vp_ref, vc_ref, b_ref, o_ref):
        n = pl.program_id(0)
        lo = lax.broadcasted_iota(jnp.int32, (1, BLK), 1) < 64
        kb = jnp.concatenate([kp_ref[...], kc_ref[...]], axis=0)
        vb = jnp.concatenate([vp_ref[...], vc_ref[...]], axis=0)
        kbs = (kb.astype(BF16), pltpu.roll(kb, 64, 1).astype(BF16))
        vbs = (vb, pltpu.roll(vb, 64, 1))
        mask = _swa_mask(n)
        outs = []
        for pr in range(NH // 2):
            qp = q_ref[:, pr * BLK:(pr + 1) * BLK]
            kv = pr // 2
            acc = jnp.zeros((BLK, BLK), F32)
            for e in range(2):
                lm = lo if e == 0 else jnp.logical_not(lo)
                sw = 0 if kv == e else 1
                qm = jnp.where(lm, qp, 0.0).astype(BF16)
                pn, _ = _swa_probs(qm, kbs[sw], mask, b_ref[2 * pr + e], sk_ref[2 * pr + e])
                acc = acc + _dot(pn.astype(BF16), jnp.where(lm, vbs[sw], 0.0).astype(BF16), 1, 0)
            outs.append(acc)
        o_ref[...] = jnp.concatenate(outs, axis=1).astype(BF16)

    return pl.pallas_call(
        body, grid=(nb,),
        in_specs=[SMEM] + _swa_specs(512) + [pl.BlockSpec((NH, BLK, 2 * BLK), lambda n: (0, 0, 0))],
        out_specs=pl.BlockSpec((BLK, 512), lambda n: (n, 0)), out_shape=jax.ShapeDtypeStruct((T, 512), BF16),
        compiler_params=_cp("parallel"), name=name)(sinks, proj, proj, proj, proj, proj, bias)


def _swa_bwd(proj, bias, sinks, do, dbias_in, name):
    T = proj.shape[0]
    nb = T // BLK

    def body(sk_ref, q_ref, kp_ref, kc_ref, vp_ref, vc_ref, b_ref, do_ref, dbi_ref,
             dq_ref, dk_ref, dv_ref, db_ref, dsk_ref, sk_acc):
        n = pl.program_id(0)
        lane = lax.broadcasted_iota(jnp.int32, (1, BLK), 1)
        lo = lane < 64
        kb = jnp.concatenate([kp_ref[...], kc_ref[...]], axis=0)
        vb = jnp.concatenate([vp_ref[...], vc_ref[...]], axis=0)
        kbs = (kb, pltpu.roll(kb, 64, 1))
        vbs = (vb, pltpu.roll(vb, 64, 1))
        mask = _swa_mask(n)

        @pl.when(n == 0)
        def _():
            db_ref[...] = dbi_ref[...]
            sk_acc[...] = jnp.zeros_like(sk_acc)

        dqs = []
        dk = jnp.zeros((2 * BLK, BLK), F32)
        dv = jnp.zeros((2 * BLK, BLK), F32)
        for pr in range(NH // 2):
            qp = q_ref[:, pr * BLK:(pr + 1) * BLK]
            dop = do_ref[:, pr * BLK:(pr + 1) * BLK].astype(F32)
            kv = pr // 2
            dq = jnp.zeros((BLK, BLK), F32)
            for e in range(2):
                h = 2 * pr + e
                lm = lo if e == 0 else jnp.logical_not(lo)
                sw = 0 if kv == e else 1
                qm = jnp.where(lm, qp, 0.0)
                dom = jnp.where(lm, dop, 0.0)
                pn, ps = _swa_probs(qm.astype(BF16), kbs[sw].astype(BF16), mask, b_ref[h], sk_ref[h])
                dp = _dot(dom.astype(BF16), vbs[sw].astype(BF16), 1, 1)
                delta = jnp.sum(pn * dp, axis=-1, keepdims=True)
                ds = pn * (dp - delta)
                db_ref[h] += ds
                sk_acc[...] += jnp.where(lane == h, -(ps * delta), 0.0)
                dsb = (ds * SCALE).astype(BF16)
                dq = dq + _dot(dsb, jnp.where(lm, kbs[sw], 0.0).astype(BF16), 1, 0)
                qk = qm if sw == 0 else pltpu.roll(qm, 64, 1)
                dok = dom if sw == 0 else pltpu.roll(dom, 64, 1)
                dk = dk + _dot(dsb, qk.astype(BF16), 0, 0)
                dv = dv + _dot(pn.astype(BF16), dok.astype(BF16), 0, 0)
            dqs.append(dq)
        dq_ref[...] = jnp.concatenate(dqs, axis=1).astype(BF16)
        dk_ref[0] = dk
        dv_ref[0] = dv

        @pl.when(n == nb - 1)
        def _():
            dsk_ref[...] = jnp.sum(sk_acc[...], axis=0, keepdims=True)

    full_b = pl.BlockSpec((NH, BLK, 2 * BLK), lambda n: (0, 0, 0))
    band = pl.BlockSpec((1, 2 * BLK, BLK), lambda n: (n, 0, 0))
    return pl.pallas_call(
        body, grid=(nb,),
        in_specs=[SMEM] + _swa_specs(512) + [full_b, pl.BlockSpec((BLK, 512), lambda n: (n, 0)), full_b],
        out_specs=[pl.BlockSpec((BLK, 512), lambda n: (n, 0)), band, band, full_b, pl.BlockSpec((1, BLK), lambda n: (0, 0))],
        out_shape=[jax.ShapeDtypeStruct((T, 512), BF16), jax.ShapeDtypeStruct((nb, 2 * BLK, BLK), F32),
                   jax.ShapeDtypeStruct((nb, 2 * BLK, BLK), F32), jax.ShapeDtypeStruct((NH, BLK, 2 * BLK), F32),
                   jax.ShapeDtypeStruct((1, BLK), F32)],
        scratch_shapes=[pltpu.VMEM((BLK, BLK), F32)],
        compiler_params=_cp("arbitrary"), name=name)(sinks, proj, proj, proj, proj, proj, bias, do, dbias_in)


def _band_fold(dkb, dvb, name):
    nb = dkb.shape[0]

    def body(ko_ref, kn_ref, vo_ref, vn_ref, dk_ref, dv_ref):
        last = pl.program_id(0) == nb - 1
        dk_ref[...] = (ko_ref[0] + jnp.where(last, 0.0, kn_ref[0])).astype(BF16)
        dv_ref[...] = (vo_ref[0] + jnp.where(last, 0.0, vn_ref[0])).astype(BF16)

    own = pl.BlockSpec((1, BLK, BLK), lambda j: (j, 1, 0))
    nxt = pl.BlockSpec((1, BLK, BLK), lambda j: (jnp.minimum(j + 1, nb - 1), 0, 0))
    out = pl.BlockSpec((BLK, BLK), lambda j: (j, 0))
    return pl.pallas_call(
        body, grid=(nb,), in_specs=[own, nxt, own, nxt], out_specs=[out, out],
        out_shape=[jax.ShapeDtypeStruct((nb * BLK, BLK), BF16)] * 2,
        compiler_params=_cp("parallel"), name=name)(dkb, dkb, dvb, dvb)


def _token_major(x, width):
    full = jnp.concatenate([x, jnp.zeros((BLK - NH, BLK), F32)], axis=0).T
    return full if width == BLK else jnp.concatenate([full, jnp.zeros((BLK, width - BLK), F32)], axis=1)


def _cum_fwd(proj, fb, name):
    T = proj.shape[0]

    def body(z_ref, fb_ref, c_ref, carry):
        b = pl.program_id(0)
        lane = lax.broadcasted_iota(jnp.int32, (NH, BLK), 1)

        @pl.when(b == 0)
        def _():
            carry[...] = jnp.zeros_like(carry)

        z = z_ref[...].T[0:NH, :] + fb_ref[...]
        x = jnp.where(b * BLK + lane >= NPAD, _log_sigmoid(z), 0.0)
        s = 1
        while s < BLK:
            x = x + jnp.where(lane >= s, pltpu.roll(x, s, 1), 0.0)
            s *= 2
        x = x + carry[...]
        carry[...] = jnp.zeros((NH, BLK), F32) + jnp.sum(jnp.where(lane == BLK - 1, x, 0.0), axis=-1, keepdims=True)
        c_ref[...] = _token_major(x, BLK)

    return pl.pallas_call(
        body, grid=(T // BLK,),
        in_specs=[pl.BlockSpec((BLK, BLK), lambda b: (b, FL // BLK)), pl.BlockSpec((NH, 1), lambda b: (0, 0))],
        out_specs=pl.BlockSpec((BLK, BLK), lambda b: (b, 0)), out_shape=jax.ShapeDtypeStruct((T, BLK), F32),
        scratch_shapes=[pltpu.VMEM((NH, BLK), F32)], compiler_params=_cp("arbitrary"), name=name)(proj, fb)


def _cum_bwd(dqx, dkx, proj, fb, name):
    T = proj.shape[0]
    nb = T // BLK

    def body(dq_ref, dk_ref, z_ref, fb_ref, dz_ref, db_ref, carry):
        k = pl.program_id(0)
        b = nb - 1 - k
        lane = lax.broadcasted_iota(jnp.int32, (NH, BLK), 1)

        @pl.when(k == 0)
        def _():
            carry[...] = jnp.zeros_like(carry)
            db_ref[...] = jnp.zeros_like(db_ref)

        def picked(ref, r_first, r_second):
            rows = []
            for p in range(NH // 2):
                t_ = ref[:, p * BLK:(p + 1) * BLK].T
                rows += [t_[r_first:r_first + 1, :], t_[r_second:r_second + 1, :]]
            return jnp.concatenate(rows, axis=0)

        x = picked(dq_ref, 64, 0) - picked(dk_ref, 67, 3)
        s = 1
        while s < BLK:
            x = x + jnp.where(lane < BLK - s, pltpu.roll(x, BLK - s, 1), 0.0)
            s *= 2
        x = x + carry[...]
        carry[...] = jnp.zeros((NH, BLK), F32) + jnp.sum(jnp.where(lane == 0, x, 0.0), axis=-1, keepdims=True)
        z = z_ref[...].T[0:NH, :] + fb_ref[...]
        dz = jnp.where(b * BLK + lane >= NPAD, x * _sigmoid(-z), 0.0)
        db_ref[...] += jnp.sum(dz, axis=-1, keepdims=True)
        dz_ref[...] = _token_major(dz, 2 * BLK).astype(BF16)

    rev = lambda k: nb - 1 - k
    wide = pl.BlockSpec((BLK, 512), lambda k: (rev(k), 0))
    return pl.pallas_call(
        body, grid=(nb,),
        in_specs=[wide, wide, pl.BlockSpec((BLK, BLK), lambda k: (rev(k), FL // BLK)), pl.BlockSpec((NH, 1), lambda k: (0, 0))],
        out_specs=[pl.BlockSpec((BLK, 2 * BLK), lambda k: (rev(k), 0)), pl.BlockSpec((NH, BLK), lambda k: (0, 0))],
        out_shape=[jax.ShapeDtypeStruct((T, 2 * BLK), BF16), jax.ShapeDtypeStruct((NH, BLK), F32)],
        scratch_shapes=[pltpu.VMEM((NH, BLK), F32)], compiler_params=_cp("arbitrary"), name=name)(dqx, dkx, proj, fb)


def _fox_prep(proj, ccol, name):
    T = proj.shape[0]
    tr = _pick(T, (1408, 384, 128))

    def body(q_ref, k_ref, v_ref, cc_ref, qa_ref, ka_ref, kt_ref, vm_ref, vo_ref):
        h = pl.program_id(1)
        lane = lax.broadcasted_iota(jnp.int32, (1, BLK), 1)
        own = (lane >> 6) == (h & 1)
        a0 = 64 * (1 - (h & 1))
        c = _lane_pick(cc_ref[...], lane, h)
        hi = c.astype(BF16).astype(F32)
        mid = (c - hi).astype(BF16).astype(F32)
        lo = (c - hi - mid).astype(BF16).astype(F32)
        ones = (lane >= a0 + 3) & (lane < a0 + 6)
        qa = jnp.where(own, q_ref[...] * SCALE, jnp.where(ones, 1.0, 0.0))
        qa = jnp.where(lane == a0, hi, jnp.where(lane == a0 + 1, mid, jnp.where(lane == a0 + 2, lo, qa)))
        ones = (lane >= a0) & (lane < a0 + 3)
        ka = jnp.where(own, k_ref[...], jnp.where(ones, 1.0, 0.0))
        ka = jnp.where(lane == a0 + 3, -hi, jnp.where(lane == a0 + 4, -mid, jnp.where(lane == a0 + 5, -lo, ka)))
        qa_ref[...] = qa.astype(BF16)
        kab = ka.astype(BF16)
        ka_ref[...] = kab
        kt_ref[...] = kab.T
        vm = jnp.where(own, v_ref[...], 0.0)
        vm_ref[...] = vm.astype(BF16)
        vo_ref[...] = jnp.where(lane == a0, 1.0, vm).astype(BF16)

    pair = lambda col0: pl.BlockSpec((tr, BLK), lambda i, h: (i, col0 // BLK + (h >> 1)))
    out = pl.BlockSpec((None, tr, BLK), lambda i, h: (h, i, 0))
    out_t = pl.BlockSpec((None, BLK, tr), lambda i, h: (h, 0, i))
    tok = jax.ShapeDtypeStruct((NH, T, BLK), BF16)
    return pl.pallas_call(
        body, grid=(T // tr, NH), in_specs=[pair(QF), pair(KF), pair(VF), pl.BlockSpec((tr, BLK), lambda i, h: (i, 0))],
        out_specs=[out, out, out_t, out, out], out_shape=[tok, tok, jax.ShapeDtypeStruct((NH, BLK, T), BF16), tok, tok],
        compiler_params=_cp("parallel", "arbitrary"), name=name)(proj, proj, proj, ccol)


def _fox_fwd(qaug, kaug_t, vo, name, gather=None):
    T = qaug.shape[1]
    t = _rt(T)
    nt = T // t
    ng = len(gather[0]) if gather else 0

    def body(q0, q1, k0, k1, v0, v1, *rest):
        o_ref, lse0_ref, lse1_ref = rest[ng:ng + 3]
        m_ref, acc_ref = rest[2 * ng + 3:2 * ng + 5]
        p_, i, j = pl.program_id(0), pl.program_id(1), pl.program_id(2)
        lane = lax.broadcasted_iota(jnp.int32, (1, BLK), 1)
        lo = lane < 64
        if gather:
            start, finish = _gather_plan(rest[ng + 3:2 * ng + 3], gather[1], *rest[2 * ng + 5:])
            pl.when((p_ == 0) & (i == 0) & (j == 0))(start)

        @pl.when(j == 0)
        def _():
            m_ref[...] = jnp.full_like(m_ref, NEG)
            acc_ref[...] = jnp.zeros_like(acc_ref)

        def step(masked):
            for e, (q_ref, k_ref, v_ref) in enumerate(((q0, k0, v0), (q1, k1, v1))):
                s = _dot(q_ref[...], k_ref[...], 1, 0)
                if masked:
                    s = jnp.where(_fox_mask(i, j, t), s, NEG)
                m_old = m_ref[e]
                m_new = jnp.maximum(m_old, jnp.max(s, axis=-1, keepdims=True))
                m_ref[e] = m_new
                pe = jnp.exp(s - jnp.concatenate([m_new] * (t // BLK), axis=1))
                acc_ref[e] = jnp.exp(m_old - m_new) * acc_ref[e] + _dot(pe.astype(BF16), v_ref[...], 1, 0)

        pl.when((j < i) & (j > 0))(lambda: step(False))
        pl.when((j == i) | ((j == 0) & (i > 0)))(lambda: step(True))

        @pl.when(j == i)
        def _():
            rows = i * t + lax.broadcasted_iota(jnp.int32, (t, 1), 0)
            l0, l1 = _lane_pick(acc_ref[0], lane, 64), _lane_pick(acc_ref[1], lane, 0)
            o = jnp.where(lo, acc_ref[0] / l0, acc_ref[1] / l1)
            o_ref[...] = jnp.where(rows >= NPAD, o, 0.0).astype(BF16)
            lse0_ref[...] = m_ref[0] + jnp.log(l0)
            lse1_ref[...] = m_ref[1] + jnp.log(l1)

        if gather:
            pl.when((p_ == NH // 2 - 1) & (i == nt - 1) & (j == nt - 1))(finish)

    kj = lambda i, j: jnp.minimum(j, i)
    qs = lambda e: pl.BlockSpec((None, t, BLK), lambda p, i, j: (2 * p + e, i, 0))
    ks = lambda e: pl.BlockSpec((None, t, BLK), lambda p, i, j: (2 * p + e, kj(i, j), 0))
    kts = lambda e: pl.BlockSpec((None, BLK, t), lambda p, i, j: (2 * p + e, 0, kj(i, j)))
    rep = pl.BlockSpec((None, t, BLK), lambda p, i, j: (p, i, 0))
    bufs = list(gather[0]) if gather else []
    return pl.pallas_call(
        body, grid=(NH // 2, nt, nt), in_specs=[qs(0), qs(1), kts(0), kts(1), ks(0), ks(1)] + [ANY] * ng,
        out_specs=[pl.BlockSpec((t, BLK), lambda p, i, j: (i, p)), rep, rep] + [ANY] * ng,
        out_shape=[jax.ShapeDtypeStruct((T, 512), BF16)] + [jax.ShapeDtypeStruct((NH // 2, T, BLK), F32)] * 2
        + [jax.ShapeDtypeStruct(b.shape, b.dtype) for b in bufs],
        input_output_aliases={6 + g: 3 + g for g in range(ng)},
        scratch_shapes=[pltpu.VMEM((2, t, BLK), F32), pltpu.VMEM((2, t, BLK), F32)]
        + ([pltpu.SemaphoreType.DMA((6 * ng,)), pltpu.SemaphoreType.DMA((6 * ng,))] if gather else []),
        compiler_params=(pltpu.CompilerParams(dimension_semantics=("arbitrary",) * 3, vmem_limit_bytes=VMEM_LIMIT,
                                              has_side_effects=True) if gather
                         else _cp("parallel", "parallel", "arbitrary")), name=name)(qaug, qaug, kaug_t, kaug_t, vo, vo, *bufs)


def _fox_delta(do, o, name):
    T = do.shape[0]
    tr = _rt(T)

    def body(do_ref, o_ref, d0_ref, d1_ref):
        lo = lax.broadcasted_iota(jnp.int32, (1, BLK), 1) < 64
        prod = do_ref[...].astype(F32) * o_ref[...].astype(F32)
        d0_ref[...] = jnp.zeros((tr, BLK), F32) + jnp.sum(jnp.where(lo, prod, 0.0), axis=-1, keepdims=True)
        d1_ref[...] = jnp.zeros((tr, BLK), F32) + jnp.sum(jnp.where(lo, 0.0, prod), axis=-1, keepdims=True)

    blk = pl.BlockSpec((tr, BLK), lambda i, p: (i, p))
    rep = pl.BlockSpec((None, tr, BLK), lambda i, p: (p, i, 0))
    return pl.pallas_call(
        body, grid=(T // tr, NH // 2), in_specs=[blk, blk], out_specs=[rep, rep],
        out_shape=[jax.ShapeDtypeStruct((NH // 2, T, BLK), F32)] * 2,
        compiler_params=_cp("parallel", "parallel"), name=name)(do, o)


def _fox_bwd(qaug, kaug, kaug_t, vm, do, lses, deltas, name, scatter=None):
    T = qaug.shape[1]
    t = _rt(T)
    nt = T // t
    ng = len(scatter[0]) if scatter else 0

    def body(q0, q1, k0, k1, kt0, kt1, v0, v1, do_ref, lse0, lse1, dl0, dl1, *rest):
        dq_ref, dqx_ref, dk_ref, dv_ref, dkx_ref = rest[ng:ng + 5]
        dq_acc, dk_acc, dv_acc = rest[2 * ng + 5:2 * ng + 8]
        p_, j, i = pl.program_id(0), pl.program_id(1), pl.program_id(2)
        lane = lax.broadcasted_iota(jnp.int32, (1, BLK), 1)
        lo = lane < 64
        if scatter:
            start, finish = _scatter_plan(rest[:ng], rest[ng + 5:2 * ng + 5], scatter[1], *rest[2 * ng + 8:])
            pl.when((p_ == 0) & (j == 0) & (i == 0))(start)

        @pl.when((j == 0) & (i == 0))
        def _():
            dq_acc[...] = jnp.zeros_like(dq_acc)

        @pl.when(i == 0)
        def _():
            dk_acc[...] = jnp.zeros_like(dk_acc)
            dv_acc[...] = jnp.zeros_like(dv_acc)

        def step(masked):
            dob = do_ref[...]
            rows = pl.ds(pl.multiple_of(i * t, t), t)
            wide = lambda ref: jnp.concatenate([ref[...]] * (t // BLK), axis=1)
            for e, (q_ref, k_ref, kt_ref, v_ref, lse_ref, dl_ref) in enumerate(
                    ((q0, k0, kt0, v0, lse0, dl0), (q1, k1, kt1, v1, lse1, dl1))):
                s = _dot(q_ref[...], kt_ref[...], 1, 0)
                if masked:
                    s = jnp.where(_fox_mask(i, j, t), s, NEG)
                pe = jnp.exp(s - wide(lse_ref))
                dp = _dot(dob, v_ref[...], 1, 1)
                ds = (pe * (dp - wide(dl_ref))).astype(BF16)
                dq_acc[e, rows, :] += _dot(ds, k_ref[...], 1, 0)
                dk_acc[e] += _dot(ds, q_ref[...], 0, 0)
                dv_acc[e] += _dot(pe.astype(BF16), dob, 0, 0)

        pl.when((i > j) & (j > 0))(lambda: step(False))
        pl.when((i == j) | ((j == 0) & (i > 0)))(lambda: step(True))

        @pl.when(i == nt - 1)
        def _():
            dk_ref[...] = jnp.where(lo, dk_acc[0], dk_acc[1]).astype(BF16)
            dv_ref[...] = jnp.where(lo, dv_acc[0], dv_acc[1]).astype(BF16)
            dkx_ref[...] = jnp.where(lo, dk_acc[1], dk_acc[0])

        @pl.when((i == nt - 1) & (j == nt - 1))
        def _():
            dq_ref[...] = (jnp.where(lo, dq_acc[0], dq_acc[1]) * SCALE).astype(BF16)
            dqx_ref[...] = jnp.where(lo, dq_acc[1], dq_acc[0])

        if scatter:
            pl.when((p_ == NH // 2 - 1) & (i == nt - 1) & (j == nt - 1))(finish)

    qi = lambda j, i: jnp.maximum(i, j)
    qs = lambda e: pl.BlockSpec((None, t, BLK), lambda p, j, i: (2 * p + e, qi(j, i), 0))
    ks = lambda e: pl.BlockSpec((None, t, BLK), lambda p, j, i: (2 * p + e, j, 0))
    kts = lambda e: pl.BlockSpec((None, BLK, t), lambda p, j, i: (2 * p + e, 0, j))
    qside = pl.BlockSpec((t, BLK), lambda p, j, i: (qi(j, i), p))
    kside = pl.BlockSpec((t, BLK), lambda p, j, i: (j, p))
    rep = pl.BlockSpec((None, t, BLK), lambda p, j, i: (p, qi(j, i), 0))
    whole = pl.BlockSpec((T, BLK), lambda p, j, i: (0, p))
    sums = list(scatter[0]) if scatter else []
    return pl.pallas_call(
        body, grid=(NH // 2, nt, nt),
        in_specs=[qs(0), qs(1), ks(0), ks(1), kts(0), kts(1), ks(0), ks(1), qside, rep, rep, rep, rep] + [ANY] * ng,
        out_specs=[whole, whole, kside, kside, kside] + [ANY] * ng,
        out_shape=[jax.ShapeDtypeStruct((T, 512), BF16), jax.ShapeDtypeStruct((T, 512), F32),
                   jax.ShapeDtypeStruct((T, 512), BF16), jax.ShapeDtypeStruct((T, 512), BF16),
                   jax.ShapeDtypeStruct((T, 512), F32)] + (_scatter_shapes(sums, scatter[1]) if scatter else []),
        scratch_shapes=[pltpu.VMEM((2, T, BLK), F32), pltpu.VMEM((2, t, BLK), F32), pltpu.VMEM((2, t, BLK), F32)]
        + ([pltpu.SemaphoreType.DMA((3 * ng,)), pltpu.SemaphoreType.DMA((3 * ng,))] if scatter else []),
        compiler_params=(pltpu.CompilerParams(dimension_semantics=("arbitrary",) * 3, vmem_limit_bytes=VMEM_LIMIT,
                                              has_side_effects=True) if scatter
                         else _cp("parallel", "arbitrary", "arbitrary")), name=name)(
            qaug, qaug, kaug, kaug, kaug_t, kaug_t, vm, vm, do, *lses, *deltas, *sums)


def _fox_mask(i, j, t):
    row = i * t + lax.broadcasted_iota(jnp.int32, (t, t), 0)
    col = j * t + lax.broadcasted_iota(jnp.int32, (t, t), 1)
    return (col <= row) & (col >= NPAD)


def _lane_pick(x, lane, idx):
    return jnp.sum(jnp.where(lane == idx, x, 0.0), axis=-1, keepdims=True)


def _lru_gates(xc, wr_ref, wi_ref, vec_ref):
    xb = xc.astype(BF16)
    pre_r = jnp.concatenate([_dot(xb[:, p * BLK:(p + 1) * BLK], wr_ref[p], 1, 0) for p in range(LW // BLK)], axis=1)
    pre_i = jnp.concatenate([_dot(xb[:, p * BLK:(p + 1) * BLK], wi_ref[p], 1, 0) for p in range(LW // BLK)], axis=1)
    r = _sigmoid(pre_r + vec_ref[0:1, :])
    gi = _sigmoid(pre_i + vec_ref[1:2, :])
    log_a = LRU_C * r * _log_sigmoid(vec_ref[2:3, :])
    a = jnp.exp(log_a)
    mult = jnp.sqrt(_neg_expm1(2.0 * log_a))
    return r, gi, a, mult


def _conv(xbuf_ref, x, cw_ref, vec_ref, tr):
    return (cw_ref[3:4, :] * x + cw_ref[2:3, :] * xbuf_ref[7:7 + tr, :] + cw_ref[1:2, :] * xbuf_ref[6:6 + tr, :]
            + cw_ref[0:1, :] * xbuf_ref[5:5 + tr, :] + vec_ref[3:4, :])


def _lru_fwd(proj, cw, wr, wi, vec, name):
    T = proj.shape[0]
    tr = _rt(T)

    def body(x_ref, y_ref, cw_ref, wr_ref, wi_ref, vec_ref, oc_ref, hs_ref, xbuf, abuf, bbuf, hcar):
        i = pl.program_id(0)

        @pl.when(i == 0)
        def _():
            xbuf[0:8, :] = jnp.zeros((8, LW), F32)
            hcar[...] = jnp.zeros_like(hcar)

        x = x_ref[...]
        xbuf[8:8 + tr, :] = x
        xc = _conv(xbuf, x, cw_ref, vec_ref, tr)
        xbuf[0:8, :] = x[tr - 8:tr, :]
        _, gi, a, mult = _lru_gates(xc, wr_ref, wi_ref, vec_ref)
        rows = i * tr + lax.broadcasted_iota(jnp.int32, (tr, 1), 0)
        abuf[...] = a
        bbuf[...] = jnp.where(rows >= NPAD, mult * (gi * xc), 0.0)
        sub = lax.broadcasted_iota(jnp.int32, (8, 1), 0)

        def step(k, h):
            sl = pl.ds(pl.multiple_of(k * 8, 8), 8)
            a8, b8 = abuf[sl, :], bbuf[sl, :]
            for s in (1, 2, 4):
                ok = sub >= s
                b8 = jnp.where(ok, a8 * pltpu.roll(b8, s, 0) + b8, b8)
                a8 = jnp.where(ok, a8 * pltpu.roll(a8, s, 0), a8)
            h8 = a8 * h + b8
            bbuf[sl, :] = h8
            return h8[7:8, :]

        hcar[...] = lax.fori_loop(0, tr // 8, step, hcar[...])
        hs = bbuf[...]
        hs_ref[...] = hs
        oc_ref[...] = (hs * _gelu(y_ref[...])).astype(BF16)

    row = pl.BlockSpec((tr, LW), lambda i: (i, 0))
    full = lambda shape: pl.BlockSpec(shape, lambda i: (0,) * len(shape))
    return pl.pallas_call(
        body, grid=(T // tr,),
        in_specs=[pl.BlockSpec((tr, LW), lambda i: (i, XC // LW)), pl.BlockSpec((tr, LW), lambda i: (i, YC // LW)),
                  full((4, LW)), full((4, BLK, BLK)), full((4, BLK, BLK)), full((8, LW))],
        out_specs=[row, row], out_shape=[jax.ShapeDtypeStruct((T, LW), BF16), jax.ShapeDtypeStruct((T, LW), F32)],
        scratch_shapes=[pltpu.VMEM((tr + 8, LW), F32), pltpu.VMEM((tr, LW), F32), pltpu.VMEM((tr, LW), F32),
                        pltpu.VMEM((1, LW), F32)],
        compiler_params=_cp("arbitrary"), name=name)(proj, proj, cw, wr, wi, vec)


def _lru_bwd(proj, hs, doc, cw, wr, wi, vec, name):
    T = proj.shape[0]
    tr = _rt(T)
    nt = T // tr
    r8 = tr // 8

    def body(x_ref, xp_ref, y_ref, hs_ref, hp_ref, do_ref, cw_ref, wr_ref, wi_ref, vec_ref,
             dx_ref, dy_ref, dwr_ref, dwi_ref, dvec_ref, xbuf, abuf, gbuf, hbuf, dbuf, gcar, acar):
        k = pl.program_id(0)
        i = nt - 1 - k

        @pl.when(k == 0)
        def _():
            dwr_ref[...] = jnp.zeros_like(dwr_ref)
            dwi_ref[...] = jnp.zeros_like(dwi_ref)
            dvec_ref[...] = jnp.zeros_like(dvec_ref)
            gcar[...] = jnp.zeros_like(gcar)
            acar[...] = jnp.zeros_like(acar)
            dbuf[tr:tr + 8, :] = jnp.zeros((8, LW), F32)

        first = i == 0
        x = x_ref[...]
        xbuf[0:8, :] = jnp.where(first, 0.0, xp_ref[...])
        xbuf[8:8 + tr, :] = x
        xc = _conv(xbuf, x, cw_ref, vec_ref, tr)
        r, gi, a, mult = _lru_gates(xc, wr_ref, wi_ref, vec_ref)
        y = y_ref[...]
        hs = hs_ref[...]
        do_ = do_ref[...].astype(F32)
        rows = i * tr + lax.broadcasted_iota(jnp.int32, (tr, 1), 0)
        abuf[0:tr, :] = a
        abuf[tr:tr + 8, :] = jnp.zeros((8, LW), F32) + acar[...]
        an = abuf[1:1 + tr, :]
        acar[...] = a[0:1, :]
        abuf[0:tr, :] = an
        gbuf[...] = do_ * _gelu(y)
        sub = lax.broadcasted_iota(jnp.int32, (8, 1), 0)

        def step(kk, g):
            sl = pl.ds(pl.multiple_of((r8 - 1 - kk) * 8, 8), 8)
            a8, b8 = abuf[sl, :], gbuf[sl, :]
            for s in (1, 2, 4):
                ok = sub < 8 - s
                b8 = jnp.where(ok, a8 * pltpu.roll(b8, 8 - s, 0) + b8, b8)
                a8 = jnp.where(ok, a8 * pltpu.roll(a8, 8 - s, 0), a8)
            g8 = a8 * g + b8
            gbuf[sl, :] = g8
            return g8[0:1, :]

        gcar[...] = lax.fori_loop(0, r8, step, gcar[...])
        g = gbuf[...]
        hbuf[0:8, :] = jnp.where(first, 0.0, hp_ref[...])
        hbuf[8:8 + tr, :] = hs
        hprev = hbuf[7:7 + tr, :]
        dinp = jnp.where(rows >= NPAD, g, 0.0)
        da = g * hprev
        dmult = dinp * gi * xc
        dgi = dinp * mult * xc
        dxc = dinp * mult * gi
        dlog_a = da * a - dmult * a * a / mult
        ls = _log_sigmoid(vec_ref[2:3, :])
        dpre_r = dlog_a * (LRU_C * ls) * r * (1.0 - r)
        dpre_i = dgi * gi * (1.0 - gi)
        xb = xc.astype(BF16)
        rb, ib = dpre_r.astype(BF16), dpre_i.astype(BF16)
        back = []
        for p in range(LW // BLK):
            c = slice(p * BLK, (p + 1) * BLK)
            back.append(_dot(rb[:, c], wr_ref[p], 1, 1) + _dot(ib[:, c], wi_ref[p], 1, 1))
            dwr_ref[p] += _dot(xb[:, c], rb[:, c], 0, 0)
            dwi_ref[p] += _dot(xb[:, c], ib[:, c], 0, 0)
        dxc = dxc + jnp.concatenate(back, axis=1)
        col = lambda v: jnp.sum(v, axis=0, keepdims=True)
        dvec_ref[0:1, :] += col(dpre_r)
        dvec_ref[1:2, :] += col(dpre_i)
        dvec_ref[2:3, :] += col(dlog_a * (LRU_C * r)) * _sigmoid(-vec_ref[2:3, :])
        dvec_ref[3:4, :] += col(dxc)
        dvec_ref[4:5, :] += col(dxc * xbuf[5:5 + tr, :])
        dvec_ref[5:6, :] += col(dxc * xbuf[6:6 + tr, :])
        dvec_ref[6:7, :] += col(dxc * xbuf[7:7 + tr, :])
        dvec_ref[7:8, :] += col(dxc * x)
        dbuf[0:tr, :] = dxc
        dxr = (cw_ref[3:4, :] * dxc + cw_ref[2:3, :] * dbuf[1:1 + tr, :] + cw_ref[1:2, :] * dbuf[2:2 + tr, :]
               + cw_ref[0:1, :] * dbuf[3:3 + tr, :])
        dbuf[tr:tr + 8, :] = dxc[0:8, :]
        dx_ref[...] = jnp.where(rows >= NPAD, dxr, 0.0).astype(BF16)
        dy_ref[...] = (do_ * hs * _gelu_grad(y)).astype(BF16)

    rev = lambda k: nt - 1 - k
    row = lambda col0: pl.BlockSpec((tr, LW), lambda k: (rev(k), col0))
    prev8 = lambda col0: pl.BlockSpec((8, LW), lambda k: (jnp.maximum(rev(k) * r8 - 1, 0), col0))
    full = lambda shape: pl.BlockSpec(shape, lambda k: (0,) * len(shape))
    return pl.pallas_call(
        body, grid=(nt,),
        in_specs=[row(XC // LW), prev8(XC // LW), row(YC // LW), row(0), prev8(0), row(0),
                  full((4, LW)), full((4, BLK, BLK)), full((4, BLK, BLK)), full((8, LW))],
        out_specs=[row(0), row(0), full((4, BLK, BLK)), full((4, BLK, BLK)), full((8, LW))],
        out_shape=[jax.ShapeDtypeStruct((T, LW), BF16), jax.ShapeDtypeStruct((T, LW), BF16),
                   jax.ShapeDtypeStruct((4, BLK, BLK), F32), jax.ShapeDtypeStruct((4, BLK, BLK), F32),
                   jax.ShapeDtypeStruct((8, LW), F32)],
        scratch_shapes=[pltpu.VMEM((tr + 8, LW), F32), pltpu.VMEM((tr + 8, LW), F32), pltpu.VMEM((tr, LW), F32),
                        pltpu.VMEM((tr + 8, LW), F32), pltpu.VMEM((tr + 8, LW), F32),
                        pltpu.VMEM((1, LW), F32), pltpu.VMEM((1, LW), F32)],
        compiler_params=_cp("arbitrary"), name=name)(proj, proj, proj, hs, hs, doc, cw, wr, wi, vec)


def _branch_merge_fwd(oa, of, oc, wb, proj, name):
    T = proj.shape[0]
    tm, tn = _rt(T), 512

    def body(a0, a1, a2, w_ref, g0, g1, g2, r0, r1, r2, m_ref):
        acc = None
        for g, (a_ref, g_ref, r_ref) in enumerate(((a0, g0, r0), (a1, g1, r1), (a2, g2, r2))):
            b = _dot(a_ref[...], w_ref[g], 1, 0)
            r_ref[...] = b
            term = _sigmoid(g_ref[...]) * b
            acc = term if acc is None else acc + term
        m_ref[...] = acc.astype(BF16)

    act = pl.BlockSpec((tm, LW), lambda j, i: (i, 0))
    gate = lambda g: pl.BlockSpec((tm, tn), lambda j, i: (i, (GT + g * D) // tn + j))
    blk = pl.BlockSpec((tm, tn), lambda j, i: (i, j))
    return pl.pallas_call(
        body, grid=(D // tn, T // tm),
        in_specs=[act, act, act, pl.BlockSpec((3, LW, tn), lambda j, i: (0, 0, j)), gate(0), gate(1), gate(2)],
        out_specs=[blk] * 4,
        out_shape=[jax.ShapeDtypeStruct((T, D), F32)] * 3 + [jax.ShapeDtypeStruct((T, D), BF16)],
        compiler_params=_cp("parallel", "parallel"), name=name)(oa, of, oc, wb, proj, proj, proj)


def _out_dx_merge_bwd(dhb, w_out, proj, b0, b1, b2, name):
    T = proj.shape[0]
    tm, tn = _rt(T), 512

    def body(dh_ref, w_ref, g0, g1, g2, r0, r1, r2, d0, d1, d2, e0, e1, e2):
        dmv = _dot(dh_ref[...], w_ref[...], 1, 1)
        for g_ref, r_ref, d_ref, e_ref in ((g0, r0, d0, e0), (g1, r1, d1, e1), (g2, r2, d2, e2)):
            sg = _sigmoid(g_ref[...])
            d_ref[...] = (dmv * sg).astype(BF16)
            e_ref[...] = (dmv * r_ref[...] * sg * (1.0 - sg)).astype(BF16)

    gate = lambda g: pl.BlockSpec((tm, tn), lambda j, i: (i, (GT + g * D) // tn + j))
    blk = pl.BlockSpec((tm, tn), lambda j, i: (i, j))
    return pl.pallas_call(
        body, grid=(D // tn, T // tm),
        in_specs=[pl.BlockSpec((tm, D), lambda j, i: (i, 0)), pl.BlockSpec((tn, D), lambda j, i: (j, 0)),
                  gate(0), gate(1), gate(2), blk, blk, blk],
        out_specs=[blk] * 6, out_shape=[jax.ShapeDtypeStruct((T, D), BF16)] * 6,
        compiler_params=_cp("parallel", "parallel"), name=name)(dhb, w_out, proj, proj, proj, b0, b1, b2)


def _ffn_in_swiglu_fwd(u, w, name):
    T = u.shape[0]
    tm, tn = _rt(T), _pick(DFF, (1408, 256))
    nj = DFF // tn

    def body(u_ref, wg_ref, wu_ref, g_ref, up_ref, a_ref):
        ub = u_ref[...]
        g = _dot(ub, wg_ref[...], 1, 0)
        up = _dot(ub, wu_ref[...], 1, 0)
        g_ref[...] = g
        up_ref[...] = up
        a_ref[...] = (g * _sigmoid(g) * up).astype(BF16)

    blk = pl.BlockSpec((tm, tn), lambda j, i: (i, j))
    return pl.pallas_call(
        body, grid=(nj, T // tm),
        in_specs=[pl.BlockSpec((tm, D), lambda j, i: (i, 0)), pl.BlockSpec((D, tn), lambda j, i: (0, j)),
                  pl.BlockSpec((D, tn), lambda j, i: (0, j + nj))],
        out_specs=[blk] * 3,
        out_shape=[jax.ShapeDtypeStruct((T, DFF), F32)] * 2 + [jax.ShapeDtypeStruct((T, DFF), BF16)],
        compiler_params=_cp("parallel", "parallel"), name=name)(u, w, w)


def _ffn_out_dx_swiglu_bwd(dhb, w, gate, up, name):
    T = dhb.shape[0]
    tm, tn = _rt(T), _pick(DFF, (1408, 256))

    def body(dh_ref, w_ref, g_ref, up_ref, dg_ref, du_ref):
        d = _dot(dh_ref[...], w_ref[...], 1, 1)
        g = g_ref[...]
        sg = _sigmoid(g)
        dg_ref[...] = (d * up_ref[...] * (sg + g * sg * (1.0 - sg))).astype(BF16)
        du_ref[...] = (d * g * sg).astype(BF16)

    blk = pl.BlockSpec((tm, tn), lambda j, i: (i, j))
    return pl.pallas_call(
        body, grid=(DFF // tn, T // tm),
        in_specs=[pl.BlockSpec((tm, D), lambda j, i: (i, 0)), pl.BlockSpec((tn, D), lambda j, i: (j, 0)), blk, blk],
        out_specs=[blk] * 2, out_shape=[jax.ShapeDtypeStruct((T, DFF), BF16)] * 2,
        compiler_params=_cp("parallel", "parallel"), name=name)(dhb, w, gate, up)


def _adamw(w, g, m, v, name):
    R, C = w.shape
    tr = _pick(R, tuple(t for t in (512, 256, 128, 64, 32, 16, 8) if t * C * 4 <= (3 << 19)))
    c1 = 1.0 - ADAM_B1 ** ADAM_STEP
    c2 = 1.0 - ADAM_B2 ** ADAM_STEP

    def body(w_ref, g_ref, m_ref, v_ref, d_ref, mo_ref, vo_ref):
        gv = g_ref[...]
        mn = ADAM_B1 * m_ref[...] + (1.0 - ADAM_B1) * gv
        vn = ADAM_B2 * v_ref[...] + (1.0 - ADAM_B2) * (gv * gv)
        d_ref[...] = -ADAM_LR * ((mn / c1) / (jnp.sqrt(vn / c2) + ADAM_EPS) + ADAM_WD * w_ref[...])
        mo_ref[...] = mn
        vo_ref[...] = vn

    blk = pl.BlockSpec((tr, C), lambda i: (i, 0))
    return pl.pallas_call(
        body, grid=(R // tr,), in_specs=[blk] * 4, out_specs=[blk] * 3,
        out_shape=[jax.ShapeDtypeStruct((R, C), F32)] * 3, compiler_params=_cp("parallel"), name=name)(w, g, m, v)


def _sum_lead(x, name):
    n, R, C = x.shape
    tr = _pick(R, (512, 256, 128, 64, 32, 16, 8))

    def body(x_ref, o_ref):
        acc = x_ref[0]
        for d in range(1, n):
            acc = acc + x_ref[d]
        o_ref[...] = acc

    return pl.pallas_call(
        body, grid=(R // tr,), in_specs=[pl.BlockSpec((n, tr, C), lambda i: (0, i, 0))],
        out_specs=pl.BlockSpec((tr, C), lambda i: (i, 0)), out_shape=jax.ShapeDtypeStruct((R, C), F32),
        compiler_params=_cp("parallel"), name=name)(x)


def _here():
    return lax.axis_index("x"), lax.axis_index("y"), lax.axis_index("c")


def _rcopy(src, dst, send_sems, recv_sems, k, to):
    return pltpu.make_async_remote_copy(src_ref=src, dst_ref=dst, send_sem=send_sems.at[k], recv_sem=recv_sems.at[k],
                                        device_id=to, device_id_type=MESH)


def _hbm_calls(body, args, out_shapes, n_sems, aliases, name):
    return pl.pallas_call(
        body, in_specs=[ANY] * len(args), out_specs=[ANY] * len(out_shapes), out_shape=out_shapes,
        input_output_aliases=aliases,
        scratch_shapes=[pltpu.SemaphoreType.DMA((n_sems,)), pltpu.SemaphoreType.DMA((n_sems,))],
        compiler_params=pltpu.CompilerParams(has_side_effects=True), name=name)(*args)


def _gather_plan(outs, axes, send_sems, recv_sems):
    x, y, c = _here()
    sib = (x, y, 1 - c)
    chips = [(1 - x, y), (x, 1 - y), (1 - x, 1 - y)]
    todo = [(t, k, chip) for t in range(len(outs)) for k, chip in enumerate(chips)]

    def win(t, chip, hc):
        o, ax = outs[t], axes[t]
        w = o.shape[ax] // N_SHARD
        first = (2 * chip[0] + chip[1]) * w
        if ax == 0:
            return o.at[pl.ds(pl.multiple_of(first + hc * (w // 2), 16), w // 2), :]
        rows = o.shape[0] // 2
        return o.at[pl.ds(pl.multiple_of(hc * rows, 16), rows), pl.ds(pl.multiple_of(first, BLK), w)]

    def copy(t, k, chip, hc, to):
        return _rcopy(win(t, chip, hc), win(t, chip, hc), send_sems, recv_sems, 6 * t + k, to)

    def start():
        for t, k, chip in todo:
            copy(t, k, (x, y), c, (*chip, c)).start()

    def finish():
        for t, k, chip in todo:
            copy(t, k, chip, c, (*chip, c)).wait_recv()
            copy(t, 3 + k, chip, c, sib).start()
        for t, k, chip in todo:
            copy(t, 3 + k, chip, 1 - c, sib).wait_recv()
        for t, k, chip in todo:
            copy(t, k, (x, y), c, (*chip, c)).wait_send()
            copy(t, 3 + k, chip, c, sib).wait_send()

    return start, finish


def _all_gather_weights(fulls, axes, name):
    nt = len(fulls)

    def body(*refs):
        start, finish = _gather_plan(refs[nt:2 * nt], axes, *refs[2 * nt:])
        start()
        finish()

    return _hbm_calls(body, fulls, [jax.ShapeDtypeStruct(f.shape, f.dtype) for f in fulls], 6 * nt,
                      {t: t for t in range(nt)}, name)


def _half(ref, ax, hc):
    n = ref.shape[ax] // 2
    sl = pl.ds(pl.multiple_of(hc * n, 8), n)
    return ref.at[sl, :] if ax == 0 else ref.at[:, sl]


def _shrunk(shape, ax, by):
    shape = list(shape)
    shape[ax] //= by
    return tuple(shape)


def _swap_halves(gs, haxes, name):
    nt = len(gs)

    def body(*refs):
        ins, outs, (send_sems, recv_sems) = refs[:nt], refs[nt:2 * nt], refs[2 * nt:]
        x, y, c = _here()
        cps = [_rcopy(_half(g, ax, 1 - c), o, send_sems, recv_sems, t, (x, y, 1 - c))
               for t, (g, o, ax) in enumerate(zip(ins, outs, haxes))]
        for cp in cps:
            cp.start()
        for cp in cps:
            cp.wait()

    return _hbm_calls(body, gs, [jax.ShapeDtypeStruct(_shrunk(g.shape, ax, 2), g.dtype) for g, ax in zip(gs, haxes)],
                      nt, {}, name)


def _scatter_plan(ins, outs, saxes, send_sems, recv_sems):
    x, y, c = _here()
    chips = [(1 - x, y), (x, 1 - y), (1 - x, 1 - y)]

    def copies():
        cps = []
        for t, (s, o, ax) in enumerate(zip(ins, outs, saxes)):
            w = s.shape[ax] // N_SHARD
            for k, chip in enumerate(chips):
                first = pl.multiple_of((2 * chip[0] + chip[1]) * w, 8)
                src = s.at[pl.ds(first, w), :] if ax == 0 else s.at[:, pl.ds(first, w)]
                cps.append(_rcopy(src, o.at[k], send_sems, recv_sems, 3 * t + k, (*chip, c)))
        return cps

    def start():
        for cp in copies():
            cp.start()

    def finish():
        for cp in copies():
            cp.wait()

    return start, finish


def _scatter_shapes(sbs, saxes):
    return [jax.ShapeDtypeStruct((3,) + _shrunk(s.shape, ax, N_SHARD), s.dtype) for s, ax in zip(sbs, saxes)]


def _scatter_to_chips(sbs, saxes, name):
    nt = len(sbs)

    def body(*refs):
        start, finish = _scatter_plan(refs[:nt], refs[nt:2 * nt], saxes, *refs[2 * nt:])
        start()
        finish()

    return _hbm_calls(body, sbs, _scatter_shapes(sbs, saxes), 3 * nt, {}, name)


def _join_halves(fins, haxes, name):
    nt = len(fins)

    def body(*refs):
        outs, (send_sems, recv_sems) = refs[nt:2 * nt], refs[2 * nt:]
        x, y, c = _here()
        cps = [_rcopy(_half(o, ax, c), _half(o, ax, c), send_sems, recv_sems, t, (x, y, 1 - c))
               for t, (o, ax) in enumerate(zip(outs, haxes))]
        for cp in cps:
            cp.start()
        for t, (o, ax) in enumerate(zip(outs, haxes)):
            _rcopy(_half(o, ax, 1 - c), _half(o, ax, 1 - c), send_sems, recv_sems, t, (x, y, 1 - c)).wait_recv()
        for cp in cps:
            cp.wait_send()

    return _hbm_calls(body, fins, [jax.ShapeDtypeStruct(f.shape, f.dtype) for f in fins], nt, {t: t for t in range(nt)}, name)


def _all_gather_small(buf, name):
    def body(_, out_ref, send_sems, recv_sems):
        x, y, c = _here()
        me = 4 * x + 2 * y + c
        cps = []
        for k in range(1, 8):
            to = (x ^ ((k >> 2) & 1), y ^ ((k >> 1) & 1), c ^ (k & 1))
            peer = 4 * to[0] + 2 * to[1] + to[2]
            cps.append((_rcopy(out_ref.at[me], out_ref.at[me], send_sems, recv_sems, k - 1, to),
                        _rcopy(out_ref.at[peer], out_ref.at[peer], send_sems, recv_sems, k - 1, to)))
        for snd, _ in cps:
            snd.start()
        for _, rcv in cps:
            rcv.wait_recv()
        for snd, _ in cps:
            snd.wait_send()

    return _hbm_calls(body, [buf], [jax.ShapeDtypeStruct(buf.shape, buf.dtype)], 7, {0: 0}, name)[0]


def _place(block, n, index):
    buf = jnp.zeros((n,) + block.shape[1:], block.dtype)
    return lax.dynamic_update_slice_in_dim(buf, block, index, axis=0)


def _add_half(g, other, hax, cidx, name):
    r, cw = other.shape
    tr = _pick(r, tuple(t for t in (512, 256, 128, 64, 32, 16, 8) if t * cw * 4 <= (1 << 21)))
    nr = r // tr

    def body(c_ref, g_ref, o_ref, s_ref, sb_ref):
        s = g_ref[...] + o_ref[...]
        s_ref[...] = s
        sb_ref[...] = s.astype(BF16)

    g_map = (lambda i, c: (c[0] * nr + i, 0)) if hax == 0 else (lambda i, c: (i, c[0]))
    blk = pl.BlockSpec((tr, cw), lambda i, c: (i, 0))
    return pl.pallas_call(
        body,
        grid_spec=pltpu.PrefetchScalarGridSpec(
            num_scalar_prefetch=1, grid=(nr,), in_specs=[pl.BlockSpec((tr, cw), g_map), blk], out_specs=[blk, blk]),
        out_shape=[jax.ShapeDtypeStruct((r, cw), F32), jax.ShapeDtypeStruct((r, cw), BF16)],
        compiler_params=_cp("parallel"), name=name)(cidx, g, other)


def _add_chips(s, recv, sax, chip_idx, name):
    _, r, cw = recv.shape
    tr = _pick(r, tuple(t for t in (512, 352, 256, 128, 64, 32, 16, 8) if t * cw * 4 <= (1 << 21)))
    nr = r // tr

    def body(c_ref, s_ref, r_ref, out_ref):
        out_ref[...] = ((s_ref[...] + r_ref[0].astype(F32)) + r_ref[1].astype(F32)) + r_ref[2].astype(F32)

    s_map = (lambda i, c: (c[0] * nr + i, 0)) if sax == 0 else (lambda i, c: (i, c[0]))
    return pl.pallas_call(
        body,
        grid_spec=pltpu.PrefetchScalarGridSpec(
            num_scalar_prefetch=1, grid=(nr,),
            in_specs=[pl.BlockSpec((tr, cw), s_map), pl.BlockSpec((3, tr, cw), lambda i, c: (0, i, 0))],
            out_specs=pl.BlockSpec((tr, cw), lambda i, c: (i, 0))),
        out_shape=jax.ShapeDtypeStruct((r, cw), F32), compiler_params=_cp("parallel"), name=name)(chip_idx, s, recv)


IN_SHARD = IN_COLS // N_SHARD
IN_SLOT = INP // N_SHARD
IN_PIECES = ((0, 512, QA), (512, 640, KA), (640, 768, VA), (768, 1280, QF), (1280, 1792, KF), (1792, 2304, VF),
             (2304, 2312, FL), (2312, 2824, XC), (2824, 3336, YC), (3336, 6408, GT))


def _gathered_to_kernel_cols(w):
    parts, pos = [], 0
    for a, b, k in sorted(IN_PIECES, key=lambda p: p[2]):
        assert k == pos
        while a < b:
            j = a // IN_SHARD
            e = min(b, (j + 1) * IN_SHARD)
            g = j * IN_SLOT + a - j * IN_SHARD
            parts.append(w[..., g:g + e - a])
            pos += e - a
            a = e
    parts.append(jnp.zeros(w.shape[:-1] + (INP - pos,), w.dtype))
    return jnp.concatenate(parts, axis=-1)


def _kernel_to_gathered_cols(w):
    parts = []
    for j in range(N_SHARD):
        lo, hi = j * IN_SHARD, (j + 1) * IN_SHARD
        for a, b, k in IN_PIECES:
            s, e = max(a, lo), min(b, hi)
            if s < e:
                parts.append(w[..., k + s - a:k + e - a])
        parts.append(jnp.zeros(w.shape[:-1] + (IN_SLOT - IN_SHARD,), w.dtype))
    return jnp.concatenate(parts, axis=-1)


def _pair_blocks(w):
    z = jnp.zeros((4, 64, 64), w.dtype)
    w = w.reshape(4, 2, 64, 64)
    top = jnp.concatenate([w[:, 0], z], axis=2)
    bot = jnp.concatenate([z, w[:, 1]], axis=2)
    return jnp.concatenate([top, bot], axis=1)


def _unpair_blocks(w):
    return jnp.stack([w[:, :64, :64], w[:, 64:, 64:]], axis=1).reshape(8, 64, 64)


BIG = ("w_in", "w_branch", "w_out", "w_ffn_in", "w_ffn_out")
TINY = ("conv_w", "meta_tokens")
SMALL = ("rel_bias_table", "norm_mix", "swa_sinks", "fox_forget_bias", "conv_b", "lru_w_r", "lru_b_r", "lru_w_i",
         "lru_b_i", "lru_lambda", "norm_ffn", "norm_final")
SHARD_AXIS = {"conv_w": 2, "meta_tokens": 1}
BIG_AXIS = {"w_in": 2, "w_branch": 2, "w_out": 1, "w_ffn_in": 2, "w_ffn_out": 1}


def _pack(d, names):
    flat = jnp.concatenate([d[n].reshape(-1) for n in names])
    pad = (-flat.shape[0]) % (256 * 128)
    return jnp.concatenate([flat, jnp.zeros((pad,), F32)]).reshape(-1, 128)


def _unpack(buf, names, shapes):
    flat, out, off = buf.reshape(-1), {}, 0
    for n in names:
        sz = int(np.prod(shapes[n]))
        out[n] = flat[off:off + sz].reshape(shapes[n])
        off += sz
    return out


def _layer_layout(n, a):
    if n == "w_in":
        return _gathered_to_kernel_cols(a)
    return a.reshape(3, LW, D) if n == "w_branch" else a


def _local_step(x, tgt, W, placed=None):
    S = x.shape[0]
    T = S + BLK
    tm = _pick(T, (1408, 384, 128))
    bucket = jnp.asarray(_bucket_table())
    bias = _bias_build(W["rel_bias_table"], bucket, "bias_build")
    h = jnp.concatenate([jnp.zeros((NPAD, D), F32), W["meta_tokens"], x], axis=0)
    if placed is None:
        WL = {n: [W[n][l] for l in range(DEPTH)] for n in BIG}
    else:
        WL = {n: [W[n]] + [None] * (DEPTH - 1) for n in BIG}

    saved = []
    for l in range(DEPTH):
        sv = {"h0": h}
        u = _rms_fwd(h, W["norm_mix"][l], f"rms_mix_fwd")
        proj = _mm(u, WL["w_in"][l], tm=tm, tn=512, tk=D, name="mm_in_fwd")
        oa = _swa_fwd(proj, bias, W["swa_sinks"][l], "swa_fwd")
        fb = W["fox_forget_bias"][l].reshape(NH, 1)
        qaug, kaug, kaug_t, vm, vo = _fox_prep(proj, _cum_fwd(proj, fb, "cum_fwd"), "fox_prep")
        if placed is not None and l + 1 < DEPTH:
            of, lse0, lse1, *got = _fox_fwd(qaug, kaug_t, vo, "fox_fwd_gather", gather=(placed[l + 1], GATHER_AXES))
            for n, a in zip(BIG, got):
                WL[n][l + 1] = _layer_layout(n, a)
            lse = [lse0, lse1]
        else:
            of, *lse = _fox_fwd(qaug, kaug_t, vo, "fox_fwd")
        lru_vec = jnp.concatenate([W["lru_b_r"][l][None], W["lru_b_i"][l][None], W["lru_lambda"][l][None],
                                   W["conv_b"][l][None], jnp.zeros((4, LW), F32)], axis=0)
        oc, hs = _lru_fwd(proj, W["conv_w"][l], W["lru_w_r"][l], W["lru_w_i"][l], lru_vec, "lru_fwd")
        *bs, merged = _branch_merge_fwd(oa, of, oc, WL["w_branch"][l], proj, "branch_merge_fwd")
        h2 = _mm(merged, WL["w_out"][l], res=h, tm=tm, tn=512, tk=D, name="mm_out_fwd")
        u2 = _rms_fwd(h2, W["norm_ffn"][l], "rms_ffn_fwd")
        gate, up, act = _ffn_in_swiglu_fwd(u2, WL["w_ffn_in"][l], "ffn_in_swiglu_fwd")
        h = _mm(act, WL["w_ffn_out"][l], res=h2, tm=tm, tn=512, tk=_pick(DFF, (1408, 256)), name="mm_ffn_out_fwd")
        sv.update(u=u, proj=proj, oa=oa, of=of, oc=oc, lse=lse, hs=hs, fb=fb, qaug=qaug, kaug=kaug, kaug_t=kaug_t, vm=vm, lru_vec=lru_vec,
                  bs=bs, merged=merged, h2=h2, u2=u2, gate=gate, up=up, act=act)
        saved.append(sv)

    tgt_pad = tgt
    dh, dhb, dg_final, loss_vec = _loss_head(h, tgt_pad, W["norm_final"], "loss_head")
    loss = loss_vec[0, 0]

    small = ("norm_mix", "swa_sinks", "fox_forget_bias", "conv_w", "conv_b", "lru_w_r", "lru_b_r", "lru_w_i", "lru_b_i",
             "lru_lambda", "norm_ffn")
    G = {n: [None] * DEPTH for n in small}
    G["norm_final"] = dg_final.reshape(D)
    GW = {n: [None] * DEPTH for n in BIG}
    dist = placed is not None
    if dist:
        x_, y_, c_ = _here()
        cidx = jnp.reshape(c_, (1,)).astype(jnp.int32)
        chip = jnp.reshape(2 * x_ + y_, (1,)).astype(jnp.int32)

    def finish_layer(lp, ss, recv):
        fins = []
        for n, s, r, ax, hax in zip(BIG, ss, recv, GATHER_AXES, HALF_AXES):
            tot = _add_chips(s, r, ax, chip, "rs_add_chips_" + n)
            zero = jnp.zeros_like(tot)
            fins.append(jnp.concatenate([jnp.where(c_ == hc, tot, zero) for hc in range(2)], axis=hax))
        for n, f in zip(BIG, _join_halves(fins, HALF_AXES, "rs_join_halves")):
            GW[n][lp] = f

    pend = None
    dbias = jnp.zeros((NH, BLK, 2 * BLK), F32)
    tkT = tm
    for l in reversed(range(DEPTH)):
        sv = saved[l]
        dw = {}
        dw["w_ffn_out"] = _mm(_transpose(sv["act"], "tr_act"), dhb, tm=_pick(DFF, (1408, 256)), tn=D, tk=tkT,
                              name="mm_ffn_out_dw")
        dgate, dup = _ffn_out_dx_swiglu_bwd(dhb, WL["w_ffn_out"][l], sv["gate"], sv["up"], "ffn_out_dx_swiglu_bwd")
        u2t = _transpose(sv["u2"], "tr_u2")
        du2, buf = None, None
        for half, dpart in enumerate((dgate, dup)):
            buf = _mm(u2t, dpart, tm=D, tn=_pick(DFF, (1408, 256)), tk=tkT, slab=(buf, 0, 1),
                      col0=half * DFF, cols=2 * DFF, name="mm_ffn_in_dw")
            du2 = _mm(dpart, WL["w_ffn_in"][l], tb=True, res=du2, b_k0=half * DFF, tm=tm, tn=512,
                      tk=_pick(DFF, (1408, 256)), name="mm_ffn_in_dx")
        dw["w_ffn_in"] = buf.reshape(D, 2 * DFF)
        dh, dhb, dgn = _rms_bwd(du2, sv["h2"], W["norm_ffn"][l], dh, "rms_ffn_bwd")
        G["norm_ffn"][l] = dgn.reshape(D)
        dw["w_out"] = _mm(_transpose(sv["merged"], "tr_merged"), dhb, tm=D, tn=D, tk=tkT, name="mm_out_dw")
        db0, db1, db2, dg0, dg1, dg2 = _out_dx_merge_bwd(dhb, WL["w_out"][l], sv["proj"], *sv["bs"], "out_dx_merge_bwd")
        dos, buf = [], None
        for g, (o, db) in enumerate(zip((sv["oa"], sv["of"], sv["oc"]), (db0, db1, db2))):
            buf = _mm(_transpose(o, "tr_branch"), db, tm=LW, tn=D, tk=tkT, slab=(buf, g, 3), name="mm_branch_dw")
            dos.append(_mm(db, WL["w_branch"][l][g], tb=True, out_dtype=BF16, tm=tm, tn=LW, tk=D, name="mm_branch_dx"))
        dw["w_branch"] = buf.reshape(3 * LW, D)
        dqa, dkb, dvb, dbias, dsk = _swa_bwd(sv["proj"], bias, W["swa_sinks"][l], dos[0], dbias, "swa_bwd")
        dka, dva = _band_fold(dkb, dvb, "swa_band_fold")
        G["swa_sinks"][l] = dsk[0, :NH]
        delta = _fox_delta(dos[1], sv["of"], "fox_delta")
        fox_args = (sv["qaug"], sv["kaug"], sv["kaug_t"], sv["vm"], dos[1], sv["lse"], delta)
        if pend is not None:
            dqf, dqx, dkf, dvf, dkx, *recv = _fox_bwd(*fox_args, "fox_bwd_scatter", scatter=(pend[2], GATHER_AXES))
            finish_layer(pend[0], pend[1], recv)
            pend = None
        else:
            dqf, dqx, dkf, dvf, dkx = _fox_bwd(*fox_args, "fox_bwd")
        dfl, dfb = _cum_bwd(dqx, dkx, sv["proj"], sv["fb"], "cum_bwd")
        G["fox_forget_bias"][l] = dfb[:, 0]
        dxc, dyc, dwr, dwi, dvec = _lru_bwd(sv["proj"], sv["hs"], dos[2], W["conv_w"][l], W["lru_w_r"][l], W["lru_w_i"][l],
                                            sv["lru_vec"], "lru_bwd")
        G["lru_w_r"][l], G["lru_w_i"][l] = _unpair_blocks(dwr), _unpair_blocks(dwi)
        G["lru_b_r"][l], G["lru_b_i"][l], G["lru_lambda"][l], G["conv_b"][l] = dvec[0], dvec[1], dvec[2], dvec[3]
        G["conv_w"][l] = dvec[4:8]
        dproj = jnp.concatenate([dqa, dqf, dkf, dvf, dxc, dyc, dg0, dg1, dg2, dka, dva, dfl], axis=1)
        dw["w_in"] = _mm(_transpose(sv["u"], "tr_u"), dproj, tm=D, tn=IN_SLOT, tk=tkT, name="mm_in_dw")
        du = _mm(dproj, WL["w_in"][l], tb=True, tm=tm, tn=512, tk=_pick(INP, (1664, 512)), name="mm_in_dx")
        dh, dhb, dgn = _rms_bwd(du, sv["h0"], W["norm_mix"][l], dh, "rms_mix_bwd")
        G["norm_mix"][l] = dgn.reshape(D)
        if dist:
            gs = [dw[n] for n in BIG]
            pairs = [_add_half(g, r, hax, cidx, "rs_add_half_" + n)
                     for n, g, r, hax in zip(BIG, gs, _swap_halves(gs, HALF_AXES, "rs_swap_halves"), HALF_AXES)]
            ss, sbs = [list(t) for t in zip(*pairs)]
            ss[0], sbs[0] = _kernel_to_gathered_cols(ss[0]), _kernel_to_gathered_cols(sbs[0])
            pend = (l, ss, sbs)
        else:
            for n in BIG:
                GW[n][l] = dw[n]
    if dist:
        finish_layer(pend[0], pend[1], _scatter_to_chips(pend[2], GATHER_AXES, "rs_scatter"))

    grads = {n: (jnp.stack(v) if isinstance(v, list) else v) for n, v in G.items()}
    grads.update({n: jnp.stack(GW[n]) for n in BIG})
    grads["rel_bias_table"] = _bias_bwd(dbias, bucket, "bias_bwd")
    grads["meta_tokens"] = dh[NPAD:BLK]
    return loss, dh[BLK:], grads


NAMES = ("meta_tokens", "rel_bias_table", "norm_mix", "w_in", "swa_sinks", "fox_forget_bias", "conv_w", "conv_b",
         "lru_w_r", "lru_b_r", "lru_w_i", "lru_b_i", "lru_lambda", "w_branch", "w_out", "norm_ffn", "w_ffn_in",
         "w_ffn_out", "norm_final")


def _three_d(n, a):
    return a.reshape(DEPTH, 3 * LW, -1) if n == "w_branch" else a


GATHER_AXES = [BIG_AXIS[n] - 1 for n in BIG]
HALF_AXES = [1 - a for a in GATHER_AXES]


def _gather_weights(P):
    x, y, c = _here()
    mine, me = 2 * x + y, 4 * x + 2 * y + c
    placed = []
    for l in range(DEPTH):
        bufs = []
        for n in BIG:
            shard = _three_d(n, P[n])[l].astype(BF16)
            if n == "w_in":
                shard = jnp.pad(shard, ((0, 0), (0, IN_SLOT - IN_SHARD)))
            zero = jnp.zeros_like(shard)
            bufs.append(jnp.concatenate([jnp.where(mine == j, shard, zero) for j in range(N_SHARD)], axis=BIG_AXIS[n] - 1))
        placed.append(bufs)
    full = {n: _layer_layout(n, a) for n, a in zip(BIG, _all_gather_weights(placed[0], GATHER_AXES, "ag_weights"))}
    tiny = _all_gather_small(_place(_pack(P, TINY)[None], 8, me), "ag_tiny_weights")
    parts = [_unpack(tiny[2 * j], TINY, {n: P[n].shape for n in TINY}) for j in range(N_SHARD)]
    for n in TINY:
        full[n] = jnp.concatenate([p[n] for p in parts], axis=SHARD_AXIS[n])
    for n in SMALL:
        full[n] = P[n]
    full["lru_w_r"] = jnp.stack([_pair_blocks(P["lru_w_r"][l]) for l in range(DEPTH)]).astype(BF16)
    full["lru_w_i"] = jnp.stack([_pair_blocks(P["lru_w_i"][l]) for l in range(DEPTH)]).astype(BF16)
    return full, placed


def _reduce_grads(grads, P):
    x, y, c = _here()
    mine, me = 2 * x + y, 4 * x + 2 * y + c
    out = {n: grads[n].reshape(P[n].shape) for n in BIG if n != "w_in"}
    out["w_in"] = grads["w_in"][:, :, :IN_SHARD]
    names = SMALL + TINY
    gathered = _all_gather_small(_place(_pack(grads, names)[None], 8, me), "ag_small_grads")
    small = _unpack(_sum_lead(gathered, "sum_small_grads"), names, {n: grads[n].shape for n in names})
    for n in SMALL:
        out[n] = small[n]
    for n in TINY:
        w = P[n].shape[SHARD_AXIS[n]]
        out[n] = lax.dynamic_slice_in_dim(small[n], mine * w, w, axis=SHARD_AXIS[n])
    return out


def _update(P, Gd, M, V):
    delta, new_m, new_v = {}, {}, {}
    for n in BIG + TINY:
        shp = P[n].shape
        two = (int(np.prod(shp[:-1])), shp[-1])
        d, m, v = _adamw(P[n].reshape(two), Gd[n].reshape(two), M[n].reshape(two), V[n].reshape(two), "adamw_" + n)
        delta[n], new_m[n], new_v[n] = d.reshape(shp), m.reshape(shp), v.reshape(shp)
    shapes = {n: P[n].shape for n in SMALL}
    d, m, v = _adamw(_pack(P, SMALL), _pack(Gd, SMALL), _pack(M, SMALL), _pack(V, SMALL), "adamw_small")
    for dst, buf in ((delta, d), (new_m, m), (new_v, v)):
        dst.update(_unpack(buf, SMALL, shapes))
    return delta, new_m, new_v


def kernel(x, meta_tokens, rel_bias_table, norm_mix, w_in, swa_sinks, fox_forget_bias, conv_w, conv_b, lru_w_r, lru_b_r, lru_w_i, lru_b_i, lru_lambda, w_branch, w_out, norm_ffn, w_ffn_in, w_ffn_out, norm_final, loss_target, m_meta_tokens, m_rel_bias_table, m_norm_mix, m_w_in, m_swa_sinks, m_fox_forget_bias, m_conv_w, m_conv_b, m_lru_w_r, m_lru_b_r, m_lru_w_i, m_lru_b_i, m_lru_lambda, m_w_branch, m_w_out, m_norm_ffn, m_w_ffn_in, m_w_ffn_out, m_norm_final, v_meta_tokens, v_rel_bias_table, v_norm_mix, v_w_in, v_swa_sinks, v_fox_forget_bias, v_conv_w, v_conv_b, v_lru_w_r, v_lru_b_r, v_lru_w_i, v_lru_b_i, v_lru_lambda, v_w_branch, v_w_out, v_norm_ffn, v_w_ffn_in, v_w_ffn_out, v_norm_final):
    P = dict(zip(NAMES, (meta_tokens, rel_bias_table, norm_mix, w_in, swa_sinks, fox_forget_bias, conv_w, conv_b, lru_w_r,
                         lru_b_r, lru_w_i, lru_b_i, lru_lambda, w_branch, w_out, norm_ffn, w_ffn_in, w_ffn_out, norm_final)))
    M = dict(zip(NAMES, (m_meta_tokens, m_rel_bias_table, m_norm_mix, m_w_in, m_swa_sinks, m_fox_forget_bias, m_conv_w,
                         m_conv_b, m_lru_w_r, m_lru_b_r, m_lru_w_i, m_lru_b_i, m_lru_lambda, m_w_branch, m_w_out, m_norm_ffn,
                         m_w_ffn_in, m_w_ffn_out, m_norm_final)))
    V = dict(zip(NAMES, (v_meta_tokens, v_rel_bias_table, v_norm_mix, v_w_in, v_swa_sinks, v_fox_forget_bias, v_conv_w,
                         v_conv_b, v_lru_w_r, v_lru_b_r, v_lru_w_i, v_lru_b_i, v_lru_lambda, v_w_branch, v_w_out, v_norm_ffn,
                         v_w_ffn_in, v_w_ffn_out, v_norm_final)))
    W, placed = _gather_weights(P)
    loss_local, grad_x, grads = _local_step(x[0], loss_target[0], W, placed)
    loss = lax.psum(loss_local, ("x", "y", "c"))
    Gd = _reduce_grads(grads, P)
    delta, new_m, new_v = _update(P, Gd, M, V)
    return (loss, grad_x[None], *[Gd[n] for n in NAMES], *[delta[n] for n in NAMES],
            *[new_m[n] for n in NAMES], *[new_v[n] for n in NAMES])
```

```python
import functools
import math

import numpy as np
import jax
import jax.numpy as jnp
from jax import lax
from jax.experimental import pallas as pl
from jax.experimental.pallas import tpu as pltpu

F32, BF16 = jnp.float32, jnp.bfloat16
MESH = pl.DeviceIdType.MESH
ANY = pl.BlockSpec(memory_space=pl.ANY)
SMEM = pl.BlockSpec(memory_space=pltpu.SMEM)

D = 1024
DEPTH = 4
BLK = 128
N_META = 16
NPAD = 112
NH = 8
LW = 512
DFF = 2816
EPS = 1e-6
NEG = -1e30
SCALE = 0.125
LRU_C = 8.0
REL_BUCKETS = 32
N_SHARD = 4
QA, QF, KF, VF, XC, YC, GT, KA, VA, FL, INP = 0, 512, 1024, 1536, 2048, 2560, 3072, 6144, 6272, 6400, 6656
IN_COLS = 6408
VMEM_LIMIT = 48 * 1024 * 1024

ADAM_LR, ADAM_B1, ADAM_B2, ADAM_EPS, ADAM_WD, ADAM_STEP = 0.001, 0.9, 0.999, 1e-08, 0.01, 10


def _cp(*sem):
    return pltpu.CompilerParams(dimension_semantics=sem or None, vmem_limit_bytes=VMEM_LIMIT)


def _pick(n, prefs):
    for p in prefs:
        if n % p == 0:
            return p
    return n


def _rt(T):
    return _pick(T, (384, 128))


def _sigmoid(z):
    return 1.0 / (1.0 + jnp.exp(-z))


def _log_sigmoid(z):
    return jnp.minimum(z, 0.0) - jnp.log(1.0 + jnp.exp(-jnp.abs(z)))


def _gelu(y):
    c = math.sqrt(2.0 / math.pi)
    return 0.5 * y * (1.0 + jnp.tanh(c * (y + 0.044715 * y * y * y)))


def _gelu_grad(y):
    c = math.sqrt(2.0 / math.pi)
    t = jnp.tanh(c * (y + 0.044715 * y * y * y))
    return 0.5 * (1.0 + t) + 0.5 * y * (1.0 - t * t) * c * (1.0 + 3.0 * 0.044715 * y * y)


def _neg_expm1(z):
    series = -z * (1.0 + z * (0.5 + z * (1.0 / 6.0 + z * (1.0 / 24.0 + z * (1.0 / 120.0)))))
    return jnp.where(z > -0.1, series, 1.0 - jnp.exp(z))


def _dot(a, b, ca, cb):
    return lax.dot_general(a, b, (((ca,), (cb,)), ((), ())), preferred_element_type=F32)


def _mm(a, b, *, ta=False, tb=False, res=None, out_dtype=F32, tm, tn, tk, name, slab=None, b_k0=0, col0=0, cols=None):
    M, K = (a.shape[1], a.shape[0]) if ta else a.shape
    N = b.shape[0] if tb else b.shape[1]
    assert (b.shape[1] if tb else b.shape[0]) >= K + b_k0 and M % tm == 0 and N % tn == 0 and K % tk == 0, (name, a.shape, b.shape)
    assert b_k0 % tk == 0 and col0 % tn == 0
    nk, kb, jb = K // tk, b_k0 // tk, col0 // tn
    ca, cb = (0 if ta else 1), (1 if tb else 0)
    n_in = 2 + (res is not None) + (slab is not None and slab[0] is not None)

    def body(*refs):
        a_ref, b_ref = refs[:2]
        r_ref = refs[2] if res is not None else None
        o_ref = refs[n_in]
        part = _dot(a_ref[...].astype(BF16), b_ref[...].astype(BF16), ca, cb)

        def fin(acc):
            if res is not None:
                acc = acc + r_ref[...]
            o_ref[...] = acc.astype(out_dtype)

        if nk == 1:
            fin(part)
        else:
            acc_ref = refs[-1]
            k = pl.program_id(2)

            @pl.when(k == 0)
            def _():
                acc_ref[...] = part

            @pl.when(k > 0)
            def _():
                acc_ref[...] += part

            @pl.when(k == nk - 1)
            def _():
                fin(acc_ref[...])

    a_spec = pl.BlockSpec((tk, tm), lambda i, j, k: (k, i)) if ta else pl.BlockSpec((tm, tk), lambda i, j, k: (i, k))
    b_spec = (pl.BlockSpec((tn, tk), lambda i, j, k: (j, k + kb)) if tb
              else pl.BlockSpec((tk, tn), lambda i, j, k: (k + kb, j)))
    o_spec = pl.BlockSpec((tm, tn), lambda i, j, k: (i, j))
    in_specs, ops = [a_spec, b_spec], [a, b]
    if res is not None:
        in_specs.append(o_spec)
        ops.append(res)
    out_shape, aliases = jax.ShapeDtypeStruct((M, N), out_dtype), {}
    if slab is not None:
        buf, idx, n = slab
        o_spec = pl.BlockSpec((None, tm, tn), lambda i, j, k: (idx, i, j + jb))
        out_shape = jax.ShapeDtypeStruct((n, M, cols or N), out_dtype)
        if buf is not None:
            aliases = {len(ops): 0}
            in_specs.append(ANY)
            ops.append(buf)
    return pl.pallas_call(
        body, grid=(M // tm, N // tn, nk), in_specs=in_specs, out_specs=o_spec, out_shape=out_shape,
        input_output_aliases=aliases, scratch_shapes=[pltpu.VMEM((tm, tn), F32)] if nk > 1 else [],
        compiler_params=_cp("parallel", "parallel", "arbitrary"), name=name)(*ops)


def _transpose(x, name):
    T, C = x.shape
    tr, tc = _rt(T), _pick(C, (1408, 1024, 512, 256, 128))

    def body(x_ref, o_ref):
        o_ref[...] = x_ref[...].T

    return pl.pallas_call(
        body, grid=(T // tr, C // tc), in_specs=[pl.BlockSpec((tr, tc), lambda i, j: (i, j))],
        out_specs=pl.BlockSpec((tc, tr), lambda i, j: (j, i)), out_shape=jax.ShapeDtypeStruct((C, T), x.dtype),
        compiler_params=_cp("parallel", "parallel"), name=name)(x)


def _rms_fwd(h, g, name):
    T = h.shape[0]
    tr = _rt(T)

    def body(h_ref, g_ref, u_ref):
        x = h_ref[...]
        r = lax.rsqrt(jnp.mean(x * x, axis=-1, keepdims=True) + EPS)
        u_ref[...] = (x * r * g_ref[...]).astype(BF16)

    return pl.pallas_call(
        body, grid=(T // tr,),
        in_specs=[pl.BlockSpec((tr, D), lambda i: (i, 0)), pl.BlockSpec((1, D), lambda i: (0, 0))],
        out_specs=pl.BlockSpec((tr, D), lambda i: (i, 0)), out_shape=jax.ShapeDtypeStruct((T, D), BF16),
        compiler_params=_cp("parallel"), name=name)(h, g.reshape(1, D))


def _rms_bwd(du, h, g, dres, name):
    T = h.shape[0]
    tr = _rt(T)

    def body(du_ref, h_ref, g_ref, dres_ref, dh_ref, dhb_ref, dg_ref):
        x = h_ref[...]
        r = lax.rsqrt(jnp.mean(x * x, axis=-1, keepdims=True) + EPS)
        xh = x * r
        dy = du_ref[...]
        dxh = dy * g_ref[...]
        dx = r * (dxh - xh * jnp.mean(dxh * xh, axis=-1, keepdims=True))
        dh = dres_ref[...] + dx
        dh_ref[...] = dh
        dhb_ref[...] = dh.astype(BF16)
        part = jnp.sum(dy * xh, axis=0, keepdims=True)

        @pl.when(pl.program_id(0) == 0)
        def _():
            dg_ref[...] = part

        @pl.when(pl.program_id(0) > 0)
        def _():
            dg_ref[...] += part

    row = pl.BlockSpec((tr, D), lambda i: (i, 0))
    vec = pl.BlockSpec((1, D), lambda i: (0, 0))
    return pl.pallas_call(
        body, grid=(T // tr,), in_specs=[row, row, vec, row], out_specs=[row, row, vec],
        out_shape=[jax.ShapeDtypeStruct((T, D), F32), jax.ShapeDtypeStruct((T, D), BF16), jax.ShapeDtypeStruct((1, D), F32)],
        compiler_params=_cp("arbitrary"), name=name)(du, h, g.reshape(1, D), dres)


def _loss_head(h, tgt, g, name):
    T = h.shape[0]
    nb = T // BLK

    def body(h_ref, t_ref, g_ref, dh_ref, dhb_ref, dg_ref, loss_ref):
        i = pl.program_id(0)
        x = h_ref[...]
        r = lax.rsqrt(jnp.mean(x * x, axis=-1, keepdims=True) + EPS)
        xh = x * r
        gv = g_ref[...]
        tok = i >= 1
        err = jnp.where(tok, xh * gv - t_ref[...], 0.0)
        dy = err * (1.0 / D)
        dxh = dy * gv
        dx = r * (dxh - xh * jnp.mean(dxh * xh, axis=-1, keepdims=True))
        dh_ref[...] = dx
        dhb_ref[...] = dx.astype(BF16)
        dg = jnp.sum(dy * xh, axis=0, keepdims=True)
        ls = jnp.zeros((1, BLK), F32) + jnp.sum(err * err) * (0.5 / D)

        @pl.when(i == 0)
        def _():
            dg_ref[...] = dg
            loss_ref[...] = ls

        @pl.when(i > 0)
        def _():
            dg_ref[...] += dg
            loss_ref[...] += ls

    row = pl.BlockSpec((BLK, D), lambda i: (i, 0))
    vec = pl.BlockSpec((1, D), lambda i: (0, 0))
    return pl.pallas_call(
        body, grid=(nb,),
        in_specs=[row, pl.BlockSpec((BLK, D), lambda i: (jnp.maximum(i - 1, 0), 0)), vec],
        out_specs=[row, row, vec, pl.BlockSpec((1, BLK), lambda i: (0, 0))],
        out_shape=[jax.ShapeDtypeStruct((T, D), F32), jax.ShapeDtypeStruct((T, D), BF16),
                   jax.ShapeDtypeStruct((1, D), F32), jax.ShapeDtypeStruct((1, BLK), F32)],
        compiler_params=_cp("arbitrary"), name=name)(h, tgt, g.reshape(1, D))


def _bucket_table():
    q = np.arange(BLK)[:, None]
    k = np.arange(2 * BLK)[None, :]
    d = np.maximum(q + BLK - k, 0)
    max_exact = REL_BUCKETS // 2
    scaled = np.log(np.maximum(d, 1).astype(np.float32) / np.float32(max_exact)) / np.float32(math.log(128 / max_exact))
    large = np.minimum(max_exact + (scaled.astype(np.float32) * (REL_BUCKETS - max_exact)).astype(np.int32), REL_BUCKETS - 1)
    return np.where(d < max_exact, d, large).astype(np.int32)


def _bias_build(table, bucket, name):
    def body(t_ref, bk_ref, o_ref):
        bk = bk_ref[...]
        for h in range(NH):
            acc = jnp.zeros((BLK, 2 * BLK), F32)
            for b in range(REL_BUCKETS):
                acc = jnp.where(bk == b, t_ref[b, h], acc)
            o_ref[h] = acc

    return pl.pallas_call(
        body, in_specs=[SMEM, pl.BlockSpec(memory_space=pltpu.VMEM)], out_specs=pl.BlockSpec(memory_space=pltpu.VMEM),
        out_shape=jax.ShapeDtypeStruct((NH, BLK, 2 * BLK), F32), compiler_params=_cp(), name=name)(table, bucket)


def _bias_bwd(dbias, bucket, name):
    def body(d_ref, bk_ref, o_ref):
        bk = bk_ref[...]
        for h in range(NH):
            dh = d_ref[h]
            for b in range(REL_BUCKETS):
                o_ref[b, h] = jnp.sum(jnp.where(bk == b, dh, 0.0))

    return pl.pallas_call(
        body, in_specs=[pl.BlockSpec(memory_space=pltpu.VMEM)] * 2, out_specs=SMEM,
        out_shape=jax.ShapeDtypeStruct((REL_BUCKETS, NH), F32), compiler_params=_cp(), name=name)(dbias, bucket)


def _swa_specs(nq_cols):
    prev = lambda n: jnp.maximum(n - 1, 0)
    return [
        pl.BlockSpec((BLK, nq_cols), lambda n: (n, QA // nq_cols)),
        pl.BlockSpec((BLK, BLK), lambda n: (prev(n), KA // BLK)), pl.BlockSpec((BLK, BLK), lambda n: (n, KA // BLK)),
        pl.BlockSpec((BLK, BLK), lambda n: (prev(n), VA // BLK)), pl.BlockSpec((BLK, BLK), lambda n: (n, VA // BLK)),
    ]


def _swa_mask(n):
    row = lax.broadcasted_iota(jnp.int32, (BLK, 2 * BLK), 0)
    col = lax.broadcasted_iota(jnp.int32, (BLK, 2 * BLK), 1)
    dist = row + BLK - col
    return (dist >= 0) & (dist < BLK) & ((n - 1) * BLK + col >= NPAD)


def _swa_probs(qm, ksel, mask, bias_h, sink):
    s = _dot(qm, ksel, 1, 1) * SCALE
    s = jnp.where(mask, s + bias_h, NEG)
    m = jnp.maximum(jnp.max(s, axis=-1, keepdims=True), sink)
    p = jnp.exp(s - m)
    psink = jnp.exp(sink - m)
    inv = 1.0 / (jnp.sum(p, axis=-1, keepdims=True) + psink)
    return p * inv, psink * inv


def _swa_fwd(proj, bias, sinks, name):
    T = proj.shape[0]
    nb = T // BLK

    def body(sk_ref, q_ref, kp_ref, kc_ref, vp_ref, vc_ref, b_ref, o_ref):
        n = pl.program_id(0)
        lo = lax.broadcasted_iota(jnp.int32, (1, BLK), 1) < 64
        kb = jnp.concatenate([kp_ref[...], kc_ref[...]], axis=0)
        vb = jnp.concatenate([vp_ref[...], vc_ref[...]], axis=0)
        kbs = (kb.astype(BF16), pltpu.roll(kb, 64, 1).astype(BF16))
        vbs = (vb, pltpu.roll(vb, 64, 1))
        mask = _swa_mask(n)
        outs = []
        for pr in range(NH // 2):
            qp = q_ref[:, pr * BLK:(pr + 1) * BLK]
            kv = pr // 2
            acc = jnp.zeros((BLK, BLK), F32)
            for e in range(2):
                lm = lo if e == 0 else jnp.logical_not(lo)
                sw = 0 if kv == e else 1
                qm = jnp.where(lm, qp, 0.0).astype(BF16)
                pn, _ = _swa_probs(qm, kbs[sw], mask, b_ref[2 * pr + e], sk_ref[2 * pr + e])
                acc = acc + _dot(pn.astype(BF16), jnp.where(lm, vbs[sw], 0.0).astype(BF16), 1, 0)
            outs.append(acc)
        o_ref[...] = jnp.concatenate(outs, axis=1).astype(BF16)

    return pl.pallas_call(
        body, grid=(nb,),
        in_specs=[SMEM] + _swa_specs(512) + [pl.BlockSpec((NH, BLK, 2 * BLK), lambda n: (0, 0, 0))],
        out_specs=pl.BlockSpec((BLK, 512), lambda n: (n, 0)), out_shape=jax.ShapeDtypeStruct((T, 512), BF16),
        compiler_params=_cp("parallel"), name=name)(sinks, proj, proj, proj, proj, proj, bias)


def _swa_bwd(proj, bias, sinks, do, dbias_in, name):
    T = proj.shape[0]
    nb = T // BLK

    def body(sk_ref, q_ref, kp_ref, kc_ref, vp_ref, vc_ref, b_ref, do_ref, dbi_ref,
             dq_ref, dk_ref, dv_ref, db_ref, dsk_ref, sk_acc):
        n = pl.program_id(0)
        lane = lax.broadcasted_iota(jnp.int32, (1, BLK), 1)
        lo = lane < 64
        kb = jnp.concatenate([kp_ref[...], kc_ref[...]], axis=0)
        vb = jnp.concatenate([vp_ref[...], vc_ref[...]], axis=0)
        kbs = (kb, pltpu.roll(kb, 64, 1))
        vbs = (vb, pltpu.roll(vb, 64, 1))
        mask = _swa_mask(n)

        @pl.when(n == 0)
        def _():
            db_ref[...] = dbi_ref[...]
            sk_acc[...] = jnp.zeros_like(sk_acc)

        dqs = []
        dk = jnp.zeros((2 * BLK, BLK), F32)
        dv = jnp.zeros((2 * BLK, BLK), F32)
        for pr in range(NH // 2):
            qp = q_ref[:, pr * BLK:(pr + 1) * BLK]
            dop = do_ref[:, pr * BLK:(pr + 1) * BLK].astype(F32)
            kv = pr // 2
            dq = jnp.zeros((BLK, BLK), F32)
            for e in range(2):
                h = 2 * pr + e
                lm = lo if e == 0 else jnp.logical_not(lo)
                sw = 0 if kv == e else 1
                qm = jnp.where(lm, qp, 0.0)
                dom = jnp.where(lm, dop, 0.0)
                pn, ps = _swa_probs(qm.astype(BF16), kbs[sw].astype(BF16), mask, b_ref[h], sk_ref[h])
                dp = _dot(dom.astype(BF16), vbs[sw].astype(BF16), 1, 1)
                delta = jnp.sum(pn * dp, axis=-1, keepdims=True)
                ds = pn * (dp - delta)
                db_ref[h] += ds
                sk_acc[...] += jnp.where(lane == h, -(ps * delta), 0.0)
                dsb = (ds * SCALE).astype(BF16)
                dq = dq + _dot(dsb, jnp.where(lm, kbs[sw], 0.0).astype(BF16), 1, 0)
                qk = qm if sw == 0 else pltpu.roll(qm, 64, 1)
                dok = dom if sw == 0 else pltpu.roll(dom, 64, 1)
                dk = dk + _dot(dsb, qk.astype(BF16), 0, 0)
                dv = dv + _dot(pn.astype(BF16), dok.astype(BF16), 0, 0)
            dqs.append(dq)
        dq_ref[...] = jnp.concatenate(dqs, axis=1).astype(BF16)
        dk_ref[0] = dk
        dv_ref[0] = dv

        @pl.when(n == nb - 1)
        def _():
            dsk_ref[...] = jnp.sum(sk_acc[...], axis=0, keepdims=True)

    full_b = pl.BlockSpec((NH, BLK, 2 * BLK), lambda n: (0, 0, 0))
    band = pl.BlockSpec((1, 2 * BLK, BLK), lambda n: (n, 0, 0))
    return pl.pallas_call(
        body, grid=(nb,),
        in_specs=[SMEM] + _swa_specs(512) + [full_b, pl.BlockSpec((BLK, 512), lambda n: (n, 0)), full_b],
        out_specs=[pl.BlockSpec((BLK, 512), lambda n: (n, 0)), band, band, full_b, pl.BlockSpec((1, BLK), lambda n: (0, 0))],
        out_shape=[jax.ShapeDtypeStruct((T, 512), BF16), jax.ShapeDtypeStruct((nb, 2 * BLK, BLK), F32),
                   jax.ShapeDtypeStruct((nb, 2 * BLK, BLK), F32), jax.ShapeDtypeStruct((NH, BLK, 2 * BLK), F32),
                   jax.ShapeDtypeStruct((1, BLK), F32)],
        scratch_shapes=[pltpu.VMEM((BLK, BLK), F32)],
        compiler_params=_cp("arbitrary"), name=name)(sinks, proj, proj, proj, proj, proj, bias, do, dbias_in)


def _band_fold(dkb, dvb, name):
    nb = dkb.shape[0]

    def body(ko_ref, kn_ref, vo_ref, vn_ref, dk_ref, dv_ref):
        last = pl.program_id(0) == nb - 1
        dk_ref[...] = (ko_ref[0] + jnp.where(last, 0.0, kn_ref[0])).astype(BF16)
        dv_ref[...] = (vo_ref[0] + jnp.where(last, 0.0, vn_ref[0])).astype(BF16)

    own = pl.BlockSpec((1, BLK, BLK), lambda j: (j, 1, 0))
    nxt = pl.BlockSpec((1, BLK, BLK), lambda j: (jnp.minimum(j + 1, nb - 1), 0, 0))
    out = pl.BlockSpec((BLK, BLK), lambda j: (j, 0))
    return pl.pallas_call(
        body, grid=(nb,), in_specs=[own, nxt, own, nxt], out_specs=[out, out],
        out_shape=[jax.ShapeDtypeStruct((nb * BLK, BLK), BF16)] * 2,
        compiler_params=_cp("parallel"), name=name)(dkb, dkb, dvb, dvb)


def _token_major(x, width):
    full = jnp.concatenate([x, jnp.zeros((BLK - NH, BLK), F32)], axis=0).T
    return full if width == BLK else jnp.concatenate([full, jnp.zeros((BLK, width - BLK), F32)], axis=1)


def _cum_fwd(proj, fb, name):
    T = proj.shape[0]

    def body(z_ref, fb_ref, c_ref, carry):
        b = pl.program_id(0)
        lane = lax.broadcasted_iota(jnp.int32, (NH, BLK), 1)

        @pl.when(b == 0)
        def _():
            carry[...] = jnp.zeros_like(carry)

        z = z_ref[...].T[0:NH, :] + fb_ref[...]
        x = jnp.where(b * BLK + lane >= NPAD, _log_sigmoid(z), 0.0)
        s = 1
        while s < BLK:
            x = x + jnp.where(lane >= s, pltpu.roll(x, s, 1), 0.0)
            s *= 2
        x = x + carry[...]
        carry[...] = jnp.zeros((NH, BLK), F32) + jnp.sum(jnp.where(lane == BLK - 1, x, 0.0), axis=-1, keepdims=True)
        c_ref[...] = _token_major(x, BLK)

    return pl.pallas_call(
        body, grid=(T // BLK,),
        in_specs=[pl.BlockSpec((BLK, BLK), lambda b: (b, FL // BLK)), pl.BlockSpec((NH, 1), lambda b: (0, 0))],
        out_specs=pl.BlockSpec((BLK, BLK), lambda b: (b, 0)), out_shape=jax.ShapeDtypeStruct((T, BLK), F32),
        scratch_shapes=[pltpu.VMEM((NH, BLK), F32)], compiler_params=_cp("arbitrary"), name=name)(proj, fb)


def _cum_bwd(dqx, dkx, proj, fb, name):
    T = proj.shape[0]
    nb = T // BLK

    def body(dq_ref, dk_ref, z_ref, fb_ref, dz_ref, db_ref, carry):
        k = pl.program_id(0)
        b = nb - 1 - k
        lane = lax.broadcasted_iota(jnp.int32, (NH, BLK), 1)

        @pl.when(k == 0)
        def _():
            carry[...] = jnp.zeros_like(carry)
            db_ref[...] = jnp.zeros_like(db_ref)

        def picked(ref, r_first, r_second):
            rows = []
            for p in range(NH // 2):
                t_ = ref[:, p * BLK:(p + 1) * BLK].T
                rows += [t_[r_first:r_first + 1, :], t_[r_second:r_second + 1, :]]
            return jnp.concatenate(rows, axis=0)

        x = picked(dq_ref, 64, 0) - picked(dk_ref, 67, 3)
        s = 1
        while s < BLK:
            x = x + jnp.where(lane < BLK - s, pltpu.roll(x, BLK - s, 1), 0.0)
            s *= 2
        x = x + carry[...]
        carry[...] = jnp.zeros((NH, BLK), F32) + jnp.sum(jnp.where(lane == 0, x, 0.0), axis=-1, keepdims=True)
        z = z_ref[...].T[0:NH, :] + fb_ref[...]
        dz = jnp.where(b * BLK + lane >= NPAD, x * _sigmoid(-z), 0.0)
        db_ref[...] += jnp.sum(dz, axis=-1, keepdims=True)
        dz_ref[...] = _token_major(dz, 2 * BLK).astype(BF16)

    rev = lambda k: nb - 1 - k
    wide = pl.BlockSpec((BLK, 512), lambda k: (rev(k), 0))
    return pl.pallas_call(
        body, grid=(nb,),
        in_specs=[wide, wide, pl.BlockSpec((BLK, BLK), lambda k: (rev(k), FL // BLK)), pl.BlockSpec((NH, 1), lambda k: (0, 0))],
        out_specs=[pl.BlockSpec((BLK, 2 * BLK), lambda k: (rev(k), 0)), pl.BlockSpec((NH, BLK), lambda k: (0, 0))],
        out_shape=[jax.ShapeDtypeStruct((T, 2 * BLK), BF16), jax.ShapeDtypeStruct((NH, BLK), F32)],
        scratch_shapes=[pltpu.VMEM((NH, BLK), F32)], compiler_params=_cp("arbitrary"), name=name)(dqx, dkx, proj, fb)


def _fox_prep(proj, ccol, name):
    T = proj.shape[0]
    tr = _pick(T, (1408, 384, 128))

    def body(q_ref, k_ref, v_ref, cc_ref, qa_ref, ka_ref, kt_ref, vm_ref, vo_ref):
        h = pl.program_id(1)
        lane = lax.broadcasted_iota(jnp.int32, (1, BLK), 1)
        own = (lane >> 6) == (h & 1)
        a0 = 64 * (1 - (h & 1))
        c = _lane_pick(cc_ref[...], lane, h)
        hi = c.astype(BF16).astype(F32)
        mid = (c - hi).astype(BF16).astype(F32)
        lo = (c - hi - mid).astype(BF16).astype(F32)
        ones = (lane >= a0 + 3) & (lane < a0 + 6)
        qa = jnp.where(own, q_ref[...] * SCALE, jnp.where(ones, 1.0, 0.0))
        qa = jnp.where(lane == a0, hi, jnp.where(lane == a0 + 1, mid, jnp.where(lane == a0 + 2, lo, qa)))
        ones = (lane >= a0) & (lane < a0 + 3)
        ka = jnp.where(own, k_ref[...], jnp.where(ones, 1.0, 0.0))
        ka = jnp.where(lane == a0 + 3, -hi, jnp.where(lane == a0 + 4, -mid, jnp.where(lane == a0 + 5, -lo, ka)))
        qa_ref[...] = qa.astype(BF16)
        kab = ka.astype(BF16)
        ka_ref[...] = kab
        kt_ref[...] = kab.T
        vm = jnp.where(own, v_ref[...], 0.0)
        vm_ref[...] = vm.astype(BF16)
        vo_ref[...] = jnp.where(lane == a0, 1.0, vm).astype(BF16)

    pair = lambda col0: pl.BlockSpec((tr, BLK), lambda i, h: (i, col0 // BLK + (h >> 1)))
    out = pl.BlockSpec((None, tr, BLK), lambda i, h: (h, i, 0))
    out_t = pl.BlockSpec((None, BLK, tr), lambda i, h: (h, 0, i))
    tok = jax.ShapeDtypeStruct((NH, T, BLK), BF16)
    return pl.pallas_call(
        body, grid=(T // tr, NH), in_specs=[pair(QF), pair(KF), pair(VF), pl.BlockSpec((tr, BLK), lambda i, h: (i, 0))],
        out_specs=[out, out, out_t, out, out], out_shape=[tok, tok, jax.ShapeDtypeStruct((NH, BLK, T), BF16), tok, tok],
        compiler_params=_cp("parallel", "arbitrary"), name=name)(proj, proj, proj, ccol)


def _fox_fwd(qaug, kaug_t, vo, name, gather=None):
    T = qaug.shape[1]
    t = _rt(T)
    nt = T // t
    ng = len(gather[0]) if gather else 0

    pairs = [(i, j) for i in range(nt) for j in range(i + 1)]
    i_of = jnp.asarray(np.array([p[0] for p in pairs], np.int32))
    j_of = jnp.asarray(np.array([p[1] for p in pairs], np.int32))
    ns = len(pairs)

    def body(i_ref, j_ref, q0, q1, k0, k1, v0, v1, *rest):
        o_ref, lse0_ref, lse1_ref = rest[ng:ng + 3]
        m_ref, acc_ref = rest[2 * ng + 3:2 * ng + 5]
        p_, s_ = pl.program_id(0), pl.program_id(1)
        i, j = i_ref[s_], j_ref[s_]
        lane = lax.broadcasted_iota(jnp.int32, (1, BLK), 1)
        lo = lane < 64
        if gather:
            start, finish = _gather_plan(rest[ng + 3:2 * ng + 3], gather[1], *rest[2 * ng + 5:])
            pl.when((p_ == 0) & (s_ == 0))(start)

        @pl.when(j == 0)
        def _():
            m_ref[...] = jnp.full_like(m_ref, NEG)
            acc_ref[...] = jnp.zeros_like(acc_ref)

        def step(masked):
            for e, (q_ref, k_ref, v_ref) in enumerate(((q0, k0, v0), (q1, k1, v1))):
                s = _dot(q_ref[...], k_ref[...], 1, 0)
                if masked:
                    s = jnp.where(_fox_mask(i, j, t), s, NEG)
                m_old = m_ref[e]
                m_new = jnp.maximum(m_old, jnp.max(s, axis=-1, keepdims=True))
                m_ref[e] = m_new
                pe = jnp.exp(s - jnp.concatenate([m_new] * (t // BLK), axis=1))
                acc_ref[e] = jnp.exp(m_old - m_new) * acc_ref[e] + _dot(pe.astype(BF16), v_ref[...], 1, 0)

        pl.when((j < i) & (j > 0))(lambda: step(False))
        pl.when((j == i) | ((j == 0) & (i > 0)))(lambda: step(True))

        @pl.when(j == i)
        def _():
            rows = i * t + lax.broadcasted_iota(jnp.int32, (t, 1), 0)
            l0, l1 = _lane_pick(acc_ref[0], lane, 64), _lane_pick(acc_ref[1], lane, 0)
            o = jnp.where(lo, acc_ref[0] / l0, acc_ref[1] / l1)
            o_ref[...] = jnp.where(rows >= NPAD, o, 0.0).astype(BF16)
            lse0_ref[...] = m_ref[0] + jnp.log(l0)
            lse1_ref[...] = m_ref[1] + jnp.log(l1)

        if gather:
            pl.when((p_ == NH // 2 - 1) & (s_ == ns - 1))(finish)

    qs = lambda e: pl.BlockSpec((None, t, BLK), lambda p, s, ii, jj: (2 * p + e, ii[s], 0))
    ks = lambda e: pl.BlockSpec((None, t, BLK), lambda p, s, ii, jj: (2 * p + e, jj[s], 0))
    kts = lambda e: pl.BlockSpec((None, BLK, t), lambda p, s, ii, jj: (2 * p + e, 0, jj[s]))
    rep = pl.BlockSpec((None, t, BLK), lambda p, s, ii, jj: (p, ii[s], 0))
    bufs = list(gather[0]) if gather else []
    return pl.pallas_call(
        body,
        grid_spec=pltpu.PrefetchScalarGridSpec(
            num_scalar_prefetch=2, grid=(NH // 2, ns),
            in_specs=[qs(0), qs(1), kts(0), kts(1), ks(0), ks(1)] + [ANY] * ng,
            out_specs=[pl.BlockSpec((t, BLK), lambda p, s, ii, jj: (ii[s], p)), rep, rep] + [ANY] * ng,
            scratch_shapes=[pltpu.VMEM((2, t, BLK), F32), pltpu.VMEM((2, t, BLK), F32)]
            + ([pltpu.SemaphoreType.DMA((6 * ng,)), pltpu.SemaphoreType.DMA((6 * ng,))] if gather else [])),
        out_shape=[jax.ShapeDtypeStruct((T, 512), BF16)] + [jax.ShapeDtypeStruct((NH // 2, T, BLK), F32)] * 2
        + [jax.ShapeDtypeStruct(b.shape, b.dtype) for b in bufs],
        input_output_aliases={8 + g: 3 + g for g in range(ng)},
        compiler_params=(pltpu.CompilerParams(dimension_semantics=("arbitrary",) * 2, vmem_limit_bytes=VMEM_LIMIT,
                                              has_side_effects=True) if gather
                         else _cp("parallel", "arbitrary")), name=name)(i_of, j_of, qaug, qaug, kaug_t, kaug_t, vo, vo, *bufs)


def _fox_delta(do, o, name):
    T = do.shape[0]
    tr = _rt(T)

    def body(do_ref, o_ref, d0_ref, d1_ref):
        lo = lax.broadcasted_iota(jnp.int32, (1, BLK), 1) < 64
        prod = do_ref[...].astype(F32) * o_ref[...].astype(F32)
        d0_ref[...] = jnp.zeros((tr, BLK), F32) + jnp.sum(jnp.where(lo, prod, 0.0), axis=-1, keepdims=True)
        d1_ref[...] = jnp.zeros((tr, BLK), F32) + jnp.sum(jnp.where(lo, 0.0, prod), axis=-1, keepdims=True)

    blk = pl.BlockSpec((tr, BLK), lambda i, p: (i, p))
    rep = pl.BlockSpec((None, tr, BLK), lambda i, p: (p, i, 0))
    return pl.pallas_call(
        body, grid=(T // tr, NH // 2), in_specs=[blk, blk], out_specs=[rep, rep],
        out_shape=[jax.ShapeDtypeStruct((NH // 2, T, BLK), F32)] * 2,
        compiler_params=_cp("parallel", "parallel"), name=name)(do, o)


def _fox_bwd(qaug, kaug, kaug_t, vm, do, lses, deltas, name, scatter=None):
    T = qaug.shape[1]
    t = _rt(T)
    nt = T // t
    ng = len(scatter[0]) if scatter else 0
    pairs = [(i, j) for j in range(nt) for i in range(j, nt)]
    i_of = jnp.asarray(np.array([p[0] for p in pairs], np.int32))
    j_of = jnp.asarray(np.array([p[1] for p in pairs], np.int32))
    ns = len(pairs)

    def body(i_ref, j_ref, q0, q1, k0, k1, kt0, kt1, v0, v1, do_ref, lse0, lse1, dl0, dl1, *rest):
        dq_ref, dqx_ref, dk_ref, dv_ref, dkx_ref = rest[ng:ng + 5]
        dq_acc, dk_acc, dv_acc = rest[2 * ng + 5:2 * ng + 8]
        p_, s_ = pl.program_id(0), pl.program_id(1)
        i, j = i_ref[s_], j_ref[s_]
        lane = lax.broadcasted_iota(jnp.int32, (1, BLK), 1)
        lo = lane < 64
        if scatter:
            start, finish = _scatter_plan(rest[:ng], rest[ng + 5:2 * ng + 5], scatter[1], *rest[2 * ng + 8:])
            pl.when((p_ == 0) & (s_ == 0))(start)

        @pl.when(s_ == 0)
        def _():
            dq_acc[...] = jnp.zeros_like(dq_acc)

        @pl.when(i == j)
        def _():
            dk_acc[...] = jnp.zeros_like(dk_acc)
            dv_acc[...] = jnp.zeros_like(dv_acc)

        def step(masked):
            dob = do_ref[...]
            rows = pl.ds(pl.multiple_of(i * t, t), t)
            wide = lambda ref: jnp.concatenate([ref[...]] * (t // BLK), axis=1)
            for e, (q_ref, k_ref, kt_ref, v_ref, lse_ref, dl_ref) in enumerate(
                    ((q0, k0, kt0, v0, lse0, dl0), (q1, k1, kt1, v1, lse1, dl1))):
                s = _dot(q_ref[...], kt_ref[...], 1, 0)
                if masked:
                    s = jnp.where(_fox_mask(i, j, t), s, NEG)
                pe = jnp.exp(s - wide(lse_ref))
                dp = _dot(dob, v_ref[...], 1, 1)
                ds = (pe * (dp - wide(dl_ref))).astype(BF16)
                dq_acc[e, rows, :] += _dot(ds, k_ref[...], 1, 0)
                dk_acc[e] += _dot(ds, q_ref[...], 0, 0)
                dv_acc[e] += _dot(pe.astype(BF16), dob, 0, 0)

        pl.when((i > j) & (j > 0))(lambda: step(False))
        pl.when((i == j) | ((j == 0) & (i > 0)))(lambda: step(True))

        @pl.when(i == nt - 1)
        def _():
            dk_ref[...] = jnp.where(lo, dk_acc[0], dk_acc[1]).astype(BF16)
            dv_ref[...] = jnp.where(lo, dv_acc[0], dv_acc[1]).astype(BF16)
            dkx_ref[...] = jnp.where(lo, dk_acc[1], dk_acc[0])

        @pl.when(s_ == ns - 1)
        def _():
            dq_ref[...] = (jnp.where(lo, dq_acc[0], dq_acc[1]) * SCALE).astype(BF16)
            dqx_ref[...] = jnp.where(lo, dq_acc[1], dq_acc[0])

        if scatter:
            pl.when((p_ == NH // 2 - 1) & (s_ == ns - 1))(finish)

    qs = lambda e: pl.BlockSpec((None, t, BLK), lambda p, s, ii, jj: (2 * p + e, ii[s], 0))
    ks = lambda e: pl.BlockSpec((None, t, BLK), lambda p, s, ii, jj: (2 * p + e, jj[s], 0))
    kts = lambda e: pl.BlockSpec((None, BLK, t), lambda p, s, ii, jj: (2 * p + e, 0, jj[s]))
    qside = pl.BlockSpec((t, BLK), lambda p, s, ii, jj: (ii[s], p))
    kside = pl.BlockSpec((t, BLK), lambda p, s, ii, jj: (jj[s], p))
    rep = pl.BlockSpec((None, t, BLK), lambda p, s, ii, jj: (p, ii[s], 0))
    whole = pl.BlockSpec((T, BLK), lambda p, s, ii, jj: (0, p))
    sums = list(scatter[0]) if scatter else []
    return pl.pallas_call(
        body,
        grid_spec=pltpu.PrefetchScalarGridSpec(
            num_scalar_prefetch=2, grid=(NH // 2, ns),
            in_specs=[qs(0), qs(1), ks(0), ks(1), kts(0), kts(1), ks(0), ks(1), qside, rep, rep, rep, rep] + [ANY] * ng,
            out_specs=[whole, whole, kside, kside, kside] + [ANY] * ng,
            scratch_shapes=[pltpu.VMEM((2, T, BLK), F32), pltpu.VMEM((2, t, BLK), F32), pltpu.VMEM((2, t, BLK), F32)]
            + ([pltpu.SemaphoreType.DMA((3 * ng,)), pltpu.SemaphoreType.DMA((3 * ng,))] if scatter else [])),
        out_shape=[jax.ShapeDtypeStruct((T, 512), BF16), jax.ShapeDtypeStruct((T, 512), F32),
                   jax.ShapeDtypeStruct((T, 512), BF16), jax.ShapeDtypeStruct((T, 512), BF16),
                   jax.ShapeDtypeStruct((T, 512), F32)] + (_scatter_shapes(sums, scatter[1]) if scatter else []),
        compiler_params=(pltpu.CompilerParams(dimension_semantics=("arbitrary",) * 2, vmem_limit_bytes=VMEM_LIMIT,
                                              has_side_effects=True) if scatter
                         else _cp("parallel", "arbitrary")), name=name)(
            i_of, j_of, qaug, qaug, kaug, kaug, kaug_t, kaug_t, vm, vm, do, *lses, *deltas, *sums)


def _fox_mask(i, j, t):
    row = i * t + lax.broadcasted_iota(jnp.int32, (t, t), 0)
    col = j * t + lax.broadcasted_iota(jnp.int32, (t, t), 1)
    return (col <= row) & (col >= NPAD)


def _lane_pick(x, lane, idx):
    return jnp.sum(jnp.where(lane == idx, x, 0.0), axis=-1, keepdims=True)


def _lru_gates(xc, wr_ref, wi_ref, vec_ref):
    xb = xc.astype(BF16)
    pre_r = jnp.concatenate([_dot(xb[:, p * BLK:(p + 1) * BLK], wr_ref[p], 1, 0) for p in range(LW // BLK)], axis=1)
    pre_i = jnp.concatenate([_dot(xb[:, p * BLK:(p + 1) * BLK], wi_ref[p], 1, 0) for p in range(LW // BLK)], axis=1)
    r = _sigmoid(pre_r + vec_ref[0:1, :])
    gi = _sigmoid(pre_i + vec_ref[1:2, :])
    log_a = LRU_C * r * _log_sigmoid(vec_ref[2:3, :])
    a = jnp.exp(log_a)
    mult = jnp.sqrt(_neg_expm1(2.0 * log_a))
    return r, gi, a, mult


def _conv(xbuf_ref, x, cw_ref, vec_ref, tr):
    return (cw_ref[3:4, :] * x + cw_ref[2:3, :] * xbuf_ref[7:7 + tr, :] + cw_ref[1:2, :] * xbuf_ref[6:6 + tr, :]
            + cw_ref[0:1, :] * xbuf_ref[5:5 + tr, :] + vec_ref[3:4, :])


def _lru_fwd(proj, cw, wr, wi, vec, name):
    T = proj.shape[0]
    tr = _rt(T)

    def body(x_ref, y_ref, cw_ref, wr_ref, wi_ref, vec_ref, oc_ref, hs_ref, xbuf, abuf, bbuf, hcar):
        i = pl.program_id(0)

        @pl.when(i == 0)
        def _():
            xbuf[0:8, :] = jnp.zeros((8, LW), F32)
            hcar[...] = jnp.zeros_like(hcar)

        x = x_ref[...]
        xbuf[8:8 + tr, :] = x
        xc = _conv(xbuf, x, cw_ref, vec_ref, tr)
        xbuf[0:8, :] = x[tr - 8:tr, :]
        _, gi, a, mult = _lru_gates(xc, wr_ref, wi_ref, vec_ref)
        rows = i * tr + lax.broadcasted_iota(jnp.int32, (tr, 1), 0)
        abuf[...] = a
        bbuf[...] = jnp.where(rows >= NPAD, mult * (gi * xc), 0.0)
        sub = lax.broadcasted_iota(jnp.int32, (8, 1), 0)

        def step(k, h):
            sl = pl.ds(pl.multiple_of(k * 8, 8), 8)
            a8, b8 = abuf[sl, :], bbuf[sl, :]
            for s in (1, 2, 4):
                ok = sub >= s
                b8 = jnp.where(ok, a8 * pltpu.roll(b8, s, 0) + b8, b8)
                a8 = jnp.where(ok, a8 * pltpu.roll(a8, s, 0), a8)
            h8 = a8 * h + b8
            bbuf[sl, :] = h8
            return h8[7:8, :]

        hcar[...] = lax.fori_loop(0, tr // 8, step, hcar[...])
        hs = bbuf[...]
        hs_ref[...] = hs
        oc_ref[...] = (hs * _gelu(y_ref[...])).astype(BF16)

    row = pl.BlockSpec((tr, LW), lambda i: (i, 0))
    full = lambda shape: pl.BlockSpec(shape, lambda i: (0,) * len(shape))
    return pl.pallas_call(
        body, grid=(T // tr,),
        in_specs=[pl.BlockSpec((tr, LW), lambda i: (i, XC // LW)), pl.BlockSpec((tr, LW), lambda i: (i, YC // LW)),
                  full((4, LW)), full((4, BLK, BLK)), full((4, BLK, BLK)), full((8, LW))],
        out_specs=[row, row], out_shape=[jax.ShapeDtypeStruct((T, LW), BF16), jax.ShapeDtypeStruct((T, LW), F32)],
        scratch_shapes=[pltpu.VMEM((tr + 8, LW), F32), pltpu.VMEM((tr, LW), F32), pltpu.VMEM((tr, LW), F32),
                        pltpu.VMEM((1, LW), F32)],
        compiler_params=_cp("arbitrary"), name=name)(proj, proj, cw, wr, wi, vec)


def _lru_bwd(proj, hs, doc, cw, wr, wi, vec, name):
    T = proj.shape[0]
    tr = _rt(T)
    nt = T // tr
    r8 = tr // 8

    def body(x_ref, xp_ref, y_ref, hs_ref, hp_ref, do_ref, cw_ref, wr_ref, wi_ref, vec_ref,
             dx_ref, dy_ref, dwr_ref, dwi_ref, dvec_ref, xbuf, abuf, gbuf, hbuf, dbuf, gcar, acar):
        k = pl.program_id(0)
        i = nt - 1 - k

        @pl.when(k == 0)
        def _():
            dwr_ref[...] = jnp.zeros_like(dwr_ref)
            dwi_ref[...] = jnp.zeros_like(dwi_ref)
            dvec_ref[...] = jnp.zeros_like(dvec_ref)
            gcar[...] = jnp.zeros_like(gcar)
            acar[...] = jnp.zeros_like(acar)
            dbuf[tr:tr + 8, :] = jnp.zeros((8, LW), F32)

        first = i == 0
        x = x_ref[...]
        xbuf[0:8, :] = jnp.where(first, 0.0, xp_ref[...])
        xbuf[8:8 + tr, :] = x
        xc = _conv(xbuf, x, cw_ref, vec_ref, tr)
        r, gi, a, mult = _lru_gates(xc, wr_ref, wi_ref, vec_ref)
        y = y_ref[...]
        hs = hs_ref[...]
        do_ = do_ref[...].astype(F32)
        rows = i * tr + lax.broadcasted_iota(jnp.int32, (tr, 1), 0)
        abuf[0:tr, :] = a
        abuf[tr:tr + 8, :] = jnp.zeros((8, LW), F32) + acar[...]
        an = abuf[1:1 + tr, :]
        acar[...] = a[0:1, :]
        abuf[0:tr, :] = an
        gbuf[...] = do_ * _gelu(y)
        sub = lax.broadcasted_iota(jnp.int32, (8, 1), 0)

        def step(kk, g):
            sl = pl.ds(pl.multiple_of((r8 - 1 - kk) * 8, 8), 8)
            a8, b8 = abuf[sl, :], gbuf[sl, :]
            for s in (1, 2, 4):
                ok = sub < 8 - s
                b8 = jnp.where(ok, a8 * pltpu.roll(b8, 8 - s, 0) + b8, b8)
                a8 = jnp.where(ok, a8 * pltpu.roll(a8, 8 - s, 0), a8)
            g8 = a8 * g + b8
            gbuf[sl, :] = g8
            return g8[0:1, :]

        gcar[...] = lax.fori_loop(0, r8, step, gcar[...])
        g = gbuf[...]
        hbuf[0:8, :] = jnp.where(first, 0.0, hp_ref[...])
        hbuf[8:8 + tr, :] = hs
        hprev = hbuf[7:7 + tr, :]
        dinp = jnp.where(rows >= NPAD, g, 0.0)
        da = g * hprev
        dmult = dinp * gi * xc
        dgi = dinp * mult * xc
        dxc = dinp * mult * gi
        dlog_a = da * a - dmult * a * a / mult
        ls = _log_sigmoid(vec_ref[2:3, :])
        dpre_r = dlog_a * (LRU_C * ls) * r * (1.0 - r)
        dpre_i = dgi * gi * (1.0 - gi)
        xb = xc.astype(BF16)
        rb, ib = dpre_r.astype(BF16), dpre_i.astype(BF16)
        back = []
        for p in range(LW // BLK):
            c = slice(p * BLK, (p + 1) * BLK)
            back.append(_dot(rb[:, c], wr_ref[p], 1, 1) + _dot(ib[:, c], wi_ref[p], 1, 1))
            dwr_ref[p] += _dot(xb[:, c], rb[:, c], 0, 0)
            dwi_ref[p] += _dot(xb[:, c], ib[:, c], 0, 0)
        dxc = dxc + jnp.concatenate(back, axis=1)
        col = lambda v: jnp.sum(v, axis=0, keepdims=True)
        dvec_ref[0:1, :] += col(dpre_r)
        dvec_ref[1:2, :] += col(dpre_i)
        dvec_ref[2:3, :] += col(dlog_a * (LRU_C * r)) * _sigmoid(-vec_ref[2:3, :])
        dvec_ref[3:4, :] += col(dxc)
        dvec_ref[4:5, :] += col(dxc * xbuf[5:5 + tr, :])
        dvec_ref[5:6, :] += col(dxc * xbuf[6:6 + tr, :])
        dvec_ref[6:7, :] += col(dxc * xbuf[7:7 + tr, :])
        dvec_ref[7:8, :] += col(dxc * x)
        dbuf[0:tr, :] = dxc
        dxr = (cw_ref[3:4, :] * dxc + cw_ref[2:3, :] * dbuf[1:1 + tr, :] + cw_ref[1:2, :] * dbuf[2:2 + tr, :]
               + cw_ref[0:1, :] * dbuf[3:3 + tr, :])
        dbuf[tr:tr + 8, :] = dxc[0:8, :]
        dx_ref[...] = jnp.where(rows >= NPAD, dxr, 0.0).astype(BF16)
        dy_ref[...] = (do_ * hs * _gelu_grad(y)).astype(BF16)

    rev = lambda k: nt - 1 - k
    row = lambda col0: pl.BlockSpec((tr, LW), lambda k: (rev(k), col0))
    prev8 = lambda col0: pl.BlockSpec((8, LW), lambda k: (jnp.maximum(rev(k) * r8 - 1, 0), col0))
    full = lambda shape: pl.BlockSpec(shape, lambda k: (0,) * len(shape))
    return pl.pallas_call(
        body, grid=(nt,),
        in_specs=[row(XC // LW), prev8(XC // LW), row(YC // LW), row(0), prev8(0), row(0),
                  full((4, LW)), full((4, BLK, BLK)), full((4, BLK, BLK)), full((8, LW))],
        out_specs=[row(0), row(0), full((4, BLK, BLK)), full((4, BLK, BLK)), full((8, LW))],
        out_shape=[jax.ShapeDtypeStruct((T, LW), BF16), jax.ShapeDtypeStruct((T, LW), BF16),
                   jax.ShapeDtypeStruct((4, BLK, BLK), F32), jax.ShapeDtypeStruct((4, BLK, BLK), F32),
                   jax.ShapeDtypeStruct((8, LW), F32)],
        scratch_shapes=[pltpu.VMEM((tr + 8, LW), F32), pltpu.VMEM((tr + 8, LW), F32), pltpu.VMEM((tr, LW), F32),
                        pltpu.VMEM((tr + 8, LW), F32), pltpu.VMEM((tr + 8, LW), F32),
                        pltpu.VMEM((1, LW), F32), pltpu.VMEM((1, LW), F32)],
        compiler_params=_cp("arbitrary"), name=name)(proj, proj, proj, hs, hs, doc, cw, wr, wi, vec)


def _branch_merge_fwd(oa, of, oc, wb, proj, name):
    T = proj.shape[0]
    tm, tn = _rt(T), 512

    def body(a0, a1, a2, w_ref, g0, g1, g2, r0, r1, r2, m_ref):
        acc = None
        for g, (a_ref, g_ref, r_ref) in enumerate(((a0, g0, r0), (a1, g1, r1), (a2, g2, r2))):
            b = _dot(a_ref[...], w_ref[g], 1, 0)
            r_ref[...] = b
            term = _sigmoid(g_ref[...]) * b
            acc = term if acc is None else acc + term
        m_ref[...] = acc.astype(BF16)

    act = pl.BlockSpec((tm, LW), lambda j, i: (i, 0))
    gate = lambda g: pl.BlockSpec((tm, tn), lambda j, i: (i, (GT + g * D) // tn + j))
    blk = pl.BlockSpec((tm, tn), lambda j, i: (i, j))
    return pl.pallas_call(
        body, grid=(D // tn, T // tm),
        in_specs=[act, act, act, pl.BlockSpec((3, LW, tn), lambda j, i: (0, 0, j)), gate(0), gate(1), gate(2)],
        out_specs=[blk] * 4,
        out_shape=[jax.ShapeDtypeStruct((T, D), F32)] * 3 + [jax.ShapeDtypeStruct((T, D), BF16)],
        compiler_params=_cp("parallel", "parallel"), name=name)(oa, of, oc, wb, proj, proj, proj)


def _out_dx_merge_bwd(dhb, w_out, proj, b0, b1, b2, name):
    T = proj.shape[0]
    tm, tn = _rt(T), 512

    def body(dh_ref, w_ref, g0, g1, g2, r0, r1, r2, d0, d1, d2, e0, e1, e2):
        dmv = _dot(dh_ref[...], w_ref[...], 1, 1)
        for g_ref, r_ref, d_ref, e_ref in ((g0, r0, d0, e0), (g1, r1, d1, e1), (g2, r2, d2, e2)):
            sg = _sigmoid(g_ref[...])
            d_ref[...] = (dmv * sg).astype(BF16)
            e_ref[...] = (dmv * r_ref[...] * sg * (1.0 - sg)).astype(BF16)

    gate = lambda g: pl.BlockSpec((tm, tn), lambda j, i: (i, (GT + g * D) // tn + j))
    blk = pl.BlockSpec((tm, tn), lambda j, i: (i, j))
    return pl.pallas_call(
        body, grid=(D // tn, T // tm),
        in_specs=[pl.BlockSpec((tm, D), lambda j, i: (i, 0)), pl.BlockSpec((tn, D), lambda j, i: (j, 0)),
                  gate(0), gate(1), gate(2), blk, blk, blk],
        out_specs=[blk] * 6, out_shape=[jax.ShapeDtypeStruct((T, D), BF16)] * 6,
        compiler_params=_cp("parallel", "parallel"), name=name)(dhb, w_out, proj, proj, proj, b0, b1, b2)


def _ffn_in_swiglu_fwd(u, w, name):
    T = u.shape[0]
    tm, tn = _rt(T), _pick(DFF, (1408, 256))
    nj = DFF // tn

    def body(u_ref, wg_ref, wu_ref, g_ref, up_ref, a_ref):
        ub = u_ref[...]
        g = _dot(ub, wg_ref[...], 1, 0)
        up = _dot(ub, wu_ref[...], 1, 0)
        g_ref[...] = g
        up_ref[...] = up
        a_ref[...] = (g * _sigmoid(g) * up).astype(BF16)

    blk = pl.BlockSpec((tm, tn), lambda j, i: (i, j))
    return pl.pallas_call(
        body, grid=(nj, T // tm),
        in_specs=[pl.BlockSpec((tm, D), lambda j, i: (i, 0)), pl.BlockSpec((D, tn), lambda j, i: (0, j)),
                  pl.BlockSpec((D, tn), lambda j, i: (0, j + nj))],
        out_specs=[blk] * 3,
        out_shape=[jax.ShapeDtypeStruct((T, DFF), F32)] * 2 + [jax.ShapeDtypeStruct((T, DFF), BF16)],
        compiler_params=_cp("parallel", "parallel"), name=name)(u, w, w)


def _ffn_out_dx_swiglu_bwd(dhb, w, gate, up, name):
    T = dhb.shape[0]
    tm, tn = _rt(T), _pick(DFF, (1408, 256))

    def body(dh_ref, w_ref, g_ref, up_ref, dg_ref, du_ref):
        d = _dot(dh_ref[...], w_ref[...], 1, 1)
        g = g_ref[...]
        sg = _sigmoid(g)
        dg_ref[...] = (d * up_ref[...] * (sg + g * sg * (1.0 - sg))).astype(BF16)
        du_ref[...] = (d * g * sg).astype(BF16)

    blk = pl.BlockSpec((tm, tn), lambda j, i: (i, j))
    return pl.pallas_call(
        body, grid=(DFF // tn, T // tm),
        in_specs=[pl.BlockSpec((tm, D), lambda j, i: (i, 0)), pl.BlockSpec((tn, D), lambda j, i: (j, 0)), blk, blk],
        out_specs=[blk] * 2, out_shape=[jax.ShapeDtypeStruct((T, DFF), BF16)] * 2,
        compiler_params=_cp("parallel", "parallel"), name=name)(dhb, w, gate, up)


def _adamw(w, g, m, v, name):
    R, C = w.shape
    tr = _pick(R, tuple(t for t in (512, 256, 128, 64, 32, 16, 8) if t * C * 4 <= (3 << 19)))
    c1 = 1.0 - ADAM_B1 ** ADAM_STEP
    c2 = 1.0 - ADAM_B2 ** ADAM_STEP

    def body(w_ref, g_ref, m_ref, v_ref, d_ref, mo_ref, vo_ref):
        gv = g_ref[...]
        mn = ADAM_B1 * m_ref[...] + (1.0 - ADAM_B1) * gv
        vn = ADAM_B2 * v_ref[...] + (1.0 - ADAM_B2) * (gv * gv)
        d_ref[...] = -ADAM_LR * ((mn / c1) / (jnp.sqrt(vn / c2) + ADAM_EPS) + ADAM_WD * w_ref[...])
        mo_ref[...] = mn
        vo_ref[...] = vn

    blk = pl.BlockSpec((tr, C), lambda i: (i, 0))
    return pl.pallas_call(
        body, grid=(R // tr,), in_specs=[blk] * 4, out_specs=[blk] * 3,
        out_shape=[jax.ShapeDtypeStruct((R, C), F32)] * 3, compiler_params=_cp("parallel"), name=name)(w, g, m, v)


def _sum_lead(x, name):
    n, R, C = x.shape
    tr = _pick(R, (512, 256, 128, 64, 32, 16, 8))

    def body(x_ref, o_ref):
        acc = x_ref[0]
        for d in range(1, n):
            acc = acc + x_ref[d]
        o_ref[...] = acc

    return pl.pallas_call(
        body, grid=(R // tr,), in_specs=[pl.BlockSpec((n, tr, C), lambda i: (0, i, 0))],
        out_specs=pl.BlockSpec((tr, C), lambda i: (i, 0)), out_shape=jax.ShapeDtypeStruct((R, C), F32),
        compiler_params=_cp("parallel"), name=name)(x)


def _here():
    return lax.axis_index("x"), lax.axis_index("y"), lax.axis_index("c")


def _rcopy(src, dst, send_sems, recv_sems, k, to):
    return pltpu.make_async_remote_copy(src_ref=src, dst_ref=dst, send_sem=send_sems.at[k], recv_sem=recv_sems.at[k],
                                        device_id=to, device_id_type=MESH)


def _hbm_calls(body, args, out_shapes, n_sems, aliases, name):
    return pl.pallas_call(
        body, in_specs=[ANY] * len(args), out_specs=[ANY] * len(out_shapes), out_shape=out_shapes,
        input_output_aliases=aliases,
        scratch_shapes=[pltpu.SemaphoreType.DMA((n_sems,)), pltpu.SemaphoreType.DMA((n_sems,))],
        compiler_params=pltpu.CompilerParams(has_side_effects=True), name=name)(*args)


def _gather_plan(outs, axes, send_sems, recv_sems):
    x, y, c = _here()
    sib = (x, y, 1 - c)
    chips = [(1 - x, y), (x, 1 - y), (1 - x, 1 - y)]
    todo = [(t, k, chip) for t in range(len(outs)) for k, chip in enumerate(chips)]

    def win(t, chip, hc):
        o, ax = outs[t], axes[t]
        w = o.shape[ax] // N_SHARD
        first = (2 * chip[0] + chip[1]) * w
        if ax == 0:
            return o.at[pl.ds(pl.multiple_of(first + hc * (w // 2), 16), w // 2), :]
        rows = o.shape[0] // 2
        return o.at[pl.ds(pl.multiple_of(hc * rows, 16), rows), pl.ds(pl.multiple_of(first, BLK), w)]

    def copy(t, k, chip, hc, to):
        return _rcopy(win(t, chip, hc), win(t, chip, hc), send_sems, recv_sems, 6 * t + k, to)

    def start():
        for t, k, chip in todo:
            copy(t, k, (x, y), c, (*chip, c)).start()

    def finish():
        for t, k, chip in todo:
            copy(t, k, chip, c, (*chip, c)).wait_recv()
            copy(t, 3 + k, chip, c, sib).start()
        for t, k, chip in todo:
            copy(t, 3 + k, chip, 1 - c, sib).wait_recv()
        for t, k, chip in todo:
            copy(t, k, (x, y), c, (*chip, c)).wait_send()
            copy(t, 3 + k, chip, c, sib).wait_send()

    return start, finish


def _all_gather_weights(fulls, axes, name):
    nt = len(fulls)

    def body(*refs):
        start, finish = _gather_plan(refs[nt:2 * nt], axes, *refs[2 * nt:])
        start()
        finish()

    return _hbm_calls(body, fulls, [jax.ShapeDtypeStruct(f.shape, f.dtype) for f in fulls], 6 * nt,
                      {t: t for t in range(nt)}, name)


def _half(ref, ax, hc):
    n = ref.shape[ax] // 2
    sl = pl.ds(pl.multiple_of(hc * n, 8), n)
    return ref.at[sl, :] if ax == 0 else ref.at[:, sl]


def _shrunk(shape, ax, by):
    shape = list(shape)
    shape[ax] //= by
    return tuple(shape)


def _swap_halves(gs, haxes, name):
    nt = len(gs)

    def body(*refs):
        ins, outs, (send_sems, recv_sems) = refs[:nt], refs[nt:2 * nt], refs[2 * nt:]
        x, y, c = _here()
        cps = [_rcopy(_half(g, ax, 1 - c), o, send_sems, recv_sems, t, (x, y, 1 - c))
               for t, (g, o, ax) in enumerate(zip(ins, outs, haxes))]
        for cp in cps:
            cp.start()
        for cp in cps:
            cp.wait()

    return _hbm_calls(body, gs, [jax.ShapeDtypeStruct(_shrunk(g.shape, ax, 2), g.dtype) for g, ax in zip(gs, haxes)],
                      nt, {}, name)


def _scatter_plan(ins, outs, saxes, send_sems, recv_sems):
    x, y, c = _here()
    chips = [(1 - x, y), (x, 1 - y), (1 - x, 1 - y)]

    def copies():
        cps = []
        for t, (s, o, ax) in enumerate(zip(ins, outs, saxes)):
            w = s.shape[ax] // N_SHARD
            for k, chip in enumerate(chips):
                first = pl.multiple_of((2 * chip[0] + chip[1]) * w, 8)
                src = s.at[pl.ds(first, w), :] if ax == 0 else s.at[:, pl.ds(first, w)]
                cps.append(_rcopy(src, o.at[k], send_sems, recv_sems, 3 * t + k, (*chip, c)))
        return cps

    def start():
        for cp in copies():
            cp.start()

    def finish():
        for cp in copies():
            cp.wait()

    return start, finish


def _scatter_shapes(sbs, saxes):
    return [jax.ShapeDtypeStruct((3,) + _shrunk(s.shape, ax, N_SHARD), s.dtype) for s, ax in zip(sbs, saxes)]


def _scatter_to_chips(sbs, saxes, name):
    nt = len(sbs)

    def body(*refs):
        start, finish = _scatter_plan(refs[:nt], refs[nt:2 * nt], saxes, *refs[2 * nt:])
        start()
        finish()

    return _hbm_calls(body, sbs, _scatter_shapes(sbs, saxes), 3 * nt, {}, name)


def _join_halves(fins, haxes, name):
    nt = len(fins)

    def body(*refs):
        outs, (send_sems, recv_sems) = refs[nt:2 * nt], refs[2 * nt:]
        x, y, c = _here()
        cps = [_rcopy(_half(o, ax, c), _half(o, ax, c), send_sems, recv_sems, t, (x, y, 1 - c))
               for t, (o, ax) in enumerate(zip(outs, haxes))]
        for cp in cps:
            cp.start()
        for t, (o, ax) in enumerate(zip(outs, haxes)):
            _rcopy(_half(o, ax, 1 - c), _half(o, ax, 1 - c), send_sems, recv_sems, t, (x, y, 1 - c)).wait_recv()
        for cp in cps:
            cp.wait_send()

    return _hbm_calls(body, fins, [jax.ShapeDtypeStruct(f.shape, f.dtype) for f in fins], nt, {t: t for t in range(nt)}, name)


def _all_gather_small(buf, name):
    def body(_, out_ref, send_sems, recv_sems):
        x, y, c = _here()
        me = 4 * x + 2 * y + c
        cps = []
        for k in range(1, 8):
            to = (x ^ ((k >> 2) & 1), y ^ ((k >> 1) & 1), c ^ (k & 1))
            peer = 4 * to[0] + 2 * to[1] + to[2]
            cps.append((_rcopy(out_ref.at[me], out_ref.at[me], send_sems, recv_sems, k - 1, to),
                        _rcopy(out_ref.at[peer], out_ref.at[peer], send_sems, recv_sems, k - 1, to)))
        for snd, _ in cps:
            snd.start()
        for _, rcv in cps:
            rcv.wait_recv()
        for snd, _ in cps:
            snd.wait_send()

    return _hbm_calls(body, [buf], [jax.ShapeDtypeStruct(buf.shape, buf.dtype)], 7, {0: 0}, name)[0]


def _place(block, n, index):
    buf = jnp.zeros((n,) + block.shape[1:], block.dtype)
    return lax.dynamic_update_slice_in_dim(buf, block, index, axis=0)


def _add_half(g, other, hax, cidx, name):
    r, cw = other.shape
    tr = _pick(r, tuple(t for t in (512, 256, 128, 64, 32, 16, 8) if t * cw * 4 <= (1 << 21)))
    nr = r // tr

    def body(c_ref, g_ref, o_ref, s_ref, sb_ref):
        s = g_ref[...] + o_ref[...]
        s_ref[...] = s
        sb_ref[...] = s.astype(BF16)

    g_map = (lambda i, c: (c[0] * nr + i, 0)) if hax == 0 else (lambda i, c: (i, c[0]))
    blk = pl.BlockSpec((tr, cw), lambda i, c: (i, 0))
    return pl.pallas_call(
        body,
        grid_spec=pltpu.PrefetchScalarGridSpec(
            num_scalar_prefetch=1, grid=(nr,), in_specs=[pl.BlockSpec((tr, cw), g_map), blk], out_specs=[blk, blk]),
        out_shape=[jax.ShapeDtypeStruct((r, cw), F32), jax.ShapeDtypeStruct((r, cw), BF16)],
        compiler_params=_cp("parallel"), name=name)(cidx, g, other)


def _add_chips(s, recv, sax, chip_idx, name):
    _, r, cw = recv.shape
    tr = _pick(r, tuple(t for t in (512, 352, 256, 128, 64, 32, 16, 8) if t * cw * 4 <= (1 << 21)))
    nr = r // tr

    def body(c_ref, s_ref, r_ref, out_ref):
        out_ref[...] = ((s_ref[...] + r_ref[0].astype(F32)) + r_ref[1].astype(F32)) + r_ref[2].astype(F32)

    s_map = (lambda i, c: (c[0] * nr + i, 0)) if sax == 0 else (lambda i, c: (i, c[0]))
    return pl.pallas_call(
        body,
        grid_spec=pltpu.PrefetchScalarGridSpec(
            num_scalar_prefetch=1, grid=(nr,),
            in_specs=[pl.BlockSpec((tr, cw), s_map), pl.BlockSpec((3, tr, cw), lambda i, c: (0, i, 0))],
            out_specs=pl.BlockSpec((tr, cw), lambda i, c: (i, 0))),
        out_shape=jax.ShapeDtypeStruct((r, cw), F32), compiler_params=_cp("parallel"), name=name)(chip_idx, s, recv)


IN_SHARD = IN_COLS // N_SHARD
IN_SLOT = INP // N_SHARD
IN_PIECES = ((0, 512, QA), (512, 640, KA), (640, 768, VA), (768, 1280, QF), (1280, 1792, KF), (1792, 2304, VF),
             (2304, 2312, FL), (2312, 2824, XC), (2824, 3336, YC), (3336, 6408, GT))


def _gathered_to_kernel_cols(w):
    parts, pos = [], 0
    for a, b, k in sorted(IN_PIECES, key=lambda p: p[2]):
        assert k == pos
        while a < b:
            j = a // IN_SHARD
            e = min(b, (j + 1) * IN_SHARD)
            g = j * IN_SLOT + a - j * IN_SHARD
            parts.append(w[..., g:g + e - a])
            pos += e - a
            a = e
    parts.append(jnp.zeros(w.shape[:-1] + (INP - pos,), w.dtype))
    return jnp.concatenate(parts, axis=-1)


def _kernel_to_gathered_cols(w):
    parts = []
    for j in range(N_SHARD):
        lo, hi = j * IN_SHARD, (j + 1) * IN_SHARD
        for a, b, k in IN_PIECES:
            s, e = max(a, lo), min(b, hi)
            if s < e:
                parts.append(w[..., k + s - a:k + e - a])
        parts.append(jnp.zeros(w.shape[:-1] + (IN_SLOT - IN_SHARD,), w.dtype))
    return jnp.concatenate(parts, axis=-1)


def _pair_blocks(w):
    z = jnp.zeros((4, 64, 64), w.dtype)
    w = w.reshape(4, 2, 64, 64)
    top = jnp.concatenate([w[:, 0], z], axis=2)
    bot = jnp.concatenate([z, w[:, 1]], axis=2)
    return jnp.concatenate([top, bot], axis=1)


def _unpair_blocks(w):
    return jnp.stack([w[:, :64, :64], w[:, 64:, 64:]], axis=1).reshape(8, 64, 64)


BIG = ("w_in", "w_branch", "w_out", "w_ffn_in", "w_ffn_out")
TINY = ("conv_w", "meta_tokens")
SMALL = ("rel_bias_table", "norm_mix", "swa_sinks", "fox_forget_bias", "conv_b", "lru_w_r", "lru_b_r", "lru_w_i",
         "lru_b_i", "lru_lambda", "norm_ffn", "norm_final")
SHARD_AXIS = {"conv_w": 2, "meta_tokens": 1}
BIG_AXIS = {"w_in": 2, "w_branch": 2, "w_out": 1, "w_ffn_in": 2, "w_ffn_out": 1}


def _pack(d, names):
    flat = jnp.concatenate([d[n].reshape(-1) for n in names])
    pad = (-flat.shape[0]) % (256 * 128)
    return jnp.concatenate([flat, jnp.zeros((pad,), F32)]).reshape(-1, 128)


def _unpack(buf, names, shapes):
    flat, out, off = buf.reshape(-1), {}, 0
    for n in names:
        sz = int(np.prod(shapes[n]))
        out[n] = flat[off:off + sz].reshape(shapes[n])
        off += sz
    return out


def _layer_layout(n, a):
    if n == "w_in":
        return _gathered_to_kernel_cols(a)
    return a.reshape(3, LW, D) if n == "w_branch" else a


def _local_step(x, tgt, W, placed=None):
    S = x.shape[0]
    T = S + BLK
    tm = _pick(T, (1408, 384, 128))
    bucket = jnp.asarray(_bucket_table())
    bias = _bias_build(W["rel_bias_table"], bucket, "bias_build")
    h = jnp.concatenate([jnp.zeros((NPAD, D), F32), W["meta_tokens"], x], axis=0)
    if placed is None:
        WL = {n: [W[n][l] for l in range(DEPTH)] for n in BIG}
    else:
        WL = {n: [W[n]] + [None] * (DEPTH - 1) for n in BIG}

    saved = []
    for l in range(DEPTH):
        sv = {"h0": h}
        u = _rms_fwd(h, W["norm_mix"][l], f"rms_mix_fwd")
        proj = _mm(u, WL["w_in"][l], tm=tm, tn=512, tk=D, name="mm_in_fwd")
        oa = _swa_fwd(proj, bias, W["swa_sinks"][l], "swa_fwd")
        fb = W["fox_forget_bias"][l].reshape(NH, 1)
        qaug, kaug, kaug_t, vm, vo = _fox_prep(proj, _cum_fwd(proj, fb, "cum_fwd"), "fox_prep")
        if placed is not None and l + 1 < DEPTH:
            of, lse0, lse1, *got = _fox_fwd(qaug, kaug_t, vo, "fox_fwd_gather", gather=(placed[l + 1], GATHER_AXES))
            for n, a in zip(BIG, got):
                WL[n][l + 1] = _layer_layout(n, a)
            lse = [lse0, lse1]
        else:
            of, *lse = _fox_fwd(qaug, kaug_t, vo, "fox_fwd")
        lru_vec = jnp.concatenate([W["lru_b_r"][l][None], W["lru_b_i"][l][None], W["lru_lambda"][l][None],
                                   W["conv_b"][l][None], jnp.zeros((4, LW), F32)], axis=0)
        oc, hs = _lru_fwd(proj, W["conv_w"][l], W["lru_w_r"][l], W["lru_w_i"][l], lru_vec, "lru_fwd")
        *bs, merged = _branch_merge_fwd(oa, of, oc, WL["w_branch"][l], proj, "branch_merge_fwd")
        h2 = _mm(merged, WL["w_out"][l], res=h, tm=tm, tn=512, tk=D, name="mm_out_fwd")
        u2 = _rms_fwd(h2, W["norm_ffn"][l], "rms_ffn_fwd")
        gate, up, act = _ffn_in_swiglu_fwd(u2, WL["w_ffn_in"][l], "ffn_in_swiglu_fwd")
        h = _mm(act, WL["w_ffn_out"][l], res=h2, tm=tm, tn=512, tk=_pick(DFF, (1408, 256)), name="mm_ffn_out_fwd")
        sv.update(u=u, proj=proj, oa=oa, of=of, oc=oc, lse=lse, hs=hs, fb=fb, qaug=qaug, kaug=kaug, kaug_t=kaug_t, vm=vm, lru_vec=lru_vec,
                  bs=bs, merged=merged, h2=h2, u2=u2, gate=gate, up=up, act=act)
        saved.append(sv)

    tgt_pad = tgt
    dh, dhb, dg_final, loss_vec = _loss_head(h, tgt_pad, W["norm_final"], "loss_head")
    loss = loss_vec[0, 0]

    small = ("norm_mix", "swa_sinks", "fox_forget_bias", "conv_w", "conv_b", "lru_w_r", "lru_b_r", "lru_w_i", "lru_b_i",
             "lru_lambda", "norm_ffn")
    G = {n: [None] * DEPTH for n in small}
    G["norm_final"] = dg_final.reshape(D)
    GW = {n: [None] * DEPTH for n in BIG}
    dist = placed is not None
    if dist:
        x_, y_, c_ = _here()
        cidx = jnp.reshape(c_, (1,)).astype(jnp.int32)
        chip = jnp.reshape(2 * x_ + y_, (1,)).astype(jnp.int32)

    def finish_layer(lp, ss, recv):
        fins = []
        for n, s, r, ax, hax in zip(BIG, ss, recv, GATHER_AXES, HALF_AXES):
            tot = _add_chips(s, r, ax, chip, "rs_add_chips_" + n)
            zero = jnp.zeros_like(tot)
            fins.append(jnp.concatenate([jnp.where(c_ == hc, tot, zero) for hc in range(2)], axis=hax))
        for n, f in zip(BIG, _join_halves(fins, HALF_AXES, "rs_join_halves")):
            GW[n][lp] = f

    pend = None
    dbias = jnp.zeros((NH, BLK, 2 * BLK), F32)
    tkT = tm
    for l in reversed(range(DEPTH)):
        sv = saved[l]
        dw = {}
        dw["w_ffn_out"] = _mm(_transpose(sv["act"], "tr_act"), dhb, tm=_pick(DFF, (1408, 256)), tn=D, tk=tkT,
                              name="mm_ffn_out_dw")
        dgate, dup = _ffn_out_dx_swiglu_bwd(dhb, WL["w_ffn_out"][l], sv["gate"], sv["up"], "ffn_out_dx_swiglu_bwd")
        u2t = _transpose(sv["u2"], "tr_u2")
        du2, buf = None, None
        for half, dpart in enumerate((dgate, dup)):
            buf = _mm(u2t, dpart, tm=D, tn=_pick(DFF, (1408, 256)), tk=tkT, slab=(buf, 0, 1),
                      col0=half * DFF, cols=2 * DFF, name="mm_ffn_in_dw")
            du2 = _mm(dpart, WL["w_ffn_in"][l], tb=True, res=du2, b_k0=half * DFF, tm=tm, tn=512,
                      tk=_pick(DFF, (1408, 256)), name="mm_ffn_in_dx")
        dw["w_ffn_in"] = buf.reshape(D, 2 * DFF)
        dh, dhb, dgn = _rms_bwd(du2, sv["h2"], W["norm_ffn"][l], dh, "rms_ffn_bwd")
        G["norm_ffn"][l] = dgn.reshape(D)
        dw["w_out"] = _mm(_transpose(sv["merged"], "tr_merged"), dhb, tm=D, tn=D, tk=tkT, name="mm_out_dw")
        db0, db1, db2, dg0, dg1, dg2 = _out_dx_merge_bwd(dhb, WL["w_out"][l], sv["proj"], *sv["bs"], "out_dx_merge_bwd")
        dos, buf = [], None
        for g, (o, db) in enumerate(zip((sv["oa"], sv["of"], sv["oc"]), (db0, db1, db2))):
            buf = _mm(_transpose(o, "tr_branch"), db, tm=LW, tn=D, tk=tkT, slab=(buf, g, 3), name="mm_branch_dw")
            dos.append(_mm(db, WL["w_branch"][l][g], tb=True, out_dtype=BF16, tm=tm, tn=LW, tk=D, name="mm_branch_dx"))
        dw["w_branch"] = buf.reshape(3 * LW, D)
        dqa, dkb, dvb, dbias, dsk = _swa_bwd(sv["proj"], bias, W["swa_sinks"][l], dos[0], dbias, "swa_bwd")
        dka, dva = _band_fold(dkb, dvb, "swa_band_fold")
        G["swa_sinks"][l] = dsk[0, :NH]
        delta = _fox_delta(dos[1], sv["of"], "fox_delta")
        fox_args = (sv["qaug"], sv["kaug"], sv["kaug_t"], sv["vm"], dos[1], sv["lse"], delta)
        if pend is not None:
            dqf, dqx, dkf, dvf, dkx, *recv = _fox_bwd(*fox_args, "fox_bwd_scatter", scatter=(pend[2], GATHER_AXES))
            finish_layer(pend[0], pend[1], recv)
            pend = None
        else:
            dqf, dqx, dkf, dvf, dkx = _fox_bwd(*fox_args, "fox_bwd")
        dfl, dfb = _cum_bwd(dqx, dkx, sv["proj"], sv["fb"], "cum_bwd")
        G["fox_forget_bias"][l] = dfb[:, 0]
        dxc, dyc, dwr, dwi, dvec = _lru_bwd(sv["proj"], sv["hs"], dos[2], W["conv_w"][l], W["lru_w_r"][l], W["lru_w_i"][l],
                                            sv["lru_vec"], "lru_bwd")
        G["lru_w_r"][l], G["lru_w_i"][l] = _unpair_blocks(dwr), _unpair_blocks(dwi)
        G["lru_b_r"][l], G["lru_b_i"][l], G["lru_lambda"][l], G["conv_b"][l] = dvec[0], dvec[1], dvec[2], dvec[3]
        G["conv_w"][l] = dvec[4:8]
        dproj = jnp.concatenate([dqa, dqf, dkf, dvf, dxc, dyc, dg0, dg1, dg2, dka, dva, dfl], axis=1)
        dw["w_in"] = _mm(_transpose(sv["u"], "tr_u"), dproj, tm=D, tn=IN_SLOT, tk=tkT, name="mm_in_dw")
        du = _mm(dproj, WL["w_in"][l], tb=True, tm=tm, tn=512, tk=_pick(INP, (1664, 512)), name="mm_in_dx")
        dh, dhb, dgn = _rms_bwd(du, sv["h0"], W["norm_mix"][l], dh, "rms_mix_bwd")
        G["norm_mix"][l] = dgn.reshape(D)
        if dist:
            gs = [dw[n] for n in BIG]
            pairs = [_add_half(g, r, hax, cidx, "rs_add_half_" + n)
                     for n, g, r, hax in zip(BIG, gs, _swap_halves(gs, HALF_AXES, "rs_swap_halves"), HALF_AXES)]
            ss, sbs = [list(t) for t in zip(*pairs)]
            ss[0], sbs[0] = _kernel_to_gathered_cols(ss[0]), _kernel_to_gathered_cols(sbs[0])
            pend = (l, ss, sbs)
        else:
            for n in BIG:
                GW[n][l] = dw[n]
    if dist:
        finish_layer(pend[0], pend[1], _scatter_to_chips(pend[2], GATHER_AXES, "rs_scatter"))

    grads = {n: (jnp.stack(v) if isinstance(v, list) else v) for n, v in G.items()}
    grads.update({n: jnp.stack(GW[n]) for n in BIG})
    grads["rel_bias_table"] = _bias_bwd(dbias, bucket, "bias_bwd")
    grads["meta_tokens"] = dh[NPAD:BLK]
    return loss, dh[BLK:], grads


NAMES = ("meta_tokens", "rel_bias_table", "norm_mix", "w_in", "swa_sinks", "fox_forget_bias", "conv_w", "conv_b",
         "lru_w_r", "lru_b_r", "lru_w_i", "lru_b_i", "lru_lambda", "w_branch", "w_out", "norm_ffn", "w_ffn_in",
         "w_ffn_out", "norm_final")


def _three_d(n, a):
    return a.reshape(DEPTH, 3 * LW, -1) if n == "w_branch" else a


GATHER_AXES = [BIG_AXIS[n] - 1 for n in BIG]
HALF_AXES = [1 - a for a in GATHER_AXES]


def _gather_weights(P):
    x, y, c = _here()
    mine, me = 2 * x + y, 4 * x + 2 * y + c
    placed = []
    for l in range(DEPTH):
        bufs = []
        for n in BIG:
            shard = _three_d(n, P[n])[l].astype(BF16)
            if n == "w_in":
                shard = jnp.pad(shard, ((0, 0), (0, IN_SLOT - IN_SHARD)))
            zero = jnp.zeros_like(shard)
            bufs.append(jnp.concatenate([jnp.where(mine == j, shard, zero) for j in range(N_SHARD)], axis=BIG_AXIS[n] - 1))
        placed.append(bufs)
    full = {n: _layer_layout(n, a) for n, a in zip(BIG, _all_gather_weights(placed[0], GATHER_AXES, "ag_weights"))}
    tiny = _all_gather_small(_place(_pack(P, TINY)[None], 8, me), "ag_tiny_weights")
    parts = [_unpack(tiny[2 * j], TINY, {n: P[n].shape for n in TINY}) for j in range(N_SHARD)]
    for n in TINY:
        full[n] = jnp.concatenate([p[n] for p in parts], axis=SHARD_AXIS[n])
    for n in SMALL:
        full[n] = P[n]
    full["lru_w_r"] = jnp.stack([_pair_blocks(P["lru_w_r"][l]) for l in range(DEPTH)]).astype(BF16)
    full["lru_w_i"] = jnp.stack([_pair_blocks(P["lru_w_i"][l]) for l in range(DEPTH)]).astype(BF16)
    return full, placed


def _reduce_grads(grads, P):
    x, y, c = _here()
    mine, me = 2 * x + y, 4 * x + 2 * y + c
    out = {n: grads[n].reshape(P[n].shape) for n in BIG if n != "w_in"}
    out["w_in"] = grads["w_in"][:, :, :IN_SHARD]
    names = SMALL + TINY
    gathered = _all_gather_small(_place(_pack(grads, names)[None], 8, me), "ag_small_grads")
    small = _unpack(_sum_lead(gathered, "sum_small_grads"), names, {n: grads[n].shape for n in names})
    for n in SMALL:
        out[n] = small[n]
    for n in TINY:
        w = P[n].shape[SHARD_AXIS[n]]
        out[n] = lax.dynamic_slice_in_dim(small[n], mine * w, w, axis=SHARD_AXIS[n])
    return out


def _update(P, Gd, M, V):
    delta, new_m, new_v = {}, {}, {}
    for n in BIG + TINY:
        shp = P[n].shape
        two = (int(np.prod(shp[:-1])), shp[-1])
        d, m, v = _adamw(P[n].reshape(two), Gd[n].reshape(two), M[n].reshape(two), V[n].reshape(two), "adamw_" + n)
        delta[n], new_m[n], new_v[n] = d.reshape(shp), m.reshape(shp), v.reshape(shp)
    shapes = {n: P[n].shape for n in SMALL}
    d, m, v = _adamw(_pack(P, SMALL), _pack(Gd, SMALL), _pack(M, SMALL), _pack(V, SMALL), "adamw_small")
    for dst, buf in ((delta, d), (new_m, m), (new_v, v)):
        dst.update(_unpack(buf, SMALL, shapes))
    return delta, new_m, new_v


def kernel(x, meta_tokens, rel_bias_table, norm_mix, w_in, swa_sinks, fox_forget_bias, conv_w, conv_b, lru_w_r, lru_b_r, lru_w_i, lru_b_i, lru_lambda, w_branch, w_out, norm_ffn, w_ffn_in, w_ffn_out, norm_final, loss_target, m_meta_tokens, m_rel_bias_table, m_norm_mix, m_w_in, m_swa_sinks, m_fox_forget_bias, m_conv_w, m_conv_b, m_lru_w_r, m_lru_b_r, m_lru_w_i, m_lru_b_i, m_lru_lambda, m_w_branch, m_w_out, m_norm_ffn, m_w_ffn_in, m_w_ffn_out, m_norm_final, v_meta_tokens, v_rel_bias_table, v_norm_mix, v_w_in, v_swa_sinks, v_fox_forget_bias, v_conv_w, v_conv_b, v_lru_w_r, v_lru_b_r, v_lru_w_i, v_lru_b_i, v_lru_lambda, v_w_branch, v_w_out, v_norm_ffn, v_w_ffn_in, v_w_ffn_out, v_norm_final):
    P = dict(zip(NAMES, (meta_tokens, rel_bias_table, norm_mix, w_in, swa_sinks, fox_forget_bias, conv_w, conv_b, lru_w_r,
                         lru_b_r, lru_w_i, lru_b_i, lru_lambda, w_branch, w_out, norm_ffn, w_ffn_in, w_ffn_out, norm_final)))
    M = dict(zip(NAMES, (m_meta_tokens, m_rel_bias_table, m_norm_mix, m_w_in, m_swa_sinks, m_fox_forget_bias, m_conv_w,
                         m_conv_b, m_lru_w_r, m_lru_b_r, m_lru_w_i, m_lru_b_i, m_lru_lambda, m_w_branch, m_w_out, m_norm_ffn,
                         m_w_ffn_in, m_w_ffn_out, m_norm_final)))
    V = dict(zip(NAMES, (v_meta_tokens, v_rel_bias_table, v_norm_mix, v_w_in, v_swa_sinks, v_fox_forget_bias, v_conv_w,
                         v_conv_b, v_lru_w_r, v_lru_b_r, v_lru_w_i, v_lru_b_i, v_lru_lambda, v_w_branch, v_w_out, v_norm_ffn,
                         v_w_ffn_in, v_w_ffn_out, v_norm_final)))
    W, placed = _gather_weights(P)
    loss_local, grad_x, grads = _local_step(x[0], loss_target[0], W, placed)
    loss = lax.psum(loss_local, ("x", "y", "c"))
    Gd = _reduce_grads(grads, P)
    delta, new_m, new_v = _update(P, Gd, M, V)
    return (loss, grad_x[None], *[Gd[n] for n in NAMES], *[delta[n] for n in NAMES],
            *[new_m[n] for n in NAMES], *[new_v[n] for n in NAMES])
```

```python
import functools
import math

import numpy as np
import jax
import jax.numpy as jnp
from jax import lax
from jax.experimental import pallas as pl
from jax.experimental.pallas import tpu as pltpu

F32, BF16 = jnp.float32, jnp.bfloat16
MESH = pl.DeviceIdType.MESH
ANY = pl.BlockSpec(memory_space=pl.ANY)
SMEM = pl.BlockSpec(memory_space=pltpu.SMEM)

D = 1024
DEPTH = 4
BLK = 128
N_META = 16
NPAD = 112
NH = 8
LW = 512
DFF = 2816
EPS = 1e-6
NEG = -1e30
SCALE = 0.125
LRU_C = 8.0
REL_BUCKETS = 32
N_SHARD = 4
QA, QF, KF, VF, XC, YC, GT, KA, VA, FL, INP = 0, 512, 1024, 1536, 2048, 2560, 3072, 6144, 6272, 6400, 6656
IN_COLS = 6408
VMEM_LIMIT = 48 * 1024 * 1024

ADAM_LR, ADAM_B1, ADAM_B2, ADAM_EPS, ADAM_WD, ADAM_STEP = 0.001, 0.9, 0.999, 1e-08, 0.01, 10


def _cp(*sem):
    return pltpu.CompilerParams(dimension_semantics=sem or None, vmem_limit_bytes=VMEM_LIMIT)


def _pick(n, prefs):
    for p in prefs:
        if n % p == 0:
            return p
    return n


def _rt(T):
    return _pick(T, (384, 128))


def _sigmoid(z):
    return 1.0 / (1.0 + jnp.exp(-z))


def _log_sigmoid(z):
    return jnp.minimum(z, 0.0) - jnp.log(1.0 + jnp.exp(-jnp.abs(z)))


def _gelu(y):
    c = math.sqrt(2.0 / math.pi)
    return 0.5 * y * (1.0 + jnp.tanh(c * (y + 0.044715 * y * y * y)))


def _gelu_grad(y):
    c = math.sqrt(2.0 / math.pi)
    t = jnp.tanh(c * (y + 0.044715 * y * y * y))
    return 0.5 * (1.0 + t) + 0.5 * y * (1.0 - t * t) * c * (1.0 + 3.0 * 0.044715 * y * y)


def _neg_expm1(z):
    series = -z * (1.0 + z * (0.5 + z * (1.0 / 6.0 + z * (1.0 / 24.0 + z * (1.0 / 120.0)))))
    return jnp.where(z > -0.1, series, 1.0 - jnp.exp(z))


def _dot(a, b, ca, cb):
    return lax.dot_general(a, b, (((ca,), (cb,)), ((), ())), preferred_element_type=F32)


def _mm(a, b, *, ta=False, tb=False, res=None, out_dtype=F32, tm, tn, tk, name, slab=None, b_k0=0, col0=0, cols=None):
    M, K = (a.shape[1], a.shape[0]) if ta else a.shape
    N = b.shape[0] if tb else b.shape[1]
    assert (b.shape[1] if tb else b.shape[0]) >= K + b_k0 and M % tm == 0 and N % tn == 0 and K % tk == 0, (name, a.shape, b.shape)
    assert b_k0 % tk == 0 and col0 % tn == 0
    nk, kb, jb = K // tk, b_k0 // tk, col0 // tn
    ca, cb = (0 if ta else 1), (1 if tb else 0)
    n_in = 2 + (res is not None) + (slab is not None and slab[0] is not None)

    def body(*refs):
        a_ref, b_ref = refs[:2]
        r_ref = refs[2] if res is not None else None
        o_ref = refs[n_in]
        part = _dot(a_ref[...].astype(BF16), b_ref[...].astype(BF16), ca, cb)

        def fin(acc):
            if res is not None:
                acc = acc + r_ref[...]
            o_ref[...] = acc.astype(out_dtype)

        if nk == 1:
            fin(part)
        else:
            acc_ref = refs[-1]
            k = pl.program_id(2)

            @pl.when(k == 0)
            def _():
                acc_ref[...] = part

            @pl.when(k > 0)
            def _():
                acc_ref[...] += part

            @pl.when(k == nk - 1)
            def _():
                fin(acc_ref[...])

    a_spec = pl.BlockSpec((tk, tm), lambda i, j, k: (k, i)) if ta else pl.BlockSpec((tm, tk), lambda i, j, k: (i, k))
    b_spec = (pl.BlockSpec((tn, tk), lambda i, j, k: (j, k + kb)) if tb
              else pl.BlockSpec((tk, tn), lambda i, j, k: (k + kb, j)))
    o_spec = pl.BlockSpec((tm, tn), lambda i, j, k: (i, j))
    in_specs, ops = [a_spec, b_spec], [a, b]
    if res is not None:
        in_specs.append(o_spec)
        ops.append(res)
    out_shape, aliases = jax.ShapeDtypeStruct((M, N), out_dtype), {}
    if slab is not None:
        buf, idx, n = slab
        o_spec = pl.BlockSpec((None, tm, tn), lambda i, j, k: (idx, i, j + jb))
        out_shape = jax.ShapeDtypeStruct((n, M, cols or N), out_dtype)
        if buf is not None:
            aliases = {len(ops): 0}
            in_specs.append(ANY)
            ops.append(buf)
    return pl.pallas_call(
        body, grid=(M // tm, N // tn, nk), in_specs=in_specs, out_specs=o_spec, out_shape=out_shape,
        input_output_aliases=aliases, scratch_shapes=[pltpu.VMEM((tm, tn), F32)] if nk > 1 else [],
        compiler_params=_cp("parallel", "parallel", "arbitrary"), name=name)(*ops)


def _transpose(x, name):
    T, C = x.shape
    tr, tc = _rt(T), _pick(C, (1408, 1024, 512, 256, 128))

    def body(x_ref, o_ref):
        o_ref[...] = x_ref[...].T

    return pl.pallas_call(
        body, grid=(T // tr, C // tc), in_specs=[pl.BlockSpec((tr, tc), lambda i, j: (i, j))],
        out_specs=pl.BlockSpec((tc, tr), lambda i, j: (j, i)), out_shape=jax.ShapeDtypeStruct((C, T), x.dtype),
        compiler_params=_cp("parallel", "parallel"), name=name)(x)


def _rms_fwd(h, g, name):
    T = h.shape[0]
    tr = _rt(T)

    def body(h_ref, g_ref, u_ref, ut_ref):
        x = h_ref[...]
        r = lax.rsqrt(jnp.mean(x * x, axis=-1, keepdims=True) + EPS)
        u = (x * r * g_ref[...]).astype(BF16)
        u_ref[...] = u
        ut_ref[...] = u.T

    return pl.pallas_call(
        body, grid=(T // tr,),
        in_specs=[pl.BlockSpec((tr, D), lambda i: (i, 0)), pl.BlockSpec((1, D), lambda i: (0, 0))],
        out_specs=[pl.BlockSpec((tr, D), lambda i: (i, 0)), pl.BlockSpec((D, tr), lambda i: (0, i))],
        out_shape=[jax.ShapeDtypeStruct((T, D), BF16), jax.ShapeDtypeStruct((D, T), BF16)],
        compiler_params=_cp("parallel"), name=name)(h, g.reshape(1, D))


def _rms_bwd(du, h, g, dres, name):
    T = h.shape[0]
    tr = _rt(T)

    def body(du_ref, h_ref, g_ref, dres_ref, dh_ref, dhb_ref, dg_ref):
        x = h_ref[...]
        r = lax.rsqrt(jnp.mean(x * x, axis=-1, keepdims=True) + EPS)
        xh = x * r
        dy = du_ref[...]
        dxh = dy * g_ref[...]
        dx = r * (dxh - xh * jnp.mean(dxh * xh, axis=-1, keepdims=True))
        dh = dres_ref[...] + dx
        dh_ref[...] = dh
        dhb_ref[...] = dh.astype(BF16)
        part = jnp.sum(dy * xh, axis=0, keepdims=True)

        @pl.when(pl.program_id(0) == 0)
        def _():
            dg_ref[...] = part

        @pl.when(pl.program_id(0) > 0)
        def _():
            dg_ref[...] += part

    row = pl.BlockSpec((tr, D), lambda i: (i, 0))
    vec = pl.BlockSpec((1, D), lambda i: (0, 0))
    return pl.pallas_call(
        body, grid=(T // tr,), in_specs=[row, row, vec, row], out_specs=[row, row, vec],
        out_shape=[jax.ShapeDtypeStruct((T, D), F32), jax.ShapeDtypeStruct((T, D), BF16), jax.ShapeDtypeStruct((1, D), F32)],
        compiler_params=_cp("arbitrary"), name=name)(du, h, g.reshape(1, D), dres)


def _loss_head(h, tgt, g, name):
    T = h.shape[0]
    nb = T // BLK

    def body(h_ref, t_ref, g_ref, dh_ref, dhb_ref, dg_ref, loss_ref):
        i = pl.program_id(0)
        x = h_ref[...]
        r = lax.rsqrt(jnp.mean(x * x, axis=-1, keepdims=True) + EPS)
        xh = x * r
        gv = g_ref[...]
        tok = i >= 1
        err = jnp.where(tok, xh * gv - t_ref[...], 0.0)
        dy = err * (1.0 / D)
        dxh = dy * gv
        dx = r * (dxh - xh * jnp.mean(dxh * xh, axis=-1, keepdims=True))
        dh_ref[...] = dx
        dhb_ref[...] = dx.astype(BF16)
        dg = jnp.sum(dy * xh, axis=0, keepdims=True)
        ls = jnp.zeros((1, BLK), F32) + jnp.sum(err * err) * (0.5 / D)

        @pl.when(i == 0)
        def _():
            dg_ref[...] = dg
            loss_ref[...] = ls

        @pl.when(i > 0)
        def _():
            dg_ref[...] += dg
            loss_ref[...] += ls

    row = pl.BlockSpec((BLK, D), lambda i: (i, 0))
    vec = pl.BlockSpec((1, D), lambda i: (0, 0))
    return pl.pallas_call(
        body, grid=(nb,),
        in_specs=[row, pl.BlockSpec((BLK, D), lambda i: (jnp.maximum(i - 1, 0), 0)), vec],
        out_specs=[row, row, vec, pl.BlockSpec((1, BLK), lambda i: (0, 0))],
        out_shape=[jax.ShapeDtypeStruct((T, D), F32), jax.ShapeDtypeStruct((T, D), BF16),
                   jax.ShapeDtypeStruct((1, D), F32), jax.ShapeDtypeStruct((1, BLK), F32)],
        compiler_params=_cp("arbitrary"), name=name)(h, tgt, g.reshape(1, D))


def _bucket_table():
    q = np.arange(BLK)[:, None]
    k = np.arange(2 * BLK)[None, :]
    d = np.maximum(q + BLK - k, 0)
    max_exact = REL_BUCKETS // 2
    scaled = np.log(np.maximum(d, 1).astype(np.float32) / np.float32(max_exact)) / np.float32(math.log(128 / max_exact))
    large = np.minimum(max_exact + (scaled.astype(np.float32) * (REL_BUCKETS - max_exact)).astype(np.int32), REL_BUCKETS - 1)
    return np.where(d < max_exact, d, large).astype(np.int32)


def _bias_build(table, bucket, name):
    def body(t_ref, bk_ref, o_ref):
        bk = bk_ref[...]
        for h in range(NH):
            acc = jnp.zeros((BLK, 2 * BLK), F32)
            for b in range(REL_BUCKETS):
                acc = jnp.where(bk == b, t_ref[b, h], acc)
            o_ref[h] = acc

    return pl.pallas_call(
        body, in_specs=[SMEM, pl.BlockSpec(memory_space=pltpu.VMEM)], out_specs=pl.BlockSpec(memory_space=pltpu.VMEM),
        out_shape=jax.ShapeDtypeStruct((NH, BLK, 2 * BLK), F32), compiler_params=_cp(), name=name)(table, bucket)


def _bias_bwd(dbias, bucket, name):
    def body(d_ref, bk_ref, o_ref):
        bk = bk_ref[...]
        for h in range(NH):
            dh = d_ref[h]
            for b in range(REL_BUCKETS):
                o_ref[b, h] = jnp.sum(jnp.where(bk == b, dh, 0.0))

    return pl.pallas_call(
        body, in_specs=[pl.BlockSpec(memory_space=pltpu.VMEM)] * 2, out_specs=SMEM,
        out_shape=jax.ShapeDtypeStruct((REL_BUCKETS, NH), F32), compiler_params=_cp(), name=name)(dbias, bucket)


def _swa_specs(nq_cols):
    prev = lambda n: jnp.maximum(n - 1, 0)
    return [
        pl.BlockSpec((BLK, nq_cols), lambda n: (n, QA // nq_cols)),
        pl.BlockSpec((BLK, BLK), lambda n: (prev(n), KA // BLK)), pl.BlockSpec((BLK, BLK), lambda n: (n, KA // BLK)),
        pl.BlockSpec((BLK, BLK), lambda n: (prev(n), VA // BLK)), pl.BlockSpec((BLK, BLK), lambda n: (n, VA // BLK)),
    ]


def _swa_mask(n):
    row = lax.broadcasted_iota(jnp.int32, (BLK, 2 * BLK), 0)
    col = lax.broadcasted_iota(jnp.int32, (BLK, 2 * BLK), 1)
    dist = row + BLK - col
    return (dist >= 0) & (dist < BLK) & ((n - 1) * BLK + col >= NPAD)


def _swa_probs(qm, ksel, mask, bias_h, sink):
    s = _dot(qm, ksel, 1, 1) * SCALE
    s = jnp.where(mask, s + bias_h, NEG)
    m = jnp.maximum(jnp.max(s, axis=-1, keepdims=True), sink)
    p = jnp.exp(s - m)
    psink = jnp.exp(sink - m)
    inv = 1.0 / (jnp.sum(p, axis=-1, keepdims=True) + psink)
    return p * inv, psink * inv


def _swa_fwd(proj, bias, sinks, name):
    T = proj.shape[0]
    nb = T // BLK

    def body(sk_ref, q_ref, kp_ref, kc_ref, vp_ref, vc_ref, b_ref, o_ref):
        n = pl.program_id(0)
        lo = lax.broadcasted_iota(jnp.int32, (1, BLK), 1) < 64
        kb = jnp.concatenate([kp_ref[...], kc_ref[...]], axis=0)
        vb = jnp.concatenate([vp_ref[...], vc_ref[...]], axis=0)
        kbs = (kb.astype(BF16), pltpu.roll(kb, 64, 1).astype(BF16))
        vbs = (vb, pltpu.roll(vb, 64, 1))
        mask = _swa_mask(n)
        outs = []
        for pr in range(NH // 2):
            qp = q_ref[:, pr * BLK:(pr + 1) * BLK]
            kv = pr // 2
            acc = jnp.zeros((BLK, BLK), F32)
            for e in range(2):
                lm = lo if e == 0 else jnp.logical_not(lo)
                sw = 0 if kv == e else 1
                qm = jnp.where(lm, qp, 0.0).astype(BF16)
                pn, _ = _swa_probs(qm, kbs[sw], mask, b_ref[2 * pr + e], sk_ref[2 * pr + e])
                acc = acc + _dot(pn.astype(BF16), jnp.where(lm, vbs[sw], 0.0).astype(BF16), 1, 0)
            outs.append(acc)
        o_ref[...] = jnp.concatenate(outs, axis=1).astype(BF16)

    return pl.pallas_call(
        body, grid=(nb,),
        in_specs=[SMEM] + _swa_specs(512) + [pl.BlockSpec((NH, BLK, 2 * BLK), lambda n: (0, 0, 0))],
        out_specs=pl.BlockSpec((BLK, 512), lambda n: (n, 0)), out_shape=jax.ShapeDtypeStruct((T, 512), BF16),
        compiler_params=_cp("parallel"), name=name)(sinks, proj, proj, proj, proj, proj, bias)


def _swa_bwd(proj, bias, sinks, do, dbias_in, name):
    T = proj.shape[0]
    nb = T // BLK

    def body(sk_ref, q_ref, kp_ref, kc_ref, vp_ref, vc_ref, b_ref, do_ref, dbi_ref,
             dq_ref, dk_ref, dv_ref, db_ref, dsk_ref, sk_acc):
        n = pl.program_id(0)
        lane = lax.broadcasted_iota(jnp.int32, (1, BLK), 1)
        lo = lane < 64
        kb = jnp.concatenate([kp_ref[...], kc_ref[...]], axis=0)
        vb = jnp.concatenate([vp_ref[...], vc_ref[...]], axis=0)
        kbs = (kb, pltpu.roll(kb, 64, 1))
        vbs = (vb, pltpu.roll(vb, 64, 1))
        mask = _swa_mask(n)

        @pl.when(n == 0)
        def _():
            db_ref[...] = dbi_ref[...]
            sk_acc[...] = jnp.zeros_like(sk_acc)

        dqs = []
        dk = jnp.zeros((2 * BLK, BLK), F32)
        dv = jnp.zeros((2 * BLK, BLK), F32)
        for pr in range(NH // 2):
            qp = q_ref[:, pr * BLK:(pr + 1) * BLK]
            dop = do_ref[:, pr * BLK:(pr + 1) * BLK].astype(F32)
            kv = pr // 2
            dq = jnp.zeros((BLK, BLK), F32)
            for e in range(2):
                h = 2 * pr + e
                lm = lo if e == 0 else jnp.logical_not(lo)
                sw = 0 if kv == e else 1
                qm = jnp.where(lm, qp, 0.0)
                dom = jnp.where(lm, dop, 0.0)
                pn, ps = _swa_probs(qm.astype(BF16), kbs[sw].astype(BF16), mask, b_ref[h], sk_ref[h])
                dp = _dot(dom.astype(BF16), vbs[sw].astype(BF16), 1, 1)
                delta = jnp.sum(pn * dp, axis=-1, keepdims=True)
                ds = pn * (dp - delta)
                db_ref[h] += ds
                sk_acc[...] += jnp.where(lane == h, -(ps * delta), 0.0)
                dsb = (ds * SCALE).astype(BF16)
                dq = dq + _dot(dsb, jnp.where(lm, kbs[sw], 0.0).astype(BF16), 1, 0)
                qk = qm if sw == 0 else pltpu.roll(qm, 64, 1)
                dok = dom if sw == 0 else pltpu.roll(dom, 64, 1)
                dk = dk + _dot(dsb, qk.astype(BF16), 0, 0)
                dv = dv + _dot(pn.astype(BF16), dok.astype(BF16), 0, 0)
            dqs.append(dq)
        dq_ref[...] = jnp.concatenate(dqs, axis=1).astype(BF16)
        dk_ref[0] = dk
        dv_ref[0] = dv

        @pl.when(n == nb - 1)
        def _():
            dsk_ref[...] = jnp.sum(sk_acc[...], axis=0, keepdims=True)

    full_b = pl.BlockSpec((NH, BLK, 2 * BLK), lambda n: (0, 0, 0))
    band = pl.BlockSpec((1, 2 * BLK, BLK), lambda n: (n, 0, 0))
    return pl.pallas_call(
        body, grid=(nb,),
        in_specs=[SMEM] + _swa_specs(512) + [full_b, pl.BlockSpec((BLK, 512), lambda n: (n, 0)), full_b],
        out_specs=[pl.BlockSpec((BLK, 512), lambda n: (n, 0)), band, band, full_b, pl.BlockSpec((1, BLK), lambda n: (0, 0))],
        out_shape=[jax.ShapeDtypeStruct((T, 512), BF16), jax.ShapeDtypeStruct((nb, 2 * BLK, BLK), F32),
                   jax.ShapeDtypeStruct((nb, 2 * BLK, BLK), F32), jax.ShapeDtypeStruct((NH, BLK, 2 * BLK), F32),
                   jax.ShapeDtypeStruct((1, BLK), F32)],
        scratch_shapes=[pltpu.VMEM((BLK, BLK), F32)],
        compiler_params=_cp("arbitrary"), name=name)(sinks, proj, proj, proj, proj, proj, bias, do, dbias_in)


def _band_fold(dkb, dvb, name):
    nb = dkb.shape[0]

    def body(ko_ref, kn_ref, vo_ref, vn_ref, dk_ref, dv_ref):
        last = pl.program_id(0) == nb - 1
        dk_ref[...] = (ko_ref[0] + jnp.where(last, 0.0, kn_ref[0])).astype(BF16)
        dv_ref[...] = (vo_ref[0] + jnp.where(last, 0.0, vn_ref[0])).astype(BF16)

    own = pl.BlockSpec((1, BLK, BLK), lambda j: (j, 1, 0))
    nxt = pl.BlockSpec((1, BLK, BLK), lambda j: (jnp.minimum(j + 1, nb - 1), 0, 0))
    out = pl.BlockSpec((BLK, BLK), lambda j: (j, 0))
    return pl.pallas_call(
        body, grid=(nb,), in_specs=[own, nxt, own, nxt], out_specs=[out, out],
        out_shape=[jax.ShapeDtypeStruct((nb * BLK, BLK), BF16)] * 2,
        compiler_params=_cp("parallel"), name=name)(dkb, dkb, dvb, dvb)


def _token_major(x, width):
    full = jnp.concatenate([x, jnp.zeros((BLK - NH, BLK), F32)], axis=0).T
    return full if width == BLK else jnp.concatenate([full, jnp.zeros((BLK, width - BLK), F32)], axis=1)


def _cum_fwd(proj, fb, name):
    T = proj.shape[0]

    def body(z_ref, fb_ref, c_ref, carry):
        b = pl.program_id(0)
        lane = lax.broadcasted_iota(jnp.int32, (NH, BLK), 1)

        @pl.when(b == 0)
        def _():
            carry[...] = jnp.zeros_like(carry)

        z = z_ref[...].T[0:NH, :] + fb_ref[...]
        x = jnp.where(b * BLK + lane >= NPAD, _log_sigmoid(z), 0.0)
        s = 1
        while s < BLK:
            x = x + jnp.where(lane >= s, pltpu.roll(x, s, 1), 0.0)
            s *= 2
        x = x + carry[...]
        carry[...] = jnp.zeros((NH, BLK), F32) + jnp.sum(jnp.where(lane == BLK - 1, x, 0.0), axis=-1, keepdims=True)
        c_ref[...] = _token_major(x, BLK)

    return pl.pallas_call(
        body, grid=(T // BLK,),
        in_specs=[pl.BlockSpec((BLK, BLK), lambda b: (b, FL // BLK)), pl.BlockSpec((NH, 1), lambda b: (0, 0))],
        out_specs=pl.BlockSpec((BLK, BLK), lambda b: (b, 0)), out_shape=jax.ShapeDtypeStruct((T, BLK), F32),
        scratch_shapes=[pltpu.VMEM((NH, BLK), F32)], compiler_params=_cp("arbitrary"), name=name)(proj, fb)


def _cum_bwd(dqx, dkx, proj, fb, name):
    T = proj.shape[0]
    nb = T // BLK

    def body(dq_ref, dk_ref, z_ref, fb_ref, dz_ref, db_ref, carry):
        k = pl.program_id(0)
        b = nb - 1 - k
        lane = lax.broadcasted_iota(jnp.int32, (NH, BLK), 1)

        @pl.when(k == 0)
        def _():
            carry[...] = jnp.zeros_like(carry)
            db_ref[...] = jnp.zeros_like(db_ref)

        def picked(ref, r_first, r_second):
            rows = []
            for p in range(NH // 2):
                t_ = ref[:, p * BLK:(p + 1) * BLK].T
                rows += [t_[r_first:r_first + 1, :], t_[r_second:r_second + 1, :]]
            return jnp.concatenate(rows, axis=0)

        x = picked(dq_ref, 64, 0) - picked(dk_ref, 67, 3)
        s = 1
        while s < BLK:
            x = x + jnp.where(lane < BLK - s, pltpu.roll(x, BLK - s, 1), 0.0)
            s *= 2
        x = x + carry[...]
        carry[...] = jnp.zeros((NH, BLK), F32) + jnp.sum(jnp.where(lane == 0, x, 0.0), axis=-1, keepdims=True)
        z = z_ref[...].T[0:NH, :] + fb_ref[...]
        dz = jnp.where(b * BLK + lane >= NPAD, x * _sigmoid(-z), 0.0)
        db_ref[...] += jnp.sum(dz, axis=-1, keepdims=True)
        dz_ref[...] = _token_major(dz, 2 * BLK).astype(BF16)

    rev = lambda k: nb - 1 - k
    wide = pl.BlockSpec((BLK, 512), lambda k: (rev(k), 0))
    return pl.pallas_call(
        body, grid=(nb,),
        in_specs=[wide, wide, pl.BlockSpec((BLK, BLK), lambda k: (rev(k), FL // BLK)), pl.BlockSpec((NH, 1), lambda k: (0, 0))],
        out_specs=[pl.BlockSpec((BLK, 2 * BLK), lambda k: (rev(k), 0)), pl.BlockSpec((NH, BLK), lambda k: (0, 0))],
        out_shape=[jax.ShapeDtypeStruct((T, 2 * BLK), BF16), jax.ShapeDtypeStruct((NH, BLK), F32)],
        scratch_shapes=[pltpu.VMEM((NH, BLK), F32)], compiler_params=_cp("arbitrary"), name=name)(dqx, dkx, proj, fb)


def _fox_prep(proj, ccol, name):
    T = proj.shape[0]
    tr = _pick(T, (1408, 384, 128))

    def body(q_ref, k_ref, v_ref, cc_ref, qa_ref, ka_ref, kt_ref, vm_ref, vo_ref):
        h = pl.program_id(1)
        lane = lax.broadcasted_iota(jnp.int32, (1, BLK), 1)
        own = (lane >> 6) == (h & 1)
        a0 = 64 * (1 - (h & 1))
        c = _lane_pick(cc_ref[...], lane, h)
        hi = c.astype(BF16).astype(F32)
        mid = (c - hi).astype(BF16).astype(F32)
        lo = (c - hi - mid).astype(BF16).astype(F32)
        ones = (lane >= a0 + 3) & (lane < a0 + 6)
        qa = jnp.where(own, q_ref[...] * SCALE, jnp.where(ones, 1.0, 0.0))
        qa = jnp.where(lane == a0, hi, jnp.where(lane == a0 + 1, mid, jnp.where(lane == a0 + 2, lo, qa)))
        ones = (lane >= a0) & (lane < a0 + 3)
        ka = jnp.where(own, k_ref[...], jnp.where(ones, 1.0, 0.0))
        ka = jnp.where(lane == a0 + 3, -hi, jnp.where(lane == a0 + 4, -mid, jnp.where(lane == a0 + 5, -lo, ka)))
        qa_ref[...] = qa.astype(BF16)
        kab = ka.astype(BF16)
        ka_ref[...] = kab
        kt_ref[...] = kab.T
        vm = jnp.where(own, v_ref[...], 0.0)
        vm_ref[...] = vm.astype(BF16)
        vo_ref[...] = jnp.where(lane == a0, 1.0, vm).astype(BF16)

    pair = lambda col0: pl.BlockSpec((tr, BLK), lambda i, h: (i, col0 // BLK + (h >> 1)))
    out = pl.BlockSpec((None, tr, BLK), lambda i, h: (h, i, 0))
    out_t = pl.BlockSpec((None, BLK, tr), lambda i, h: (h, 0, i))
    tok = jax.ShapeDtypeStruct((NH, T, BLK), BF16)
    return pl.pallas_call(
        body, grid=(T // tr, NH), in_specs=[pair(QF), pair(KF), pair(VF), pl.BlockSpec((tr, BLK), lambda i, h: (i, 0))],
        out_specs=[out, out, out_t, out, out], out_shape=[tok, tok, jax.ShapeDtypeStruct((NH, BLK, T), BF16), tok, tok],
        compiler_params=_cp("parallel", "arbitrary"), name=name)(proj, proj, proj, ccol)


def _fox_fwd(qaug, kaug_t, vo, name, gather=None):
    T = qaug.shape[1]
    t = _rt(T)
    nt = T // t
    ng = len(gather[0]) if gather else 0

    pairs = [(i, j) for i in range(nt) for j in range(i + 1)]
    i_of = jnp.asarray(np.array([p[0] for p in pairs], np.int32))
    j_of = jnp.asarray(np.array([p[1] for p in pairs], np.int32))
    ns = len(pairs)

    def body(i_ref, j_ref, q0, q1, k0, k1, v0, v1, *rest):
        o_ref, lse0_ref, lse1_ref = rest[ng:ng + 3]
        m_ref, acc_ref = rest[2 * ng + 3:2 * ng + 5]
        p_, s_ = pl.program_id(0), pl.program_id(1)
        i, j = i_ref[s_], j_ref[s_]
        lane = lax.broadcasted_iota(jnp.int32, (1, BLK), 1)
        lo = lane < 64
        if gather:
            start, finish = _gather_plan(rest[ng + 3:2 * ng + 3], gather[1], *rest[2 * ng + 5:])
            pl.when((p_ == 0) & (s_ == 0))(start)

        @pl.when(j == 0)
        def _():
            m_ref[...] = jnp.full_like(m_ref, NEG)
            acc_ref[...] = jnp.zeros_like(acc_ref)

        def step(masked):
            for e, (q_ref, k_ref, v_ref) in enumerate(((q0, k0, v0), (q1, k1, v1))):
                s = _dot(q_ref[...], k_ref[...], 1, 0)
                if masked:
                    s = jnp.where(_fox_mask(i, j, t), s, NEG)
                m_old = m_ref[e]
                m_new = jnp.maximum(m_old, jnp.max(s, axis=-1, keepdims=True))
                m_ref[e] = m_new
                pe = jnp.exp(s - jnp.concatenate([m_new] * (t // BLK), axis=1))
                acc_ref[e] = jnp.exp(m_old - m_new) * acc_ref[e] + _dot(pe.astype(BF16), v_ref[...], 1, 0)

        pl.when((j < i) & (j > 0))(lambda: step(False))
        pl.when((j == i) | ((j == 0) & (i > 0)))(lambda: step(True))

        @pl.when(j == i)
        def _():
            rows = i * t + lax.broadcasted_iota(jnp.int32, (t, 1), 0)
            l0, l1 = _lane_pick(acc_ref[0], lane, 64), _lane_pick(acc_ref[1], lane, 0)
            o = jnp.where(lo, acc_ref[0] / l0, acc_ref[1] / l1)
            o_ref[...] = jnp.where(rows >= NPAD, o, 0.0).astype(BF16)
            lse0_ref[...] = m_ref[0] + jnp.log(l0)
            lse1_ref[...] = m_ref[1] + jnp.log(l1)

        if gather:
            pl.when((p_ == NH // 2 - 1) & (s_ == ns - 1))(finish)

    qs = lambda e: pl.BlockSpec((None, t, BLK), lambda p, s, ii, jj: (2 * p + e, ii[s], 0))
    ks = lambda e: pl.BlockSpec((None, t, BLK), lambda p, s, ii, jj: (2 * p + e, jj[s], 0))
    kts = lambda e: pl.BlockSpec((None, BLK, t), lambda p, s, ii, jj: (2 * p + e, 0, jj[s]))
    rep = pl.BlockSpec((None, t, BLK), lambda p, s, ii, jj: (p, ii[s], 0))
    bufs = list(gather[0]) if gather else []
    return pl.pallas_call(
        body,
        grid_spec=pltpu.PrefetchScalarGridSpec(
            num_scalar_prefetch=2, grid=(NH // 2, ns),
            in_specs=[qs(0), qs(1), kts(0), kts(1), ks(0), ks(1)] + [ANY] * ng,
            out_specs=[pl.BlockSpec((t, BLK), lambda p, s, ii, jj: (ii[s], p)), rep, rep] + [ANY] * ng,
            scratch_shapes=[pltpu.VMEM((2, t, BLK), F32), pltpu.VMEM((2, t, BLK), F32)]
            + ([pltpu.SemaphoreType.DMA((6 * ng,)), pltpu.SemaphoreType.DMA((6 * ng,))] if gather else [])),
        out_shape=[jax.ShapeDtypeStruct((T, 512), BF16)] + [jax.ShapeDtypeStruct((NH // 2, T, BLK), F32)] * 2
        + [jax.ShapeDtypeStruct(b.shape, b.dtype) for b in bufs],
        input_output_aliases={8 + g: 3 + g for g in range(ng)},
        compiler_params=(pltpu.CompilerParams(dimension_semantics=("arbitrary",) * 2, vmem_limit_bytes=VMEM_LIMIT,
                                              has_side_effects=True) if gather
                         else _cp("parallel", "arbitrary")), name=name)(i_of, j_of, qaug, qaug, kaug_t, kaug_t, vo, vo, *bufs)


def _fox_delta(do, o, name):
    T = do.shape[0]
    tr = _rt(T)

    def body(do_ref, o_ref, d0_ref, d1_ref):
        lo = lax.broadcasted_iota(jnp.int32, (1, BLK), 1) < 64
        prod = do_ref[...].astype(F32) * o_ref[...].astype(F32)
        d0_ref[...] = jnp.zeros((tr, BLK), F32) + jnp.sum(jnp.where(lo, prod, 0.0), axis=-1, keepdims=True)
        d1_ref[...] = jnp.zeros((tr, BLK), F32) + jnp.sum(jnp.where(lo, 0.0, prod), axis=-1, keepdims=True)

    blk = pl.BlockSpec((tr, BLK), lambda i, p: (i, p))
    rep = pl.BlockSpec((None, tr, BLK), lambda i, p: (p, i, 0))
    return pl.pallas_call(
        body, grid=(T // tr, NH // 2), in_specs=[blk, blk], out_specs=[rep, rep],
        out_shape=[jax.ShapeDtypeStruct((NH // 2, T, BLK), F32)] * 2,
        compiler_params=_cp("parallel", "parallel"), name=name)(do, o)


def _fox_bwd(qaug, kaug, kaug_t, vm, do, lses, deltas, name, scatter=None):
    T = qaug.shape[1]
    t = _rt(T)
    nt = T // t
    ng = len(scatter[0]) if scatter else 0
    pairs = [(i, j) for j in range(nt) for i in range(j, nt)]
    i_of = jnp.asarray(np.array([p[0] for p in pairs], np.int32))
    j_of = jnp.asarray(np.array([p[1] for p in pairs], np.int32))
    ns = len(pairs)

    def body(i_ref, j_ref, q0, q1, k0, k1, kt0, kt1, v0, v1, do_ref, lse0, lse1, dl0, dl1, *rest):
        dq_ref, dqx_ref, dk_ref, dv_ref, dkx_ref = rest[ng:ng + 5]
        dq_acc, dk_acc, dv_acc = rest[2 * ng + 5:2 * ng + 8]
        p_, s_ = pl.program_id(0), pl.program_id(1)
        i, j = i_ref[s_], j_ref[s_]
        lane = lax.broadcasted_iota(jnp.int32, (1, BLK), 1)
        lo = lane < 64
        if scatter:
            start, finish = _scatter_plan(rest[:ng], rest[ng + 5:2 * ng + 5], scatter[1], *rest[2 * ng + 8:])
            pl.when((p_ == 0) & (s_ == 0))(start)

        @pl.when(s_ == 0)
        def _():
            dq_acc[...] = jnp.zeros_like(dq_acc)

        @pl.when(i == j)
        def _():
            dk_acc[...] = jnp.zeros_like(dk_acc)
            dv_acc[...] = jnp.zeros_like(dv_acc)

        def step(masked):
            dob = do_ref[...]
            rows = pl.ds(pl.multiple_of(i * t, t), t)
            wide = lambda ref: jnp.concatenate([ref[...]] * (t // BLK), axis=1)
            for e, (q_ref, k_ref, kt_ref, v_ref, lse_ref, dl_ref) in enumerate(
                    ((q0, k0, kt0, v0, lse0, dl0), (q1, k1, kt1, v1, lse1, dl1))):
                s = _dot(q_ref[...], kt_ref[...], 1, 0)
                if masked:
                    s = jnp.where(_fox_mask(i, j, t), s, NEG)
                pe = jnp.exp(s - wide(lse_ref))
                dp = _dot(dob, v_ref[...], 1, 1)
                ds = (pe * (dp - wide(dl_ref))).astype(BF16)
                dq_acc[e, rows, :] += _dot(ds, k_ref[...], 1, 0)
                dk_acc[e] += _dot(ds, q_ref[...], 0, 0)
                dv_acc[e] += _dot(pe.astype(BF16), dob, 0, 0)

        pl.when((i > j) & (j > 0))(lambda: step(False))
        pl.when((i == j) | ((j == 0) & (i > 0)))(lambda: step(True))

        @pl.when(i == nt - 1)
        def _():
            dk_ref[...] = jnp.where(lo, dk_acc[0], dk_acc[1]).astype(BF16)
            dv_ref[...] = jnp.where(lo, dv_acc[0], dv_acc[1]).astype(BF16)
            dkx_ref[...] = jnp.where(lo, dk_acc[1], dk_acc[0])

        @pl.when(s_ == ns - 1)
        def _():
            dq_ref[...] = (jnp.where(lo, dq_acc[0], dq_acc[1]) * SCALE).astype(BF16)
            dqx_ref[...] = jnp.where(lo, dq_acc[1], dq_acc[0])

        if scatter:
            pl.when((p_ == NH // 2 - 1) & (s_ == ns - 1))(finish)

    qs = lambda e: pl.BlockSpec((None, t, BLK), lambda p, s, ii, jj: (2 * p + e, ii[s], 0))
    ks = lambda e: pl.BlockSpec((None, t, BLK), lambda p, s, ii, jj: (2 * p + e, jj[s], 0))
    kts = lambda e: pl.BlockSpec((None, BLK, t), lambda p, s, ii, jj: (2 * p + e, 0, jj[s]))
    qside = pl.BlockSpec((t, BLK), lambda p, s, ii, jj: (ii[s], p))
    kside = pl.BlockSpec((t, BLK), lambda p, s, ii, jj: (jj[s], p))
    rep = pl.BlockSpec((None, t, BLK), lambda p, s, ii, jj: (p, ii[s], 0))
    whole = pl.BlockSpec((T, BLK), lambda p, s, ii, jj: (0, p))
    sums = list(scatter[0]) if scatter else []
    return pl.pallas_call(
        body,
        grid_spec=pltpu.PrefetchScalarGridSpec(
            num_scalar_prefetch=2, grid=(NH // 2, ns),
            in_specs=[qs(0), qs(1), ks(0), ks(1), kts(0), kts(1), ks(0), ks(1), qside, rep, rep, rep, rep] + [ANY] * ng,
            out_specs=[whole, whole, kside, kside, kside] + [ANY] * ng,
            scratch_shapes=[pltpu.VMEM((2, T, BLK), F32), pltpu.VMEM((2, t, BLK), F32), pltpu.VMEM((2, t, BLK), F32)]
            + ([pltpu.SemaphoreType.DMA((3 * ng,)), pltpu.SemaphoreType.DMA((3 * ng,))] if scatter else [])),
        out_shape=[jax.ShapeDtypeStruct((T, 512), BF16), jax.ShapeDtypeStruct((T, 512), F32),
                   jax.ShapeDtypeStruct((T, 512), BF16), jax.ShapeDtypeStruct((T, 512), BF16),
                   jax.ShapeDtypeStruct((T, 512), F32)] + (_scatter_shapes(sums, scatter[1]) if scatter else []),
        compiler_params=(pltpu.CompilerParams(dimension_semantics=("arbitrary",) * 2, vmem_limit_bytes=VMEM_LIMIT,
                                              has_side_effects=True) if scatter
                         else _cp("parallel", "arbitrary")), name=name)(
            i_of, j_of, qaug, qaug, kaug, kaug, kaug_t, kaug_t, vm, vm, do, *lses, *deltas, *sums)


def _fox_mask(i, j, t):
    row = i * t + lax.broadcasted_iota(jnp.int32, (t, t), 0)
    col = j * t + lax.broadcasted_iota(jnp.int32, (t, t), 1)
    return (col <= row) & (col >= NPAD)


def _lane_pick(x, lane, idx):
    return jnp.sum(jnp.where(lane == idx, x, 0.0), axis=-1, keepdims=True)


def _lru_gates(xc, wr_ref, wi_ref, vec_ref):
    xb = xc.astype(BF16)
    pre_r = jnp.concatenate([_dot(xb[:, p * BLK:(p + 1) * BLK], wr_ref[p], 1, 0) for p in range(LW // BLK)], axis=1)
    pre_i = jnp.concatenate([_dot(xb[:, p * BLK:(p + 1) * BLK], wi_ref[p], 1, 0) for p in range(LW // BLK)], axis=1)
    r = _sigmoid(pre_r + vec_ref[0:1, :])
    gi = _sigmoid(pre_i + vec_ref[1:2, :])
    log_a = LRU_C * r * _log_sigmoid(vec_ref[2:3, :])
    a = jnp.exp(log_a)
    mult = jnp.sqrt(_neg_expm1(2.0 * log_a))
    return r, gi, a, mult


def _conv(xbuf_ref, x, cw_ref, vec_ref, tr):
    return (cw_ref[3:4, :] * x + cw_ref[2:3, :] * xbuf_ref[7:7 + tr, :] + cw_ref[1:2, :] * xbuf_ref[6:6 + tr, :]
            + cw_ref[0:1, :] * xbuf_ref[5:5 + tr, :] + vec_ref[3:4, :])


def _lru_fwd(proj, cw, wr, wi, vec, name):
    T = proj.shape[0]
    tr = _rt(T)

    def body(x_ref, y_ref, cw_ref, wr_ref, wi_ref, vec_ref, oc_ref, hs_ref, xbuf, abuf, bbuf, hcar):
        i = pl.program_id(0)

        @pl.when(i == 0)
        def _():
            xbuf[0:8, :] = jnp.zeros((8, LW), F32)
            hcar[...] = jnp.zeros_like(hcar)

        x = x_ref[...]
        xbuf[8:8 + tr, :] = x
        xc = _conv(xbuf, x, cw_ref, vec_ref, tr)
        xbuf[0:8, :] = x[tr - 8:tr, :]
        _, gi, a, mult = _lru_gates(xc, wr_ref, wi_ref, vec_ref)
        rows = i * tr + lax.broadcasted_iota(jnp.int32, (tr, 1), 0)
        abuf[...] = a
        bbuf[...] = jnp.where(rows >= NPAD, mult * (gi * xc), 0.0)
        sub = lax.broadcasted_iota(jnp.int32, (8, 1), 0)

        def step(k, h):
            sl = pl.ds(pl.multiple_of(k * 8, 8), 8)
            a8, b8 = abuf[sl, :], bbuf[sl, :]
            for s in (1, 2, 4):
                ok = sub >= s
                b8 = jnp.where(ok, a8 * pltpu.roll(b8, s, 0) + b8, b8)
                a8 = jnp.where(ok, a8 * pltpu.roll(a8, s, 0), a8)
            h8 = a8 * h + b8
            bbuf[sl, :] = h8
            return h8[7:8, :]

        hcar[...] = lax.fori_loop(0, tr // 8, step, hcar[...])
        hs = bbuf[...]
        hs_ref[...] = hs
        oc_ref[...] = (hs * _gelu(y_ref[...])).astype(BF16)

    row = pl.BlockSpec((tr, LW), lambda i: (i, 0))
    full = lambda shape: pl.BlockSpec(shape, lambda i: (0,) * len(shape))
    return pl.pallas_call(
        body, grid=(T // tr,),
        in_specs=[pl.BlockSpec((tr, LW), lambda i: (i, XC // LW)), pl.BlockSpec((tr, LW), lambda i: (i, YC // LW)),
                  full((4, LW)), full((4, BLK, BLK)), full((4, BLK, BLK)), full((8, LW))],
        out_specs=[row, row], out_shape=[jax.ShapeDtypeStruct((T, LW), BF16), jax.ShapeDtypeStruct((T, LW), F32)],
        scratch_shapes=[pltpu.VMEM((tr + 8, LW), F32), pltpu.VMEM((tr, LW), F32), pltpu.VMEM((tr, LW), F32),
                        pltpu.VMEM((1, LW), F32)],
        compiler_params=_cp("arbitrary"), name=name)(proj, proj, cw, wr, wi, vec)


def _lru_bwd(proj, hs, doc, cw, wr, wi, vec, name):
    T = proj.shape[0]
    tr = _rt(T)
    nt = T // tr
    r8 = tr // 8

    def body(x_ref, xp_ref, y_ref, hs_ref, hp_ref, do_ref, cw_ref, wr_ref, wi_ref, vec_ref,
             dx_ref, dy_ref, dwr_ref, dwi_ref, dvec_ref, xbuf, abuf, gbuf, hbuf, dbuf, gcar, acar):
        k = pl.program_id(0)
        i = nt - 1 - k

        @pl.when(k == 0)
        def _():
            dwr_ref[...] = jnp.zeros_like(dwr_ref)
            dwi_ref[...] = jnp.zeros_like(dwi_ref)
            dvec_ref[...] = jnp.zeros_like(dvec_ref)
            gcar[...] = jnp.zeros_like(gcar)
            acar[...] = jnp.zeros_like(acar)
            dbuf[tr:tr + 8, :] = jnp.zeros((8, LW), F32)

        first = i == 0
        x = x_ref[...]
        xbuf[0:8, :] = jnp.where(first, 0.0, xp_ref[...])
        xbuf[8:8 + tr, :] = x
        xc = _conv(xbuf, x, cw_ref, vec_ref, tr)
        r, gi, a, mult = _lru_gates(xc, wr_ref, wi_ref, vec_ref)
        y = y_ref[...]
        hs = hs_ref[...]
        do_ = do_ref[...].astype(F32)
        rows = i * tr + lax.broadcasted_iota(jnp.int32, (tr, 1), 0)
        abuf[0:tr, :] = a
        abuf[tr:tr + 8, :] = jnp.zeros((8, LW), F32) + acar[...]
        an = abuf[1:1 + tr, :]
        acar[...] = a[0:1, :]
        abuf[0:tr, :] = an
        gbuf[...] = do_ * _gelu(y)
        sub = lax.broadcasted_iota(jnp.int32, (8, 1), 0)

        def step(kk, g):
            sl = pl.ds(pl.multiple_of((r8 - 1 - kk) * 8, 8), 8)
            a8, b8 = abuf[sl, :], gbuf[sl, :]
            for s in (1, 2, 4):
                ok = sub < 8 - s
                b8 = jnp.where(ok, a8 * pltpu.roll(b8, 8 - s, 0) + b8, b8)
                a8 = jnp.where(ok, a8 * pltpu.roll(a8, 8 - s, 0), a8)
            g8 = a8 * g + b8
            gbuf[sl, :] = g8
            return g8[0:1, :]

        gcar[...] = lax.fori_loop(0, r8, step, gcar[...])
        g = gbuf[...]
        hbuf[0:8, :] = jnp.where(first, 0.0, hp_ref[...])
        hbuf[8:8 + tr, :] = hs
        hprev = hbuf[7:7 + tr, :]
        dinp = jnp.where(rows >= NPAD, g, 0.0)
        da = g * hprev
        dmult = dinp * gi * xc
        dgi = dinp * mult * xc
        dxc = dinp * mult * gi
        dlog_a = da * a - dmult * a * a / mult
        ls = _log_sigmoid(vec_ref[2:3, :])
        dpre_r = dlog_a * (LRU_C * ls) * r * (1.0 - r)
        dpre_i = dgi * gi * (1.0 - gi)
        xb = xc.astype(BF16)
        rb, ib = dpre_r.astype(BF16), dpre_i.astype(BF16)
        back = []
        for p in range(LW // BLK):
            c = slice(p * BLK, (p + 1) * BLK)
            back.append(_dot(rb[:, c], wr_ref[p], 1, 1) + _dot(ib[:, c], wi_ref[p], 1, 1))
            dwr_ref[p] += _dot(xb[:, c], rb[:, c], 0, 0)
            dwi_ref[p] += _dot(xb[:, c], ib[:, c], 0, 0)
        dxc = dxc + jnp.concatenate(back, axis=1)
        col = lambda v: jnp.sum(v, axis=0, keepdims=True)
        dvec_ref[0:1, :] += col(dpre_r)
        dvec_ref[1:2, :] += col(dpre_i)
        dvec_ref[2:3, :] += col(dlog_a * (LRU_C * r)) * _sigmoid(-vec_ref[2:3, :])
        dvec_ref[3:4, :] += col(dxc)
        dvec_ref[4:5, :] += col(dxc * xbuf[5:5 + tr, :])
        dvec_ref[5:6, :] += col(dxc * xbuf[6:6 + tr, :])
        dvec_ref[6:7, :] += col(dxc * xbuf[7:7 + tr, :])
        dvec_ref[7:8, :] += col(dxc * x)
        dbuf[0:tr, :] = dxc
        dxr = (cw_ref[3:4, :] * dxc + cw_ref[2:3, :] * dbuf[1:1 + tr, :] + cw_ref[1:2, :] * dbuf[2:2 + tr, :]
               + cw_ref[0:1, :] * dbuf[3:3 + tr, :])
        dbuf[tr:tr + 8, :] = dxc[0:8, :]
        dx_ref[...] = jnp.where(rows >= NPAD, dxr, 0.0).astype(BF16)
        dy_ref[...] = (do_ * hs * _gelu_grad(y)).astype(BF16)

    rev = lambda k: nt - 1 - k
    row = lambda col0: pl.BlockSpec((tr, LW), lambda k: (rev(k), col0))
    prev8 = lambda col0: pl.BlockSpec((8, LW), lambda k: (jnp.maximum(rev(k) * r8 - 1, 0), col0))
    full = lambda shape: pl.BlockSpec(shape, lambda k: (0,) * len(shape))
    return pl.pallas_call(
        body, grid=(nt,),
        in_specs=[row(XC // LW), prev8(XC // LW), row(YC // LW), row(0), prev8(0), row(0),
                  full((4, LW)), full((4, BLK, BLK)), full((4, BLK, BLK)), full((8, LW))],
        out_specs=[row(0), row(0), full((4, BLK, BLK)), full((4, BLK, BLK)), full((8, LW))],
        out_shape=[jax.ShapeDtypeStruct((T, LW), BF16), jax.ShapeDtypeStruct((T, LW), BF16),
                   jax.ShapeDtypeStruct((4, BLK, BLK), F32), jax.ShapeDtypeStruct((4, BLK, BLK), F32),
                   jax.ShapeDtypeStruct((8, LW), F32)],
        scratch_shapes=[pltpu.VMEM((tr + 8, LW), F32), pltpu.VMEM((tr + 8, LW), F32), pltpu.VMEM((tr, LW), F32),
                        pltpu.VMEM((tr + 8, LW), F32), pltpu.VMEM((tr + 8, LW), F32),
                        pltpu.VMEM((1, LW), F32), pltpu.VMEM((1, LW), F32)],
        compiler_params=_cp("arbitrary"), name=name)(proj, proj, proj, hs, hs, doc, cw, wr, wi, vec)


def _branch_merge_fwd(oa, of, oc, wb, proj, name):
    T = proj.shape[0]
    tm, tn = _rt(T), 512

    def body(a0, a1, a2, w_ref, g0, g1, g2, r0, r1, r2, m_ref, mt_ref):
        acc = None
        for g, (a_ref, g_ref, r_ref) in enumerate(((a0, g0, r0), (a1, g1, r1), (a2, g2, r2))):
            b = _dot(a_ref[...], w_ref[g], 1, 0)
            r_ref[...] = b
            term = _sigmoid(g_ref[...]) * b
            acc = term if acc is None else acc + term
        m = acc.astype(BF16)
        m_ref[...] = m
        mt_ref[...] = m.T

    act = pl.BlockSpec((tm, LW), lambda j, i: (i, 0))
    gate = lambda g: pl.BlockSpec((tm, tn), lambda j, i: (i, (GT + g * D) // tn + j))
    blk = pl.BlockSpec((tm, tn), lambda j, i: (i, j))
    return pl.pallas_call(
        body, grid=(D // tn, T // tm),
        in_specs=[act, act, act, pl.BlockSpec((3, LW, tn), lambda j, i: (0, 0, j)), gate(0), gate(1), gate(2)],
        out_specs=[blk] * 4 + [pl.BlockSpec((tn, tm), lambda j, i: (j, i))],
        out_shape=[jax.ShapeDtypeStruct((T, D), F32)] * 3 + [jax.ShapeDtypeStruct((T, D), BF16), jax.ShapeDtypeStruct((D, T), BF16)],
        compiler_params=_cp("parallel", "parallel"), name=name)(oa, of, oc, wb, proj, proj, proj)


def _out_dx_merge_bwd(dhb, w_out, proj, b0, b1, b2, name):
    T = proj.shape[0]
    tm, tn = _rt(T), 512

    def body(dh_ref, w_ref, g0, g1, g2, r0, r1, r2, d0, d1, d2, e0, e1, e2):
        dmv = _dot(dh_ref[...], w_ref[...], 1, 1)
        for g_ref, r_ref, d_ref, e_ref in ((g0, r0, d0, e0), (g1, r1, d1, e1), (g2, r2, d2, e2)):
            sg = _sigmoid(g_ref[...])
            d_ref[...] = (dmv * sg).astype(BF16)
            e_ref[...] = (dmv * r_ref[...] * sg * (1.0 - sg)).astype(BF16)

    gate = lambda g: pl.BlockSpec((tm, tn), lambda j, i: (i, (GT + g * D) // tn + j))
    blk = pl.BlockSpec((tm, tn), lambda j, i: (i, j))
    return pl.pallas_call(
        body, grid=(D // tn, T // tm),
        in_specs=[pl.BlockSpec((tm, D), lambda j, i: (i, 0)), pl.BlockSpec((tn, D), lambda j, i: (j, 0)),
                  gate(0), gate(1), gate(2), blk, blk, blk],
        out_specs=[blk] * 6, out_shape=[jax.ShapeDtypeStruct((T, D), BF16)] * 6,
        compiler_params=_cp("parallel", "parallel"), name=name)(dhb, w_out, proj, proj, proj, b0, b1, b2)


def _ffn_in_swiglu_fwd(u, w, name):
    T = u.shape[0]
    tm, tn = _rt(T), _pick(DFF, (1408, 256))
    nj = DFF // tn

    def body(u_ref, wg_ref, wu_ref, g_ref, up_ref, a_ref, at_ref):
        ub = u_ref[...]
        g = _dot(ub, wg_ref[...], 1, 0)
        up = _dot(ub, wu_ref[...], 1, 0)
        g_ref[...] = g
        up_ref[...] = up
        a = (g * _sigmoid(g) * up).astype(BF16)
        a_ref[...] = a
        at_ref[...] = a.T

    blk = pl.BlockSpec((tm, tn), lambda j, i: (i, j))
    return pl.pallas_call(
        body, grid=(nj, T // tm),
        in_specs=[pl.BlockSpec((tm, D), lambda j, i: (i, 0)), pl.BlockSpec((D, tn), lambda j, i: (0, j)),
                  pl.BlockSpec((D, tn), lambda j, i: (0, j + nj))],
        out_specs=[blk] * 3 + [pl.BlockSpec((tn, tm), lambda j, i: (j, i))],
        out_shape=[jax.ShapeDtypeStruct((T, DFF), F32)] * 2 + [jax.ShapeDtypeStruct((T, DFF), BF16),
                                                               jax.ShapeDtypeStruct((DFF, T), BF16)],
        compiler_params=_cp("parallel", "parallel"), name=name)(u, w, w)


def _ffn_out_dx_swiglu_bwd(dhb, w, gate, up, name):
    T = dhb.shape[0]
    tm, tn = _rt(T), _pick(DFF, (1408, 256))

    def body(dh_ref, w_ref, g_ref, up_ref, dg_ref, du_ref):
        d = _dot(dh_ref[...], w_ref[...], 1, 1)
        g = g_ref[...]
        sg = _sigmoid(g)
        dg_ref[...] = (d * up_ref[...] * (sg + g * sg * (1.0 - sg))).astype(BF16)
        du_ref[...] = (d * g * sg).astype(BF16)

    blk = pl.BlockSpec((tm, tn), lambda j, i: (i, j))
    return pl.pallas_call(
        body, grid=(DFF // tn, T // tm),
        in_specs=[pl.BlockSpec((tm, D), lambda j, i: (i, 0)), pl.BlockSpec((tn, D), lambda j, i: (j, 0)), blk, blk],
        out_specs=[blk] * 2, out_shape=[jax.ShapeDtypeStruct((T, DFF), BF16)] * 2,
        compiler_params=_cp("parallel", "parallel"), name=name)(dhb, w, gate, up)


def _adamw(w, g, m, v, name):
    R, C = w.shape
    tr = _pick(R, tuple(t for t in (512, 256, 128, 64, 32, 16, 8) if t * C * 4 <= (3 << 19)))
    c1 = 1.0 - ADAM_B1 ** ADAM_STEP
    c2 = 1.0 - ADAM_B2 ** ADAM_STEP

    def body(w_ref, g_ref, m_ref, v_ref, d_ref, mo_ref, vo_ref):
        gv = g_ref[...]
        mn = ADAM_B1 * m_ref[...] + (1.0 - ADAM_B1) * gv
        vn = ADAM_B2 * v_ref[...] + (1.0 - ADAM_B2) * (gv * gv)
        d_ref[...] = -ADAM_LR * ((mn / c1) / (jnp.sqrt(vn / c2) + ADAM_EPS) + ADAM_WD * w_ref[...])
        mo_ref[...] = mn
        vo_ref[...] = vn

    blk = pl.BlockSpec((tr, C), lambda i: (i, 0))
    return pl.pallas_call(
        body, grid=(R // tr,), in_specs=[blk] * 4, out_specs=[blk] * 3,
        out_shape=[jax.ShapeDtypeStruct((R, C), F32)] * 3, compiler_params=_cp("parallel"), name=name)(w, g, m, v)


def _sum_lead(x, name):
    n, R, C = x.shape
    tr = _pick(R, (512, 256, 128, 64, 32, 16, 8))

    def body(x_ref, o_ref):
        acc = x_ref[0]
        for d in range(1, n):
            acc = acc + x_ref[d]
        o_ref[...] = acc

    return pl.pallas_call(
        body, grid=(R // tr,), in_specs=[pl.BlockSpec((n, tr, C), lambda i: (0, i, 0))],
        out_specs=pl.BlockSpec((tr, C), lambda i: (i, 0)), out_shape=jax.ShapeDtypeStruct((R, C), F32),
        compiler_params=_cp("parallel"), name=name)(x)


def _here():
    return lax.axis_index("x"), lax.axis_index("y"), lax.axis_index("c")


def _rcopy(src, dst, send_sems, recv_sems, k, to):
    return pltpu.make_async_remote_copy(src_ref=src, dst_ref=dst, send_sem=send_sems.at[k], recv_sem=recv_sems.at[k],
                                        device_id=to, device_id_type=MESH)


def _hbm_calls(body, args, out_shapes, n_sems, aliases, name):
    return pl.pallas_call(
        body, in_specs=[ANY] * len(args), out_specs=[ANY] * len(out_shapes), out_shape=out_shapes,
        input_output_aliases=aliases,
        scratch_shapes=[pltpu.SemaphoreType.DMA((n_sems,)), pltpu.SemaphoreType.DMA((n_sems,))],
        compiler_params=pltpu.CompilerParams(has_side_effects=True), name=name)(*args)


def _gather_plan(outs, axes, send_sems, recv_sems):
    x, y, c = _here()
    sib = (x, y, 1 - c)
    chips = [(1 - x, y), (x, 1 - y), (1 - x, 1 - y)]
    todo = [(t, k, chip) for t in range(len(outs)) for k, chip in enumerate(chips)]

    def win(t, chip, hc):
        o, ax = outs[t], axes[t]
        w = o.shape[ax] // N_SHARD
        first = (2 * chip[0] + chip[1]) * w
        if ax == 0:
            return o.at[pl.ds(pl.multiple_of(first + hc * (w // 2), 16), w // 2), :]
        rows = o.shape[0] // 2
        return o.at[pl.ds(pl.multiple_of(hc * rows, 16), rows), pl.ds(pl.multiple_of(first, BLK), w)]

    def copy(t, k, chip, hc, to):
        return _rcopy(win(t, chip, hc), win(t, chip, hc), send_sems, recv_sems, 6 * t + k, to)

    def start():
        for t, k, chip in todo:
            copy(t, k, (x, y), c, (*chip, c)).start()

    def finish():
        for t, k, chip in todo:
            copy(t, k, chip, c, (*chip, c)).wait_recv()
            copy(t, 3 + k, chip, c, sib).start()
        for t, k, chip in todo:
            copy(t, 3 + k, chip, 1 - c, sib).wait_recv()
        for t, k, chip in todo:
            copy(t, k, (x, y), c, (*chip, c)).wait_send()
            copy(t, 3 + k, chip, c, sib).wait_send()

    return start, finish


def _all_gather_weights(fulls, axes, name):
    nt = len(fulls)

    def body(*refs):
        start, finish = _gather_plan(refs[nt:2 * nt], axes, *refs[2 * nt:])
        start()
        finish()

    return _hbm_calls(body, fulls, [jax.ShapeDtypeStruct(f.shape, f.dtype) for f in fulls], 6 * nt,
                      {t: t for t in range(nt)}, name)


def _half(ref, ax, hc):
    n = ref.shape[ax] // 2
    sl = pl.ds(pl.multiple_of(hc * n, 8), n)
    return ref.at[sl, :] if ax == 0 else ref.at[:, sl]


def _shrunk(shape, ax, by):
    shape = list(shape)
    shape[ax] //= by
    return tuple(shape)


def _swap_halves(gs, haxes, name):
    nt = len(gs)

    def body(*refs):
        ins, outs, (send_sems, recv_sems) = refs[:nt], refs[nt:2 * nt], refs[2 * nt:]
        x, y, c = _here()
        cps = [_rcopy(_half(g, ax, 1 - c), o, send_sems, recv_sems, t, (x, y, 1 - c))
               for t, (g, o, ax) in enumerate(zip(ins, outs, haxes))]
        for cp in cps:
            cp.start()
        for cp in cps:
            cp.wait()

    return _hbm_calls(body, gs, [jax.ShapeDtypeStruct(_shrunk(g.shape, ax, 2), g.dtype) for g, ax in zip(gs, haxes)],
                      nt, {}, name)


def _scatter_plan(ins, outs, saxes, send_sems, recv_sems):
    x, y, c = _here()
    chips = [(1 - x, y), (x, 1 - y), (1 - x, 1 - y)]

    def copies():
        cps = []
        for t, (s, o, ax) in enumerate(zip(ins, outs, saxes)):
            w = s.shape[ax] // N_SHARD
            for k, chip in enumerate(chips):
                first = pl.multiple_of((2 * chip[0] + chip[1]) * w, 8)
                src = s.at[pl.ds(first, w), :] if ax == 0 else s.at[:, pl.ds(first, w)]
                cps.append(_rcopy(src, o.at[k], send_sems, recv_sems, 3 * t + k, (*chip, c)))
        return cps

    def start():
        for cp in copies():
            cp.start()

    def finish():
        for cp in copies():
            cp.wait()

    return start, finish


def _scatter_shapes(sbs, saxes):
    return [jax.ShapeDtypeStruct((3,) + _shrunk(s.shape, ax, N_SHARD), s.dtype) for s, ax in zip(sbs, saxes)]


def _scatter_to_chips(sbs, saxes, name):
    nt = len(sbs)

    def body(*refs):
        start, finish = _scatter_plan(refs[:nt], refs[nt:2 * nt], saxes, *refs[2 * nt:])
        start()
        finish()

    return _hbm_calls(body, sbs, _scatter_shapes(sbs, saxes), 3 * nt, {}, name)


def _join_halves(fins, haxes, name):
    nt = len(fins)

    def body(*refs):
        outs, (send_sems, recv_sems) = refs[nt:2 * nt], refs[2 * nt:]
        x, y, c = _here()
        cps = [_rcopy(_half(o, ax, c), _half(o, ax, c), send_sems, recv_sems, t, (x, y, 1 - c))
               for t, (o, ax) in enumerate(zip(outs, haxes))]
        for cp in cps:
            cp.start()
        for t, (o, ax) in enumerate(zip(outs, haxes)):
            _rcopy(_half(o, ax, 1 - c), _half(o, ax, 1 - c), send_sems, recv_sems, t, (x, y, 1 - c)).wait_recv()
        for cp in cps:
            cp.wait_send()

    return _hbm_calls(body, fins, [jax.ShapeDtypeStruct(f.shape, f.dtype) for f in fins], nt, {t: t for t in range(nt)}, name)


def _all_gather_small(buf, name):
    def body(_, out_ref, send_sems, recv_sems):
        x, y, c = _here()
        me = 4 * x + 2 * y + c
        cps = []
        for k in range(1, 8):
            to = (x ^ ((k >> 2) & 1), y ^ ((k >> 1) & 1), c ^ (k & 1))
            peer = 4 * to[0] + 2 * to[1] + to[2]
            cps.append((_rcopy(out_ref.at[me], out_ref.at[me], send_sems, recv_sems, k - 1, to),
                        _rcopy(out_ref.at[peer], out_ref.at[peer], send_sems, recv_sems, k - 1, to)))
        for snd, _ in cps:
            snd.start()
        for _, rcv in cps:
            rcv.wait_recv()
        for snd, _ in cps:
            snd.wait_send()

    return _hbm_calls(body, [buf], [jax.ShapeDtypeStruct(buf.shape, buf.dtype)], 7, {0: 0}, name)[0]


def _place(block, n, index):
    buf = jnp.zeros((n,) + block.shape[1:], block.dtype)
    return lax.dynamic_update_slice_in_dim(buf, block, index, axis=0)


def _add_half(g, other, hax, cidx, name):
    r, cw = other.shape
    tr = _pick(r, tuple(t for t in (512, 256, 128, 64, 32, 16, 8) if t * cw * 4 <= (1 << 21)))
    nr = r // tr

    def body(c_ref, g_ref, o_ref, s_ref, sb_ref):
        s = g_ref[...] + o_ref[...]
        s_ref[...] = s
        sb_ref[...] = s.astype(BF16)

    g_map = (lambda i, c: (c[0] * nr + i, 0)) if hax == 0 else (lambda i, c: (i, c[0]))
    blk = pl.BlockSpec((tr, cw), lambda i, c: (i, 0))
    return pl.pallas_call(
        body,
        grid_spec=pltpu.PrefetchScalarGridSpec(
            num_scalar_prefetch=1, grid=(nr,), in_specs=[pl.BlockSpec((tr, cw), g_map), blk], out_specs=[blk, blk]),
        out_shape=[jax.ShapeDtypeStruct((r, cw), F32), jax.ShapeDtypeStruct((r, cw), BF16)],
        compiler_params=_cp("parallel"), name=name)(cidx, g, other)


def _add_chips(s, recv, sax, chip_idx, name):
    _, r, cw = recv.shape
    tr = _pick(r, tuple(t for t in (512, 352, 256, 128, 64, 32, 16, 8) if t * cw * 4 <= (1 << 21)))
    nr = r // tr

    def body(c_ref, s_ref, r_ref, out_ref):
        out_ref[...] = ((s_ref[...] + r_ref[0].astype(F32)) + r_ref[1].astype(F32)) + r_ref[2].astype(F32)

    s_map = (lambda i, c: (c[0] * nr + i, 0)) if sax == 0 else (lambda i, c: (i, c[0]))
    return pl.pallas_call(
        body,
        grid_spec=pltpu.PrefetchScalarGridSpec(
            num_scalar_prefetch=1, grid=(nr,),
            in_specs=[pl.BlockSpec((tr, cw), s_map), pl.BlockSpec((3, tr, cw), lambda i, c: (0, i, 0))],
            out_specs=pl.BlockSpec((tr, cw), lambda i, c: (i, 0))),
        out_shape=jax.ShapeDtypeStruct((r, cw), F32), compiler_params=_cp("parallel"), name=name)(chip_idx, s, recv)


IN_SHARD = IN_COLS // N_SHARD
IN_SLOT = INP // N_SHARD
IN_PIECES = ((0, 512, QA), (512, 640, KA), (640, 768, VA), (768, 1280, QF), (1280, 1792, KF), (1792, 2304, VF),
             (2304, 2312, FL), (2312, 2824, XC), (2824, 3336, YC), (3336, 6408, GT))


def _gathered_to_kernel_cols(w):
    parts, pos = [], 0
    for a, b, k in sorted(IN_PIECES, key=lambda p: p[2]):
        assert k == pos
        while a < b:
            j = a // IN_SHARD
            e = min(b, (j + 1) * IN_SHARD)
            g = j * IN_SLOT + a - j * IN_SHARD
            parts.append(w[..., g:g + e - a])
            pos += e - a
            a = e
    parts.append(jnp.zeros(w.shape[:-1] + (INP - pos,), w.dtype))
    return jnp.concatenate(parts, axis=-1)


def _kernel_to_gathered_cols(w):
    parts = []
    for j in range(N_SHARD):
        lo, hi = j * IN_SHARD, (j + 1) * IN_SHARD
        for a, b, k in IN_PIECES:
            s, e = max(a, lo), min(b, hi)
            if s < e:
                parts.append(w[..., k + s - a:k + e - a])
        parts.append(jnp.zeros(w.shape[:-1] + (IN_SLOT - IN_SHARD,), w.dtype))
    return jnp.concatenate(parts, axis=-1)


def _pair_blocks(w):
    z = jnp.zeros((4, 64, 64), w.dtype)
    w = w.reshape(4, 2, 64, 64)
    top = jnp.concatenate([w[:, 0], z], axis=2)
    bot = jnp.concatenate([z, w[:, 1]], axis=2)
    return jnp.concatenate([top, bot], axis=1)


def _unpair_blocks(w):
    return jnp.stack([w[:, :64, :64], w[:, 64:, 64:]], axis=1).reshape(8, 64, 64)


BIG = ("w_in", "w_branch", "w_out", "w_ffn_in", "w_ffn_out")
TINY = ("conv_w", "meta_tokens")
SMALL = ("rel_bias_table", "norm_mix", "swa_sinks", "fox_forget_bias", "conv_b", "lru_w_r", "lru_b_r", "lru_w_i",
         "lru_b_i", "lru_lambda", "norm_ffn", "norm_final")
SHARD_AXIS = {"conv_w": 2, "meta_tokens": 1}
BIG_AXIS = {"w_in": 2, "w_branch": 2, "w_out": 1, "w_ffn_in": 2, "w_ffn_out": 1}


def _pack(d, names):
    flat = jnp.concatenate([d[n].reshape(-1) for n in names])
    pad = (-flat.shape[0]) % (256 * 128)
    return jnp.concatenate([flat, jnp.zeros((pad,), F32)]).reshape(-1, 128)


def _unpack(buf, names, shapes):
    flat, out, off = buf.reshape(-1), {}, 0
    for n in names:
        sz = int(np.prod(shapes[n]))
        out[n] = flat[off:off + sz].reshape(shapes[n])
        off += sz
    return out


def _layer_layout(n, a):
    if n == "w_in":
        return _gathered_to_kernel_cols(a)
    return a.reshape(3, LW, D) if n == "w_branch" else a


def _local_step(x, tgt, W, placed=None):
    S = x.shape[0]
    T = S + BLK
    tm = _pick(T, (1408, 384, 128))
    bucket = jnp.asarray(_bucket_table())
    bias = _bias_build(W["rel_bias_table"], bucket, "bias_build")
    h = jnp.concatenate([jnp.zeros((NPAD, D), F32), W["meta_tokens"], x], axis=0)
    if placed is None:
        WL = {n: [W[n][l] for l in range(DEPTH)] for n in BIG}
    else:
        WL = {n: [W[n]] + [None] * (DEPTH - 1) for n in BIG}

    saved = []
    for l in range(DEPTH):
        sv = {"h0": h}
        u, u_t = _rms_fwd(h, W["norm_mix"][l], f"rms_mix_fwd")
        proj = _mm(u, WL["w_in"][l], tm=tm, tn=512, tk=D, name="mm_in_fwd")
        oa = _swa_fwd(proj, bias, W["swa_sinks"][l], "swa_fwd")
        fb = W["fox_forget_bias"][l].reshape(NH, 1)
        qaug, kaug, kaug_t, vm, vo = _fox_prep(proj, _cum_fwd(proj, fb, "cum_fwd"), "fox_prep")
        if placed is not None and l + 1 < DEPTH:
            of, lse0, lse1, *got = _fox_fwd(qaug, kaug_t, vo, "fox_fwd_gather", gather=(placed[l + 1], GATHER_AXES))
            for n, a in zip(BIG, got):
                WL[n][l + 1] = _layer_layout(n, a)
            lse = [lse0, lse1]
        else:
            of, *lse = _fox_fwd(qaug, kaug_t, vo, "fox_fwd")
        lru_vec = jnp.concatenate([W["lru_b_r"][l][None], W["lru_b_i"][l][None], W["lru_lambda"][l][None],
                                   W["conv_b"][l][None], jnp.zeros((4, LW), F32)], axis=0)
        oc, hs = _lru_fwd(proj, W["conv_w"][l], W["lru_w_r"][l], W["lru_w_i"][l], lru_vec, "lru_fwd")
        *bs, merged, merged_t = _branch_merge_fwd(oa, of, oc, WL["w_branch"][l], proj, "branch_merge_fwd")
        h2 = _mm(merged, WL["w_out"][l], res=h, tm=tm, tn=512, tk=D, name="mm_out_fwd")
        u2, u2_t = _rms_fwd(h2, W["norm_ffn"][l], "rms_ffn_fwd")
        gate, up, act, act_t = _ffn_in_swiglu_fwd(u2, WL["w_ffn_in"][l], "ffn_in_swiglu_fwd")
        h = _mm(act, WL["w_ffn_out"][l], res=h2, tm=tm, tn=512, tk=_pick(DFF, (1408, 256)), name="mm_ffn_out_fwd")
        sv.update(u_t=u_t, proj=proj, oa=oa, of=of, oc=oc, lse=lse, hs=hs, fb=fb, qaug=qaug, kaug=kaug, kaug_t=kaug_t, vm=vm,
                  lru_vec=lru_vec, bs=bs, merged_t=merged_t, h2=h2, u2_t=u2_t, gate=gate, up=up, act_t=act_t)
        saved.append(sv)

    tgt_pad = tgt
    dh, dhb, dg_final, loss_vec = _loss_head(h, tgt_pad, W["norm_final"], "loss_head")
    loss = loss_vec[0, 0]

    small = ("norm_mix", "swa_sinks", "fox_forget_bias", "conv_w", "conv_b", "lru_w_r", "lru_b_r", "lru_w_i", "lru_b_i",
             "lru_lambda", "norm_ffn")
    G = {n: [None] * DEPTH for n in small}
    G["norm_final"] = dg_final.reshape(D)
    GW = {n: [None] * DEPTH for n in BIG}
    dist = placed is not None
    if dist:
        x_, y_, c_ = _here()
        cidx = jnp.reshape(c_, (1,)).astype(jnp.int32)
        chip = jnp.reshape(2 * x_ + y_, (1,)).astype(jnp.int32)

    def finish_layer(lp, ss, recv):
        fins = []
        for n, s, r, ax, hax in zip(BIG, ss, recv, GATHER_AXES, HALF_AXES):
            tot = _add_chips(s, r, ax, chip, "rs_add_chips_" + n)
            zero = jnp.zeros_like(tot)
            fins.append(jnp.concatenate([jnp.where(c_ == hc, tot, zero) for hc in range(2)], axis=hax))
        for n, f in zip(BIG, _join_halves(fins, HALF_AXES, "rs_join_halves")):
            GW[n][lp] = f

    pend = None
    dbias = jnp.zeros((NH, BLK, 2 * BLK), F32)
    tkT = tm
    for l in reversed(range(DEPTH)):
        sv = saved[l]
        dw = {}
        dw["w_ffn_out"] = _mm(sv["act_t"], dhb, tm=_pick(DFF, (1408, 256)), tn=D, tk=tkT,
                              name="mm_ffn_out_dw")
        dgate, dup = _ffn_out_dx_swiglu_bwd(dhb, WL["w_ffn_out"][l], sv["gate"], sv["up"], "ffn_out_dx_swiglu_bwd")
        u2t = sv["u2_t"]
        du2, buf = None, None
        for half, dpart in enumerate((dgate, dup)):
            buf = _mm(u2t, dpart, tm=D, tn=_pick(DFF, (1408, 256)), tk=tkT, slab=(buf, 0, 1),
                      col0=half * DFF, cols=2 * DFF, name="mm_ffn_in_dw")
            du2 = _mm(dpart, WL["w_ffn_in"][l], tb=True, res=du2, b_k0=half * DFF, tm=tm, tn=512,
                      tk=_pick(DFF, (1408, 256)), name="mm_ffn_in_dx")
        dw["w_ffn_in"] = buf.reshape(D, 2 * DFF)
        dh, dhb, dgn = _rms_bwd(du2, sv["h2"], W["norm_ffn"][l], dh, "rms_ffn_bwd")
        G["norm_ffn"][l] = dgn.reshape(D)
        dw["w_out"] = _mm(sv["merged_t"], dhb, tm=D, tn=D, tk=tkT, name="mm_out_dw")
        db0, db1, db2, dg0, dg1, dg2 = _out_dx_merge_bwd(dhb, WL["w_out"][l], sv["proj"], *sv["bs"], "out_dx_merge_bwd")
        dos, buf = [], None
        for g, (o, db) in enumerate(zip((sv["oa"], sv["of"], sv["oc"]), (db0, db1, db2))):
            buf = _mm(_transpose(o, "tr_branch"), db, tm=LW, tn=D, tk=tkT, slab=(buf, g, 3), name="mm_branch_dw")
            dos.append(_mm(db, WL["w_branch"][l][g], tb=True, out_dtype=BF16, tm=tm, tn=LW, tk=D, name="mm_branch_dx"))
        dw["w_branch"] = buf.reshape(3 * LW, D)
        dqa, dkb, dvb, dbias, dsk = _swa_bwd(sv["proj"], bias, W["swa_sinks"][l], dos[0], dbias, "swa_bwd")
        dka, dva = _band_fold(dkb, dvb, "swa_band_fold")
        G["swa_sinks"][l] = dsk[0, :NH]
        delta = _fox_delta(dos[1], sv["of"], "fox_delta")
        fox_args = (sv["qaug"], sv["kaug"], sv["kaug_t"], sv["vm"], dos[1], sv["lse"], delta)
        if pend is not None:
            dqf, dqx, dkf, dvf, dkx, *recv = _fox_bwd(*fox_args, "fox_bwd_scatter", scatter=(pend[2], GATHER_AXES))
            finish_layer(pend[0], pend[1], recv)
            pend = None
        else:
            dqf, dqx, dkf, dvf, dkx = _fox_bwd(*fox_args, "fox_bwd")
        dfl, dfb = _cum_bwd(dqx, dkx, sv["proj"], sv["fb"], "cum_bwd")
        G["fox_forget_bias"][l] = dfb[:, 0]
        dxc, dyc, dwr, dwi, dvec = _lru_bwd(sv["proj"], sv["hs"], dos[2], W["conv_w"][l], W["lru_w_r"][l], W["lru_w_i"][l],
                                            sv["lru_vec"], "lru_bwd")
        G["lru_w_r"][l], G["lru_w_i"][l] = _unpair_blocks(dwr), _unpair_blocks(dwi)
        G["lru_b_r"][l], G["lru_b_i"][l], G["lru_lambda"][l], G["conv_b"][l] = dvec[0], dvec[1], dvec[2], dvec[3]
        G["conv_w"][l] = dvec[4:8]
        dproj = jnp.concatenate([dqa, dqf, dkf, dvf, dxc, dyc, dg0, dg1, dg2, dka, dva, dfl], axis=1)
        dw["w_in"] = _mm(sv["u_t"], dproj, tm=D, tn=IN_SLOT, tk=tkT, name="mm_in_dw")
        du = _mm(dproj, WL["w_in"][l], tb=True, tm=tm, tn=512, tk=_pick(INP, (1664, 512)), name="mm_in_dx")
        dh, dhb, dgn = _rms_bwd(du, sv["h0"], W["norm_mix"][l], dh, "rms_mix_bwd")
        G["norm_mix"][l] = dgn.reshape(D)
        if dist:
            gs = [dw[n] for n in BIG]
            pairs = [_add_half(g, r, hax, cidx, "rs_add_half_" + n)
                     for n, g, r, hax in zip(BIG, gs, _swap_halves(gs, HALF_AXES, "rs_swap_halves"), HALF_AXES)]
            ss, sbs = [list(t) for t in zip(*pairs)]
            ss[0], sbs[0] = _kernel_to_gathered_cols(ss[0]), _kernel_to_gathered_cols(sbs[0])
            pend = (l, ss, sbs)
        else:
            for n in BIG:
                GW[n][l] = dw[n]
    if dist:
        finish_layer(pend[0], pend[1], _scatter_to_chips(pend[2], GATHER_AXES, "rs_scatter"))

    grads = {n: (jnp.stack(v) if isinstance(v, list) else v) for n, v in G.items()}
    grads.update({n: jnp.stack(GW[n]) for n in BIG})
    grads["rel_bias_table"] = _bias_bwd(dbias, bucket, "bias_bwd")
    grads["meta_tokens"] = dh[NPAD:BLK]
    return loss, dh[BLK:], grads


NAMES = ("meta_tokens", "rel_bias_table", "norm_mix", "w_in", "swa_sinks", "fox_forget_bias", "conv_w", "conv_b",
         "lru_w_r", "lru_b_r", "lru_w_i", "lru_b_i", "lru_lambda", "w_branch", "w_out", "norm_ffn", "w_ffn_in",
         "w_ffn_out", "norm_final")


def _three_d(n, a):
    return a.reshape(DEPTH, 3 * LW, -1) if n == "w_branch" else a


GATHER_AXES = [BIG_AXIS[n] - 1 for n in BIG]
HALF_AXES = [1 - a for a in GATHER_AXES]


def _gather_weights(P):
    x, y, c = _here()
    mine, me = 2 * x + y, 4 * x + 2 * y + c
    placed = []
    for l in range(DEPTH):
        bufs = []
        for n in BIG:
            shard = _three_d(n, P[n])[l].astype(BF16)
            if n == "w_in":
                shard = jnp.pad(shard, ((0, 0), (0, IN_SLOT - IN_SHARD)))
            zero = jnp.zeros_like(shard)
            bufs.append(jnp.concatenate([jnp.where(mine == j, shard, zero) for j in range(N_SHARD)], axis=BIG_AXIS[n] - 1))
        placed.append(bufs)
    full = {n: _layer_layout(n, a) for n, a in zip(BIG, _all_gather_weights(placed[0], GATHER_AXES, "ag_weights"))}
    tiny = _all_gather_small(_place(_pack(P, TINY)[None], 8, me), "ag_tiny_weights")
    parts = [_unpack(tiny[2 * j], TINY, {n: P[n].shape for n in TINY}) for j in range(N_SHARD)]
    for n in TINY:
        full[n] = jnp.concatenate([p[n] for p in parts], axis=SHARD_AXIS[n])
    for n in SMALL:
        full[n] = P[n]
    full["lru_w_r"] = jnp.stack([_pair_blocks(P["lru_w_r"][l]) for l in range(DEPTH)]).astype(BF16)
    full["lru_w_i"] = jnp.stack([_pair_blocks(P["lru_w_i"][l]) for l in range(DEPTH)]).astype(BF16)
    return full, placed


def _reduce_grads(grads, P):
    x, y, c = _here()
    mine, me = 2 * x + y, 4 * x + 2 * y + c
    out = {n: grads[n].reshape(P[n].shape) for n in BIG if n != "w_in"}
    out["w_in"] = grads["w_in"][:, :, :IN_SHARD]
    names = SMALL + TINY
    gathered = _all_gather_small(_place(_pack(grads, names)[None], 8, me), "ag_small_grads")
    small = _unpack(_sum_lead(gathered, "sum_small_grads"), names, {n: grads[n].shape for n in names})
    for n in SMALL:
        out[n] = small[n]
    for n in TINY:
        w = P[n].shape[SHARD_AXIS[n]]
        out[n] = lax.dynamic_slice_in_dim(small[n], mine * w, w, axis=SHARD_AXIS[n])
    return out


def _update(P, Gd, M, V):
    delta, new_m, new_v = {}, {}, {}
    for n in BIG + TINY:
        shp = P[n].shape
        two = (int(np.prod(shp[:-1])), shp[-1])
        d, m, v = _adamw(P[n].reshape(two), Gd[n].reshape(two), M[n].reshape(two), V[n].reshape(two), "adamw_" + n)
        delta[n], new_m[n], new_v[n] = d.reshape(shp), m.reshape(shp), v.reshape(shp)
    shapes = {n: P[n].shape for n in SMALL}
    d, m, v = _adamw(_pack(P, SMALL), _pack(Gd, SMALL), _pack(M, SMALL), _pack(V, SMALL), "adamw_small")
    for dst, buf in ((delta, d), (new_m, m), (new_v, v)):
        dst.update(_unpack(buf, SMALL, shapes))
    return delta, new_m, new_v


def kernel(x, meta_tokens, rel_bias_table, norm_mix, w_in, swa_sinks, fox_forget_bias, conv_w, conv_b, lru_w_r, lru_b_r, lru_w_i, lru_b_i, lru_lambda, w_branch, w_out, norm_ffn, w_ffn_in, w_ffn_out, norm_final, loss_target, m_meta_tokens, m_rel_bias_table, m_norm_mix, m_w_in, m_swa_sinks, m_fox_forget_bias, m_conv_w, m_conv_b, m_lru_w_r, m_lru_b_r, m_lru_w_i, m_lru_b_i, m_lru_lambda, m_w_branch, m_w_out, m_norm_ffn, m_w_ffn_in, m_w_ffn_out, m_norm_final, v_meta_tokens, v_rel_bias_table, v_norm_mix, v_w_in, v_swa_sinks, v_fox_forget_bias, v_conv_w, v_conv_b, v_lru_w_r, v_lru_b_r, v_lru_w_i, v_lru_b_i, v_lru_lambda, v_w_branch, v_w_out, v_norm_ffn, v_w_ffn_in, v_w_ffn_out, v_norm_final):
    P = dict(zip(NAMES, (meta_tokens, rel_bias_table, norm_mix, w_in, swa_sinks, fox_forget_bias, conv_w, conv_b, lru_w_r,
                         lru_b_r, lru_w_i, lru_b_i, lru_lambda, w_branch, w_out, norm_ffn, w_ffn_in, w_ffn_out, norm_final)))
    M = dict(zip(NAMES, (m_meta_tokens, m_rel_bias_table, m_norm_mix, m_w_in, m_swa_sinks, m_fox_forget_bias, m_conv_w,
                         m_conv_b, m_lru_w_r, m_lru_b_r, m_lru_w_i, m_lru_b_i, m_lru_lambda, m_w_branch, m_w_out, m_norm_ffn,
                         m_w_ffn_in, m_w_ffn_out, m_norm_final)))
    V = dict(zip(NAMES, (v_meta_tokens, v_rel_bias_table, v_norm_mix, v_w_in, v_swa_sinks, v_fox_forget_bias, v_conv_w,
                         v_conv_b, v_lru_w_r, v_lru_b_r, v_lru_w_i, v_lru_b_i, v_lru_lambda, v_w_branch, v_w_out, v_norm_ffn,
                         v_w_ffn_in, v_w_ffn_out, v_norm_final)))
    W, placed = _gather_weights(P)
    loss_local, grad_x, grads = _local_step(x[0], loss_target[0], W, placed)
    loss = lax.psum(loss_local, ("x", "y", "c"))
    Gd = _reduce_grads(grads, P)
    delta, new_m, new_v = _update(P, Gd, M, V)
    return (loss, grad_x[None], *[Gd[n] for n in NAMES], *[delta[n] for n in NAMES],
            *[new_m[n] for n in NAMES], *[new_v[n] for n in NAMES])
```

```python
import functools
import math

import numpy as np
import jax
import jax.numpy as jnp
from jax import lax
from jax.experimental import pallas as pl
from jax.experimental.pallas import tpu as pltpu

F32, BF16 = jnp.float32, jnp.bfloat16
MESH = pl.DeviceIdType.MESH
ANY = pl.BlockSpec(memory_space=pl.ANY)
SMEM = pl.BlockSpec(memory_space=pltpu.SMEM)

D = 1024
DEPTH = 4
BLK = 128
N_META = 16
NPAD = 112
NH = 8
LW = 512
DFF = 2816
EPS = 1e-6
NEG = -1e30
SCALE = 0.125
LRU_C = 8.0
REL_BUCKETS = 32
N_SHARD = 4
QA, QF, KF, VF, XC, YC, GT, KA, VA, FL, INP = 0, 512, 1024, 1536, 2048, 2560, 3072, 6144, 6272, 6400, 6656
IN_COLS = 6408
VMEM_LIMIT = 48 * 1024 * 1024

ADAM_LR, ADAM_B1, ADAM_B2, ADAM_EPS, ADAM_WD, ADAM_STEP = 0.001, 0.9, 0.999, 1e-08, 0.01, 10


def _cp(*sem):
    return pltpu.CompilerParams(dimension_semantics=sem or None, vmem_limit_bytes=VMEM_LIMIT)


def _pick(n, prefs):
    for p in prefs:
        if n % p == 0:
            return p
    return n


def _rt(T):
    return _pick(T, (384, 128))


def _sigmoid(z):
    return 1.0 / (1.0 + jnp.exp(-z))


def _log_sigmoid(z):
    return jnp.minimum(z, 0.0) - jnp.log(1.0 + jnp.exp(-jnp.abs(z)))


def _gelu(y):
    c = math.sqrt(2.0 / math.pi)
    return 0.5 * y * (1.0 + jnp.tanh(c * (y + 0.044715 * y * y * y)))


def _gelu_grad(y):
    c = math.sqrt(2.0 / math.pi)
    t = jnp.tanh(c * (y + 0.044715 * y * y * y))
    return 0.5 * (1.0 + t) + 0.5 * y * (1.0 - t * t) * c * (1.0 + 3.0 * 0.044715 * y * y)


def _neg_expm1(z):
    series = -z * (1.0 + z * (0.5 + z * (1.0 / 6.0 + z * (1.0 / 24.0 + z * (1.0 / 120.0)))))
    return jnp.where(z > -0.1, series, 1.0 - jnp.exp(z))


def _dot(a, b, ca, cb):
    return lax.dot_general(a, b, (((ca,), (cb,)), ((), ())), preferred_element_type=F32)


def _mm(a, b, *, ta=False, tb=False, res=None, out_dtype=F32, tm, tn, tk, name, slab=None, b_k0=0, col0=0, cols=None):
    M, K = (a.shape[1], a.shape[0]) if ta else a.shape
    N = b.shape[0] if tb else b.shape[1]
    assert (b.shape[1] if tb else b.shape[0]) >= K + b_k0 and M % tm == 0 and N % tn == 0 and K % tk == 0, (name, a.shape, b.shape)
    assert b_k0 % tk == 0 and col0 % tn == 0
    nk, kb, jb = K // tk, b_k0 // tk, col0 // tn
    ca, cb = (0 if ta else 1), (1 if tb else 0)
    n_in = 2 + (res is not None) + (slab is not None and slab[0] is not None)

    def body(*refs):
        a_ref, b_ref = refs[:2]
        r_ref = refs[2] if res is not None else None
        o_ref = refs[n_in]
        part = _dot(a_ref[...].astype(BF16), b_ref[...].astype(BF16), ca, cb)

        def fin(acc):
            if res is not None:
                acc = acc + r_ref[...]
            o_ref[...] = acc.astype(out_dtype)

        if nk == 1:
            fin(part)
        else:
            acc_ref = refs[-1]
            k = pl.program_id(2)

            @pl.when(k == 0)
            def _():
                acc_ref[...] = part

            @pl.when(k > 0)
            def _():
                acc_ref[...] += part

            @pl.when(k == nk - 1)
            def _():
                fin(acc_ref[...])

    a_spec = pl.BlockSpec((tk, tm), lambda i, j, k: (k, i)) if ta else pl.BlockSpec((tm, tk), lambda i, j, k: (i, k))
    b_spec = (pl.BlockSpec((tn, tk), lambda i, j, k: (j, k + kb)) if tb
              else pl.BlockSpec((tk, tn), lambda i, j, k: (k + kb, j)))
    o_spec = pl.BlockSpec((tm, tn), lambda i, j, k: (i, j))
    in_specs, ops = [a_spec, b_spec], [a, b]
    if res is not None:
        in_specs.append(o_spec)
        ops.append(res)
    out_shape, aliases = jax.ShapeDtypeStruct((M, N), out_dtype), {}
    if slab is not None:
        buf, idx, n = slab
        o_spec = pl.BlockSpec((None, tm, tn), lambda i, j, k: (idx, i, j + jb))
        out_shape = jax.ShapeDtypeStruct((n, M, cols or N), out_dtype)
        if buf is not None:
            aliases = {len(ops): 0}
            in_specs.append(ANY)
            ops.append(buf)
    return pl.pallas_call(
        body, grid=(M // tm, N // tn, nk), in_specs=in_specs, out_specs=o_spec, out_shape=out_shape,
        input_output_aliases=aliases, scratch_shapes=[pltpu.VMEM((tm, tn), F32)] if nk > 1 else [],
        compiler_params=_cp("parallel", "parallel", "arbitrary"), name=name)(*ops)


def _rms_fwd(h, g, name):
    T = h.shape[0]
    tr = _rt(T)

    def body(h_ref, g_ref, u_ref, ut_ref):
        x = h_ref[...]
        r = lax.rsqrt(jnp.mean(x * x, axis=-1, keepdims=True) + EPS)
        u = (x * r * g_ref[...]).astype(BF16)
        u_ref[...] = u
        ut_ref[...] = u.T

    return pl.pallas_call(
        body, grid=(T // tr,),
        in_specs=[pl.BlockSpec((tr, D), lambda i: (i, 0)), pl.BlockSpec((1, D), lambda i: (0, 0))],
        out_specs=[pl.BlockSpec((tr, D), lambda i: (i, 0)), pl.BlockSpec((D, tr), lambda i: (0, i))],
        out_shape=[jax.ShapeDtypeStruct((T, D), BF16), jax.ShapeDtypeStruct((D, T), BF16)],
        compiler_params=_cp("parallel"), name=name)(h, g.reshape(1, D))


def _rms_bwd(du, h, g, dres, name):
    T = h.shape[0]
    tr = _rt(T)

    def body(du_ref, h_ref, g_ref, dres_ref, dh_ref, dhb_ref, dg_ref):
        x = h_ref[...]
        r = lax.rsqrt(jnp.mean(x * x, axis=-1, keepdims=True) + EPS)
        xh = x * r
        dy = du_ref[...]
        dxh = dy * g_ref[...]
        dx = r * (dxh - xh * jnp.mean(dxh * xh, axis=-1, keepdims=True))
        dh = dres_ref[...] + dx
        dh_ref[...] = dh
        dhb_ref[...] = dh.astype(BF16)
        part = jnp.sum(dy * xh, axis=0, keepdims=True)

        @pl.when(pl.program_id(0) == 0)
        def _():
            dg_ref[...] = part

        @pl.when(pl.program_id(0) > 0)
        def _():
            dg_ref[...] += part

    row = pl.BlockSpec((tr, D), lambda i: (i, 0))
    vec = pl.BlockSpec((1, D), lambda i: (0, 0))
    return pl.pallas_call(
        body, grid=(T // tr,), in_specs=[row, row, vec, row], out_specs=[row, row, vec],
        out_shape=[jax.ShapeDtypeStruct((T, D), F32), jax.ShapeDtypeStruct((T, D), BF16), jax.ShapeDtypeStruct((1, D), F32)],
        compiler_params=_cp("arbitrary"), name=name)(du, h, g.reshape(1, D), dres)


def _loss_head(h, tgt, g, name):
    T = h.shape[0]
    nb = T // BLK

    def body(h_ref, t_ref, g_ref, dh_ref, dhb_ref, dg_ref, loss_ref):
        i = pl.program_id(0)
        x = h_ref[...]
        r = lax.rsqrt(jnp.mean(x * x, axis=-1, keepdims=True) + EPS)
        xh = x * r
        gv = g_ref[...]
        tok = i >= 1
        err = jnp.where(tok, xh * gv - t_ref[...], 0.0)
        dy = err * (1.0 / D)
        dxh = dy * gv
        dx = r * (dxh - xh * jnp.mean(dxh * xh, axis=-1, keepdims=True))
        dh_ref[...] = dx
        dhb_ref[...] = dx.astype(BF16)
        dg = jnp.sum(dy * xh, axis=0, keepdims=True)
        ls = jnp.zeros((1, BLK), F32) + jnp.sum(err * err) * (0.5 / D)

        @pl.when(i == 0)
        def _():
            dg_ref[...] = dg
            loss_ref[...] = ls

        @pl.when(i > 0)
        def _():
            dg_ref[...] += dg
            loss_ref[...] += ls

    row = pl.BlockSpec((BLK, D), lambda i: (i, 0))
    vec = pl.BlockSpec((1, D), lambda i: (0, 0))
    return pl.pallas_call(
        body, grid=(nb,),
        in_specs=[row, pl.BlockSpec((BLK, D), lambda i: (jnp.maximum(i - 1, 0), 0)), vec],
        out_specs=[row, row, vec, pl.BlockSpec((1, BLK), lambda i: (0, 0))],
        out_shape=[jax.ShapeDtypeStruct((T, D), F32), jax.ShapeDtypeStruct((T, D), BF16),
                   jax.ShapeDtypeStruct((1, D), F32), jax.ShapeDtypeStruct((1, BLK), F32)],
        compiler_params=_cp("arbitrary"), name=name)(h, tgt, g.reshape(1, D))


def _bucket_table():
    q = np.arange(BLK)[:, None]
    k = np.arange(2 * BLK)[None, :]
    d = np.maximum(q + BLK - k, 0)
    max_exact = REL_BUCKETS // 2
    scaled = np.log(np.maximum(d, 1).astype(np.float32) / np.float32(max_exact)) / np.float32(math.log(128 / max_exact))
    large = np.minimum(max_exact + (scaled.astype(np.float32) * (REL_BUCKETS - max_exact)).astype(np.int32), REL_BUCKETS - 1)
    return np.where(d < max_exact, d, large).astype(np.int32)


def _bias_build(table, bucket, name):
    def body(t_ref, bk_ref, o_ref):
        bk = bk_ref[...]
        for h in range(NH):
            acc = jnp.zeros((BLK, 2 * BLK), F32)
            for b in range(REL_BUCKETS):
                acc = jnp.where(bk == b, t_ref[b, h], acc)
            o_ref[h] = acc

    return pl.pallas_call(
        body, in_specs=[SMEM, pl.BlockSpec(memory_space=pltpu.VMEM)], out_specs=pl.BlockSpec(memory_space=pltpu.VMEM),
        out_shape=jax.ShapeDtypeStruct((NH, BLK, 2 * BLK), F32), compiler_params=_cp(), name=name)(table, bucket)


def _bias_bwd(dbias, bucket, name):
    def body(d_ref, bk_ref, o_ref):
        bk = bk_ref[...]
        for h in range(NH):
            dh = d_ref[h]
            for b in range(REL_BUCKETS):
                o_ref[b, h] = jnp.sum(jnp.where(bk == b, dh, 0.0))

    return pl.pallas_call(
        body, in_specs=[pl.BlockSpec(memory_space=pltpu.VMEM)] * 2, out_specs=SMEM,
        out_shape=jax.ShapeDtypeStruct((REL_BUCKETS, NH), F32), compiler_params=_cp(), name=name)(dbias, bucket)


def _swa_specs(nq_cols):
    prev = lambda n: jnp.maximum(n - 1, 0)
    return [
        pl.BlockSpec((BLK, nq_cols), lambda n: (n, QA // nq_cols)),
        pl.BlockSpec((BLK, BLK), lambda n: (prev(n), KA // BLK)), pl.BlockSpec((BLK, BLK), lambda n: (n, KA // BLK)),
        pl.BlockSpec((BLK, BLK), lambda n: (prev(n), VA // BLK)), pl.BlockSpec((BLK, BLK), lambda n: (n, VA // BLK)),
    ]


def _swa_mask(n):
    row = lax.broadcasted_iota(jnp.int32, (BLK, 2 * BLK), 0)
    col = lax.broadcasted_iota(jnp.int32, (BLK, 2 * BLK), 1)
    dist = row + BLK - col
    return (dist >= 0) & (dist < BLK) & ((n - 1) * BLK + col >= NPAD)


def _swa_probs(qm, ksel, mask, bias_h, sink):
    s = _dot(qm, ksel, 1, 1) * SCALE
    s = jnp.where(mask, s + bias_h, NEG)
    m = jnp.maximum(jnp.max(s, axis=-1, keepdims=True), sink)
    p = jnp.exp(s - m)
    psink = jnp.exp(sink - m)
    inv = 1.0 / (jnp.sum(p, axis=-1, keepdims=True) + psink)
    return p * inv, psink * inv


def _swa_fwd(proj, bias, sinks, name):
    T = proj.shape[0]
    nb = T // BLK

    def body(sk_ref, q_ref, kp_ref, kc_ref, vp_ref, vc_ref, b_ref, o_ref, ot_ref):
        n = pl.program_id(0)
        lo = lax.broadcasted_iota(jnp.int32, (1, BLK), 1) < 64
        kb = jnp.concatenate([kp_ref[...], kc_ref[...]], axis=0)
        vb = jnp.concatenate([vp_ref[...], vc_ref[...]], axis=0)
        kbs = (kb.astype(BF16), pltpu.roll(kb, 64, 1).astype(BF16))
        vbs = (vb, pltpu.roll(vb, 64, 1))
        mask = _swa_mask(n)
        outs = []
        for pr in range(NH // 2):
            qp = q_ref[:, pr * BLK:(pr + 1) * BLK]
            kv = pr // 2
            acc = jnp.zeros((BLK, BLK), F32)
            for e in range(2):
                lm = lo if e == 0 else jnp.logical_not(lo)
                sw = 0 if kv == e else 1
                qm = jnp.where(lm, qp, 0.0).astype(BF16)
                pn, _ = _swa_probs(qm, kbs[sw], mask, b_ref[2 * pr + e], sk_ref[2 * pr + e])
                acc = acc + _dot(pn.astype(BF16), jnp.where(lm, vbs[sw], 0.0).astype(BF16), 1, 0)
            outs.append(acc)
        o = jnp.concatenate(outs, axis=1).astype(BF16)
        o_ref[...] = o
        ot_ref[...] = o.T

    return pl.pallas_call(
        body, grid=(nb,),
        in_specs=[SMEM] + _swa_specs(512) + [pl.BlockSpec((NH, BLK, 2 * BLK), lambda n: (0, 0, 0))],
        out_specs=[pl.BlockSpec((BLK, 512), lambda n: (n, 0)), pl.BlockSpec((512, BLK), lambda n: (0, n))],
        out_shape=[jax.ShapeDtypeStruct((T, 512), BF16), jax.ShapeDtypeStruct((512, T), BF16)],
        compiler_params=_cp("parallel"), name=name)(sinks, proj, proj, proj, proj, proj, bias)


def _swa_bwd(proj, bias, sinks, do, dbias_in, name):
    T = proj.shape[0]
    nb = T // BLK

    def body(sk_ref, q_ref, kp_ref, kc_ref, vp_ref, vc_ref, b_ref, do_ref, dbi_ref,
             dq_ref, dk_ref, dv_ref, db_ref, dsk_ref, sk_acc):
        n = pl.program_id(0)
        lane = lax.broadcasted_iota(jnp.int32, (1, BLK), 1)
        lo = lane < 64
        kb = jnp.concatenate([kp_ref[...], kc_ref[...]], axis=0)
        vb = jnp.concatenate([vp_ref[...], vc_ref[...]], axis=0)
        kbs = (kb, pltpu.roll(kb, 64, 1))
        vbs = (vb, pltpu.roll(vb, 64, 1))
        mask = _swa_mask(n)

        @pl.when(n == 0)
        def _():
            db_ref[...] = dbi_ref[...]
            sk_acc[...] = jnp.zeros_like(sk_acc)

        dqs = []
        dk = jnp.zeros((2 * BLK, BLK), F32)
        dv = jnp.zeros((2 * BLK, BLK), F32)
        for pr in range(NH // 2):
            qp = q_ref[:, pr * BLK:(pr + 1) * BLK]
            dop = do_ref[:, pr * BLK:(pr + 1) * BLK].astype(F32)
            kv = pr // 2
            dq = jnp.zeros((BLK, BLK), F32)
            for e in range(2):
                h = 2 * pr + e
                lm = lo if e == 0 else jnp.logical_not(lo)
                sw = 0 if kv == e else 1
                qm = jnp.where(lm, qp, 0.0)
                dom = jnp.where(lm, dop, 0.0)
                pn, ps = _swa_probs(qm.astype(BF16), kbs[sw].astype(BF16), mask, b_ref[h], sk_ref[h])
                dp = _dot(dom.astype(BF16), vbs[sw].astype(BF16), 1, 1)
                delta = jnp.sum(pn * dp, axis=-1, keepdims=True)
                ds = pn * (dp - delta)
                db_ref[h] += ds
                sk_acc[...] += jnp.where(lane == h, -(ps * delta), 0.0)
                dsb = (ds * SCALE).astype(BF16)
                dq = dq + _dot(dsb, jnp.where(lm, kbs[sw], 0.0).astype(BF16), 1, 0)
                qk = qm if sw == 0 else pltpu.roll(qm, 64, 1)
                dok = dom if sw == 0 else pltpu.roll(dom, 64, 1)
                dk = dk + _dot(dsb, qk.astype(BF16), 0, 0)
                dv = dv + _dot(pn.astype(BF16), dok.astype(BF16), 0, 0)
            dqs.append(dq)
        dq_ref[...] = jnp.concatenate(dqs, axis=1).astype(BF16)
        dk_ref[0] = dk
        dv_ref[0] = dv

        @pl.when(n == nb - 1)
        def _():
            dsk_ref[...] = jnp.sum(sk_acc[...], axis=0, keepdims=True)

    full_b = pl.BlockSpec((NH, BLK, 2 * BLK), lambda n: (0, 0, 0))
    band = pl.BlockSpec((1, 2 * BLK, BLK), lambda n: (n, 0, 0))
    return pl.pallas_call(
        body, grid=(nb,),
        in_specs=[SMEM] + _swa_specs(512) + [full_b, pl.BlockSpec((BLK, 512), lambda n: (n, 0)), full_b],
        out_specs=[pl.BlockSpec((BLK, 512), lambda n: (n, 0)), band, band, full_b, pl.BlockSpec((1, BLK), lambda n: (0, 0))],
        out_shape=[jax.ShapeDtypeStruct((T, 512), BF16), jax.ShapeDtypeStruct((nb, 2 * BLK, BLK), F32),
                   jax.ShapeDtypeStruct((nb, 2 * BLK, BLK), F32), jax.ShapeDtypeStruct((NH, BLK, 2 * BLK), F32),
                   jax.ShapeDtypeStruct((1, BLK), F32)],
        scratch_shapes=[pltpu.VMEM((BLK, BLK), F32)],
        compiler_params=_cp("arbitrary"), name=name)(sinks, proj, proj, proj, proj, proj, bias, do, dbias_in)


def _band_fold(dkb, dvb, name):
    nb = dkb.shape[0]

    def body(ko_ref, kn_ref, vo_ref, vn_ref, dk_ref, dv_ref):
        last = pl.program_id(0) == nb - 1
        dk_ref[...] = (ko_ref[0] + jnp.where(last, 0.0, kn_ref[0])).astype(BF16)
        dv_ref[...] = (vo_ref[0] + jnp.where(last, 0.0, vn_ref[0])).astype(BF16)

    own = pl.BlockSpec((1, BLK, BLK), lambda j: (j, 1, 0))
    nxt = pl.BlockSpec((1, BLK, BLK), lambda j: (jnp.minimum(j + 1, nb - 1), 0, 0))
    out = pl.BlockSpec((BLK, BLK), lambda j: (j, 0))
    return pl.pallas_call(
        body, grid=(nb,), in_specs=[own, nxt, own, nxt], out_specs=[out, out],
        out_shape=[jax.ShapeDtypeStruct((nb * BLK, BLK), BF16)] * 2,
        compiler_params=_cp("parallel"), name=name)(dkb, dkb, dvb, dvb)


def _token_major(x, width):
    full = jnp.concatenate([x, jnp.zeros((BLK - NH, BLK), F32)], axis=0).T
    return full if width == BLK else jnp.concatenate([full, jnp.zeros((BLK, width - BLK), F32)], axis=1)


def _cum_fwd(proj, fb, name):
    T = proj.shape[0]
    tr = _rt(T)

    def body(z_ref, fb_ref, c_ref, carry):
        g = pl.program_id(0)
        lane = lax.broadcasted_iota(jnp.int32, (NH, BLK), 1)

        @pl.when(g == 0)
        def _():
            carry[...] = jnp.zeros_like(carry)

        run = carry[...]
        for sb in range(tr // BLK):
            r = slice(sb * BLK, (sb + 1) * BLK)
            z = z_ref[r, :].T[0:NH, :] + fb_ref[...]
            x = jnp.where(g * tr + sb * BLK + lane >= NPAD, _log_sigmoid(z), 0.0)
            s = 1
            while s < BLK:
                x = x + jnp.where(lane >= s, pltpu.roll(x, s, 1), 0.0)
                s *= 2
            x = x + run
            run = jnp.zeros((NH, BLK), F32) + jnp.sum(jnp.where(lane == BLK - 1, x, 0.0), axis=-1, keepdims=True)
            c_ref[r, :] = _token_major(x, BLK)
        carry[...] = run

    return pl.pallas_call(
        body, grid=(T // tr,),
        in_specs=[pl.BlockSpec((tr, BLK), lambda g: (g, FL // BLK)), pl.BlockSpec((NH, 1), lambda g: (0, 0))],
        out_specs=pl.BlockSpec((tr, BLK), lambda g: (g, 0)), out_shape=jax.ShapeDtypeStruct((T, BLK), F32),
        scratch_shapes=[pltpu.VMEM((NH, BLK), F32)], compiler_params=_cp("arbitrary"), name=name)(proj, fb)


def _cum_bwd(dqx, dkx, proj, fb, name):
    T = proj.shape[0]
    tr = _rt(T)
    nb = T // tr

    def body(dq_ref, dk_ref, z_ref, fb_ref, dz_ref, db_ref, carry):
        k = pl.program_id(0)
        g = nb - 1 - k
        lane = lax.broadcasted_iota(jnp.int32, (NH, BLK), 1)

        @pl.when(k == 0)
        def _():
            carry[...] = jnp.zeros_like(carry)
            db_ref[...] = jnp.zeros_like(db_ref)

        def picked(ref, r, r_first, r_second):
            rows = []
            for p in range(NH // 2):
                t_ = ref[r, p * BLK:(p + 1) * BLK].T
                rows += [t_[r_first:r_first + 1, :], t_[r_second:r_second + 1, :]]
            return jnp.concatenate(rows, axis=0)

        run, tot = carry[...], jnp.zeros((NH, 1), F32)
        for sb in reversed(range(tr // BLK)):
            r = slice(sb * BLK, (sb + 1) * BLK)
            x = picked(dq_ref, r, 64, 0) - picked(dk_ref, r, 67, 3)
            s = 1
            while s < BLK:
                x = x + jnp.where(lane < BLK - s, pltpu.roll(x, BLK - s, 1), 0.0)
                s *= 2
            x = x + run
            run = jnp.zeros((NH, BLK), F32) + jnp.sum(jnp.where(lane == 0, x, 0.0), axis=-1, keepdims=True)
            z = z_ref[r, :].T[0:NH, :] + fb_ref[...]
            dz = jnp.where(g * tr + sb * BLK + lane >= NPAD, x * _sigmoid(-z), 0.0)
            tot = tot + jnp.sum(dz, axis=-1, keepdims=True)
            dz_ref[r, :] = _token_major(dz, 2 * BLK).astype(BF16)
        carry[...] = run
        db_ref[...] += tot

    rev = lambda k: nb - 1 - k
    wide = pl.BlockSpec((tr, 512), lambda k: (rev(k), 0))
    return pl.pallas_call(
        body, grid=(nb,),
        in_specs=[wide, wide, pl.BlockSpec((tr, BLK), lambda k: (rev(k), FL // BLK)), pl.BlockSpec((NH, 1), lambda k: (0, 0))],
        out_specs=[pl.BlockSpec((tr, 2 * BLK), lambda k: (rev(k), 0)), pl.BlockSpec((NH, BLK), lambda k: (0, 0))],
        out_shape=[jax.ShapeDtypeStruct((T, 2 * BLK), BF16), jax.ShapeDtypeStruct((NH, BLK), F32)],
        scratch_shapes=[pltpu.VMEM((NH, BLK), F32)], compiler_params=_cp("arbitrary"), name=name)(dqx, dkx, proj, fb)


def _fox_prep(proj, ccol, name):
    T = proj.shape[0]
    tr = _pick(T, (1408, 384, 128))

    def body(q_ref, k_ref, v_ref, cc_ref, qa_ref, ka_ref, kt_ref, vm_ref, vo_ref):
        h = pl.program_id(1)
        lane = lax.broadcasted_iota(jnp.int32, (1, BLK), 1)
        own = (lane >> 6) == (h & 1)
        a0 = 64 * (1 - (h & 1))
        c = _lane_pick(cc_ref[...], lane, h)
        hi = c.astype(BF16).astype(F32)
        mid = (c - hi).astype(BF16).astype(F32)
        lo = (c - hi - mid).astype(BF16).astype(F32)
        ones = (lane >= a0 + 3) & (lane < a0 + 6)
        qa = jnp.where(own, q_ref[...] * SCALE, jnp.where(ones, 1.0, 0.0))
        qa = jnp.where(lane == a0, hi, jnp.where(lane == a0 + 1, mid, jnp.where(lane == a0 + 2, lo, qa)))
        ones = (lane >= a0) & (lane < a0 + 3)
        ka = jnp.where(own, k_ref[...], jnp.where(ones, 1.0, 0.0))
        ka = jnp.where(lane == a0 + 3, -hi, jnp.where(lane == a0 + 4, -mid, jnp.where(lane == a0 + 5, -lo, ka)))
        qa_ref[...] = qa.astype(BF16)
        kab = ka.astype(BF16)
        ka_ref[...] = kab
        kt_ref[...] = kab.T
        vm = jnp.where(own, v_ref[...], 0.0)
        vm_ref[...] = vm.astype(BF16)
        vo_ref[...] = jnp.where(lane == a0, 1.0, vm).astype(BF16)

    pair = lambda col0: pl.BlockSpec((tr, BLK), lambda i, h: (i, col0 // BLK + (h >> 1)))
    out = pl.BlockSpec((None, tr, BLK), lambda i, h: (h, i, 0))
    out_t = pl.BlockSpec((None, BLK, tr), lambda i, h: (h, 0, i))
    tok = jax.ShapeDtypeStruct((NH, T, BLK), BF16)
    return pl.pallas_call(
        body, grid=(T // tr, NH), in_specs=[pair(QF), pair(KF), pair(VF), pl.BlockSpec((tr, BLK), lambda i, h: (i, 0))],
        out_specs=[out, out, out_t, out, out], out_shape=[tok, tok, jax.ShapeDtypeStruct((NH, BLK, T), BF16), tok, tok],
        compiler_params=_cp("parallel", "arbitrary"), name=name)(proj, proj, proj, ccol)


def _fox_fwd(qaug, kaug_t, vo, name, gather=None):
    T = qaug.shape[1]
    t = _rt(T)
    nt = T // t
    ng = len(gather[0]) if gather else 0

    pairs = [(i, j) for i in range(nt) for j in range(i + 1)]
    i_of = jnp.asarray(np.array([p[0] for p in pairs], np.int32))
    j_of = jnp.asarray(np.array([p[1] for p in pairs], np.int32))
    ns = len(pairs)

    def body(i_ref, j_ref, q0, q1, k0, k1, v0, v1, *rest):
        o_ref, ot_ref, lse0_ref, lse1_ref = rest[ng:ng + 4]
        m_ref, acc_ref = rest[2 * ng + 4:2 * ng + 6]
        p_, s_ = pl.program_id(0), pl.program_id(1)
        i, j = i_ref[s_], j_ref[s_]
        lane = lax.broadcasted_iota(jnp.int32, (1, BLK), 1)
        lo = lane < 64
        if gather:
            start, finish = _gather_plan(rest[ng + 4:2 * ng + 4], gather[1], *rest[2 * ng + 6:])
            pl.when((p_ == 0) & (s_ == 0))(start)

        @pl.when(j == 0)
        def _():
            m_ref[...] = jnp.full_like(m_ref, NEG)
            acc_ref[...] = jnp.zeros_like(acc_ref)

        def step(masked):
            for e, (q_ref, k_ref, v_ref) in enumerate(((q0, k0, v0), (q1, k1, v1))):
                s = _dot(q_ref[...], k_ref[...], 1, 0)
                if masked:
                    s = jnp.where(_fox_mask(i, j, t), s, NEG)
                m_old = m_ref[e]
                m_new = jnp.maximum(m_old, jnp.max(s, axis=-1, keepdims=True))
                m_ref[e] = m_new
                pe = jnp.exp(s - jnp.concatenate([m_new] * (t // BLK), axis=1))
                acc_ref[e] = jnp.exp(m_old - m_new) * acc_ref[e] + _dot(pe.astype(BF16), v_ref[...], 1, 0)

        pl.when((j < i) & (j > 0))(lambda: step(False))
        pl.when((j == i) | ((j == 0) & (i > 0)))(lambda: step(True))

        @pl.when(j == i)
        def _():
            rows = i * t + lax.broadcasted_iota(jnp.int32, (t, 1), 0)
            l0, l1 = _lane_pick(acc_ref[0], lane, 64), _lane_pick(acc_ref[1], lane, 0)
            o = jnp.where(rows >= NPAD, jnp.where(lo, acc_ref[0] / l0, acc_ref[1] / l1), 0.0).astype(BF16)
            o_ref[...] = o
            ot_ref[...] = o.T
            lse0_ref[...] = m_ref[0] + jnp.log(l0)
            lse1_ref[...] = m_ref[1] + jnp.log(l1)

        if gather:
            pl.when((p_ == NH // 2 - 1) & (s_ == ns - 1))(finish)

    qs = lambda e: pl.BlockSpec((None, t, BLK), lambda p, s, ii, jj: (2 * p + e, ii[s], 0))
    ks = lambda e: pl.BlockSpec((None, t, BLK), lambda p, s, ii, jj: (2 * p + e, jj[s], 0))
    kts = lambda e: pl.BlockSpec((None, BLK, t), lambda p, s, ii, jj: (2 * p + e, 0, jj[s]))
    rep = pl.BlockSpec((None, t, BLK), lambda p, s, ii, jj: (p, ii[s], 0))
    bufs = list(gather[0]) if gather else []
    return pl.pallas_call(
        body,
        grid_spec=pltpu.PrefetchScalarGridSpec(
            num_scalar_prefetch=2, grid=(NH // 2, ns),
            in_specs=[qs(0), qs(1), kts(0), kts(1), ks(0), ks(1)] + [ANY] * ng,
            out_specs=[pl.BlockSpec((t, BLK), lambda p, s, ii, jj: (ii[s], p)),
                       pl.BlockSpec((BLK, t), lambda p, s, ii, jj: (p, ii[s])), rep, rep] + [ANY] * ng,
            scratch_shapes=[pltpu.VMEM((2, t, BLK), F32), pltpu.VMEM((2, t, BLK), F32)]
            + ([pltpu.SemaphoreType.DMA((6 * ng,)), pltpu.SemaphoreType.DMA((6 * ng,))] if gather else [])),
        out_shape=[jax.ShapeDtypeStruct((T, 512), BF16), jax.ShapeDtypeStruct((512, T), BF16)]
        + [jax.ShapeDtypeStruct((NH // 2, T, BLK), F32)] * 2 + [jax.ShapeDtypeStruct(b.shape, b.dtype) for b in bufs],
        input_output_aliases={8 + g: 4 + g for g in range(ng)},
        compiler_params=(pltpu.CompilerParams(dimension_semantics=("arbitrary",) * 2, vmem_limit_bytes=VMEM_LIMIT,
                                              has_side_effects=True) if gather
                         else _cp("parallel", "arbitrary")), name=name)(i_of, j_of, qaug, qaug, kaug_t, kaug_t, vo, vo, *bufs)


def _fox_delta(do, o, name):
    T = do.shape[0]
    tr = _rt(T)

    def body(do_ref, o_ref, d0_ref, d1_ref):
        lo = lax.broadcasted_iota(jnp.int32, (1, BLK), 1) < 64
        prod = do_ref[...].astype(F32) * o_ref[...].astype(F32)
        d0_ref[...] = jnp.zeros((tr, BLK), F32) + jnp.sum(jnp.where(lo, prod, 0.0), axis=-1, keepdims=True)
        d1_ref[...] = jnp.zeros((tr, BLK), F32) + jnp.sum(jnp.where(lo, 0.0, prod), axis=-1, keepdims=True)

    blk = pl.BlockSpec((tr, BLK), lambda i, p: (i, p))
    rep = pl.BlockSpec((None, tr, BLK), lambda i, p: (p, i, 0))
    return pl.pallas_call(
        body, grid=(T // tr, NH // 2), in_specs=[blk, blk], out_specs=[rep, rep],
        out_shape=[jax.ShapeDtypeStruct((NH // 2, T, BLK), F32)] * 2,
        compiler_params=_cp("parallel", "parallel"), name=name)(do, o)


def _fox_bwd(qaug, kaug, kaug_t, vm, do, lses, deltas, name, scatter=None):
    T = qaug.shape[1]
    t = _rt(T)
    nt = T // t
    ng = len(scatter[0]) if scatter else 0
    pairs = [(i, j) for j in range(nt) for i in range(j, nt)]
    i_of = jnp.asarray(np.array([p[0] for p in pairs], np.int32))
    j_of = jnp.asarray(np.array([p[1] for p in pairs], np.int32))
    ns = len(pairs)

    def body(i_ref, j_ref, q0, q1, k0, k1, kt0, kt1, v0, v1, do_ref, lse0, lse1, dl0, dl1, *rest):
        dq_ref, dqx_ref, dk_ref, dv_ref, dkx_ref = rest[ng:ng + 5]
        dq_acc, dk_acc, dv_acc = rest[2 * ng + 5:2 * ng + 8]
        p_, s_ = pl.program_id(0), pl.program_id(1)
        i, j = i_ref[s_], j_ref[s_]
        lane = lax.broadcasted_iota(jnp.int32, (1, BLK), 1)
        lo = lane < 64
        if scatter:
            start, finish = _scatter_plan(rest[:ng], rest[ng + 5:2 * ng + 5], scatter[1], *rest[2 * ng + 8:])
            pl.when((p_ == 0) & (s_ == 0))(start)

        @pl.when(s_ == 0)
        def _():
            dq_acc[...] = jnp.zeros_like(dq_acc)

        @pl.when(i == j)
        def _():
            dk_acc[...] = jnp.zeros_like(dk_acc)
            dv_acc[...] = jnp.zeros_like(dv_acc)

        def step(masked):
            dob = do_ref[...]
            rows = pl.ds(pl.multiple_of(i * t, t), t)
            wide = lambda ref: jnp.concatenate([ref[...]] * (t // BLK), axis=1)
            for e, (q_ref, k_ref, kt_ref, v_ref, lse_ref, dl_ref) in enumerate(
                    ((q0, k0, kt0, v0, lse0, dl0), (q1, k1, kt1, v1, lse1, dl1))):
                s = _dot(q_ref[...], kt_ref[...], 1, 0)
                if masked:
                    s = jnp.where(_fox_mask(i, j, t), s, NEG)
                pe = jnp.exp(s - wide(lse_ref))
                dp = _dot(dob, v_ref[...], 1, 1)
                ds = (pe * (dp - wide(dl_ref))).astype(BF16)
                dq_acc[e, rows, :] += _dot(ds, k_ref[...], 1, 0)
                dk_acc[e] += _dot(ds, q_ref[...], 0, 0)
                dv_acc[e] += _dot(pe.astype(BF16), dob, 0, 0)

        pl.when((i > j) & (j > 0))(lambda: step(False))
        pl.when((i == j) | ((j == 0) & (i > 0)))(lambda: step(True))

        @pl.when(i == nt - 1)
        def _():
            dk_ref[...] = jnp.where(lo, dk_acc[0], dk_acc[1]).astype(BF16)
            dv_ref[...] = jnp.where(lo, dv_acc[0], dv_acc[1]).astype(BF16)
            dkx_ref[...] = jnp.where(lo, dk_acc[1], dk_acc[0])

        @pl.when(s_ == ns - 1)
        def _():
            dq_ref[...] = (jnp.where(lo, dq_acc[0], dq_acc[1]) * SCALE).astype(BF16)
            dqx_ref[...] = jnp.where(lo, dq_acc[1], dq_acc[0])

        if scatter:
            pl.when((p_ == NH // 2 - 1) & (s_ == ns - 1))(finish)

    qs = lambda e: pl.BlockSpec((None, t, BLK), lambda p, s, ii, jj: (2 * p + e, ii[s], 0))
    ks = lambda e: pl.BlockSpec((None, t, BLK), lambda p, s, ii, jj: (2 * p + e, jj[s], 0))
    kts = lambda e: pl.BlockSpec((None, BLK, t), lambda p, s, ii, jj: (2 * p + e, 0, jj[s]))
    qside = pl.BlockSpec((t, BLK), lambda p, s, ii, jj: (ii[s], p))
    kside = pl.BlockSpec((t, BLK), lambda p, s, ii, jj: (jj[s], p))
    rep = pl.BlockSpec((None, t, BLK), lambda p, s, ii, jj: (p, ii[s], 0))
    whole = pl.BlockSpec((T, BLK), lambda p, s, ii, jj: (0, p))
    sums = list(scatter[0]) if scatter else []
    return pl.pallas_call(
        body,
        grid_spec=pltpu.PrefetchScalarGridSpec(
            num_scalar_prefetch=2, grid=(NH // 2, ns),
            in_specs=[qs(0), qs(1), ks(0), ks(1), kts(0), kts(1), ks(0), ks(1), qside, rep, rep, rep, rep] + [ANY] * ng,
            out_specs=[whole, whole, kside, kside, kside] + [ANY] * ng,
            scratch_shapes=[pltpu.VMEM((2, T, BLK), F32), pltpu.VMEM((2, t, BLK), F32), pltpu.VMEM((2, t, BLK), F32)]
            + ([pltpu.SemaphoreType.DMA((3 * ng,)), pltpu.SemaphoreType.DMA((3 * ng,))] if scatter else [])),
        out_shape=[jax.ShapeDtypeStruct((T, 512), BF16), jax.ShapeDtypeStruct((T, 512), F32),
                   jax.ShapeDtypeStruct((T, 512), BF16), jax.ShapeDtypeStruct((T, 512), BF16),
                   jax.ShapeDtypeStruct((T, 512), F32)] + (_scatter_shapes(sums, scatter[1]) if scatter else []),
        compiler_params=(pltpu.CompilerParams(dimension_semantics=("arbitrary",) * 2, vmem_limit_bytes=VMEM_LIMIT,
                                              has_side_effects=True) if scatter
                         else _cp("parallel", "arbitrary")), name=name)(
            i_of, j_of, qaug, qaug, kaug, kaug, kaug_t, kaug_t, vm, vm, do, *lses, *deltas, *sums)


def _fox_mask(i, j, t):
    row = i * t + lax.broadcasted_iota(jnp.int32, (t, t), 0)
    col = j * t + lax.broadcasted_iota(jnp.int32, (t, t), 1)
    return (col <= row) & (col >= NPAD)


def _lane_pick(x, lane, idx):
    return jnp.sum(jnp.where(lane == idx, x, 0.0), axis=-1, keepdims=True)


def _lru_gates(xc, wr_ref, wi_ref, vec_ref):
    xb = xc.astype(BF16)
    pre_r = jnp.concatenate([_dot(xb[:, p * BLK:(p + 1) * BLK], wr_ref[p], 1, 0) for p in range(LW // BLK)], axis=1)
    pre_i = jnp.concatenate([_dot(xb[:, p * BLK:(p + 1) * BLK], wi_ref[p], 1, 0) for p in range(LW // BLK)], axis=1)
    r = _sigmoid(pre_r + vec_ref[0:1, :])
    gi = _sigmoid(pre_i + vec_ref[1:2, :])
    log_a = LRU_C * r * _log_sigmoid(vec_ref[2:3, :])
    a = jnp.exp(log_a)
    mult = jnp.sqrt(_neg_expm1(2.0 * log_a))
    return r, gi, a, mult


def _conv(xbuf_ref, x, cw_ref, vec_ref, tr):
    return (cw_ref[3:4, :] * x + cw_ref[2:3, :] * xbuf_ref[7:7 + tr, :] + cw_ref[1:2, :] * xbuf_ref[6:6 + tr, :]
            + cw_ref[0:1, :] * xbuf_ref[5:5 + tr, :] + vec_ref[3:4, :])


def _lru_fwd(proj, cw, wr, wi, vec, name):
    T = proj.shape[0]
    tr = _rt(T)

    def body(x_ref, y_ref, cw_ref, wr_ref, wi_ref, vec_ref, oc_ref, oct_ref, hs_ref, xbuf, abuf, bbuf, hcar):
        i = pl.program_id(0)

        @pl.when(i == 0)
        def _():
            xbuf[0:8, :] = jnp.zeros((8, LW), F32)
            hcar[...] = jnp.zeros_like(hcar)

        x = x_ref[...]
        xbuf[8:8 + tr, :] = x
        xc = _conv(xbuf, x, cw_ref, vec_ref, tr)
        xbuf[0:8, :] = x[tr - 8:tr, :]
        _, gi, a, mult = _lru_gates(xc, wr_ref, wi_ref, vec_ref)
        rows = i * tr + lax.broadcasted_iota(jnp.int32, (tr, 1), 0)
        abuf[...] = a
        bbuf[...] = jnp.where(rows >= NPAD, mult * (gi * xc), 0.0)
        sub = lax.broadcasted_iota(jnp.int32, (8, 1), 0)

        def step(k, h):
            sl = pl.ds(pl.multiple_of(k * 8, 8), 8)
            a8, b8 = abuf[sl, :], bbuf[sl, :]
            for s in (1, 2, 4):
                ok = sub >= s
                b8 = jnp.where(ok, a8 * pltpu.roll(b8, s, 0) + b8, b8)
                a8 = jnp.where(ok, a8 * pltpu.roll(a8, s, 0), a8)
            h8 = a8 * h + b8
            bbuf[sl, :] = h8
            return h8[7:8, :]

        hcar[...] = lax.fori_loop(0, tr // 8, step, hcar[...])
        hs = bbuf[...]
        hs_ref[...] = hs
        oc = (hs * _gelu(y_ref[...])).astype(BF16)
        oc_ref[...] = oc
        oct_ref[...] = oc.T

    row = pl.BlockSpec((tr, LW), lambda i: (i, 0))
    full = lambda shape: pl.BlockSpec(shape, lambda i: (0,) * len(shape))
    return pl.pallas_call(
        body, grid=(T // tr,),
        in_specs=[pl.BlockSpec((tr, LW), lambda i: (i, XC // LW)), pl.BlockSpec((tr, LW), lambda i: (i, YC // LW)),
                  full((4, LW)), full((4, BLK, BLK)), full((4, BLK, BLK)), full((8, LW))],
        out_specs=[row, pl.BlockSpec((LW, tr), lambda i: (0, i)), row],
        out_shape=[jax.ShapeDtypeStruct((T, LW), BF16), jax.ShapeDtypeStruct((LW, T), BF16), jax.ShapeDtypeStruct((T, LW), F32)],
        scratch_shapes=[pltpu.VMEM((tr + 8, LW), F32), pltpu.VMEM((tr, LW), F32), pltpu.VMEM((tr, LW), F32),
                        pltpu.VMEM((1, LW), F32)],
        compiler_params=_cp("arbitrary"), name=name)(proj, proj, cw, wr, wi, vec)


def _lru_bwd(proj, hs, doc, cw, wr, wi, vec, name):
    T = proj.shape[0]
    tr = _rt(T)
    nt = T // tr
    r8 = tr // 8

    def body(x_ref, xp_ref, y_ref, hs_ref, hp_ref, do_ref, cw_ref, wr_ref, wi_ref, vec_ref,
             dx_ref, dy_ref, dwr_ref, dwi_ref, dvec_ref, xbuf, abuf, gbuf, hbuf, dbuf, gcar, acar):
        k = pl.program_id(0)
        i = nt - 1 - k

        @pl.when(k == 0)
        def _():
            dwr_ref[...] = jnp.zeros_like(dwr_ref)
            dwi_ref[...] = jnp.zeros_like(dwi_ref)
            dvec_ref[...] = jnp.zeros_like(dvec_ref)
            gcar[...] = jnp.zeros_like(gcar)
            acar[...] = jnp.zeros_like(acar)
            dbuf[tr:tr + 8, :] = jnp.zeros((8, LW), F32)

        first = i == 0
        x = x_ref[...]
        xbuf[0:8, :] = jnp.where(first, 0.0, xp_ref[...])
        xbuf[8:8 + tr, :] = x
        xc = _conv(xbuf, x, cw_ref, vec_ref, tr)
        r, gi, a, mult = _lru_gates(xc, wr_ref, wi_ref, vec_ref)
        y = y_ref[...]
        hs = hs_ref[...]
        do_ = do_ref[...].astype(F32)
        rows = i * tr + lax.broadcasted_iota(jnp.int32, (tr, 1), 0)
        abuf[0:tr, :] = a
        abuf[tr:tr + 8, :] = jnp.zeros((8, LW), F32) + acar[...]
        an = abuf[1:1 + tr, :]
        acar[...] = a[0:1, :]
        abuf[0:tr, :] = an
        gbuf[...] = do_ * _gelu(y)
        sub = lax.broadcasted_iota(jnp.int32, (8, 1), 0)

        def step(kk, g):
            sl = pl.ds(pl.multiple_of((r8 - 1 - kk) * 8, 8), 8)
            a8, b8 = abuf[sl, :], gbuf[sl, :]
            for s in (1, 2, 4):
                ok = sub < 8 - s
                b8 = jnp.where(ok, a8 * pltpu.roll(b8, 8 - s, 0) + b8, b8)
                a8 = jnp.where(ok, a8 * pltpu.roll(a8, 8 - s, 0), a8)
            g8 = a8 * g + b8
            gbuf[sl, :] = g8
            return g8[0:1, :]

        gcar[...] = lax.fori_loop(0, r8, step, gcar[...])
        g = gbuf[...]
        hbuf[0:8, :] = jnp.where(first, 0.0, hp_ref[...])
        hbuf[8:8 + tr, :] = hs
        hprev = hbuf[7:7 + tr, :]
        dinp = jnp.where(rows >= NPAD, g, 0.0)
        da = g * hprev
        dmult = dinp * gi * xc
        dgi = dinp * mult * xc
        dxc = dinp * mult * gi
        dlog_a = da * a - dmult * a * a / mult
        ls = _log_sigmoid(vec_ref[2:3, :])
        dpre_r = dlog_a * (LRU_C * ls) * r * (1.0 - r)
        dpre_i = dgi * gi * (1.0 - gi)
        xb = xc.astype(BF16)
        rb, ib = dpre_r.astype(BF16), dpre_i.astype(BF16)
        back = []
        for p in range(LW // BLK):
            c = slice(p * BLK, (p + 1) * BLK)
            back.append(_dot(rb[:, c], wr_ref[p], 1, 1) + _dot(ib[:, c], wi_ref[p], 1, 1))
            dwr_ref[p] += _dot(xb[:, c], rb[:, c], 0, 0)
            dwi_ref[p] += _dot(xb[:, c], ib[:, c], 0, 0)
        dxc = dxc + jnp.concatenate(back, axis=1)
        col = lambda v: jnp.sum(v, axis=0, keepdims=True)
        dvec_ref[0:1, :] += col(dpre_r)
        dvec_ref[1:2, :] += col(dpre_i)
        dvec_ref[2:3, :] += col(dlog_a * (LRU_C * r)) * _sigmoid(-vec_ref[2:3, :])
        dvec_ref[3:4, :] += col(dxc)
        dvec_ref[4:5, :] += col(dxc * xbuf[5:5 + tr, :])
        dvec_ref[5:6, :] += col(dxc * xbuf[6:6 + tr, :])
        dvec_ref[6:7, :] += col(dxc * xbuf[7:7 + tr, :])
        dvec_ref[7:8, :] += col(dxc * x)
        dbuf[0:tr, :] = dxc
        dxr = (cw_ref[3:4, :] * dxc + cw_ref[2:3, :] * dbuf[1:1 + tr, :] + cw_ref[1:2, :] * dbuf[2:2 + tr, :]
               + cw_ref[0:1, :] * dbuf[3:3 + tr, :])
        dbuf[tr:tr + 8, :] = dxc[0:8, :]
        dx_ref[...] = jnp.where(rows >= NPAD, dxr, 0.0).astype(BF16)
        dy_ref[...] = (do_ * hs * _gelu_grad(y)).astype(BF16)

    rev = lambda k: nt - 1 - k
    row = lambda col0: pl.BlockSpec((tr, LW), lambda k: (rev(k), col0))
    prev8 = lambda col0: pl.BlockSpec((8, LW), lambda k: (jnp.maximum(rev(k) * r8 - 1, 0), col0))
    full = lambda shape: pl.BlockSpec(shape, lambda k: (0,) * len(shape))
    return pl.pallas_call(
        body, grid=(nt,),
        in_specs=[row(XC // LW), prev8(XC // LW), row(YC // LW), row(0), prev8(0), row(0),
                  full((4, LW)), full((4, BLK, BLK)), full((4, BLK, BLK)), full((8, LW))],
        out_specs=[row(0), row(0), full((4, BLK, BLK)), full((4, BLK, BLK)), full((8, LW))],
        out_shape=[jax.ShapeDtypeStruct((T, LW), BF16), jax.ShapeDtypeStruct((T, LW), BF16),
                   jax.ShapeDtypeStruct((4, BLK, BLK), F32), jax.ShapeDtypeStruct((4, BLK, BLK), F32),
                   jax.ShapeDtypeStruct((8, LW), F32)],
        scratch_shapes=[pltpu.VMEM((tr + 8, LW), F32), pltpu.VMEM((tr + 8, LW), F32), pltpu.VMEM((tr, LW), F32),
                        pltpu.VMEM((tr + 8, LW), F32), pltpu.VMEM((tr + 8, LW), F32),
                        pltpu.VMEM((1, LW), F32), pltpu.VMEM((1, LW), F32)],
        compiler_params=_cp("arbitrary"), name=name)(proj, proj, proj, hs, hs, doc, cw, wr, wi, vec)


def _branch_merge_fwd(oa, of, oc, wb, proj, name):
    T = proj.shape[0]
    tm, tn = _rt(T), 512

    def body(a0, a1, a2, w_ref, g0, g1, g2, r0, r1, r2, m_ref, mt_ref):
        acc = None
        for g, (a_ref, g_ref, r_ref) in enumerate(((a0, g0, r0), (a1, g1, r1), (a2, g2, r2))):
            b = _dot(a_ref[...], w_ref[g], 1, 0)
            r_ref[...] = b
            term = _sigmoid(g_ref[...]) * b
            acc = term if acc is None else acc + term
        m = acc.astype(BF16)
        m_ref[...] = m
        mt_ref[...] = m.T

    act = pl.BlockSpec((tm, LW), lambda j, i: (i, 0))
    gate = lambda g: pl.BlockSpec((tm, tn), lambda j, i: (i, (GT + g * D) // tn + j))
    blk = pl.BlockSpec((tm, tn), lambda j, i: (i, j))
    return pl.pallas_call(
        body, grid=(D // tn, T // tm),
        in_specs=[act, act, act, pl.BlockSpec((3, LW, tn), lambda j, i: (0, 0, j)), gate(0), gate(1), gate(2)],
        out_specs=[blk] * 4 + [pl.BlockSpec((tn, tm), lambda j, i: (j, i))],
        out_shape=[jax.ShapeDtypeStruct((T, D), F32)] * 3 + [jax.ShapeDtypeStruct((T, D), BF16), jax.ShapeDtypeStruct((D, T), BF16)],
        compiler_params=_cp("parallel", "parallel"), name=name)(oa, of, oc, wb, proj, proj, proj)


def _out_dx_merge_bwd(dhb, w_out, proj, b0, b1, b2, name):
    T = proj.shape[0]
    tm, tn = _rt(T), 512

    def body(dh_ref, w_ref, g0, g1, g2, r0, r1, r2, d0, d1, d2, e0, e1, e2):
        dmv = _dot(dh_ref[...], w_ref[...], 1, 1)
        for g_ref, r_ref, d_ref, e_ref in ((g0, r0, d0, e0), (g1, r1, d1, e1), (g2, r2, d2, e2)):
            sg = _sigmoid(g_ref[...])
            d_ref[...] = (dmv * sg).astype(BF16)
            e_ref[...] = (dmv * r_ref[...] * sg * (1.0 - sg)).astype(BF16)

    gate = lambda g: pl.BlockSpec((tm, tn), lambda j, i: (i, (GT + g * D) // tn + j))
    blk = pl.BlockSpec((tm, tn), lambda j, i: (i, j))
    return pl.pallas_call(
        body, grid=(D // tn, T // tm),
        in_specs=[pl.BlockSpec((tm, D), lambda j, i: (i, 0)), pl.BlockSpec((tn, D), lambda j, i: (j, 0)),
                  gate(0), gate(1), gate(2), blk, blk, blk],
        out_specs=[blk] * 6, out_shape=[jax.ShapeDtypeStruct((T, D), BF16)] * 6,
        compiler_params=_cp("parallel", "parallel"), name=name)(dhb, w_out, proj, proj, proj, b0, b1, b2)


def _ffn_in_swiglu_fwd(u, w, name):
    T = u.shape[0]
    tm, tn = _rt(T), _pick(DFF, (1408, 256))
    nj = DFF // tn

    def body(u_ref, wg_ref, wu_ref, g_ref, up_ref, a_ref, at_ref):
        ub = u_ref[...]
        g = _dot(ub, wg_ref[...], 1, 0)
        up = _dot(ub, wu_ref[...], 1, 0)
        g_ref[...] = g
        up_ref[...] = up
        a = (g * _sigmoid(g) * up).astype(BF16)
        a_ref[...] = a
        at_ref[...] = a.T

    blk = pl.BlockSpec((tm, tn), lambda j, i: (i, j))
    return pl.pallas_call(
        body, grid=(nj, T // tm),
        in_specs=[pl.BlockSpec((tm, D), lambda j, i: (i, 0)), pl.BlockSpec((D, tn), lambda j, i: (0, j)),
                  pl.BlockSpec((D, tn), lambda j, i: (0, j + nj))],
        out_specs=[blk] * 3 + [pl.BlockSpec((tn, tm), lambda j, i: (j, i))],
        out_shape=[jax.ShapeDtypeStruct((T, DFF), F32)] * 2 + [jax.ShapeDtypeStruct((T, DFF), BF16),
                                                               jax.ShapeDtypeStruct((DFF, T), BF16)],
        compiler_params=_cp("parallel", "parallel"), name=name)(u, w, w)


def _ffn_out_dx_swiglu_bwd(dhb, w, gate, up, name):
    T = dhb.shape[0]
    tm, tn = _rt(T), _pick(DFF, (1408, 256))

    def body(dh_ref, w_ref, g_ref, up_ref, dg_ref, du_ref):
        d = _dot(dh_ref[...], w_ref[...], 1, 1)
        g = g_ref[...]
        sg = _sigmoid(g)
        dg_ref[...] = (d * up_ref[...] * (sg + g * sg * (1.0 - sg))).astype(BF16)
        du_ref[...] = (d * g * sg).astype(BF16)

    blk = pl.BlockSpec((tm, tn), lambda j, i: (i, j))
    return pl.pallas_call(
        body, grid=(DFF // tn, T // tm),
        in_specs=[pl.BlockSpec((tm, D), lambda j, i: (i, 0)), pl.BlockSpec((tn, D), lambda j, i: (j, 0)), blk, blk],
        out_specs=[blk] * 2, out_shape=[jax.ShapeDtypeStruct((T, DFF), BF16)] * 2,
        compiler_params=_cp("parallel", "parallel"), name=name)(dhb, w, gate, up)


def _adamw(w, g, m, v, name):
    R, C = w.shape
    tr = _pick(R, tuple(t for t in (512, 256, 128, 64, 32, 16, 8) if t * C * 4 <= (3 << 19)))
    c1 = 1.0 - ADAM_B1 ** ADAM_STEP
    c2 = 1.0 - ADAM_B2 ** ADAM_STEP

    def body(w_ref, g_ref, m_ref, v_ref, d_ref, mo_ref, vo_ref):
        gv = g_ref[...]
        mn = ADAM_B1 * m_ref[...] + (1.0 - ADAM_B1) * gv
        vn = ADAM_B2 * v_ref[...] + (1.0 - ADAM_B2) * (gv * gv)
        d_ref[...] = -ADAM_LR * ((mn / c1) / (jnp.sqrt(vn / c2) + ADAM_EPS) + ADAM_WD * w_ref[...])
        mo_ref[...] = mn
        vo_ref[...] = vn

    blk = pl.BlockSpec((tr, C), lambda i: (i, 0))
    return pl.pallas_call(
        body, grid=(R // tr,), in_specs=[blk] * 4, out_specs=[blk] * 3,
        out_shape=[jax.ShapeDtypeStruct((R, C), F32)] * 3, compiler_params=_cp("parallel"), name=name)(w, g, m, v)


def _sum_lead(x, name):
    n, R, C = x.shape
    tr = _pick(R, (512, 256, 128, 64, 32, 16, 8))

    def body(x_ref, o_ref):
        acc = x_ref[0]
        for d in range(1, n):
            acc = acc + x_ref[d]
        o_ref[...] = acc

    return pl.pallas_call(
        body, grid=(R // tr,), in_specs=[pl.BlockSpec((n, tr, C), lambda i: (0, i, 0))],
        out_specs=pl.BlockSpec((tr, C), lambda i: (i, 0)), out_shape=jax.ShapeDtypeStruct((R, C), F32),
        compiler_params=_cp("parallel"), name=name)(x)


def _here():
    return lax.axis_index("x"), lax.axis_index("y"), lax.axis_index("c")


def _rcopy(src, dst, send_sems, recv_sems, k, to):
    return pltpu.make_async_remote_copy(src_ref=src, dst_ref=dst, send_sem=send_sems.at[k], recv_sem=recv_sems.at[k],
                                        device_id=to, device_id_type=MESH)


def _hbm_calls(body, args, out_shapes, n_sems, aliases, name):
    return pl.pallas_call(
        body, in_specs=[ANY] * len(args), out_specs=[ANY] * len(out_shapes), out_shape=out_shapes,
        input_output_aliases=aliases,
        scratch_shapes=[pltpu.SemaphoreType.DMA((n_sems,)), pltpu.SemaphoreType.DMA((n_sems,))],
        compiler_params=pltpu.CompilerParams(has_side_effects=True), name=name)(*args)


def _gather_plan(outs, axes, send_sems, recv_sems):
    x, y, c = _here()
    sib = (x, y, 1 - c)
    chips = [(1 - x, y), (x, 1 - y), (1 - x, 1 - y)]
    todo = [(t, k, chip) for t in range(len(outs)) for k, chip in enumerate(chips)]

    def win(t, chip, hc):
        o, ax = outs[t], axes[t]
        w = o.shape[ax] // N_SHARD
        first = (2 * chip[0] + chip[1]) * w
        if ax == 0:
            return o.at[pl.ds(pl.multiple_of(first + hc * (w // 2), 16), w // 2), :]
        rows = o.shape[0] // 2
        return o.at[pl.ds(pl.multiple_of(hc * rows, 16), rows), pl.ds(pl.multiple_of(first, BLK), w)]

    def copy(t, k, chip, hc, to):
        return _rcopy(win(t, chip, hc), win(t, chip, hc), send_sems, recv_sems, 6 * t + k, to)

    def start():
        for t, k, chip in todo:
            copy(t, k, (x, y), c, (*chip, c)).start()

    def finish():
        for t, k, chip in todo:
            copy(t, k, chip, c, (*chip, c)).wait_recv()
            copy(t, 3 + k, chip, c, sib).start()
        for t, k, chip in todo:
            copy(t, 3 + k, chip, 1 - c, sib).wait_recv()
        for t, k, chip in todo:
            copy(t, k, (x, y), c, (*chip, c)).wait_send()
            copy(t, 3 + k, chip, c, sib).wait_send()

    return start, finish


def _all_gather_weights(fulls, axes, name):
    nt = len(fulls)

    def body(*refs):
        start, finish = _gather_plan(refs[nt:2 * nt], axes, *refs[2 * nt:])
        start()
        finish()

    return _hbm_calls(body, fulls, [jax.ShapeDtypeStruct(f.shape, f.dtype) for f in fulls], 6 * nt,
                      {t: t for t in range(nt)}, name)


def _half(ref, ax, hc):
    n = ref.shape[ax] // 2
    sl = pl.ds(pl.multiple_of(hc * n, 8), n)
    return ref.at[sl, :] if ax == 0 else ref.at[:, sl]


def _shrunk(shape, ax, by):
    shape = list(shape)
    shape[ax] //= by
    return tuple(shape)


def _swap_halves(gs, haxes, name):
    nt = len(gs)

    def body(*refs):
        ins, outs, (send_sems, recv_sems) = refs[:nt], refs[nt:2 * nt], refs[2 * nt:]
        x, y, c = _here()
        cps = [_rcopy(_half(g, ax, 1 - c), o, send_sems, recv_sems, t, (x, y, 1 - c))
               for t, (g, o, ax) in enumerate(zip(ins, outs, haxes))]
        for cp in cps:
            cp.start()
        for cp in cps:
            cp.wait()

    return _hbm_calls(body, gs, [jax.ShapeDtypeStruct(_shrunk(g.shape, ax, 2), g.dtype) for g, ax in zip(gs, haxes)],
                      nt, {}, name)


def _scatter_plan(ins, outs, saxes, send_sems, recv_sems):
    x, y, c = _here()
    chips = [(1 - x, y), (x, 1 - y), (1 - x, 1 - y)]

    def copies():
        cps = []
        for t, (s, o, ax) in enumerate(zip(ins, outs, saxes)):
            w = s.shape[ax] // N_SHARD
            for k, chip in enumerate(chips):
                first = pl.multiple_of((2 * chip[0] + chip[1]) * w, 8)
                src = s.at[pl.ds(first, w), :] if ax == 0 else s.at[:, pl.ds(first, w)]
                cps.append(_rcopy(src, o.at[k], send_sems, recv_sems, 3 * t + k, (*chip, c)))
        return cps

    def start():
        for cp in copies():
            cp.start()

    def finish():
        for cp in copies():
            cp.wait()

    return start, finish


def _scatter_shapes(sbs, saxes):
    return [jax.ShapeDtypeStruct((3,) + _shrunk(s.shape, ax, N_SHARD), s.dtype) for s, ax in zip(sbs, saxes)]


def _scatter_to_chips(sbs, saxes, name):
    nt = len(sbs)

    def body(*refs):
        start, finish = _scatter_plan(refs[:nt], refs[nt:2 * nt], saxes, *refs[2 * nt:])
        start()
        finish()

    return _hbm_calls(body, sbs, _scatter_shapes(sbs, saxes), 3 * nt, {}, name)


def _join_halves(fins, haxes, name):
    nt = len(fins)

    def body(*refs):
        outs, (send_sems, recv_sems) = refs[nt:2 * nt], refs[2 * nt:]
        x, y, c = _here()
        cps = [_rcopy(_half(o, ax, c), _half(o, ax, c), send_sems, recv_sems, t, (x, y, 1 - c))
               for t, (o, ax) in enumerate(zip(outs, haxes))]
        for cp in cps:
            cp.start()
        for t, (o, ax) in enumerate(zip(outs, haxes)):
            _rcopy(_half(o, ax, 1 - c), _half(o, ax, 1 - c), send_sems, recv_sems, t, (x, y, 1 - c)).wait_recv()
        for cp in cps:
            cp.wait_send()

    return _hbm_calls(body, fins, [jax.ShapeDtypeStruct(f.shape, f.dtype) for f in fins], nt, {t: t for t in range(nt)}, name)


def _all_gather_small(buf, name):
    def body(_, out_ref, send_sems, recv_sems):
        x, y, c = _here()
        me = 4 * x + 2 * y + c
        cps = []
        for k in range(1, 8):
            to = (x ^ ((k >> 2) & 1), y ^ ((k >> 1) & 1), c ^ (k & 1))
            peer = 4 * to[0] + 2 * to[1] + to[2]
            cps.append((_rcopy(out_ref.at[me], out_ref.at[me], send_sems, recv_sems, k - 1, to),
                        _rcopy(out_ref.at[peer], out_ref.at[peer], send_sems, recv_sems, k - 1, to)))
        for snd, _ in cps:
            snd.start()
        for _, rcv in cps:
            rcv.wait_recv()
        for snd, _ in cps:
            snd.wait_send()

    return _hbm_calls(body, [buf], [jax.ShapeDtypeStruct(buf.shape, buf.dtype)], 7, {0: 0}, name)[0]


def _place(block, n, index):
    buf = jnp.zeros((n,) + block.shape[1:], block.dtype)
    return lax.dynamic_update_slice_in_dim(buf, block, index, axis=0)


def _add_half(g, other, hax, cidx, name):
    r, cw = other.shape
    tr = _pick(r, tuple(t for t in (512, 256, 128, 64, 32, 16, 8) if t * cw * 4 <= (1 << 21)))
    nr = r // tr

    def body(c_ref, g_ref, o_ref, s_ref, sb_ref):
        s = g_ref[...] + o_ref[...]
        s_ref[...] = s
        sb_ref[...] = s.astype(BF16)

    g_map = (lambda i, c: (c[0] * nr + i, 0)) if hax == 0 else (lambda i, c: (i, c[0]))
    blk = pl.BlockSpec((tr, cw), lambda i, c: (i, 0))
    return pl.pallas_call(
        body,
        grid_spec=pltpu.PrefetchScalarGridSpec(
            num_scalar_prefetch=1, grid=(nr,), in_specs=[pl.BlockSpec((tr, cw), g_map), blk], out_specs=[blk, blk]),
        out_shape=[jax.ShapeDtypeStruct((r, cw), F32), jax.ShapeDtypeStruct((r, cw), BF16)],
        compiler_params=_cp("parallel"), name=name)(cidx, g, other)


def _add_chips(s, recv, sax, chip_idx, name):
    _, r, cw = recv.shape
    tr = _pick(r, tuple(t for t in (512, 352, 256, 128, 64, 32, 16, 8) if t * cw * 4 <= (1 << 21)))
    nr = r // tr

    def body(c_ref, s_ref, r_ref, out_ref):
        out_ref[...] = ((s_ref[...] + r_ref[0].astype(F32)) + r_ref[1].astype(F32)) + r_ref[2].astype(F32)

    s_map = (lambda i, c: (c[0] * nr + i, 0)) if sax == 0 else (lambda i, c: (i, c[0]))
    return pl.pallas_call(
        body,
        grid_spec=pltpu.PrefetchScalarGridSpec(
            num_scalar_prefetch=1, grid=(nr,),
            in_specs=[pl.BlockSpec((tr, cw), s_map), pl.BlockSpec((3, tr, cw), lambda i, c: (0, i, 0))],
            out_specs=pl.BlockSpec((tr, cw), lambda i, c: (i, 0))),
        out_shape=jax.ShapeDtypeStruct((r, cw), F32), compiler_params=_cp("parallel"), name=name)(chip_idx, s, recv)


IN_SHARD = IN_COLS // N_SHARD
IN_SLOT = INP // N_SHARD
IN_PIECES = ((0, 512, QA), (512, 640, KA), (640, 768, VA), (768, 1280, QF), (1280, 1792, KF), (1792, 2304, VF),
             (2304, 2312, FL), (2312, 2824, XC), (2824, 3336, YC), (3336, 6408, GT))


def _gathered_to_kernel_cols(w):
    parts, pos = [], 0
    for a, b, k in sorted(IN_PIECES, key=lambda p: p[2]):
        assert k == pos
        while a < b:
            j = a // IN_SHARD
            e = min(b, (j + 1) * IN_SHARD)
            g = j * IN_SLOT + a - j * IN_SHARD
            parts.append(w[..., g:g + e - a])
            pos += e - a
            a = e
    parts.append(jnp.zeros(w.shape[:-1] + (INP - pos,), w.dtype))
    return jnp.concatenate(parts, axis=-1)


def _kernel_to_gathered_cols(w):
    parts = []
    for j in range(N_SHARD):
        lo, hi = j * IN_SHARD, (j + 1) * IN_SHARD
        for a, b, k in IN_PIECES:
            s, e = max(a, lo), min(b, hi)
            if s < e:
                parts.append(w[..., k + s - a:k + e - a])
        parts.append(jnp.zeros(w.shape[:-1] + (IN_SLOT - IN_SHARD,), w.dtype))
    return jnp.concatenate(parts, axis=-1)


def _pair_blocks(w):
    z = jnp.zeros((4, 64, 64), w.dtype)
    w = w.reshape(4, 2, 64, 64)
    top = jnp.concatenate([w[:, 0], z], axis=2)
    bot = jnp.concatenate([z, w[:, 1]], axis=2)
    return jnp.concatenate([top, bot], axis=1)


def _unpair_blocks(w):
    return jnp.stack([w[:, :64, :64], w[:, 64:, 64:]], axis=1).reshape(8, 64, 64)


BIG = ("w_in", "w_branch", "w_out", "w_ffn_in", "w_ffn_out")
TINY = ("conv_w", "meta_tokens")
SMALL = ("rel_bias_table", "norm_mix", "swa_sinks", "fox_forget_bias", "conv_b", "lru_w_r", "lru_b_r", "lru_w_i",
         "lru_b_i", "lru_lambda", "norm_ffn", "norm_final")
SHARD_AXIS = {"conv_w": 2, "meta_tokens": 1}
BIG_AXIS = {"w_in": 2, "w_branch": 2, "w_out": 1, "w_ffn_in": 2, "w_ffn_out": 1}


def _pack(d, names):
    flat = jnp.concatenate([d[n].reshape(-1) for n in names])
    pad = (-flat.shape[0]) % (256 * 128)
    return jnp.concatenate([flat, jnp.zeros((pad,), F32)]).reshape(-1, 128)


def _unpack(buf, names, shapes):
    flat, out, off = buf.reshape(-1), {}, 0
    for n in names:
        sz = int(np.prod(shapes[n]))
        out[n] = flat[off:off + sz].reshape(shapes[n])
        off += sz
    return out


def _layer_layout(n, a):
    if n == "w_in":
        return _gathered_to_kernel_cols(a)
    return a.reshape(3, LW, D) if n == "w_branch" else a


def _local_step(x, tgt, W, placed=None):
    S = x.shape[0]
    T = S + BLK
    tm = _pick(T, (1408, 384, 128))
    bucket = jnp.asarray(_bucket_table())
    bias = _bias_build(W["rel_bias_table"], bucket, "bias_build")
    h = jnp.concatenate([jnp.zeros((NPAD, D), F32), W["meta_tokens"], x], axis=0)
    if placed is None:
        WL = {n: [W[n][l] for l in range(DEPTH)] for n in BIG}
    else:
        WL = {n: [W[n]] + [None] * (DEPTH - 1) for n in BIG}

    saved = []
    for l in range(DEPTH):
        sv = {"h0": h}
        u, u_t = _rms_fwd(h, W["norm_mix"][l], f"rms_mix_fwd")
        proj = _mm(u, WL["w_in"][l], tm=tm, tn=512, tk=D, name="mm_in_fwd")
        oa, oa_t = _swa_fwd(proj, bias, W["swa_sinks"][l], "swa_fwd")
        fb = W["fox_forget_bias"][l].reshape(NH, 1)
        qaug, kaug, kaug_t, vm, vo = _fox_prep(proj, _cum_fwd(proj, fb, "cum_fwd"), "fox_prep")
        if placed is not None and l + 1 < DEPTH:
            of, of_t, lse0, lse1, *got = _fox_fwd(qaug, kaug_t, vo, "fox_fwd_gather", gather=(placed[l + 1], GATHER_AXES))
            for n, a in zip(BIG, got):
                WL[n][l + 1] = _layer_layout(n, a)
            lse = [lse0, lse1]
        else:
            of, of_t, *lse = _fox_fwd(qaug, kaug_t, vo, "fox_fwd")
        lru_vec = jnp.concatenate([W["lru_b_r"][l][None], W["lru_b_i"][l][None], W["lru_lambda"][l][None],
                                   W["conv_b"][l][None], jnp.zeros((4, LW), F32)], axis=0)
        oc, oc_t, hs = _lru_fwd(proj, W["conv_w"][l], W["lru_w_r"][l], W["lru_w_i"][l], lru_vec, "lru_fwd")
        *bs, merged, merged_t = _branch_merge_fwd(oa, of, oc, WL["w_branch"][l], proj, "branch_merge_fwd")
        h2 = _mm(merged, WL["w_out"][l], res=h, tm=tm, tn=512, tk=D, name="mm_out_fwd")
        u2, u2_t = _rms_fwd(h2, W["norm_ffn"][l], "rms_ffn_fwd")
        gate, up, act, act_t = _ffn_in_swiglu_fwd(u2, WL["w_ffn_in"][l], "ffn_in_swiglu_fwd")
        h = _mm(act, WL["w_ffn_out"][l], res=h2, tm=tm, tn=512, tk=_pick(DFF, (1408, 256)), name="mm_ffn_out_fwd")
        sv.update(u_t=u_t, proj=proj, o_t=(oa_t, of_t, oc_t), of=of, lse=lse, hs=hs, fb=fb, qaug=qaug, kaug=kaug, kaug_t=kaug_t, vm=vm,
                  lru_vec=lru_vec, bs=bs, merged_t=merged_t, h2=h2, u2_t=u2_t, gate=gate, up=up, act_t=act_t)
        saved.append(sv)

    tgt_pad = tgt
    dh, dhb, dg_final, loss_vec = _loss_head(h, tgt_pad, W["norm_final"], "loss_head")
    loss = loss_vec[0, 0]

    small = ("norm_mix", "swa_sinks", "fox_forget_bias", "conv_w", "conv_b", "lru_w_r", "lru_b_r", "lru_w_i", "lru_b_i",
             "lru_lambda", "norm_ffn")
    G = {n: [None] * DEPTH for n in small}
    G["norm_final"] = dg_final.reshape(D)
    GW = {n: [None] * DEPTH for n in BIG}
    dist = placed is not None
    if dist:
        x_, y_, c_ = _here()
        cidx = jnp.reshape(c_, (1,)).astype(jnp.int32)
        chip = jnp.reshape(2 * x_ + y_, (1,)).astype(jnp.int32)

    def finish_layer(lp, ss, recv):
        fins = []
        for n, s, r, ax, hax in zip(BIG, ss, recv, GATHER_AXES, HALF_AXES):
            tot = _add_chips(s, r, ax, chip, "rs_add_chips_" + n)
            zero = jnp.zeros_like(tot)
            fins.append(jnp.concatenate([jnp.where(c_ == hc, tot, zero) for hc in range(2)], axis=hax))
        for n, f in zip(BIG, _join_halves(fins, HALF_AXES, "rs_join_halves")):
            GW[n][lp] = f

    pend = None
    dbias = jnp.zeros((NH, BLK, 2 * BLK), F32)
    tkT = tm
    for l in reversed(range(DEPTH)):
        sv = saved[l]
        dw = {}
        dw["w_ffn_out"] = _mm(sv["act_t"], dhb, tm=_pick(DFF, (1408, 256)), tn=D, tk=tkT,
                              name="mm_ffn_out_dw")
        dgate, dup = _ffn_out_dx_swiglu_bwd(dhb, WL["w_ffn_out"][l], sv["gate"], sv["up"], "ffn_out_dx_swiglu_bwd")
        u2t = sv["u2_t"]
        du2, buf = None, None
        for half, dpart in enumerate((dgate, dup)):
            buf = _mm(u2t, dpart, tm=D, tn=_pick(DFF, (1408, 256)), tk=tkT, slab=(buf, 0, 1),
                      col0=half * DFF, cols=2 * DFF, name="mm_ffn_in_dw")
            du2 = _mm(dpart, WL["w_ffn_in"][l], tb=True, res=du2, b_k0=half * DFF, tm=tm, tn=512,
                      tk=_pick(DFF, (1408, 256)), name="mm_ffn_in_dx")
        dw["w_ffn_in"] = buf.reshape(D, 2 * DFF)
        dh, dhb, dgn = _rms_bwd(du2, sv["h2"], W["norm_ffn"][l], dh, "rms_ffn_bwd")
        G["norm_ffn"][l] = dgn.reshape(D)
        dw["w_out"] = _mm(sv["merged_t"], dhb, tm=D, tn=D, tk=tkT, name="mm_out_dw")
        db0, db1, db2, dg0, dg1, dg2 = _out_dx_merge_bwd(dhb, WL["w_out"][l], sv["proj"], *sv["bs"], "out_dx_merge_bwd")
        dos, buf = [], None
        for g, (o_t, db) in enumerate(zip(sv["o_t"], (db0, db1, db2))):
            buf = _mm(o_t, db, tm=LW, tn=D, tk=tkT, slab=(buf, g, 3), name="mm_branch_dw")
            dos.append(_mm(db, WL["w_branch"][l][g], tb=True, out_dtype=BF16, tm=tm, tn=LW, tk=D, name="mm_branch_dx"))
        dw["w_branch"] = buf.reshape(3 * LW, D)
        dqa, dkb, dvb, dbias, dsk = _swa_bwd(sv["proj"], bias, W["swa_sinks"][l], dos[0], dbias, "swa_bwd")
        dka, dva = _band_fold(dkb, dvb, "swa_band_fold")
        G["swa_sinks"][l] = dsk[0, :NH]
        delta = _fox_delta(dos[1], sv["of"], "fox_delta")
        fox_args = (sv["qaug"], sv["kaug"], sv["kaug_t"], sv["vm"], dos[1], sv["lse"], delta)
        if pend is not None:
            dqf, dqx, dkf, dvf, dkx, *recv = _fox_bwd(*fox_args, "fox_bwd_scatter", scatter=(pend[2], GATHER_AXES))
            finish_layer(pend[0], pend[1], recv)
            pend = None
        else:
            dqf, dqx, dkf, dvf, dkx = _fox_bwd(*fox_args, "fox_bwd")
        dfl, dfb = _cum_bwd(dqx, dkx, sv["proj"], sv["fb"], "cum_bwd")
        G["fox_forget_bias"][l] = dfb[:, 0]
        dxc, dyc, dwr, dwi, dvec = _lru_bwd(sv["proj"], sv["hs"], dos[2], W["conv_w"][l], W["lru_w_r"][l], W["lru_w_i"][l],
                                            sv["lru_vec"], "lru_bwd")
        G["lru_w_r"][l], G["lru_w_i"][l] = _unpair_blocks(dwr), _unpair_blocks(dwi)
        G["lru_b_r"][l], G["lru_b_i"][l], G["lru_lambda"][l], G["conv_b"][l] = dvec[0], dvec[1], dvec[2], dvec[3]
        G["conv_w"][l] = dvec[4:8]
        dproj = jnp.concatenate([dqa, dqf, dkf, dvf, dxc, dyc, dg0, dg1, dg2, dka, dva, dfl], axis=1)
        dw["w_in"] = _mm(sv["u_t"], dproj, tm=D, tn=IN_SLOT, tk=tkT, name="mm_in_dw")
        du = _mm(dproj, WL["w_in"][l], tb=True, tm=tm, tn=512, tk=_pick(INP, (1664, 512)), name="mm_in_dx")
        dh, dhb, dgn = _rms_bwd(du, sv["h0"], W["norm_mix"][l], dh, "rms_mix_bwd")
        G["norm_mix"][l] = dgn.reshape(D)
        if dist:
            gs = [dw[n] for n in BIG]
            pairs = [_add_half(g, r, hax, cidx, "rs_add_half_" + n)
                     for n, g, r, hax in zip(BIG, gs, _swap_halves(gs, HALF_AXES, "rs_swap_halves"), HALF_AXES)]
            ss, sbs = [list(t) for t in zip(*pairs)]
            ss[0], sbs[0] = _kernel_to_gathered_cols(ss[0]), _kernel_to_gathered_cols(sbs[0])
            pend = (l, ss, sbs)
        else:
            for n in BIG:
                GW[n][l] = dw[n]
    if dist:
        finish_layer(pend[0], pend[1], _scatter_to_chips(pend[2], GATHER_AXES, "rs_scatter"))

    grads = {n: (jnp.stack(v) if isinstance(v, list) else v) for n, v in G.items()}
    grads.update({n: jnp.stack(GW[n]) for n in BIG})
    grads["rel_bias_table"] = _bias_bwd(dbias, bucket, "bias_bwd")
    grads["meta_tokens"] = dh[NPAD:BLK]
    return loss, dh[BLK:], grads


NAMES = ("meta_tokens", "rel_bias_table", "norm_mix", "w_in", "swa_sinks", "fox_forget_bias", "conv_w", "conv_b",
         "lru_w_r", "lru_b_r", "lru_w_i", "lru_b_i", "lru_lambda", "w_branch", "w_out", "norm_ffn", "w_ffn_in",
         "w_ffn_out", "norm_final")


def _three_d(n, a):
    return a.reshape(DEPTH, 3 * LW, -1) if n == "w_branch" else a


GATHER_AXES = [BIG_AXIS[n] - 1 for n in BIG]
HALF_AXES = [1 - a for a in GATHER_AXES]


def _gather_weights(P):
    x, y, c = _here()
    mine, me = 2 * x + y, 4 * x + 2 * y + c
    placed = []
    for l in range(DEPTH):
        bufs = []
        for n in BIG:
            shard = _three_d(n, P[n])[l].astype(BF16)
            if n == "w_in":
                shard = jnp.pad(shard, ((0, 0), (0, IN_SLOT - IN_SHARD)))
            zero = jnp.zeros_like(shard)
            bufs.append(jnp.concatenate([jnp.where(mine == j, shard, zero) for j in range(N_SHARD)], axis=BIG_AXIS[n] - 1))
        placed.append(bufs)
    full = {n: _layer_layout(n, a) for n, a in zip(BIG, _all_gather_weights(placed[0], GATHER_AXES, "ag_weights"))}
    tiny = _all_gather_small(_place(_pack(P, TINY)[None], 8, me), "ag_tiny_weights")
    parts = [_unpack(tiny[2 * j], TINY, {n: P[n].shape for n in TINY}) for j in range(N_SHARD)]
    for n in TINY:
        full[n] = jnp.concatenate([p[n] for p in parts], axis=SHARD_AXIS[n])
    for n in SMALL:
        full[n] = P[n]
    full["lru_w_r"] = jnp.stack([_pair_blocks(P["lru_w_r"][l]) for l in range(DEPTH)]).astype(BF16)
    full["lru_w_i"] = jnp.stack([_pair_blocks(P["lru_w_i"][l]) for l in range(DEPTH)]).astype(BF16)
    return full, placed


def _reduce_grads(grads, P):
    x, y, c = _here()
    mine, me = 2 * x + y, 4 * x + 2 * y + c
    out = {n: grads[n].reshape(P[n].shape) for n in BIG if n != "w_in"}
    out["w_in"] = grads["w_in"][:, :, :IN_SHARD]
    names = SMALL + TINY
    gathered = _all_gather_small(_place(_pack(grads, names)[None], 8, me), "ag_small_grads")
    small = _unpack(_sum_lead(gathered, "sum_small_grads"), names, {n: grads[n].shape for n in names})
    for n in SMALL:
        out[n] = small[n]
    for n in TINY:
        w = P[n].shape[SHARD_AXIS[n]]
        out[n] = lax.dynamic_slice_in_dim(small[n], mine * w, w, axis=SHARD_AXIS[n])
    return out


def _update(P, Gd, M, V):
    delta, new_m, new_v = {}, {}, {}
    for n in BIG + TINY:
        shp = P[n].shape
        two = (int(np.prod(shp[:-1])), shp[-1])
        d, m, v = _adamw(P[n].reshape(two), Gd[n].reshape(two), M[n].reshape(two), V[n].reshape(two), "adamw_" + n)
        delta[n], new_m[n], new_v[n] = d.reshape(shp), m.reshape(shp), v.reshape(shp)
    shapes = {n: P[n].shape for n in SMALL}
    d, m, v = _adamw(_pack(P, SMALL), _pack(Gd, SMALL), _pack(M, SMALL), _pack(V, SMALL), "adamw_small")
    for dst, buf in ((delta, d), (new_m, m), (new_v, v)):
        dst.update(_unpack(buf, SMALL, shapes))
    return delta, new_m, new_v


def kernel(x, meta_tokens, rel_bias_table, norm_mix, w_in, swa_sinks, fox_forget_bias, conv_w, conv_b, lru_w_r, lru_b_r, lru_w_i, lru_b_i, lru_lambda, w_branch, w_out, norm_ffn, w_ffn_in, w_ffn_out, norm_final, loss_target, m_meta_tokens, m_rel_bias_table, m_norm_mix, m_w_in, m_swa_sinks, m_fox_forget_bias, m_conv_w, m_conv_b, m_lru_w_r, m_lru_b_r, m_lru_w_i, m_lru_b_i, m_lru_lambda, m_w_branch, m_w_out, m_norm_ffn, m_w_ffn_in, m_w_ffn_out, m_norm_final, v_meta_tokens, v_rel_bias_table, v_norm_mix, v_w_in, v_swa_sinks, v_fox_forget_bias, v_conv_w, v_conv_b, v_lru_w_r, v_lru_b_r, v_lru_w_i, v_lru_b_i, v_lru_lambda, v_w_branch, v_w_out, v_norm_ffn, v_w_ffn_in, v_w_ffn_out, v_norm_final):
    P = dict(zip(NAMES, (meta_tokens, rel_bias_table, norm_mix, w_in, swa_sinks, fox_forget_bias, conv_w, conv_b, lru_w_r,
                         lru_b_r, lru_w_i, lru_b_i, lru_lambda, w_branch, w_out, norm_ffn, w_ffn_in, w_ffn_out, norm_final)))
    M = dict(zip(NAMES, (m_meta_tokens, m_rel_bias_table, m_norm_mix, m_w_in, m_swa_sinks, m_fox_forget_bias, m_conv_w,
                         m_conv_b, m_lru_w_r, m_lru_b_r, m_lru_w_i, m_lru_b_i, m_lru_lambda, m_w_branch, m_w_out, m_norm_ffn,
                         m_w_ffn_in, m_w_ffn_out, m_norm_final)))
    V = dict(zip(NAMES, (v_meta_tokens, v_rel_bias_table, v_norm_mix, v_w_in, v_swa_sinks, v_fox_forget_bias, v_conv_w,
                         v_conv_b, v_lru_w_r, v_lru_b_r, v_lru_w_i, v_lru_b_i, v_lru_lambda, v_w_branch, v_w_out, v_norm_ffn,
                         v_w_ffn_in, v_w_ffn_out, v_norm_final)))
    W, placed = _gather_weights(P)
    loss_local, grad_x, grads = _local_step(x[0], loss_target[0], W, placed)
    loss = lax.psum(loss_local, ("x", "y", "c"))
    Gd = _reduce_grads(grads, P)
    delta, new_m, new_v = _update(P, Gd, M, V)
    return (loss, grad_x[None], *[Gd[n] for n in NAMES], *[delta[n] for n in NAMES],
            *[new_m[n] for n in NAMES], *[new_v[n] for n in NAMES])
```

```python
import functools
import math

import numpy as np
import jax
import jax.numpy as jnp
from jax import lax
from jax.experimental import pallas as pl
from jax.experimental.pallas import tpu as pltpu

F32, BF16 = jnp.float32, jnp.bfloat16
MESH = pl.DeviceIdType.MESH
ANY = pl.BlockSpec(memory_space=pl.ANY)
SMEM = pl.BlockSpec(memory_space=pltpu.SMEM)

D = 1024
DEPTH = 4
BLK = 128
N_META = 16
NPAD = 112
NH = 8
LW = 512
DFF = 2816
EPS = 1e-6
NEG = -1e30
SCALE = 0.125
LRU_C = 8.0
REL_BUCKETS = 32
N_SHARD = 4
QA, QF, KF, VF, XC, YC, GT, KA, VA, FL, INP = 0, 512, 1024, 1536, 2048, 2560, 3072, 6144, 6272, 6400, 6656
IN_COLS = 6408
VMEM_LIMIT = 48 * 1024 * 1024

ADAM_LR, ADAM_B1, ADAM_B2, ADAM_EPS, ADAM_WD, ADAM_STEP = 0.001, 0.9, 0.999, 1e-08, 0.01, 10


def _cp(*sem):
    return pltpu.CompilerParams(dimension_semantics=sem or None, vmem_limit_bytes=VMEM_LIMIT)


def _pick(n, prefs):
    for p in prefs:
        if n % p == 0:
            return p
    return n


def _rt(T):
    return _pick(T, (384, 128))


def _sigmoid(z):
    return 1.0 / (1.0 + jnp.exp(-z))


def _log_sigmoid(z):
    return jnp.minimum(z, 0.0) - jnp.log(1.0 + jnp.exp(-jnp.abs(z)))


def _gelu(y):
    c = math.sqrt(2.0 / math.pi)
    return 0.5 * y * (1.0 + jnp.tanh(c * (y + 0.044715 * y * y * y)))


def _gelu_grad(y):
    c = math.sqrt(2.0 / math.pi)
    t = jnp.tanh(c * (y + 0.044715 * y * y * y))
    return 0.5 * (1.0 + t) + 0.5 * y * (1.0 - t * t) * c * (1.0 + 3.0 * 0.044715 * y * y)


def _neg_expm1(z):
    series = -z * (1.0 + z * (0.5 + z * (1.0 / 6.0 + z * (1.0 / 24.0 + z * (1.0 / 120.0)))))
    return jnp.where(z > -0.1, series, 1.0 - jnp.exp(z))


def _dot(a, b, ca, cb):
    return lax.dot_general(a, b, (((ca,), (cb,)), ((), ())), preferred_element_type=F32)


def _mm(a, b, *, ta=False, tb=False, res=None, out_dtype=F32, tm, tn, tk, name, slab=None, b_k0=0, col0=0, cols=None):
    M, K = (a.shape[1], a.shape[0]) if ta else a.shape
    N = b.shape[0] if tb else b.shape[1]
    assert (b.shape[1] if tb else b.shape[0]) >= K + b_k0 and M % tm == 0 and N % tn == 0 and K % tk == 0, (name, a.shape, b.shape)
    assert b_k0 % tk == 0 and col0 % tn == 0
    nk, kb, jb = K // tk, b_k0 // tk, col0 // tn
    ca, cb = (0 if ta else 1), (1 if tb else 0)
    n_in = 2 + (res is not None) + (slab is not None and slab[0] is not None)

    def body(*refs):
        a_ref, b_ref = refs[:2]
        r_ref = refs[2] if res is not None else None
        o_ref = refs[n_in]
        part = _dot(a_ref[...].astype(BF16), b_ref[...].astype(BF16), ca, cb)

        def fin(acc):
            if res is not None:
                acc = acc + r_ref[...]
            o_ref[...] = acc.astype(out_dtype)

        if nk == 1:
            fin(part)
        else:
            acc_ref = refs[-1]
            k = pl.program_id(2)

            @pl.when(k == 0)
            def _():
                acc_ref[...] = part

            @pl.when(k > 0)
            def _():
                acc_ref[...] += part

            @pl.when(k == nk - 1)
            def _():
                fin(acc_ref[...])

    a_spec = pl.BlockSpec((tk, tm), lambda i, j, k: (k, i)) if ta else pl.BlockSpec((tm, tk), lambda i, j, k: (i, k))
    b_spec = (pl.BlockSpec((tn, tk), lambda i, j, k: (j, k + kb)) if tb
              else pl.BlockSpec((tk, tn), lambda i, j, k: (k + kb, j)))
    o_spec = pl.BlockSpec((tm, tn), lambda i, j, k: (i, j))
    in_specs, ops = [a_spec, b_spec], [a, b]
    if res is not None:
        in_specs.append(o_spec)
        ops.append(res)
    out_shape, aliases = jax.ShapeDtypeStruct((M, N), out_dtype), {}
    if slab is not None:
        buf, idx, n = slab
        o_spec = pl.BlockSpec((None, tm, tn), lambda i, j, k: (idx, i, j + jb))
        out_shape = jax.ShapeDtypeStruct((n, M, cols or N), out_dtype)
        if buf is not None:
            aliases = {len(ops): 0}
            in_specs.append(ANY)
            ops.append(buf)
    return pl.pallas_call(
        body, grid=(M // tm, N // tn, nk), in_specs=in_specs, out_specs=o_spec, out_shape=out_shape,
        input_output_aliases=aliases, scratch_shapes=[pltpu.VMEM((tm, tn), F32)] if nk > 1 else [],
        compiler_params=_cp("parallel", "parallel", "arbitrary"), name=name)(*ops)


def _rms_fwd(h, g, name):
    T = h.shape[0]
    tr = _rt(T)

    def body(h_ref, g_ref, u_ref, ut_ref):
        x = h_ref[...]
        r = lax.rsqrt(jnp.mean(x * x, axis=-1, keepdims=True) + EPS)
        u = (x * r * g_ref[...]).astype(BF16)
        u_ref[...] = u
        ut_ref[...] = u.T

    return pl.pallas_call(
        body, grid=(T // tr,),
        in_specs=[pl.BlockSpec((tr, D), lambda i: (i, 0)), pl.BlockSpec((1, D), lambda i: (0, 0))],
        out_specs=[pl.BlockSpec((tr, D), lambda i: (i, 0)), pl.BlockSpec((D, tr), lambda i: (0, i))],
        out_shape=[jax.ShapeDtypeStruct((T, D), BF16), jax.ShapeDtypeStruct((D, T), BF16)],
        compiler_params=_cp("parallel"), name=name)(h, g.reshape(1, D))


def _rms_bwd(du, h, g, dres, name):
    T = h.shape[0]
    tr = _rt(T)

    def body(du_ref, h_ref, g_ref, dres_ref, dh_ref, dhb_ref, dg_ref):
        x = h_ref[...]
        r = lax.rsqrt(jnp.mean(x * x, axis=-1, keepdims=True) + EPS)
        xh = x * r
        dy = du_ref[...]
        dxh = dy * g_ref[...]
        dx = r * (dxh - xh * jnp.mean(dxh * xh, axis=-1, keepdims=True))
        dh = dres_ref[...] + dx
        dh_ref[...] = dh
        dhb_ref[...] = dh.astype(BF16)
        part = jnp.sum(dy * xh, axis=0, keepdims=True)

        @pl.when(pl.program_id(0) == 0)
        def _():
            dg_ref[...] = part

        @pl.when(pl.program_id(0) > 0)
        def _():
            dg_ref[...] += part

    row = pl.BlockSpec((tr, D), lambda i: (i, 0))
    vec = pl.BlockSpec((1, D), lambda i: (0, 0))
    return pl.pallas_call(
        body, grid=(T // tr,), in_specs=[row, row, vec, row], out_specs=[row, row, vec],
        out_shape=[jax.ShapeDtypeStruct((T, D), F32), jax.ShapeDtypeStruct((T, D), BF16), jax.ShapeDtypeStruct((1, D), F32)],
        compiler_params=_cp("arbitrary"), name=name)(du, h, g.reshape(1, D), dres)


def _loss_head(h, tgt, g, name):
    T = h.shape[0]
    nb = T // BLK

    def body(h_ref, t_ref, g_ref, dh_ref, dhb_ref, dg_ref, loss_ref):
        i = pl.program_id(0)
        x = h_ref[...]
        r = lax.rsqrt(jnp.mean(x * x, axis=-1, keepdims=True) + EPS)
        xh = x * r
        gv = g_ref[...]
        tok = i >= 1
        err = jnp.where(tok, xh * gv - t_ref[...], 0.0)
        dy = err * (1.0 / D)
        dxh = dy * gv
        dx = r * (dxh - xh * jnp.mean(dxh * xh, axis=-1, keepdims=True))
        dh_ref[...] = dx
        dhb_ref[...] = dx.astype(BF16)
        dg = jnp.sum(dy * xh, axis=0, keepdims=True)
        ls = jnp.zeros((1, BLK), F32) + jnp.sum(err * err) * (0.5 / D)

        @pl.when(i == 0)
        def _():
            dg_ref[...] = dg
            loss_ref[...] = ls

        @pl.when(i > 0)
        def _():
            dg_ref[...] += dg
            loss_ref[...] += ls

    row = pl.BlockSpec((BLK, D), lambda i: (i, 0))
    vec = pl.BlockSpec((1, D), lambda i: (0, 0))
    return pl.pallas_call(
        body, grid=(nb,),
        in_specs=[row, pl.BlockSpec((BLK, D), lambda i: (jnp.maximum(i - 1, 0), 0)), vec],
        out_specs=[row, row, vec, pl.BlockSpec((1, BLK), lambda i: (0, 0))],
        out_shape=[jax.ShapeDtypeStruct((T, D), F32), jax.ShapeDtypeStruct((T, D), BF16),
                   jax.ShapeDtypeStruct((1, D), F32), jax.ShapeDtypeStruct((1, BLK), F32)],
        compiler_params=_cp("arbitrary"), name=name)(h, tgt, g.reshape(1, D))


def _bucket_table():
    q = np.arange(BLK)[:, None]
    k = np.arange(2 * BLK)[None, :]
    d = np.maximum(q + BLK - k, 0)
    max_exact = REL_BUCKETS // 2
    scaled = np.log(np.maximum(d, 1).astype(np.float32) / np.float32(max_exact)) / np.float32(math.log(128 / max_exact))
    large = np.minimum(max_exact + (scaled.astype(np.float32) * (REL_BUCKETS - max_exact)).astype(np.int32), REL_BUCKETS - 1)
    return np.where(d < max_exact, d, large).astype(np.int32)


def _bias_build(table, bucket, name):
    def body(t_ref, bk_ref, o_ref):
        bk = bk_ref[...]
        for h in range(NH):
            acc = jnp.zeros((BLK, 2 * BLK), F32)
            for b in range(REL_BUCKETS):
                acc = jnp.where(bk == b, t_ref[b, h], acc)
            o_ref[h] = acc

    return pl.pallas_call(
        body, in_specs=[SMEM, pl.BlockSpec(memory_space=pltpu.VMEM)], out_specs=pl.BlockSpec(memory_space=pltpu.VMEM),
        out_shape=jax.ShapeDtypeStruct((NH, BLK, 2 * BLK), F32), compiler_params=_cp(), name=name)(table, bucket)


def _bias_bwd(dbias, bucket, name):
    def body(d_ref, bk_ref, o_ref):
        bk = bk_ref[...]
        for h in range(NH):
            dh = d_ref[h]
            for b in range(REL_BUCKETS):
                o_ref[b, h] = jnp.sum(jnp.where(bk == b, dh, 0.0))

    return pl.pallas_call(
        body, in_specs=[pl.BlockSpec(memory_space=pltpu.VMEM)] * 2, out_specs=SMEM,
        out_shape=jax.ShapeDtypeStruct((REL_BUCKETS, NH), F32), compiler_params=_cp(), name=name)(dbias, bucket)


def _swa_specs(nq_cols):
    prev = lambda n: jnp.maximum(n - 1, 0)
    return [
        pl.BlockSpec((BLK, nq_cols), lambda n: (n, QA // nq_cols)),
        pl.BlockSpec((BLK, BLK), lambda n: (prev(n), KA // BLK)), pl.BlockSpec((BLK, BLK), lambda n: (n, KA // BLK)),
        pl.BlockSpec((BLK, BLK), lambda n: (prev(n), VA // BLK)), pl.BlockSpec((BLK, BLK), lambda n: (n, VA // BLK)),
    ]


def _swa_mask(n):
    row = lax.broadcasted_iota(jnp.int32, (BLK, 2 * BLK), 0)
    col = lax.broadcasted_iota(jnp.int32, (BLK, 2 * BLK), 1)
    dist = row + BLK - col
    return (dist >= 0) & (dist < BLK) & ((n - 1) * BLK + col >= NPAD)


def _swa_probs(qm, ksel, mask, bias_h, sink):
    s = _dot(qm, ksel, 1, 1) * SCALE
    s = jnp.where(mask, s + bias_h, NEG)
    m = jnp.maximum(jnp.max(s, axis=-1, keepdims=True), sink)
    p = jnp.exp(s - m)
    psink = jnp.exp(sink - m)
    inv = 1.0 / (jnp.sum(p, axis=-1, keepdims=True) + psink)
    return p * inv, psink * inv


def _swa_fwd(proj, bias, sinks, name):
    T = proj.shape[0]
    nb = T // BLK

    def body(sk_ref, q_ref, kp_ref, kc_ref, vp_ref, vc_ref, b_ref, o_ref, ot_ref):
        n = pl.program_id(0)
        lo = lax.broadcasted_iota(jnp.int32, (1, BLK), 1) < 64
        kb = jnp.concatenate([kp_ref[...], kc_ref[...]], axis=0)
        vb = jnp.concatenate([vp_ref[...], vc_ref[...]], axis=0)
        kbs = (kb.astype(BF16), pltpu.roll(kb, 64, 1).astype(BF16))
        vbs = (vb, pltpu.roll(vb, 64, 1))
        mask = _swa_mask(n)
        outs = []
        for pr in range(NH // 2):
            qp = q_ref[:, pr * BLK:(pr + 1) * BLK]
            kv = pr // 2
            acc = jnp.zeros((BLK, BLK), F32)
            for e in range(2):
                lm = lo if e == 0 else jnp.logical_not(lo)
                sw = 0 if kv == e else 1
                qm = jnp.where(lm, qp, 0.0).astype(BF16)
                pn, _ = _swa_probs(qm, kbs[sw], mask, b_ref[2 * pr + e], sk_ref[2 * pr + e])
                acc = acc + _dot(pn.astype(BF16), jnp.where(lm, vbs[sw], 0.0).astype(BF16), 1, 0)
            outs.append(acc)
        o = jnp.concatenate(outs, axis=1).astype(BF16)
        o_ref[...] = o
        ot_ref[...] = o.T

    return pl.pallas_call(
        body, grid=(nb,),
        in_specs=[SMEM] + _swa_specs(512) + [pl.BlockSpec((NH, BLK, 2 * BLK), lambda n: (0, 0, 0))],
        out_specs=[pl.BlockSpec((BLK, 512), lambda n: (n, 0)), pl.BlockSpec((512, BLK), lambda n: (0, n))],
        out_shape=[jax.ShapeDtypeStruct((T, 512), BF16), jax.ShapeDtypeStruct((512, T), BF16)],
        compiler_params=_cp("parallel"), name=name)(sinks, proj, proj, proj, proj, proj, bias)


def _swa_bwd(proj, bias, sinks, do, dbias_in, name):
    T = proj.shape[0]
    nb = T // BLK

    def body(sk_ref, q_ref, kp_ref, kc_ref, vp_ref, vc_ref, b_ref, do_ref, dbi_ref,
             dq_ref, dk_ref, dv_ref, db_ref, dsk_ref, sk_acc):
        n = pl.program_id(0)
        lane = lax.broadcasted_iota(jnp.int32, (1, BLK), 1)
        lo = lane < 64
        kb = jnp.concatenate([kp_ref[...], kc_ref[...]], axis=0)
        vb = jnp.concatenate([vp_ref[...], vc_ref[...]], axis=0)
        kbs = (kb, pltpu.roll(kb, 64, 1))
        vbs = (vb, pltpu.roll(vb, 64, 1))
        mask = _swa_mask(n)

        @pl.when(n == 0)
        def _():
            db_ref[...] = dbi_ref[...]
            sk_acc[...] = jnp.zeros_like(sk_acc)

        dqs = []
        dk = jnp.zeros((2 * BLK, BLK), F32)
        dv = jnp.zeros((2 * BLK, BLK), F32)
        for pr in range(NH // 2):
            qp = q_ref[:, pr * BLK:(pr + 1) * BLK]
            dop = do_ref[:, pr * BLK:(pr + 1) * BLK].astype(F32)
            kv = pr // 2
            dq = jnp.zeros((BLK, BLK), F32)
            for e in range(2):
                h = 2 * pr + e
                lm = lo if e == 0 else jnp.logical_not(lo)
                sw = 0 if kv == e else 1
                qm = jnp.where(lm, qp, 0.0)
                dom = jnp.where(lm, dop, 0.0)
                pn, ps = _swa_probs(qm.astype(BF16), kbs[sw].astype(BF16), mask, b_ref[h], sk_ref[h])
                dp = _dot(dom.astype(BF16), vbs[sw].astype(BF16), 1, 1)
                delta = jnp.sum(pn * dp, axis=-1, keepdims=True)
                ds = pn * (dp - delta)
                db_ref[h] += ds
                sk_acc[...] += jnp.where(lane == h, -(ps * delta), 0.0)
                dsb = (ds * SCALE).astype(BF16)
                dq = dq + _dot(dsb, jnp.where(lm, kbs[sw], 0.0).astype(BF16), 1, 0)
                qk = qm if sw == 0 else pltpu.roll(qm, 64, 1)
                dok = dom if sw == 0 else pltpu.roll(dom, 64, 1)
                dk = dk + _dot(dsb, qk.astype(BF16), 0, 0)
                dv = dv + _dot(pn.astype(BF16), dok.astype(BF16), 0, 0)
            dqs.append(dq)
        dq_ref[...] = jnp.concatenate(dqs, axis=1).astype(BF16)
        dk_ref[0] = dk
        dv_ref[0] = dv

        @pl.when(n == nb - 1)
        def _():
            dsk_ref[...] = jnp.sum(sk_acc[...], axis=0, keepdims=True)

    full_b = pl.BlockSpec((NH, BLK, 2 * BLK), lambda n: (0, 0, 0))
    band = pl.BlockSpec((1, 2 * BLK, BLK), lambda n: (n, 0, 0))
    return pl.pallas_call(
        body, grid=(nb,),
        in_specs=[SMEM] + _swa_specs(512) + [full_b, pl.BlockSpec((BLK, 512), lambda n: (n, 0)), full_b],
        out_specs=[pl.BlockSpec((BLK, 512), lambda n: (n, 0)), band, band, full_b, pl.BlockSpec((1, BLK), lambda n: (0, 0))],
        out_shape=[jax.ShapeDtypeStruct((T, 512), BF16), jax.ShapeDtypeStruct((nb, 2 * BLK, BLK), F32),
                   jax.ShapeDtypeStruct((nb, 2 * BLK, BLK), F32), jax.ShapeDtypeStruct((NH, BLK, 2 * BLK), F32),
                   jax.ShapeDtypeStruct((1, BLK), F32)],
        scratch_shapes=[pltpu.VMEM((BLK, BLK), F32)],
        compiler_params=_cp("arbitrary"), name=name)(sinks, proj, proj, proj, proj, proj, bias, do, dbias_in)


def _band_fold(dkb, dvb, name):
    nb = dkb.shape[0]

    def body(ko_ref, kn_ref, vo_ref, vn_ref, dk_ref, dv_ref):
        last = pl.program_id(0) == nb - 1
        dk_ref[...] = (ko_ref[0] + jnp.where(last, 0.0, kn_ref[0])).astype(BF16)
        dv_ref[...] = (vo_ref[0] + jnp.where(last, 0.0, vn_ref[0])).astype(BF16)

    own = pl.BlockSpec((1, BLK, BLK), lambda j: (j, 1, 0))
    nxt = pl.BlockSpec((1, BLK, BLK), lambda j: (jnp.minimum(j + 1, nb - 1), 0, 0))
    out = pl.BlockSpec((BLK, BLK), lambda j: (j, 0))
    return pl.pallas_call(
        body, grid=(nb,), in_specs=[own, nxt, own, nxt], out_specs=[out, out],
        out_shape=[jax.ShapeDtypeStruct((nb * BLK, BLK), BF16)] * 2,
        compiler_params=_cp("parallel"), name=name)(dkb, dkb, dvb, dvb)


def _token_major(x, width):
    full = jnp.concatenate([x, jnp.zeros((BLK - NH, BLK), F32)], axis=0).T
    return full if width == BLK else jnp.concatenate([full, jnp.zeros((BLK, width - BLK), F32)], axis=1)


def _cum_fwd(proj, fb, name):
    T = proj.shape[0]
    tr = _rt(T)

    def body(z_ref, fb_ref, c_ref, carry):
        g = pl.program_id(0)
        lane = lax.broadcasted_iota(jnp.int32, (NH, BLK), 1)

        @pl.when(g == 0)
        def _():
            carry[...] = jnp.zeros_like(carry)

        run = carry[...]
        for sb in range(tr // BLK):
            r = slice(sb * BLK, (sb + 1) * BLK)
            z = z_ref[r, :].T[0:NH, :] + fb_ref[...]
            x = jnp.where(g * tr + sb * BLK + lane >= NPAD, _log_sigmoid(z), 0.0)
            s = 1
            while s < BLK:
                x = x + jnp.where(lane >= s, pltpu.roll(x, s, 1), 0.0)
                s *= 2
            x = x + run
            run = jnp.zeros((NH, BLK), F32) + jnp.sum(jnp.where(lane == BLK - 1, x, 0.0), axis=-1, keepdims=True)
            c_ref[r, :] = _token_major(x, BLK)
        carry[...] = run

    return pl.pallas_call(
        body, grid=(T // tr,),
        in_specs=[pl.BlockSpec((tr, BLK), lambda g: (g, FL // BLK)), pl.BlockSpec((NH, 1), lambda g: (0, 0))],
        out_specs=pl.BlockSpec((tr, BLK), lambda g: (g, 0)), out_shape=jax.ShapeDtypeStruct((T, BLK), F32),
        scratch_shapes=[pltpu.VMEM((NH, BLK), F32)], compiler_params=_cp("arbitrary"), name=name)(proj, fb)


def _cum_bwd(dqx, dkx, proj, fb, name):
    T = proj.shape[0]
    tr = _rt(T)
    nb = T // tr

    def body(dq_ref, dk_ref, z_ref, fb_ref, dz_ref, db_ref, carry):
        k = pl.program_id(0)
        g = nb - 1 - k
        lane = lax.broadcasted_iota(jnp.int32, (NH, BLK), 1)

        @pl.when(k == 0)
        def _():
            carry[...] = jnp.zeros_like(carry)
            db_ref[...] = jnp.zeros_like(db_ref)

        def picked(ref, r, r_first, r_second):
            rows = []
            for p in range(NH // 2):
                t_ = ref[r, p * BLK:(p + 1) * BLK].T
                rows += [t_[r_first:r_first + 1, :], t_[r_second:r_second + 1, :]]
            return jnp.concatenate(rows, axis=0)

        run, tot = carry[...], jnp.zeros((NH, 1), F32)
        for sb in reversed(range(tr // BLK)):
            r = slice(sb * BLK, (sb + 1) * BLK)
            x = picked(dq_ref, r, 64, 0) - picked(dk_ref, r, 67, 3)
            s = 1
            while s < BLK:
                x = x + jnp.where(lane < BLK - s, pltpu.roll(x, BLK - s, 1), 0.0)
                s *= 2
            x = x + run
            run = jnp.zeros((NH, BLK), F32) + jnp.sum(jnp.where(lane == 0, x, 0.0), axis=-1, keepdims=True)
            z = z_ref[r, :].T[0:NH, :] + fb_ref[...]
            dz = jnp.where(g * tr + sb * BLK + lane >= NPAD, x * _sigmoid(-z), 0.0)
            tot = tot + jnp.sum(dz, axis=-1, keepdims=True)
            dz_ref[r, :] = _token_major(dz, 2 * BLK).astype(BF16)
        carry[...] = run
        db_ref[...] += tot

    rev = lambda k: nb - 1 - k
    wide = pl.BlockSpec((tr, 512), lambda k: (rev(k), 0))
    return pl.pallas_call(
        body, grid=(nb,),
        in_specs=[wide, wide, pl.BlockSpec((tr, BLK), lambda k: (rev(k), FL // BLK)), pl.BlockSpec((NH, 1), lambda k: (0, 0))],
        out_specs=[pl.BlockSpec((tr, 2 * BLK), lambda k: (rev(k), 0)), pl.BlockSpec((NH, BLK), lambda k: (0, 0))],
        out_shape=[jax.ShapeDtypeStruct((T, 2 * BLK), BF16), jax.ShapeDtypeStruct((NH, BLK), F32)],
        scratch_shapes=[pltpu.VMEM((NH, BLK), F32)], compiler_params=_cp("arbitrary"), name=name)(dqx, dkx, proj, fb)


def _fox_prep(proj, ccol, name):
    T = proj.shape[0]
    tr = _pick(T, (1408, 384, 128))

    def body(q_ref, k_ref, v_ref, cc_ref, qa_ref, ka_ref, kt_ref, vm_ref, vo_ref):
        h = pl.program_id(1)
        lane = lax.broadcasted_iota(jnp.int32, (1, BLK), 1)
        own = (lane >> 6) == (h & 1)
        a0 = 64 * (1 - (h & 1))
        c = _lane_pick(cc_ref[...], lane, h)
        hi = c.astype(BF16).astype(F32)
        mid = (c - hi).astype(BF16).astype(F32)
        lo = (c - hi - mid).astype(BF16).astype(F32)
        ones = (lane >= a0 + 3) & (lane < a0 + 6)
        qa = jnp.where(own, q_ref[...] * SCALE, jnp.where(ones, 1.0, 0.0))
        qa = jnp.where(lane == a0, hi, jnp.where(lane == a0 + 1, mid, jnp.where(lane == a0 + 2, lo, qa)))
        ones = (lane >= a0) & (lane < a0 + 3)
        ka = jnp.where(own, k_ref[...], jnp.where(ones, 1.0, 0.0))
        ka = jnp.where(lane == a0 + 3, -hi, jnp.where(lane == a0 + 4, -mid, jnp.where(lane == a0 + 5, -lo, ka)))
        qa_ref[...] = qa.astype(BF16)
        kab = ka.astype(BF16)
        ka_ref[...] = kab
        kt_ref[...] = kab.T
        vm = jnp.where(own, v_ref[...], 0.0)
        vm_ref[...] = vm.astype(BF16)
        vo_ref[...] = jnp.where(lane == a0, 1.0, vm).astype(BF16)

    pair = lambda col0: pl.BlockSpec((tr, BLK), lambda i, h: (i, col0 // BLK + (h >> 1)))
    out = pl.BlockSpec((None, tr, BLK), lambda i, h: (h, i, 0))
    out_t = pl.BlockSpec((None, BLK, tr), lambda i, h: (h, 0, i))
    tok = jax.ShapeDtypeStruct((NH, T, BLK), BF16)
    return pl.pallas_call(
        body, grid=(T // tr, NH), in_specs=[pair(QF), pair(KF), pair(VF), pl.BlockSpec((tr, BLK), lambda i, h: (i, 0))],
        out_specs=[out, out, out_t, out, out], out_shape=[tok, tok, jax.ShapeDtypeStruct((NH, BLK, T), BF16), tok, tok],
        compiler_params=_cp("parallel", "arbitrary"), name=name)(proj, proj, proj, ccol)


def _fox_fwd(qaug, kaug_t, vo, name, gather=None):
    T = qaug.shape[1]
    t = _rt(T)
    nt = T // t
    ng = len(gather[0]) if gather else 0

    pairs = [(i, j) for i in range(nt) for j in range(i + 1)]
    i_of = jnp.asarray(np.array([p[0] for p in pairs], np.int32))
    j_of = jnp.asarray(np.array([p[1] for p in pairs], np.int32))
    ns = len(pairs)

    def body(i_ref, j_ref, q0, q1, k0, k1, v0, v1, *rest):
        o_ref, ot_ref, lse0_ref, lse1_ref = rest[ng:ng + 4]
        m_ref, acc_ref = rest[2 * ng + 4:2 * ng + 6]
        p_, s_ = pl.program_id(0), pl.program_id(1)
        i, j = i_ref[s_], j_ref[s_]
        lane = lax.broadcasted_iota(jnp.int32, (1, BLK), 1)
        lo = lane < 64
        if gather:
            start, finish = _gather_plan(rest[ng + 4:2 * ng + 4], gather[1], *rest[2 * ng + 6:])
            pl.when((p_ == 0) & (s_ == 0))(start)

        @pl.when(j == 0)
        def _():
            m_ref[...] = jnp.full_like(m_ref, NEG)
            acc_ref[...] = jnp.zeros_like(acc_ref)

        def step(masked):
            for e, (q_ref, k_ref, v_ref) in enumerate(((q0, k0, v0), (q1, k1, v1))):
                s = _dot(q_ref[...], k_ref[...], 1, 0)
                if masked:
                    s = jnp.where(_fox_mask(i, j, t), s, NEG)
                m_old = m_ref[e]
                m_new = jnp.maximum(m_old, jnp.max(s, axis=-1, keepdims=True))
                m_ref[e] = m_new
                pe = jnp.exp(s - jnp.concatenate([m_new] * (t // BLK), axis=1))
                acc_ref[e] = jnp.exp(m_old - m_new) * acc_ref[e] + _dot(pe.astype(BF16), v_ref[...], 1, 0)

        pl.when((j < i) & (j > 0))(lambda: step(False))
        pl.when((j == i) | ((j == 0) & (i > 0)))(lambda: step(True))

        @pl.when(j == i)
        def _():
            rows = i * t + lax.broadcasted_iota(jnp.int32, (t, 1), 0)
            l0, l1 = _lane_pick(acc_ref[0], lane, 64), _lane_pick(acc_ref[1], lane, 0)
            o = jnp.where(rows >= NPAD, jnp.where(lo, acc_ref[0] / l0, acc_ref[1] / l1), 0.0).astype(BF16)
            o_ref[...] = o
            ot_ref[...] = o.T
            lse0_ref[...] = m_ref[0] + jnp.log(l0)
            lse1_ref[...] = m_ref[1] + jnp.log(l1)

        if gather:
            pl.when((p_ == NH // 2 - 1) & (s_ == ns - 1))(finish)

    qs = lambda e: pl.BlockSpec((None, t, BLK), lambda p, s, ii, jj: (2 * p + e, ii[s], 0))
    ks = lambda e: pl.BlockSpec((None, t, BLK), lambda p, s, ii, jj: (2 * p + e, jj[s], 0))
    kts = lambda e: pl.BlockSpec((None, BLK, t), lambda p, s, ii, jj: (2 * p + e, 0, jj[s]))
    rep = pl.BlockSpec((None, t, BLK), lambda p, s, ii, jj: (p, ii[s], 0))
    bufs = list(gather[0]) if gather else []
    return pl.pallas_call(
        body,
        grid_spec=pltpu.PrefetchScalarGridSpec(
            num_scalar_prefetch=2, grid=(NH // 2, ns),
            in_specs=[qs(0), qs(1), kts(0), kts(1), ks(0), ks(1)] + [ANY] * ng,
            out_specs=[pl.BlockSpec((t, BLK), lambda p, s, ii, jj: (ii[s], p)),
                       pl.BlockSpec((BLK, t), lambda p, s, ii, jj: (p, ii[s])), rep, rep] + [ANY] * ng,
            scratch_shapes=[pltpu.VMEM((2, t, BLK), F32), pltpu.VMEM((2, t, BLK), F32)]
            + ([pltpu.SemaphoreType.DMA((6 * ng,)), pltpu.SemaphoreType.DMA((6 * ng,))] if gather else [])),
        out_shape=[jax.ShapeDtypeStruct((T, 512), BF16), jax.ShapeDtypeStruct((512, T), BF16)]
        + [jax.ShapeDtypeStruct((NH // 2, T, BLK), F32)] * 2 + [jax.ShapeDtypeStruct(b.shape, b.dtype) for b in bufs],
        input_output_aliases={8 + g: 4 + g for g in range(ng)},
        compiler_params=(pltpu.CompilerParams(dimension_semantics=("arbitrary",) * 2, vmem_limit_bytes=VMEM_LIMIT,
                                              has_side_effects=True) if gather
                         else _cp("parallel", "arbitrary")), name=name)(i_of, j_of, qaug, qaug, kaug_t, kaug_t, vo, vo, *bufs)


def _fox_delta(do, o, name):
    T = do.shape[0]
    tr = _rt(T)

    def body(do_ref, o_ref, d0_ref, d1_ref):
        lo = lax.broadcasted_iota(jnp.int32, (1, BLK), 1) < 64
        prod = do_ref[...].astype(F32) * o_ref[...].astype(F32)
        d0_ref[...] = jnp.zeros((tr, BLK), F32) + jnp.sum(jnp.where(lo, prod, 0.0), axis=-1, keepdims=True)
        d1_ref[...] = jnp.zeros((tr, BLK), F32) + jnp.sum(jnp.where(lo, 0.0, prod), axis=-1, keepdims=True)

    blk = pl.BlockSpec((tr, BLK), lambda i, p: (i, p))
    rep = pl.BlockSpec((None, tr, BLK), lambda i, p: (p, i, 0))
    return pl.pallas_call(
        body, grid=(T // tr, NH // 2), in_specs=[blk, blk], out_specs=[rep, rep],
        out_shape=[jax.ShapeDtypeStruct((NH // 2, T, BLK), F32)] * 2,
        compiler_params=_cp("parallel", "parallel"), name=name)(do, o)


def _fox_bwd(qaug, kaug, kaug_t, vm, do, lses, deltas, name, scatter=None):
    T = qaug.shape[1]
    t = _rt(T)
    nt = T // t
    ng = len(scatter[0]) if scatter else 0
    pairs = [(i, j) for j in range(nt) for i in range(j, nt)]
    i_of = jnp.asarray(np.array([p[0] for p in pairs], np.int32))
    j_of = jnp.asarray(np.array([p[1] for p in pairs], np.int32))
    ns = len(pairs)

    def body(i_ref, j_ref, q0, q1, k0, k1, kt0, kt1, v0, v1, do_ref, lse0, lse1, dl0, dl1, *rest):
        dq_ref, dqx_ref, dk_ref, dv_ref, dkx_ref = rest[ng:ng + 5]
        dq_acc, dk_acc, dv_acc = rest[2 * ng + 5:2 * ng + 8]
        p_, s_ = pl.program_id(0), pl.program_id(1)
        i, j = i_ref[s_], j_ref[s_]
        lane = lax.broadcasted_iota(jnp.int32, (1, BLK), 1)
        lo = lane < 64
        if scatter:
            start, finish = _scatter_plan(rest[:ng], rest[ng + 5:2 * ng + 5], scatter[1], *rest[2 * ng + 8:])
            pl.when((p_ == 0) & (s_ == 0))(start)

        @pl.when(s_ == 0)
        def _():
            dq_acc[...] = jnp.zeros_like(dq_acc)

        @pl.when(i == j)
        def _():
            dk_acc[...] = jnp.zeros_like(dk_acc)
            dv_acc[...] = jnp.zeros_like(dv_acc)

        def step(masked):
            dob = do_ref[...]
            rows = pl.ds(pl.multiple_of(i * t, t), t)
            wide = lambda ref: jnp.concatenate([ref[...]] * (t // BLK), axis=1)
            for e, (q_ref, k_ref, kt_ref, v_ref, lse_ref, dl_ref) in enumerate(
                    ((q0, k0, kt0, v0, lse0, dl0), (q1, k1, kt1, v1, lse1, dl1))):
                s = _dot(q_ref[...], kt_ref[...], 1, 0)
                if masked:
                    s = jnp.where(_fox_mask(i, j, t), s, NEG)
                pe = jnp.exp(s - wide(lse_ref))
                dp = _dot(dob, v_ref[...], 1, 1)
                ds = (pe * (dp - wide(dl_ref))).astype(BF16)
                dq_acc[e, rows, :] += _dot(ds, k_ref[...], 1, 0)
                dk_acc[e] += _dot(ds, q_ref[...], 0, 0)
                dv_acc[e] += _dot(pe.astype(BF16), dob, 0, 0)

        pl.when((i > j) & (j > 0))(lambda: step(False))
        pl.when((i == j) | ((j == 0) & (i > 0)))(lambda: step(True))

        @pl.when(i == nt - 1)
        def _():
            dk_ref[...] = jnp.where(lo, dk_acc[0], dk_acc[1]).astype(BF16)
            dv_ref[...] = jnp.where(lo, dv_acc[0], dv_acc[1]).astype(BF16)
            dkx_ref[...] = jnp.where(lo, dk_acc[1], dk_acc[0])

        @pl.when(s_ == ns - 1)
        def _():
            dq_ref[...] = (jnp.where(lo, dq_acc[0], dq_acc[1]) * SCALE).astype(BF16)
            dqx_ref[...] = jnp.where(lo, dq_acc[1], dq_acc[0])

        if scatter:
            pl.when((p_ == NH // 2 - 1) & (s_ == ns - 1))(finish)

    qs = lambda e: pl.BlockSpec((None, t, BLK), lambda p, s, ii, jj: (2 * p + e, ii[s], 0))
    ks = lambda e: pl.BlockSpec((None, t, BLK), lambda p, s, ii, jj: (2 * p + e, jj[s], 0))
    kts = lambda e: pl.BlockSpec((None, BLK, t), lambda p, s, ii, jj: (2 * p + e, 0, jj[s]))
    qside = pl.BlockSpec((t, BLK), lambda p, s, ii, jj: (ii[s], p))
    kside = pl.BlockSpec((t, BLK), lambda p, s, ii, jj: (jj[s], p))
    rep = pl.BlockSpec((None, t, BLK), lambda p, s, ii, jj: (p, ii[s], 0))
    whole = pl.BlockSpec((T, BLK), lambda p, s, ii, jj: (0, p))
    sums = list(scatter[0]) if scatter else []
    return pl.pallas_call(
        body,
        grid_spec=pltpu.PrefetchScalarGridSpec(
            num_scalar_prefetch=2, grid=(NH // 2, ns),
            in_specs=[qs(0), qs(1), ks(0), ks(1), kts(0), kts(1), ks(0), ks(1), qside, rep, rep, rep, rep] + [ANY] * ng,
            out_specs=[whole, whole, kside, kside, kside] + [ANY] * ng,
            scratch_shapes=[pltpu.VMEM((2, T, BLK), F32), pltpu.VMEM((2, t, BLK), F32), pltpu.VMEM((2, t, BLK), F32)]
            + ([pltpu.SemaphoreType.DMA((3 * ng,)), pltpu.SemaphoreType.DMA((3 * ng,))] if scatter else [])),
        out_shape=[jax.ShapeDtypeStruct((T, 512), BF16), jax.ShapeDtypeStruct((T, 512), F32),
                   jax.ShapeDtypeStruct((T, 512), BF16), jax.ShapeDtypeStruct((T, 512), BF16),
                   jax.ShapeDtypeStruct((T, 512), F32)] + (_scatter_shapes(sums, scatter[1]) if scatter else []),
        compiler_params=(pltpu.CompilerParams(dimension_semantics=("arbitrary",) * 2, vmem_limit_bytes=VMEM_LIMIT,
                                              has_side_effects=True) if scatter
                         else _cp("parallel", "arbitrary")), name=name)(
            i_of, j_of, qaug, qaug, kaug, kaug, kaug_t, kaug_t, vm, vm, do, *lses, *deltas, *sums)


def _fox_mask(i, j, t):
    row = i * t + lax.broadcasted_iota(jnp.int32, (t, t), 0)
    col = j * t + lax.broadcasted_iota(jnp.int32, (t, t), 1)
    return (col <= row) & (col >= NPAD)


def _lane_pick(x, lane, idx):
    return jnp.sum(jnp.where(lane == idx, x, 0.0), axis=-1, keepdims=True)


def _lru_gates(xc, wr_ref, wi_ref, vec_ref):
    xb = xc.astype(BF16)
    pre_r = jnp.concatenate([_dot(xb[:, p * BLK:(p + 1) * BLK], wr_ref[p], 1, 0) for p in range(LW // BLK)], axis=1)
    pre_i = jnp.concatenate([_dot(xb[:, p * BLK:(p + 1) * BLK], wi_ref[p], 1, 0) for p in range(LW // BLK)], axis=1)
    r = _sigmoid(pre_r + vec_ref[0:1, :])
    gi = _sigmoid(pre_i + vec_ref[1:2, :])
    log_a = LRU_C * r * _log_sigmoid(vec_ref[2:3, :])
    a = jnp.exp(log_a)
    mult = jnp.sqrt(_neg_expm1(2.0 * log_a))
    return r, gi, a, mult


def _conv(xbuf_ref, x, cw_ref, vec_ref, tr):
    return (cw_ref[3:4, :] * x + cw_ref[2:3, :] * xbuf_ref[7:7 + tr, :] + cw_ref[1:2, :] * xbuf_ref[6:6 + tr, :]
            + cw_ref[0:1, :] * xbuf_ref[5:5 + tr, :] + vec_ref[3:4, :])


def _lru_fwd(proj, cw, wr, wi, vec, name):
    T = proj.shape[0]
    tr = _rt(T)

    def body(x_ref, y_ref, cw_ref, wr_ref, wi_ref, vec_ref, oc_ref, oct_ref, hs_ref, xbuf, abuf, bbuf, hcar):
        i = pl.program_id(0)

        @pl.when(i == 0)
        def _():
            xbuf[0:8, :] = jnp.zeros((8, LW), F32)
            hcar[...] = jnp.zeros_like(hcar)

        x = x_ref[...]
        xbuf[8:8 + tr, :] = x
        xc = _conv(xbuf, x, cw_ref, vec_ref, tr)
        xbuf[0:8, :] = x[tr - 8:tr, :]
        _, gi, a, mult = _lru_gates(xc, wr_ref, wi_ref, vec_ref)
        rows = i * tr + lax.broadcasted_iota(jnp.int32, (tr, 1), 0)
        abuf[...] = a
        bbuf[...] = jnp.where(rows >= NPAD, mult * (gi * xc), 0.0)
        sub = lax.broadcasted_iota(jnp.int32, (8, 1), 0)

        def step(k, h):
            sl = pl.ds(pl.multiple_of(k * 8, 8), 8)
            a8, b8 = abuf[sl, :], bbuf[sl, :]
            for s in (1, 2, 4):
                ok = sub >= s
                b8 = jnp.where(ok, a8 * pltpu.roll(b8, s, 0) + b8, b8)
                a8 = jnp.where(ok, a8 * pltpu.roll(a8, s, 0), a8)
            h8 = a8 * h + b8
            bbuf[sl, :] = h8
            return h8[7:8, :]

        hcar[...] = lax.fori_loop(0, tr // 8, step, hcar[...])
        hs = bbuf[...]
        hs_ref[...] = hs
        oc = (hs * _gelu(y_ref[...])).astype(BF16)
        oc_ref[...] = oc
        oct_ref[...] = oc.T

    row = pl.BlockSpec((tr, LW), lambda i: (i, 0))
    full = lambda shape: pl.BlockSpec(shape, lambda i: (0,) * len(shape))
    return pl.pallas_call(
        body, grid=(T // tr,),
        in_specs=[pl.BlockSpec((tr, LW), lambda i: (i, XC // LW)), pl.BlockSpec((tr, LW), lambda i: (i, YC // LW)),
                  full((4, LW)), full((4, BLK, BLK)), full((4, BLK, BLK)), full((8, LW))],
        out_specs=[row, pl.BlockSpec((LW, tr), lambda i: (0, i)), row],
        out_shape=[jax.ShapeDtypeStruct((T, LW), BF16), jax.ShapeDtypeStruct((LW, T), BF16), jax.ShapeDtypeStruct((T, LW), F32)],
        scratch_shapes=[pltpu.VMEM((tr + 8, LW), F32), pltpu.VMEM((tr, LW), F32), pltpu.VMEM((tr, LW), F32),
                        pltpu.VMEM((1, LW), F32)],
        compiler_params=_cp("arbitrary"), name=name)(proj, proj, cw, wr, wi, vec)


def _lru_bwd(proj, hs, doc, cw, wr, wi, vec, name):
    T = proj.shape[0]
    tr = _rt(T)
    nt = T // tr
    r8 = tr // 8

    def body(x_ref, xp_ref, y_ref, hs_ref, hp_ref, do_ref, cw_ref, wr_ref, wi_ref, vec_ref,
             dx_ref, dy_ref, dwr_ref, dwi_ref, dvec_ref, xbuf, abuf, gbuf, hbuf, dbuf, gcar, acar):
        k = pl.program_id(0)
        i = nt - 1 - k

        @pl.when(k == 0)
        def _():
            dwr_ref[...] = jnp.zeros_like(dwr_ref)
            dwi_ref[...] = jnp.zeros_like(dwi_ref)
            dvec_ref[...] = jnp.zeros_like(dvec_ref)
            gcar[...] = jnp.zeros_like(gcar)
            acar[...] = jnp.zeros_like(acar)
            dbuf[tr:tr + 8, :] = jnp.zeros((8, LW), F32)

        first = i == 0
        x = x_ref[...]
        xbuf[0:8, :] = jnp.where(first, 0.0, xp_ref[...])
        xbuf[8:8 + tr, :] = x
        xc = _conv(xbuf, x, cw_ref, vec_ref, tr)
        r, gi, a, mult = _lru_gates(xc, wr_ref, wi_ref, vec_ref)
        y = y_ref[...]
        hs = hs_ref[...]
        do_ = do_ref[...].astype(F32)
        rows = i * tr + lax.broadcasted_iota(jnp.int32, (tr, 1), 0)
        abuf[0:tr, :] = a
        abuf[tr:tr + 8, :] = jnp.zeros((8, LW), F32) + acar[...]
        an = abuf[1:1 + tr, :]
        acar[...] = a[0:1, :]
        abuf[0:tr, :] = an
        gbuf[...] = do_ * _gelu(y)
        sub = lax.broadcasted_iota(jnp.int32, (8, 1), 0)

        def step(kk, g):
            sl = pl.ds(pl.multiple_of((r8 - 1 - kk) * 8, 8), 8)
            a8, b8 = abuf[sl, :], gbuf[sl, :]
            for s in (1, 2, 4):
                ok = sub < 8 - s
                b8 = jnp.where(ok, a8 * pltpu.roll(b8, 8 - s, 0) + b8, b8)
                a8 = jnp.where(ok, a8 * pltpu.roll(a8, 8 - s, 0), a8)
            g8 = a8 * g + b8
            gbuf[sl, :] = g8
            return g8[0:1, :]

        gcar[...] = lax.fori_loop(0, r8, step, gcar[...])
        g = gbuf[...]
        hbuf[0:8, :] = jnp.where(first, 0.0, hp_ref[...])
        hbuf[8:8 + tr, :] = hs
        hprev = hbuf[7:7 + tr, :]
        dinp = jnp.where(rows >= NPAD, g, 0.0)
        da = g * hprev
        dmult = dinp * gi * xc
        dgi = dinp * mult * xc
        dxc = dinp * mult * gi
        dlog_a = da * a - dmult * a * a / mult
        ls = _log_sigmoid(vec_ref[2:3, :])
        dpre_r = dlog_a * (LRU_C * ls) * r * (1.0 - r)
        dpre_i = dgi * gi * (1.0 - gi)
        xb = xc.astype(BF16)
        rb, ib = dpre_r.astype(BF16), dpre_i.astype(BF16)
        back = []
        for p in range(LW // BLK):
            c = slice(p * BLK, (p + 1) * BLK)
            back.append(_dot(rb[:, c], wr_ref[p], 1, 1) + _dot(ib[:, c], wi_ref[p], 1, 1))
            dwr_ref[p] += _dot(xb[:, c], rb[:, c], 0, 0)
            dwi_ref[p] += _dot(xb[:, c], ib[:, c], 0, 0)
        dxc = dxc + jnp.concatenate(back, axis=1)
        col = lambda v: jnp.sum(v, axis=0, keepdims=True)
        dvec_ref[0:1, :] += col(dpre_r)
        dvec_ref[1:2, :] += col(dpre_i)
        dvec_ref[2:3, :] += col(dlog_a * (LRU_C * r)) * _sigmoid(-vec_ref[2:3, :])
        dvec_ref[3:4, :] += col(dxc)
        dvec_ref[4:5, :] += col(dxc * xbuf[5:5 + tr, :])
        dvec_ref[5:6, :] += col(dxc * xbuf[6:6 + tr, :])
        dvec_ref[6:7, :] += col(dxc * xbuf[7:7 + tr, :])
        dvec_ref[7:8, :] += col(dxc * x)
        dbuf[0:tr, :] = dxc
        dxr = (cw_ref[3:4, :] * dxc + cw_ref[2:3, :] * dbuf[1:1 + tr, :] + cw_ref[1:2, :] * dbuf[2:2 + tr, :]
               + cw_ref[0:1, :] * dbuf[3:3 + tr, :])
        dbuf[tr:tr + 8, :] = dxc[0:8, :]
        dx_ref[...] = jnp.where(rows >= NPAD, dxr, 0.0).astype(BF16)
        dy_ref[...] = (do_ * hs * _gelu_grad(y)).astype(BF16)

    rev = lambda k: nt - 1 - k
    row = lambda col0: pl.BlockSpec((tr, LW), lambda k: (rev(k), col0))
    prev8 = lambda col0: pl.BlockSpec((8, LW), lambda k: (jnp.maximum(rev(k) * r8 - 1, 0), col0))
    full = lambda shape: pl.BlockSpec(shape, lambda k: (0,) * len(shape))
    return pl.pallas_call(
        body, grid=(nt,),
        in_specs=[row(XC // LW), prev8(XC // LW), row(YC // LW), row(0), prev8(0), row(0),
                  full((4, LW)), full((4, BLK, BLK)), full((4, BLK, BLK)), full((8, LW))],
        out_specs=[row(0), row(0), full((4, BLK, BLK)), full((4, BLK, BLK)), full((8, LW))],
        out_shape=[jax.ShapeDtypeStruct((T, LW), BF16), jax.ShapeDtypeStruct((T, LW), BF16),
                   jax.ShapeDtypeStruct((4, BLK, BLK), F32), jax.ShapeDtypeStruct((4, BLK, BLK), F32),
                   jax.ShapeDtypeStruct((8, LW), F32)],
        scratch_shapes=[pltpu.VMEM((tr + 8, LW), F32), pltpu.VMEM((tr + 8, LW), F32), pltpu.VMEM((tr, LW), F32),
                        pltpu.VMEM((tr + 8, LW), F32), pltpu.VMEM((tr + 8, LW), F32),
                        pltpu.VMEM((1, LW), F32), pltpu.VMEM((1, LW), F32)],
        compiler_params=_cp("arbitrary"), name=name)(proj, proj, proj, hs, hs, doc, cw, wr, wi, vec)


def _branch_merge_fwd(oa, of, oc, wb, proj, name):
    T = proj.shape[0]
    tm, tn = _rt(T), 512

    def body(a0, a1, a2, w_ref, g0, g1, g2, r0, r1, r2, m_ref, mt_ref):
        acc = None
        for g, (a_ref, g_ref, r_ref) in enumerate(((a0, g0, r0), (a1, g1, r1), (a2, g2, r2))):
            b = _dot(a_ref[...], w_ref[g], 1, 0)
            r_ref[...] = b
            term = _sigmoid(g_ref[...]) * b
            acc = term if acc is None else acc + term
        m = acc.astype(BF16)
        m_ref[...] = m
        mt_ref[...] = m.T

    act = pl.BlockSpec((tm, LW), lambda j, i: (i, 0))
    gate = lambda g: pl.BlockSpec((tm, tn), lambda j, i: (i, (GT + g * D) // tn + j))
    blk = pl.BlockSpec((tm, tn), lambda j, i: (i, j))
    return pl.pallas_call(
        body, grid=(D // tn, T // tm),
        in_specs=[act, act, act, pl.BlockSpec((3, LW, tn), lambda j, i: (0, 0, j)), gate(0), gate(1), gate(2)],
        out_specs=[blk] * 4 + [pl.BlockSpec((tn, tm), lambda j, i: (j, i))],
        out_shape=[jax.ShapeDtypeStruct((T, D), F32)] * 3 + [jax.ShapeDtypeStruct((T, D), BF16), jax.ShapeDtypeStruct((D, T), BF16)],
        compiler_params=_cp("parallel", "parallel"), name=name)(oa, of, oc, wb, proj, proj, proj)


def _out_dx_merge_bwd(dhb, w_out, proj, b0, b1, b2, name):
    T = proj.shape[0]
    tm, tn = _rt(T), 512

    def body(dh_ref, w_ref, g0, g1, g2, r0, r1, r2, d0, d1, d2, e0, e1, e2):
        dmv = _dot(dh_ref[...], w_ref[...], 1, 1)
        for g_ref, r_ref, d_ref, e_ref in ((g0, r0, d0, e0), (g1, r1, d1, e1), (g2, r2, d2, e2)):
            sg = _sigmoid(g_ref[...])
            d_ref[...] = (dmv * sg).astype(BF16)
            e_ref[...] = (dmv * r_ref[...] * sg * (1.0 - sg)).astype(BF16)

    gate = lambda g: pl.BlockSpec((tm, tn), lambda j, i: (i, (GT + g * D) // tn + j))
    blk = pl.BlockSpec((tm, tn), lambda j, i: (i, j))
    return pl.pallas_call(
        body, grid=(D // tn, T // tm),
        in_specs=[pl.BlockSpec((tm, D), lambda j, i: (i, 0)), pl.BlockSpec((tn, D), lambda j, i: (j, 0)),
                  gate(0), gate(1), gate(2), blk, blk, blk],
        out_specs=[blk] * 6, out_shape=[jax.ShapeDtypeStruct((T, D), BF16)] * 6,
        compiler_params=_cp("parallel", "parallel"), name=name)(dhb, w_out, proj, proj, proj, b0, b1, b2)


def _ffn_in_swiglu_fwd(u, w, name):
    T = u.shape[0]
    tm, tn = _rt(T), _pick(DFF, (1408, 256))
    nj = DFF // tn

    def body(u_ref, wg_ref, wu_ref, g_ref, up_ref, a_ref, at_ref):
        ub = u_ref[...]
        g = _dot(ub, wg_ref[...], 1, 0)
        up = _dot(ub, wu_ref[...], 1, 0)
        g_ref[...] = g
        up_ref[...] = up
        a = (g * _sigmoid(g) * up).astype(BF16)
        a_ref[...] = a
        at_ref[...] = a.T

    blk = pl.BlockSpec((tm, tn), lambda j, i: (i, j))
    return pl.pallas_call(
        body, grid=(nj, T // tm),
        in_specs=[pl.BlockSpec((tm, D), lambda j, i: (i, 0)), pl.BlockSpec((D, tn), lambda j, i: (0, j)),
                  pl.BlockSpec((D, tn), lambda j, i: (0, j + nj))],
        out_specs=[blk] * 3 + [pl.BlockSpec((tn, tm), lambda j, i: (j, i))],
        out_shape=[jax.ShapeDtypeStruct((T, DFF), F32)] * 2 + [jax.ShapeDtypeStruct((T, DFF), BF16),
                                                               jax.ShapeDtypeStruct((DFF, T), BF16)],
        compiler_params=_cp("parallel", "parallel"), name=name)(u, w, w)


def _ffn_out_dx_swiglu_bwd(dhb, w, gate, up, name):
    T = dhb.shape[0]
    tm, tn = _rt(T), _pick(DFF, (1408, 256))

    def body(dh_ref, w_ref, g_ref, up_ref, dg_ref, du_ref):
        d = _dot(dh_ref[...], w_ref[...], 1, 1)
        g = g_ref[...]
        sg = _sigmoid(g)
        dg_ref[...] = (d * up_ref[...] * (sg + g * sg * (1.0 - sg))).astype(BF16)
        du_ref[...] = (d * g * sg).astype(BF16)

    blk = pl.BlockSpec((tm, tn), lambda j, i: (i, j))
    return pl.pallas_call(
        body, grid=(DFF // tn, T // tm),
        in_specs=[pl.BlockSpec((tm, D), lambda j, i: (i, 0)), pl.BlockSpec((tn, D), lambda j, i: (j, 0)), blk, blk],
        out_specs=[blk] * 2, out_shape=[jax.ShapeDtypeStruct((T, DFF), BF16)] * 2,
        compiler_params=_cp("parallel", "parallel"), name=name)(dhb, w, gate, up)


def _adamw(w, g, m, v, name):
    R, C = w.shape
    tr = _pick(R, tuple(t for t in (512, 256, 128, 64, 32, 16, 8) if t * C * 4 <= (3 << 19)))
    c1 = 1.0 - ADAM_B1 ** ADAM_STEP
    c2 = 1.0 - ADAM_B2 ** ADAM_STEP

    def body(w_ref, g_ref, m_ref, v_ref, d_ref, mo_ref, vo_ref):
        gv = g_ref[...]
        mn = ADAM_B1 * m_ref[...] + (1.0 - ADAM_B1) * gv
        vn = ADAM_B2 * v_ref[...] + (1.0 - ADAM_B2) * (gv * gv)
        d_ref[...] = -ADAM_LR * ((mn / c1) / (jnp.sqrt(vn / c2) + ADAM_EPS) + ADAM_WD * w_ref[...])
        mo_ref[...] = mn
        vo_ref[...] = vn

    blk = pl.BlockSpec((tr, C), lambda i: (i, 0))
    return pl.pallas_call(
        body, grid=(R // tr,), in_specs=[blk] * 4, out_specs=[blk] * 3,
        out_shape=[jax.ShapeDtypeStruct((R, C), F32)] * 3, compiler_params=_cp("parallel"), name=name)(w, g, m, v)


def _sum_lead(x, name):
    n, R, C = x.shape
    tr = _pick(R, (512, 256, 128, 64, 32, 16, 8))

    def body(x_ref, o_ref):
        acc = x_ref[0]
        for d in range(1, n):
            acc = acc + x_ref[d]
        o_ref[...] = acc

    return pl.pallas_call(
        body, grid=(R // tr,), in_specs=[pl.BlockSpec((n, tr, C), lambda i: (0, i, 0))],
        out_specs=pl.BlockSpec((tr, C), lambda i: (i, 0)), out_shape=jax.ShapeDtypeStruct((R, C), F32),
        compiler_params=_cp("parallel"), name=name)(x)


def _here():
    return lax.axis_index("x"), lax.axis_index("y"), lax.axis_index("c")


def _rcopy(src, dst, send_sems, recv_sems, k, to):
    return pltpu.make_async_remote_copy(src_ref=src, dst_ref=dst, send_sem=send_sems.at[k], recv_sem=recv_sems.at[k],
                                        device_id=to, device_id_type=MESH)


def _hbm_calls(body, args, out_shapes, n_sems, aliases, name):
    return pl.pallas_call(
        body, in_specs=[ANY] * len(args), out_specs=[ANY] * len(out_shapes), out_shape=out_shapes,
        input_output_aliases=aliases,
        scratch_shapes=[pltpu.SemaphoreType.DMA((n_sems,)), pltpu.SemaphoreType.DMA((n_sems,))],
        compiler_params=pltpu.CompilerParams(has_side_effects=True), name=name)(*args)


def _gather_plan(outs, axes, send_sems, recv_sems):
    x, y, c = _here()
    sib = (x, y, 1 - c)
    chips = [(1 - x, y), (x, 1 - y), (1 - x, 1 - y)]
    todo = [(t, k, chip) for t in range(len(outs)) for k, chip in enumerate(chips)]

    def win(t, chip, hc):
        o, ax = outs[t], axes[t]
        w = o.shape[ax] // N_SHARD
        first = (2 * chip[0] + chip[1]) * w
        if ax == 0:
            return o.at[pl.ds(pl.multiple_of(first + hc * (w // 2), 16), w // 2), :]
        rows = o.shape[0] // 2
        return o.at[pl.ds(pl.multiple_of(hc * rows, 16), rows), pl.ds(pl.multiple_of(first, BLK), w)]

    def copy(t, k, chip, hc, to):
        return _rcopy(win(t, chip, hc), win(t, chip, hc), send_sems, recv_sems, 6 * t + k, to)

    def start():
        for t, k, chip in todo:
            copy(t, k, (x, y), c, (*chip, c)).start()

    def finish():
        for t, k, chip in todo:
            copy(t, k, chip, c, (*chip, c)).wait_recv()
            copy(t, 3 + k, chip, c, sib).start()
        for t, k, chip in todo:
            copy(t, 3 + k, chip, 1 - c, sib).wait_recv()
        for t, k, chip in todo:
            copy(t, k, (x, y), c, (*chip, c)).wait_send()
            copy(t, 3 + k, chip, c, sib).wait_send()

    return start, finish


def _all_gather_weights(fulls, axes, name):
    nt = len(fulls)

    def body(*refs):
        start, finish = _gather_plan(refs[nt:2 * nt], axes, *refs[2 * nt:])
        start()
        finish()

    return _hbm_calls(body, fulls, [jax.ShapeDtypeStruct(f.shape, f.dtype) for f in fulls], 6 * nt,
                      {t: t for t in range(nt)}, name)


def _half(ref, ax, hc):
    n = ref.shape[ax] // 2
    sl = pl.ds(pl.multiple_of(hc * n, 8), n)
    return ref.at[sl, :] if ax == 0 else ref.at[:, sl]


def _shrunk(shape, ax, by):
    shape = list(shape)
    shape[ax] //= by
    return tuple(shape)


def _swap_halves(gs, haxes, name):
    nt = len(gs)

    def body(*refs):
        ins, outs, (send_sems, recv_sems) = refs[:nt], refs[nt:2 * nt], refs[2 * nt:]
        x, y, c = _here()
        cps = [_rcopy(_half(g, ax, 1 - c), o, send_sems, recv_sems, t, (x, y, 1 - c))
               for t, (g, o, ax) in enumerate(zip(ins, outs, haxes))]
        for cp in cps:
            cp.start()
        for cp in cps:
            cp.wait()

    return _hbm_calls(body, gs, [jax.ShapeDtypeStruct(_shrunk(g.shape, ax, 2), g.dtype) for g, ax in zip(gs, haxes)],
                      nt, {}, name)


def _scatter_plan(ins, outs, saxes, send_sems, recv_sems):
    x, y, c = _here()
    chips = [(1 - x, y), (x, 1 - y), (1 - x, 1 - y)]

    def copies():
        cps = []
        for t, (s, o, ax) in enumerate(zip(ins, outs, saxes)):
            w = s.shape[ax] // N_SHARD
            for k, chip in enumerate(chips):
                first = pl.multiple_of((2 * chip[0] + chip[1]) * w, 8)
                src = s.at[pl.ds(first, w), :] if ax == 0 else s.at[:, pl.ds(first, w)]
                cps.append(_rcopy(src, o.at[k], send_sems, recv_sems, 3 * t + k, (*chip, c)))
        return cps

    def start():
        for cp in copies():
            cp.start()

    def finish():
        for cp in copies():
            cp.wait()

    return start, finish


def _scatter_shapes(sbs, saxes):
    return [jax.ShapeDtypeStruct((3,) + _shrunk(s.shape, ax, N_SHARD), s.dtype) for s, ax in zip(sbs, saxes)]


def _small_gather_copies(out_ref, send_sems, recv_sems, k0):
    x, y, c = _here()
    me = 4 * x + 2 * y + c
    cps = []
    for k in range(1, 8):
        to = (x ^ ((k >> 2) & 1), y ^ ((k >> 1) & 1), c ^ (k & 1))
        peer = 4 * to[0] + 2 * to[1] + to[2]
        cps.append((_rcopy(out_ref.at[me], out_ref.at[me], send_sems, recv_sems, k0 + k - 1, to),
                    _rcopy(out_ref.at[peer], out_ref.at[peer], send_sems, recv_sems, k0 + k - 1, to)))
    return cps


def _scatter_to_chips(sbs, saxes, small, name):
    nt = len(sbs)

    def body(*refs):
        small_out = refs[2 * nt + 1]
        send_sems, recv_sems = refs[2 * nt + 2:]
        start, finish = _scatter_plan(refs[:nt], refs[nt + 1:2 * nt + 1], saxes, send_sems, recv_sems)
        start()
        cps = _small_gather_copies(small_out, send_sems, recv_sems, 3 * nt)
        for snd, _ in cps:
            snd.start()
        for _, rcv in cps:
            rcv.wait_recv()
        for snd, _ in cps:
            snd.wait_send()
        finish()

    outs = _hbm_calls(body, list(sbs) + [small], _scatter_shapes(sbs, saxes) + [jax.ShapeDtypeStruct(small.shape, small.dtype)],
                      3 * nt + 7, {nt: nt}, name)
    return outs[:nt], outs[nt]


def _join_halves(fins, haxes, name):
    nt = len(fins)

    def body(*refs):
        outs, (send_sems, recv_sems) = refs[nt:2 * nt], refs[2 * nt:]
        x, y, c = _here()
        cps = [_rcopy(_half(o, ax, c), _half(o, ax, c), send_sems, recv_sems, t, (x, y, 1 - c))
               for t, (o, ax) in enumerate(zip(outs, haxes))]
        for cp in cps:
            cp.start()
        for t, (o, ax) in enumerate(zip(outs, haxes)):
            _rcopy(_half(o, ax, 1 - c), _half(o, ax, 1 - c), send_sems, recv_sems, t, (x, y, 1 - c)).wait_recv()
        for cp in cps:
            cp.wait_send()

    return _hbm_calls(body, fins, [jax.ShapeDtypeStruct(f.shape, f.dtype) for f in fins], nt, {t: t for t in range(nt)}, name)


def _all_gather_small(buf, name):
    def body(_, out_ref, send_sems, recv_sems):
        cps = _small_gather_copies(out_ref, send_sems, recv_sems, 0)
        for snd, _ in cps:
            snd.start()
        for _, rcv in cps:
            rcv.wait_recv()
        for snd, _ in cps:
            snd.wait_send()

    return _hbm_calls(body, [buf], [jax.ShapeDtypeStruct(buf.shape, buf.dtype)], 7, {0: 0}, name)[0]


def _place(block, n, index):
    buf = jnp.zeros((n,) + block.shape[1:], block.dtype)
    return lax.dynamic_update_slice_in_dim(buf, block, index, axis=0)


def _add_half(g, other, hax, cidx, name):
    r, cw = other.shape
    tr = _pick(r, tuple(t for t in (512, 256, 128, 64, 32, 16, 8) if t * cw * 4 <= (1 << 21)))
    nr = r // tr

    def body(c_ref, g_ref, o_ref, s_ref, sb_ref):
        s = g_ref[...] + o_ref[...]
        s_ref[...] = s
        sb_ref[...] = s.astype(BF16)

    g_map = (lambda i, c: (c[0] * nr + i, 0)) if hax == 0 else (lambda i, c: (i, c[0]))
    blk = pl.BlockSpec((tr, cw), lambda i, c: (i, 0))
    return pl.pallas_call(
        body,
        grid_spec=pltpu.PrefetchScalarGridSpec(
            num_scalar_prefetch=1, grid=(nr,), in_specs=[pl.BlockSpec((tr, cw), g_map), blk], out_specs=[blk, blk]),
        out_shape=[jax.ShapeDtypeStruct((r, cw), F32), jax.ShapeDtypeStruct((r, cw), BF16)],
        compiler_params=_cp("parallel"), name=name)(cidx, g, other)


def _add_chips(s, recv, sax, chip_idx, cidx, name):
    _, r, cw = recv.shape
    tr = _pick(r, tuple(t for t in (512, 352, 256, 128, 64, 32, 16, 8) if t * cw * 4 <= (1 << 21)))
    nr = r // tr

    def body(chip_ref, c_ref, s_ref, r_ref, out_ref):
        out_ref[...] = ((s_ref[...] + r_ref[0].astype(F32)) + r_ref[1].astype(F32)) + r_ref[2].astype(F32)

    if sax == 0:
        s_map, o_map, shape = (lambda i, chip, c: (chip[0] * nr + i, 0)), (lambda i, chip, c: (i, c[0])), (r, 2 * cw)
    else:
        s_map, o_map, shape = (lambda i, chip, c: (i, chip[0])), (lambda i, chip, c: (c[0] * nr + i, 0)), (2 * r, cw)
    return pl.pallas_call(
        body,
        grid_spec=pltpu.PrefetchScalarGridSpec(
            num_scalar_prefetch=2, grid=(nr,),
            in_specs=[pl.BlockSpec((tr, cw), s_map), pl.BlockSpec((3, tr, cw), lambda i, chip, c: (0, i, 0))],
            out_specs=pl.BlockSpec((tr, cw), o_map)),
        out_shape=jax.ShapeDtypeStruct(shape, F32), compiler_params=_cp("parallel"), name=name)(chip_idx, cidx, s, recv)


IN_SHARD = IN_COLS // N_SHARD
IN_SLOT = INP // N_SHARD
IN_PIECES = ((0, 512, QA), (512, 640, KA), (640, 768, VA), (768, 1280, QF), (1280, 1792, KF), (1792, 2304, VF),
             (2304, 2312, FL), (2312, 2824, XC), (2824, 3336, YC), (3336, 6408, GT))


def _gathered_to_kernel_cols(w):
    parts, pos = [], 0
    for a, b, k in sorted(IN_PIECES, key=lambda p: p[2]):
        assert k == pos
        while a < b:
            j = a // IN_SHARD
            e = min(b, (j + 1) * IN_SHARD)
            g = j * IN_SLOT + a - j * IN_SHARD
            parts.append(w[..., g:g + e - a])
            pos += e - a
            a = e
    parts.append(jnp.zeros(w.shape[:-1] + (INP - pos,), w.dtype))
    return jnp.concatenate(parts, axis=-1)


def _kernel_to_gathered_cols(w):
    parts = []
    for j in range(N_SHARD):
        lo, hi = j * IN_SHARD, (j + 1) * IN_SHARD
        for a, b, k in IN_PIECES:
            s, e = max(a, lo), min(b, hi)
            if s < e:
                parts.append(w[..., k + s - a:k + e - a])
        parts.append(jnp.zeros(w.shape[:-1] + (IN_SLOT - IN_SHARD,), w.dtype))
    return jnp.concatenate(parts, axis=-1)


def _pair_blocks(w):
    z = jnp.zeros((4, 64, 64), w.dtype)
    w = w.reshape(4, 2, 64, 64)
    top = jnp.concatenate([w[:, 0], z], axis=2)
    bot = jnp.concatenate([z, w[:, 1]], axis=2)
    return jnp.concatenate([top, bot], axis=1)


def _unpair_blocks(w):
    return jnp.stack([w[:, :64, :64], w[:, 64:, 64:]], axis=1).reshape(8, 64, 64)


BIG = ("w_in", "w_branch", "w_out", "w_ffn_in", "w_ffn_out")
TINY = ("conv_w", "meta_tokens")
SMALL = ("rel_bias_table", "norm_mix", "swa_sinks", "fox_forget_bias", "conv_b", "lru_w_r", "lru_b_r", "lru_w_i",
         "lru_b_i", "lru_lambda", "norm_ffn", "norm_final")
SHARD_AXIS = {"conv_w": 2, "meta_tokens": 1}
BIG_AXIS = {"w_in": 2, "w_branch": 2, "w_out": 1, "w_ffn_in": 2, "w_ffn_out": 1}


def _pack(d, names):
    flat = jnp.concatenate([d[n].reshape(-1) for n in names])
    pad = (-flat.shape[0]) % (256 * 128)
    return jnp.concatenate([flat, jnp.zeros((pad,), F32)]).reshape(-1, 128)


def _unpack(buf, names, shapes):
    flat, out, off = buf.reshape(-1), {}, 0
    for n in names:
        sz = int(np.prod(shapes[n]))
        out[n] = flat[off:off + sz].reshape(shapes[n])
        off += sz
    return out


def _layer_layout(n, a):
    if n == "w_in":
        return _gathered_to_kernel_cols(a)
    return a.reshape(3, LW, D) if n == "w_branch" else a


def _local_step(x, tgt, W, placed=None):
    S = x.shape[0]
    T = S + BLK
    tm = _pick(T, (1408, 384, 128))
    bucket = jnp.asarray(_bucket_table())
    bias = _bias_build(W["rel_bias_table"], bucket, "bias_build")
    h = jnp.concatenate([jnp.zeros((NPAD, D), F32), W["meta_tokens"], x], axis=0)
    if placed is None:
        WL = {n: [W[n][l] for l in range(DEPTH)] for n in BIG}
    else:
        WL = {n: [W[n]] + [None] * (DEPTH - 1) for n in BIG}

    saved = []
    for l in range(DEPTH):
        sv = {"h0": h}
        u, u_t = _rms_fwd(h, W["norm_mix"][l], f"rms_mix_fwd")
        proj = _mm(u, WL["w_in"][l], tm=tm, tn=512, tk=D, name="mm_in_fwd")
        oa, oa_t = _swa_fwd(proj, bias, W["swa_sinks"][l], "swa_fwd")
        fb = W["fox_forget_bias"][l].reshape(NH, 1)
        qaug, kaug, kaug_t, vm, vo = _fox_prep(proj, _cum_fwd(proj, fb, "cum_fwd"), "fox_prep")
        if placed is not None and l + 1 < DEPTH:
            of, of_t, lse0, lse1, *got = _fox_fwd(qaug, kaug_t, vo, "fox_fwd_gather", gather=(placed[l + 1], GATHER_AXES))
            for n, a in zip(BIG, got):
                WL[n][l + 1] = _layer_layout(n, a)
            lse = [lse0, lse1]
        else:
            of, of_t, *lse = _fox_fwd(qaug, kaug_t, vo, "fox_fwd")
        lru_vec = jnp.concatenate([W["lru_b_r"][l][None], W["lru_b_i"][l][None], W["lru_lambda"][l][None],
                                   W["conv_b"][l][None], jnp.zeros((4, LW), F32)], axis=0)
        oc, oc_t, hs = _lru_fwd(proj, W["conv_w"][l], W["lru_w_r"][l], W["lru_w_i"][l], lru_vec, "lru_fwd")
        *bs, merged, merged_t = _branch_merge_fwd(oa, of, oc, WL["w_branch"][l], proj, "branch_merge_fwd")
        h2 = _mm(merged, WL["w_out"][l], res=h, tm=tm, tn=512, tk=D, name="mm_out_fwd")
        u2, u2_t = _rms_fwd(h2, W["norm_ffn"][l], "rms_ffn_fwd")
        gate, up, act, act_t = _ffn_in_swiglu_fwd(u2, WL["w_ffn_in"][l], "ffn_in_swiglu_fwd")
        h = _mm(act, WL["w_ffn_out"][l], res=h2, tm=tm, tn=512, tk=_pick(DFF, (1408, 256)), name="mm_ffn_out_fwd")
        sv.update(u_t=u_t, proj=proj, o_t=(oa_t, of_t, oc_t), of=of, lse=lse, hs=hs, fb=fb, qaug=qaug, kaug=kaug, kaug_t=kaug_t, vm=vm,
                  lru_vec=lru_vec, bs=bs, merged_t=merged_t, h2=h2, u2_t=u2_t, gate=gate, up=up, act_t=act_t)
        saved.append(sv)

    tgt_pad = tgt
    dh, dhb, dg_final, loss_vec = _loss_head(h, tgt_pad, W["norm_final"], "loss_head")
    loss = loss_vec[0, 0]

    small = ("norm_mix", "swa_sinks", "fox_forget_bias", "conv_w", "conv_b", "lru_w_r", "lru_b_r", "lru_w_i", "lru_b_i",
             "lru_lambda", "norm_ffn")
    G = {n: [None] * DEPTH for n in small}
    G["norm_final"] = dg_final.reshape(D)
    GW = {n: [None] * DEPTH for n in BIG}
    dist = placed is not None
    if dist:
        x_, y_, c_ = _here()
        cidx = jnp.reshape(c_, (1,)).astype(jnp.int32)
        chip = jnp.reshape(2 * x_ + y_, (1,)).astype(jnp.int32)

    def finish_layer(lp, ss, recv):
        fins = [_add_chips(s, r, ax, chip, cidx, "rs_add_chips_" + n) for n, s, r, ax in zip(BIG, ss, recv, GATHER_AXES)]
        for n, f in zip(BIG, _join_halves(fins, HALF_AXES, "rs_join_halves")):
            GW[n][lp] = f

    pend = None
    dbias = jnp.zeros((NH, BLK, 2 * BLK), F32)
    tkT = tm
    for l in reversed(range(DEPTH)):
        sv = saved[l]
        dw = {}
        dw["w_ffn_out"] = _mm(sv["act_t"], dhb, tm=_pick(DFF, (1408, 256)), tn=D, tk=tkT,
                              name="mm_ffn_out_dw")
        dgate, dup = _ffn_out_dx_swiglu_bwd(dhb, WL["w_ffn_out"][l], sv["gate"], sv["up"], "ffn_out_dx_swiglu_bwd")
        u2t = sv["u2_t"]
        du2, buf = None, None
        for half, dpart in enumerate((dgate, dup)):
            buf = _mm(u2t, dpart, tm=D, tn=_pick(DFF, (1408, 256)), tk=tkT, slab=(buf, 0, 1),
                      col0=half * DFF, cols=2 * DFF, name="mm_ffn_in_dw")
            du2 = _mm(dpart, WL["w_ffn_in"][l], tb=True, res=du2, b_k0=half * DFF, tm=tm, tn=512,
                      tk=_pick(DFF, (1408, 256)), name="mm_ffn_in_dx")
        dw["w_ffn_in"] = buf.reshape(D, 2 * DFF)
        dh, dhb, dgn = _rms_bwd(du2, sv["h2"], W["norm_ffn"][l], dh, "rms_ffn_bwd")
        G["norm_ffn"][l] = dgn.reshape(D)
        dw["w_out"] = _mm(sv["merged_t"], dhb, tm=D, tn=D, tk=tkT, name="mm_out_dw")
        db0, db1, db2, dg0, dg1, dg2 = _out_dx_merge_bwd(dhb, WL["w_out"][l], sv["proj"], *sv["bs"], "out_dx_merge_bwd")
        dos, buf = [], None
        for g, (o_t, db) in enumerate(zip(sv["o_t"], (db0, db1, db2))):
            buf = _mm(o_t, db, tm=LW, tn=D, tk=tkT, slab=(buf, g, 3), name="mm_branch_dw")
            dos.append(_mm(db, WL["w_branch"][l][g], tb=True, out_dtype=BF16, tm=tm, tn=LW, tk=D, name="mm_branch_dx"))
        dw["w_branch"] = buf.reshape(3 * LW, D)
        dqa, dkb, dvb, dbias, dsk = _swa_bwd(sv["proj"], bias, W["swa_sinks"][l], dos[0], dbias, "swa_bwd")
        dka, dva = _band_fold(dkb, dvb, "swa_band_fold")
        G["swa_sinks"][l] = dsk[0, :NH]
        delta = _fox_delta(dos[1], sv["of"], "fox_delta")
        fox_args = (sv["qaug"], sv["kaug"], sv["kaug_t"], sv["vm"], dos[1], sv["lse"], delta)
        if pend is not None:
            dqf, dqx, dkf, dvf, dkx, *recv = _fox_bwd(*fox_args, "fox_bwd_scatter", scatter=(pend[2], GATHER_AXES))
            finish_layer(pend[0], pend[1], recv)
            pend = None
        else:
            dqf, dqx, dkf, dvf, dkx = _fox_bwd(*fox_args, "fox_bwd")
        dfl, dfb = _cum_bwd(dqx, dkx, sv["proj"], sv["fb"], "cum_bwd")
        G["fox_forget_bias"][l] = dfb[:, 0]
        dxc, dyc, dwr, dwi, dvec = _lru_bwd(sv["proj"], sv["hs"], dos[2], W["conv_w"][l], W["lru_w_r"][l], W["lru_w_i"][l],
                                            sv["lru_vec"], "lru_bwd")
        G["lru_w_r"][l], G["lru_w_i"][l] = _unpair_blocks(dwr), _unpair_blocks(dwi)
        G["lru_b_r"][l], G["lru_b_i"][l], G["lru_lambda"][l], G["conv_b"][l] = dvec[0], dvec[1], dvec[2], dvec[3]
        G["conv_w"][l] = dvec[4:8]
        dproj = jnp.concatenate([dqa, dqf, dkf, dvf, dxc, dyc, dg0, dg1, dg2, dka, dva, dfl], axis=1)
        dw["w_in"] = _mm(sv["u_t"], dproj, tm=D, tn=IN_SLOT, tk=tkT, name="mm_in_dw")
        du = _mm(dproj, WL["w_in"][l], tb=True, tm=tm, tn=512, tk=_pick(INP, (1664, 512)), name="mm_in_dx")
        dh, dhb, dgn = _rms_bwd(du, sv["h0"], W["norm_mix"][l], dh, "rms_mix_bwd")
        G["norm_mix"][l] = dgn.reshape(D)
        if dist:
            gs = [dw[n] for n in BIG]
            pairs = [_add_half(g, r, hax, cidx, "rs_add_half_" + n)
                     for n, g, r, hax in zip(BIG, gs, _swap_halves(gs, HALF_AXES, "rs_swap_halves"), HALF_AXES)]
            ss, sbs = [list(t) for t in zip(*pairs)]
            ss[0], sbs[0] = _kernel_to_gathered_cols(ss[0]), _kernel_to_gathered_cols(sbs[0])
            pend = (l, ss, sbs)
        else:
            for n in BIG:
                GW[n][l] = dw[n]
    grads = {n: (jnp.stack(v) if isinstance(v, list) else v) for n, v in G.items()}
    grads["rel_bias_table"] = _bias_bwd(dbias, bucket, "bias_bwd")
    grads["meta_tokens"] = dh[NPAD:BLK]
    if dist:
        mine = _place(_pack(grads, SMALL + TINY)[None], 8, 4 * x_ + 2 * y_ + c_)
        recv, grads["small_gathered"] = _scatter_to_chips(pend[2], GATHER_AXES, mine, "rs_scatter")
        finish_layer(pend[0], pend[1], recv)
    grads.update({n: jnp.stack(GW[n]) for n in BIG})
    return loss, dh[BLK:], grads


NAMES = ("meta_tokens", "rel_bias_table", "norm_mix", "w_in", "swa_sinks", "fox_forget_bias", "conv_w", "conv_b",
         "lru_w_r", "lru_b_r", "lru_w_i", "lru_b_i", "lru_lambda", "w_branch", "w_out", "norm_ffn", "w_ffn_in",
         "w_ffn_out", "norm_final")


def _three_d(n, a):
    return a.reshape(DEPTH, 3 * LW, -1) if n == "w_branch" else a


GATHER_AXES = [BIG_AXIS[n] - 1 for n in BIG]
HALF_AXES = [1 - a for a in GATHER_AXES]


def _gather_weights(P):
    x, y, c = _here()
    mine, me = 2 * x + y, 4 * x + 2 * y + c
    placed = []
    for l in range(DEPTH):
        bufs = []
        for n in BIG:
            shard = _three_d(n, P[n])[l].astype(BF16)
            if n == "w_in":
                shard = jnp.pad(shard, ((0, 0), (0, IN_SLOT - IN_SHARD)))
            zero = jnp.zeros_like(shard)
            bufs.append(jnp.concatenate([jnp.where(mine == j, shard, zero) for j in range(N_SHARD)], axis=BIG_AXIS[n] - 1))
        placed.append(bufs)
    full = {n: _layer_layout(n, a) for n, a in zip(BIG, _all_gather_weights(placed[0], GATHER_AXES, "ag_weights"))}
    tiny = _all_gather_small(_place(_pack(P, TINY)[None], 8, me), "ag_tiny_weights")
    parts = [_unpack(tiny[2 * j], TINY, {n: P[n].shape for n in TINY}) for j in range(N_SHARD)]
    for n in TINY:
        full[n] = jnp.concatenate([p[n] for p in parts], axis=SHARD_AXIS[n])
    for n in SMALL:
        full[n] = P[n]
    full["lru_w_r"] = jnp.stack([_pair_blocks(P["lru_w_r"][l]) for l in range(DEPTH)]).astype(BF16)
    full["lru_w_i"] = jnp.stack([_pair_blocks(P["lru_w_i"][l]) for l in range(DEPTH)]).astype(BF16)
    return full, placed


def _reduce_grads(grads, P):
    x, y, c = _here()
    mine, me = 2 * x + y, 4 * x + 2 * y + c
    out = {n: grads[n].reshape(P[n].shape) for n in BIG if n != "w_in"}
    out["w_in"] = grads["w_in"][:, :, :IN_SHARD]
    names = SMALL + TINY
    small = _unpack(_sum_lead(grads["small_gathered"], "sum_small_grads"), names, {n: grads[n].shape for n in names})
    for n in SMALL:
        out[n] = small[n]
    for n in TINY:
        w = P[n].shape[SHARD_AXIS[n]]
        out[n] = lax.dynamic_slice_in_dim(small[n], mine * w, w, axis=SHARD_AXIS[n])
    return out


def _update(P, Gd, M, V):
    delta, new_m, new_v = {}, {}, {}
    for n in BIG + TINY:
        shp = P[n].shape
        two = (int(np.prod(shp[:-1])), shp[-1])
        d, m, v = _adamw(P[n].reshape(two), Gd[n].reshape(two), M[n].reshape(two), V[n].reshape(two), "adamw_" + n)
        delta[n], new_m[n], new_v[n] = d.reshape(shp), m.reshape(shp), v.reshape(shp)
    shapes = {n: P[n].shape for n in SMALL}
    d, m, v = _adamw(_pack(P, SMALL), _pack(Gd, SMALL), _pack(M, SMALL), _pack(V, SMALL), "adamw_small")
    for dst, buf in ((delta, d), (new_m, m), (new_v, v)):
        dst.update(_unpack(buf, SMALL, shapes))
    return delta, new_m, new_v


def kernel(x, meta_tokens, rel_bias_table, norm_mix, w_in, swa_sinks, fox_forget_bias, conv_w, conv_b, lru_w_r, lru_b_r, lru_w_i, lru_b_i, lru_lambda, w_branch, w_out, norm_ffn, w_ffn_in, w_ffn_out, norm_final, loss_target, m_meta_tokens, m_rel_bias_table, m_norm_mix, m_w_in, m_swa_sinks, m_fox_forget_bias, m_conv_w, m_conv_b, m_lru_w_r, m_lru_b_r, m_lru_w_i, m_lru_b_i, m_lru_lambda, m_w_branch, m_w_out, m_norm_ffn, m_w_ffn_in, m_w_ffn_out, m_norm_final, v_meta_tokens, v_rel_bias_table, v_norm_mix, v_w_in, v_swa_sinks, v_fox_forget_bias, v_conv_w, v_conv_b, v_lru_w_r, v_lru_b_r, v_lru_w_i, v_lru_b_i, v_lru_lambda, v_w_branch, v_w_out, v_norm_ffn, v_w_ffn_in, v_w_ffn_out, v_norm_final):
    P = dict(zip(NAMES, (meta_tokens, rel_bias_table, norm_mix, w_in, swa_sinks, fox_forget_bias, conv_w, conv_b, lru_w_r,
                         lru_b_r, lru_w_i, lru_b_i, lru_lambda, w_branch, w_out, norm_ffn, w_ffn_in, w_ffn_out, norm_final)))
    M = dict(zip(NAMES, (m_meta_tokens, m_rel_bias_table, m_norm_mix, m_w_in, m_swa_sinks, m_fox_forget_bias, m_conv_w,
                         m_conv_b, m_lru_w_r, m_lru_b_r, m_lru_w_i, m_lru_b_i, m_lru_lambda, m_w_branch, m_w_out, m_norm_ffn,
                         m_w_ffn_in, m_w_ffn_out, m_norm_final)))
    V = dict(zip(NAMES, (v_meta_tokens, v_rel_bias_table, v_norm_mix, v_w_in, v_swa_sinks, v_fox_forget_bias, v_conv_w,
                         v_conv_b, v_lru_w_r, v_lru_b_r, v_lru_w_i, v_lru_b_i, v_lru_lambda, v_w_branch, v_w_out, v_norm_ffn,
                         v_w_ffn_in, v_w_ffn_out, v_norm_final)))
    W, placed = _gather_weights(P)
    loss_local, grad_x, grads = _local_step(x[0], loss_target[0], W, placed)
    loss = lax.psum(loss_local, ("x", "y", "c"))
    Gd = _reduce_grads(grads, P)
    delta, new_m, new_v = _update(P, Gd, M, V)
    return (loss, grad_x[None], *[Gd[n] for n in NAMES], *[delta[n] for n in NAMES],
            *[new_m[n] for n in NAMES], *[new_v[n] for n in NAMES])
```

```python
import math

import numpy as np
import jax
import jax.numpy as jnp
from jax import lax
from jax.experimental import pallas as pl
from jax.experimental.pallas import tpu as pltpu

F32, BF16 = jnp.float32, jnp.bfloat16
MESH = pl.DeviceIdType.MESH
ANY = pl.BlockSpec(memory_space=pl.ANY)
SMEM = pl.BlockSpec(memory_space=pltpu.SMEM)

D = 1024
DEPTH = 4
BLK = 128
N_META = 16
NPAD = 112
NH = 8
LW = 512
DFF = 2816
EPS = 1e-6
NEG = -1e30
SCALE = 0.125
LRU_C = 8.0
REL_BUCKETS = 32
N_SHARD = 4
QA, QF, KF, VF, XC, YC, GT, KA, VA, FL, INP = 0, 512, 1024, 1536, 2048, 2560, 3072, 6144, 6272, 6400, 6656
IN_COLS = 6408
VMEM_LIMIT = 48 * 1024 * 1024

ADAM_LR, ADAM_B1, ADAM_B2, ADAM_EPS, ADAM_WD, ADAM_STEP = 0.001, 0.9, 0.999, 1e-08, 0.01, 10


def _cp(*sem):
    return pltpu.CompilerParams(dimension_semantics=sem or None, vmem_limit_bytes=VMEM_LIMIT)


def _pick(n, prefs):
    for p in prefs:
        if n % p == 0:
            return p
    return n


def _rt(T):
    return _pick(T, (384, 128))


def _sigmoid(z):
    return 1.0 / (1.0 + jnp.exp(-z))


def _log_sigmoid(z):
    return jnp.minimum(z, 0.0) - jnp.log(1.0 + jnp.exp(-jnp.abs(z)))


def _gelu(y):
    c = math.sqrt(2.0 / math.pi)
    return 0.5 * y * (1.0 + jnp.tanh(c * (y + 0.044715 * y * y * y)))


def _gelu_grad(y):
    c = math.sqrt(2.0 / math.pi)
    t = jnp.tanh(c * (y + 0.044715 * y * y * y))
    return 0.5 * (1.0 + t) + 0.5 * y * (1.0 - t * t) * c * (1.0 + 3.0 * 0.044715 * y * y)


def _neg_expm1(z):
    series = -z * (1.0 + z * (0.5 + z * (1.0 / 6.0 + z * (1.0 / 24.0 + z * (1.0 / 120.0)))))
    return jnp.where(z > -0.1, series, 1.0 - jnp.exp(z))


def _dot(a, b, ca, cb):
    return lax.dot_general(a, b, (((ca,), (cb,)), ((), ())), preferred_element_type=F32)


def _mm(a, b, *, ta=False, tb=False, res=None, out_dtype=F32, tm, tn, tk, name, slab=None, b_k0=0, col0=0, cols=None):
    M, K = (a.shape[1], a.shape[0]) if ta else a.shape
    N = b.shape[0] if tb else b.shape[1]
    assert (b.shape[1] if tb else b.shape[0]) >= K + b_k0 and M % tm == 0 and N % tn == 0 and K % tk == 0, (name, a.shape, b.shape)
    assert b_k0 % tk == 0 and col0 % tn == 0
    nk, kb, jb = K // tk, b_k0 // tk, col0 // tn
    ca, cb = (0 if ta else 1), (1 if tb else 0)
    n_in = 2 + (res is not None) + (slab is not None and slab[0] is not None)

    def body(*refs):
        a_ref, b_ref = refs[:2]
        r_ref = refs[2] if res is not None else None
        o_ref = refs[n_in]
        part = _dot(a_ref[...].astype(BF16), b_ref[...].astype(BF16), ca, cb)

        def fin(acc):
            if res is not None:
                acc = acc + r_ref[...]
            o_ref[...] = acc.astype(out_dtype)

        if nk == 1:
            fin(part)
        else:
            acc_ref = refs[-1]
            k = pl.program_id(2)

            @pl.when(k == 0)
            def _():
                acc_ref[...] = part

            @pl.when(k > 0)
            def _():
                acc_ref[...] += part

            @pl.when(k == nk - 1)
            def _():
                fin(acc_ref[...])

    a_spec = pl.BlockSpec((tk, tm), lambda i, j, k: (k, i)) if ta else pl.BlockSpec((tm, tk), lambda i, j, k: (i, k))
    b_spec = (pl.BlockSpec((tn, tk), lambda i, j, k: (j, k + kb)) if tb
              else pl.BlockSpec((tk, tn), lambda i, j, k: (k + kb, j)))
    o_spec = pl.BlockSpec((tm, tn), lambda i, j, k: (i, j))
    in_specs, ops = [a_spec, b_spec], [a, b]
    if res is not None:
        in_specs.append(o_spec)
        ops.append(res)
    out_shape, aliases = jax.ShapeDtypeStruct((M, N), out_dtype), {}
    if slab is not None:
        buf, idx, n = slab
        o_spec = pl.BlockSpec((None, tm, tn), lambda i, j, k: (idx, i, j + jb))
        out_shape = jax.ShapeDtypeStruct((n, M, cols or N), out_dtype)
        if buf is not None:
            aliases = {len(ops): 0}
            in_specs.append(ANY)
            ops.append(buf)
    return pl.pallas_call(
        body, grid=(M // tm, N // tn, nk), in_specs=in_specs, out_specs=o_spec, out_shape=out_shape,
        input_output_aliases=aliases, scratch_shapes=[pltpu.VMEM((tm, tn), F32)] if nk > 1 else [],
        compiler_params=_cp("parallel", "parallel", "arbitrary"), name=name)(*ops)


def _rms_fwd(h, g, name):
    T = h.shape[0]
    tr = _rt(T)

    def body(h_ref, g_ref, u_ref, ut_ref):
        x = h_ref[...]
        r = lax.rsqrt(jnp.mean(x * x, axis=-1, keepdims=True) + EPS)
        u = (x * r * g_ref[...]).astype(BF16)
        u_ref[...] = u
        ut_ref[...] = u.T

    return pl.pallas_call(
        body, grid=(T // tr,),
        in_specs=[pl.BlockSpec((tr, D), lambda i: (i, 0)), pl.BlockSpec((1, D), lambda i: (0, 0))],
        out_specs=[pl.BlockSpec((tr, D), lambda i: (i, 0)), pl.BlockSpec((D, tr), lambda i: (0, i))],
        out_shape=[jax.ShapeDtypeStruct((T, D), BF16), jax.ShapeDtypeStruct((D, T), BF16)],
        compiler_params=_cp("parallel"), name=name)(h, g.reshape(1, D))


def _rms_bwd(du, h, g, dres, name):
    T = h.shape[0]
    tr = _rt(T)

    def body(du_ref, h_ref, g_ref, dres_ref, dh_ref, dhb_ref, dg_ref):
        x = h_ref[...]
        r = lax.rsqrt(jnp.mean(x * x, axis=-1, keepdims=True) + EPS)
        xh = x * r
        dy = du_ref[...]
        dxh = dy * g_ref[...]
        dx = r * (dxh - xh * jnp.mean(dxh * xh, axis=-1, keepdims=True))
        dh = dres_ref[...] + dx
        dh_ref[...] = dh
        dhb_ref[...] = dh.astype(BF16)
        part = jnp.sum(dy * xh, axis=0, keepdims=True)

        @pl.when(pl.program_id(0) == 0)
        def _():
            dg_ref[...] = part

        @pl.when(pl.program_id(0) > 0)
        def _():
            dg_ref[...] += part

    row = pl.BlockSpec((tr, D), lambda i: (i, 0))
    vec = pl.BlockSpec((1, D), lambda i: (0, 0))
    return pl.pallas_call(
        body, grid=(T // tr,), in_specs=[row, row, vec, row], out_specs=[row, row, vec],
        out_shape=[jax.ShapeDtypeStruct((T, D), F32), jax.ShapeDtypeStruct((T, D), BF16), jax.ShapeDtypeStruct((1, D), F32)],
        compiler_params=_cp("arbitrary"), name=name)(du, h, g.reshape(1, D), dres)


def _loss_head(h, tgt, g, name):
    T = h.shape[0]
    nb = T // BLK

    def body(h_ref, t_ref, g_ref, dh_ref, dhb_ref, dg_ref, loss_ref):
        i = pl.program_id(0)
        x = h_ref[...]
        r = lax.rsqrt(jnp.mean(x * x, axis=-1, keepdims=True) + EPS)
        xh = x * r
        gv = g_ref[...]
        tok = i >= 1
        err = jnp.where(tok, xh * gv - t_ref[...], 0.0)
        dy = err * (1.0 / D)
        dxh = dy * gv
        dx = r * (dxh - xh * jnp.mean(dxh * xh, axis=-1, keepdims=True))
        dh_ref[...] = dx
        dhb_ref[...] = dx.astype(BF16)
        dg = jnp.sum(dy * xh, axis=0, keepdims=True)
        ls = jnp.zeros((1, BLK), F32) + jnp.sum(err * err) * (0.5 / D)

        @pl.when(i == 0)
        def _():
            dg_ref[...] = dg
            loss_ref[...] = ls

        @pl.when(i > 0)
        def _():
            dg_ref[...] += dg
            loss_ref[...] += ls

    row = pl.BlockSpec((BLK, D), lambda i: (i, 0))
    vec = pl.BlockSpec((1, D), lambda i: (0, 0))
    return pl.pallas_call(
        body, grid=(nb,),
        in_specs=[row, pl.BlockSpec((BLK, D), lambda i: (jnp.maximum(i - 1, 0), 0)), vec],
        out_specs=[row, row, vec, pl.BlockSpec((1, BLK), lambda i: (0, 0))],
        out_shape=[jax.ShapeDtypeStruct((T, D), F32), jax.ShapeDtypeStruct((T, D), BF16),
                   jax.ShapeDtypeStruct((1, D), F32), jax.ShapeDtypeStruct((1, BLK), F32)],
        compiler_params=_cp("arbitrary"), name=name)(h, tgt, g.reshape(1, D))


def _bucket_table():
    q = np.arange(BLK)[:, None]
    k = np.arange(2 * BLK)[None, :]
    d = np.maximum(q + BLK - k, 0)
    max_exact = REL_BUCKETS // 2
    scaled = np.log(np.maximum(d, 1).astype(np.float32) / np.float32(max_exact)) / np.float32(math.log(128 / max_exact))
    large = np.minimum(max_exact + (scaled.astype(np.float32) * (REL_BUCKETS - max_exact)).astype(np.int32), REL_BUCKETS - 1)
    return np.where(d < max_exact, d, large).astype(np.int32)


def _bias_build(table, bucket, name):
    def body(t_ref, bk_ref, o_ref):
        bk = bk_ref[...]
        for h in range(NH):
            acc = jnp.zeros((BLK, 2 * BLK), F32)
            for b in range(REL_BUCKETS):
                acc = jnp.where(bk == b, t_ref[b, h], acc)
            o_ref[h] = acc

    return pl.pallas_call(
        body, in_specs=[SMEM, pl.BlockSpec(memory_space=pltpu.VMEM)], out_specs=pl.BlockSpec(memory_space=pltpu.VMEM),
        out_shape=jax.ShapeDtypeStruct((NH, BLK, 2 * BLK), F32), compiler_params=_cp(), name=name)(table, bucket)


def _bias_bwd(dbias, bucket, name):
    def body(d_ref, bk_ref, o_ref):
        bk = bk_ref[...]
        for h in range(NH):
            dh = d_ref[h]
            for b in range(REL_BUCKETS):
                o_ref[b, h] = jnp.sum(jnp.where(bk == b, dh, 0.0))

    return pl.pallas_call(
        body, in_specs=[pl.BlockSpec(memory_space=pltpu.VMEM)] * 2, out_specs=SMEM,
        out_shape=jax.ShapeDtypeStruct((REL_BUCKETS, NH), F32), compiler_params=_cp(), name=name)(dbias, bucket)


def _swa_specs(nq_cols):
    prev = lambda n: jnp.maximum(n - 1, 0)
    return [
        pl.BlockSpec((BLK, nq_cols), lambda n: (n, QA // nq_cols)),
        pl.BlockSpec((BLK, BLK), lambda n: (prev(n), KA // BLK)), pl.BlockSpec((BLK, BLK), lambda n: (n, KA // BLK)),
        pl.BlockSpec((BLK, BLK), lambda n: (prev(n), VA // BLK)), pl.BlockSpec((BLK, BLK), lambda n: (n, VA // BLK)),
    ]


def _swa_mask(n):
    row = lax.broadcasted_iota(jnp.int32, (BLK, 2 * BLK), 0)
    col = lax.broadcasted_iota(jnp.int32, (BLK, 2 * BLK), 1)
    dist = row + BLK - col
    return (dist >= 0) & (dist < BLK) & ((n - 1) * BLK + col >= NPAD)


def _swa_probs(qm, ksel, mask, bias_h, sink):
    s = _dot(qm, ksel, 1, 1) * SCALE
    s = jnp.where(mask, s + bias_h, NEG)
    m = jnp.maximum(jnp.max(s, axis=-1, keepdims=True), sink)
    p = jnp.exp(s - m)
    psink = jnp.exp(sink - m)
    inv = 1.0 / (jnp.sum(p, axis=-1, keepdims=True) + psink)
    return p * inv, psink * inv


def _swa_fwd(proj, bias, sinks, name):
    T = proj.shape[0]
    nb = T // BLK

    def body(sk_ref, q_ref, kp_ref, kc_ref, vp_ref, vc_ref, b_ref, o_ref, ot_ref):
        n = pl.program_id(0)
        lo = lax.broadcasted_iota(jnp.int32, (1, BLK), 1) < 64
        kb = jnp.concatenate([kp_ref[...], kc_ref[...]], axis=0)
        vb = jnp.concatenate([vp_ref[...], vc_ref[...]], axis=0)
        kbs = (kb.astype(BF16), pltpu.roll(kb, 64, 1).astype(BF16))
        vbs = (vb, pltpu.roll(vb, 64, 1))
        mask = _swa_mask(n)
        outs = []
        for pr in range(NH // 2):
            qp = q_ref[:, pr * BLK:(pr + 1) * BLK]
            kv = pr // 2
            acc = jnp.zeros((BLK, BLK), F32)
            for e in range(2):
                lm = lo if e == 0 else jnp.logical_not(lo)
                sw = 0 if kv == e else 1
                qm = jnp.where(lm, qp, 0.0).astype(BF16)
                pn, _ = _swa_probs(qm, kbs[sw], mask, b_ref[2 * pr + e], sk_ref[2 * pr + e])
                acc = acc + _dot(pn.astype(BF16), jnp.where(lm, vbs[sw], 0.0).astype(BF16), 1, 0)
            outs.append(acc)
        o = jnp.concatenate(outs, axis=1).astype(BF16)
        o_ref[...] = o
        ot_ref[...] = o.T

    return pl.pallas_call(
        body, grid=(nb,),
        in_specs=[SMEM] + _swa_specs(512) + [pl.BlockSpec((NH, BLK, 2 * BLK), lambda n: (0, 0, 0))],
        out_specs=[pl.BlockSpec((BLK, 512), lambda n: (n, 0)), pl.BlockSpec((512, BLK), lambda n: (0, n))],
        out_shape=[jax.ShapeDtypeStruct((T, 512), BF16), jax.ShapeDtypeStruct((512, T), BF16)],
        compiler_params=_cp("parallel"), name=name)(sinks, proj, proj, proj, proj, proj, bias)


def _swa_bwd(proj, bias, sinks, do, dbias_in, name):
    T = proj.shape[0]
    nb = T // BLK

    def body(sk_ref, q_ref, kp_ref, kc_ref, vp_ref, vc_ref, b_ref, do_ref, dbi_ref,
             dq_ref, dk_ref, dv_ref, db_ref, dsk_ref, sk_acc):
        n = pl.program_id(0)
        lane = lax.broadcasted_iota(jnp.int32, (1, BLK), 1)
        lo = lane < 64
        kb = jnp.concatenate([kp_ref[...], kc_ref[...]], axis=0)
        vb = jnp.concatenate([vp_ref[...], vc_ref[...]], axis=0)
        kbs = (kb, pltpu.roll(kb, 64, 1))
        vbs = (vb, pltpu.roll(vb, 64, 1))
        mask = _swa_mask(n)

        @pl.when(n == 0)
        def _():
            db_ref[...] = dbi_ref[...]
            sk_acc[...] = jnp.zeros_like(sk_acc)

        dqs = []
        dk = jnp.zeros((2 * BLK, BLK), F32)
        dv = jnp.zeros((2 * BLK, BLK), F32)
        for pr in range(NH // 2):
            qp = q_ref[:, pr * BLK:(pr + 1) * BLK]
            dop = do_ref[:, pr * BLK:(pr + 1) * BLK].astype(F32)
            kv = pr // 2
            dq = jnp.zeros((BLK, BLK), F32)
            for e in range(2):
                h = 2 * pr + e
                lm = lo if e == 0 else jnp.logical_not(lo)
                sw = 0 if kv == e else 1
                qm = jnp.where(lm, qp, 0.0)
                dom = jnp.where(lm, dop, 0.0)
                pn, ps = _swa_probs(qm.astype(BF16), kbs[sw].astype(BF16), mask, b_ref[h], sk_ref[h])
                dp = _dot(dom.astype(BF16), vbs[sw].astype(BF16), 1, 1)
                delta = jnp.sum(pn * dp, axis=-1, keepdims=True)
                ds = pn * (dp - delta)
                db_ref[h] += ds
                sk_acc[...] += jnp.where(lane == h, -(ps * delta), 0.0)
                dsb = (ds * SCALE).astype(BF16)
                dq = dq + _dot(dsb, jnp.where(lm, kbs[sw], 0.0).astype(BF16), 1, 0)
                qk = qm if sw == 0 else pltpu.roll(qm, 64, 1)
                dok = dom if sw == 0 else pltpu.roll(dom, 64, 1)
                dk = dk + _dot(dsb, qk.astype(BF16), 0, 0)
                dv = dv + _dot(pn.astype(BF16), dok.astype(BF16), 0, 0)
            dqs.append(dq)
        dq_ref[...] = jnp.concatenate(dqs, axis=1).astype(BF16)
        dk_ref[0] = dk
        dv_ref[0] = dv

        @pl.when(n == nb - 1)
        def _():
            dsk_ref[...] = jnp.sum(sk_acc[...], axis=0, keepdims=True)

    full_b = pl.BlockSpec((NH, BLK, 2 * BLK), lambda n: (0, 0, 0))
    band = pl.BlockSpec((1, 2 * BLK, BLK), lambda n: (n, 0, 0))
    return pl.pallas_call(
        body, grid=(nb,),
        in_specs=[SMEM] + _swa_specs(512) + [full_b, pl.BlockSpec((BLK, 512), lambda n: (n, 0)), full_b],
        out_specs=[pl.BlockSpec((BLK, 512), lambda n: (n, 0)), band, band, full_b, pl.BlockSpec((1, BLK), lambda n: (0, 0))],
        out_shape=[jax.ShapeDtypeStruct((T, 512), BF16), jax.ShapeDtypeStruct((nb, 2 * BLK, BLK), F32),
                   jax.ShapeDtypeStruct((nb, 2 * BLK, BLK), F32), jax.ShapeDtypeStruct((NH, BLK, 2 * BLK), F32),
                   jax.ShapeDtypeStruct((1, BLK), F32)],
        scratch_shapes=[pltpu.VMEM((BLK, BLK), F32)],
        compiler_params=_cp("arbitrary"), name=name)(sinks, proj, proj, proj, proj, proj, bias, do, dbias_in)


def _band_fold(dkb, dvb, name):
    nb = dkb.shape[0]

    def body(ko_ref, kn_ref, vo_ref, vn_ref, dk_ref, dv_ref):
        last = pl.program_id(0) == nb - 1
        dk_ref[...] = (ko_ref[0] + jnp.where(last, 0.0, kn_ref[0])).astype(BF16)
        dv_ref[...] = (vo_ref[0] + jnp.where(last, 0.0, vn_ref[0])).astype(BF16)

    own = pl.BlockSpec((1, BLK, BLK), lambda j: (j, 1, 0))
    nxt = pl.BlockSpec((1, BLK, BLK), lambda j: (jnp.minimum(j + 1, nb - 1), 0, 0))
    out = pl.BlockSpec((BLK, BLK), lambda j: (j, 0))
    return pl.pallas_call(
        body, grid=(nb,), in_specs=[own, nxt, own, nxt], out_specs=[out, out],
        out_shape=[jax.ShapeDtypeStruct((nb * BLK, BLK), BF16)] * 2,
        compiler_params=_cp("parallel"), name=name)(dkb, dkb, dvb, dvb)


def _token_major(x, width):
    full = jnp.concatenate([x, jnp.zeros((BLK - NH, BLK), F32)], axis=0).T
    return full if width == BLK else jnp.concatenate([full, jnp.zeros((BLK, width - BLK), F32)], axis=1)


def _cum_fwd(proj, fb, name):
    T = proj.shape[0]
    tr = _rt(T)

    def body(z_ref, fb_ref, c_ref, carry):
        g = pl.program_id(0)
        lane = lax.broadcasted_iota(jnp.int32, (NH, BLK), 1)

        @pl.when(g == 0)
        def _():
            carry[...] = jnp.zeros_like(carry)

        run = carry[...]
        for sb in range(tr // BLK):
            r = slice(sb * BLK, (sb + 1) * BLK)
            z = z_ref[r, :].T[0:NH, :] + fb_ref[...]
            x = jnp.where(g * tr + sb * BLK + lane >= NPAD, _log_sigmoid(z), 0.0)
            s = 1
            while s < BLK:
                x = x + jnp.where(lane >= s, pltpu.roll(x, s, 1), 0.0)
                s *= 2
            x = x + run
            run = jnp.zeros((NH, BLK), F32) + jnp.sum(jnp.where(lane == BLK - 1, x, 0.0), axis=-1, keepdims=True)
            c_ref[r, :] = _token_major(x, BLK)
        carry[...] = run

    return pl.pallas_call(
        body, grid=(T // tr,),
        in_specs=[pl.BlockSpec((tr, BLK), lambda g: (g, FL // BLK)), pl.BlockSpec((NH, 1), lambda g: (0, 0))],
        out_specs=pl.BlockSpec((tr, BLK), lambda g: (g, 0)), out_shape=jax.ShapeDtypeStruct((T, BLK), F32),
        scratch_shapes=[pltpu.VMEM((NH, BLK), F32)], compiler_params=_cp("arbitrary"), name=name)(proj, fb)


def _cum_bwd(dqx, dkx, proj, fb, name):
    T = proj.shape[0]
    tr = _rt(T)
    nb = T // tr

    def body(dq_ref, dk_ref, z_ref, fb_ref, dz_ref, db_ref, carry):
        k = pl.program_id(0)
        g = nb - 1 - k
        lane = lax.broadcasted_iota(jnp.int32, (NH, BLK), 1)

        @pl.when(k == 0)
        def _():
            carry[...] = jnp.zeros_like(carry)
            db_ref[...] = jnp.zeros_like(db_ref)

        def picked(ref, r, r_first, r_second):
            rows = []
            for p in range(NH // 2):
                t_ = ref[r, p * BLK:(p + 1) * BLK].T
                rows += [t_[r_first:r_first + 1, :], t_[r_second:r_second + 1, :]]
            return jnp.concatenate(rows, axis=0)

        run, tot = carry[...], jnp.zeros((NH, 1), F32)
        for sb in reversed(range(tr // BLK)):
            r = slice(sb * BLK, (sb + 1) * BLK)
            x = picked(dq_ref, r, 64, 0) - picked(dk_ref, r, 67, 3)
            s = 1
            while s < BLK:
                x = x + jnp.where(lane < BLK - s, pltpu.roll(x, BLK - s, 1), 0.0)
                s *= 2
            x = x + run
            run = jnp.zeros((NH, BLK), F32) + jnp.sum(jnp.where(lane == 0, x, 0.0), axis=-1, keepdims=True)
            z = z_ref[r, :].T[0:NH, :] + fb_ref[...]
            dz = jnp.where(g * tr + sb * BLK + lane >= NPAD, x * _sigmoid(-z), 0.0)
            tot = tot + jnp.sum(dz, axis=-1, keepdims=True)
            dz_ref[r, :] = _token_major(dz, 2 * BLK).astype(BF16)
        carry[...] = run
        db_ref[...] += tot

    rev = lambda k: nb - 1 - k
    wide = pl.BlockSpec((tr, 512), lambda k: (rev(k), 0))
    return pl.pallas_call(
        body, grid=(nb,),
        in_specs=[wide, wide, pl.BlockSpec((tr, BLK), lambda k: (rev(k), FL // BLK)), pl.BlockSpec((NH, 1), lambda k: (0, 0))],
        out_specs=[pl.BlockSpec((tr, 2 * BLK), lambda k: (rev(k), 0)), pl.BlockSpec((NH, BLK), lambda k: (0, 0))],
        out_shape=[jax.ShapeDtypeStruct((T, 2 * BLK), BF16), jax.ShapeDtypeStruct((NH, BLK), F32)],
        scratch_shapes=[pltpu.VMEM((NH, BLK), F32)], compiler_params=_cp("arbitrary"), name=name)(dqx, dkx, proj, fb)


def _fox_prep(proj, ccol, name):
    T = proj.shape[0]
    tr = _pick(T, (1408, 384, 128))

    def body(q_ref, k_ref, v_ref, cc_ref, qa_ref, ka_ref, kt_ref, vmt_ref, vo_ref):
        h = pl.program_id(1)
        lane = lax.broadcasted_iota(jnp.int32, (1, BLK), 1)
        own = (lane >> 6) == (h & 1)
        a0 = 64 * (1 - (h & 1))
        c = _lane_pick(cc_ref[...], lane, h)
        hi = c.astype(BF16).astype(F32)
        mid = (c - hi).astype(BF16).astype(F32)
        lo = (c - hi - mid).astype(BF16).astype(F32)
        ones = (lane >= a0 + 3) & (lane < a0 + 6)
        qa = jnp.where(own, q_ref[...] * SCALE, jnp.where(ones, 1.0, 0.0))
        qa = jnp.where(lane == a0, hi, jnp.where(lane == a0 + 1, mid, jnp.where(lane == a0 + 2, lo, qa)))
        ones = (lane >= a0) & (lane < a0 + 3)
        ka = jnp.where(own, k_ref[...], jnp.where(ones, 1.0, 0.0))
        ka = jnp.where(lane == a0 + 3, -hi, jnp.where(lane == a0 + 4, -mid, jnp.where(lane == a0 + 5, -lo, ka)))
        qa_ref[...] = qa.astype(BF16)
        kab = ka.astype(BF16)
        ka_ref[...] = kab
        kt_ref[...] = kab.T
        vm = jnp.where(own, v_ref[...], 0.0)
        vmt_ref[...] = vm.astype(BF16).T
        vo_ref[...] = jnp.where(lane == a0, 1.0, vm).astype(BF16)

    pair = lambda col0: pl.BlockSpec((tr, BLK), lambda i, h: (i, col0 // BLK + (h >> 1)))
    out = pl.BlockSpec((None, tr, BLK), lambda i, h: (h, i, 0))
    out_t = pl.BlockSpec((None, BLK, tr), lambda i, h: (h, 0, i))
    tok = jax.ShapeDtypeStruct((NH, T, BLK), BF16)
    return pl.pallas_call(
        body, grid=(T // tr, NH), in_specs=[pair(QF), pair(KF), pair(VF), pl.BlockSpec((tr, BLK), lambda i, h: (i, 0))],
        out_specs=[out, out, out_t, out_t, out],
        out_shape=[tok, tok, jax.ShapeDtypeStruct((NH, BLK, T), BF16), jax.ShapeDtypeStruct((NH, BLK, T), BF16), tok],
        compiler_params=_cp("parallel", "arbitrary"), name=name)(proj, proj, proj, ccol)


def _fox_fwd(qaug, kaug_t, vo, name, gather=None):
    T = qaug.shape[1]
    t = _rt(T)
    nt = T // t
    ng = len(gather[0]) if gather else 0

    pairs = [(i, j) for i in range(nt) for j in range(i + 1)]
    i_of = jnp.asarray(np.array([p[0] for p in pairs], np.int32))
    j_of = jnp.asarray(np.array([p[1] for p in pairs], np.int32))
    ns = len(pairs)

    def body(i_ref, j_ref, q0, q1, k0, k1, v0, v1, *rest):
        o_ref, ot_ref, lse0_ref, lse1_ref = rest[ng:ng + 4]
        m_ref, acc_ref = rest[2 * ng + 4:2 * ng + 6]
        p_, s_ = pl.program_id(0), pl.program_id(1)
        i, j = i_ref[s_], j_ref[s_]
        lane = lax.broadcasted_iota(jnp.int32, (1, BLK), 1)
        lo = lane < 64
        if gather:
            start, finish = _gather_plan(rest[ng + 4:2 * ng + 4], gather[1], *rest[2 * ng + 6:])
            pl.when((p_ == 0) & (s_ == 0))(start)

        @pl.when(j == 0)
        def _():
            m_ref[...] = jnp.full_like(m_ref, NEG)
            acc_ref[...] = jnp.zeros_like(acc_ref)

        def step(masked):
            for e, (q_ref, k_ref, v_ref) in enumerate(((q0, k0, v0), (q1, k1, v1))):
                s = _dot(q_ref[...], k_ref[...], 1, 0)
                if masked:
                    s = jnp.where(_fox_mask(i, j, t), s, NEG)
                m_old = m_ref[e]
                m_new = jnp.maximum(m_old, jnp.max(s, axis=-1, keepdims=True))
                m_ref[e] = m_new
                pe = jnp.exp(s - jnp.concatenate([m_new] * (t // BLK), axis=1))
                acc_ref[e] = jnp.exp(m_old - m_new) * acc_ref[e] + _dot(pe.astype(BF16), v_ref[...], 1, 0)

        pl.when((j < i) & (j > 0))(lambda: step(False))
        pl.when((j == i) | ((j == 0) & (i > 0)))(lambda: step(True))

        @pl.when(j == i)
        def _():
            rows = i * t + lax.broadcasted_iota(jnp.int32, (t, 1), 0)
            l0, l1 = _lane_pick(acc_ref[0], lane, 64), _lane_pick(acc_ref[1], lane, 0)
            o = jnp.where(rows >= NPAD, jnp.where(lo, acc_ref[0] / l0, acc_ref[1] / l1), 0.0).astype(BF16)
            o_ref[...] = o
            ot_ref[...] = o.T
            lse0_ref[...] = m_ref[0] + jnp.log(l0)
            lse1_ref[...] = m_ref[1] + jnp.log(l1)

        if gather:
            pl.when((p_ == NH // 2 - 1) & (s_ == ns - 1))(finish)

    qs = lambda e: pl.BlockSpec((None, t, BLK), lambda p, s, ii, jj: (2 * p + e, ii[s], 0))
    ks = lambda e: pl.BlockSpec((None, t, BLK), lambda p, s, ii, jj: (2 * p + e, jj[s], 0))
    kts = lambda e: pl.BlockSpec((None, BLK, t), lambda p, s, ii, jj: (2 * p + e, 0, jj[s]))
    rep = pl.BlockSpec((None, t, BLK), lambda p, s, ii, jj: (p, ii[s], 0))
    bufs = list(gather[0]) if gather else []
    return pl.pallas_call(
        body,
        grid_spec=pltpu.PrefetchScalarGridSpec(
            num_scalar_prefetch=2, grid=(NH // 2, ns),
            in_specs=[qs(0), qs(1), kts(0), kts(1), ks(0), ks(1)] + [ANY] * ng,
            out_specs=[pl.BlockSpec((t, BLK), lambda p, s, ii, jj: (ii[s], p)),
                       pl.BlockSpec((BLK, t), lambda p, s, ii, jj: (p, ii[s])), rep, rep] + [ANY] * ng,
            scratch_shapes=[pltpu.VMEM((2, t, BLK), F32), pltpu.VMEM((2, t, BLK), F32)]
            + ([pltpu.SemaphoreType.DMA((6 * ng,)), pltpu.SemaphoreType.DMA((6 * ng,))] if gather else [])),
        out_shape=[jax.ShapeDtypeStruct((T, 512), BF16), jax.ShapeDtypeStruct((512, T), BF16)]
        + [jax.ShapeDtypeStruct((NH // 2, T, BLK), F32)] * 2 + [jax.ShapeDtypeStruct(b.shape, b.dtype) for b in bufs],
        input_output_aliases={8 + g: 4 + g for g in range(ng)},
        compiler_params=(pltpu.CompilerParams(dimension_semantics=("arbitrary",) * 2, vmem_limit_bytes=VMEM_LIMIT,
                                              has_side_effects=True) if gather
                         else _cp("parallel", "arbitrary")), name=name)(i_of, j_of, qaug, qaug, kaug_t, kaug_t, vo, vo, *bufs)


def _fox_delta(do, o, name):
    T = do.shape[0]
    tr = _rt(T)

    def body(do_ref, o_ref, d0_ref, d1_ref):
        lo = lax.broadcasted_iota(jnp.int32, (1, BLK), 1) < 64
        prod = do_ref[...].astype(F32) * o_ref[...].astype(F32)
        d0_ref[...] = jnp.zeros((tr, BLK), F32) + jnp.sum(jnp.where(lo, prod, 0.0), axis=-1, keepdims=True)
        d1_ref[...] = jnp.zeros((tr, BLK), F32) + jnp.sum(jnp.where(lo, 0.0, prod), axis=-1, keepdims=True)

    blk = pl.BlockSpec((tr, BLK), lambda i, p: (i, p))
    rep = pl.BlockSpec((None, tr, BLK), lambda i, p: (p, i, 0))
    return pl.pallas_call(
        body, grid=(T // tr, NH // 2), in_specs=[blk, blk], out_specs=[rep, rep],
        out_shape=[jax.ShapeDtypeStruct((NH // 2, T, BLK), F32)] * 2,
        compiler_params=_cp("parallel", "parallel"), name=name)(do, o)


def _fox_bwd(qaug, kaug, kaug_t, vm_t, do, lses, deltas, name, scatter=None):
    T = qaug.shape[1]
    t = _rt(T)
    nt = T // t
    ng = len(scatter[0]) if scatter else 0
    pairs = [(i, j) for j in range(nt) for i in range(j, nt)]
    i_of = jnp.asarray(np.array([p[0] for p in pairs], np.int32))
    j_of = jnp.asarray(np.array([p[1] for p in pairs], np.int32))
    ns = len(pairs)

    def body(i_ref, j_ref, q0, q1, k0, k1, kt0, kt1, v0, v1, do_ref, lse0, lse1, dl0, dl1, *rest):
        dq_ref, dqx_ref, dk_ref, dv_ref, dkx_ref = rest[ng:ng + 5]
        dq_acc, dk_acc, dv_acc = rest[2 * ng + 5:2 * ng + 8]
        p_, s_ = pl.program_id(0), pl.program_id(1)
        i, j = i_ref[s_], j_ref[s_]
        lane = lax.broadcasted_iota(jnp.int32, (1, BLK), 1)
        lo = lane < 64
        if scatter:
            start, finish = _scatter_plan(rest[:ng], rest[ng + 5:2 * ng + 5], scatter[1], *rest[2 * ng + 8:])
            pl.when((p_ == 0) & (s_ == 0))(start)

        @pl.when(s_ == 0)
        def _():
            dq_acc[...] = jnp.zeros_like(dq_acc)

        @pl.when(i == j)
        def _():
            dk_acc[...] = jnp.zeros_like(dk_acc)
            dv_acc[...] = jnp.zeros_like(dv_acc)

        def step(masked):
            dob = do_ref[...]
            rows = pl.ds(pl.multiple_of(i * t, t), t)
            wide = lambda ref: jnp.concatenate([ref[...]] * (t // BLK), axis=1)
            for e, (q_ref, k_ref, kt_ref, v_ref, lse_ref, dl_ref) in enumerate(
                    ((q0, k0, kt0, v0, lse0, dl0), (q1, k1, kt1, v1, lse1, dl1))):
                s = _dot(q_ref[...], kt_ref[...], 1, 0)
                if masked:
                    s = jnp.where(_fox_mask(i, j, t), s, NEG)
                pe = jnp.exp(s - wide(lse_ref))
                dp = _dot(dob, v_ref[...], 1, 0)
                ds = (pe * (dp - wide(dl_ref))).astype(BF16)
                dq_acc[e, rows, :] += _dot(ds, k_ref[...], 1, 0)
                dk_acc[e] += _dot(ds, q_ref[...], 0, 0)
                dv_acc[e] += _dot(pe.astype(BF16), dob, 0, 0)

        pl.when((i > j) & (j > 0))(lambda: step(False))
        pl.when((i == j) | ((j == 0) & (i > 0)))(lambda: step(True))

        @pl.when(i == nt - 1)
        def _():
            dk_ref[...] = jnp.where(lo, dk_acc[0], dk_acc[1]).astype(BF16)
            dv_ref[...] = jnp.where(lo, dv_acc[0], dv_acc[1]).astype(BF16)
            dkx_ref[...] = jnp.where(lo, dk_acc[1], dk_acc[0])

        @pl.when(s_ == ns - 1)
        def _():
            dq_ref[...] = (jnp.where(lo, dq_acc[0], dq_acc[1]) * SCALE).astype(BF16)
            dqx_ref[...] = jnp.where(lo, dq_acc[1], dq_acc[0])

        if scatter:
            pl.when((p_ == NH // 2 - 1) & (s_ == ns - 1))(finish)

    qs = lambda e: pl.BlockSpec((None, t, BLK), lambda p, s, ii, jj: (2 * p + e, ii[s], 0))
    ks = lambda e: pl.BlockSpec((None, t, BLK), lambda p, s, ii, jj: (2 * p + e, jj[s], 0))
    kts = lambda e: pl.BlockSpec((None, BLK, t), lambda p, s, ii, jj: (2 * p + e, 0, jj[s]))
    qside = pl.BlockSpec((t, BLK), lambda p, s, ii, jj: (ii[s], p))
    kside = pl.BlockSpec((t, BLK), lambda p, s, ii, jj: (jj[s], p))
    rep = pl.BlockSpec((None, t, BLK), lambda p, s, ii, jj: (p, ii[s], 0))
    whole = pl.BlockSpec((T, BLK), lambda p, s, ii, jj: (0, p))
    sums = list(scatter[0]) if scatter else []
    return pl.pallas_call(
        body,
        grid_spec=pltpu.PrefetchScalarGridSpec(
            num_scalar_prefetch=2, grid=(NH // 2, ns),
            in_specs=[qs(0), qs(1), ks(0), ks(1), kts(0), kts(1), kts(0), kts(1), qside, rep, rep, rep, rep] + [ANY] * ng,
            out_specs=[whole, whole, kside, kside, kside] + [ANY] * ng,
            scratch_shapes=[pltpu.VMEM((2, T, BLK), F32), pltpu.VMEM((2, t, BLK), F32), pltpu.VMEM((2, t, BLK), F32)]
            + ([pltpu.SemaphoreType.DMA((3 * ng,)), pltpu.SemaphoreType.DMA((3 * ng,))] if scatter else [])),
        out_shape=[jax.ShapeDtypeStruct((T, 512), BF16), jax.ShapeDtypeStruct((T, 512), F32),
                   jax.ShapeDtypeStruct((T, 512), BF16), jax.ShapeDtypeStruct((T, 512), BF16),
                   jax.ShapeDtypeStruct((T, 512), F32)] + (_scatter_shapes(sums, scatter[1]) if scatter else []),
        compiler_params=(pltpu.CompilerParams(dimension_semantics=("arbitrary",) * 2, vmem_limit_bytes=VMEM_LIMIT,
                                              has_side_effects=True) if scatter
                         else _cp("parallel", "arbitrary")), name=name)(
            i_of, j_of, qaug, qaug, kaug, kaug, kaug_t, kaug_t, vm_t, vm_t, do, *lses, *deltas, *sums)


def _fox_mask(i, j, t):
    row = i * t + lax.broadcasted_iota(jnp.int32, (t, t), 0)
    col = j * t + lax.broadcasted_iota(jnp.int32, (t, t), 1)
    return (col <= row) & (col >= NPAD)


def _lane_pick(x, lane, idx):
    return jnp.sum(jnp.where(lane == idx, x, 0.0), axis=-1, keepdims=True)


def _lru_gates(xc, wr_ref, wi_ref, vec_ref):
    xb = xc.astype(BF16)
    pre_r = jnp.concatenate([_dot(xb[:, p * BLK:(p + 1) * BLK], wr_ref[p], 1, 0) for p in range(LW // BLK)], axis=1)
    pre_i = jnp.concatenate([_dot(xb[:, p * BLK:(p + 1) * BLK], wi_ref[p], 1, 0) for p in range(LW // BLK)], axis=1)
    r = _sigmoid(pre_r + vec_ref[0:1, :])
    gi = _sigmoid(pre_i + vec_ref[1:2, :])
    log_a = LRU_C * r * _log_sigmoid(vec_ref[2:3, :])
    a = jnp.exp(log_a)
    mult = jnp.sqrt(_neg_expm1(2.0 * log_a))
    return r, gi, a, mult


def _conv(xbuf_ref, x, cw_ref, vec_ref, tr):
    return (cw_ref[3:4, :] * x + cw_ref[2:3, :] * xbuf_ref[7:7 + tr, :] + cw_ref[1:2, :] * xbuf_ref[6:6 + tr, :]
            + cw_ref[0:1, :] * xbuf_ref[5:5 + tr, :] + vec_ref[3:4, :])


def _lru_fwd(proj, cw, wr, wi, vec, name):
    T = proj.shape[0]
    tr = _rt(T)

    def body(x_ref, y_ref, cw_ref, wr_ref, wi_ref, vec_ref, oc_ref, oct_ref, hs_ref, xbuf, abuf, bbuf, hcar):
        i = pl.program_id(0)

        @pl.when(i == 0)
        def _():
            xbuf[0:8, :] = jnp.zeros((8, LW), F32)
            hcar[...] = jnp.zeros_like(hcar)

        x = x_ref[...]
        xbuf[8:8 + tr, :] = x
        xc = _conv(xbuf, x, cw_ref, vec_ref, tr)
        xbuf[0:8, :] = x[tr - 8:tr, :]
        _, gi, a, mult = _lru_gates(xc, wr_ref, wi_ref, vec_ref)
        rows = i * tr + lax.broadcasted_iota(jnp.int32, (tr, 1), 0)
        abuf[...] = a
        bbuf[...] = jnp.where(rows >= NPAD, mult * (gi * xc), 0.0)
        sub = lax.broadcasted_iota(jnp.int32, (8, 1), 0)

        def step(k, h):
            sl = pl.ds(pl.multiple_of(k * 8, 8), 8)
            a8, b8 = abuf[sl, :], bbuf[sl, :]
            for s in (1, 2, 4):
                ok = sub >= s
                b8 = jnp.where(ok, a8 * pltpu.roll(b8, s, 0) + b8, b8)
                a8 = jnp.where(ok, a8 * pltpu.roll(a8, s, 0), a8)
            h8 = a8 * h + b8
            bbuf[sl, :] = h8
            return h8[7:8, :]

        hcar[...] = lax.fori_loop(0, tr // 8, step, hcar[...])
        hs = bbuf[...]
        hs_ref[...] = hs
        oc = (hs * _gelu(y_ref[...])).astype(BF16)
        oc_ref[...] = oc
        oct_ref[...] = oc.T

    row = pl.BlockSpec((tr, LW), lambda i: (i, 0))
    full = lambda shape: pl.BlockSpec(shape, lambda i: (0,) * len(shape))
    return pl.pallas_call(
        body, grid=(T // tr,),
        in_specs=[pl.BlockSpec((tr, LW), lambda i: (i, XC // LW)), pl.BlockSpec((tr, LW), lambda i: (i, YC // LW)),
                  full((4, LW)), full((4, BLK, BLK)), full((4, BLK, BLK)), full((8, LW))],
        out_specs=[row, pl.BlockSpec((LW, tr), lambda i: (0, i)), row],
        out_shape=[jax.ShapeDtypeStruct((T, LW), BF16), jax.ShapeDtypeStruct((LW, T), BF16), jax.ShapeDtypeStruct((T, LW), F32)],
        scratch_shapes=[pltpu.VMEM((tr + 8, LW), F32), pltpu.VMEM((tr, LW), F32), pltpu.VMEM((tr, LW), F32),
                        pltpu.VMEM((1, LW), F32)],
        compiler_params=_cp("arbitrary"), name=name)(proj, proj, cw, wr, wi, vec)


def _lru_bwd(proj, hs, doc, cw, wr, wi, vec, name):
    T = proj.shape[0]
    tr = _rt(T)
    nt = T // tr
    r8 = tr // 8

    def body(x_ref, xp_ref, y_ref, hs_ref, hp_ref, do_ref, cw_ref, wr_ref, wi_ref, vec_ref,
             dx_ref, dy_ref, dwr_ref, dwi_ref, dvec_ref, xbuf, abuf, gbuf, hbuf, dbuf, gcar, acar):
        k = pl.program_id(0)
        i = nt - 1 - k

        @pl.when(k == 0)
        def _():
            dwr_ref[...] = jnp.zeros_like(dwr_ref)
            dwi_ref[...] = jnp.zeros_like(dwi_ref)
            dvec_ref[...] = jnp.zeros_like(dvec_ref)
            gcar[...] = jnp.zeros_like(gcar)
            acar[...] = jnp.zeros_like(acar)
            dbuf[tr:tr + 8, :] = jnp.zeros((8, LW), F32)

        first = i == 0
        x = x_ref[...]
        xbuf[0:8, :] = jnp.where(first, 0.0, xp_ref[...])
        xbuf[8:8 + tr, :] = x
        xc = _conv(xbuf, x, cw_ref, vec_ref, tr)
        r, gi, a, mult = _lru_gates(xc, wr_ref, wi_ref, vec_ref)
        y = y_ref[...]
        hs = hs_ref[...]
        do_ = do_ref[...].astype(F32)
        rows = i * tr + lax.broadcasted_iota(jnp.int32, (tr, 1), 0)
        abuf[0:tr, :] = a
        abuf[tr:tr + 8, :] = jnp.zeros((8, LW), F32) + acar[...]
        an = abuf[1:1 + tr, :]
        acar[...] = a[0:1, :]
        abuf[0:tr, :] = an
        gbuf[...] = do_ * _gelu(y)
        sub = lax.broadcasted_iota(jnp.int32, (8, 1), 0)

        def step(kk, g):
            sl = pl.ds(pl.multiple_of((r8 - 1 - kk) * 8, 8), 8)
            a8, b8 = abuf[sl, :], gbuf[sl, :]
            for s in (1, 2, 4):
                ok = sub < 8 - s
                b8 = jnp.where(ok, a8 * pltpu.roll(b8, 8 - s, 0) + b8, b8)
                a8 = jnp.where(ok, a8 * pltpu.roll(a8, 8 - s, 0), a8)
            g8 = a8 * g + b8
            gbuf[sl, :] = g8
            return g8[0:1, :]

        gcar[...] = lax.fori_loop(0, r8, step, gcar[...])
        g = gbuf[...]
        hbuf[0:8, :] = jnp.where(first, 0.0, hp_ref[...])
        hbuf[8:8 + tr, :] = hs
        hprev = hbuf[7:7 + tr, :]
        dinp = jnp.where(rows >= NPAD, g, 0.0)
        da = g * hprev
        dmult = dinp * gi * xc
        dgi = dinp * mult * xc
        dxc = dinp * mult * gi
        dlog_a = da * a - dmult * a * a / mult
        ls = _log_sigmoid(vec_ref[2:3, :])
        dpre_r = dlog_a * (LRU_C * ls) * r * (1.0 - r)
        dpre_i = dgi * gi * (1.0 - gi)
        xb = xc.astype(BF16)
        rb, ib = dpre_r.astype(BF16), dpre_i.astype(BF16)
        back = []
        for p in range(LW // BLK):
            c = slice(p * BLK, (p + 1) * BLK)
            back.append(_dot(rb[:, c], wr_ref[p], 1, 1) + _dot(ib[:, c], wi_ref[p], 1, 1))
            dwr_ref[p] += _dot(xb[:, c], rb[:, c], 0, 0)
            dwi_ref[p] += _dot(xb[:, c], ib[:, c], 0, 0)
        dxc = dxc + jnp.concatenate(back, axis=1)
        col = lambda v: jnp.sum(v, axis=0, keepdims=True)
        dvec_ref[0:1, :] += col(dpre_r)
        dvec_ref[1:2, :] += col(dpre_i)
        dvec_ref[2:3, :] += col(dlog_a * (LRU_C * r)) * _sigmoid(-vec_ref[2:3, :])
        dvec_ref[3:4, :] += col(dxc)
        dvec_ref[4:5, :] += col(dxc * xbuf[5:5 + tr, :])
        dvec_ref[5:6, :] += col(dxc * xbuf[6:6 + tr, :])
        dvec_ref[6:7, :] += col(dxc * xbuf[7:7 + tr, :])
        dvec_ref[7:8, :] += col(dxc * x)
        dbuf[0:tr, :] = dxc
        dxr = (cw_ref[3:4, :] * dxc + cw_ref[2:3, :] * dbuf[1:1 + tr, :] + cw_ref[1:2, :] * dbuf[2:2 + tr, :]
               + cw_ref[0:1, :] * dbuf[3:3 + tr, :])
        dbuf[tr:tr + 8, :] = dxc[0:8, :]
        dx_ref[...] = jnp.where(rows >= NPAD, dxr, 0.0).astype(BF16)
        dy_ref[...] = (do_ * hs * _gelu_grad(y)).astype(BF16)

    rev = lambda k: nt - 1 - k
    row = lambda col0: pl.BlockSpec((tr, LW), lambda k: (rev(k), col0))
    prev8 = lambda col0: pl.BlockSpec((8, LW), lambda k: (jnp.maximum(rev(k) * r8 - 1, 0), col0))
    full = lambda shape: pl.BlockSpec(shape, lambda k: (0,) * len(shape))
    return pl.pallas_call(
        body, grid=(nt,),
        in_specs=[row(XC // LW), prev8(XC // LW), row(YC // LW), row(0), prev8(0), row(0),
                  full((4, LW)), full((4, BLK, BLK)), full((4, BLK, BLK)), full((8, LW))],
        out_specs=[row(0), row(0), full((4, BLK, BLK)), full((4, BLK, BLK)), full((8, LW))],
        out_shape=[jax.ShapeDtypeStruct((T, LW), BF16), jax.ShapeDtypeStruct((T, LW), BF16),
                   jax.ShapeDtypeStruct((4, BLK, BLK), F32), jax.ShapeDtypeStruct((4, BLK, BLK), F32),
                   jax.ShapeDtypeStruct((8, LW), F32)],
        scratch_shapes=[pltpu.VMEM((tr + 8, LW), F32), pltpu.VMEM((tr + 8, LW), F32), pltpu.VMEM((tr, LW), F32),
                        pltpu.VMEM((tr + 8, LW), F32), pltpu.VMEM((tr + 8, LW), F32),
                        pltpu.VMEM((1, LW), F32), pltpu.VMEM((1, LW), F32)],
        compiler_params=_cp("arbitrary"), name=name)(proj, proj, proj, hs, hs, doc, cw, wr, wi, vec)


def _branch_merge_fwd(oa, of, oc, wb, proj, name):
    T = proj.shape[0]
    tm, tn = _rt(T), 512

    def body(a0, a1, a2, w_ref, g0, g1, g2, r0, r1, r2, m_ref, mt_ref):
        acc = None
        for g, (a_ref, g_ref, r_ref) in enumerate(((a0, g0, r0), (a1, g1, r1), (a2, g2, r2))):
            b = _dot(a_ref[...], w_ref[g], 1, 0)
            r_ref[...] = b
            term = _sigmoid(g_ref[...]) * b
            acc = term if acc is None else acc + term
        m = acc.astype(BF16)
        m_ref[...] = m
        mt_ref[...] = m.T

    act = pl.BlockSpec((tm, LW), lambda j, i: (i, 0))
    gate = lambda g: pl.BlockSpec((tm, tn), lambda j, i: (i, (GT + g * D) // tn + j))
    blk = pl.BlockSpec((tm, tn), lambda j, i: (i, j))
    return pl.pallas_call(
        body, grid=(D // tn, T // tm),
        in_specs=[act, act, act, pl.BlockSpec((3, LW, tn), lambda j, i: (0, 0, j)), gate(0), gate(1), gate(2)],
        out_specs=[blk] * 4 + [pl.BlockSpec((tn, tm), lambda j, i: (j, i))],
        out_shape=[jax.ShapeDtypeStruct((T, D), F32)] * 3 + [jax.ShapeDtypeStruct((T, D), BF16), jax.ShapeDtypeStruct((D, T), BF16)],
        compiler_params=_cp("parallel", "parallel"), name=name)(oa, of, oc, wb, proj, proj, proj)


def _out_dx_merge_bwd(dhb, w_out, proj, b0, b1, b2, name):
    T = proj.shape[0]
    tm, tn = _rt(T), 512

    def body(dh_ref, w_ref, g0, g1, g2, r0, r1, r2, d0, d1, d2, e0, e1, e2):
        dmv = _dot(dh_ref[...], w_ref[...], 1, 1)
        for g_ref, r_ref, d_ref, e_ref in ((g0, r0, d0, e0), (g1, r1, d1, e1), (g2, r2, d2, e2)):
            sg = _sigmoid(g_ref[...])
            d_ref[...] = (dmv * sg).astype(BF16)
            e_ref[...] = (dmv * r_ref[...] * sg * (1.0 - sg)).astype(BF16)

    gate = lambda g: pl.BlockSpec((tm, tn), lambda j, i: (i, (GT + g * D) // tn + j))
    blk = pl.BlockSpec((tm, tn), lambda j, i: (i, j))
    return pl.pallas_call(
        body, grid=(D // tn, T // tm),
        in_specs=[pl.BlockSpec((tm, D), lambda j, i: (i, 0)), pl.BlockSpec((tn, D), lambda j, i: (j, 0)),
                  gate(0), gate(1), gate(2), blk, blk, blk],
        out_specs=[blk] * 6, out_shape=[jax.ShapeDtypeStruct((T, D), BF16)] * 6,
        compiler_params=_cp("parallel", "parallel"), name=name)(dhb, w_out, proj, proj, proj, b0, b1, b2)


def _ffn_in_swiglu_fwd(u, w, name):
    T = u.shape[0]
    tm, tn = _rt(T), _pick(DFF, (1408, 256))
    nj = DFF // tn

    def body(u_ref, wg_ref, wu_ref, g_ref, up_ref, a_ref, at_ref):
        ub = u_ref[...]
        g = _dot(ub, wg_ref[...], 1, 0)
        up = _dot(ub, wu_ref[...], 1, 0)
        g_ref[...] = g
        up_ref[...] = up
        a = (g * _sigmoid(g) * up).astype(BF16)
        a_ref[...] = a
        at_ref[...] = a.T

    blk = pl.BlockSpec((tm, tn), lambda j, i: (i, j))
    return pl.pallas_call(
        body, grid=(nj, T // tm),
        in_specs=[pl.BlockSpec((tm, D), lambda j, i: (i, 0)), pl.BlockSpec((D, tn), lambda j, i: (0, j)),
                  pl.BlockSpec((D, tn), lambda j, i: (0, j + nj))],
        out_specs=[blk] * 3 + [pl.BlockSpec((tn, tm), lambda j, i: (j, i))],
        out_shape=[jax.ShapeDtypeStruct((T, DFF), F32)] * 2 + [jax.ShapeDtypeStruct((T, DFF), BF16),
                                                               jax.ShapeDtypeStruct((DFF, T), BF16)],
        compiler_params=_cp("parallel", "parallel"), name=name)(u, w, w)


def _ffn_out_dx_swiglu_bwd(dhb, w, gate, up, name):
    T = dhb.shape[0]
    tm, tn = _rt(T), _pick(DFF, (1408, 256))

    def body(dh_ref, w_ref, g_ref, up_ref, dg_ref, du_ref):
        d = _dot(dh_ref[...], w_ref[...], 1, 1)
        g = g_ref[...]
        sg = _sigmoid(g)
        dg_ref[...] = (d * up_ref[...] * (sg + g * sg * (1.0 - sg))).astype(BF16)
        du_ref[...] = (d * g * sg).astype(BF16)

    blk = pl.BlockSpec((tm, tn), lambda j, i: (i, j))
    return pl.pallas_call(
        body, grid=(DFF // tn, T // tm),
        in_specs=[pl.BlockSpec((tm, D), lambda j, i: (i, 0)), pl.BlockSpec((tn, D), lambda j, i: (j, 0)), blk, blk],
        out_specs=[blk] * 2, out_shape=[jax.ShapeDtypeStruct((T, DFF), BF16)] * 2,
        compiler_params=_cp("parallel", "parallel"), name=name)(dhb, w, gate, up)


def _adamw(w, g, m, v, name):
    R, C = w.shape
    tr = _pick(R, tuple(t for t in (512, 256, 128, 64, 32, 16, 8) if t * C * 4 <= (3 << 19)))
    c1 = 1.0 - ADAM_B1 ** ADAM_STEP
    c2 = 1.0 - ADAM_B2 ** ADAM_STEP

    def body(w_ref, g_ref, m_ref, v_ref, d_ref, mo_ref, vo_ref):
        gv = g_ref[...]
        mn = ADAM_B1 * m_ref[...] + (1.0 - ADAM_B1) * gv
        vn = ADAM_B2 * v_ref[...] + (1.0 - ADAM_B2) * (gv * gv)
        d_ref[...] = -ADAM_LR * ((mn / c1) / (jnp.sqrt(vn / c2) + ADAM_EPS) + ADAM_WD * w_ref[...])
        mo_ref[...] = mn
        vo_ref[...] = vn

    blk = pl.BlockSpec((tr, C), lambda i: (i, 0))
    return pl.pallas_call(
        body, grid=(R // tr,), in_specs=[blk] * 4, out_specs=[blk] * 3,
        out_shape=[jax.ShapeDtypeStruct((R, C), F32)] * 3, compiler_params=_cp("parallel"), name=name)(w, g, m, v)


def _sum_lead(x, name):
    n, R, C = x.shape
    tr = _pick(R, (512, 256, 128, 64, 32, 16, 8))

    def body(x_ref, o_ref):
        acc = x_ref[0]
        for d in range(1, n):
            acc = acc + x_ref[d]
        o_ref[...] = acc

    return pl.pallas_call(
        body, grid=(R // tr,), in_specs=[pl.BlockSpec((n, tr, C), lambda i: (0, i, 0))],
        out_specs=pl.BlockSpec((tr, C), lambda i: (i, 0)), out_shape=jax.ShapeDtypeStruct((R, C), F32),
        compiler_params=_cp("parallel"), name=name)(x)


def _here():
    return lax.axis_index("x"), lax.axis_index("y"), lax.axis_index("c")


def _rcopy(src, dst, send_sems, recv_sems, k, to):
    return pltpu.make_async_remote_copy(src_ref=src, dst_ref=dst, send_sem=send_sems.at[k], recv_sem=recv_sems.at[k],
                                        device_id=to, device_id_type=MESH)


def _hbm_calls(body, args, out_shapes, n_sems, aliases, name):
    return pl.pallas_call(
        body, in_specs=[ANY] * len(args), out_specs=[ANY] * len(out_shapes), out_shape=out_shapes,
        input_output_aliases=aliases,
        scratch_shapes=[pltpu.SemaphoreType.DMA((n_sems,)), pltpu.SemaphoreType.DMA((n_sems,))],
        compiler_params=pltpu.CompilerParams(has_side_effects=True), name=name)(*args)


def _gather_plan(outs, axes, send_sems, recv_sems):
    x, y, c = _here()
    sib = (x, y, 1 - c)
    chips = [(1 - x, y), (x, 1 - y), (1 - x, 1 - y)]
    todo = [(t, k, chip) for t in range(len(outs)) for k, chip in enumerate(chips)]

    def win(t, chip, hc):
        o, ax = outs[t], axes[t]
        w = o.shape[ax] // N_SHARD
        first = (2 * chip[0] + chip[1]) * w
        if ax == 0:
            return o.at[pl.ds(pl.multiple_of(first + hc * (w // 2), 16), w // 2), :]
        rows = o.shape[0] // 2
        return o.at[pl.ds(pl.multiple_of(hc * rows, 16), rows), pl.ds(pl.multiple_of(first, BLK), w)]

    def copy(t, k, chip, hc, to):
        return _rcopy(win(t, chip, hc), win(t, chip, hc), send_sems, recv_sems, 6 * t + k, to)

    def start():
        for t, k, chip in todo:
            copy(t, k, (x, y), c, (*chip, c)).start()

    def finish():
        for t, k, chip in todo:
            copy(t, k, chip, c, (*chip, c)).wait_recv()
            copy(t, 3 + k, chip, c, sib).start()
        for t, k, chip in todo:
            copy(t, 3 + k, chip, 1 - c, sib).wait_recv()
        for t, k, chip in todo:
            copy(t, k, (x, y), c, (*chip, c)).wait_send()
            copy(t, 3 + k, chip, c, sib).wait_send()

    return start, finish


def _all_gather_weights(fulls, axes, name):
    nt = len(fulls)

    def body(*refs):
        start, finish = _gather_plan(refs[nt:2 * nt], axes, *refs[2 * nt:])
        start()
        finish()

    return _hbm_calls(body, fulls, [jax.ShapeDtypeStruct(f.shape, f.dtype) for f in fulls], 6 * nt,
                      {t: t for t in range(nt)}, name)


def _half(ref, ax, hc):
    n = ref.shape[ax] // 2
    sl = pl.ds(pl.multiple_of(hc * n, 8), n)
    return ref.at[sl, :] if ax == 0 else ref.at[:, sl]


def _shrunk(shape, ax, by):
    shape = list(shape)
    shape[ax] //= by
    return tuple(shape)


def _swap_halves(gs, haxes, name):
    nt = len(gs)

    def body(*refs):
        ins, outs, (send_sems, recv_sems) = refs[:nt], refs[nt:2 * nt], refs[2 * nt:]
        x, y, c = _here()
        cps = [_rcopy(_half(g, ax, 1 - c), o, send_sems, recv_sems, t, (x, y, 1 - c))
               for t, (g, o, ax) in enumerate(zip(ins, outs, haxes))]
        for cp in cps:
            cp.start()
        for cp in cps:
            cp.wait()

    return _hbm_calls(body, gs, [jax.ShapeDtypeStruct(_shrunk(g.shape, ax, 2), g.dtype) for g, ax in zip(gs, haxes)],
                      nt, {}, name)


def _scatter_plan(ins, outs, saxes, send_sems, recv_sems):
    x, y, c = _here()
    chips = [(1 - x, y), (x, 1 - y), (1 - x, 1 - y)]

    def copies():
        cps = []
        for t, (s, o, ax) in enumerate(zip(ins, outs, saxes)):
            w = s.shape[ax] // N_SHARD
            for k, chip in enumerate(chips):
                first = pl.multiple_of((2 * chip[0] + chip[1]) * w, 8)
                src = s.at[pl.ds(first, w), :] if ax == 0 else s.at[:, pl.ds(first, w)]
                cps.append(_rcopy(src, o.at[k], send_sems, recv_sems, 3 * t + k, (*chip, c)))
        return cps

    def start():
        for cp in copies():
            cp.start()

    def finish():
        for cp in copies():
            cp.wait()

    return start, finish


def _scatter_shapes(sbs, saxes):
    return [jax.ShapeDtypeStruct((3,) + _shrunk(s.shape, ax, N_SHARD), s.dtype) for s, ax in zip(sbs, saxes)]


def _small_gather_copies(out_ref, send_sems, recv_sems, k0):
    x, y, c = _here()
    me = 4 * x + 2 * y + c
    cps = []
    for k in range(1, 8):
        to = (x ^ ((k >> 2) & 1), y ^ ((k >> 1) & 1), c ^ (k & 1))
        peer = 4 * to[0] + 2 * to[1] + to[2]
        cps.append((_rcopy(out_ref.at[me], out_ref.at[me], send_sems, recv_sems, k0 + k - 1, to),
                    _rcopy(out_ref.at[peer], out_ref.at[peer], send_sems, recv_sems, k0 + k - 1, to)))
    return cps


def _scatter_to_chips(sbs, saxes, small, name):
    nt = len(sbs)

    def body(*refs):
        small_out = refs[2 * nt + 1]
        send_sems, recv_sems = refs[2 * nt + 2:]
        start, finish = _scatter_plan(refs[:nt], refs[nt + 1:2 * nt + 1], saxes, send_sems, recv_sems)
        start()
        cps = _small_gather_copies(small_out, send_sems, recv_sems, 3 * nt)
        for snd, _ in cps:
            snd.start()
        for _, rcv in cps:
            rcv.wait_recv()
        for snd, _ in cps:
            snd.wait_send()
        finish()

    outs = _hbm_calls(body, list(sbs) + [small], _scatter_shapes(sbs, saxes) + [jax.ShapeDtypeStruct(small.shape, small.dtype)],
                      3 * nt + 7, {nt: nt}, name)
    return outs[:nt], outs[nt]


def _join_halves(fins, haxes, name):
    nt = len(fins)

    def body(*refs):
        outs, (send_sems, recv_sems) = refs[nt:2 * nt], refs[2 * nt:]
        x, y, c = _here()
        cps = [_rcopy(_half(o, ax, c), _half(o, ax, c), send_sems, recv_sems, t, (x, y, 1 - c))
               for t, (o, ax) in enumerate(zip(outs, haxes))]
        for cp in cps:
            cp.start()
        for t, (o, ax) in enumerate(zip(outs, haxes)):
            _rcopy(_half(o, ax, 1 - c), _half(o, ax, 1 - c), send_sems, recv_sems, t, (x, y, 1 - c)).wait_recv()
        for cp in cps:
            cp.wait_send()

    return _hbm_calls(body, fins, [jax.ShapeDtypeStruct(f.shape, f.dtype) for f in fins], nt, {t: t for t in range(nt)}, name)


def _all_gather_small(buf, name):
    def body(_, out_ref, send_sems, recv_sems):
        cps = _small_gather_copies(out_ref, send_sems, recv_sems, 0)
        for snd, _ in cps:
            snd.start()
        for _, rcv in cps:
            rcv.wait_recv()
        for snd, _ in cps:
            snd.wait_send()

    return _hbm_calls(body, [buf], [jax.ShapeDtypeStruct(buf.shape, buf.dtype)], 7, {0: 0}, name)[0]


def _place(block, n, index):
    buf = jnp.zeros((n,) + block.shape[1:], block.dtype)
    return lax.dynamic_update_slice_in_dim(buf, block, index, axis=0)


def _add_half(g, other, hax, cidx, name):
    r, cw = other.shape
    tr = _pick(r, tuple(t for t in (512, 256, 128, 64, 32, 16, 8) if t * cw * 4 <= (1 << 21)))
    nr = r // tr

    def body(c_ref, g_ref, o_ref, s_ref, sb_ref):
        s = g_ref[...] + o_ref[...]
        s_ref[...] = s
        sb_ref[...] = s.astype(BF16)

    g_map = (lambda i, c: (c[0] * nr + i, 0)) if hax == 0 else (lambda i, c: (i, c[0]))
    blk = pl.BlockSpec((tr, cw), lambda i, c: (i, 0))
    return pl.pallas_call(
        body,
        grid_spec=pltpu.PrefetchScalarGridSpec(
            num_scalar_prefetch=1, grid=(nr,), in_specs=[pl.BlockSpec((tr, cw), g_map), blk], out_specs=[blk, blk]),
        out_shape=[jax.ShapeDtypeStruct((r, cw), F32), jax.ShapeDtypeStruct((r, cw), BF16)],
        compiler_params=_cp("parallel"), name=name)(cidx, g, other)


def _add_chips(s, recv, sax, chip_idx, cidx, name):
    _, r, cw = recv.shape
    tr = _pick(r, tuple(t for t in (512, 352, 256, 128, 64, 32, 16, 8) if t * cw * 4 <= (1 << 21)))
    nr = r // tr

    def body(chip_ref, c_ref, s_ref, r_ref, out_ref):
        out_ref[...] = ((s_ref[...] + r_ref[0].astype(F32)) + r_ref[1].astype(F32)) + r_ref[2].astype(F32)

    if sax == 0:
        s_map, o_map, shape = (lambda i, chip, c: (chip[0] * nr + i, 0)), (lambda i, chip, c: (i, c[0])), (r, 2 * cw)
    else:
        s_map, o_map, shape = (lambda i, chip, c: (i, chip[0])), (lambda i, chip, c: (c[0] * nr + i, 0)), (2 * r, cw)
    return pl.pallas_call(
        body,
        grid_spec=pltpu.PrefetchScalarGridSpec(
            num_scalar_prefetch=2, grid=(nr,),
            in_specs=[pl.BlockSpec((tr, cw), s_map), pl.BlockSpec((3, tr, cw), lambda i, chip, c: (0, i, 0))],
            out_specs=pl.BlockSpec((tr, cw), o_map)),
        out_shape=jax.ShapeDtypeStruct(shape, F32), compiler_params=_cp("parallel"), name=name)(chip_idx, cidx, s, recv)


IN_SHARD = IN_COLS // N_SHARD
IN_SLOT = INP // N_SHARD
IN_PIECES = ((0, 512, QA), (512, 640, KA), (640, 768, VA), (768, 1280, QF), (1280, 1792, KF), (1792, 2304, VF),
             (2304, 2312, FL), (2312, 2824, XC), (2824, 3336, YC), (3336, 6408, GT))


def _gathered_to_kernel_cols(w):
    parts, pos = [], 0
    for a, b, k in sorted(IN_PIECES, key=lambda p: p[2]):
        assert k == pos
        while a < b:
            j = a // IN_SHARD
            e = min(b, (j + 1) * IN_SHARD)
            g = j * IN_SLOT + a - j * IN_SHARD
            parts.append(w[..., g:g + e - a])
            pos += e - a
            a = e
    parts.append(jnp.zeros(w.shape[:-1] + (INP - pos,), w.dtype))
    return jnp.concatenate(parts, axis=-1)


def _kernel_to_gathered_cols(w):
    parts = []
    for j in range(N_SHARD):
        lo, hi = j * IN_SHARD, (j + 1) * IN_SHARD
        for a, b, k in IN_PIECES:
            s, e = max(a, lo), min(b, hi)
            if s < e:
                parts.append(w[..., k + s - a:k + e - a])
        parts.append(jnp.zeros(w.shape[:-1] + (IN_SLOT - IN_SHARD,), w.dtype))
    return jnp.concatenate(parts, axis=-1)


def _pair_blocks(w):
    z = jnp.zeros((4, 64, 64), w.dtype)
    w = w.reshape(4, 2, 64, 64)
    top = jnp.concatenate([w[:, 0], z], axis=2)
    bot = jnp.concatenate([z, w[:, 1]], axis=2)
    return jnp.concatenate([top, bot], axis=1)


def _unpair_blocks(w):
    return jnp.stack([w[:, :64, :64], w[:, 64:, 64:]], axis=1).reshape(8, 64, 64)


BIG = ("w_in", "w_branch", "w_out", "w_ffn_in", "w_ffn_out")
TINY = ("conv_w", "meta_tokens")
SMALL = ("rel_bias_table", "norm_mix", "swa_sinks", "fox_forget_bias", "conv_b", "lru_w_r", "lru_b_r", "lru_w_i",
         "lru_b_i", "lru_lambda", "norm_ffn", "norm_final")
SHARD_AXIS = {"conv_w": 2, "meta_tokens": 1}
BIG_AXIS = {"w_in": 2, "w_branch": 2, "w_out": 1, "w_ffn_in": 2, "w_ffn_out": 1}


def _pack(d, names):
    flat = jnp.concatenate([d[n].reshape(-1) for n in names])
    pad = (-flat.shape[0]) % (256 * 128)
    return jnp.concatenate([flat, jnp.zeros((pad,), F32)]).reshape(-1, 128)


def _unpack(buf, names, shapes):
    flat, out, off = buf.reshape(-1), {}, 0
    for n in names:
        sz = int(np.prod(shapes[n]))
        out[n] = flat[off:off + sz].reshape(shapes[n])
        off += sz
    return out


def _layer_layout(n, a):
    if n == "w_in":
        return _gathered_to_kernel_cols(a)
    return a.reshape(3, LW, D) if n == "w_branch" else a


def _local_step(x, tgt, W, placed=None):
    S = x.shape[0]
    T = S + BLK
    tm = _pick(T, (1408, 384, 128))
    bucket = jnp.asarray(_bucket_table())
    bias = _bias_build(W["rel_bias_table"], bucket, "bias_build")
    h = jnp.concatenate([jnp.zeros((NPAD, D), F32), W["meta_tokens"], x], axis=0)
    if placed is None:
        WL = {n: [W[n][l] for l in range(DEPTH)] for n in BIG}
    else:
        WL = {n: [W[n]] + [None] * (DEPTH - 1) for n in BIG}

    saved = []
    for l in range(DEPTH):
        sv = {"h0": h}
        u, u_t = _rms_fwd(h, W["norm_mix"][l], "rms_mix_fwd")
        proj = _mm(u, WL["w_in"][l], tm=tm, tn=512, tk=D, name="mm_in_fwd")
        oa, oa_t = _swa_fwd(proj, bias, W["swa_sinks"][l], "swa_fwd")
        fb = W["fox_forget_bias"][l].reshape(NH, 1)
        qaug, kaug, kaug_t, vm_t, vo = _fox_prep(proj, _cum_fwd(proj, fb, "cum_fwd"), "fox_prep")
        if placed is not None and l + 1 < DEPTH:
            of, of_t, lse0, lse1, *got = _fox_fwd(qaug, kaug_t, vo, "fox_fwd_gather", gather=(placed[l + 1], GATHER_AXES))
            for n, a in zip(BIG, got):
                WL[n][l + 1] = _layer_layout(n, a)
            lse = [lse0, lse1]
        else:
            of, of_t, *lse = _fox_fwd(qaug, kaug_t, vo, "fox_fwd")
        lru_vec = jnp.concatenate([W["lru_b_r"][l][None], W["lru_b_i"][l][None], W["lru_lambda"][l][None],
                                   W["conv_b"][l][None], jnp.zeros((4, LW), F32)], axis=0)
        oc, oc_t, hs = _lru_fwd(proj, W["conv_w"][l], W["lru_w_r"][l], W["lru_w_i"][l], lru_vec, "lru_fwd")
        *bs, merged, merged_t = _branch_merge_fwd(oa, of, oc, WL["w_branch"][l], proj, "branch_merge_fwd")
        h2 = _mm(merged, WL["w_out"][l], res=h, tm=tm, tn=512, tk=D, name="mm_out_fwd")
        u2, u2_t = _rms_fwd(h2, W["norm_ffn"][l], "rms_ffn_fwd")
        gate, up, act, act_t = _ffn_in_swiglu_fwd(u2, WL["w_ffn_in"][l], "ffn_in_swiglu_fwd")
        h = _mm(act, WL["w_ffn_out"][l], res=h2, tm=tm, tn=512, tk=_pick(DFF, (1408, 256)), name="mm_ffn_out_fwd")
        sv.update(u_t=u_t, proj=proj, o_t=(oa_t, of_t, oc_t), of=of, lse=lse, hs=hs, fb=fb, qaug=qaug, kaug=kaug, kaug_t=kaug_t, vm_t=vm_t,
                  lru_vec=lru_vec, bs=bs, merged_t=merged_t, h2=h2, u2_t=u2_t, gate=gate, up=up, act_t=act_t)
        saved.append(sv)

    dh, dhb, dg_final, loss_vec = _loss_head(h, tgt, W["norm_final"], "loss_head")
    loss = loss_vec[0, 0]

    small = ("norm_mix", "swa_sinks", "fox_forget_bias", "conv_w", "conv_b", "lru_w_r", "lru_b_r", "lru_w_i", "lru_b_i",
             "lru_lambda", "norm_ffn")
    G = {n: [None] * DEPTH for n in small}
    G["norm_final"] = dg_final.reshape(D)
    GW = {n: [None] * DEPTH for n in BIG}
    dist = placed is not None
    if dist:
        x_, y_, c_ = _here()
        cidx = jnp.reshape(c_, (1,)).astype(jnp.int32)
        chip = jnp.reshape(2 * x_ + y_, (1,)).astype(jnp.int32)

    def finish_layer(lp, ss, recv):
        fins = [_add_chips(s, r, ax, chip, cidx, "rs_add_chips_" + n) for n, s, r, ax in zip(BIG, ss, recv, GATHER_AXES)]
        for n, f in zip(BIG, _join_halves(fins, HALF_AXES, "rs_join_halves")):
            GW[n][lp] = f

    pend = None
    dbias = jnp.zeros((NH, BLK, 2 * BLK), F32)
    tkT = tm
    for l in reversed(range(DEPTH)):
        sv = saved[l]
        dw = {}
        dw["w_ffn_out"] = _mm(sv["act_t"], dhb, tm=_pick(DFF, (1408, 256)), tn=D, tk=tkT,
                              name="mm_ffn_out_dw")
        dgate, dup = _ffn_out_dx_swiglu_bwd(dhb, WL["w_ffn_out"][l], sv["gate"], sv["up"], "ffn_out_dx_swiglu_bwd")
        u2t = sv["u2_t"]
        du2, buf = None, None
        for half, dpart in enumerate((dgate, dup)):
            buf = _mm(u2t, dpart, tm=D, tn=_pick(DFF, (1408, 256)), tk=tkT, slab=(buf, 0, 1),
                      col0=half * DFF, cols=2 * DFF, name="mm_ffn_in_dw")
            du2 = _mm(dpart, WL["w_ffn_in"][l], tb=True, res=du2, b_k0=half * DFF, tm=tm, tn=512,
                      tk=_pick(DFF, (1408, 256)), name="mm_ffn_in_dx")
        dw["w_ffn_in"] = buf.reshape(D, 2 * DFF)
        dh, dhb, dgn = _rms_bwd(du2, sv["h2"], W["norm_ffn"][l], dh, "rms_ffn_bwd")
        G["norm_ffn"][l] = dgn.reshape(D)
        dw["w_out"] = _mm(sv["merged_t"], dhb, tm=D, tn=D, tk=tkT, name="mm_out_dw")
        db0, db1, db2, dg0, dg1, dg2 = _out_dx_merge_bwd(dhb, WL["w_out"][l], sv["proj"], *sv["bs"], "out_dx_merge_bwd")
        dos, buf = [], None
        for g, (o_t, db) in enumerate(zip(sv["o_t"], (db0, db1, db2))):
            buf = _mm(o_t, db, tm=LW, tn=D, tk=tkT, slab=(buf, g, 3), name="mm_branch_dw")
            dos.append(_mm(db, WL["w_branch"][l][g], tb=True, out_dtype=BF16, tm=tm, tn=LW, tk=D, name="mm_branch_dx"))
        dw["w_branch"] = buf.reshape(3 * LW, D)
        dqa, dkb, dvb, dbias, dsk = _swa_bwd(sv["proj"], bias, W["swa_sinks"][l], dos[0], dbias, "swa_bwd")
        dka, dva = _band_fold(dkb, dvb, "swa_band_fold")
        G["swa_sinks"][l] = dsk[0, :NH]
        delta = _fox_delta(dos[1], sv["of"], "fox_delta")
        fox_args = (sv["qaug"], sv["kaug"], sv["kaug_t"], sv["vm_t"], dos[1], sv["lse"], delta)
        if pend is not None:
            dqf, dqx, dkf, dvf, dkx, *recv = _fox_bwd(*fox_args, "fox_bwd_scatter", scatter=(pend[2], GATHER_AXES))
            finish_layer(pend[0], pend[1], recv)
            pend = None
        else:
            dqf, dqx, dkf, dvf, dkx = _fox_bwd(*fox_args, "fox_bwd")
        dfl, dfb = _cum_bwd(dqx, dkx, sv["proj"], sv["fb"], "cum_bwd")
        G["fox_forget_bias"][l] = dfb[:, 0]
        dxc, dyc, dwr, dwi, dvec = _lru_bwd(sv["proj"], sv["hs"], dos[2], W["conv_w"][l], W["lru_w_r"][l], W["lru_w_i"][l],
                                            sv["lru_vec"], "lru_bwd")
        G["lru_w_r"][l], G["lru_w_i"][l] = _unpair_blocks(dwr), _unpair_blocks(dwi)
        G["lru_b_r"][l], G["lru_b_i"][l], G["lru_lambda"][l], G["conv_b"][l] = dvec[0], dvec[1], dvec[2], dvec[3]
        G["conv_w"][l] = dvec[4:8]
        dproj = jnp.concatenate([dqa, dqf, dkf, dvf, dxc, dyc, dg0, dg1, dg2, dka, dva, dfl], axis=1)
        dw["w_in"] = _mm(sv["u_t"], dproj, tm=D, tn=IN_SLOT, tk=tkT, name="mm_in_dw")
        du = _mm(dproj, WL["w_in"][l], tb=True, tm=tm, tn=512, tk=_pick(INP, (3328, 512)), name="mm_in_dx")
        dh, dhb, dgn = _rms_bwd(du, sv["h0"], W["norm_mix"][l], dh, "rms_mix_bwd")
        G["norm_mix"][l] = dgn.reshape(D)
        if dist:
            gs = [dw[n] for n in BIG]
            pairs = [_add_half(g, r, hax, cidx, "rs_add_half_" + n)
                     for n, g, r, hax in zip(BIG, gs, _swap_halves(gs, HALF_AXES, "rs_swap_halves"), HALF_AXES)]
            ss, sbs = [list(t) for t in zip(*pairs)]
            ss[0], sbs[0] = _kernel_to_gathered_cols(ss[0]), _kernel_to_gathered_cols(sbs[0])
            pend = (l, ss, sbs)
        else:
            for n in BIG:
                GW[n][l] = dw[n]
    grads = {n: (jnp.stack(v) if isinstance(v, list) else v) for n, v in G.items()}
    grads["rel_bias_table"] = _bias_bwd(dbias, bucket, "bias_bwd")
    grads["meta_tokens"] = dh[NPAD:BLK]
    if dist:
        mine = _place(_pack(grads, SMALL + TINY)[None], 8, 4 * x_ + 2 * y_ + c_)
        recv, grads["small_gathered"] = _scatter_to_chips(pend[2], GATHER_AXES, mine, "rs_scatter")
        finish_layer(pend[0], pend[1], recv)
    grads.update({n: jnp.stack(GW[n]) for n in BIG})
    return loss, dh[BLK:], grads


NAMES = ("meta_tokens", "rel_bias_table", "norm_mix", "w_in", "swa_sinks", "fox_forget_bias", "conv_w", "conv_b",
         "lru_w_r", "lru_b_r", "lru_w_i", "lru_b_i", "lru_lambda", "w_branch", "w_out", "norm_ffn", "w_ffn_in",
         "w_ffn_out", "norm_final")


def _three_d(n, a):
    return a.reshape(DEPTH, 3 * LW, -1) if n == "w_branch" else a


GATHER_AXES = [BIG_AXIS[n] - 1 for n in BIG]
HALF_AXES = [1 - a for a in GATHER_AXES]


def _gather_weights(P):
    x, y, c = _here()
    mine, me = 2 * x + y, 4 * x + 2 * y + c
    placed = []
    for l in range(DEPTH):
        bufs = []
        for n in BIG:
            shard = _three_d(n, P[n])[l].astype(BF16)
            if n == "w_in":
                shard = jnp.pad(shard, ((0, 0), (0, IN_SLOT - IN_SHARD)))
            zero = jnp.zeros_like(shard)
            bufs.append(jnp.concatenate([jnp.where(mine == j, shard, zero) for j in range(N_SHARD)], axis=BIG_AXIS[n] - 1))
        placed.append(bufs)
    full = {n: _layer_layout(n, a) for n, a in zip(BIG, _all_gather_weights(placed[0], GATHER_AXES, "ag_weights"))}
    tiny = _all_gather_small(_place(_pack(P, TINY)[None], 8, me), "ag_tiny_weights")
    parts = [_unpack(tiny[2 * j], TINY, {n: P[n].shape for n in TINY}) for j in range(N_SHARD)]
    for n in TINY:
        full[n] = jnp.concatenate([p[n] for p in parts], axis=SHARD_AXIS[n])
    for n in SMALL:
        full[n] = P[n]
    full["lru_w_r"] = jnp.stack([_pair_blocks(P["lru_w_r"][l]) for l in range(DEPTH)]).astype(BF16)
    full["lru_w_i"] = jnp.stack([_pair_blocks(P["lru_w_i"][l]) for l in range(DEPTH)]).astype(BF16)
    return full, placed


def _reduce_grads(grads, P):
    x, y, c = _here()
    mine, me = 2 * x + y, 4 * x + 2 * y + c
    out = {n: grads[n].reshape(P[n].shape) for n in BIG if n != "w_in"}
    out["w_in"] = grads["w_in"][:, :, :IN_SHARD]
    names = SMALL + TINY
    small = _unpack(_sum_lead(grads["small_gathered"], "sum_small_grads"), names, {n: grads[n].shape for n in names})
    for n in SMALL:
        out[n] = small[n]
    for n in TINY:
        w = P[n].shape[SHARD_AXIS[n]]
        out[n] = lax.dynamic_slice_in_dim(small[n], mine * w, w, axis=SHARD_AXIS[n])
    return out


def _update(P, Gd, M, V):
    delta, new_m, new_v = {}, {}, {}
    for n in BIG + TINY:
        shp = P[n].shape
        two = (int(np.prod(shp[:-1])), shp[-1])
        d, m, v = _adamw(P[n].reshape(two), Gd[n].reshape(two), M[n].reshape(two), V[n].reshape(two), "adamw_" + n)
        delta[n], new_m[n], new_v[n] = d.reshape(shp), m.reshape(shp), v.reshape(shp)
    shapes = {n: P[n].shape for n in SMALL}
    d, m, v = _adamw(_pack(P, SMALL), _pack(Gd, SMALL), _pack(M, SMALL), _pack(V, SMALL), "adamw_small")
    for dst, buf in ((delta, d), (new_m, m), (new_v, v)):
        dst.update(_unpack(buf, SMALL, shapes))
    return delta, new_m, new_v


def kernel(x, meta_tokens, rel_bias_table, norm_mix, w_in, swa_sinks, fox_forget_bias, conv_w, conv_b, lru_w_r, lru_b_r, lru_w_i, lru_b_i, lru_lambda, w_branch, w_out, norm_ffn, w_ffn_in, w_ffn_out, norm_final, loss_target, m_meta_tokens, m_rel_bias_table, m_norm_mix, m_w_in, m_swa_sinks, m_fox_forget_bias, m_conv_w, m_conv_b, m_lru_w_r, m_lru_b_r, m_lru_w_i, m_lru_b_i, m_lru_lambda, m_w_branch, m_w_out, m_norm_ffn, m_w_ffn_in, m_w_ffn_out, m_norm_final, v_meta_tokens, v_rel_bias_table, v_norm_mix, v_w_in, v_swa_sinks, v_fox_forget_bias, v_conv_w, v_conv_b, v_lru_w_r, v_lru_b_r, v_lru_w_i, v_lru_b_i, v_lru_lambda, v_w_branch, v_w_out, v_norm_ffn, v_w_ffn_in, v_w_ffn_out, v_norm_final):
    P = dict(zip(NAMES, (meta_tokens, rel_bias_table, norm_mix, w_in, swa_sinks, fox_forget_bias, conv_w, conv_b, lru_w_r,
                         lru_b_r, lru_w_i, lru_b_i, lru_lambda, w_branch, w_out, norm_ffn, w_ffn_in, w_ffn_out, norm_final)))
    M = dict(zip(NAMES, (m_meta_tokens, m_rel_bias_table, m_norm_mix, m_w_in, m_swa_sinks, m_fox_forget_bias, m_conv_w,
                         m_conv_b, m_lru_w_r, m_lru_b_r, m_lru_w_i, m_lru_b_i, m_lru_lambda, m_w_branch, m_w_out, m_norm_ffn,
                         m_w_ffn_in, m_w_ffn_out, m_norm_final)))
    V = dict(zip(NAMES, (v_meta_tokens, v_rel_bias_table, v_norm_mix, v_w_in, v_swa_sinks, v_fox_forget_bias, v_conv_w,
                         v_conv_b, v_lru_w_r, v_lru_b_r, v_lru_w_i, v_lru_b_i, v_lru_lambda, v_w_branch, v_w_out, v_norm_ffn,
                         v_w_ffn_in, v_w_ffn_out, v_norm_final)))
    W, placed = _gather_weights(P)
    loss_local, grad_x, grads = _local_step(x[0], loss_target[0], W, placed)
    loss = lax.psum(loss_local, ("x", "y", "c"))
    Gd = _reduce_grads(grads, P)
    delta, new_m, new_v = _update(P, Gd, M, V)
    return (loss, grad_x[None], *[Gd[n] for n in NAMES], *[delta[n] for n in NAMES],
            *[new_m[n] for n in NAMES], *[new_v[n] for n in NAMES])
```

```python
import math

import numpy as np
import jax
import jax.numpy as jnp
from jax import lax
from jax.experimental import pallas as pl
from jax.experimental.pallas import tpu as pltpu

F32, BF16 = jnp.float32, jnp.bfloat16
MESH = pl.DeviceIdType.MESH
ANY = pl.BlockSpec(memory_space=pl.ANY)
SMEM = pl.BlockSpec(memory_space=pltpu.SMEM)

D = 1024
DEPTH = 4
BLK = 128
N_META = 16
NPAD = 112
NH = 8
LW = 512
DFF = 2816
EPS = 1e-6
NEG = -1e30
SCALE = 0.125
LRU_C = 8.0
REL_BUCKETS = 32
N_SHARD = 4
QA, QF, KF, VF, XC, YC, GT, KA, VA, FL, INP = 0, 512, 1024, 1536, 2048, 2560, 3072, 6144, 6272, 6400, 6656
IN_COLS = 6408
VMEM_LIMIT = 48 * 1024 * 1024

ADAM_LR, ADAM_B1, ADAM_B2, ADAM_EPS, ADAM_WD, ADAM_STEP = 0.001, 0.9, 0.999, 1e-08, 0.01, 10


def _cp(*sem):
    return pltpu.CompilerParams(dimension_semantics=sem or None, vmem_limit_bytes=VMEM_LIMIT)


def _pick(n, prefs):
    for p in prefs:
        if n % p == 0:
            return p
    return n


def _rt(T):
    return _pick(T, (384, 128))


def _sigmoid(z):
    return 1.0 / (1.0 + jnp.exp(-z))


def _log_sigmoid(z):
    return jnp.minimum(z, 0.0) - jnp.log(1.0 + jnp.exp(-jnp.abs(z)))


def _gelu(y):
    c = math.sqrt(2.0 / math.pi)
    return 0.5 * y * (1.0 + jnp.tanh(c * (y + 0.044715 * y * y * y)))


def _gelu_grad(y):
    c = math.sqrt(2.0 / math.pi)
    t = jnp.tanh(c * (y + 0.044715 * y * y * y))
    return 0.5 * (1.0 + t) + 0.5 * y * (1.0 - t * t) * c * (1.0 + 3.0 * 0.044715 * y * y)


def _neg_expm1(z):
    series = -z * (1.0 + z * (0.5 + z * (1.0 / 6.0 + z * (1.0 / 24.0 + z * (1.0 / 120.0)))))
    return jnp.where(z > -0.1, series, 1.0 - jnp.exp(z))


def _dot(a, b, ca, cb):
    return lax.dot_general(a, b, (((ca,), (cb,)), ((), ())), preferred_element_type=F32)


def _mm(a, b, *, ta=False, tb=False, res=None, out_dtype=F32, tm, tn, tk, name, slab=None, b_k0=0, col0=0, cols=None):
    M, K = (a.shape[1], a.shape[0]) if ta else a.shape
    N = b.shape[0] if tb else b.shape[1]
    assert (b.shape[1] if tb else b.shape[0]) >= K + b_k0 and M % tm == 0 and N % tn == 0 and K % tk == 0, (name, a.shape, b.shape)
    assert b_k0 % tk == 0 and col0 % tn == 0
    nk, kb, jb = K // tk, b_k0 // tk, col0 // tn
    ca, cb = (0 if ta else 1), (1 if tb else 0)
    n_in = 2 + (res is not None) + (slab is not None and slab[0] is not None)

    def body(*refs):
        a_ref, b_ref = refs[:2]
        r_ref = refs[2] if res is not None else None
        o_ref = refs[n_in]
        part = _dot(a_ref[...].astype(BF16), b_ref[...].astype(BF16), ca, cb)

        def fin(acc):
            if res is not None:
                acc = acc + r_ref[...]
            o_ref[...] = acc.astype(out_dtype)

        if nk == 1:
            fin(part)
        else:
            acc_ref = refs[-1]
            k = pl.program_id(2)

            @pl.when(k == 0)
            def _():
                acc_ref[...] = part

            @pl.when(k > 0)
            def _():
                acc_ref[...] += part

            @pl.when(k == nk - 1)
            def _():
                fin(acc_ref[...])

    a_spec = pl.BlockSpec((tk, tm), lambda i, j, k: (k, i)) if ta else pl.BlockSpec((tm, tk), lambda i, j, k: (i, k))
    b_spec = (pl.BlockSpec((tn, tk), lambda i, j, k: (j, k + kb)) if tb
              else pl.BlockSpec((tk, tn), lambda i, j, k: (k + kb, j)))
    o_spec = pl.BlockSpec((tm, tn), lambda i, j, k: (i, j))
    in_specs, ops = [a_spec, b_spec], [a, b]
    if res is not None:
        in_specs.append(o_spec)
        ops.append(res)
    out_shape, aliases = jax.ShapeDtypeStruct((M, N), out_dtype), {}
    if slab is not None:
        buf, idx, n = slab
        o_spec = pl.BlockSpec((None, tm, tn), lambda i, j, k: (idx, i, j + jb))
        out_shape = jax.ShapeDtypeStruct((n, M, cols or N), out_dtype)
        if buf is not None:
            aliases = {len(ops): 0}
            in_specs.append(ANY)
            ops.append(buf)
    return pl.pallas_call(
        body, grid=(M // tm, N // tn, nk), in_specs=in_specs, out_specs=o_spec, out_shape=out_shape,
        input_output_aliases=aliases, scratch_shapes=[pltpu.VMEM((tm, tn), F32)] if nk > 1 else [],
        compiler_params=_cp("parallel", "parallel", "arbitrary"), name=name)(*ops)


def _rms_fwd(h, g, name):
    T = h.shape[0]
    tr = _rt(T)

    def body(h_ref, g_ref, u_ref, ut_ref):
        x = h_ref[...]
        r = lax.rsqrt(jnp.mean(x * x, axis=-1, keepdims=True) + EPS)
        u = (x * r * g_ref[...]).astype(BF16)
        u_ref[...] = u
        ut_ref[...] = u.T

    return pl.pallas_call(
        body, grid=(T // tr,),
        in_specs=[pl.BlockSpec((tr, D), lambda i: (i, 0)), pl.BlockSpec((1, D), lambda i: (0, 0))],
        out_specs=[pl.BlockSpec((tr, D), lambda i: (i, 0)), pl.BlockSpec((D, tr), lambda i: (0, i))],
        out_shape=[jax.ShapeDtypeStruct((T, D), BF16), jax.ShapeDtypeStruct((D, T), BF16)],
        compiler_params=_cp("parallel"), name=name)(h, g.reshape(1, D))


def _rms_bwd(du, h, g, dres, name):
    T = h.shape[0]
    tr = _rt(T)

    def body(du_ref, h_ref, g_ref, dres_ref, dh_ref, dhb_ref, dg_ref):
        x = h_ref[...]
        r = lax.rsqrt(jnp.mean(x * x, axis=-1, keepdims=True) + EPS)
        xh = x * r
        dy = du_ref[...]
        dxh = dy * g_ref[...]
        dx = r * (dxh - xh * jnp.mean(dxh * xh, axis=-1, keepdims=True))
        dh = dres_ref[...] + dx
        dh_ref[...] = dh
        dhb_ref[...] = dh.astype(BF16)
        part = jnp.sum(dy * xh, axis=0, keepdims=True)

        @pl.when(pl.program_id(0) == 0)
        def _():
            dg_ref[...] = part

        @pl.when(pl.program_id(0) > 0)
        def _():
            dg_ref[...] += part

    row = pl.BlockSpec((tr, D), lambda i: (i, 0))
    vec = pl.BlockSpec((1, D), lambda i: (0, 0))
    return pl.pallas_call(
        body, grid=(T // tr,), in_specs=[row, row, vec, row], out_specs=[row, row, vec],
        out_shape=[jax.ShapeDtypeStruct((T, D), F32), jax.ShapeDtypeStruct((T, D), BF16), jax.ShapeDtypeStruct((1, D), F32)],
        compiler_params=_cp("arbitrary"), name=name)(du, h, g.reshape(1, D), dres)


def _loss_head(h, tgt, g, name):
    T = h.shape[0]
    nb = T // BLK

    def body(h_ref, t_ref, g_ref, dh_ref, dhb_ref, dg_ref, loss_ref):
        i = pl.program_id(0)
        x = h_ref[...]
        r = lax.rsqrt(jnp.mean(x * x, axis=-1, keepdims=True) + EPS)
        xh = x * r
        gv = g_ref[...]
        tok = i >= 1
        err = jnp.where(tok, xh * gv - t_ref[...], 0.0)
        dy = err * (1.0 / D)
        dxh = dy * gv
        dx = r * (dxh - xh * jnp.mean(dxh * xh, axis=-1, keepdims=True))
        dh_ref[...] = dx
        dhb_ref[...] = dx.astype(BF16)
        dg = jnp.sum(dy * xh, axis=0, keepdims=True)
        ls = jnp.zeros((1, BLK), F32) + jnp.sum(err * err) * (0.5 / D)

        @pl.when(i == 0)
        def _():
            dg_ref[...] = dg
            loss_ref[...] = ls

        @pl.when(i > 0)
        def _():
            dg_ref[...] += dg
            loss_ref[...] += ls

    row = pl.BlockSpec((BLK, D), lambda i: (i, 0))
    vec = pl.BlockSpec((1, D), lambda i: (0, 0))
    return pl.pallas_call(
        body, grid=(nb,),
        in_specs=[row, pl.BlockSpec((BLK, D), lambda i: (jnp.maximum(i - 1, 0), 0)), vec],
        out_specs=[row, row, vec, pl.BlockSpec((1, BLK), lambda i: (0, 0))],
        out_shape=[jax.ShapeDtypeStruct((T, D), F32), jax.ShapeDtypeStruct((T, D), BF16),
                   jax.ShapeDtypeStruct((1, D), F32), jax.ShapeDtypeStruct((1, BLK), F32)],
        compiler_params=_cp("arbitrary"), name=name)(h, tgt, g.reshape(1, D))


def _bucket_table():
    q = np.arange(BLK)[:, None]
    k = np.arange(2 * BLK)[None, :]
    d = np.maximum(q + BLK - k, 0)
    max_exact = REL_BUCKETS // 2
    scaled = np.log(np.maximum(d, 1).astype(np.float32) / np.float32(max_exact)) / np.float32(math.log(128 / max_exact))
    large = np.minimum(max_exact + (scaled.astype(np.float32) * (REL_BUCKETS - max_exact)).astype(np.int32), REL_BUCKETS - 1)
    return np.where(d < max_exact, d, large).astype(np.int32)


def _bias_build(table, bucket, name):
    def body(t_ref, bk_ref, o_ref):
        bk = bk_ref[...]
        for h in range(NH):
            acc = jnp.zeros((BLK, 2 * BLK), F32)
            for b in range(REL_BUCKETS):
                acc = jnp.where(bk == b, t_ref[b, h], acc)
            o_ref[h] = acc

    return pl.pallas_call(
        body, in_specs=[SMEM, pl.BlockSpec(memory_space=pltpu.VMEM)], out_specs=pl.BlockSpec(memory_space=pltpu.VMEM),
        out_shape=jax.ShapeDtypeStruct((NH, BLK, 2 * BLK), F32), compiler_params=_cp(), name=name)(table, bucket)


def _bias_bwd(dbias, bucket, name):
    def body(d_ref, bk_ref, o_ref):
        bk = bk_ref[...]
        for h in range(NH):
            dh = d_ref[h]
            for b in range(REL_BUCKETS):
                o_ref[b, h] = jnp.sum(jnp.where(bk == b, dh, 0.0))

    return pl.pallas_call(
        body, in_specs=[pl.BlockSpec(memory_space=pltpu.VMEM)] * 2, out_specs=SMEM,
        out_shape=jax.ShapeDtypeStruct((REL_BUCKETS, NH), F32), compiler_params=_cp(), name=name)(dbias, bucket)


def _swa_specs(nq_cols):
    prev = lambda n: jnp.maximum(n - 1, 0)
    return [
        pl.BlockSpec((BLK, nq_cols), lambda n: (n, QA // nq_cols)),
        pl.BlockSpec((BLK, BLK), lambda n: (prev(n), KA // BLK)), pl.BlockSpec((BLK, BLK), lambda n: (n, KA // BLK)),
        pl.BlockSpec((BLK, BLK), lambda n: (prev(n), VA // BLK)), pl.BlockSpec((BLK, BLK), lambda n: (n, VA // BLK)),
    ]


def _swa_mask(n):
    row = lax.broadcasted_iota(jnp.int32, (BLK, 2 * BLK), 0)
    col = lax.broadcasted_iota(jnp.int32, (BLK, 2 * BLK), 1)
    dist = row + BLK - col
    return (dist >= 0) & (dist < BLK) & ((n - 1) * BLK + col >= NPAD)


def _swa_probs(qm, ksel, mask, bias_h, sink):
    s = _dot(qm, ksel, 1, 1) * SCALE
    s = jnp.where(mask, s + bias_h, NEG)
    m = jnp.maximum(jnp.max(s, axis=-1, keepdims=True), sink)
    p = jnp.exp(s - m)
    psink = jnp.exp(sink - m)
    inv = 1.0 / (jnp.sum(p, axis=-1, keepdims=True) + psink)
    return p * inv, psink * inv


def _swa_fwd(proj, bias, sinks, name):
    T = proj.shape[0]
    nb = T // BLK

    def body(sk_ref, q_ref, kp_ref, kc_ref, vp_ref, vc_ref, b_ref, o_ref, ot_ref):
        n = pl.program_id(0)
        lo = lax.broadcasted_iota(jnp.int32, (1, BLK), 1) < 64
        kb = jnp.concatenate([kp_ref[...], kc_ref[...]], axis=0)
        vb = jnp.concatenate([vp_ref[...], vc_ref[...]], axis=0)
        kbs = (kb.astype(BF16), pltpu.roll(kb, 64, 1).astype(BF16))
        vbs = (vb, pltpu.roll(vb, 64, 1))
        mask = _swa_mask(n)
        outs = []
        for pr in range(NH // 2):
            qp = q_ref[:, pr * BLK:(pr + 1) * BLK]
            kv = pr // 2
            acc = jnp.zeros((BLK, BLK), F32)
            for e in range(2):
                lm = lo if e == 0 else jnp.logical_not(lo)
                sw = 0 if kv == e else 1
                qm = jnp.where(lm, qp, 0.0).astype(BF16)
                pn, _ = _swa_probs(qm, kbs[sw], mask, b_ref[2 * pr + e], sk_ref[2 * pr + e])
                acc = acc + _dot(pn.astype(BF16), jnp.where(lm, vbs[sw], 0.0).astype(BF16), 1, 0)
            outs.append(acc)
        o = jnp.concatenate(outs, axis=1).astype(BF16)
        o_ref[...] = o
        ot_ref[...] = o.T

    return pl.pallas_call(
        body, grid=(nb,),
        in_specs=[SMEM] + _swa_specs(512) + [pl.BlockSpec((NH, BLK, 2 * BLK), lambda n: (0, 0, 0))],
        out_specs=[pl.BlockSpec((BLK, 512), lambda n: (n, 0)), pl.BlockSpec((512, BLK), lambda n: (0, n))],
        out_shape=[jax.ShapeDtypeStruct((T, 512), BF16), jax.ShapeDtypeStruct((512, T), BF16)],
        compiler_params=_cp("parallel"), name=name)(sinks, proj, proj, proj, proj, proj, bias)


def _swa_bwd(proj, bias, sinks, do, dbias_in, name):
    T = proj.shape[0]
    nb = T // BLK

    def body(sk_ref, q_ref, kp_ref, kc_ref, vp_ref, vc_ref, b_ref, do_ref, dbi_ref,
             dq_ref, dk_ref, dv_ref, db_ref, dsk_ref, sk_acc):
        n = pl.program_id(0)
        lane = lax.broadcasted_iota(jnp.int32, (1, BLK), 1)
        lo = lane < 64
        kb = jnp.concatenate([kp_ref[...], kc_ref[...]], axis=0)
        vb = jnp.concatenate([vp_ref[...], vc_ref[...]], axis=0)
        kbs = (kb, pltpu.roll(kb, 64, 1))
        vbs = (vb, pltpu.roll(vb, 64, 1))
        mask = _swa_mask(n)

        @pl.when(n == 0)
        def _():
            db_ref[...] = dbi_ref[...]
            sk_acc[...] = jnp.zeros_like(sk_acc)

        dqs = []
        dk = jnp.zeros((2 * BLK, BLK), F32)
        dv = jnp.zeros((2 * BLK, BLK), F32)
        for pr in range(NH // 2):
            qp = q_ref[:, pr * BLK:(pr + 1) * BLK]
            dop = do_ref[:, pr * BLK:(pr + 1) * BLK].astype(F32)
            kv = pr // 2
            dq = jnp.zeros((BLK, BLK), F32)
            for e in range(2):
                h = 2 * pr + e
                lm = lo if e == 0 else jnp.logical_not(lo)
                sw = 0 if kv == e else 1
                qm = jnp.where(lm, qp, 0.0)
                dom = jnp.where(lm, dop, 0.0)
                pn, ps = _swa_probs(qm.astype(BF16), kbs[sw].astype(BF16), mask, b_ref[h], sk_ref[h])
                dp = _dot(dom.astype(BF16), vbs[sw].astype(BF16), 1, 1)
                delta = jnp.sum(pn * dp, axis=-1, keepdims=True)
                ds = pn * (dp - delta)
                db_ref[h] += ds
                sk_acc[...] += jnp.where(lane == h, -(ps * delta), 0.0)
                dsb = (ds * SCALE).astype(BF16)
                dq = dq + _dot(dsb, jnp.where(lm, kbs[sw], 0.0).astype(BF16), 1, 0)
                qk = qm if sw == 0 else pltpu.roll(qm, 64, 1)
                dok = dom if sw == 0 else pltpu.roll(dom, 64, 1)
                dk = dk + _dot(dsb, qk.astype(BF16), 0, 0)
                dv = dv + _dot(pn.astype(BF16), dok.astype(BF16), 0, 0)
            dqs.append(dq)
        dq_ref[...] = jnp.concatenate(dqs, axis=1).astype(BF16)
        dk_ref[0] = dk
        dv_ref[0] = dv

        @pl.when(n == nb - 1)
        def _():
            dsk_ref[...] = jnp.sum(sk_acc[...], axis=0, keepdims=True)

    full_b = pl.BlockSpec((NH, BLK, 2 * BLK), lambda n: (0, 0, 0))
    band = pl.BlockSpec((1, 2 * BLK, BLK), lambda n: (n, 0, 0))
    return pl.pallas_call(
        body, grid=(nb,),
        in_specs=[SMEM] + _swa_specs(512) + [full_b, pl.BlockSpec((BLK, 512), lambda n: (n, 0)), full_b],
        out_specs=[pl.BlockSpec((BLK, 512), lambda n: (n, 0)), band, band, full_b, pl.BlockSpec((1, BLK), lambda n: (0, 0))],
        out_shape=[jax.ShapeDtypeStruct((T, 512), BF16), jax.ShapeDtypeStruct((nb, 2 * BLK, BLK), F32),
                   jax.ShapeDtypeStruct((nb, 2 * BLK, BLK), F32), jax.ShapeDtypeStruct((NH, BLK, 2 * BLK), F32),
                   jax.ShapeDtypeStruct((1, BLK), F32)],
        scratch_shapes=[pltpu.VMEM((BLK, BLK), F32)],
        compiler_params=_cp("arbitrary"), name=name)(sinks, proj, proj, proj, proj, proj, bias, do, dbias_in)


def _band_fold(dkb, dvb, name):
    nb = dkb.shape[0]

    def body(ko_ref, kn_ref, vo_ref, vn_ref, dk_ref, dv_ref):
        last = pl.program_id(0) == nb - 1
        dk_ref[...] = (ko_ref[0] + jnp.where(last, 0.0, kn_ref[0])).astype(BF16)
        dv_ref[...] = (vo_ref[0] + jnp.where(last, 0.0, vn_ref[0])).astype(BF16)

    own = pl.BlockSpec((1, BLK, BLK), lambda j: (j, 1, 0))
    nxt = pl.BlockSpec((1, BLK, BLK), lambda j: (jnp.minimum(j + 1, nb - 1), 0, 0))
    out = pl.BlockSpec((BLK, BLK), lambda j: (j, 0))
    return pl.pallas_call(
        body, grid=(nb,), in_specs=[own, nxt, own, nxt], out_specs=[out, out],
        out_shape=[jax.ShapeDtypeStruct((nb * BLK, BLK), BF16)] * 2,
        compiler_params=_cp("parallel"), name=name)(dkb, dkb, dvb, dvb)


def _token_major(x, width):
    full = jnp.concatenate([x, jnp.zeros((BLK - NH, BLK), F32)], axis=0).T
    return full if width == BLK else jnp.concatenate([full, jnp.zeros((BLK, width - BLK), F32)], axis=1)


def _cum_fwd(proj, fb, name):
    T = proj.shape[0]
    tr = _rt(T)

    def body(z_ref, fb_ref, c_ref, carry):
        g = pl.program_id(0)
        lane = lax.broadcasted_iota(jnp.int32, (NH, BLK), 1)

        @pl.when(g == 0)
        def _():
            carry[...] = jnp.zeros_like(carry)

        run = carry[...]
        for sb in range(tr // BLK):
            r = slice(sb * BLK, (sb + 1) * BLK)
            z = z_ref[r, :].T[0:NH, :] + fb_ref[...]
            x = jnp.where(g * tr + sb * BLK + lane >= NPAD, _log_sigmoid(z), 0.0)
            s = 1
            while s < BLK:
                x = x + jnp.where(lane >= s, pltpu.roll(x, s, 1), 0.0)
                s *= 2
            x = x + run
            run = jnp.zeros((NH, BLK), F32) + jnp.sum(jnp.where(lane == BLK - 1, x, 0.0), axis=-1, keepdims=True)
            c_ref[r, :] = _token_major(x, BLK)
        carry[...] = run

    return pl.pallas_call(
        body, grid=(T // tr,),
        in_specs=[pl.BlockSpec((tr, BLK), lambda g: (g, FL // BLK)), pl.BlockSpec((NH, 1), lambda g: (0, 0))],
        out_specs=pl.BlockSpec((tr, BLK), lambda g: (g, 0)), out_shape=jax.ShapeDtypeStruct((T, BLK), F32),
        scratch_shapes=[pltpu.VMEM((NH, BLK), F32)], compiler_params=_cp("arbitrary"), name=name)(proj, fb)


def _cum_bwd(dqx, dkx, proj, fb, name):
    T = proj.shape[0]
    tr = _rt(T)
    nb = T // tr

    def body(dq_ref, dk_ref, z_ref, fb_ref, dz_ref, db_ref, carry):
        k = pl.program_id(0)
        g = nb - 1 - k
        lane = lax.broadcasted_iota(jnp.int32, (NH, BLK), 1)

        @pl.when(k == 0)
        def _():
            carry[...] = jnp.zeros_like(carry)
            db_ref[...] = jnp.zeros_like(db_ref)

        def picked(ref, r, r_first, r_second):
            rows = []
            for p in range(NH // 2):
                t_ = ref[r, p * BLK:(p + 1) * BLK].T
                rows += [t_[r_first:r_first + 1, :], t_[r_second:r_second + 1, :]]
            return jnp.concatenate(rows, axis=0)

        run, tot = carry[...], jnp.zeros((NH, 1), F32)
        for sb in reversed(range(tr // BLK)):
            r = slice(sb * BLK, (sb + 1) * BLK)
            x = picked(dq_ref, r, 64, 0) - picked(dk_ref, r, 67, 3)
            s = 1
            while s < BLK:
                x = x + jnp.where(lane < BLK - s, pltpu.roll(x, BLK - s, 1), 0.0)
                s *= 2
            x = x + run
            run = jnp.zeros((NH, BLK), F32) + jnp.sum(jnp.where(lane == 0, x, 0.0), axis=-1, keepdims=True)
            z = z_ref[r, :].T[0:NH, :] + fb_ref[...]
            dz = jnp.where(g * tr + sb * BLK + lane >= NPAD, x * _sigmoid(-z), 0.0)
            tot = tot + jnp.sum(dz, axis=-1, keepdims=True)
            dz_ref[r, :] = _token_major(dz, 2 * BLK).astype(BF16)
        carry[...] = run
        db_ref[...] += tot

    rev = lambda k: nb - 1 - k
    wide = pl.BlockSpec((tr, 512), lambda k: (rev(k), 0))
    return pl.pallas_call(
        body, grid=(nb,),
        in_specs=[wide, wide, pl.BlockSpec((tr, BLK), lambda k: (rev(k), FL // BLK)), pl.BlockSpec((NH, 1), lambda k: (0, 0))],
        out_specs=[pl.BlockSpec((tr, 2 * BLK), lambda k: (rev(k), 0)), pl.BlockSpec((NH, BLK), lambda k: (0, 0))],
        out_shape=[jax.ShapeDtypeStruct((T, 2 * BLK), BF16), jax.ShapeDtypeStruct((NH, BLK), F32)],
        scratch_shapes=[pltpu.VMEM((NH, BLK), F32)], compiler_params=_cp("arbitrary"), name=name)(dqx, dkx, proj, fb)


def _fox_prep(proj, ccol, name):
    T = proj.shape[0]
    tr = _pick(T, (1408, 384, 128))

    def body(q_ref, k_ref, v_ref, cc_ref, qa_ref, ka_ref, kt_ref, vmt_ref, vo_ref):
        h = pl.program_id(1)
        lane = lax.broadcasted_iota(jnp.int32, (1, BLK), 1)
        own = (lane >> 6) == (h & 1)
        a0 = 64 * (1 - (h & 1))
        c = _lane_pick(cc_ref[...], lane, h)
        hi = c.astype(BF16).astype(F32)
        mid = (c - hi).astype(BF16).astype(F32)
        lo = (c - hi - mid).astype(BF16).astype(F32)
        ones = (lane >= a0 + 3) & (lane < a0 + 6)
        qa = jnp.where(own, q_ref[...] * SCALE, jnp.where(ones, 1.0, 0.0))
        qa = jnp.where(lane == a0, hi, jnp.where(lane == a0 + 1, mid, jnp.where(lane == a0 + 2, lo, qa)))
        ones = (lane >= a0) & (lane < a0 + 3)
        ka = jnp.where(own, k_ref[...], jnp.where(ones, 1.0, 0.0))
        ka = jnp.where(lane == a0 + 3, -hi, jnp.where(lane == a0 + 4, -mid, jnp.where(lane == a0 + 5, -lo, ka)))
        qa_ref[...] = qa.astype(BF16)
        kab = ka.astype(BF16)
        ka_ref[...] = kab
        kt_ref[...] = kab.T
        vm = jnp.where(own, v_ref[...], 0.0)
        vmt_ref[...] = vm.astype(BF16).T
        vo_ref[...] = jnp.where(lane == a0, 1.0, vm).astype(BF16)

    pair = lambda col0: pl.BlockSpec((tr, BLK), lambda i, h: (i, col0 // BLK + (h >> 1)))
    out = pl.BlockSpec((None, tr, BLK), lambda i, h: (h, i, 0))
    out_t = pl.BlockSpec((None, BLK, tr), lambda i, h: (h, 0, i))
    tok = jax.ShapeDtypeStruct((NH, T, BLK), BF16)
    return pl.pallas_call(
        body, grid=(T // tr, NH), in_specs=[pair(QF), pair(KF), pair(VF), pl.BlockSpec((tr, BLK), lambda i, h: (i, 0))],
        out_specs=[out, out, out_t, out_t, out],
        out_shape=[tok, tok, jax.ShapeDtypeStruct((NH, BLK, T), BF16), jax.ShapeDtypeStruct((NH, BLK, T), BF16), tok],
        compiler_params=_cp("parallel", "arbitrary"), name=name)(proj, proj, proj, ccol)


def _fox_fwd(qaug, kaug_t, vo, name, gather=None):
    T = qaug.shape[1]
    t = _rt(T)
    nt = T // t
    ng = len(gather[0]) if gather else 0

    pairs = [(i, j) for i in range(nt) for j in range(i + 1)]
    i_of = jnp.asarray(np.array([p[0] for p in pairs], np.int32))
    j_of = jnp.asarray(np.array([p[1] for p in pairs], np.int32))
    ns = len(pairs)

    def body(i_ref, j_ref, q0, q1, k0, k1, v0, v1, *rest):
        o_ref, ot_ref, lse0_ref, lse1_ref = rest[ng:ng + 4]
        m_ref, acc_ref = rest[2 * ng + 4:2 * ng + 6]
        p_, s_ = pl.program_id(0), pl.program_id(1)
        i, j = i_ref[s_], j_ref[s_]
        lane = lax.broadcasted_iota(jnp.int32, (1, BLK), 1)
        lo = lane < 64
        if gather:
            start, finish = _gather_plan(rest[ng + 4:2 * ng + 4], gather[1], *rest[2 * ng + 6:])
            pl.when((p_ == 0) & (s_ == 0))(start)

        @pl.when(j == 0)
        def _():
            m_ref[...] = jnp.full_like(m_ref, NEG)
            acc_ref[...] = jnp.zeros_like(acc_ref)

        def step(masked):
            for e, (q_ref, k_ref, v_ref) in enumerate(((q0, k0, v0), (q1, k1, v1))):
                s = _dot(q_ref[...], k_ref[...], 1, 0)
                if masked:
                    s = jnp.where(_fox_mask(i, j, t), s, NEG)
                m_old = m_ref[e]
                m_new = jnp.maximum(m_old, jnp.max(s, axis=-1, keepdims=True))
                m_ref[e] = m_new
                pe = jnp.exp(s - jnp.concatenate([m_new] * (t // BLK), axis=1))
                acc_ref[e] = jnp.exp(m_old - m_new) * acc_ref[e] + _dot(pe.astype(BF16), v_ref[...], 1, 0)

        pl.when((j < i) & (j > 0))(lambda: step(False))
        pl.when((j == i) | ((j == 0) & (i > 0)))(lambda: step(True))

        @pl.when(j == i)
        def _():
            rows = i * t + lax.broadcasted_iota(jnp.int32, (t, 1), 0)
            l0, l1 = _lane_pick(acc_ref[0], lane, 64), _lane_pick(acc_ref[1], lane, 0)
            o = jnp.where(rows >= NPAD, jnp.where(lo, acc_ref[0] / l0, acc_ref[1] / l1), 0.0).astype(BF16)
            o_ref[...] = o
            ot_ref[...] = o.T
            lse0_ref[...] = m_ref[0] + jnp.log(l0)
            lse1_ref[...] = m_ref[1] + jnp.log(l1)

        if gather:
            pl.when((p_ == NH // 2 - 1) & (s_ == ns - 1))(finish)

    qs = lambda e: pl.BlockSpec((None, t, BLK), lambda p, s, ii, jj: (2 * p + e, ii[s], 0))
    ks = lambda e: pl.BlockSpec((None, t, BLK), lambda p, s, ii, jj: (2 * p + e, jj[s], 0))
    kts = lambda e: pl.BlockSpec((None, BLK, t), lambda p, s, ii, jj: (2 * p + e, 0, jj[s]))
    rep = pl.BlockSpec((None, t, BLK), lambda p, s, ii, jj: (p, ii[s], 0))
    bufs = list(gather[0]) if gather else []
    return pl.pallas_call(
        body,
        grid_spec=pltpu.PrefetchScalarGridSpec(
            num_scalar_prefetch=2, grid=(NH // 2, ns),
            in_specs=[qs(0), qs(1), kts(0), kts(1), ks(0), ks(1)] + [ANY] * ng,
            out_specs=[pl.BlockSpec((t, BLK), lambda p, s, ii, jj: (ii[s], p)),
                       pl.BlockSpec((BLK, t), lambda p, s, ii, jj: (p, ii[s])), rep, rep] + [ANY] * ng,
            scratch_shapes=[pltpu.VMEM((2, t, BLK), F32), pltpu.VMEM((2, t, BLK), F32)]
            + ([pltpu.SemaphoreType.DMA((6 * ng,)), pltpu.SemaphoreType.DMA((6 * ng,))] if gather else [])),
        out_shape=[jax.ShapeDtypeStruct((T, 512), BF16), jax.ShapeDtypeStruct((512, T), BF16)]
        + [jax.ShapeDtypeStruct((NH // 2, T, BLK), F32)] * 2 + [jax.ShapeDtypeStruct(b.shape, b.dtype) for b in bufs],
        input_output_aliases={8 + g: 4 + g for g in range(ng)},
        compiler_params=(pltpu.CompilerParams(dimension_semantics=("arbitrary",) * 2, vmem_limit_bytes=VMEM_LIMIT,
                                              has_side_effects=True) if gather
                         else _cp("parallel", "arbitrary")), name=name)(i_of, j_of, qaug, qaug, kaug_t, kaug_t, vo, vo, *bufs)


def _fox_delta(do, o, name):
    T = do.shape[0]
    tr = _rt(T)

    def body(do_ref, o_ref, d0_ref, d1_ref):
        lo = lax.broadcasted_iota(jnp.int32, (1, BLK), 1) < 64
        prod = do_ref[...].astype(F32) * o_ref[...].astype(F32)
        d0_ref[...] = jnp.zeros((tr, BLK), F32) + jnp.sum(jnp.where(lo, prod, 0.0), axis=-1, keepdims=True)
        d1_ref[...] = jnp.zeros((tr, BLK), F32) + jnp.sum(jnp.where(lo, 0.0, prod), axis=-1, keepdims=True)

    blk = pl.BlockSpec((tr, BLK), lambda i, p: (i, p))
    rep = pl.BlockSpec((None, tr, BLK), lambda i, p: (p, i, 0))
    return pl.pallas_call(
        body, grid=(T // tr, NH // 2), in_specs=[blk, blk], out_specs=[rep, rep],
        out_shape=[jax.ShapeDtypeStruct((NH // 2, T, BLK), F32)] * 2,
        compiler_params=_cp("parallel", "parallel"), name=name)(do, o)


def _fox_bwd(qaug, kaug, kaug_t, vm_t, do, lses, deltas, name, scatter=None):
    T = qaug.shape[1]
    t = _rt(T)
    nt = T // t
    ng = len(scatter[0]) if scatter else 0
    pairs = [(i, j) for j in range(nt) for i in range(j, nt)]
    i_of = jnp.asarray(np.array([p[0] for p in pairs], np.int32))
    j_of = jnp.asarray(np.array([p[1] for p in pairs], np.int32))
    ns = len(pairs)

    def body(i_ref, j_ref, q0, q1, k0, k1, kt0, kt1, v0, v1, do_ref, lse0, lse1, dl0, dl1, *rest):
        dq_ref, dqx_ref, dk_ref, dv_ref, dkx_ref = rest[ng:ng + 5]
        dq_acc, dk_acc, dv_acc = rest[2 * ng + 5:2 * ng + 8]
        p_, s_ = pl.program_id(0), pl.program_id(1)
        i, j = i_ref[s_], j_ref[s_]
        lane = lax.broadcasted_iota(jnp.int32, (1, BLK), 1)
        lo = lane < 64
        if scatter:
            start, finish = _scatter_plan(rest[:ng], rest[ng + 5:2 * ng + 5], scatter[1], *rest[2 * ng + 8:])
            pl.when((p_ == 0) & (s_ == 0))(start)

        @pl.when(s_ == 0)
        def _():
            dq_acc[...] = jnp.zeros_like(dq_acc)

        @pl.when(i == j)
        def _():
            dk_acc[...] = jnp.zeros_like(dk_acc)
            dv_acc[...] = jnp.zeros_like(dv_acc)

        def step(masked):
            dob = do_ref[...]
            rows = pl.ds(pl.multiple_of(i * t, t), t)
            wide = lambda ref: jnp.concatenate([ref[...]] * (t // BLK), axis=1)
            for e, (q_ref, k_ref, kt_ref, v_ref, lse_ref, dl_ref) in enumerate(
                    ((q0, k0, kt0, v0, lse0, dl0), (q1, k1, kt1, v1, lse1, dl1))):
                s = _dot(q_ref[...], kt_ref[...], 1, 0)
                if masked:
                    s = jnp.where(_fox_mask(i, j, t), s, NEG)
                pe = jnp.exp(s - wide(lse_ref))
                dp = _dot(dob, v_ref[...], 1, 0)
                ds = (pe * (dp - wide(dl_ref))).astype(BF16)
                dq_acc[e, rows, :] += _dot(ds, k_ref[...], 1, 0)
                dk_acc[e] += _dot(ds, q_ref[...], 0, 0)
                dv_acc[e] += _dot(pe.astype(BF16), dob, 0, 0)

        pl.when((i > j) & (j > 0))(lambda: step(False))
        pl.when((i == j) | ((j == 0) & (i > 0)))(lambda: step(True))

        @pl.when(i == nt - 1)
        def _():
            dk_ref[...] = jnp.where(lo, dk_acc[0], dk_acc[1]).astype(BF16)
            dv_ref[...] = jnp.where(lo, dv_acc[0], dv_acc[1]).astype(BF16)
            dkx_ref[...] = jnp.where(lo, dk_acc[1], dk_acc[0])

        @pl.when(s_ == ns - 1)
        def _():
            dq_ref[...] = (jnp.where(lo, dq_acc[0], dq_acc[1]) * SCALE).astype(BF16)
            dqx_ref[...] = jnp.where(lo, dq_acc[1], dq_acc[0])

        if scatter:
            pl.when((p_ == NH // 2 - 1) & (s_ == ns - 1))(finish)

    qs = lambda e: pl.BlockSpec((None, t, BLK), lambda p, s, ii, jj: (2 * p + e, ii[s], 0))
    ks = lambda e: pl.BlockSpec((None, t, BLK), lambda p, s, ii, jj: (2 * p + e, jj[s], 0))
    kts = lambda e: pl.BlockSpec((None, BLK, t), lambda p, s, ii, jj: (2 * p + e, 0, jj[s]))
    qside = pl.BlockSpec((t, BLK), lambda p, s, ii, jj: (ii[s], p))
    kside = pl.BlockSpec((t, BLK), lambda p, s, ii, jj: (jj[s], p))
    rep = pl.BlockSpec((None, t, BLK), lambda p, s, ii, jj: (p, ii[s], 0))
    whole = pl.BlockSpec((T, BLK), lambda p, s, ii, jj: (0, p))
    sums = list(scatter[0]) if scatter else []
    return pl.pallas_call(
        body,
        grid_spec=pltpu.PrefetchScalarGridSpec(
            num_scalar_prefetch=2, grid=(NH // 2, ns),
            in_specs=[qs(0), qs(1), ks(0), ks(1), kts(0), kts(1), kts(0), kts(1), qside, rep, rep, rep, rep] + [ANY] * ng,
            out_specs=[whole, whole, kside, kside, kside] + [ANY] * ng,
            scratch_shapes=[pltpu.VMEM((2, T, BLK), F32), pltpu.VMEM((2, t, BLK), F32), pltpu.VMEM((2, t, BLK), F32)]
            + ([pltpu.SemaphoreType.DMA((3 * ng,)), pltpu.SemaphoreType.DMA((3 * ng,))] if scatter else [])),
        out_shape=[jax.ShapeDtypeStruct((T, 512), BF16), jax.ShapeDtypeStruct((T, 512), F32),
                   jax.ShapeDtypeStruct((T, 512), BF16), jax.ShapeDtypeStruct((T, 512), BF16),
                   jax.ShapeDtypeStruct((T, 512), F32)] + (_scatter_shapes(sums, scatter[1]) if scatter else []),
        compiler_params=(pltpu.CompilerParams(dimension_semantics=("arbitrary",) * 2, vmem_limit_bytes=VMEM_LIMIT,
                                              has_side_effects=True) if scatter
                         else _cp("parallel", "arbitrary")), name=name)(
            i_of, j_of, qaug, qaug, kaug, kaug, kaug_t, kaug_t, vm_t, vm_t, do, *lses, *deltas, *sums)


def _fox_mask(i, j, t):
    row = i * t + lax.broadcasted_iota(jnp.int32, (t, t), 0)
    col = j * t + lax.broadcasted_iota(jnp.int32, (t, t), 1)
    return (col <= row) & (col >= NPAD)


def _lane_pick(x, lane, idx):
    return jnp.sum(jnp.where(lane == idx, x, 0.0), axis=-1, keepdims=True)


def _lru_gates(xc, wr_ref, wi_ref, vec_ref):
    xb = xc.astype(BF16)
    pre_r = jnp.concatenate([_dot(xb[:, p * BLK:(p + 1) * BLK], wr_ref[p], 1, 0) for p in range(LW // BLK)], axis=1)
    pre_i = jnp.concatenate([_dot(xb[:, p * BLK:(p + 1) * BLK], wi_ref[p], 1, 0) for p in range(LW // BLK)], axis=1)
    r = _sigmoid(pre_r + vec_ref[0:1, :])
    gi = _sigmoid(pre_i + vec_ref[1:2, :])
    log_a = LRU_C * r * _log_sigmoid(vec_ref[2:3, :])
    a = jnp.exp(log_a)
    mult = jnp.sqrt(_neg_expm1(2.0 * log_a))
    return r, gi, a, mult


def _conv(xbuf_ref, x, cw_ref, vec_ref, tr):
    return (cw_ref[3:4, :] * x + cw_ref[2:3, :] * xbuf_ref[7:7 + tr, :] + cw_ref[1:2, :] * xbuf_ref[6:6 + tr, :]
            + cw_ref[0:1, :] * xbuf_ref[5:5 + tr, :] + vec_ref[3:4, :])


def _lru_fwd(proj, cw, wr, wi, vec, name):
    T = proj.shape[0]
    tr = _rt(T)

    def body(x_ref, y_ref, cw_ref, wr_ref, wi_ref, vec_ref, oc_ref, oct_ref, hs_ref, xbuf, abuf, bbuf, hcar):
        i = pl.program_id(0)

        @pl.when(i == 0)
        def _():
            xbuf[0:8, :] = jnp.zeros((8, LW), F32)
            hcar[...] = jnp.zeros_like(hcar)

        x = x_ref[...]
        xbuf[8:8 + tr, :] = x
        xc = _conv(xbuf, x, cw_ref, vec_ref, tr)
        xbuf[0:8, :] = x[tr - 8:tr, :]
        _, gi, a, mult = _lru_gates(xc, wr_ref, wi_ref, vec_ref)
        rows = i * tr + lax.broadcasted_iota(jnp.int32, (tr, 1), 0)
        abuf[...] = a
        bbuf[...] = jnp.where(rows >= NPAD, mult * (gi * xc), 0.0)
        sub = lax.broadcasted_iota(jnp.int32, (8, 1), 0)

        def step(k, h):
            sl = pl.ds(pl.multiple_of(k * 8, 8), 8)
            a8, b8 = abuf[sl, :], bbuf[sl, :]
            for s in (1, 2, 4):
                ok = sub >= s
                b8 = jnp.where(ok, a8 * pltpu.roll(b8, s, 0) + b8, b8)
                a8 = jnp.where(ok, a8 * pltpu.roll(a8, s, 0), a8)
            h8 = a8 * h + b8
            bbuf[sl, :] = h8
            return h8[7:8, :]

        hcar[...] = lax.fori_loop(0, tr // 8, step, hcar[...])
        hs = bbuf[...]
        hs_ref[...] = hs
        oc = (hs * _gelu(y_ref[...])).astype(BF16)
        oc_ref[...] = oc
        oct_ref[...] = oc.T

    row = pl.BlockSpec((tr, LW), lambda i: (i, 0))
    full = lambda shape: pl.BlockSpec(shape, lambda i: (0,) * len(shape))
    return pl.pallas_call(
        body, grid=(T // tr,),
        in_specs=[pl.BlockSpec((tr, LW), lambda i: (i, XC // LW)), pl.BlockSpec((tr, LW), lambda i: (i, YC // LW)),
                  full((4, LW)), full((4, BLK, BLK)), full((4, BLK, BLK)), full((8, LW))],
        out_specs=[row, pl.BlockSpec((LW, tr), lambda i: (0, i)), row],
        out_shape=[jax.ShapeDtypeStruct((T, LW), BF16), jax.ShapeDtypeStruct((LW, T), BF16), jax.ShapeDtypeStruct((T, LW), F32)],
        scratch_shapes=[pltpu.VMEM((tr + 8, LW), F32), pltpu.VMEM((tr, LW), F32), pltpu.VMEM((tr, LW), F32),
                        pltpu.VMEM((1, LW), F32)],
        compiler_params=_cp("arbitrary"), name=name)(proj, proj, cw, wr, wi, vec)


def _lru_bwd(proj, hs, doc, cw, wr, wi, vec, name):
    T = proj.shape[0]
    tr = _rt(T)
    nt = T // tr
    r8 = tr // 8

    def body(x_ref, xp_ref, y_ref, hs_ref, hp_ref, do_ref, cw_ref, wr_ref, wi_ref, vec_ref,
             dx_ref, dy_ref, dwr_ref, dwi_ref, dvec_ref, xbuf, abuf, gbuf, hbuf, dbuf, gcar, acar):
        k = pl.program_id(0)
        i = nt - 1 - k

        @pl.when(k == 0)
        def _():
            dwr_ref[...] = jnp.zeros_like(dwr_ref)
            dwi_ref[...] = jnp.zeros_like(dwi_ref)
            dvec_ref[...] = jnp.zeros_like(dvec_ref)
            gcar[...] = jnp.zeros_like(gcar)
            acar[...] = jnp.zeros_like(acar)
            dbuf[tr:tr + 8, :] = jnp.zeros((8, LW), F32)

        first = i == 0
        x = x_ref[...]
        xbuf[0:8, :] = jnp.where(first, 0.0, xp_ref[...])
        xbuf[8:8 + tr, :] = x
        xc = _conv(xbuf, x, cw_ref, vec_ref, tr)
        r, gi, a, mult = _lru_gates(xc, wr_ref, wi_ref, vec_ref)
        y = y_ref[...]
        hs = hs_ref[...]
        do_ = do_ref[...].astype(F32)
        rows = i * tr + lax.broadcasted_iota(jnp.int32, (tr, 1), 0)
        abuf[0:tr, :] = a
        abuf[tr:tr + 8, :] = jnp.zeros((8, LW), F32) + acar[...]
        an = abuf[1:1 + tr, :]
        acar[...] = a[0:1, :]
        abuf[0:tr, :] = an
        gbuf[...] = do_ * _gelu(y)
        sub = lax.broadcasted_iota(jnp.int32, (8, 1), 0)

        def step(kk, g):
            sl = pl.ds(pl.multiple_of((r8 - 1 - kk) * 8, 8), 8)
            a8, b8 = abuf[sl, :], gbuf[sl, :]
            for s in (1, 2, 4):
                ok = sub < 8 - s
                b8 = jnp.where(ok, a8 * pltpu.roll(b8, 8 - s, 0) + b8, b8)
                a8 = jnp.where(ok, a8 * pltpu.roll(a8, 8 - s, 0), a8)
            g8 = a8 * g + b8
            gbuf[sl, :] = g8
            return g8[0:1, :]

        gcar[...] = lax.fori_loop(0, r8, step, gcar[...])
        g = gbuf[...]
        hbuf[0:8, :] = jnp.where(first, 0.0, hp_ref[...])
        hbuf[8:8 + tr, :] = hs
        hprev = hbuf[7:7 + tr, :]
        dinp = jnp.where(rows >= NPAD, g, 0.0)
        da = g * hprev
        dmult = dinp * gi * xc
        dgi = dinp * mult * xc
        dxc = dinp * mult * gi
        dlog_a = da * a - dmult * a * a / mult
        ls = _log_sigmoid(vec_ref[2:3, :])
        dpre_r = dlog_a * (LRU_C * ls) * r * (1.0 - r)
        dpre_i = dgi * gi * (1.0 - gi)
        xb = xc.astype(BF16)
        rb, ib = dpre_r.astype(BF16), dpre_i.astype(BF16)
        back = []
        for p in range(LW // BLK):
            c = slice(p * BLK, (p + 1) * BLK)
            back.append(_dot(rb[:, c], wr_ref[p], 1, 1) + _dot(ib[:, c], wi_ref[p], 1, 1))
            dwr_ref[p] += _dot(xb[:, c], rb[:, c], 0, 0)
            dwi_ref[p] += _dot(xb[:, c], ib[:, c], 0, 0)
        dxc = dxc + jnp.concatenate(back, axis=1)
        col = lambda v: jnp.sum(v, axis=0, keepdims=True)
        dvec_ref[0:1, :] += col(dpre_r)
        dvec_ref[1:2, :] += col(dpre_i)
        dvec_ref[2:3, :] += col(dlog_a * (LRU_C * r)) * _sigmoid(-vec_ref[2:3, :])
        dvec_ref[3:4, :] += col(dxc)
        dvec_ref[4:5, :] += col(dxc * xbuf[5:5 + tr, :])
        dvec_ref[5:6, :] += col(dxc * xbuf[6:6 + tr, :])
        dvec_ref[6:7, :] += col(dxc * xbuf[7:7 + tr, :])
        dvec_ref[7:8, :] += col(dxc * x)
        dbuf[0:tr, :] = dxc
        dxr = (cw_ref[3:4, :] * dxc + cw_ref[2:3, :] * dbuf[1:1 + tr, :] + cw_ref[1:2, :] * dbuf[2:2 + tr, :]
               + cw_ref[0:1, :] * dbuf[3:3 + tr, :])
        dbuf[tr:tr + 8, :] = dxc[0:8, :]
        dx_ref[...] = jnp.where(rows >= NPAD, dxr, 0.0).astype(BF16)
        dy_ref[...] = (do_ * hs * _gelu_grad(y)).astype(BF16)

    rev = lambda k: nt - 1 - k
    row = lambda col0: pl.BlockSpec((tr, LW), lambda k: (rev(k), col0))
    prev8 = lambda col0: pl.BlockSpec((8, LW), lambda k: (jnp.maximum(rev(k) * r8 - 1, 0), col0))
    full = lambda shape: pl.BlockSpec(shape, lambda k: (0,) * len(shape))
    return pl.pallas_call(
        body, grid=(nt,),
        in_specs=[row(XC // LW), prev8(XC // LW), row(YC // LW), row(0), prev8(0), row(0),
                  full((4, LW)), full((4, BLK, BLK)), full((4, BLK, BLK)), full((8, LW))],
        out_specs=[row(0), row(0), full((4, BLK, BLK)), full((4, BLK, BLK)), full((8, LW))],
        out_shape=[jax.ShapeDtypeStruct((T, LW), BF16), jax.ShapeDtypeStruct((T, LW), BF16),
                   jax.ShapeDtypeStruct((4, BLK, BLK), F32), jax.ShapeDtypeStruct((4, BLK, BLK), F32),
                   jax.ShapeDtypeStruct((8, LW), F32)],
        scratch_shapes=[pltpu.VMEM((tr + 8, LW), F32), pltpu.VMEM((tr + 8, LW), F32), pltpu.VMEM((tr, LW), F32),
                        pltpu.VMEM((tr + 8, LW), F32), pltpu.VMEM((tr + 8, LW), F32),
                        pltpu.VMEM((1, LW), F32), pltpu.VMEM((1, LW), F32)],
        compiler_params=_cp("arbitrary"), name=name)(proj, proj, proj, hs, hs, doc, cw, wr, wi, vec)


def _branch_merge_fwd(oa, of, oc, wb, proj, name):
    T = proj.shape[0]
    tm, tn = _rt(T), 512

    def body(a0, a1, a2, w_ref, g0, g1, g2, r0, r1, r2, m_ref, mt_ref):
        acc = None
        for g, (a_ref, g_ref, r_ref) in enumerate(((a0, g0, r0), (a1, g1, r1), (a2, g2, r2))):
            b = _dot(a_ref[...], w_ref[g], 1, 0)
            r_ref[...] = b
            term = _sigmoid(g_ref[...]) * b
            acc = term if acc is None else acc + term
        m = acc.astype(BF16)
        m_ref[...] = m
        mt_ref[...] = m.T

    act = pl.BlockSpec((tm, LW), lambda j, i: (i, 0))
    gate = lambda g: pl.BlockSpec((tm, tn), lambda j, i: (i, (GT + g * D) // tn + j))
    blk = pl.BlockSpec((tm, tn), lambda j, i: (i, j))
    return pl.pallas_call(
        body, grid=(D // tn, T // tm),
        in_specs=[act, act, act, pl.BlockSpec((3, LW, tn), lambda j, i: (0, 0, j)), gate(0), gate(1), gate(2)],
        out_specs=[blk] * 4 + [pl.BlockSpec((tn, tm), lambda j, i: (j, i))],
        out_shape=[jax.ShapeDtypeStruct((T, D), F32)] * 3 + [jax.ShapeDtypeStruct((T, D), BF16), jax.ShapeDtypeStruct((D, T), BF16)],
        compiler_params=_cp("parallel", "parallel"), name=name)(oa, of, oc, wb, proj, proj, proj)


def _out_dx_merge_bwd(dhb, w_out, proj, b0, b1, b2, name):
    T = proj.shape[0]
    tm, tn = _rt(T), 512

    def body(dh_ref, w_ref, g0, g1, g2, r0, r1, r2, d0, d1, d2, e0, e1, e2):
        dmv = _dot(dh_ref[...], w_ref[...], 1, 1)
        for g_ref, r_ref, d_ref, e_ref in ((g0, r0, d0, e0), (g1, r1, d1, e1), (g2, r2, d2, e2)):
            sg = _sigmoid(g_ref[...])
            d_ref[...] = (dmv * sg).astype(BF16)
            e_ref[...] = (dmv * r_ref[...] * sg * (1.0 - sg)).astype(BF16)

    gate = lambda g: pl.BlockSpec((tm, tn), lambda j, i: (i, (GT + g * D) // tn + j))
    blk = pl.BlockSpec((tm, tn), lambda j, i: (i, j))
    return pl.pallas_call(
        body, grid=(D // tn, T // tm),
        in_specs=[pl.BlockSpec((tm, D), lambda j, i: (i, 0)), pl.BlockSpec((tn, D), lambda j, i: (j, 0)),
                  gate(0), gate(1), gate(2), blk, blk, blk],
        out_specs=[blk] * 6, out_shape=[jax.ShapeDtypeStruct((T, D), BF16)] * 6,
        compiler_params=_cp("parallel", "parallel"), name=name)(dhb, w_out, proj, proj, proj, b0, b1, b2)


def _ffn_in_swiglu_fwd(u, w, name):
    T = u.shape[0]
    tm, tn = _rt(T), _pick(DFF, (1408, 256))
    nj = DFF // tn

    def body(u_ref, wg_ref, wu_ref, g_ref, up_ref, a_ref, at_ref):
        ub = u_ref[...]
        g = _dot(ub, wg_ref[...], 1, 0)
        up = _dot(ub, wu_ref[...], 1, 0)
        g_ref[...] = g
        up_ref[...] = up
        a = (g * _sigmoid(g) * up).astype(BF16)
        a_ref[...] = a
        at_ref[...] = a.T

    blk = pl.BlockSpec((tm, tn), lambda j, i: (i, j))
    return pl.pallas_call(
        body, grid=(nj, T // tm),
        in_specs=[pl.BlockSpec((tm, D), lambda j, i: (i, 0)), pl.BlockSpec((D, tn), lambda j, i: (0, j)),
                  pl.BlockSpec((D, tn), lambda j, i: (0, j + nj))],
        out_specs=[blk] * 3 + [pl.BlockSpec((tn, tm), lambda j, i: (j, i))],
        out_shape=[jax.ShapeDtypeStruct((T, DFF), F32)] * 2 + [jax.ShapeDtypeStruct((T, DFF), BF16),
                                                               jax.ShapeDtypeStruct((DFF, T), BF16)],
        compiler_params=_cp("parallel", "parallel"), name=name)(u, w, w)


def _ffn_out_dx_swiglu_bwd(dhb, w, gate, up, name):
    T = dhb.shape[0]
    tm, tn = _rt(T), _pick(DFF, (1408, 256))

    def body(dh_ref, w_ref, g_ref, up_ref, dg_ref, du_ref):
        d = _dot(dh_ref[...], w_ref[...], 1, 1)
        g = g_ref[...]
        sg = _sigmoid(g)
        dg_ref[...] = (d * up_ref[...] * (sg + g * sg * (1.0 - sg))).astype(BF16)
        du_ref[...] = (d * g * sg).astype(BF16)

    blk = pl.BlockSpec((tm, tn), lambda j, i: (i, j))
    return pl.pallas_call(
        body, grid=(DFF // tn, T // tm),
        in_specs=[pl.BlockSpec((tm, D), lambda j, i: (i, 0)), pl.BlockSpec((tn, D), lambda j, i: (j, 0)), blk, blk],
        out_specs=[blk] * 2, out_shape=[jax.ShapeDtypeStruct((T, DFF), BF16)] * 2,
        compiler_params=_cp("parallel", "parallel"), name=name)(dhb, w, gate, up)


def _adamw(w, g, m, v, name):
    R, C = w.shape
    tr = _pick(R, tuple(t for t in (512, 256, 128, 64, 32, 16, 8) if t * C * 4 <= (3 << 19)))
    c1 = 1.0 - ADAM_B1 ** ADAM_STEP
    c2 = 1.0 - ADAM_B2 ** ADAM_STEP

    def body(w_ref, g_ref, m_ref, v_ref, d_ref, mo_ref, vo_ref):
        gv = g_ref[...]
        mn = ADAM_B1 * m_ref[...] + (1.0 - ADAM_B1) * gv
        vn = ADAM_B2 * v_ref[...] + (1.0 - ADAM_B2) * (gv * gv)
        d_ref[...] = -ADAM_LR * ((mn / c1) / (jnp.sqrt(vn / c2) + ADAM_EPS) + ADAM_WD * w_ref[...])
        mo_ref[...] = mn
        vo_ref[...] = vn

    blk = pl.BlockSpec((tr, C), lambda i: (i, 0))
    return pl.pallas_call(
        body, grid=(R // tr,), in_specs=[blk] * 4, out_specs=[blk] * 3,
        out_shape=[jax.ShapeDtypeStruct((R, C), F32)] * 3, compiler_params=_cp("parallel"), name=name)(w, g, m, v)


def _sum_lead(x, name):
    n, R, C = x.shape
    tr = _pick(R, (512, 256, 128, 64, 32, 16, 8))

    def body(x_ref, o_ref):
        acc = x_ref[0]
        for d in range(1, n):
            acc = acc + x_ref[d]
        o_ref[...] = acc

    return pl.pallas_call(
        body, grid=(R // tr,), in_specs=[pl.BlockSpec((n, tr, C), lambda i: (0, i, 0))],
        out_specs=pl.BlockSpec((tr, C), lambda i: (i, 0)), out_shape=jax.ShapeDtypeStruct((R, C), F32),
        compiler_params=_cp("parallel"), name=name)(x)


def _here():
    return lax.axis_index("x"), lax.axis_index("y"), lax.axis_index("c")


def _rcopy(src, dst, send_sems, recv_sems, k, to):
    return pltpu.make_async_remote_copy(src_ref=src, dst_ref=dst, send_sem=send_sems.at[k], recv_sem=recv_sems.at[k],
                                        device_id=to, device_id_type=MESH)


def _hbm_calls(body, args, out_shapes, n_sems, aliases, name):
    return pl.pallas_call(
        body, in_specs=[ANY] * len(args), out_specs=[ANY] * len(out_shapes), out_shape=out_shapes,
        input_output_aliases=aliases,
        scratch_shapes=[pltpu.SemaphoreType.DMA((n_sems,)), pltpu.SemaphoreType.DMA((n_sems,))],
        compiler_params=pltpu.CompilerParams(has_side_effects=True), name=name)(*args)


def _gather_plan(outs, axes, send_sems, recv_sems):
    x, y, c = _here()
    sib = (x, y, 1 - c)
    chips = [(1 - x, y), (x, 1 - y), (1 - x, 1 - y)]
    todo = [(t, k, chip) for t in range(len(outs)) for k, chip in enumerate(chips)]

    def win(t, chip, hc):
        o, ax = outs[t], axes[t]
        w = o.shape[ax] // N_SHARD
        first = (2 * chip[0] + chip[1]) * w
        if ax == 0:
            return o.at[pl.ds(pl.multiple_of(first + hc * (w // 2), 16), w // 2), :]
        rows = o.shape[0] // 2
        return o.at[pl.ds(pl.multiple_of(hc * rows, 16), rows), pl.ds(pl.multiple_of(first, BLK), w)]

    def copy(t, k, chip, hc, to):
        return _rcopy(win(t, chip, hc), win(t, chip, hc), send_sems, recv_sems, 6 * t + k, to)

    def start():
        for t, k, chip in todo:
            copy(t, k, (x, y), c, (*chip, c)).start()

    def finish():
        for t, k, chip in todo:
            copy(t, k, chip, c, (*chip, c)).wait_recv()
            copy(t, 3 + k, chip, c, sib).start()
        for t, k, chip in todo:
            copy(t, 3 + k, chip, 1 - c, sib).wait_recv()
        for t, k, chip in todo:
            copy(t, k, (x, y), c, (*chip, c)).wait_send()
            copy(t, 3 + k, chip, c, sib).wait_send()

    return start, finish


def _all_gather_weights(fulls, axes, name):
    nt = len(fulls)

    def body(*refs):
        start, finish = _gather_plan(refs[nt:2 * nt], axes, *refs[2 * nt:])
        start()
        finish()

    return _hbm_calls(body, fulls, [jax.ShapeDtypeStruct(f.shape, f.dtype) for f in fulls], 6 * nt,
                      {t: t for t in range(nt)}, name)


def _half(ref, ax, hc):
    n = ref.shape[ax] // 2
    sl = pl.ds(pl.multiple_of(hc * n, 8), n)
    return ref.at[sl, :] if ax == 0 else ref.at[:, sl]


def _shrunk(shape, ax, by):
    shape = list(shape)
    shape[ax] //= by
    return tuple(shape)


def _swap_halves(gs, haxes, name):
    nt = len(gs)

    def body(*refs):
        ins, outs, (send_sems, recv_sems) = refs[:nt], refs[nt:2 * nt], refs[2 * nt:]
        x, y, c = _here()
        cps = [_rcopy(_half(g, ax, 1 - c), o, send_sems, recv_sems, t, (x, y, 1 - c))
               for t, (g, o, ax) in enumerate(zip(ins, outs, haxes))]
        for cp in cps:
            cp.start()
        for cp in cps:
            cp.wait()

    return _hbm_calls(body, gs, [jax.ShapeDtypeStruct(_shrunk(g.shape, ax, 2), g.dtype) for g, ax in zip(gs, haxes)],
                      nt, {}, name)


def _scatter_plan(ins, outs, saxes, send_sems, recv_sems):
    x, y, c = _here()
    chips = [(1 - x, y), (x, 1 - y), (1 - x, 1 - y)]

    def copies():
        cps = []
        for t, (s, o, ax) in enumerate(zip(ins, outs, saxes)):
            w = s.shape[ax] // N_SHARD
            for k, chip in enumerate(chips):
                first = pl.multiple_of((2 * chip[0] + chip[1]) * w, 8)
                src = s.at[pl.ds(first, w), :] if ax == 0 else s.at[:, pl.ds(first, w)]
                cps.append(_rcopy(src, o.at[k], send_sems, recv_sems, 3 * t + k, (*chip, c)))
        return cps

    def start():
        for cp in copies():
            cp.start()

    def finish():
        for cp in copies():
            cp.wait()

    return start, finish


def _scatter_shapes(sbs, saxes):
    return [jax.ShapeDtypeStruct((3,) + _shrunk(s.shape, ax, N_SHARD), s.dtype) for s, ax in zip(sbs, saxes)]


def _small_gather_copies(out_ref, send_sems, recv_sems, k0):
    x, y, c = _here()
    me = 4 * x + 2 * y + c
    cps = []
    for k in range(1, 8):
        to = (x ^ ((k >> 2) & 1), y ^ ((k >> 1) & 1), c ^ (k & 1))
        peer = 4 * to[0] + 2 * to[1] + to[2]
        cps.append((_rcopy(out_ref.at[me], out_ref.at[me], send_sems, recv_sems, k0 + k - 1, to),
                    _rcopy(out_ref.at[peer], out_ref.at[peer], send_sems, recv_sems, k0 + k - 1, to)))
    return cps


def _scatter_to_chips(sbs, saxes, small, name):
    nt = len(sbs)

    def body(*refs):
        small_out = refs[2 * nt + 1]
        send_sems, recv_sems = refs[2 * nt + 2:]
        start, finish = _scatter_plan(refs[:nt], refs[nt + 1:2 * nt + 1], saxes, send_sems, recv_sems)
        start()
        cps = _small_gather_copies(small_out, send_sems, recv_sems, 3 * nt)
        for snd, _ in cps:
            snd.start()
        for _, rcv in cps:
            rcv.wait_recv()
        for snd, _ in cps:
            snd.wait_send()
        finish()

    outs = _hbm_calls(body, list(sbs) + [small], _scatter_shapes(sbs, saxes) + [jax.ShapeDtypeStruct(small.shape, small.dtype)],
                      3 * nt + 7, {nt: nt}, name)
    return outs[:nt], outs[nt]


def _join_halves(fins, haxes, name):
    nt = len(fins)

    def body(*refs):
        outs, (send_sems, recv_sems) = refs[nt:2 * nt], refs[2 * nt:]
        x, y, c = _here()
        cps = [_rcopy(_half(o, ax, c), _half(o, ax, c), send_sems, recv_sems, t, (x, y, 1 - c))
               for t, (o, ax) in enumerate(zip(outs, haxes))]
        for cp in cps:
            cp.start()
        for t, (o, ax) in enumerate(zip(outs, haxes)):
            _rcopy(_half(o, ax, 1 - c), _half(o, ax, 1 - c), send_sems, recv_sems, t, (x, y, 1 - c)).wait_recv()
        for cp in cps:
            cp.wait_send()

    return _hbm_calls(body, fins, [jax.ShapeDtypeStruct(f.shape, f.dtype) for f in fins], nt, {t: t for t in range(nt)}, name)


def _all_gather_small(buf, name):
    def body(_, out_ref, send_sems, recv_sems):
        cps = _small_gather_copies(out_ref, send_sems, recv_sems, 0)
        for snd, _ in cps:
            snd.start()
        for _, rcv in cps:
            rcv.wait_recv()
        for snd, _ in cps:
            snd.wait_send()

    return _hbm_calls(body, [buf], [jax.ShapeDtypeStruct(buf.shape, buf.dtype)], 7, {0: 0}, name)[0]


def _place(block, n, index):
    buf = jnp.zeros((n,) + block.shape[1:], block.dtype)
    return lax.dynamic_update_slice_in_dim(buf, block, index, axis=0)


def _add_half(g, other, hax, cidx, name):
    r, cw = other.shape
    tr = _pick(r, tuple(t for t in (512, 256, 128, 64, 32, 16, 8) if t * cw * 4 <= (1 << 21)))
    nr = r // tr

    def body(c_ref, g_ref, o_ref, s_ref, sb_ref):
        s = g_ref[...] + o_ref[...]
        s_ref[...] = s
        sb_ref[...] = s.astype(BF16)

    g_map = (lambda i, c: (c[0] * nr + i, 0)) if hax == 0 else (lambda i, c: (i, c[0]))
    blk = pl.BlockSpec((tr, cw), lambda i, c: (i, 0))
    return pl.pallas_call(
        body,
        grid_spec=pltpu.PrefetchScalarGridSpec(
            num_scalar_prefetch=1, grid=(nr,), in_specs=[pl.BlockSpec((tr, cw), g_map), blk], out_specs=[blk, blk]),
        out_shape=[jax.ShapeDtypeStruct((r, cw), F32), jax.ShapeDtypeStruct((r, cw), BF16)],
        compiler_params=_cp("parallel"), name=name)(cidx, g, other)


def _add_chips(s, recv, sax, chip_idx, cidx, name):
    _, r, cw = recv.shape
    tr = _pick(r, tuple(t for t in (512, 352, 256, 128, 64, 32, 16, 8) if t * cw * 4 <= (1 << 21)))
    nr = r // tr

    def body(chip_ref, c_ref, s_ref, r_ref, out_ref):
        out_ref[...] = ((s_ref[...] + r_ref[0].astype(F32)) + r_ref[1].astype(F32)) + r_ref[2].astype(F32)

    if sax == 0:
        s_map, o_map, shape = (lambda i, chip, c: (chip[0] * nr + i, 0)), (lambda i, chip, c: (i, c[0])), (r, 2 * cw)
    else:
        s_map, o_map, shape = (lambda i, chip, c: (i, chip[0])), (lambda i, chip, c: (c[0] * nr + i, 0)), (2 * r, cw)
    return pl.pallas_call(
        body,
        grid_spec=pltpu.PrefetchScalarGridSpec(
            num_scalar_prefetch=2, grid=(nr,),
            in_specs=[pl.BlockSpec((tr, cw), s_map), pl.BlockSpec((3, tr, cw), lambda i, chip, c: (0, i, 0))],
            out_specs=pl.BlockSpec((tr, cw), o_map)),
        out_shape=jax.ShapeDtypeStruct(shape, F32), compiler_params=_cp("parallel"), name=name)(chip_idx, cidx, s, recv)


IN_SHARD = IN_COLS // N_SHARD
IN_SLOT = INP // N_SHARD
IN_PIECES = ((0, 512, QA), (512, 640, KA), (640, 768, VA), (768, 1280, QF), (1280, 1792, KF), (1792, 2304, VF),
             (2304, 2312, FL), (2312, 2824, XC), (2824, 3336, YC), (3336, 6408, GT))


def _gathered_to_kernel_cols(w):
    parts, pos = [], 0
    for a, b, k in sorted(IN_PIECES, key=lambda p: p[2]):
        assert k == pos
        while a < b:
            j = a // IN_SHARD
            e = min(b, (j + 1) * IN_SHARD)
            g = j * IN_SLOT + a - j * IN_SHARD
            parts.append(w[..., g:g + e - a])
            pos += e - a
            a = e
    parts.append(jnp.zeros(w.shape[:-1] + (INP - pos,), w.dtype))
    return jnp.concatenate(parts, axis=-1)


def _kernel_to_gathered_cols(w):
    parts = []
    for j in range(N_SHARD):
        lo, hi = j * IN_SHARD, (j + 1) * IN_SHARD
        for a, b, k in IN_PIECES:
            s, e = max(a, lo), min(b, hi)
            if s < e:
                parts.append(w[..., k + s - a:k + e - a])
        parts.append(jnp.zeros(w.shape[:-1] + (IN_SLOT - IN_SHARD,), w.dtype))
    return jnp.concatenate(parts, axis=-1)


def _pair_blocks(w):
    z = jnp.zeros((4, 64, 64), w.dtype)
    w = w.reshape(4, 2, 64, 64)
    top = jnp.concatenate([w[:, 0], z], axis=2)
    bot = jnp.concatenate([z, w[:, 1]], axis=2)
    return jnp.concatenate([top, bot], axis=1)


def _unpair_blocks(w):
    return jnp.stack([w[:, :64, :64], w[:, 64:, 64:]], axis=1).reshape(8, 64, 64)


BIG = ("w_in", "w_branch", "w_out", "w_ffn_in", "w_ffn_out")
TINY = ("conv_w", "meta_tokens")
SMALL = ("rel_bias_table", "norm_mix", "swa_sinks", "fox_forget_bias", "conv_b", "lru_w_r", "lru_b_r", "lru_w_i",
         "lru_b_i", "lru_lambda", "norm_ffn", "norm_final")
SHARD_AXIS = {"conv_w": 2, "meta_tokens": 1}
BIG_AXIS = {"w_in": 2, "w_branch": 2, "w_out": 1, "w_ffn_in": 2, "w_ffn_out": 1}


def _pack(d, names):
    flat = jnp.concatenate([d[n].reshape(-1) for n in names])
    pad = (-flat.shape[0]) % (256 * 128)
    return jnp.concatenate([flat, jnp.zeros((pad,), F32)]).reshape(-1, 128)


def _unpack(buf, names, shapes):
    flat, out, off = buf.reshape(-1), {}, 0
    for n in names:
        sz = int(np.prod(shapes[n]))
        out[n] = flat[off:off + sz].reshape(shapes[n])
        off += sz
    return out


def _layer_layout(n, a):
    if n == "w_in":
        return _gathered_to_kernel_cols(a)
    return a.reshape(3, LW, D) if n == "w_branch" else a


def _local_step(x, tgt, W, placed=None):
    S = x.shape[0]
    T = S + BLK
    tm = _pick(T, (1408, 384, 128))
    bucket = jnp.asarray(_bucket_table())
    bias = _bias_build(W["rel_bias_table"], bucket, "bias_build")
    h = jnp.concatenate([jnp.zeros((NPAD, D), F32), W["meta_tokens"], x], axis=0)
    if placed is None:
        WL = {n: [W[n][l] for l in range(DEPTH)] for n in BIG}
    else:
        WL = {n: [W[n]] + [None] * (DEPTH - 1) for n in BIG}

    saved = []
    for l in range(DEPTH):
        sv = {"h0": h}
        u, u_t = _rms_fwd(h, W["norm_mix"][l], "rms_mix_fwd")
        proj = _mm(u, WL["w_in"][l], tm=tm, tn=512, tk=D, name="mm_in_fwd")
        oa, oa_t = _swa_fwd(proj, bias, W["swa_sinks"][l], "swa_fwd")
        fb = W["fox_forget_bias"][l].reshape(NH, 1)
        qaug, kaug, kaug_t, vm_t, vo = _fox_prep(proj, _cum_fwd(proj, fb, "cum_fwd"), "fox_prep")
        if placed is not None and l + 1 < DEPTH:
            of, of_t, lse0, lse1, *got = _fox_fwd(qaug, kaug_t, vo, "fox_fwd_gather", gather=(placed[l + 1], GATHER_AXES))
            for n, a in zip(BIG, got):
                WL[n][l + 1] = _layer_layout(n, a)
            lse = [lse0, lse1]
        else:
            of, of_t, *lse = _fox_fwd(qaug, kaug_t, vo, "fox_fwd")
        lru_vec = jnp.concatenate([W["lru_b_r"][l][None], W["lru_b_i"][l][None], W["lru_lambda"][l][None],
                                   W["conv_b"][l][None], jnp.zeros((4, LW), F32)], axis=0)
        oc, oc_t, hs = _lru_fwd(proj, W["conv_w"][l], W["lru_w_r"][l], W["lru_w_i"][l], lru_vec, "lru_fwd")
        *bs, merged, merged_t = _branch_merge_fwd(oa, of, oc, WL["w_branch"][l], proj, "branch_merge_fwd")
        h2 = _mm(merged, WL["w_out"][l], res=h, tm=tm, tn=512, tk=D, name="mm_out_fwd")
        u2, u2_t = _rms_fwd(h2, W["norm_ffn"][l], "rms_ffn_fwd")
        gate, up, act, act_t = _ffn_in_swiglu_fwd(u2, WL["w_ffn_in"][l], "ffn_in_swiglu_fwd")
        h = _mm(act, WL["w_ffn_out"][l], res=h2, tm=tm, tn=512, tk=DFF, name="mm_ffn_out_fwd")
        sv.update(u_t=u_t, proj=proj, o_t=(oa_t, of_t, oc_t), of=of, lse=lse, hs=hs, fb=fb, qaug=qaug, kaug=kaug, kaug_t=kaug_t, vm_t=vm_t,
                  lru_vec=lru_vec, bs=bs, merged_t=merged_t, h2=h2, u2_t=u2_t, gate=gate, up=up, act_t=act_t)
        saved.append(sv)

    dh, dhb, dg_final, loss_vec = _loss_head(h, tgt, W["norm_final"], "loss_head")
    loss = loss_vec[0, 0]

    small = ("norm_mix", "swa_sinks", "fox_forget_bias", "conv_w", "conv_b", "lru_w_r", "lru_b_r", "lru_w_i", "lru_b_i",
             "lru_lambda", "norm_ffn")
    G = {n: [None] * DEPTH for n in small}
    G["norm_final"] = dg_final.reshape(D)
    GW = {n: [None] * DEPTH for n in BIG}
    dist = placed is not None
    if dist:
        x_, y_, c_ = _here()
        cidx = jnp.reshape(c_, (1,)).astype(jnp.int32)
        chip = jnp.reshape(2 * x_ + y_, (1,)).astype(jnp.int32)

    def finish_layer(lp, ss, recv):
        fins = [_add_chips(s, r, ax, chip, cidx, "rs_add_chips_" + n) for n, s, r, ax in zip(BIG, ss, recv, GATHER_AXES)]
        for n, f in zip(BIG, _join_halves(fins, HALF_AXES, "rs_join_halves")):
            GW[n][lp] = f

    pend = None
    dbias = jnp.zeros((NH, BLK, 2 * BLK), F32)
    tkT = tm
    for l in reversed(range(DEPTH)):
        sv = saved[l]
        dw = {}
        dw["w_ffn_out"] = _mm(sv["act_t"], dhb, tm=_pick(DFF, (1408, 256)), tn=D, tk=tkT,
                              name="mm_ffn_out_dw")
        dgate, dup = _ffn_out_dx_swiglu_bwd(dhb, WL["w_ffn_out"][l], sv["gate"], sv["up"], "ffn_out_dx_swiglu_bwd")
        u2t = sv["u2_t"]
        du2, buf = None, None
        for half, dpart in enumerate((dgate, dup)):
            buf = _mm(u2t, dpart, tm=D, tn=_pick(DFF, (1408, 256)), tk=tkT, slab=(buf, 0, 1),
                      col0=half * DFF, cols=2 * DFF, name="mm_ffn_in_dw")
            du2 = _mm(dpart, WL["w_ffn_in"][l], tb=True, res=du2, b_k0=half * DFF, tm=tm, tn=512, tk=DFF,
                      name="mm_ffn_in_dx")
        dw["w_ffn_in"] = buf.reshape(D, 2 * DFF)
        dh, dhb, dgn = _rms_bwd(du2, sv["h2"], W["norm_ffn"][l], dh, "rms_ffn_bwd")
        G["norm_ffn"][l] = dgn.reshape(D)
        dw["w_out"] = _mm(sv["merged_t"], dhb, tm=D, tn=D, tk=tkT, name="mm_out_dw")
        db0, db1, db2, dg0, dg1, dg2 = _out_dx_merge_bwd(dhb, WL["w_out"][l], sv["proj"], *sv["bs"], "out_dx_merge_bwd")
        dos, buf = [], None
        for g, (o_t, db) in enumerate(zip(sv["o_t"], (db0, db1, db2))):
            buf = _mm(o_t, db, tm=LW, tn=D, tk=tkT, slab=(buf, g, 3), name="mm_branch_dw")
            dos.append(_mm(db, WL["w_branch"][l][g], tb=True, out_dtype=BF16, tm=tm, tn=LW, tk=D, name="mm_branch_dx"))
        dw["w_branch"] = buf.reshape(3 * LW, D)
        dqa, dkb, dvb, dbias, dsk = _swa_bwd(sv["proj"], bias, W["swa_sinks"][l], dos[0], dbias, "swa_bwd")
        dka, dva = _band_fold(dkb, dvb, "swa_band_fold")
        G["swa_sinks"][l] = dsk[0, :NH]
        delta = _fox_delta(dos[1], sv["of"], "fox_delta")
        fox_args = (sv["qaug"], sv["kaug"], sv["kaug_t"], sv["vm_t"], dos[1], sv["lse"], delta)
        if pend is not None:
            dqf, dqx, dkf, dvf, dkx, *recv = _fox_bwd(*fox_args, "fox_bwd_scatter", scatter=(pend[2], GATHER_AXES))
            finish_layer(pend[0], pend[1], recv)
            pend = None
        else:
            dqf, dqx, dkf, dvf, dkx = _fox_bwd(*fox_args, "fox_bwd")
        dfl, dfb = _cum_bwd(dqx, dkx, sv["proj"], sv["fb"], "cum_bwd")
        G["fox_forget_bias"][l] = dfb[:, 0]
        dxc, dyc, dwr, dwi, dvec = _lru_bwd(sv["proj"], sv["hs"], dos[2], W["conv_w"][l], W["lru_w_r"][l], W["lru_w_i"][l],
                                            sv["lru_vec"], "lru_bwd")
        G["lru_w_r"][l], G["lru_w_i"][l] = _unpair_blocks(dwr), _unpair_blocks(dwi)
        G["lru_b_r"][l], G["lru_b_i"][l], G["lru_lambda"][l], G["conv_b"][l] = dvec[0], dvec[1], dvec[2], dvec[3]
        G["conv_w"][l] = dvec[4:8]
        dproj = jnp.concatenate([dqa, dqf, dkf, dvf, dxc, dyc, dg0, dg1, dg2, dka, dva, dfl], axis=1)
        dw["w_in"] = _mm(sv["u_t"], dproj, tm=D, tn=IN_SLOT, tk=tkT, name="mm_in_dw")
        du = _mm(dproj, WL["w_in"][l], tb=True, tm=tm, tn=512, tk=_pick(INP, (3328, 512)), name="mm_in_dx")
        dh, dhb, dgn = _rms_bwd(du, sv["h0"], W["norm_mix"][l], dh, "rms_mix_bwd")
        G["norm_mix"][l] = dgn.reshape(D)
        if dist:
            gs = [dw[n] for n in BIG]
            pairs = [_add_half(g, r, hax, cidx, "rs_add_half_" + n)
                     for n, g, r, hax in zip(BIG, gs, _swap_halves(gs, HALF_AXES, "rs_swap_halves"), HALF_AXES)]
            ss, sbs = [list(t) for t in zip(*pairs)]
            ss[0], sbs[0] = _kernel_to_gathered_cols(ss[0]), _kernel_to_gathered_cols(sbs[0])
            pend = (l, ss, sbs)
        else:
            for n in BIG:
                GW[n][l] = dw[n]
    grads = {n: (jnp.stack(v) if isinstance(v, list) else v) for n, v in G.items()}
    grads["rel_bias_table"] = _bias_bwd(dbias, bucket, "bias_bwd")
    grads["meta_tokens"] = dh[NPAD:BLK]
    if dist:
        mine = _place(_pack(grads, SMALL + TINY)[None], 8, 4 * x_ + 2 * y_ + c_)
        recv, grads["small_gathered"] = _scatter_to_chips(pend[2], GATHER_AXES, mine, "rs_scatter")
        finish_layer(pend[0], pend[1], recv)
    grads.update({n: jnp.stack(GW[n]) for n in BIG})
    return loss, dh[BLK:], grads


NAMES = ("meta_tokens", "rel_bias_table", "norm_mix", "w_in", "swa_sinks", "fox_forget_bias", "conv_w", "conv_b",
         "lru_w_r", "lru_b_r", "lru_w_i", "lru_b_i", "lru_lambda", "w_branch", "w_out", "norm_ffn", "w_ffn_in",
         "w_ffn_out", "norm_final")


def _three_d(n, a):
    return a.reshape(DEPTH, 3 * LW, -1) if n == "w_branch" else a


GATHER_AXES = [BIG_AXIS[n] - 1 for n in BIG]
HALF_AXES = [1 - a for a in GATHER_AXES]


def _gather_weights(P):
    x, y, c = _here()
    mine, me = 2 * x + y, 4 * x + 2 * y + c
    placed = []
    for l in range(DEPTH):
        bufs = []
        for n in BIG:
            shard = _three_d(n, P[n])[l].astype(BF16)
            if n == "w_in":
                shard = jnp.pad(shard, ((0, 0), (0, IN_SLOT - IN_SHARD)))
            zero = jnp.zeros_like(shard)
            bufs.append(jnp.concatenate([jnp.where(mine == j, shard, zero) for j in range(N_SHARD)], axis=BIG_AXIS[n] - 1))
        placed.append(bufs)
    full = {n: _layer_layout(n, a) for n, a in zip(BIG, _all_gather_weights(placed[0], GATHER_AXES, "ag_weights"))}
    tiny = _all_gather_small(_place(_pack(P, TINY)[None], 8, me), "ag_tiny_weights")
    parts = [_unpack(tiny[2 * j], TINY, {n: P[n].shape for n in TINY}) for j in range(N_SHARD)]
    for n in TINY:
        full[n] = jnp.concatenate([p[n] for p in parts], axis=SHARD_AXIS[n])
    for n in SMALL:
        full[n] = P[n]
    full["lru_w_r"] = jnp.stack([_pair_blocks(P["lru_w_r"][l]) for l in range(DEPTH)]).astype(BF16)
    full["lru_w_i"] = jnp.stack([_pair_blocks(P["lru_w_i"][l]) for l in range(DEPTH)]).astype(BF16)
    return full, placed


def _reduce_grads(grads, P):
    x, y, c = _here()
    mine, me = 2 * x + y, 4 * x + 2 * y + c
    out = {n: grads[n].reshape(P[n].shape) for n in BIG if n != "w_in"}
    out["w_in"] = grads["w_in"][:, :, :IN_SHARD]
    names = SMALL + TINY
    small = _unpack(_sum_lead(grads["small_gathered"], "sum_small_grads"), names, {n: grads[n].shape for n in names})
    for n in SMALL:
        out[n] = small[n]
    for n in TINY:
        w = P[n].shape[SHARD_AXIS[n]]
        out[n] = lax.dynamic_slice_in_dim(small[n], mine * w, w, axis=SHARD_AXIS[n])
    return out


def _update(P, Gd, M, V):
    delta, new_m, new_v = {}, {}, {}
    for n in BIG + TINY:
        shp = P[n].shape
        two = (int(np.prod(shp[:-1])), shp[-1])
        d, m, v = _adamw(P[n].reshape(two), Gd[n].reshape(two), M[n].reshape(two), V[n].reshape(two), "adamw_" + n)
        delta[n], new_m[n], new_v[n] = d.reshape(shp), m.reshape(shp), v.reshape(shp)
    shapes = {n: P[n].shape for n in SMALL}
    d, m, v = _adamw(_pack(P, SMALL), _pack(Gd, SMALL), _pack(M, SMALL), _pack(V, SMALL), "adamw_small")
    for dst, buf in ((delta, d), (new_m, m), (new_v, v)):
        dst.update(_unpack(buf, SMALL, shapes))
    return delta, new_m, new_v


def kernel(x, meta_tokens, rel_bias_table, norm_mix, w_in, swa_sinks, fox_forget_bias, conv_w, conv_b, lru_w_r, lru_b_r, lru_w_i, lru_b_i, lru_lambda, w_branch, w_out, norm_ffn, w_ffn_in, w_ffn_out, norm_final, loss_target, m_meta_tokens, m_rel_bias_table, m_norm_mix, m_w_in, m_swa_sinks, m_fox_forget_bias, m_conv_w, m_conv_b, m_lru_w_r, m_lru_b_r, m_lru_w_i, m_lru_b_i, m_lru_lambda, m_w_branch, m_w_out, m_norm_ffn, m_w_ffn_in, m_w_ffn_out, m_norm_final, v_meta_tokens, v_rel_bias_table, v_norm_mix, v_w_in, v_swa_sinks, v_fox_forget_bias, v_conv_w, v_conv_b, v_lru_w_r, v_lru_b_r, v_lru_w_i, v_lru_b_i, v_lru_lambda, v_w_branch, v_w_out, v_norm_ffn, v_w_ffn_in, v_w_ffn_out, v_norm_final):
    P = dict(zip(NAMES, (meta_tokens, rel_bias_table, norm_mix, w_in, swa_sinks, fox_forget_bias, conv_w, conv_b, lru_w_r,
                         lru_b_r, lru_w_i, lru_b_i, lru_lambda, w_branch, w_out, norm_ffn, w_ffn_in, w_ffn_out, norm_final)))
    M = dict(zip(NAMES, (m_meta_tokens, m_rel_bias_table, m_norm_mix, m_w_in, m_swa_sinks, m_fox_forget_bias, m_conv_w,
                         m_conv_b, m_lru_w_r, m_lru_b_r, m_lru_w_i, m_lru_b_i, m_lru_lambda, m_w_branch, m_w_out, m_norm_ffn,
                         m_w_ffn_in, m_w_ffn_out, m_norm_final)))
    V = dict(zip(NAMES, (v_meta_tokens, v_rel_bias_table, v_norm_mix, v_w_in, v_swa_sinks, v_fox_forget_bias, v_conv_w,
                         v_conv_b, v_lru_w_r, v_lru_b_r, v_lru_w_i, v_lru_b_i, v_lru_lambda, v_w_branch, v_w_out, v_norm_ffn,
                         v_w_ffn_in, v_w_ffn_out, v_norm_final)))
    W, placed = _gather_weights(P)
    loss_local, grad_x, grads = _local_step(x[0], loss_target[0], W, placed)
    loss = lax.psum(loss_local, ("x", "y", "c"))
    Gd = _reduce_grads(grads, P)
    delta, new_m, new_v = _update(P, Gd, M, V)
    return (loss, grad_x[None], *[Gd[n] for n in NAMES], *[delta[n] for n in NAMES],
            *[new_m[n] for n in NAMES], *[new_v[n] for n in NAMES])
```

```python
import math

import numpy as np
import jax
import jax.numpy as jnp
from jax import lax
from jax.experimental import pallas as pl
from jax.experimental.pallas import tpu as pltpu

F32, BF16 = jnp.float32, jnp.bfloat16
MESH = pl.DeviceIdType.MESH
ANY = pl.BlockSpec(memory_space=pl.ANY)
SMEM = pl.BlockSpec(memory_space=pltpu.SMEM)

D = 1024
DEPTH = 4
BLK = 128
N_META = 16
NPAD = 112
NH = 8
LW = 512
DFF = 2816
EPS = 1e-6
NEG = -1e30
SCALE = 0.125
LRU_C = 8.0
REL_BUCKETS = 32
N_SHARD = 4
QA, QF, KF, VF, XC, YC, GT, KA, VA, FL, INP = 0, 512, 1024, 1536, 2048, 2560, 3072, 6144, 6272, 6400, 6656
IN_COLS = 6408
VMEM_LIMIT = 48 * 1024 * 1024

ADAM_LR, ADAM_B1, ADAM_B2, ADAM_EPS, ADAM_WD, ADAM_STEP = 0.001, 0.9, 0.999, 1e-08, 0.01, 10


def _cp(*sem):
    return pltpu.CompilerParams(dimension_semantics=sem or None, vmem_limit_bytes=VMEM_LIMIT)


def _pick(n, prefs):
    for p in prefs:
        if n % p == 0:
            return p
    return n


def _rt(T):
    return _pick(T, (384, 128))


def _sigmoid(z):
    return 1.0 / (1.0 + jnp.exp(-z))


def _log_sigmoid(z):
    return jnp.minimum(z, 0.0) - jnp.log(1.0 + jnp.exp(-jnp.abs(z)))


def _gelu(y):
    c = math.sqrt(2.0 / math.pi)
    return 0.5 * y * (1.0 + jnp.tanh(c * (y + 0.044715 * y * y * y)))


def _gelu_grad(y):
    c = math.sqrt(2.0 / math.pi)
    t = jnp.tanh(c * (y + 0.044715 * y * y * y))
    return 0.5 * (1.0 + t) + 0.5 * y * (1.0 - t * t) * c * (1.0 + 3.0 * 0.044715 * y * y)


def _neg_expm1(z):
    series = -z * (1.0 + z * (0.5 + z * (1.0 / 6.0 + z * (1.0 / 24.0 + z * (1.0 / 120.0)))))
    return jnp.where(z > -0.1, series, 1.0 - jnp.exp(z))


def _dot(a, b, ca, cb):
    return lax.dot_general(a, b, (((ca,), (cb,)), ((), ())), preferred_element_type=F32)


def _mm(a, b, *, ta=False, tb=False, res=None, out_dtype=F32, tm, tn, tk, name, slab=None, b_k0=0, col0=0, cols=None):
    M, K = (a.shape[1], a.shape[0]) if ta else a.shape
    N = b.shape[0] if tb else b.shape[1]
    assert (b.shape[1] if tb else b.shape[0]) >= K + b_k0 and M % tm == 0 and N % tn == 0 and K % tk == 0, (name, a.shape, b.shape)
    assert b_k0 % tk == 0 and col0 % tn == 0
    nk, kb, jb = K // tk, b_k0 // tk, col0 // tn
    ca, cb = (0 if ta else 1), (1 if tb else 0)
    n_in = 2 + (res is not None) + (slab is not None and slab[0] is not None)

    def body(*refs):
        a_ref, b_ref = refs[:2]
        r_ref = refs[2] if res is not None else None
        o_ref = refs[n_in]
        part = _dot(a_ref[...].astype(BF16), b_ref[...].astype(BF16), ca, cb)

        def fin(acc):
            if res is not None:
                acc = acc + r_ref[...]
            o_ref[...] = acc.astype(out_dtype)

        if nk == 1:
            fin(part)
        else:
            acc_ref = refs[-1]
            k = pl.program_id(2)

            @pl.when(k == 0)
            def _():
                acc_ref[...] = part

            @pl.when(k > 0)
            def _():
                acc_ref[...] += part

            @pl.when(k == nk - 1)
            def _():
                fin(acc_ref[...])

    a_spec = pl.BlockSpec((tk, tm), lambda i, j, k: (k, i)) if ta else pl.BlockSpec((tm, tk), lambda i, j, k: (i, k))
    b_spec = (pl.BlockSpec((tn, tk), lambda i, j, k: (j, k + kb)) if tb
              else pl.BlockSpec((tk, tn), lambda i, j, k: (k + kb, j)))
    o_spec = pl.BlockSpec((tm, tn), lambda i, j, k: (i, j))
    in_specs, ops = [a_spec, b_spec], [a, b]
    if res is not None:
        in_specs.append(o_spec)
        ops.append(res)
    out_shape, aliases = jax.ShapeDtypeStruct((M, N), out_dtype), {}
    if slab is not None:
        buf, idx, n = slab
        o_spec = pl.BlockSpec((None, tm, tn), lambda i, j, k: (idx, i, j + jb))
        out_shape = jax.ShapeDtypeStruct((n, M, cols or N), out_dtype)
        if buf is not None:
            aliases = {len(ops): 0}
            in_specs.append(ANY)
            ops.append(buf)
    return pl.pallas_call(
        body, grid=(M // tm, N // tn, nk), in_specs=in_specs, out_specs=o_spec, out_shape=out_shape,
        input_output_aliases=aliases, scratch_shapes=[pltpu.VMEM((tm, tn), F32)] if nk > 1 else [],
        compiler_params=_cp("parallel", "parallel", "arbitrary"), name=name)(*ops)


def _rms_fwd(h, g, name):
    T = h.shape[0]
    tr = _rt(T)

    def body(h_ref, g_ref, u_ref, ut_ref):
        x = h_ref[...]
        r = lax.rsqrt(jnp.mean(x * x, axis=-1, keepdims=True) + EPS)
        u = (x * r * g_ref[...]).astype(BF16)
        u_ref[...] = u
        ut_ref[...] = u.T

    return pl.pallas_call(
        body, grid=(T // tr,),
        in_specs=[pl.BlockSpec((tr, D), lambda i: (i, 0)), pl.BlockSpec((1, D), lambda i: (0, 0))],
        out_specs=[pl.BlockSpec((tr, D), lambda i: (i, 0)), pl.BlockSpec((D, tr), lambda i: (0, i))],
        out_shape=[jax.ShapeDtypeStruct((T, D), BF16), jax.ShapeDtypeStruct((D, T), BF16)],
        compiler_params=_cp("parallel"), name=name)(h, g.reshape(1, D))


def _rms_bwd(du, h, g, dres, name):
    T = h.shape[0]
    tr = _rt(T)

    def body(du_ref, h_ref, g_ref, dres_ref, dh_ref, dhb_ref, dg_ref):
        x = h_ref[...]
        r = lax.rsqrt(jnp.mean(x * x, axis=-1, keepdims=True) + EPS)
        xh = x * r
        dy = du_ref[...]
        dxh = dy * g_ref[...]
        dx = r * (dxh - xh * jnp.mean(dxh * xh, axis=-1, keepdims=True))
        dh = dres_ref[...] + dx
        dh_ref[...] = dh
        dhb_ref[...] = dh.astype(BF16)
        part = jnp.sum(dy * xh, axis=0, keepdims=True)

        @pl.when(pl.program_id(0) == 0)
        def _():
            dg_ref[...] = part

        @pl.when(pl.program_id(0) > 0)
        def _():
            dg_ref[...] += part

    row = pl.BlockSpec((tr, D), lambda i: (i, 0))
    vec = pl.BlockSpec((1, D), lambda i: (0, 0))
    return pl.pallas_call(
        body, grid=(T // tr,), in_specs=[row, row, vec, row], out_specs=[row, row, vec],
        out_shape=[jax.ShapeDtypeStruct((T, D), F32), jax.ShapeDtypeStruct((T, D), BF16), jax.ShapeDtypeStruct((1, D), F32)],
        compiler_params=_cp("arbitrary"), name=name)(du, h, g.reshape(1, D), dres)


def _loss_head(h, tgt, g, name):
    T = h.shape[0]
    nb = T // BLK

    def body(h_ref, t_ref, g_ref, dh_ref, dhb_ref, dg_ref, loss_ref):
        i = pl.program_id(0)
        x = h_ref[...]
        r = lax.rsqrt(jnp.mean(x * x, axis=-1, keepdims=True) + EPS)
        xh = x * r
        gv = g_ref[...]
        tok = i >= 1
        err = jnp.where(tok, xh * gv - t_ref[...], 0.0)
        dy = err * (1.0 / D)
        dxh = dy * gv
        dx = r * (dxh - xh * jnp.mean(dxh * xh, axis=-1, keepdims=True))
        dh_ref[...] = dx
        dhb_ref[...] = dx.astype(BF16)
        dg = jnp.sum(dy * xh, axis=0, keepdims=True)
        ls = jnp.zeros((1, BLK), F32) + jnp.sum(err * err) * (0.5 / D)

        @pl.when(i == 0)
        def _():
            dg_ref[...] = dg
            loss_ref[...] = ls

        @pl.when(i > 0)
        def _():
            dg_ref[...] += dg
            loss_ref[...] += ls

    row = pl.BlockSpec((BLK, D), lambda i: (i, 0))
    vec = pl.BlockSpec((1, D), lambda i: (0, 0))
    return pl.pallas_call(
        body, grid=(nb,),
        in_specs=[row, pl.BlockSpec((BLK, D), lambda i: (jnp.maximum(i - 1, 0), 0)), vec],
        out_specs=[row, row, vec, pl.BlockSpec((1, BLK), lambda i: (0, 0))],
        out_shape=[jax.ShapeDtypeStruct((T, D), F32), jax.ShapeDtypeStruct((T, D), BF16),
                   jax.ShapeDtypeStruct((1, D), F32), jax.ShapeDtypeStruct((1, BLK), F32)],
        compiler_params=_cp("arbitrary"), name=name)(h, tgt, g.reshape(1, D))


def _bucket_table():
    q = np.arange(BLK)[:, None]
    k = np.arange(2 * BLK)[None, :]
    d = np.maximum(q + BLK - k, 0)
    max_exact = REL_BUCKETS // 2
    scaled = np.log(np.maximum(d, 1).astype(np.float32) / np.float32(max_exact)) / np.float32(math.log(128 / max_exact))
    large = np.minimum(max_exact + (scaled.astype(np.float32) * (REL_BUCKETS - max_exact)).astype(np.int32), REL_BUCKETS - 1)
    return np.where(d < max_exact, d, large).astype(np.int32)


def _bias_build(table, bucket, name):
    def body(t_ref, bk_ref, o_ref):
        bk = bk_ref[...]
        for h in range(NH):
            acc = jnp.zeros((BLK, 2 * BLK), F32)
            for b in range(REL_BUCKETS):
                acc = jnp.where(bk == b, t_ref[b, h], acc)
            o_ref[h] = acc

    return pl.pallas_call(
        body, in_specs=[SMEM, pl.BlockSpec(memory_space=pltpu.VMEM)], out_specs=pl.BlockSpec(memory_space=pltpu.VMEM),
        out_shape=jax.ShapeDtypeStruct((NH, BLK, 2 * BLK), F32), compiler_params=_cp(), name=name)(table, bucket)


def _bias_bwd(dbias, bucket, name):
    def body(d_ref, bk_ref, o_ref):
        bk = bk_ref[...]
        for h in range(NH):
            dh = d_ref[h]
            for b in range(REL_BUCKETS):
                o_ref[b, h] = jnp.sum(jnp.where(bk == b, dh, 0.0))

    return pl.pallas_call(
        body, in_specs=[pl.BlockSpec(memory_space=pltpu.VMEM)] * 2, out_specs=SMEM,
        out_shape=jax.ShapeDtypeStruct((REL_BUCKETS, NH), F32), compiler_params=_cp(), name=name)(dbias, bucket)


def _swa_specs(nq_cols):
    prev = lambda n: jnp.maximum(n - 1, 0)
    return [
        pl.BlockSpec((BLK, nq_cols), lambda n: (n, QA // nq_cols)),
        pl.BlockSpec((BLK, BLK), lambda n: (prev(n), KA // BLK)), pl.BlockSpec((BLK, BLK), lambda n: (n, KA // BLK)),
        pl.BlockSpec((BLK, BLK), lambda n: (prev(n), VA // BLK)), pl.BlockSpec((BLK, BLK), lambda n: (n, VA // BLK)),
    ]


def _swa_mask(n):
    row = lax.broadcasted_iota(jnp.int32, (BLK, 2 * BLK), 0)
    col = lax.broadcasted_iota(jnp.int32, (BLK, 2 * BLK), 1)
    dist = row + BLK - col
    return (dist >= 0) & (dist < BLK) & ((n - 1) * BLK + col >= NPAD)


def _swa_probs(qm, ksel, mask, bias_h, sink):
    s = _dot(qm, ksel, 1, 1) * SCALE
    s = jnp.where(mask, s + bias_h, NEG)
    m = jnp.maximum(jnp.max(s, axis=-1, keepdims=True), sink)
    p = jnp.exp(s - m)
    psink = jnp.exp(sink - m)
    inv = 1.0 / (jnp.sum(p, axis=-1, keepdims=True) + psink)
    return p * inv, psink * inv


def _swa_fwd(proj, bias, sinks, name):
    T = proj.shape[0]
    nb = T // BLK

    def body(sk_ref, q_ref, kp_ref, kc_ref, vp_ref, vc_ref, b_ref, o_ref, ot_ref):
        n = pl.program_id(0)
        lo = lax.broadcasted_iota(jnp.int32, (1, BLK), 1) < 64
        kb = jnp.concatenate([kp_ref[...], kc_ref[...]], axis=0)
        vb = jnp.concatenate([vp_ref[...], vc_ref[...]], axis=0)
        kbs = (kb.astype(BF16), pltpu.roll(kb, 64, 1).astype(BF16))
        vbs = (vb, pltpu.roll(vb, 64, 1))
        mask = _swa_mask(n)
        outs = []
        for pr in range(NH // 2):
            qp = q_ref[:, pr * BLK:(pr + 1) * BLK]
            kv = pr // 2
            acc = jnp.zeros((BLK, BLK), F32)
            for e in range(2):
                lm = lo if e == 0 else jnp.logical_not(lo)
                sw = 0 if kv == e else 1
                qm = jnp.where(lm, qp, 0.0).astype(BF16)
                pn, _ = _swa_probs(qm, kbs[sw], mask, b_ref[2 * pr + e], sk_ref[2 * pr + e])
                acc = acc + _dot(pn.astype(BF16), jnp.where(lm, vbs[sw], 0.0).astype(BF16), 1, 0)
            outs.append(acc)
        o = jnp.concatenate(outs, axis=1).astype(BF16)
        o_ref[...] = o
        ot_ref[...] = o.T

    return pl.pallas_call(
        body, grid=(nb,),
        in_specs=[SMEM] + _swa_specs(512) + [pl.BlockSpec((NH, BLK, 2 * BLK), lambda n: (0, 0, 0))],
        out_specs=[pl.BlockSpec((BLK, 512), lambda n: (n, 0)), pl.BlockSpec((512, BLK), lambda n: (0, n))],
        out_shape=[jax.ShapeDtypeStruct((T, 512), BF16), jax.ShapeDtypeStruct((512, T), BF16)],
        compiler_params=_cp("parallel"), name=name)(sinks, proj, proj, proj, proj, proj, bias)


def _swa_bwd(proj, bias, sinks, do, dbias_in, name):
    T = proj.shape[0]
    nb = T // BLK

    def body(sk_ref, q_ref, kp_ref, kc_ref, vp_ref, vc_ref, b_ref, do_ref, dbi_ref,
             dq_ref, dk_ref, dv_ref, db_ref, dsk_ref, sk_acc):
        n = pl.program_id(0)
        lane = lax.broadcasted_iota(jnp.int32, (1, BLK), 1)
        lo = lane < 64
        kb = jnp.concatenate([kp_ref[...], kc_ref[...]], axis=0)
        vb = jnp.concatenate([vp_ref[...], vc_ref[...]], axis=0)
        kbs = (kb, pltpu.roll(kb, 64, 1))
        vbs = (vb, pltpu.roll(vb, 64, 1))
        mask = _swa_mask(n)

        @pl.when(n == 0)
        def _():
            db_ref[...] = dbi_ref[...]
            sk_acc[...] = jnp.zeros_like(sk_acc)

        dqs = []
        dk = jnp.zeros((2 * BLK, BLK), F32)
        dv = jnp.zeros((2 * BLK, BLK), F32)
        for pr in range(NH // 2):
            qp = q_ref[:, pr * BLK:(pr + 1) * BLK]
            dop = do_ref[:, pr * BLK:(pr + 1) * BLK].astype(F32)
            kv = pr // 2
            dq = jnp.zeros((BLK, BLK), F32)
            for e in range(2):
                h = 2 * pr + e
                lm = lo if e == 0 else jnp.logical_not(lo)
                sw = 0 if kv == e else 1
                qm = jnp.where(lm, qp, 0.0)
                dom = jnp.where(lm, dop, 0.0)
                pn, ps = _swa_probs(qm.astype(BF16), kbs[sw].astype(BF16), mask, b_ref[h], sk_ref[h])
                dp = _dot(dom.astype(BF16), vbs[sw].astype(BF16), 1, 1)
                delta = jnp.sum(pn * dp, axis=-1, keepdims=True)
                ds = pn * (dp - delta)
                db_ref[h] += ds
                sk_acc[...] += jnp.where(lane == h, -(ps * delta), 0.0)
                dsb = (ds * SCALE).astype(BF16)
                dq = dq + _dot(dsb, jnp.where(lm, kbs[sw], 0.0).astype(BF16), 1, 0)
                qk = qm if sw == 0 else pltpu.roll(qm, 64, 1)
                dok = dom if sw == 0 else pltpu.roll(dom, 64, 1)
                dk = dk + _dot(dsb, qk.astype(BF16), 0, 0)
                dv = dv + _dot(pn.astype(BF16), dok.astype(BF16), 0, 0)
            dqs.append(dq)
        dq_ref[...] = jnp.concatenate(dqs, axis=1).astype(BF16)
        dk_ref[0] = dk
        dv_ref[0] = dv

        @pl.when(n == nb - 1)
        def _():
            dsk_ref[...] = jnp.sum(sk_acc[...], axis=0, keepdims=True)

    full_b = pl.BlockSpec((NH, BLK, 2 * BLK), lambda n: (0, 0, 0))
    band = pl.BlockSpec((1, 2 * BLK, BLK), lambda n: (n, 0, 0))
    return pl.pallas_call(
        body, grid=(nb,),
        in_specs=[SMEM] + _swa_specs(512) + [full_b, pl.BlockSpec((BLK, 512), lambda n: (n, 0)), full_b],
        out_specs=[pl.BlockSpec((BLK, 512), lambda n: (n, 0)), band, band, full_b, pl.BlockSpec((1, BLK), lambda n: (0, 0))],
        out_shape=[jax.ShapeDtypeStruct((T, 512), BF16), jax.ShapeDtypeStruct((nb, 2 * BLK, BLK), F32),
                   jax.ShapeDtypeStruct((nb, 2 * BLK, BLK), F32), jax.ShapeDtypeStruct((NH, BLK, 2 * BLK), F32),
                   jax.ShapeDtypeStruct((1, BLK), F32)],
        scratch_shapes=[pltpu.VMEM((BLK, BLK), F32)],
        compiler_params=_cp("arbitrary"), name=name)(sinks, proj, proj, proj, proj, proj, bias, do, dbias_in)


def _band_fold(dkb, dvb, name):
    nb = dkb.shape[0]

    def body(ko_ref, kn_ref, vo_ref, vn_ref, dk_ref, dv_ref):
        last = pl.program_id(0) == nb - 1
        dk_ref[...] = (ko_ref[0] + jnp.where(last, 0.0, kn_ref[0])).astype(BF16)
        dv_ref[...] = (vo_ref[0] + jnp.where(last, 0.0, vn_ref[0])).astype(BF16)

    own = pl.BlockSpec((1, BLK, BLK), lambda j: (j, 1, 0))
    nxt = pl.BlockSpec((1, BLK, BLK), lambda j: (jnp.minimum(j + 1, nb - 1), 0, 0))
    out = pl.BlockSpec((BLK, BLK), lambda j: (j, 0))
    return pl.pallas_call(
        body, grid=(nb,), in_specs=[own, nxt, own, nxt], out_specs=[out, out],
        out_shape=[jax.ShapeDtypeStruct((nb * BLK, BLK), BF16)] * 2,
        compiler_params=_cp("parallel"), name=name)(dkb, dkb, dvb, dvb)


def _token_major(x, width):
    full = jnp.concatenate([x, jnp.zeros((BLK - NH, BLK), F32)], axis=0).T
    return full if width == BLK else jnp.concatenate([full, jnp.zeros((BLK, width - BLK), F32)], axis=1)


def _cum_fwd(proj, fb, name):
    T = proj.shape[0]
    tr = _rt(T)

    def body(z_ref, fb_ref, c_ref, carry):
        g = pl.program_id(0)
        lane = lax.broadcasted_iota(jnp.int32, (NH, BLK), 1)

        @pl.when(g == 0)
        def _():
            carry[...] = jnp.zeros_like(carry)

        run = carry[...]
        for sb in range(tr // BLK):
            r = slice(sb * BLK, (sb + 1) * BLK)
            z = z_ref[r, :].T[0:NH, :] + fb_ref[...]
            x = jnp.where(g * tr + sb * BLK + lane >= NPAD, _log_sigmoid(z), 0.0)
            s = 1
            while s < BLK:
                x = x + jnp.where(lane >= s, pltpu.roll(x, s, 1), 0.0)
                s *= 2
            x = x + run
            run = jnp.zeros((NH, BLK), F32) + jnp.sum(jnp.where(lane == BLK - 1, x, 0.0), axis=-1, keepdims=True)
            c_ref[r, :] = _token_major(x, BLK)
        carry[...] = run

    return pl.pallas_call(
        body, grid=(T // tr,),
        in_specs=[pl.BlockSpec((tr, BLK), lambda g: (g, FL // BLK)), pl.BlockSpec((NH, 1), lambda g: (0, 0))],
        out_specs=pl.BlockSpec((tr, BLK), lambda g: (g, 0)), out_shape=jax.ShapeDtypeStruct((T, BLK), F32),
        scratch_shapes=[pltpu.VMEM((NH, BLK), F32)], compiler_params=_cp("arbitrary"), name=name)(proj, fb)


def _cum_bwd(dqx, dkx, proj, fb, name):
    T = proj.shape[0]
    tr = _rt(T)
    nb = T // tr

    def body(dq_ref, dk_ref, z_ref, fb_ref, dz_ref, db_ref, carry):
        k = pl.program_id(0)
        g = nb - 1 - k
        lane = lax.broadcasted_iota(jnp.int32, (NH, BLK), 1)

        @pl.when(k == 0)
        def _():
            carry[...] = jnp.zeros_like(carry)
            db_ref[...] = jnp.zeros_like(db_ref)

        def picked(ref, r, r_first, r_second):
            rows = []
            for p in range(NH // 2):
                t_ = ref[r, p * BLK:(p + 1) * BLK].T
                rows += [t_[r_first:r_first + 1, :], t_[r_second:r_second + 1, :]]
            return jnp.concatenate(rows, axis=0)

        run, tot = carry[...], jnp.zeros((NH, 1), F32)
        for sb in reversed(range(tr // BLK)):
            r = slice(sb * BLK, (sb + 1) * BLK)
            x = picked(dq_ref, r, 64, 0) - picked(dk_ref, r, 67, 3)
            s = 1
            while s < BLK:
                x = x + jnp.where(lane < BLK - s, pltpu.roll(x, BLK - s, 1), 0.0)
                s *= 2
            x = x + run
            run = jnp.zeros((NH, BLK), F32) + jnp.sum(jnp.where(lane == 0, x, 0.0), axis=-1, keepdims=True)
            z = z_ref[r, :].T[0:NH, :] + fb_ref[...]
            dz = jnp.where(g * tr + sb * BLK + lane >= NPAD, x * _sigmoid(-z), 0.0)
            tot = tot + jnp.sum(dz, axis=-1, keepdims=True)
            dz_ref[r, :] = _token_major(dz, 2 * BLK).astype(BF16)
        carry[...] = run
        db_ref[...] += tot

    rev = lambda k: nb - 1 - k
    wide = pl.BlockSpec((tr, 512), lambda k: (rev(k), 0))
    return pl.pallas_call(
        body, grid=(nb,),
        in_specs=[wide, wide, pl.BlockSpec((tr, BLK), lambda k: (rev(k), FL // BLK)), pl.BlockSpec((NH, 1), lambda k: (0, 0))],
        out_specs=[pl.BlockSpec((tr, 2 * BLK), lambda k: (rev(k), 0)), pl.BlockSpec((NH, BLK), lambda k: (0, 0))],
        out_shape=[jax.ShapeDtypeStruct((T, 2 * BLK), BF16), jax.ShapeDtypeStruct((NH, BLK), F32)],
        scratch_shapes=[pltpu.VMEM((NH, BLK), F32)], compiler_params=_cp("arbitrary"), name=name)(dqx, dkx, proj, fb)


def _fox_prep(proj, ccol, name):
    T = proj.shape[0]
    tr = _pick(T, (1408, 384, 128))

    def body(q_ref, k_ref, v_ref, cc_ref, qa_ref, ka_ref, kt_ref, vmt_ref, vo_ref):
        h = pl.program_id(1)
        lane = lax.broadcasted_iota(jnp.int32, (1, BLK), 1)
        own = (lane >> 6) == (h & 1)
        a0 = 64 * (1 - (h & 1))
        c = _lane_pick(cc_ref[...], lane, h)
        hi = c.astype(BF16).astype(F32)
        mid = (c - hi).astype(BF16).astype(F32)
        lo = (c - hi - mid).astype(BF16).astype(F32)
        ones = (lane >= a0 + 3) & (lane < a0 + 6)
        qa = jnp.where(own, q_ref[...] * SCALE, jnp.where(ones, 1.0, 0.0))
        qa = jnp.where(lane == a0, hi, jnp.where(lane == a0 + 1, mid, jnp.where(lane == a0 + 2, lo, qa)))
        ones = (lane >= a0) & (lane < a0 + 3)
        ka = jnp.where(own, k_ref[...], jnp.where(ones, 1.0, 0.0))
        ka = jnp.where(lane == a0 + 3, -hi, jnp.where(lane == a0 + 4, -mid, jnp.where(lane == a0 + 5, -lo, ka)))
        qa_ref[...] = qa.astype(BF16)
        kab = ka.astype(BF16)
        ka_ref[...] = kab
        kt_ref[...] = kab.T
        vm = jnp.where(own, v_ref[...], 0.0)
        vmt_ref[...] = vm.astype(BF16).T
        vo_ref[...] = jnp.where(lane == a0, 1.0, vm).astype(BF16)

    pair = lambda col0: pl.BlockSpec((tr, BLK), lambda i, h: (i, col0 // BLK + (h >> 1)))
    out = pl.BlockSpec((None, tr, BLK), lambda i, h: (h, i, 0))
    out_t = pl.BlockSpec((None, BLK, tr), lambda i, h: (h, 0, i))
    tok = jax.ShapeDtypeStruct((NH, T, BLK), BF16)
    return pl.pallas_call(
        body, grid=(T // tr, NH), in_specs=[pair(QF), pair(KF), pair(VF), pl.BlockSpec((tr, BLK), lambda i, h: (i, 0))],
        out_specs=[out, out, out_t, out_t, out],
        out_shape=[tok, tok, jax.ShapeDtypeStruct((NH, BLK, T), BF16), jax.ShapeDtypeStruct((NH, BLK, T), BF16), tok],
        compiler_params=_cp("parallel", "arbitrary"), name=name)(proj, proj, proj, ccol)


def _fox_fwd(qaug, kaug_t, vo, name, gather=None):
    T = qaug.shape[1]
    t = _rt(T)
    nt = T // t
    ng = len(gather[0]) if gather else 0

    pairs = [(i, j) for i in range(nt) for j in range(i + 1)]
    i_of = jnp.asarray(np.array([p[0] for p in pairs], np.int32))
    j_of = jnp.asarray(np.array([p[1] for p in pairs], np.int32))
    ns = len(pairs)

    def body(i_ref, j_ref, q0, q1, k0, k1, v0, v1, *rest):
        o_ref, ot_ref, lse0_ref, lse1_ref = rest[ng:ng + 4]
        m_ref, acc_ref = rest[2 * ng + 4:2 * ng + 6]
        p_, s_ = pl.program_id(0), pl.program_id(1)
        i, j = i_ref[s_], j_ref[s_]
        lane = lax.broadcasted_iota(jnp.int32, (1, BLK), 1)
        lo = lane < 64
        if gather:
            start, finish = _gather_plan(rest[ng + 4:2 * ng + 4], gather[1], *rest[2 * ng + 6:])
            pl.when((p_ == 0) & (s_ == 0))(start)

        @pl.when(j == 0)
        def _():
            m_ref[...] = jnp.full_like(m_ref, NEG)
            acc_ref[...] = jnp.zeros_like(acc_ref)

        def step(masked):
            for e, (q_ref, k_ref, v_ref) in enumerate(((q0, k0, v0), (q1, k1, v1))):
                s = _dot(q_ref[...], k_ref[...], 1, 0)
                if masked:
                    s = jnp.where(_fox_mask(i, j, t), s, NEG)
                m_old = m_ref[e]
                m_new = jnp.maximum(m_old, jnp.max(s, axis=-1, keepdims=True))
                m_ref[e] = m_new
                pe = jnp.exp(s - jnp.concatenate([m_new] * (t // BLK), axis=1))
                acc_ref[e] = jnp.exp(m_old - m_new) * acc_ref[e] + _dot(pe.astype(BF16), v_ref[...], 1, 0)

        pl.when((j < i) & (j > 0))(lambda: step(False))
        pl.when((j == i) | ((j == 0) & (i > 0)))(lambda: step(True))

        @pl.when(j == i)
        def _():
            rows = i * t + lax.broadcasted_iota(jnp.int32, (t, 1), 0)
            l0, l1 = _lane_pick(acc_ref[0], lane, 64), _lane_pick(acc_ref[1], lane, 0)
            o = jnp.where(rows >= NPAD, jnp.where(lo, acc_ref[0] / l0, acc_ref[1] / l1), 0.0).astype(BF16)
            o_ref[...] = o
            ot_ref[...] = o.T
            lse0_ref[...] = m_ref[0] + jnp.log(l0)
            lse1_ref[...] = m_ref[1] + jnp.log(l1)

        if gather:
            pl.when((p_ == NH // 2 - 1) & (s_ == ns - 1))(finish)

    qs = lambda e: pl.BlockSpec((None, t, BLK), lambda p, s, ii, jj: (2 * p + e, ii[s], 0))
    ks = lambda e: pl.BlockSpec((None, t, BLK), lambda p, s, ii, jj: (2 * p + e, jj[s], 0))
    kts = lambda e: pl.BlockSpec((None, BLK, t), lambda p, s, ii, jj: (2 * p + e, 0, jj[s]))
    rep = pl.BlockSpec((None, t, BLK), lambda p, s, ii, jj: (p, ii[s], 0))
    bufs = list(gather[0]) if gather else []
    return pl.pallas_call(
        body,
        grid_spec=pltpu.PrefetchScalarGridSpec(
            num_scalar_prefetch=2, grid=(NH // 2, ns),
            in_specs=[qs(0), qs(1), kts(0), kts(1), ks(0), ks(1)] + [ANY] * ng,
            out_specs=[pl.BlockSpec((t, BLK), lambda p, s, ii, jj: (ii[s], p)),
                       pl.BlockSpec((BLK, t), lambda p, s, ii, jj: (p, ii[s])), rep, rep] + [ANY] * ng,
            scratch_shapes=[pltpu.VMEM((2, t, BLK), F32), pltpu.VMEM((2, t, BLK), F32)]
            + ([pltpu.SemaphoreType.DMA((6 * ng,)), pltpu.SemaphoreType.DMA((6 * ng,))] if gather else [])),
        out_shape=[jax.ShapeDtypeStruct((T, 512), BF16), jax.ShapeDtypeStruct((512, T), BF16)]
        + [jax.ShapeDtypeStruct((NH // 2, T, BLK), F32)] * 2 + [jax.ShapeDtypeStruct(b.shape, b.dtype) for b in bufs],
        input_output_aliases={8 + g: 4 + g for g in range(ng)},
        compiler_params=(pltpu.CompilerParams(dimension_semantics=("arbitrary",) * 2, vmem_limit_bytes=VMEM_LIMIT,
                                              has_side_effects=True) if gather
                         else _cp("parallel", "arbitrary")), name=name)(i_of, j_of, qaug, qaug, kaug_t, kaug_t, vo, vo, *bufs)


def _fox_delta(do, o, name):
    T = do.shape[0]
    tr = _rt(T)

    def body(do_ref, o_ref, d0_ref, d1_ref):
        lo = lax.broadcasted_iota(jnp.int32, (1, BLK), 1) < 64
        prod = do_ref[...].astype(F32) * o_ref[...].astype(F32)
        d0_ref[...] = jnp.zeros((tr, BLK), F32) + jnp.sum(jnp.where(lo, prod, 0.0), axis=-1, keepdims=True)
        d1_ref[...] = jnp.zeros((tr, BLK), F32) + jnp.sum(jnp.where(lo, 0.0, prod), axis=-1, keepdims=True)

    blk = pl.BlockSpec((tr, BLK), lambda i, p: (i, p))
    rep = pl.BlockSpec((None, tr, BLK), lambda i, p: (p, i, 0))
    return pl.pallas_call(
        body, grid=(T // tr, NH // 2), in_specs=[blk, blk], out_specs=[rep, rep],
        out_shape=[jax.ShapeDtypeStruct((NH // 2, T, BLK), F32)] * 2,
        compiler_params=_cp("parallel", "parallel"), name=name)(do, o)


def _fox_bwd(qaug, kaug, kaug_t, vm_t, do, lses, deltas, name, scatter=None):
    T = qaug.shape[1]
    t = _rt(T)
    nt = T // t
    ng = len(scatter[0]) if scatter else 0
    pairs = [(i, j) for j in range(nt) for i in range(j, nt)]
    i_of = jnp.asarray(np.array([p[0] for p in pairs], np.int32))
    j_of = jnp.asarray(np.array([p[1] for p in pairs], np.int32))
    ns = len(pairs)

    def body(i_ref, j_ref, q0, q1, k0, k1, kt0, kt1, v0, v1, do_ref, lse0, lse1, dl0, dl1, *rest):
        dq_ref, dqx_ref, dk_ref, dv_ref, dkx_ref = rest[ng:ng + 5]
        dq_acc, dk_acc, dv_acc = rest[2 * ng + 5:2 * ng + 8]
        p_, s_ = pl.program_id(0), pl.program_id(1)
        i, j = i_ref[s_], j_ref[s_]
        lane = lax.broadcasted_iota(jnp.int32, (1, BLK), 1)
        lo = lane < 64
        if scatter:
            start, finish = _scatter_plan(rest[:ng], rest[ng + 5:2 * ng + 5], scatter[1], *rest[2 * ng + 8:])
            pl.when((p_ == 0) & (s_ == 0))(start)

        @pl.when(s_ == 0)
        def _():
            dq_acc[...] = jnp.zeros_like(dq_acc)

        @pl.when(i == j)
        def _():
            dk_acc[...] = jnp.zeros_like(dk_acc)
            dv_acc[...] = jnp.zeros_like(dv_acc)

        def step(masked):
            dob = do_ref[...]
            rows = pl.ds(pl.multiple_of(i * t, t), t)
            wide = lambda ref: jnp.concatenate([ref[...]] * (t // BLK), axis=1)
            for e, (q_ref, k_ref, kt_ref, v_ref, lse_ref, dl_ref) in enumerate(
                    ((q0, k0, kt0, v0, lse0, dl0), (q1, k1, kt1, v1, lse1, dl1))):
                s = _dot(q_ref[...], kt_ref[...], 1, 0)
                if masked:
                    s = jnp.where(_fox_mask(i, j, t), s, NEG)
                pe = jnp.exp(s - wide(lse_ref))
                dp = _dot(dob, v_ref[...], 1, 0)
                ds = (pe * (dp - wide(dl_ref))).astype(BF16)
                dq_acc[e, rows, :] += _dot(ds, k_ref[...], 1, 0)
                dk_acc[e] += _dot(ds, q_ref[...], 0, 0)
                dv_acc[e] += _dot(pe.astype(BF16), dob, 0, 0)

        pl.when((i > j) & (j > 0))(lambda: step(False))
        pl.when((i == j) | ((j == 0) & (i > 0)))(lambda: step(True))

        @pl.when(i == nt - 1)
        def _():
            dk_ref[...] = jnp.where(lo, dk_acc[0], dk_acc[1]).astype(BF16)
            dv_ref[...] = jnp.where(lo, dv_acc[0], dv_acc[1]).astype(BF16)
            dkx_ref[...] = jnp.where(lo, dk_acc[1], dk_acc[0])

        @pl.when(s_ == ns - 1)
        def _():
            dq_ref[...] = (jnp.where(lo, dq_acc[0], dq_acc[1]) * SCALE).astype(BF16)
            dqx_ref[...] = jnp.where(lo, dq_acc[1], dq_acc[0])

        if scatter:
            pl.when((p_ == NH // 2 - 1) & (s_ == ns - 1))(finish)

    qs = lambda e: pl.BlockSpec((None, t, BLK), lambda p, s, ii, jj: (2 * p + e, ii[s], 0))
    ks = lambda e: pl.BlockSpec((None, t, BLK), lambda p, s, ii, jj: (2 * p + e, jj[s], 0))
    kts = lambda e: pl.BlockSpec((None, BLK, t), lambda p, s, ii, jj: (2 * p + e, 0, jj[s]))
    qside = pl.BlockSpec((t, BLK), lambda p, s, ii, jj: (ii[s], p))
    kside = pl.BlockSpec((t, BLK), lambda p, s, ii, jj: (jj[s], p))
    rep = pl.BlockSpec((None, t, BLK), lambda p, s, ii, jj: (p, ii[s], 0))
    whole = pl.BlockSpec((T, BLK), lambda p, s, ii, jj: (0, p))
    sums = list(scatter[0]) if scatter else []
    return pl.pallas_call(
        body,
        grid_spec=pltpu.PrefetchScalarGridSpec(
            num_scalar_prefetch=2, grid=(NH // 2, ns),
            in_specs=[qs(0), qs(1), ks(0), ks(1), kts(0), kts(1), kts(0), kts(1), qside, rep, rep, rep, rep] + [ANY] * ng,
            out_specs=[whole, whole, kside, kside, kside] + [ANY] * ng,
            scratch_shapes=[pltpu.VMEM((2, T, BLK), F32), pltpu.VMEM((2, t, BLK), F32), pltpu.VMEM((2, t, BLK), F32)]
            + ([pltpu.SemaphoreType.DMA((3 * ng,)), pltpu.SemaphoreType.DMA((3 * ng,))] if scatter else [])),
        out_shape=[jax.ShapeDtypeStruct((T, 512), BF16), jax.ShapeDtypeStruct((T, 512), F32),
                   jax.ShapeDtypeStruct((T, 512), BF16), jax.ShapeDtypeStruct((T, 512), BF16),
                   jax.ShapeDtypeStruct((T, 512), F32)] + (_scatter_shapes(sums, scatter[1]) if scatter else []),
        compiler_params=(pltpu.CompilerParams(dimension_semantics=("arbitrary",) * 2, vmem_limit_bytes=VMEM_LIMIT,
                                              has_side_effects=True) if scatter
                         else _cp("parallel", "arbitrary")), name=name)(
            i_of, j_of, qaug, qaug, kaug, kaug, kaug_t, kaug_t, vm_t, vm_t, do, *lses, *deltas, *sums)


def _fox_mask(i, j, t):
    row = i * t + lax.broadcasted_iota(jnp.int32, (t, t), 0)
    col = j * t + lax.broadcasted_iota(jnp.int32, (t, t), 1)
    return (col <= row) & (col >= NPAD)


def _lane_pick(x, lane, idx):
    return jnp.sum(jnp.where(lane == idx, x, 0.0), axis=-1, keepdims=True)


def _lru_gates(xc, wr_ref, wi_ref, vec_ref):
    xb = xc.astype(BF16)
    pre_r = jnp.concatenate([_dot(xb[:, p * BLK:(p + 1) * BLK], wr_ref[p], 1, 0) for p in range(LW // BLK)], axis=1)
    pre_i = jnp.concatenate([_dot(xb[:, p * BLK:(p + 1) * BLK], wi_ref[p], 1, 0) for p in range(LW // BLK)], axis=1)
    r = _sigmoid(pre_r + vec_ref[0:1, :])
    gi = _sigmoid(pre_i + vec_ref[1:2, :])
    log_a = LRU_C * r * _log_sigmoid(vec_ref[2:3, :])
    a = jnp.exp(log_a)
    mult = jnp.sqrt(_neg_expm1(2.0 * log_a))
    return r, gi, a, mult


def _conv(xbuf_ref, x, cw_ref, vec_ref, tr):
    return (cw_ref[3:4, :] * x + cw_ref[2:3, :] * xbuf_ref[7:7 + tr, :] + cw_ref[1:2, :] * xbuf_ref[6:6 + tr, :]
            + cw_ref[0:1, :] * xbuf_ref[5:5 + tr, :] + vec_ref[3:4, :])


def _lru_fwd(proj, cw, wr, wi, vec, name):
    T = proj.shape[0]
    tr = _rt(T)

    def body(x_ref, y_ref, cw_ref, wr_ref, wi_ref, vec_ref, oc_ref, oct_ref, hs_ref, xbuf, abuf, bbuf, hcar):
        i = pl.program_id(0)

        @pl.when(i == 0)
        def _():
            xbuf[0:8, :] = jnp.zeros((8, LW), F32)
            hcar[...] = jnp.zeros_like(hcar)

        x = x_ref[...]
        xbuf[8:8 + tr, :] = x
        xc = _conv(xbuf, x, cw_ref, vec_ref, tr)
        xbuf[0:8, :] = x[tr - 8:tr, :]
        _, gi, a, mult = _lru_gates(xc, wr_ref, wi_ref, vec_ref)
        rows = i * tr + lax.broadcasted_iota(jnp.int32, (tr, 1), 0)
        abuf[...] = a
        bbuf[...] = jnp.where(rows >= NPAD, mult * (gi * xc), 0.0)
        sub = lax.broadcasted_iota(jnp.int32, (8, 1), 0)

        def step(k, h):
            sl = pl.ds(pl.multiple_of(k * 8, 8), 8)
            a8, b8 = abuf[sl, :], bbuf[sl, :]
            for s in (1, 2, 4):
                ok = sub >= s
                b8 = jnp.where(ok, a8 * pltpu.roll(b8, s, 0) + b8, b8)
                a8 = jnp.where(ok, a8 * pltpu.roll(a8, s, 0), a8)
            h8 = a8 * h + b8
            bbuf[sl, :] = h8
            return h8[7:8, :]

        hcar[...] = lax.fori_loop(0, tr // 8, step, hcar[...])
        hs = bbuf[...]
        hs_ref[...] = hs
        oc = (hs * _gelu(y_ref[...])).astype(BF16)
        oc_ref[...] = oc
        oct_ref[...] = oc.T

    row = pl.BlockSpec((tr, LW), lambda i: (i, 0))
    full = lambda shape: pl.BlockSpec(shape, lambda i: (0,) * len(shape))
    return pl.pallas_call(
        body, grid=(T // tr,),
        in_specs=[pl.BlockSpec((tr, LW), lambda i: (i, XC // LW)), pl.BlockSpec((tr, LW), lambda i: (i, YC // LW)),
                  full((4, LW)), full((4, BLK, BLK)), full((4, BLK, BLK)), full((8, LW))],
        out_specs=[row, pl.BlockSpec((LW, tr), lambda i: (0, i)), row],
        out_shape=[jax.ShapeDtypeStruct((T, LW), BF16), jax.ShapeDtypeStruct((LW, T), BF16), jax.ShapeDtypeStruct((T, LW), F32)],
        scratch_shapes=[pltpu.VMEM((tr + 8, LW), F32), pltpu.VMEM((tr, LW), F32), pltpu.VMEM((tr, LW), F32),
                        pltpu.VMEM((1, LW), F32)],
        compiler_params=_cp("arbitrary"), name=name)(proj, proj, cw, wr, wi, vec)


def _lru_bwd(proj, hs, doc, cw, wr, wi, vec, name):
    T = proj.shape[0]
    tr = _rt(T)
    nt = T // tr
    r8 = tr // 8

    def body(x_ref, xp_ref, y_ref, hs_ref, hp_ref, do_ref, cw_ref, wr_ref, wi_ref, vec_ref,
             dx_ref, dy_ref, dwr_ref, dwi_ref, dvec_ref, xbuf, abuf, gbuf, hbuf, dbuf, gcar, acar):
        k = pl.program_id(0)
        i = nt - 1 - k

        @pl.when(k == 0)
        def _():
            dwr_ref[...] = jnp.zeros_like(dwr_ref)
            dwi_ref[...] = jnp.zeros_like(dwi_ref)
            dvec_ref[...] = jnp.zeros_like(dvec_ref)
            gcar[...] = jnp.zeros_like(gcar)
            acar[...] = jnp.zeros_like(acar)
            dbuf[tr:tr + 8, :] = jnp.zeros((8, LW), F32)

        first = i == 0
        x = x_ref[...]
        xbuf[0:8, :] = jnp.where(first, 0.0, xp_ref[...])
        xbuf[8:8 + tr, :] = x
        xc = _conv(xbuf, x, cw_ref, vec_ref, tr)
        r, gi, a, mult = _lru_gates(xc, wr_ref, wi_ref, vec_ref)
        y = y_ref[...]
        hs = hs_ref[...]
        do_ = do_ref[...].astype(F32)
        rows = i * tr + lax.broadcasted_iota(jnp.int32, (tr, 1), 0)
        abuf[0:tr, :] = a
        abuf[tr:tr + 8, :] = jnp.zeros((8, LW), F32) + acar[...]
        an = abuf[1:1 + tr, :]
        acar[...] = a[0:1, :]
        abuf[0:tr, :] = an
        gbuf[...] = do_ * _gelu(y)
        sub = lax.broadcasted_iota(jnp.int32, (8, 1), 0)

        def step(kk, g):
            sl = pl.ds(pl.multiple_of((r8 - 1 - kk) * 8, 8), 8)
            a8, b8 = abuf[sl, :], gbuf[sl, :]
            for s in (1, 2, 4):
                ok = sub < 8 - s
                b8 = jnp.where(ok, a8 * pltpu.roll(b8, 8 - s, 0) + b8, b8)
                a8 = jnp.where(ok, a8 * pltpu.roll(a8, 8 - s, 0), a8)
            g8 = a8 * g + b8
            gbuf[sl, :] = g8
            return g8[0:1, :]

        gcar[...] = lax.fori_loop(0, r8, step, gcar[...])
        g = gbuf[...]
        hbuf[0:8, :] = jnp.where(first, 0.0, hp_ref[...])
        hbuf[8:8 + tr, :] = hs
        hprev = hbuf[7:7 + tr, :]
        dinp = jnp.where(rows >= NPAD, g, 0.0)
        da = g * hprev
        dmult = dinp * gi * xc
        dgi = dinp * mult * xc
        dxc = dinp * mult * gi
        dlog_a = da * a - dmult * a * a / mult
        ls = _log_sigmoid(vec_ref[2:3, :])
        dpre_r = dlog_a * (LRU_C * ls) * r * (1.0 - r)
        dpre_i = dgi * gi * (1.0 - gi)
        xb = xc.astype(BF16)
        rb, ib = dpre_r.astype(BF16), dpre_i.astype(BF16)
        back = []
        for p in range(LW // BLK):
            c = slice(p * BLK, (p + 1) * BLK)
            back.append(_dot(rb[:, c], wr_ref[p], 1, 1) + _dot(ib[:, c], wi_ref[p], 1, 1))
            dwr_ref[p] += _dot(xb[:, c], rb[:, c], 0, 0)
            dwi_ref[p] += _dot(xb[:, c], ib[:, c], 0, 0)
        dxc = dxc + jnp.concatenate(back, axis=1)
        col = lambda v: jnp.sum(v, axis=0, keepdims=True)
        dvec_ref[0:1, :] += col(dpre_r)
        dvec_ref[1:2, :] += col(dpre_i)
        dvec_ref[2:3, :] += col(dlog_a * (LRU_C * r)) * _sigmoid(-vec_ref[2:3, :])
        dvec_ref[3:4, :] += col(dxc)
        dvec_ref[4:5, :] += col(dxc * xbuf[5:5 + tr, :])
        dvec_ref[5:6, :] += col(dxc * xbuf[6:6 + tr, :])
        dvec_ref[6:7, :] += col(dxc * xbuf[7:7 + tr, :])
        dvec_ref[7:8, :] += col(dxc * x)
        dbuf[0:tr, :] = dxc
        dxr = (cw_ref[3:4, :] * dxc + cw_ref[2:3, :] * dbuf[1:1 + tr, :] + cw_ref[1:2, :] * dbuf[2:2 + tr, :]
               + cw_ref[0:1, :] * dbuf[3:3 + tr, :])
        dbuf[tr:tr + 8, :] = dxc[0:8, :]
        dx_ref[...] = jnp.where(rows >= NPAD, dxr, 0.0).astype(BF16)
        dy_ref[...] = (do_ * hs * _gelu_grad(y)).astype(BF16)

    rev = lambda k: nt - 1 - k
    row = lambda col0: pl.BlockSpec((tr, LW), lambda k: (rev(k), col0))
    prev8 = lambda col0: pl.BlockSpec((8, LW), lambda k: (jnp.maximum(rev(k) * r8 - 1, 0), col0))
    full = lambda shape: pl.BlockSpec(shape, lambda k: (0,) * len(shape))
    return pl.pallas_call(
        body, grid=(nt,),
        in_specs=[row(XC // LW), prev8(XC // LW), row(YC // LW), row(0), prev8(0), row(0),
                  full((4, LW)), full((4, BLK, BLK)), full((4, BLK, BLK)), full((8, LW))],
        out_specs=[row(0), row(0), full((4, BLK, BLK)), full((4, BLK, BLK)), full((8, LW))],
        out_shape=[jax.ShapeDtypeStruct((T, LW), BF16), jax.ShapeDtypeStruct((T, LW), BF16),
                   jax.ShapeDtypeStruct((4, BLK, BLK), F32), jax.ShapeDtypeStruct((4, BLK, BLK), F32),
                   jax.ShapeDtypeStruct((8, LW), F32)],
        scratch_shapes=[pltpu.VMEM((tr + 8, LW), F32), pltpu.VMEM((tr + 8, LW), F32), pltpu.VMEM((tr, LW), F32),
                        pltpu.VMEM((tr + 8, LW), F32), pltpu.VMEM((tr + 8, LW), F32),
                        pltpu.VMEM((1, LW), F32), pltpu.VMEM((1, LW), F32)],
        compiler_params=_cp("arbitrary"), name=name)(proj, proj, proj, hs, hs, doc, cw, wr, wi, vec)


def _branch_merge_fwd(oa, of, oc, wb, proj, name):
    T = proj.shape[0]
    tm, tn = _rt(T), 512

    def body(a0, a1, a2, w_ref, g0, g1, g2, r0, r1, r2, m_ref, mt_ref):
        acc = None
        for g, (a_ref, g_ref, r_ref) in enumerate(((a0, g0, r0), (a1, g1, r1), (a2, g2, r2))):
            b = _dot(a_ref[...], w_ref[g], 1, 0)
            r_ref[...] = b
            term = _sigmoid(g_ref[...]) * b
            acc = term if acc is None else acc + term
        m = acc.astype(BF16)
        m_ref[...] = m
        mt_ref[...] = m.T

    act = pl.BlockSpec((tm, LW), lambda j, i: (i, 0))
    gate = lambda g: pl.BlockSpec((tm, tn), lambda j, i: (i, (GT + g * D) // tn + j))
    blk = pl.BlockSpec((tm, tn), lambda j, i: (i, j))
    return pl.pallas_call(
        body, grid=(D // tn, T // tm),
        in_specs=[act, act, act, pl.BlockSpec((3, LW, tn), lambda j, i: (0, 0, j)), gate(0), gate(1), gate(2)],
        out_specs=[blk] * 4 + [pl.BlockSpec((tn, tm), lambda j, i: (j, i))],
        out_shape=[jax.ShapeDtypeStruct((T, D), F32)] * 3 + [jax.ShapeDtypeStruct((T, D), BF16), jax.ShapeDtypeStruct((D, T), BF16)],
        compiler_params=_cp("parallel", "parallel"), name=name)(oa, of, oc, wb, proj, proj, proj)


def _out_dx_merge_bwd(dhb, w_out, proj, b0, b1, b2, name):
    T = proj.shape[0]
    tm, tn = _rt(T), 512

    def body(dh_ref, w_ref, g0, g1, g2, r0, r1, r2, d0, d1, d2, e0, e1, e2):
        dmv = _dot(dh_ref[...], w_ref[...], 1, 1)
        for g_ref, r_ref, d_ref, e_ref in ((g0, r0, d0, e0), (g1, r1, d1, e1), (g2, r2, d2, e2)):
            sg = _sigmoid(g_ref[...])
            d_ref[...] = (dmv * sg).astype(BF16)
            e_ref[...] = (dmv * r_ref[...] * sg * (1.0 - sg)).astype(BF16)

    gate = lambda g: pl.BlockSpec((tm, tn), lambda j, i: (i, (GT + g * D) // tn + j))
    blk = pl.BlockSpec((tm, tn), lambda j, i: (i, j))
    return pl.pallas_call(
        body, grid=(D // tn, T // tm),
        in_specs=[pl.BlockSpec((tm, D), lambda j, i: (i, 0)), pl.BlockSpec((tn, D), lambda j, i: (j, 0)),
                  gate(0), gate(1), gate(2), blk, blk, blk],
        out_specs=[blk] * 6, out_shape=[jax.ShapeDtypeStruct((T, D), BF16)] * 6,
        compiler_params=_cp("parallel", "parallel"), name=name)(dhb, w_out, proj, proj, proj, b0, b1, b2)


def _ffn_in_swiglu_fwd(u, w, name):
    T = u.shape[0]
    tm, tn = _rt(T), _pick(DFF, (1408, 256))
    nj = DFF // tn

    def body(u_ref, wg_ref, wu_ref, g_ref, up_ref, a_ref, at_ref):
        ub = u_ref[...]
        g = _dot(ub, wg_ref[...], 1, 0)
        up = _dot(ub, wu_ref[...], 1, 0)
        g_ref[...] = g
        up_ref[...] = up
        a = (g * _sigmoid(g) * up).astype(BF16)
        a_ref[...] = a
        at_ref[...] = a.T

    blk = pl.BlockSpec((tm, tn), lambda j, i: (i, j))
    return pl.pallas_call(
        body, grid=(nj, T // tm),
        in_specs=[pl.BlockSpec((tm, D), lambda j, i: (i, 0)), pl.BlockSpec((D, tn), lambda j, i: (0, j)),
                  pl.BlockSpec((D, tn), lambda j, i: (0, j + nj))],
        out_specs=[blk] * 3 + [pl.BlockSpec((tn, tm), lambda j, i: (j, i))],
        out_shape=[jax.ShapeDtypeStruct((T, DFF), F32)] * 2 + [jax.ShapeDtypeStruct((T, DFF), BF16),
                                                               jax.ShapeDtypeStruct((DFF, T), BF16)],
        compiler_params=_cp("parallel", "parallel"), name=name)(u, w, w)


def _ffn_out_dx_swiglu_bwd(dhb, w, gate, up, name):
    T = dhb.shape[0]
    tm, tn = _rt(T), _pick(DFF, (1408, 256))

    def body(dh_ref, w_ref, g_ref, up_ref, dg_ref, du_ref):
        d = _dot(dh_ref[...], w_ref[...], 1, 1)
        g = g_ref[...]
        sg = _sigmoid(g)
        dg_ref[...] = (d * up_ref[...] * (sg + g * sg * (1.0 - sg))).astype(BF16)
        du_ref[...] = (d * g * sg).astype(BF16)

    blk = pl.BlockSpec((tm, tn), lambda j, i: (i, j))
    return pl.pallas_call(
        body, grid=(DFF // tn, T // tm),
        in_specs=[pl.BlockSpec((tm, D), lambda j, i: (i, 0)), pl.BlockSpec((tn, D), lambda j, i: (j, 0)), blk, blk],
        out_specs=[blk] * 2, out_shape=[jax.ShapeDtypeStruct((T, DFF), BF16)] * 2,
        compiler_params=_cp("parallel", "parallel"), name=name)(dhb, w, gate, up)


def _adamw(w, g, m, v, name):
    R, C = w.shape
    tr = _pick(R, tuple(t for t in (512, 256, 128, 64, 32, 16, 8) if t * C * 4 <= (3 << 19)))
    c1 = 1.0 - ADAM_B1 ** ADAM_STEP
    c2 = 1.0 - ADAM_B2 ** ADAM_STEP

    def body(w_ref, g_ref, m_ref, v_ref, d_ref, mo_ref, vo_ref):
        gv = g_ref[...]
        mn = ADAM_B1 * m_ref[...] + (1.0 - ADAM_B1) * gv
        vn = ADAM_B2 * v_ref[...] + (1.0 - ADAM_B2) * (gv * gv)
        d_ref[...] = -ADAM_LR * ((mn / c1) / (jnp.sqrt(vn / c2) + ADAM_EPS) + ADAM_WD * w_ref[...])
        mo_ref[...] = mn
        vo_ref[...] = vn

    blk = pl.BlockSpec((tr, C), lambda i: (i, 0))
    return pl.pallas_call(
        body, grid=(R // tr,), in_specs=[blk] * 4, out_specs=[blk] * 3,
        out_shape=[jax.ShapeDtypeStruct((R, C), F32)] * 3, compiler_params=_cp("parallel"), name=name)(w, g, m, v)


def _sum_lead(x, name):
    n, R, C = x.shape
    tr = _pick(R, (512, 256, 128, 64, 32, 16, 8))

    def body(x_ref, o_ref):
        acc = x_ref[0]
        for d in range(1, n):
            acc = acc + x_ref[d]
        o_ref[...] = acc

    return pl.pallas_call(
        body, grid=(R // tr,), in_specs=[pl.BlockSpec((n, tr, C), lambda i: (0, i, 0))],
        out_specs=pl.BlockSpec((tr, C), lambda i: (i, 0)), out_shape=jax.ShapeDtypeStruct((R, C), F32),
        compiler_params=_cp("parallel"), name=name)(x)


def _here():
    return lax.axis_index("x"), lax.axis_index("y"), lax.axis_index("c")


def _rcopy(src, dst, send_sems, recv_sems, k, to):
    return pltpu.make_async_remote_copy(src_ref=src, dst_ref=dst, send_sem=send_sems.at[k], recv_sem=recv_sems.at[k],
                                        device_id=to, device_id_type=MESH)


def _hbm_calls(body, args, out_shapes, n_sems, aliases, name):
    return pl.pallas_call(
        body, in_specs=[ANY] * len(args), out_specs=[ANY] * len(out_shapes), out_shape=out_shapes,
        input_output_aliases=aliases,
        scratch_shapes=[pltpu.SemaphoreType.DMA((n_sems,)), pltpu.SemaphoreType.DMA((n_sems,))],
        compiler_params=pltpu.CompilerParams(has_side_effects=True), name=name)(*args)


def _gather_plan(outs, axes, send_sems, recv_sems):
    x, y, c = _here()
    sib = (x, y, 1 - c)
    chips = [(1 - x, y), (x, 1 - y), (1 - x, 1 - y)]
    todo = [(t, k, chip) for t in range(len(outs)) for k, chip in enumerate(chips)]

    def win(t, chip, hc):
        o, ax = outs[t], axes[t]
        w = o.shape[ax] // N_SHARD
        first = (2 * chip[0] + chip[1]) * w
        if ax == 0:
            return o.at[pl.ds(pl.multiple_of(first + hc * (w // 2), 16), w // 2), :]
        rows = o.shape[0] // 2
        return o.at[pl.ds(pl.multiple_of(hc * rows, 16), rows), pl.ds(pl.multiple_of(first, BLK), w)]

    def copy(t, k, chip, hc, to):
        return _rcopy(win(t, chip, hc), win(t, chip, hc), send_sems, recv_sems, 6 * t + k, to)

    def start():
        for t, k, chip in todo:
            copy(t, k, (x, y), c, (*chip, c)).start()

    def finish():
        for t, k, chip in todo:
            copy(t, k, chip, c, (*chip, c)).wait_recv()
            copy(t, 3 + k, chip, c, sib).start()
        for t, k, chip in todo:
            copy(t, 3 + k, chip, 1 - c, sib).wait_recv()
        for t, k, chip in todo:
            copy(t, k, (x, y), c, (*chip, c)).wait_send()
            copy(t, 3 + k, chip, c, sib).wait_send()

    return start, finish


def _all_gather_weights(fulls, axes, name):
    nt = len(fulls)

    def body(*refs):
        start, finish = _gather_plan(refs[nt:2 * nt], axes, *refs[2 * nt:])
        start()
        finish()

    return _hbm_calls(body, fulls, [jax.ShapeDtypeStruct(f.shape, f.dtype) for f in fulls], 6 * nt,
                      {t: t for t in range(nt)}, name)


def _half(ref, ax, hc):
    n = ref.shape[ax] // 2
    sl = pl.ds(pl.multiple_of(hc * n, 8), n)
    return ref.at[sl, :] if ax == 0 else ref.at[:, sl]


def _shrunk(shape, ax, by):
    shape = list(shape)
    shape[ax] //= by
    return tuple(shape)


def _swap_halves(gs, haxes, name):
    nt = len(gs)

    def body(*refs):
        ins, outs, (send_sems, recv_sems) = refs[:nt], refs[nt:2 * nt], refs[2 * nt:]
        x, y, c = _here()
        cps = [_rcopy(_half(g, ax, 1 - c), o, send_sems, recv_sems, t, (x, y, 1 - c))
               for t, (g, o, ax) in enumerate(zip(ins, outs, haxes))]
        for cp in cps:
            cp.start()
        for cp in cps:
            cp.wait()

    return _hbm_calls(body, gs, [jax.ShapeDtypeStruct(_shrunk(g.shape, ax, 2), g.dtype) for g, ax in zip(gs, haxes)],
                      nt, {}, name)


def _scatter_plan(ins, outs, saxes, send_sems, recv_sems):
    x, y, c = _here()
    chips = [(1 - x, y), (x, 1 - y), (1 - x, 1 - y)]

    def copies():
        cps = []
        for t, (s, o, ax) in enumerate(zip(ins, outs, saxes)):
            w = s.shape[ax] // N_SHARD
            for k, chip in enumerate(chips):
                first = pl.multiple_of((2 * chip[0] + chip[1]) * w, 8)
                src = s.at[pl.ds(first, w), :] if ax == 0 else s.at[:, pl.ds(first, w)]
                cps.append(_rcopy(src, o.at[k], send_sems, recv_sems, 3 * t + k, (*chip, c)))
        return cps

    def start():
        for cp in copies():
            cp.start()

    def finish():
        for cp in copies():
            cp.wait()

    return start, finish


def _scatter_shapes(sbs, saxes):
    return [jax.ShapeDtypeStruct((3,) + _shrunk(s.shape, ax, N_SHARD), s.dtype) for s, ax in zip(sbs, saxes)]


def _small_gather_copies(out_ref, send_sems, recv_sems, k0):
    x, y, c = _here()
    me = 4 * x + 2 * y + c
    cps = []
    for k in range(1, 8):
        to = (x ^ ((k >> 2) & 1), y ^ ((k >> 1) & 1), c ^ (k & 1))
        peer = 4 * to[0] + 2 * to[1] + to[2]
        cps.append((_rcopy(out_ref.at[me], out_ref.at[me], send_sems, recv_sems, k0 + k - 1, to),
                    _rcopy(out_ref.at[peer], out_ref.at[peer], send_sems, recv_sems, k0 + k - 1, to)))
    return cps


def _scatter_to_chips(sbs, saxes, small, name):
    nt = len(sbs)

    def body(*refs):
        small_out = refs[2 * nt + 1]
        send_sems, recv_sems = refs[2 * nt + 2:]
        start, finish = _scatter_plan(refs[:nt], refs[nt + 1:2 * nt + 1], saxes, send_sems, recv_sems)
        start()
        cps = _small_gather_copies(small_out, send_sems, recv_sems, 3 * nt)
        for snd, _ in cps:
            snd.start()
        for _, rcv in cps:
            rcv.wait_recv()
        for snd, _ in cps:
            snd.wait_send()
        finish()

    outs = _hbm_calls(body, list(sbs) + [small], _scatter_shapes(sbs, saxes) + [jax.ShapeDtypeStruct(small.shape, small.dtype)],
                      3 * nt + 7, {nt: nt}, name)
    return outs[:nt], outs[nt]


def _join_halves(fins, haxes, name):
    nt = len(fins)

    def body(*refs):
        outs, (send_sems, recv_sems) = refs[nt:2 * nt], refs[2 * nt:]
        x, y, c = _here()
        cps = [_rcopy(_half(o, ax, c), _half(o, ax, c), send_sems, recv_sems, t, (x, y, 1 - c))
               for t, (o, ax) in enumerate(zip(outs, haxes))]
        for cp in cps:
            cp.start()
        for t, (o, ax) in enumerate(zip(outs, haxes)):
            _rcopy(_half(o, ax, 1 - c), _half(o, ax, 1 - c), send_sems, recv_sems, t, (x, y, 1 - c)).wait_recv()
        for cp in cps:
            cp.wait_send()

    return _hbm_calls(body, fins, [jax.ShapeDtypeStruct(f.shape, f.dtype) for f in fins], nt, {t: t for t in range(nt)}, name)


def _all_gather_small(buf, name):
    def body(_, out_ref, send_sems, recv_sems):
        cps = _small_gather_copies(out_ref, send_sems, recv_sems, 0)
        for snd, _ in cps:
            snd.start()
        for _, rcv in cps:
            rcv.wait_recv()
        for snd, _ in cps:
            snd.wait_send()

    return _hbm_calls(body, [buf], [jax.ShapeDtypeStruct(buf.shape, buf.dtype)], 7, {0: 0}, name)[0]


def _place(block, n, index):
    buf = jnp.zeros((n,) + block.shape[1:], block.dtype)
    return lax.dynamic_update_slice_in_dim(buf, block, index, axis=0)


def _add_half(g, other, hax, cidx, name):
    r, cw = other.shape
    tr = _pick(r, tuple(t for t in (512, 256, 128, 64, 32, 16, 8) if t * cw * 4 <= (1 << 21)))
    nr = r // tr

    def body(c_ref, g_ref, o_ref, s_ref, sb_ref):
        s = g_ref[...] + o_ref[...]
        s_ref[...] = s
        sb_ref[...] = s.astype(BF16)

    g_map = (lambda i, c: (c[0] * nr + i, 0)) if hax == 0 else (lambda i, c: (i, c[0]))
    blk = pl.BlockSpec((tr, cw), lambda i, c: (i, 0))
    return pl.pallas_call(
        body,
        grid_spec=pltpu.PrefetchScalarGridSpec(
            num_scalar_prefetch=1, grid=(nr,), in_specs=[pl.BlockSpec((tr, cw), g_map), blk], out_specs=[blk, blk]),
        out_shape=[jax.ShapeDtypeStruct((r, cw), F32), jax.ShapeDtypeStruct((r, cw), BF16)],
        compiler_params=_cp("parallel"), name=name)(cidx, g, other)


def _add_chips(s, recv, sax, chip_idx, cidx, name):
    _, r, cw = recv.shape
    tr = _pick(r, tuple(t for t in (512, 352, 256, 128, 64, 32, 16, 8) if t * cw * 4 <= (1 << 21)))
    nr = r // tr

    def body(chip_ref, c_ref, s_ref, r_ref, out_ref):
        out_ref[...] = ((s_ref[...] + r_ref[0].astype(F32)) + r_ref[1].astype(F32)) + r_ref[2].astype(F32)

    if sax == 0:
        s_map, o_map, shape = (lambda i, chip, c: (chip[0] * nr + i, 0)), (lambda i, chip, c: (i, c[0])), (r, 2 * cw)
    else:
        s_map, o_map, shape = (lambda i, chip, c: (i, chip[0])), (lambda i, chip, c: (c[0] * nr + i, 0)), (2 * r, cw)
    return pl.pallas_call(
        body,
        grid_spec=pltpu.PrefetchScalarGridSpec(
            num_scalar_prefetch=2, grid=(nr,),
            in_specs=[pl.BlockSpec((tr, cw), s_map), pl.BlockSpec((3, tr, cw), lambda i, chip, c: (0, i, 0))],
            out_specs=pl.BlockSpec((tr, cw), o_map)),
        out_shape=jax.ShapeDtypeStruct(shape, F32), compiler_params=_cp("parallel"), name=name)(chip_idx, cidx, s, recv)


IN_SHARD = IN_COLS // N_SHARD
IN_SLOT = INP // N_SHARD
IN_PIECES = ((0, 512, QA), (512, 640, KA), (640, 768, VA), (768, 1280, QF), (1280, 1792, KF), (1792, 2304, VF),
             (2304, 2312, FL), (2312, 2824, XC), (2824, 3336, YC), (3336, 6408, GT))


def _gathered_to_kernel_cols(w):
    parts, pos = [], 0
    for a, b, k in sorted(IN_PIECES, key=lambda p: p[2]):
        assert k == pos
        while a < b:
            j = a // IN_SHARD
            e = min(b, (j + 1) * IN_SHARD)
            g = j * IN_SLOT + a - j * IN_SHARD
            parts.append(w[..., g:g + e - a])
            pos += e - a
            a = e
    parts.append(jnp.zeros(w.shape[:-1] + (INP - pos,), w.dtype))
    return jnp.concatenate(parts, axis=-1)


def _kernel_to_gathered_cols(w):
    parts = []
    for j in range(N_SHARD):
        lo, hi = j * IN_SHARD, (j + 1) * IN_SHARD
        for a, b, k in IN_PIECES:
            s, e = max(a, lo), min(b, hi)
            if s < e:
                parts.append(w[..., k + s - a:k + e - a])
        parts.append(jnp.zeros(w.shape[:-1] + (IN_SLOT - IN_SHARD,), w.dtype))
    return jnp.concatenate(parts, axis=-1)


def _pair_blocks(w):
    z = jnp.zeros((4, 64, 64), w.dtype)
    w = w.reshape(4, 2, 64, 64)
    top = jnp.concatenate([w[:, 0], z], axis=2)
    bot = jnp.concatenate([z, w[:, 1]], axis=2)
    return jnp.concatenate([top, bot], axis=1)


def _unpair_blocks(w):
    return jnp.stack([w[:, :64, :64], w[:, 64:, 64:]], axis=1).reshape(8, 64, 64)


BIG = ("w_in", "w_branch", "w_out", "w_ffn_in", "w_ffn_out")
TINY = ("conv_w", "meta_tokens")
SMALL = ("rel_bias_table", "norm_mix", "swa_sinks", "fox_forget_bias", "conv_b", "lru_w_r", "lru_b_r", "lru_w_i",
         "lru_b_i", "lru_lambda", "norm_ffn", "norm_final")
SHARD_AXIS = {"conv_w": 2, "meta_tokens": 1}
BIG_AXIS = {"w_in": 2, "w_branch": 2, "w_out": 1, "w_ffn_in": 2, "w_ffn_out": 1}


def _pack(d, names):
    flat = jnp.concatenate([d[n].reshape(-1) for n in names])
    pad = (-flat.shape[0]) % (256 * 128)
    return jnp.concatenate([flat, jnp.zeros((pad,), F32)]).reshape(-1, 128)


def _unpack(buf, names, shapes):
    flat, out, off = buf.reshape(-1), {}, 0
    for n in names:
        sz = int(np.prod(shapes[n]))
        out[n] = flat[off:off + sz].reshape(shapes[n])
        off += sz
    return out


def _layer_layout(n, a):
    if n == "w_in":
        return _gathered_to_kernel_cols(a)
    return a.reshape(3, LW, D) if n == "w_branch" else a


def _local_step(x, tgt, W, placed=None):
    S = x.shape[0]
    T = S + BLK
    tm = _pick(T, (1408, 384, 128))
    bucket = jnp.asarray(_bucket_table())
    bias = _bias_build(W["rel_bias_table"], bucket, "bias_build")
    h = jnp.concatenate([jnp.zeros((NPAD, D), F32), W["meta_tokens"], x], axis=0)
    if placed is None:
        WL = {n: [W[n][l] for l in range(DEPTH)] for n in BIG}
    else:
        WL = {n: [W[n]] + [None] * (DEPTH - 1) for n in BIG}

    saved = []
    for l in range(DEPTH):
        sv = {"h0": h}
        u, u_t = _rms_fwd(h, W["norm_mix"][l], "rms_mix_fwd")
        proj = _mm(u, WL["w_in"][l], tm=tm, tn=512, tk=D, name="mm_in_fwd")
        oa, oa_t = _swa_fwd(proj, bias, W["swa_sinks"][l], "swa_fwd")
        fb = W["fox_forget_bias"][l].reshape(NH, 1)
        qaug, kaug, kaug_t, vm_t, vo = _fox_prep(proj, _cum_fwd(proj, fb, "cum_fwd"), "fox_prep")
        if placed is not None and l + 1 < DEPTH:
            of, of_t, lse0, lse1, *got = _fox_fwd(qaug, kaug_t, vo, "fox_fwd_gather", gather=(placed[l + 1], GATHER_AXES))
            for n, a in zip(BIG, got):
                WL[n][l + 1] = _layer_layout(n, a)
            lse = [lse0, lse1]
        else:
            of, of_t, *lse = _fox_fwd(qaug, kaug_t, vo, "fox_fwd")
        lru_vec = jnp.concatenate([W["lru_b_r"][l][None], W["lru_b_i"][l][None], W["lru_lambda"][l][None],
                                   W["conv_b"][l][None], jnp.zeros((4, LW), F32)], axis=0)
        oc, oc_t, hs = _lru_fwd(proj, W["conv_w"][l], W["lru_w_r"][l], W["lru_w_i"][l], lru_vec, "lru_fwd")
        *bs, merged, merged_t = _branch_merge_fwd(oa, of, oc, WL["w_branch"][l], proj, "branch_merge_fwd")
        h2 = _mm(merged, WL["w_out"][l], res=h, tm=tm, tn=512, tk=D, name="mm_out_fwd")
        u2, u2_t = _rms_fwd(h2, W["norm_ffn"][l], "rms_ffn_fwd")
        gate, up, act, act_t = _ffn_in_swiglu_fwd(u2, WL["w_ffn_in"][l], "ffn_in_swiglu_fwd")
        h = _mm(act, WL["w_ffn_out"][l], res=h2, tm=tm, tn=512, tk=DFF, name="mm_ffn_out_fwd")
        sv.update(u_t=u_t, proj=proj, o_t=(oa_t, of_t, oc_t), of=of, lse=lse, hs=hs, fb=fb, qaug=qaug, kaug=kaug, kaug_t=kaug_t, vm_t=vm_t,
                  lru_vec=lru_vec, bs=bs, merged_t=merged_t, h2=h2, u2_t=u2_t, gate=gate, up=up, act_t=act_t)
        saved.append(sv)

    dh, dhb, dg_final, loss_vec = _loss_head(h, tgt, W["norm_final"], "loss_head")
    loss = loss_vec[0, 0]

    small = ("norm_mix", "swa_sinks", "fox_forget_bias", "conv_w", "conv_b", "lru_w_r", "lru_b_r", "lru_w_i", "lru_b_i",
             "lru_lambda", "norm_ffn")
    G = {n: [None] * DEPTH for n in small}
    G["norm_final"] = dg_final.reshape(D)
    GW = {n: [None] * DEPTH for n in BIG}
    dist = placed is not None
    if dist:
        x_, y_, c_ = _here()
        cidx = jnp.reshape(c_, (1,)).astype(jnp.int32)
        chip = jnp.reshape(2 * x_ + y_, (1,)).astype(jnp.int32)

    def finish_layer(lp, ss, recv):
        fins = [_add_chips(s, r, ax, chip, cidx, "rs_add_chips_" + n) for n, s, r, ax in zip(BIG, ss, recv, GATHER_AXES)]
        for n, f in zip(BIG, _join_halves(fins, HALF_AXES, "rs_join_halves")):
            GW[n][lp] = f

    pend = None
    dbias = jnp.zeros((NH, BLK, 2 * BLK), F32)
    tkT = tkL = tm
    for l in reversed(range(DEPTH)):
        sv = saved[l]
        dw = {}
        dw["w_ffn_out"] = _mm(sv["act_t"], dhb, tm=_pick(DFF, (1408, 256)), tn=D, tk=tkL,
                              name="mm_ffn_out_dw")
        dgate, dup = _ffn_out_dx_swiglu_bwd(dhb, WL["w_ffn_out"][l], sv["gate"], sv["up"], "ffn_out_dx_swiglu_bwd")
        u2t = sv["u2_t"]
        du2, buf = None, None
        for half, dpart in enumerate((dgate, dup)):
            buf = _mm(u2t, dpart, tm=D, tn=_pick(DFF, (1408, 256)), tk=tkL, slab=(buf, 0, 1),
                      col0=half * DFF, cols=2 * DFF, name="mm_ffn_in_dw")
            du2 = _mm(dpart, WL["w_ffn_in"][l], tb=True, res=du2, b_k0=half * DFF, tm=tm, tn=512, tk=DFF,
                      name="mm_ffn_in_dx")
        dw["w_ffn_in"] = buf.reshape(D, 2 * DFF)
        dh, dhb, dgn = _rms_bwd(du2, sv["h2"], W["norm_ffn"][l], dh, "rms_ffn_bwd")
        G["norm_ffn"][l] = dgn.reshape(D)
        dw["w_out"] = _mm(sv["merged_t"], dhb, tm=D, tn=D, tk=tkL, name="mm_out_dw")
        db0, db1, db2, dg0, dg1, dg2 = _out_dx_merge_bwd(dhb, WL["w_out"][l], sv["proj"], *sv["bs"], "out_dx_merge_bwd")
        dos, buf = [], None
        for g, (o_t, db) in enumerate(zip(sv["o_t"], (db0, db1, db2))):
            buf = _mm(o_t, db, tm=LW, tn=D, tk=T, slab=(buf, g, 3), name="mm_branch_dw")
            dos.append(_mm(db, WL["w_branch"][l][g], tb=True, out_dtype=BF16, tm=tm, tn=LW, tk=D, name="mm_branch_dx"))
        dw["w_branch"] = buf.reshape(3 * LW, D)
        dqa, dkb, dvb, dbias, dsk = _swa_bwd(sv["proj"], bias, W["swa_sinks"][l], dos[0], dbias, "swa_bwd")
        dka, dva = _band_fold(dkb, dvb, "swa_band_fold")
        G["swa_sinks"][l] = dsk[0, :NH]
        delta = _fox_delta(dos[1], sv["of"], "fox_delta")
        fox_args = (sv["qaug"], sv["kaug"], sv["kaug_t"], sv["vm_t"], dos[1], sv["lse"], delta)
        if pend is not None:
            dqf, dqx, dkf, dvf, dkx, *recv = _fox_bwd(*fox_args, "fox_bwd_scatter", scatter=(pend[2], GATHER_AXES))
            finish_layer(pend[0], pend[1], recv)
            pend = None
        else:
            dqf, dqx, dkf, dvf, dkx = _fox_bwd(*fox_args, "fox_bwd")
        dfl, dfb = _cum_bwd(dqx, dkx, sv["proj"], sv["fb"], "cum_bwd")
        G["fox_forget_bias"][l] = dfb[:, 0]
        dxc, dyc, dwr, dwi, dvec = _lru_bwd(sv["proj"], sv["hs"], dos[2], W["conv_w"][l], W["lru_w_r"][l], W["lru_w_i"][l],
                                            sv["lru_vec"], "lru_bwd")
        G["lru_w_r"][l], G["lru_w_i"][l] = _unpair_blocks(dwr), _unpair_blocks(dwi)
        G["lru_b_r"][l], G["lru_b_i"][l], G["lru_lambda"][l], G["conv_b"][l] = dvec[0], dvec[1], dvec[2], dvec[3]
        G["conv_w"][l] = dvec[4:8]
        dproj = jnp.concatenate([dqa, dqf, dkf, dvf, dxc, dyc, dg0, dg1, dg2, dka, dva, dfl], axis=1)
        dw["w_in"] = _mm(sv["u_t"], dproj, tm=D, tn=IN_SLOT, tk=tkT, name="mm_in_dw")
        du = _mm(dproj, WL["w_in"][l], tb=True, tm=tm, tn=512, tk=_pick(INP, (3328, 512)), name="mm_in_dx")
        dh, dhb, dgn = _rms_bwd(du, sv["h0"], W["norm_mix"][l], dh, "rms_mix_bwd")
        G["norm_mix"][l] = dgn.reshape(D)
        if dist:
            gs = [dw[n] for n in BIG]
            pairs = [_add_half(g, r, hax, cidx, "rs_add_half_" + n)
                     for n, g, r, hax in zip(BIG, gs, _swap_halves(gs, HALF_AXES, "rs_swap_halves"), HALF_AXES)]
            ss, sbs = [list(t) for t in zip(*pairs)]
            ss[0], sbs[0] = _kernel_to_gathered_cols(ss[0]), _kernel_to_gathered_cols(sbs[0])
            pend = (l, ss, sbs)
        else:
            for n in BIG:
                GW[n][l] = dw[n]
    grads = {n: (jnp.stack(v) if isinstance(v, list) else v) for n, v in G.items()}
    grads["rel_bias_table"] = _bias_bwd(dbias, bucket, "bias_bwd")
    grads["meta_tokens"] = dh[NPAD:BLK]
    if dist:
        mine = _place(_pack(grads, SMALL + TINY)[None], 8, 4 * x_ + 2 * y_ + c_)
        recv, grads["small_gathered"] = _scatter_to_chips(pend[2], GATHER_AXES, mine, "rs_scatter")
        finish_layer(pend[0], pend[1], recv)
    grads.update({n: jnp.stack(GW[n]) for n in BIG})
    return loss, dh[BLK:], grads


NAMES = ("meta_tokens", "rel_bias_table", "norm_mix", "w_in", "swa_sinks", "fox_forget_bias", "conv_w", "conv_b",
         "lru_w_r", "lru_b_r", "lru_w_i", "lru_b_i", "lru_lambda", "w_branch", "w_out", "norm_ffn", "w_ffn_in",
         "w_ffn_out", "norm_final")


def _three_d(n, a):
    return a.reshape(DEPTH, 3 * LW, -1) if n == "w_branch" else a


GATHER_AXES = [BIG_AXIS[n] - 1 for n in BIG]
HALF_AXES = [1 - a for a in GATHER_AXES]


def _gather_weights(P):
    x, y, c = _here()
    mine, me = 2 * x + y, 4 * x + 2 * y + c
    placed = []
    for l in range(DEPTH):
        bufs = []
        for n in BIG:
            shard = _three_d(n, P[n])[l].astype(BF16)
            if n == "w_in":
                shard = jnp.pad(shard, ((0, 0), (0, IN_SLOT - IN_SHARD)))
            zero = jnp.zeros_like(shard)
            bufs.append(jnp.concatenate([jnp.where(mine == j, shard, zero) for j in range(N_SHARD)], axis=BIG_AXIS[n] - 1))
        placed.append(bufs)
    full = {n: _layer_layout(n, a) for n, a in zip(BIG, _all_gather_weights(placed[0], GATHER_AXES, "ag_weights"))}
    tiny = _all_gather_small(_place(_pack(P, TINY)[None], 8, me), "ag_tiny_weights")
    parts = [_unpack(tiny[2 * j], TINY, {n: P[n].shape for n in TINY}) for j in range(N_SHARD)]
    for n in TINY:
        full[n] = jnp.concatenate([p[n] for p in parts], axis=SHARD_AXIS[n])
    for n in SMALL:
        full[n] = P[n]
    full["lru_w_r"] = jnp.stack([_pair_blocks(P["lru_w_r"][l]) for l in range(DEPTH)]).astype(BF16)
    full["lru_w_i"] = jnp.stack([_pair_blocks(P["lru_w_i"][l]) for l in range(DEPTH)]).astype(BF16)
    return full, placed


def _reduce_grads(grads, P):
    x, y, c = _here()
    mine, me = 2 * x + y, 4 * x + 2 * y + c
    out = {n: grads[n].reshape(P[n].shape) for n in BIG if n != "w_in"}
    out["w_in"] = grads["w_in"][:, :, :IN_SHARD]
    names = SMALL + TINY
    small = _unpack(_sum_lead(grads["small_gathered"], "sum_small_grads"), names, {n: grads[n].shape for n in names})
    for n in SMALL:
        out[n] = small[n]
    for n in TINY:
        w = P[n].shape[SHARD_AXIS[n]]
        out[n] = lax.dynamic_slice_in_dim(small[n], mine * w, w, axis=SHARD_AXIS[n])
    return out


def _update(P, Gd, M, V):
    delta, new_m, new_v = {}, {}, {}
    for n in BIG + TINY:
        shp = P[n].shape
        two = (int(np.prod(shp[:-1])), shp[-1])
        d, m, v = _adamw(P[n].reshape(two), Gd[n].reshape(two), M[n].reshape(two), V[n].reshape(two), "adamw_" + n)
        delta[n], new_m[n], new_v[n] = d.reshape(shp), m.reshape(shp), v.reshape(shp)
    shapes = {n: P[n].shape for n in SMALL}
    d, m, v = _adamw(_pack(P, SMALL), _pack(Gd, SMALL), _pack(M, SMALL), _pack(V, SMALL), "adamw_small")
    for dst, buf in ((delta, d), (new_m, m), (new_v, v)):
        dst.update(_unpack(buf, SMALL, shapes))
    return delta, new_m, new_v


def kernel(x, meta_tokens, rel_bias_table, norm_mix, w_in, swa_sinks, fox_forget_bias, conv_w, conv_b, lru_w_r, lru_b_r, lru_w_i, lru_b_i, lru_lambda, w_branch, w_out, norm_ffn, w_ffn_in, w_ffn_out, norm_final, loss_target, m_meta_tokens, m_rel_bias_table, m_norm_mix, m_w_in, m_swa_sinks, m_fox_forget_bias, m_conv_w, m_conv_b, m_lru_w_r, m_lru_b_r, m_lru_w_i, m_lru_b_i, m_lru_lambda, m_w_branch, m_w_out, m_norm_ffn, m_w_ffn_in, m_w_ffn_out, m_norm_final, v_meta_tokens, v_rel_bias_table, v_norm_mix, v_w_in, v_swa_sinks, v_fox_forget_bias, v_conv_w, v_conv_b, v_lru_w_r, v_lru_b_r, v_lru_w_i, v_lru_b_i, v_lru_lambda, v_w_branch, v_w_out, v_norm_ffn, v_w_ffn_in, v_w_ffn_out, v_norm_final):
    P = dict(zip(NAMES, (meta_tokens, rel_bias_table, norm_mix, w_in, swa_sinks, fox_forget_bias, conv_w, conv_b, lru_w_r,
                         lru_b_r, lru_w_i, lru_b_i, lru_lambda, w_branch, w_out, norm_ffn, w_ffn_in, w_ffn_out, norm_final)))
    M = dict(zip(NAMES, (m_meta_tokens, m_rel_bias_table, m_norm_mix, m_w_in, m_swa_sinks, m_fox_forget_bias, m_conv_w,
                         m_conv_b, m_lru_w_r, m_lru_b_r, m_lru_w_i, m_lru_b_i, m_lru_lambda, m_w_branch, m_w_out, m_norm_ffn,
                         m_w_ffn_in, m_w_ffn_out, m_norm_final)))
    V = dict(zip(NAMES, (v_meta_tokens, v_rel_bias_table, v_norm_mix, v_w_in, v_swa_sinks, v_fox_forget_bias, v_conv_w,
                         v_conv_b, v_lru_w_r, v_lru_b_r, v_lru_w_i, v_lru_b_i, v_lru_lambda, v_w_branch, v_w_out, v_norm_ffn,
                         v_w_ffn_in, v_w_ffn_out, v_norm_final)))
    W, placed = _gather_weights(P)
    loss_local, grad_x, grads = _local_step(x[0], loss_target[0], W, placed)
    loss = lax.psum(loss_local, ("x", "y", "c"))
    Gd = _reduce_grads(grads, P)
    delta, new_m, new_v = _update(P, Gd, M, V)
    return (loss, grad_x[None], *[Gd[n] for n in NAMES], *[delta[n] for n in NAMES],
            *[new_m[n] for n in NAMES], *[new_v[n] for n in NAMES])
```

```python
import math

import numpy as np
import jax
import jax.numpy as jnp
from jax import lax
from jax.experimental import pallas as pl
from jax.experimental.pallas import tpu as pltpu

F32, BF16 = jnp.float32, jnp.bfloat16
MESH = pl.DeviceIdType.MESH
ANY = pl.BlockSpec(memory_space=pl.ANY)
SMEM = pl.BlockSpec(memory_space=pltpu.SMEM)

D = 1024
DEPTH = 4
BLK = 128
N_META = 16
NPAD = 112
NH = 8
LW = 512
DFF = 2816
EPS = 1e-6
NEG = -1e30
SCALE = 0.125
LRU_C = 8.0
REL_BUCKETS = 32
N_SHARD = 4
QA, QF, KF, VF, XC, YC, GT, KA, VA, FL, INP = 0, 512, 1024, 1536, 2048, 2560, 3072, 6144, 6272, 6400, 6656
IN_COLS = 6408
VMEM_LIMIT = 48 * 1024 * 1024

ADAM_LR, ADAM_B1, ADAM_B2, ADAM_EPS, ADAM_WD, ADAM_STEP = 0.001, 0.9, 0.999, 1e-08, 0.01, 10


def _cp(*sem):
    return pltpu.CompilerParams(dimension_semantics=sem or None, vmem_limit_bytes=VMEM_LIMIT)


def _pick(n, prefs):
    for p in prefs:
        if n % p == 0:
            return p
    return n


def _rt(T):
    return _pick(T, (384, 128))


def _sigmoid(z):
    return 1.0 / (1.0 + jnp.exp(-z))


def _log_sigmoid(z):
    return jnp.minimum(z, 0.0) - jnp.log(1.0 + jnp.exp(-jnp.abs(z)))


def _gelu(y):
    c = math.sqrt(2.0 / math.pi)
    return 0.5 * y * (1.0 + jnp.tanh(c * (y + 0.044715 * y * y * y)))


def _gelu_grad(y):
    c = math.sqrt(2.0 / math.pi)
    t = jnp.tanh(c * (y + 0.044715 * y * y * y))
    return 0.5 * (1.0 + t) + 0.5 * y * (1.0 - t * t) * c * (1.0 + 3.0 * 0.044715 * y * y)


def _neg_expm1(z):
    series = -z * (1.0 + z * (0.5 + z * (1.0 / 6.0 + z * (1.0 / 24.0 + z * (1.0 / 120.0)))))
    return jnp.where(z > -0.1, series, 1.0 - jnp.exp(z))


def _dot(a, b, ca, cb):
    return lax.dot_general(a, b, (((ca,), (cb,)), ((), ())), preferred_element_type=F32)


def _mm(a, b, *, ta=False, tb=False, res=None, out_dtype=F32, tm, tn, tk, name, slab=None, b_k0=0, col0=0, cols=None):
    M, K = (a.shape[1], a.shape[0]) if ta else a.shape
    N = b.shape[0] if tb else b.shape[1]
    assert (b.shape[1] if tb else b.shape[0]) >= K + b_k0 and M % tm == 0 and N % tn == 0 and K % tk == 0, (name, a.shape, b.shape)
    assert b_k0 % tk == 0 and col0 % tn == 0
    nk, kb, jb = K // tk, b_k0 // tk, col0 // tn
    ca, cb = (0 if ta else 1), (1 if tb else 0)
    n_in = 2 + (res is not None) + (slab is not None and slab[0] is not None)

    def body(*refs):
        a_ref, b_ref = refs[:2]
        r_ref = refs[2] if res is not None else None
        o_ref = refs[n_in]
        part = _dot(a_ref[...].astype(BF16), b_ref[...].astype(BF16), ca, cb)

        def fin(acc):
            if res is not None:
                acc = acc + r_ref[...]
            o_ref[...] = acc.astype(out_dtype)

        if nk == 1:
            fin(part)
        else:
            acc_ref = refs[-1]
            k = pl.program_id(2)

            @pl.when(k == 0)
            def _():
                acc_ref[...] = part

            @pl.when(k > 0)
            def _():
                acc_ref[...] += part

            @pl.when(k == nk - 1)
            def _():
                fin(acc_ref[...])

    a_spec = pl.BlockSpec((tk, tm), lambda i, j, k: (k, i)) if ta else pl.BlockSpec((tm, tk), lambda i, j, k: (i, k))
    b_spec = (pl.BlockSpec((tn, tk), lambda i, j, k: (j, k + kb)) if tb
              else pl.BlockSpec((tk, tn), lambda i, j, k: (k + kb, j)))
    o_spec = pl.BlockSpec((tm, tn), lambda i, j, k: (i, j))
    in_specs, ops = [a_spec, b_spec], [a, b]
    if res is not None:
        in_specs.append(o_spec)
        ops.append(res)
    out_shape, aliases = jax.ShapeDtypeStruct((M, N), out_dtype), {}
    if slab is not None:
        buf, idx, n = slab
        o_spec = pl.BlockSpec((None, tm, tn), lambda i, j, k: (idx, i, j + jb))
        out_shape = jax.ShapeDtypeStruct((n, M, cols or N), out_dtype)
        if buf is not None:
            aliases = {len(ops): 0}
            in_specs.append(ANY)
            ops.append(buf)
    return pl.pallas_call(
        body, grid=(M // tm, N // tn, nk), in_specs=in_specs, out_specs=o_spec, out_shape=out_shape,
        input_output_aliases=aliases, scratch_shapes=[pltpu.VMEM((tm, tn), F32)] if nk > 1 else [],
        compiler_params=_cp("parallel", "parallel", "arbitrary"), name=name)(*ops)


def _rms_fwd(h, g, name):
    T = h.shape[0]
    tr = _rt(T)

    def body(h_ref, g_ref, u_ref, ut_ref):
        x = h_ref[...]
        r = lax.rsqrt(jnp.mean(x * x, axis=-1, keepdims=True) + EPS)
        u = (x * r * g_ref[...]).astype(BF16)
        u_ref[...] = u
        ut_ref[...] = u.T

    return pl.pallas_call(
        body, grid=(T // tr,),
        in_specs=[pl.BlockSpec((tr, D), lambda i: (i, 0)), pl.BlockSpec((1, D), lambda i: (0, 0))],
        out_specs=[pl.BlockSpec((tr, D), lambda i: (i, 0)), pl.BlockSpec((D, tr), lambda i: (0, i))],
        out_shape=[jax.ShapeDtypeStruct((T, D), BF16), jax.ShapeDtypeStruct((D, T), BF16)],
        compiler_params=_cp("parallel"), name=name)(h, g.reshape(1, D))


def _rms_bwd(du, h, g, dres, name):
    T = h.shape[0]
    tr = _rt(T)

    def body(du_ref, h_ref, g_ref, dres_ref, dh_ref, dhb_ref, dg_ref):
        x = h_ref[...]
        r = lax.rsqrt(jnp.mean(x * x, axis=-1, keepdims=True) + EPS)
        xh = x * r
        dy = du_ref[...]
        dxh = dy * g_ref[...]
        dx = r * (dxh - xh * jnp.mean(dxh * xh, axis=-1, keepdims=True))
        dh = dres_ref[...] + dx
        dh_ref[...] = dh
        dhb_ref[...] = dh.astype(BF16)
        part = jnp.sum(dy * xh, axis=0, keepdims=True)

        @pl.when(pl.program_id(0) == 0)
        def _():
            dg_ref[...] = part

        @pl.when(pl.program_id(0) > 0)
        def _():
            dg_ref[...] += part

    row = pl.BlockSpec((tr, D), lambda i: (i, 0))
    vec = pl.BlockSpec((1, D), lambda i: (0, 0))
    return pl.pallas_call(
        body, grid=(T // tr,), in_specs=[row, row, vec, row], out_specs=[row, row, vec],
        out_shape=[jax.ShapeDtypeStruct((T, D), F32), jax.ShapeDtypeStruct((T, D), BF16), jax.ShapeDtypeStruct((1, D), F32)],
        compiler_params=_cp("arbitrary"), name=name)(du, h, g.reshape(1, D), dres)


def _loss_head(h, tgt, g, name):
    T = h.shape[0]
    nb = T // BLK

    def body(h_ref, t_ref, g_ref, dh_ref, dhb_ref, dg_ref, loss_ref):
        i = pl.program_id(0)
        x = h_ref[...]
        r = lax.rsqrt(jnp.mean(x * x, axis=-1, keepdims=True) + EPS)
        xh = x * r
        gv = g_ref[...]
        tok = i >= 1
        err = jnp.where(tok, xh * gv - t_ref[...], 0.0)
        dy = err * (1.0 / D)
        dxh = dy * gv
        dx = r * (dxh - xh * jnp.mean(dxh * xh, axis=-1, keepdims=True))
        dh_ref[...] = dx
        dhb_ref[...] = dx.astype(BF16)
        dg = jnp.sum(dy * xh, axis=0, keepdims=True)
        ls = jnp.zeros((1, BLK), F32) + jnp.sum(err * err) * (0.5 / D)

        @pl.when(i == 0)
        def _():
            dg_ref[...] = dg
            loss_ref[...] = ls

        @pl.when(i > 0)
        def _():
            dg_ref[...] += dg
            loss_ref[...] += ls

    row = pl.BlockSpec((BLK, D), lambda i: (i, 0))
    vec = pl.BlockSpec((1, D), lambda i: (0, 0))
    return pl.pallas_call(
        body, grid=(nb,),
        in_specs=[row, pl.BlockSpec((BLK, D), lambda i: (jnp.maximum(i - 1, 0), 0)), vec],
        out_specs=[row, row, vec, pl.BlockSpec((1, BLK), lambda i: (0, 0))],
        out_shape=[jax.ShapeDtypeStruct((T, D), F32), jax.ShapeDtypeStruct((T, D), BF16),
                   jax.ShapeDtypeStruct((1, D), F32), jax.ShapeDtypeStruct((1, BLK), F32)],
        compiler_params=_cp("arbitrary"), name=name)(h, tgt, g.reshape(1, D))


def _bucket_table():
    q = np.arange(BLK)[:, None]
    k = np.arange(2 * BLK)[None, :]
    d = np.maximum(q + BLK - k, 0)
    max_exact = REL_BUCKETS // 2
    scaled = np.log(np.maximum(d, 1).astype(np.float32) / np.float32(max_exact)) / np.float32(math.log(128 / max_exact))
    large = np.minimum(max_exact + (scaled.astype(np.float32) * (REL_BUCKETS - max_exact)).astype(np.int32), REL_BUCKETS - 1)
    return np.where(d < max_exact, d, large).astype(np.int32)


def _bias_build(table, bucket, name):
    def body(t_ref, bk_ref, o_ref):
        bk = bk_ref[...]
        for h in range(NH):
            acc = jnp.zeros((BLK, 2 * BLK), F32)
            for b in range(REL_BUCKETS):
                acc = jnp.where(bk == b, t_ref[b, h], acc)
            o_ref[h] = acc

    return pl.pallas_call(
        body, in_specs=[SMEM, pl.BlockSpec(memory_space=pltpu.VMEM)], out_specs=pl.BlockSpec(memory_space=pltpu.VMEM),
        out_shape=jax.ShapeDtypeStruct((NH, BLK, 2 * BLK), F32), compiler_params=_cp(), name=name)(table, bucket)


def _bias_bwd(dbias, bucket, name):
    def body(d_ref, bk_ref, o_ref):
        bk = bk_ref[...]
        for h in range(NH):
            dh = d_ref[h]
            for b in range(REL_BUCKETS):
                o_ref[b, h] = jnp.sum(jnp.where(bk == b, dh, 0.0))

    return pl.pallas_call(
        body, in_specs=[pl.BlockSpec(memory_space=pltpu.VMEM)] * 2, out_specs=SMEM,
        out_shape=jax.ShapeDtypeStruct((REL_BUCKETS, NH), F32), compiler_params=_cp(), name=name)(dbias, bucket)


def _swa_specs(nq_cols):
    prev = lambda n: jnp.maximum(n - 1, 0)
    return [
        pl.BlockSpec((BLK, nq_cols), lambda n: (n, QA // nq_cols)),
        pl.BlockSpec((BLK, BLK), lambda n: (prev(n), KA // BLK)), pl.BlockSpec((BLK, BLK), lambda n: (n, KA // BLK)),
        pl.BlockSpec((BLK, BLK), lambda n: (prev(n), VA // BLK)), pl.BlockSpec((BLK, BLK), lambda n: (n, VA // BLK)),
    ]


def _swa_mask(n):
    row = lax.broadcasted_iota(jnp.int32, (BLK, 2 * BLK), 0)
    col = lax.broadcasted_iota(jnp.int32, (BLK, 2 * BLK), 1)
    dist = row + BLK - col
    return (dist >= 0) & (dist < BLK) & ((n - 1) * BLK + col >= NPAD)


def _swa_probs(qm, ksel, mask, bias_h, sink):
    s = _dot(qm, ksel, 1, 1) * SCALE
    s = jnp.where(mask, s + bias_h, NEG)
    m = jnp.maximum(jnp.max(s, axis=-1, keepdims=True), sink)
    p = jnp.exp(s - m)
    psink = jnp.exp(sink - m)
    inv = 1.0 / (jnp.sum(p, axis=-1, keepdims=True) + psink)
    return p * inv, psink * inv


def _swa_fwd(proj, bias, sinks, name):
    T = proj.shape[0]
    nb = T // BLK

    def body(sk_ref, q_ref, kp_ref, kc_ref, vp_ref, vc_ref, b_ref, o_ref, ot_ref):
        n = pl.program_id(0)
        lo = lax.broadcasted_iota(jnp.int32, (1, BLK), 1) < 64
        kb = jnp.concatenate([kp_ref[...], kc_ref[...]], axis=0)
        vb = jnp.concatenate([vp_ref[...], vc_ref[...]], axis=0)
        kbs = (kb.astype(BF16), pltpu.roll(kb, 64, 1).astype(BF16))
        vbs = (vb, pltpu.roll(vb, 64, 1))
        mask = _swa_mask(n)
        outs = []
        for pr in range(NH // 2):
            qp = q_ref[:, pr * BLK:(pr + 1) * BLK]
            kv = pr // 2
            acc = jnp.zeros((BLK, BLK), F32)
            for e in range(2):
                lm = lo if e == 0 else jnp.logical_not(lo)
                sw = 0 if kv == e else 1
                qm = jnp.where(lm, qp, 0.0).astype(BF16)
                pn, _ = _swa_probs(qm, kbs[sw], mask, b_ref[2 * pr + e], sk_ref[2 * pr + e])
                acc = acc + _dot(pn.astype(BF16), jnp.where(lm, vbs[sw], 0.0).astype(BF16), 1, 0)
            outs.append(acc)
        o = jnp.concatenate(outs, axis=1).astype(BF16)
        o_ref[...] = o
        ot_ref[...] = o.T

    return pl.pallas_call(
        body, grid=(nb,),
        in_specs=[SMEM] + _swa_specs(512) + [pl.BlockSpec((NH, BLK, 2 * BLK), lambda n: (0, 0, 0))],
        out_specs=[pl.BlockSpec((BLK, 512), lambda n: (n, 0)), pl.BlockSpec((512, BLK), lambda n: (0, n))],
        out_shape=[jax.ShapeDtypeStruct((T, 512), BF16), jax.ShapeDtypeStruct((512, T), BF16)],
        compiler_params=_cp("parallel"), name=name)(sinks, proj, proj, proj, proj, proj, bias)


def _swa_bwd(proj, bias, sinks, do, dbias_in, name):
    T = proj.shape[0]
    nb = T // BLK

    def body(sk_ref, q_ref, kp_ref, kc_ref, vp_ref, vc_ref, b_ref, do_ref, dbi_ref,
             dq_ref, dk_ref, dv_ref, db_ref, dsk_ref, sk_acc):
        n = pl.program_id(0)
        lane = lax.broadcasted_iota(jnp.int32, (1, BLK), 1)
        lo = lane < 64
        kb = jnp.concatenate([kp_ref[...], kc_ref[...]], axis=0)
        vb = jnp.concatenate([vp_ref[...], vc_ref[...]], axis=0)
        kbs = (kb, pltpu.roll(kb, 64, 1))
        vbs = (vb, pltpu.roll(vb, 64, 1))
        mask = _swa_mask(n)

        @pl.when(n == 0)
        def _():
            db_ref[...] = dbi_ref[...]
            sk_acc[...] = jnp.zeros_like(sk_acc)

        dqs = []
        dk = jnp.zeros((2 * BLK, BLK), F32)
        dv = jnp.zeros((2 * BLK, BLK), F32)
        for pr in range(NH // 2):
            qp = q_ref[:, pr * BLK:(pr + 1) * BLK]
            dop = do_ref[:, pr * BLK:(pr + 1) * BLK].astype(F32)
            kv = pr // 2
            dq = jnp.zeros((BLK, BLK), F32)
            for e in range(2):
                h = 2 * pr + e
                lm = lo if e == 0 else jnp.logical_not(lo)
                sw = 0 if kv == e else 1
                qm = jnp.where(lm, qp, 0.0)
                dom = jnp.where(lm, dop, 0.0)
                pn, ps = _swa_probs(qm.astype(BF16), kbs[sw].astype(BF16), mask, b_ref[h], sk_ref[h])
                dp = _dot(dom.astype(BF16), vbs[sw].astype(BF16), 1, 1)
                delta = jnp.sum(pn * dp, axis=-1, keepdims=True)
                ds = pn * (dp - delta)
                db_ref[h] += ds
                sk_acc[...] += jnp.where(lane == h, -(ps * delta), 0.0)
                dsb = (ds * SCALE).astype(BF16)
                dq = dq + _dot(dsb, jnp.where(lm, kbs[sw], 0.0).astype(BF16), 1, 0)
                qk = qm if sw == 0 else pltpu.roll(qm, 64, 1)
                dok = dom if sw == 0 else pltpu.roll(dom, 64, 1)
                dk = dk + _dot(dsb, qk.astype(BF16), 0, 0)
                dv = dv + _dot(pn.astype(BF16), dok.astype(BF16), 0, 0)
            dqs.append(dq)
        dq_ref[...] = jnp.concatenate(dqs, axis=1).astype(BF16)
        dk_ref[0] = dk
        dv_ref[0] = dv

        @pl.when(n == nb - 1)
        def _():
            dsk_ref[...] = jnp.sum(sk_acc[...], axis=0, keepdims=True)

    full_b = pl.BlockSpec((NH, BLK, 2 * BLK), lambda n: (0, 0, 0))
    band = pl.BlockSpec((1, 2 * BLK, BLK), lambda n: (n, 0, 0))
    return pl.pallas_call(
        body, grid=(nb,),
        in_specs=[SMEM] + _swa_specs(512) + [full_b, pl.BlockSpec((BLK, 512), lambda n: (n, 0)), full_b],
        out_specs=[pl.BlockSpec((BLK, 512), lambda n: (n, 0)), band, band, full_b, pl.BlockSpec((1, BLK), lambda n: (0, 0))],
        out_shape=[jax.ShapeDtypeStruct((T, 512), BF16), jax.ShapeDtypeStruct((nb, 2 * BLK, BLK), F32),
                   jax.ShapeDtypeStruct((nb, 2 * BLK, BLK), F32), jax.ShapeDtypeStruct((NH, BLK, 2 * BLK), F32),
                   jax.ShapeDtypeStruct((1, BLK), F32)],
        scratch_shapes=[pltpu.VMEM((BLK, BLK), F32)],
        compiler_params=_cp("arbitrary"), name=name)(sinks, proj, proj, proj, proj, proj, bias, do, dbias_in)


def _band_fold(dkb, dvb, name):
    nb = dkb.shape[0]

    def body(ko_ref, kn_ref, vo_ref, vn_ref, dk_ref, dv_ref):
        last = pl.program_id(0) == nb - 1
        dk_ref[...] = (ko_ref[0] + jnp.where(last, 0.0, kn_ref[0])).astype(BF16)
        dv_ref[...] = (vo_ref[0] + jnp.where(last, 0.0, vn_ref[0])).astype(BF16)

    own = pl.BlockSpec((1, BLK, BLK), lambda j: (j, 1, 0))
    nxt = pl.BlockSpec((1, BLK, BLK), lambda j: (jnp.minimum(j + 1, nb - 1), 0, 0))
    out = pl.BlockSpec((BLK, BLK), lambda j: (j, 0))
    return pl.pallas_call(
        body, grid=(nb,), in_specs=[own, nxt, own, nxt], out_specs=[out, out],
        out_shape=[jax.ShapeDtypeStruct((nb * BLK, BLK), BF16)] * 2,
        compiler_params=_cp("parallel"), name=name)(dkb, dkb, dvb, dvb)


def _token_major(x, width):
    full = jnp.concatenate([x, jnp.zeros((BLK - NH, BLK), F32)], axis=0).T
    return full if width == BLK else jnp.concatenate([full, jnp.zeros((BLK, width - BLK), F32)], axis=1)


def _cum_fwd(proj, fb, name):
    T = proj.shape[0]
    tr = _rt(T)

    def body(z_ref, fb_ref, c_ref, carry):
        g = pl.program_id(0)
        lane = lax.broadcasted_iota(jnp.int32, (NH, BLK), 1)

        @pl.when(g == 0)
        def _():
            carry[...] = jnp.zeros_like(carry)

        run = carry[...]
        for sb in range(tr // BLK):
            r = slice(sb * BLK, (sb + 1) * BLK)
            z = z_ref[r, :].T[0:NH, :] + fb_ref[...]
            x = jnp.where(g * tr + sb * BLK + lane >= NPAD, _log_sigmoid(z), 0.0)
            s = 1
            while s < BLK:
                x = x + jnp.where(lane >= s, pltpu.roll(x, s, 1), 0.0)
                s *= 2
            x = x + run
            run = jnp.zeros((NH, BLK), F32) + jnp.sum(jnp.where(lane == BLK - 1, x, 0.0), axis=-1, keepdims=True)
            c_ref[r, :] = _token_major(x, BLK)
        carry[...] = run

    return pl.pallas_call(
        body, grid=(T // tr,),
        in_specs=[pl.BlockSpec((tr, BLK), lambda g: (g, FL // BLK)), pl.BlockSpec((NH, 1), lambda g: (0, 0))],
        out_specs=pl.BlockSpec((tr, BLK), lambda g: (g, 0)), out_shape=jax.ShapeDtypeStruct((T, BLK), F32),
        scratch_shapes=[pltpu.VMEM((NH, BLK), F32)], compiler_params=_cp("arbitrary"), name=name)(proj, fb)


def _cum_bwd(dqx, dkx, proj, fb, name):
    T = proj.shape[0]
    tr = _rt(T)
    nb = T // tr

    def body(dq_ref, dk_ref, z_ref, fb_ref, dz_ref, db_ref, carry):
        k = pl.program_id(0)
        g = nb - 1 - k
        lane = lax.broadcasted_iota(jnp.int32, (NH, BLK), 1)

        @pl.when(k == 0)
        def _():
            carry[...] = jnp.zeros_like(carry)
            db_ref[...] = jnp.zeros_like(db_ref)

        def picked(ref, r, r_first, r_second):
            rows = []
            for p in range(NH // 2):
                t_ = ref[r, p * BLK:(p + 1) * BLK].T
                rows += [t_[r_first:r_first + 1, :], t_[r_second:r_second + 1, :]]
            return jnp.concatenate(rows, axis=0)

        run, tot = carry[...], jnp.zeros((NH, 1), F32)
        for sb in reversed(range(tr // BLK)):
            r = slice(sb * BLK, (sb + 1) * BLK)
            x = picked(dq_ref, r, 64, 0) - picked(dk_ref, r, 67, 3)
            s = 1
            while s < BLK:
                x = x + jnp.where(lane < BLK - s, pltpu.roll(x, BLK - s, 1), 0.0)
                s *= 2
            x = x + run
            run = jnp.zeros((NH, BLK), F32) + jnp.sum(jnp.where(lane == 0, x, 0.0), axis=-1, keepdims=True)
            z = z_ref[r, :].T[0:NH, :] + fb_ref[...]
            dz = jnp.where(g * tr + sb * BLK + lane >= NPAD, x * _sigmoid(-z), 0.0)
            tot = tot + jnp.sum(dz, axis=-1, keepdims=True)
            dz_ref[r, :] = _token_major(dz, 2 * BLK).astype(BF16)
        carry[...] = run
        db_ref[...] += tot

    rev = lambda k: nb - 1 - k
    wide = pl.BlockSpec((tr, 512), lambda k: (rev(k), 0))
    return pl.pallas_call(
        body, grid=(nb,),
        in_specs=[wide, wide, pl.BlockSpec((tr, BLK), lambda k: (rev(k), FL // BLK)), pl.BlockSpec((NH, 1), lambda k: (0, 0))],
        out_specs=[pl.BlockSpec((tr, 2 * BLK), lambda k: (rev(k), 0)), pl.BlockSpec((NH, BLK), lambda k: (0, 0))],
        out_shape=[jax.ShapeDtypeStruct((T, 2 * BLK), BF16), jax.ShapeDtypeStruct((NH, BLK), F32)],
        scratch_shapes=[pltpu.VMEM((NH, BLK), F32)], compiler_params=_cp("arbitrary"), name=name)(dqx, dkx, proj, fb)


def _fox_prep(proj, ccol, name):
    T = proj.shape[0]
    tr = _pick(T, (1408, 384, 128))

    def body(q_ref, k_ref, v_ref, cc_ref, qa_ref, ka_ref, kt_ref, vmt_ref, vo_ref):
        h = pl.program_id(1)
        lane = lax.broadcasted_iota(jnp.int32, (1, BLK), 1)
        own = (lane >> 6) == (h & 1)
        a0 = 64 * (1 - (h & 1))
        c = _lane_pick(cc_ref[...], lane, h)
        hi = c.astype(BF16).astype(F32)
        mid = (c - hi).astype(BF16).astype(F32)
        lo = (c - hi - mid).astype(BF16).astype(F32)
        ones = (lane >= a0 + 3) & (lane < a0 + 6)
        qa = jnp.where(own, q_ref[...] * SCALE, jnp.where(ones, 1.0, 0.0))
        qa = jnp.where(lane == a0, hi, jnp.where(lane == a0 + 1, mid, jnp.where(lane == a0 + 2, lo, qa)))
        ones = (lane >= a0) & (lane < a0 + 3)
        ka = jnp.where(own, k_ref[...], jnp.where(ones, 1.0, 0.0))
        ka = jnp.where(lane == a0 + 3, -hi, jnp.where(lane == a0 + 4, -mid, jnp.where(lane == a0 + 5, -lo, ka)))
        qa_ref[...] = qa.astype(BF16)
        kab = ka.astype(BF16)
        ka_ref[...] = kab
        kt_ref[...] = kab.T
        vm = jnp.where(own, v_ref[...], 0.0)
        vmt_ref[...] = vm.astype(BF16).T
        vo_ref[...] = jnp.where(lane == a0, 1.0, vm).astype(BF16)

    pair = lambda col0: pl.BlockSpec((tr, BLK), lambda i, h: (i, col0 // BLK + (h >> 1)))
    out = pl.BlockSpec((None, tr, BLK), lambda i, h: (h, i, 0))
    out_t = pl.BlockSpec((None, BLK, tr), lambda i, h: (h, 0, i))
    tok = jax.ShapeDtypeStruct((NH, T, BLK), BF16)
    return pl.pallas_call(
        body, grid=(T // tr, NH), in_specs=[pair(QF), pair(KF), pair(VF), pl.BlockSpec((tr, BLK), lambda i, h: (i, 0))],
        out_specs=[out, out, out_t, out_t, out],
        out_shape=[tok, tok, jax.ShapeDtypeStruct((NH, BLK, T), BF16), jax.ShapeDtypeStruct((NH, BLK, T), BF16), tok],
        compiler_params=_cp("parallel", "arbitrary"), name=name)(proj, proj, proj, ccol)


def _fox_fwd(qaug, kaug_t, vo, name, gather=None):
    T = qaug.shape[1]
    t = _rt(T)
    nt = T // t
    ng = len(gather[0]) if gather else 0

    pairs = [(i, j) for i in range(nt) for j in range(i + 1)]
    i_of = jnp.asarray(np.array([p[0] for p in pairs], np.int32))
    j_of = jnp.asarray(np.array([p[1] for p in pairs], np.int32))
    ns = len(pairs)

    def body(i_ref, j_ref, q0, q1, k0, k1, v0, v1, *rest):
        o_ref, ot_ref, lse0_ref, lse1_ref = rest[ng:ng + 4]
        m_ref, acc_ref = rest[2 * ng + 4:2 * ng + 6]
        p_, s_ = pl.program_id(0), pl.program_id(1)
        i, j = i_ref[s_], j_ref[s_]
        lane = lax.broadcasted_iota(jnp.int32, (1, BLK), 1)
        lo = lane < 64
        if gather:
            start, finish = _gather_plan(rest[ng + 4:2 * ng + 4], gather[1], *rest[2 * ng + 6:])
            pl.when((p_ == 0) & (s_ == 0))(start)

        @pl.when(j == 0)
        def _():
            m_ref[...] = jnp.full_like(m_ref, NEG)
            acc_ref[...] = jnp.zeros_like(acc_ref)

        def step(masked):
            for e, (q_ref, k_ref, v_ref) in enumerate(((q0, k0, v0), (q1, k1, v1))):
                s = _dot(q_ref[...], k_ref[...], 1, 0)
                if masked:
                    s = jnp.where(_fox_mask(i, j, t), s, NEG)
                m_old = m_ref[e]
                m_new = jnp.maximum(m_old, jnp.max(s, axis=-1, keepdims=True))
                m_ref[e] = m_new
                pe = jnp.exp(s - jnp.concatenate([m_new] * (t // BLK), axis=1))
                acc_ref[e] = jnp.exp(m_old - m_new) * acc_ref[e] + _dot(pe.astype(BF16), v_ref[...], 1, 0)

        pl.when((j < i) & (j > 0))(lambda: step(False))
        pl.when((j == i) | ((j == 0) & (i > 0)))(lambda: step(True))

        @pl.when(j == i)
        def _():
            rows = i * t + lax.broadcasted_iota(jnp.int32, (t, 1), 0)
            l0, l1 = _lane_pick(acc_ref[0], lane, 64), _lane_pick(acc_ref[1], lane, 0)
            o = jnp.where(rows >= NPAD, jnp.where(lo, acc_ref[0] / l0, acc_ref[1] / l1), 0.0).astype(BF16)
            o_ref[...] = o
            ot_ref[...] = o.T
            lse0_ref[...] = m_ref[0] + jnp.log(l0)
            lse1_ref[...] = m_ref[1] + jnp.log(l1)

        if gather:
            pl.when((p_ == NH // 2 - 1) & (s_ == ns - 1))(finish)

    qs = lambda e: pl.BlockSpec((None, t, BLK), lambda p, s, ii, jj: (2 * p + e, ii[s], 0))
    ks = lambda e: pl.BlockSpec((None, t, BLK), lambda p, s, ii, jj: (2 * p + e, jj[s], 0))
    kts = lambda e: pl.BlockSpec((None, BLK, t), lambda p, s, ii, jj: (2 * p + e, 0, jj[s]))
    rep = pl.BlockSpec((None, t, BLK), lambda p, s, ii, jj: (p, ii[s], 0))
    bufs = list(gather[0]) if gather else []
    return pl.pallas_call(
        body,
        grid_spec=pltpu.PrefetchScalarGridSpec(
            num_scalar_prefetch=2, grid=(NH // 2, ns),
            in_specs=[qs(0), qs(1), kts(0), kts(1), ks(0), ks(1)] + [ANY] * ng,
            out_specs=[pl.BlockSpec((t, BLK), lambda p, s, ii, jj: (ii[s], p)),
                       pl.BlockSpec((BLK, t), lambda p, s, ii, jj: (p, ii[s])), rep, rep] + [ANY] * ng,
            scratch_shapes=[pltpu.VMEM((2, t, BLK), F32), pltpu.VMEM((2, t, BLK), F32)]
            + ([pltpu.SemaphoreType.DMA((6 * ng,)), pltpu.SemaphoreType.DMA((6 * ng,))] if gather else [])),
        out_shape=[jax.ShapeDtypeStruct((T, 512), BF16), jax.ShapeDtypeStruct((512, T), BF16)]
        + [jax.ShapeDtypeStruct((NH // 2, T, BLK), F32)] * 2 + [jax.ShapeDtypeStruct(b.shape, b.dtype) for b in bufs],
        input_output_aliases={8 + g: 4 + g for g in range(ng)},
        compiler_params=(pltpu.CompilerParams(dimension_semantics=("arbitrary",) * 2, vmem_limit_bytes=VMEM_LIMIT,
                                              has_side_effects=True) if gather
                         else _cp("parallel", "arbitrary")), name=name)(i_of, j_of, qaug, qaug, kaug_t, kaug_t, vo, vo, *bufs)


def _fox_delta(do, o, name):
    T = do.shape[0]
    tr = _rt(T)

    def body(do_ref, o_ref, d0_ref, d1_ref):
        lo = lax.broadcasted_iota(jnp.int32, (1, BLK), 1) < 64
        prod = do_ref[...].astype(F32) * o_ref[...].astype(F32)
        d0_ref[...] = jnp.zeros((tr, BLK), F32) + jnp.sum(jnp.where(lo, prod, 0.0), axis=-1, keepdims=True)
        d1_ref[...] = jnp.zeros((tr, BLK), F32) + jnp.sum(jnp.where(lo, 0.0, prod), axis=-1, keepdims=True)

    blk = pl.BlockSpec((tr, BLK), lambda i, p: (i, p))
    rep = pl.BlockSpec((None, tr, BLK), lambda i, p: (p, i, 0))
    return pl.pallas_call(
        body, grid=(T // tr, NH // 2), in_specs=[blk, blk], out_specs=[rep, rep],
        out_shape=[jax.ShapeDtypeStruct((NH // 2, T, BLK), F32)] * 2,
        compiler_params=_cp("parallel", "parallel"), name=name)(do, o)


def _fox_bwd(qaug, kaug, kaug_t, vm_t, do, lses, deltas, name, scatter=None):
    T = qaug.shape[1]
    t = _rt(T)
    nt = T // t
    ng = len(scatter[0]) if scatter else 0
    pairs = [(i, j) for j in range(nt) for i in range(j, nt)]
    i_of = jnp.asarray(np.array([p[0] for p in pairs], np.int32))
    j_of = jnp.asarray(np.array([p[1] for p in pairs], np.int32))
    ns = len(pairs)

    def body(i_ref, j_ref, q0, q1, k0, k1, kt0, kt1, v0, v1, do_ref, lse0, lse1, dl0, dl1, *rest):
        dq_ref, dqx_ref, dk_ref, dv_ref, dkx_ref = rest[ng:ng + 5]
        dq_acc, dk_acc, dv_acc = rest[2 * ng + 5:2 * ng + 8]
        p_, s_ = pl.program_id(0), pl.program_id(1)
        i, j = i_ref[s_], j_ref[s_]
        lane = lax.broadcasted_iota(jnp.int32, (1, BLK), 1)
        lo = lane < 64
        if scatter:
            start, finish = _scatter_plan(rest[:ng], rest[ng + 5:2 * ng + 5], scatter[1], *rest[2 * ng + 8:])
            pl.when((p_ == 0) & (s_ == 0))(start)

        @pl.when(s_ == 0)
        def _():
            dq_acc[...] = jnp.zeros_like(dq_acc)

        @pl.when(i == j)
        def _():
            dk_acc[...] = jnp.zeros_like(dk_acc)
            dv_acc[...] = jnp.zeros_like(dv_acc)

        def step(masked):
            th = t // 2
            for rh in range(2):
                hr = slice(rh * th, (rh + 1) * th)
                dob = do_ref[hr, :]
                rows = pl.ds(pl.multiple_of(i * t + rh * th, th), th)
                wide = lambda ref: jnp.concatenate([ref[hr, :]] * (t // BLK), axis=1)
                for e, (q_ref, k_ref, kt_ref, v_ref, lse_ref, dl_ref) in enumerate(
                        ((q0, k0, kt0, v0, lse0, dl0), (q1, k1, kt1, v1, lse1, dl1))):
                    qh = q_ref[hr, :]
                    s = _dot(qh, kt_ref[...], 1, 0)
                    if masked:
                        row = i * t + rh * th + lax.broadcasted_iota(jnp.int32, (th, t), 0)
                        col = j * t + lax.broadcasted_iota(jnp.int32, (th, t), 1)
                        s = jnp.where((col <= row) & (col >= NPAD), s, NEG)
                    pe = jnp.exp(s - wide(lse_ref))
                    dp = _dot(dob, v_ref[...], 1, 0)
                    ds = (pe * (dp - wide(dl_ref))).astype(BF16)
                    dq_acc[e, rows, :] += _dot(ds, k_ref[...], 1, 0)
                    dk_acc[e] += _dot(ds, qh, 0, 0)
                    dv_acc[e] += _dot(pe.astype(BF16), dob, 0, 0)

        pl.when((i > j) & (j > 0))(lambda: step(False))
        pl.when((i == j) | ((j == 0) & (i > 0)))(lambda: step(True))

        @pl.when(i == nt - 1)
        def _():
            dk_ref[...] = jnp.where(lo, dk_acc[0], dk_acc[1]).astype(BF16)
            dv_ref[...] = jnp.where(lo, dv_acc[0], dv_acc[1]).astype(BF16)
            dkx_ref[...] = jnp.where(lo, dk_acc[1], dk_acc[0])

        @pl.when(s_ == ns - 1)
        def _():
            dq_ref[...] = (jnp.where(lo, dq_acc[0], dq_acc[1]) * SCALE).astype(BF16)
            dqx_ref[...] = jnp.where(lo, dq_acc[1], dq_acc[0])

        if scatter:
            pl.when((p_ == NH // 2 - 1) & (s_ == ns - 1))(finish)

    qs = lambda e: pl.BlockSpec((None, t, BLK), lambda p, s, ii, jj: (2 * p + e, ii[s], 0))
    ks = lambda e: pl.BlockSpec((None, t, BLK), lambda p, s, ii, jj: (2 * p + e, jj[s], 0))
    kts = lambda e: pl.BlockSpec((None, BLK, t), lambda p, s, ii, jj: (2 * p + e, 0, jj[s]))
    qside = pl.BlockSpec((t, BLK), lambda p, s, ii, jj: (ii[s], p))
    kside = pl.BlockSpec((t, BLK), lambda p, s, ii, jj: (jj[s], p))
    rep = pl.BlockSpec((None, t, BLK), lambda p, s, ii, jj: (p, ii[s], 0))
    whole = pl.BlockSpec((T, BLK), lambda p, s, ii, jj: (0, p))
    sums = list(scatter[0]) if scatter else []
    return pl.pallas_call(
        body,
        grid_spec=pltpu.PrefetchScalarGridSpec(
            num_scalar_prefetch=2, grid=(NH // 2, ns),
            in_specs=[qs(0), qs(1), ks(0), ks(1), kts(0), kts(1), kts(0), kts(1), qside, rep, rep, rep, rep] + [ANY] * ng,
            out_specs=[whole, whole, kside, kside, kside] + [ANY] * ng,
            scratch_shapes=[pltpu.VMEM((2, T, BLK), F32), pltpu.VMEM((2, t, BLK), F32), pltpu.VMEM((2, t, BLK), F32)]
            + ([pltpu.SemaphoreType.DMA((3 * ng,)), pltpu.SemaphoreType.DMA((3 * ng,))] if scatter else [])),
        out_shape=[jax.ShapeDtypeStruct((T, 512), BF16), jax.ShapeDtypeStruct((T, 512), F32),
                   jax.ShapeDtypeStruct((T, 512), BF16), jax.ShapeDtypeStruct((T, 512), BF16),
                   jax.ShapeDtypeStruct((T, 512), F32)] + (_scatter_shapes(sums, scatter[1]) if scatter else []),
        compiler_params=(pltpu.CompilerParams(dimension_semantics=("arbitrary",) * 2, vmem_limit_bytes=VMEM_LIMIT,
                                              has_side_effects=True) if scatter
                         else _cp("parallel", "arbitrary")), name=name)(
            i_of, j_of, qaug, qaug, kaug, kaug, kaug_t, kaug_t, vm_t, vm_t, do, *lses, *deltas, *sums)


def _fox_mask(i, j, t):
    row = i * t + lax.broadcasted_iota(jnp.int32, (t, t), 0)
    col = j * t + lax.broadcasted_iota(jnp.int32, (t, t), 1)
    return (col <= row) & (col >= NPAD)


def _lane_pick(x, lane, idx):
    return jnp.sum(jnp.where(lane == idx, x, 0.0), axis=-1, keepdims=True)


def _lru_gates(xc, wr_ref, wi_ref, vec_ref):
    xb = xc.astype(BF16)
    pre_r = jnp.concatenate([_dot(xb[:, p * BLK:(p + 1) * BLK], wr_ref[p], 1, 0) for p in range(LW // BLK)], axis=1)
    pre_i = jnp.concatenate([_dot(xb[:, p * BLK:(p + 1) * BLK], wi_ref[p], 1, 0) for p in range(LW // BLK)], axis=1)
    r = _sigmoid(pre_r + vec_ref[0:1, :])
    gi = _sigmoid(pre_i + vec_ref[1:2, :])
    log_a = LRU_C * r * _log_sigmoid(vec_ref[2:3, :])
    a = jnp.exp(log_a)
    mult = jnp.sqrt(_neg_expm1(2.0 * log_a))
    return r, gi, a, mult


def _conv(xbuf_ref, x, cw_ref, vec_ref, tr):
    return (cw_ref[3:4, :] * x + cw_ref[2:3, :] * xbuf_ref[7:7 + tr, :] + cw_ref[1:2, :] * xbuf_ref[6:6 + tr, :]
            + cw_ref[0:1, :] * xbuf_ref[5:5 + tr, :] + vec_ref[3:4, :])


def _lru_fwd(proj, cw, wr, wi, vec, name):
    T = proj.shape[0]
    tr = _rt(T)

    def body(x_ref, y_ref, cw_ref, wr_ref, wi_ref, vec_ref, oc_ref, oct_ref, hs_ref, xbuf, abuf, bbuf, hcar):
        i = pl.program_id(0)

        @pl.when(i == 0)
        def _():
            xbuf[0:8, :] = jnp.zeros((8, LW), F32)
            hcar[...] = jnp.zeros_like(hcar)

        x = x_ref[...]
        xbuf[8:8 + tr, :] = x
        xc = _conv(xbuf, x, cw_ref, vec_ref, tr)
        xbuf[0:8, :] = x[tr - 8:tr, :]
        _, gi, a, mult = _lru_gates(xc, wr_ref, wi_ref, vec_ref)
        rows = i * tr + lax.broadcasted_iota(jnp.int32, (tr, 1), 0)
        abuf[...] = a
        bbuf[...] = jnp.where(rows >= NPAD, mult * (gi * xc), 0.0)
        sub = lax.broadcasted_iota(jnp.int32, (8, 1), 0)

        def step(k, h):
            sl = pl.ds(pl.multiple_of(k * 8, 8), 8)
            a8, b8 = abuf[sl, :], bbuf[sl, :]
            for s in (1, 2, 4):
                ok = sub >= s
                b8 = jnp.where(ok, a8 * pltpu.roll(b8, s, 0) + b8, b8)
                a8 = jnp.where(ok, a8 * pltpu.roll(a8, s, 0), a8)
            h8 = a8 * h + b8
            bbuf[sl, :] = h8
            return h8[7:8, :]

        hcar[...] = lax.fori_loop(0, tr // 8, step, hcar[...])
        hs = bbuf[...]
        hs_ref[...] = hs
        oc = (hs * _gelu(y_ref[...])).astype(BF16)
        oc_ref[...] = oc
        oct_ref[...] = oc.T

    row = pl.BlockSpec((tr, LW), lambda i: (i, 0))
    full = lambda shape: pl.BlockSpec(shape, lambda i: (0,) * len(shape))
    return pl.pallas_call(
        body, grid=(T // tr,),
        in_specs=[pl.BlockSpec((tr, LW), lambda i: (i, XC // LW)), pl.BlockSpec((tr, LW), lambda i: (i, YC // LW)),
                  full((4, LW)), full((4, BLK, BLK)), full((4, BLK, BLK)), full((8, LW))],
        out_specs=[row, pl.BlockSpec((LW, tr), lambda i: (0, i)), row],
        out_shape=[jax.ShapeDtypeStruct((T, LW), BF16), jax.ShapeDtypeStruct((LW, T), BF16), jax.ShapeDtypeStruct((T, LW), F32)],
        scratch_shapes=[pltpu.VMEM((tr + 8, LW), F32), pltpu.VMEM((tr, LW), F32), pltpu.VMEM((tr, LW), F32),
                        pltpu.VMEM((1, LW), F32)],
        compiler_params=_cp("arbitrary"), name=name)(proj, proj, cw, wr, wi, vec)


def _lru_bwd(proj, hs, doc, cw, wr, wi, vec, name):
    T = proj.shape[0]
    tr = _rt(T)
    nt = T // tr
    r8 = tr // 8

    def body(x_ref, xp_ref, y_ref, hs_ref, hp_ref, do_ref, cw_ref, wr_ref, wi_ref, vec_ref,
             dx_ref, dy_ref, dwr_ref, dwi_ref, dvec_ref, xbuf, abuf, gbuf, hbuf, dbuf, gcar, acar):
        k = pl.program_id(0)
        i = nt - 1 - k

        @pl.when(k == 0)
        def _():
            dwr_ref[...] = jnp.zeros_like(dwr_ref)
            dwi_ref[...] = jnp.zeros_like(dwi_ref)
            dvec_ref[...] = jnp.zeros_like(dvec_ref)
            gcar[...] = jnp.zeros_like(gcar)
            acar[...] = jnp.zeros_like(acar)
            dbuf[tr:tr + 8, :] = jnp.zeros((8, LW), F32)

        first = i == 0
        x = x_ref[...]
        xbuf[0:8, :] = jnp.where(first, 0.0, xp_ref[...])
        xbuf[8:8 + tr, :] = x
        xc = _conv(xbuf, x, cw_ref, vec_ref, tr)
        r, gi, a, mult = _lru_gates(xc, wr_ref, wi_ref, vec_ref)
        y = y_ref[...]
        hs = hs_ref[...]
        do_ = do_ref[...].astype(F32)
        rows = i * tr + lax.broadcasted_iota(jnp.int32, (tr, 1), 0)
        abuf[0:tr, :] = a
        abuf[tr:tr + 8, :] = jnp.zeros((8, LW), F32) + acar[...]
        an = abuf[1:1 + tr, :]
        acar[...] = a[0:1, :]
        abuf[0:tr, :] = an
        gbuf[...] = do_ * _gelu(y)
        sub = lax.broadcasted_iota(jnp.int32, (8, 1), 0)

        def step(kk, g):
            sl = pl.ds(pl.multiple_of((r8 - 1 - kk) * 8, 8), 8)
            a8, b8 = abuf[sl, :], gbuf[sl, :]
            for s in (1, 2, 4):
                ok = sub < 8 - s
                b8 = jnp.where(ok, a8 * pltpu.roll(b8, 8 - s, 0) + b8, b8)
                a8 = jnp.where(ok, a8 * pltpu.roll(a8, 8 - s, 0), a8)
            g8 = a8 * g + b8
            gbuf[sl, :] = g8
            return g8[0:1, :]

        gcar[...] = lax.fori_loop(0, r8, step, gcar[...])
        g = gbuf[...]
        hbuf[0:8, :] = jnp.where(first, 0.0, hp_ref[...])
        hbuf[8:8 + tr, :] = hs
        hprev = hbuf[7:7 + tr, :]
        dinp = jnp.where(rows >= NPAD, g, 0.0)
        da = g * hprev
        dmult = dinp * gi * xc
        dgi = dinp * mult * xc
        dxc = dinp * mult * gi
        dlog_a = da * a - dmult * a * a / mult
        ls = _log_sigmoid(vec_ref[2:3, :])
        dpre_r = dlog_a * (LRU_C * ls) * r * (1.0 - r)
        dpre_i = dgi * gi * (1.0 - gi)
        xb = xc.astype(BF16)
        rb, ib = dpre_r.astype(BF16), dpre_i.astype(BF16)
        back = []
        for p in range(LW // BLK):
            c = slice(p * BLK, (p + 1) * BLK)
            back.append(_dot(rb[:, c], wr_ref[p], 1, 1) + _dot(ib[:, c], wi_ref[p], 1, 1))
            dwr_ref[p] += _dot(xb[:, c], rb[:, c], 0, 0)
            dwi_ref[p] += _dot(xb[:, c], ib[:, c], 0, 0)
        dxc = dxc + jnp.concatenate(back, axis=1)
        col = lambda v: jnp.sum(v, axis=0, keepdims=True)
        dvec_ref[0:1, :] += col(dpre_r)
        dvec_ref[1:2, :] += col(dpre_i)
        dvec_ref[2:3, :] += col(dlog_a * (LRU_C * r)) * _sigmoid(-vec_ref[2:3, :])
        dvec_ref[3:4, :] += col(dxc)
        dvec_ref[4:5, :] += col(dxc * xbuf[5:5 + tr, :])
        dvec_ref[5:6, :] += col(dxc * xbuf[6:6 + tr, :])
        dvec_ref[6:7, :] += col(dxc * xbuf[7:7 + tr, :])
        dvec_ref[7:8, :] += col(dxc * x)
        dbuf[0:tr, :] = dxc
        dxr = (cw_ref[3:4, :] * dxc + cw_ref[2:3, :] * dbuf[1:1 + tr, :] + cw_ref[1:2, :] * dbuf[2:2 + tr, :]
               + cw_ref[0:1, :] * dbuf[3:3 + tr, :])
        dbuf[tr:tr + 8, :] = dxc[0:8, :]
        dx_ref[...] = jnp.where(rows >= NPAD, dxr, 0.0).astype(BF16)
        dy_ref[...] = (do_ * hs * _gelu_grad(y)).astype(BF16)

    rev = lambda k: nt - 1 - k
    row = lambda col0: pl.BlockSpec((tr, LW), lambda k: (rev(k), col0))
    prev8 = lambda col0: pl.BlockSpec((8, LW), lambda k: (jnp.maximum(rev(k) * r8 - 1, 0), col0))
    full = lambda shape: pl.BlockSpec(shape, lambda k: (0,) * len(shape))
    return pl.pallas_call(
        body, grid=(nt,),
        in_specs=[row(XC // LW), prev8(XC // LW), row(YC // LW), row(0), prev8(0), row(0),
                  full((4, LW)), full((4, BLK, BLK)), full((4, BLK, BLK)), full((8, LW))],
        out_specs=[row(0), row(0), full((4, BLK, BLK)), full((4, BLK, BLK)), full((8, LW))],
        out_shape=[jax.ShapeDtypeStruct((T, LW), BF16), jax.ShapeDtypeStruct((T, LW), BF16),
                   jax.ShapeDtypeStruct((4, BLK, BLK), F32), jax.ShapeDtypeStruct((4, BLK, BLK), F32),
                   jax.ShapeDtypeStruct((8, LW), F32)],
        scratch_shapes=[pltpu.VMEM((tr + 8, LW), F32), pltpu.VMEM((tr + 8, LW), F32), pltpu.VMEM((tr, LW), F32),
                        pltpu.VMEM((tr + 8, LW), F32), pltpu.VMEM((tr + 8, LW), F32),
                        pltpu.VMEM((1, LW), F32), pltpu.VMEM((1, LW), F32)],
        compiler_params=_cp("arbitrary"), name=name)(proj, proj, proj, hs, hs, doc, cw, wr, wi, vec)


def _branch_merge_fwd(oa, of, oc, wb, proj, name):
    T = proj.shape[0]
    tm, tn = _rt(T), 512

    def body(a0, a1, a2, w_ref, g0, g1, g2, r0, r1, r2, m_ref, mt_ref):
        acc = None
        for g, (a_ref, g_ref, r_ref) in enumerate(((a0, g0, r0), (a1, g1, r1), (a2, g2, r2))):
            b = _dot(a_ref[...], w_ref[g], 1, 0)
            r_ref[...] = b
            term = _sigmoid(g_ref[...]) * b
            acc = term if acc is None else acc + term
        m = acc.astype(BF16)
        m_ref[...] = m
        mt_ref[...] = m.T

    act = pl.BlockSpec((tm, LW), lambda j, i: (i, 0))
    gate = lambda g: pl.BlockSpec((tm, tn), lambda j, i: (i, (GT + g * D) // tn + j))
    blk = pl.BlockSpec((tm, tn), lambda j, i: (i, j))
    return pl.pallas_call(
        body, grid=(D // tn, T // tm),
        in_specs=[act, act, act, pl.BlockSpec((3, LW, tn), lambda j, i: (0, 0, j)), gate(0), gate(1), gate(2)],
        out_specs=[blk] * 4 + [pl.BlockSpec((tn, tm), lambda j, i: (j, i))],
        out_shape=[jax.ShapeDtypeStruct((T, D), F32)] * 3 + [jax.ShapeDtypeStruct((T, D), BF16), jax.ShapeDtypeStruct((D, T), BF16)],
        compiler_params=_cp("parallel", "parallel"), name=name)(oa, of, oc, wb, proj, proj, proj)


def _out_dx_merge_bwd(dhb, w_out, proj, b0, b1, b2, name):
    T = proj.shape[0]
    tm, tn = _rt(T), 512

    def body(dh_ref, w_ref, g0, g1, g2, r0, r1, r2, d0, d1, d2, e0, e1, e2):
        dmv = _dot(dh_ref[...], w_ref[...], 1, 1)
        for g_ref, r_ref, d_ref, e_ref in ((g0, r0, d0, e0), (g1, r1, d1, e1), (g2, r2, d2, e2)):
            sg = _sigmoid(g_ref[...])
            d_ref[...] = (dmv * sg).astype(BF16)
            e_ref[...] = (dmv * r_ref[...] * sg * (1.0 - sg)).astype(BF16)

    gate = lambda g: pl.BlockSpec((tm, tn), lambda j, i: (i, (GT + g * D) // tn + j))
    blk = pl.BlockSpec((tm, tn), lambda j, i: (i, j))
    return pl.pallas_call(
        body, grid=(D // tn, T // tm),
        in_specs=[pl.BlockSpec((tm, D), lambda j, i: (i, 0)), pl.BlockSpec((tn, D), lambda j, i: (j, 0)),
                  gate(0), gate(1), gate(2), blk, blk, blk],
        out_specs=[blk] * 6, out_shape=[jax.ShapeDtypeStruct((T, D), BF16)] * 6,
        compiler_params=_cp("parallel", "parallel"), name=name)(dhb, w_out, proj, proj, proj, b0, b1, b2)


def _ffn_in_swiglu_fwd(u, w, name):
    T = u.shape[0]
    tm, tn = _rt(T), _pick(DFF, (1408, 256))
    nj = DFF // tn

    def body(u_ref, wg_ref, wu_ref, g_ref, up_ref, a_ref, at_ref):
        ub = u_ref[...]
        g = _dot(ub, wg_ref[...], 1, 0)
        up = _dot(ub, wu_ref[...], 1, 0)
        g_ref[...] = g
        up_ref[...] = up
        a = (g * _sigmoid(g) * up).astype(BF16)
        a_ref[...] = a
        at_ref[...] = a.T

    blk = pl.BlockSpec((tm, tn), lambda j, i: (i, j))
    return pl.pallas_call(
        body, grid=(nj, T // tm),
        in_specs=[pl.BlockSpec((tm, D), lambda j, i: (i, 0)), pl.BlockSpec((D, tn), lambda j, i: (0, j)),
                  pl.BlockSpec((D, tn), lambda j, i: (0, j + nj))],
        out_specs=[blk] * 3 + [pl.BlockSpec((tn, tm), lambda j, i: (j, i))],
        out_shape=[jax.ShapeDtypeStruct((T, DFF), F32)] * 2 + [jax.ShapeDtypeStruct((T, DFF), BF16),
                                                               jax.ShapeDtypeStruct((DFF, T), BF16)],
        compiler_params=_cp("parallel", "parallel"), name=name)(u, w, w)


def _ffn_out_dx_swiglu_bwd(dhb, w, gate, up, name):
    T = dhb.shape[0]
    tm, tn = _rt(T), _pick(DFF, (1408, 256))

    def body(dh_ref, w_ref, g_ref, up_ref, dg_ref, du_ref):
        d = _dot(dh_ref[...], w_ref[...], 1, 1)
        g = g_ref[...]
        sg = _sigmoid(g)
        dg_ref[...] = (d * up_ref[...] * (sg + g * sg * (1.0 - sg))).astype(BF16)
        du_ref[...] = (d * g * sg).astype(BF16)

    blk = pl.BlockSpec((tm, tn), lambda j, i: (i, j))
    return pl.pallas_call(
        body, grid=(DFF // tn, T // tm),
        in_specs=[pl.BlockSpec((tm, D), lambda j, i: (i, 0)), pl.BlockSpec((tn, D), lambda j, i: (j, 0)), blk, blk],
        out_specs=[blk] * 2, out_shape=[jax.ShapeDtypeStruct((T, DFF), BF16)] * 2,
        compiler_params=_cp("parallel", "parallel"), name=name)(dhb, w, gate, up)


def _adamw(w, g, m, v, name):
    R, C = w.shape
    tr = _pick(R, tuple(t for t in (512, 256, 128, 64, 32, 16, 8) if t * C * 4 <= (3 << 19)))
    c1 = 1.0 - ADAM_B1 ** ADAM_STEP
    c2 = 1.0 - ADAM_B2 ** ADAM_STEP

    def body(w_ref, g_ref, m_ref, v_ref, d_ref, mo_ref, vo_ref):
        gv = g_ref[...]
        mn = ADAM_B1 * m_ref[...] + (1.0 - ADAM_B1) * gv
        vn = ADAM_B2 * v_ref[...] + (1.0 - ADAM_B2) * (gv * gv)
        d_ref[...] = -ADAM_LR * ((mn / c1) / (jnp.sqrt(vn / c2) + ADAM_EPS) + ADAM_WD * w_ref[...])
        mo_ref[...] = mn
        vo_ref[...] = vn

    blk = pl.BlockSpec((tr, C), lambda i: (i, 0))
    return pl.pallas_call(
        body, grid=(R // tr,), in_specs=[blk] * 4, out_specs=[blk] * 3,
        out_shape=[jax.ShapeDtypeStruct((R, C), F32)] * 3, compiler_params=_cp("parallel"), name=name)(w, g, m, v)


def _sum_lead(x, name):
    n, R, C = x.shape
    tr = _pick(R, (512, 256, 128, 64, 32, 16, 8))

    def body(x_ref, o_ref):
        acc = x_ref[0]
        for d in range(1, n):
            acc = acc + x_ref[d]
        o_ref[...] = acc

    return pl.pallas_call(
        body, grid=(R // tr,), in_specs=[pl.BlockSpec((n, tr, C), lambda i: (0, i, 0))],
        out_specs=pl.BlockSpec((tr, C), lambda i: (i, 0)), out_shape=jax.ShapeDtypeStruct((R, C), F32),
        compiler_params=_cp("parallel"), name=name)(x)


def _here():
    return lax.axis_index("x"), lax.axis_index("y"), lax.axis_index("c")


def _rcopy(src, dst, send_sems, recv_sems, k, to):
    return pltpu.make_async_remote_copy(src_ref=src, dst_ref=dst, send_sem=send_sems.at[k], recv_sem=recv_sems.at[k],
                                        device_id=to, device_id_type=MESH)


def _hbm_calls(body, args, out_shapes, n_sems, aliases, name):
    return pl.pallas_call(
        body, in_specs=[ANY] * len(args), out_specs=[ANY] * len(out_shapes), out_shape=out_shapes,
        input_output_aliases=aliases,
        scratch_shapes=[pltpu.SemaphoreType.DMA((n_sems,)), pltpu.SemaphoreType.DMA((n_sems,))],
        compiler_params=pltpu.CompilerParams(has_side_effects=True), name=name)(*args)


def _gather_plan(outs, axes, send_sems, recv_sems):
    x, y, c = _here()
    sib = (x, y, 1 - c)
    chips = [(1 - x, y), (x, 1 - y), (1 - x, 1 - y)]
    todo = [(t, k, chip) for t in range(len(outs)) for k, chip in enumerate(chips)]

    def win(t, chip, hc):
        o, ax = outs[t], axes[t]
        w = o.shape[ax] // N_SHARD
        first = (2 * chip[0] + chip[1]) * w
        if ax == 0:
            return o.at[pl.ds(pl.multiple_of(first + hc * (w // 2), 16), w // 2), :]
        rows = o.shape[0] // 2
        return o.at[pl.ds(pl.multiple_of(hc * rows, 16), rows), pl.ds(pl.multiple_of(first, BLK), w)]

    def copy(t, k, chip, hc, to):
        return _rcopy(win(t, chip, hc), win(t, chip, hc), send_sems, recv_sems, 6 * t + k, to)

    def start():
        for t, k, chip in todo:
            copy(t, k, (x, y), c, (*chip, c)).start()

    def finish():
        for t, k, chip in todo:
            copy(t, k, chip, c, (*chip, c)).wait_recv()
            copy(t, 3 + k, chip, c, sib).start()
        for t, k, chip in todo:
            copy(t, 3 + k, chip, 1 - c, sib).wait_recv()
        for t, k, chip in todo:
            copy(t, k, (x, y), c, (*chip, c)).wait_send()
            copy(t, 3 + k, chip, c, sib).wait_send()

    return start, finish


def _all_gather_weights(fulls, axes, name):
    nt = len(fulls)

    def body(*refs):
        start, finish = _gather_plan(refs[nt:2 * nt], axes, *refs[2 * nt:])
        start()
        finish()

    return _hbm_calls(body, fulls, [jax.ShapeDtypeStruct(f.shape, f.dtype) for f in fulls], 6 * nt,
                      {t: t for t in range(nt)}, name)


def _half(ref, ax, hc):
    n = ref.shape[ax] // 2
    sl = pl.ds(pl.multiple_of(hc * n, 8), n)
    return ref.at[sl, :] if ax == 0 else ref.at[:, sl]


def _shrunk(shape, ax, by):
    shape = list(shape)
    shape[ax] //= by
    return tuple(shape)


def _swap_halves(gs, haxes, name):
    nt = len(gs)

    def body(*refs):
        ins, outs, (send_sems, recv_sems) = refs[:nt], refs[nt:2 * nt], refs[2 * nt:]
        x, y, c = _here()
        cps = [_rcopy(_half(g, ax, 1 - c), o, send_sems, recv_sems, t, (x, y, 1 - c))
               for t, (g, o, ax) in enumerate(zip(ins, outs, haxes))]
        for cp in cps:
            cp.start()
        for cp in cps:
            cp.wait()

    return _hbm_calls(body, gs, [jax.ShapeDtypeStruct(_shrunk(g.shape, ax, 2), g.dtype) for g, ax in zip(gs, haxes)],
                      nt, {}, name)


def _scatter_plan(ins, outs, saxes, send_sems, recv_sems):
    x, y, c = _here()
    chips = [(1 - x, y), (x, 1 - y), (1 - x, 1 - y)]

    def copies():
        cps = []
        for t, (s, o, ax) in enumerate(zip(ins, outs, saxes)):
            w = s.shape[ax] // N_SHARD
            for k, chip in enumerate(chips):
                first = pl.multiple_of((2 * chip[0] + chip[1]) * w, 8)
                src = s.at[pl.ds(first, w), :] if ax == 0 else s.at[:, pl.ds(first, w)]
                cps.append(_rcopy(src, o.at[k], send_sems, recv_sems, 3 * t + k, (*chip, c)))
        return cps

    def start():
        for cp in copies():
            cp.start()

    def finish():
        for cp in copies():
            cp.wait()

    return start, finish


def _scatter_shapes(sbs, saxes):
    return [jax.ShapeDtypeStruct((3,) + _shrunk(s.shape, ax, N_SHARD), s.dtype) for s, ax in zip(sbs, saxes)]


def _small_gather_copies(out_ref, send_sems, recv_sems, k0):
    x, y, c = _here()
    me = 4 * x + 2 * y + c
    cps = []
    for k in range(1, 8):
        to = (x ^ ((k >> 2) & 1), y ^ ((k >> 1) & 1), c ^ (k & 1))
        peer = 4 * to[0] + 2 * to[1] + to[2]
        cps.append((_rcopy(out_ref.at[me], out_ref.at[me], send_sems, recv_sems, k0 + k - 1, to),
                    _rcopy(out_ref.at[peer], out_ref.at[peer], send_sems, recv_sems, k0 + k - 1, to)))
    return cps


def _scatter_to_chips(sbs, saxes, small, name):
    nt = len(sbs)

    def body(*refs):
        small_out = refs[2 * nt + 1]
        send_sems, recv_sems = refs[2 * nt + 2:]
        start, finish = _scatter_plan(refs[:nt], refs[nt + 1:2 * nt + 1], saxes, send_sems, recv_sems)
        start()
        cps = _small_gather_copies(small_out, send_sems, recv_sems, 3 * nt)
        for snd, _ in cps:
            snd.start()
        for _, rcv in cps:
            rcv.wait_recv()
        for snd, _ in cps:
            snd.wait_send()
        finish()

    outs = _hbm_calls(body, list(sbs) + [small], _scatter_shapes(sbs, saxes) + [jax.ShapeDtypeStruct(small.shape, small.dtype)],
                      3 * nt + 7, {nt: nt}, name)
    return outs[:nt], outs[nt]


def _join_halves(fins, haxes, name):
    nt = len(fins)

    def body(*refs):
        outs, (send_sems, recv_sems) = refs[nt:2 * nt], refs[2 * nt:]
        x, y, c = _here()
        cps = [_rcopy(_half(o, ax, c), _half(o, ax, c), send_sems, recv_sems, t, (x, y, 1 - c))
               for t, (o, ax) in enumerate(zip(outs, haxes))]
        for cp in cps:
            cp.start()
        for t, (o, ax) in enumerate(zip(outs, haxes)):
            _rcopy(_half(o, ax, 1 - c), _half(o, ax, 1 - c), send_sems, recv_sems, t, (x, y, 1 - c)).wait_recv()
        for cp in cps:
            cp.wait_send()

    return _hbm_calls(body, fins, [jax.ShapeDtypeStruct(f.shape, f.dtype) for f in fins], nt, {t: t for t in range(nt)}, name)


def _all_gather_small(buf, name):
    def body(_, out_ref, send_sems, recv_sems):
        cps = _small_gather_copies(out_ref, send_sems, recv_sems, 0)
        for snd, _ in cps:
            snd.start()
        for _, rcv in cps:
            rcv.wait_recv()
        for snd, _ in cps:
            snd.wait_send()

    return _hbm_calls(body, [buf], [jax.ShapeDtypeStruct(buf.shape, buf.dtype)], 7, {0: 0}, name)[0]


def _place(block, n, index):
    buf = jnp.zeros((n,) + block.shape[1:], block.dtype)
    return lax.dynamic_update_slice_in_dim(buf, block, index, axis=0)


def _add_half(g, other, hax, cidx, name):
    r, cw = other.shape
    tr = _pick(r, tuple(t for t in (512, 256, 128, 64, 32, 16, 8) if t * cw * 4 <= (1 << 21)))
    nr = r // tr

    def body(c_ref, g_ref, o_ref, s_ref, sb_ref):
        s = g_ref[...] + o_ref[...]
        s_ref[...] = s
        sb_ref[...] = s.astype(BF16)

    g_map = (lambda i, c: (c[0] * nr + i, 0)) if hax == 0 else (lambda i, c: (i, c[0]))
    blk = pl.BlockSpec((tr, cw), lambda i, c: (i, 0))
    return pl.pallas_call(
        body,
        grid_spec=pltpu.PrefetchScalarGridSpec(
            num_scalar_prefetch=1, grid=(nr,), in_specs=[pl.BlockSpec((tr, cw), g_map), blk], out_specs=[blk, blk]),
        out_shape=[jax.ShapeDtypeStruct((r, cw), F32), jax.ShapeDtypeStruct((r, cw), BF16)],
        compiler_params=_cp("parallel"), name=name)(cidx, g, other)


def _add_chips(s, recv, sax, chip_idx, cidx, name):
    _, r, cw = recv.shape
    tr = _pick(r, tuple(t for t in (512, 352, 256, 128, 64, 32, 16, 8) if t * cw * 4 <= (1 << 21)))
    nr = r // tr

    def body(chip_ref, c_ref, s_ref, r_ref, out_ref):
        out_ref[...] = ((s_ref[...] + r_ref[0].astype(F32)) + r_ref[1].astype(F32)) + r_ref[2].astype(F32)

    if sax == 0:
        s_map, o_map, shape = (lambda i, chip, c: (chip[0] * nr + i, 0)), (lambda i, chip, c: (i, c[0])), (r, 2 * cw)
    else:
        s_map, o_map, shape = (lambda i, chip, c: (i, chip[0])), (lambda i, chip, c: (c[0] * nr + i, 0)), (2 * r, cw)
    return pl.pallas_call(
        body,
        grid_spec=pltpu.PrefetchScalarGridSpec(
            num_scalar_prefetch=2, grid=(nr,),
            in_specs=[pl.BlockSpec((tr, cw), s_map), pl.BlockSpec((3, tr, cw), lambda i, chip, c: (0, i, 0))],
            out_specs=pl.BlockSpec((tr, cw), o_map)),
        out_shape=jax.ShapeDtypeStruct(shape, F32), compiler_params=_cp("parallel"), name=name)(chip_idx, cidx, s, recv)


IN_SHARD = IN_COLS // N_SHARD
IN_SLOT = INP // N_SHARD
IN_PIECES = ((0, 512, QA), (512, 640, KA), (640, 768, VA), (768, 1280, QF), (1280, 1792, KF), (1792, 2304, VF),
             (2304, 2312, FL), (2312, 2824, XC), (2824, 3336, YC), (3336, 6408, GT))


def _gathered_to_kernel_cols(w):
    parts, pos = [], 0
    for a, b, k in sorted(IN_PIECES, key=lambda p: p[2]):
        assert k == pos
        while a < b:
            j = a // IN_SHARD
            e = min(b, (j + 1) * IN_SHARD)
            g = j * IN_SLOT + a - j * IN_SHARD
            parts.append(w[..., g:g + e - a])
            pos += e - a
            a = e
    parts.append(jnp.zeros(w.shape[:-1] + (INP - pos,), w.dtype))
    return jnp.concatenate(parts, axis=-1)


def _kernel_to_gathered_cols(w):
    parts = []
    for j in range(N_SHARD):
        lo, hi = j * IN_SHARD, (j + 1) * IN_SHARD
        for a, b, k in IN_PIECES:
            s, e = max(a, lo), min(b, hi)
            if s < e:
                parts.append(w[..., k + s - a:k + e - a])
        parts.append(jnp.zeros(w.shape[:-1] + (IN_SLOT - IN_SHARD,), w.dtype))
    return jnp.concatenate(parts, axis=-1)


def _pair_blocks(w):
    z = jnp.zeros((4, 64, 64), w.dtype)
    w = w.reshape(4, 2, 64, 64)
    top = jnp.concatenate([w[:, 0], z], axis=2)
    bot = jnp.concatenate([z, w[:, 1]], axis=2)
    return jnp.concatenate([top, bot], axis=1)


def _unpair_blocks(w):
    return jnp.stack([w[:, :64, :64], w[:, 64:, 64:]], axis=1).reshape(8, 64, 64)


BIG = ("w_in", "w_branch", "w_out", "w_ffn_in", "w_ffn_out")
TINY = ("conv_w", "meta_tokens")
SMALL = ("rel_bias_table", "norm_mix", "swa_sinks", "fox_forget_bias", "conv_b", "lru_w_r", "lru_b_r", "lru_w_i",
         "lru_b_i", "lru_lambda", "norm_ffn", "norm_final")
SHARD_AXIS = {"conv_w": 2, "meta_tokens": 1}
BIG_AXIS = {"w_in": 2, "w_branch": 2, "w_out": 1, "w_ffn_in": 2, "w_ffn_out": 1}


def _pack(d, names):
    flat = jnp.concatenate([d[n].reshape(-1) for n in names])
    pad = (-flat.shape[0]) % (256 * 128)
    return jnp.concatenate([flat, jnp.zeros((pad,), F32)]).reshape(-1, 128)


def _unpack(buf, names, shapes):
    flat, out, off = buf.reshape(-1), {}, 0
    for n in names:
        sz = int(np.prod(shapes[n]))
        out[n] = flat[off:off + sz].reshape(shapes[n])
        off += sz
    return out


def _layer_layout(n, a):
    if n == "w_in":
        return _gathered_to_kernel_cols(a)
    return a.reshape(3, LW, D) if n == "w_branch" else a


def _local_step(x, tgt, W, placed=None):
    S = x.shape[0]
    T = S + BLK
    tm = _pick(T, (1408, 384, 128))
    bucket = jnp.asarray(_bucket_table())
    bias = _bias_build(W["rel_bias_table"], bucket, "bias_build")
    h = jnp.concatenate([jnp.zeros((NPAD, D), F32), W["meta_tokens"], x], axis=0)
    if placed is None:
        WL = {n: [W[n][l] for l in range(DEPTH)] for n in BIG}
    else:
        WL = {n: [W[n]] + [None] * (DEPTH - 1) for n in BIG}

    saved = []
    for l in range(DEPTH):
        sv = {"h0": h}
        u, u_t = _rms_fwd(h, W["norm_mix"][l], "rms_mix_fwd")
        proj = _mm(u, WL["w_in"][l], tm=tm, tn=512, tk=D, name="mm_in_fwd")
        oa, oa_t = _swa_fwd(proj, bias, W["swa_sinks"][l], "swa_fwd")
        fb = W["fox_forget_bias"][l].reshape(NH, 1)
        qaug, kaug, kaug_t, vm_t, vo = _fox_prep(proj, _cum_fwd(proj, fb, "cum_fwd"), "fox_prep")
        if placed is not None and l + 1 < DEPTH:
            of, of_t, lse0, lse1, *got = _fox_fwd(qaug, kaug_t, vo, "fox_fwd_gather", gather=(placed[l + 1], GATHER_AXES))
            for n, a in zip(BIG, got):
                WL[n][l + 1] = _layer_layout(n, a)
            lse = [lse0, lse1]
        else:
            of, of_t, *lse = _fox_fwd(qaug, kaug_t, vo, "fox_fwd")
        lru_vec = jnp.concatenate([W["lru_b_r"][l][None], W["lru_b_i"][l][None], W["lru_lambda"][l][None],
                                   W["conv_b"][l][None], jnp.zeros((4, LW), F32)], axis=0)
        oc, oc_t, hs = _lru_fwd(proj, W["conv_w"][l], W["lru_w_r"][l], W["lru_w_i"][l], lru_vec, "lru_fwd")
        *bs, merged, merged_t = _branch_merge_fwd(oa, of, oc, WL["w_branch"][l], proj, "branch_merge_fwd")
        h2 = _mm(merged, WL["w_out"][l], res=h, tm=tm, tn=512, tk=D, name="mm_out_fwd")
        u2, u2_t = _rms_fwd(h2, W["norm_ffn"][l], "rms_ffn_fwd")
        gate, up, act, act_t = _ffn_in_swiglu_fwd(u2, WL["w_ffn_in"][l], "ffn_in_swiglu_fwd")
        h = _mm(act, WL["w_ffn_out"][l], res=h2, tm=tm, tn=512, tk=DFF, name="mm_ffn_out_fwd")
        sv.update(u_t=u_t, proj=proj, o_t=(oa_t, of_t, oc_t), of=of, lse=lse, hs=hs, fb=fb, qaug=qaug, kaug=kaug, kaug_t=kaug_t, vm_t=vm_t,
                  lru_vec=lru_vec, bs=bs, merged_t=merged_t, h2=h2, u2_t=u2_t, gate=gate, up=up, act_t=act_t)
        saved.append(sv)

    dh, dhb, dg_final, loss_vec = _loss_head(h, tgt, W["norm_final"], "loss_head")
    loss = loss_vec[0, 0]

    small = ("norm_mix", "swa_sinks", "fox_forget_bias", "conv_w", "conv_b", "lru_w_r", "lru_b_r", "lru_w_i", "lru_b_i",
             "lru_lambda", "norm_ffn")
    G = {n: [None] * DEPTH for n in small}
    G["norm_final"] = dg_final.reshape(D)
    GW = {n: [None] * DEPTH for n in BIG}
    dist = placed is not None
    if dist:
        x_, y_, c_ = _here()
        cidx = jnp.reshape(c_, (1,)).astype(jnp.int32)
        chip = jnp.reshape(2 * x_ + y_, (1,)).astype(jnp.int32)

    def finish_layer(lp, ss, recv):
        fins = [_add_chips(s, r, ax, chip, cidx, "rs_add_chips_" + n) for n, s, r, ax in zip(BIG, ss, recv, GATHER_AXES)]
        for n, f in zip(BIG, _join_halves(fins, HALF_AXES, "rs_join_halves")):
            GW[n][lp] = f

    pend = None
    dbias = jnp.zeros((NH, BLK, 2 * BLK), F32)
    tkT = tm
    for l in reversed(range(DEPTH)):
        sv = saved[l]
        dw = {}
        dw["w_ffn_out"] = _mm(sv["act_t"], dhb, tm=_pick(DFF, (1408, 256)), tn=D, tk=tkT,
                              name="mm_ffn_out_dw")
        dgate, dup = _ffn_out_dx_swiglu_bwd(dhb, WL["w_ffn_out"][l], sv["gate"], sv["up"], "ffn_out_dx_swiglu_bwd")
        u2t = sv["u2_t"]
        du2, buf = None, None
        for half, dpart in enumerate((dgate, dup)):
            buf = _mm(u2t, dpart, tm=D, tn=_pick(DFF, (1408, 256)), tk=tkT, slab=(buf, 0, 1),
                      col0=half * DFF, cols=2 * DFF, name="mm_ffn_in_dw")
            du2 = _mm(dpart, WL["w_ffn_in"][l], tb=True, res=du2, b_k0=half * DFF, tm=tm, tn=512, tk=DFF,
                      name="mm_ffn_in_dx")
        dw["w_ffn_in"] = buf.reshape(D, 2 * DFF)
        dh, dhb, dgn = _rms_bwd(du2, sv["h2"], W["norm_ffn"][l], dh, "rms_ffn_bwd")
        G["norm_ffn"][l] = dgn.reshape(D)
        dw["w_out"] = _mm(sv["merged_t"], dhb, tm=D, tn=D, tk=tkT, name="mm_out_dw")
        db0, db1, db2, dg0, dg1, dg2 = _out_dx_merge_bwd(dhb, WL["w_out"][l], sv["proj"], *sv["bs"], "out_dx_merge_bwd")
        dos, buf = [], None
        for g, (o_t, db) in enumerate(zip(sv["o_t"], (db0, db1, db2))):
            buf = _mm(o_t, db, tm=LW, tn=D, tk=tkT, slab=(buf, g, 3), name="mm_branch_dw")
            dos.append(_mm(db, WL["w_branch"][l][g], tb=True, out_dtype=BF16, tm=tm, tn=LW, tk=D, name="mm_branch_dx"))
        dw["w_branch"] = buf.reshape(3 * LW, D)
        dqa, dkb, dvb, dbias, dsk = _swa_bwd(sv["proj"], bias, W["swa_sinks"][l], dos[0], dbias, "swa_bwd")
        dka, dva = _band_fold(dkb, dvb, "swa_band_fold")
        G["swa_sinks"][l] = dsk[0, :NH]
        delta = _fox_delta(dos[1], sv["of"], "fox_delta")
        fox_args = (sv["qaug"], sv["kaug"], sv["kaug_t"], sv["vm_t"], dos[1], sv["lse"], delta)
        if pend is not None:
            dqf, dqx, dkf, dvf, dkx, *recv = _fox_bwd(*fox_args, "fox_bwd_scatter", scatter=(pend[2], GATHER_AXES))
            finish_layer(pend[0], pend[1], recv)
            pend = None
        else:
            dqf, dqx, dkf, dvf, dkx = _fox_bwd(*fox_args, "fox_bwd")
        dfl, dfb = _cum_bwd(dqx, dkx, sv["proj"], sv["fb"], "cum_bwd")
        G["fox_forget_bias"][l] = dfb[:, 0]
        dxc, dyc, dwr, dwi, dvec = _lru_bwd(sv["proj"], sv["hs"], dos[2], W["conv_w"][l], W["lru_w_r"][l], W["lru_w_i"][l],
                                            sv["lru_vec"], "lru_bwd")
        G["lru_w_r"][l], G["lru_w_i"][l] = _unpair_blocks(dwr), _unpair_blocks(dwi)
        G["lru_b_r"][l], G["lru_b_i"][l], G["lru_lambda"][l], G["conv_b"][l] = dvec[0], dvec[1], dvec[2], dvec[3]
        G["conv_w"][l] = dvec[4:8]
        dproj = jnp.concatenate([dqa, dqf, dkf, dvf, dxc, dyc, dg0, dg1, dg2, dka, dva, dfl], axis=1)
        dw["w_in"] = _mm(sv["u_t"], dproj, tm=D, tn=IN_SLOT, tk=tkT, name="mm_in_dw")
        du = _mm(dproj, WL["w_in"][l], tb=True, tm=tm, tn=512, tk=_pick(INP, (3328, 512)), name="mm_in_dx")
        dh, dhb, dgn = _rms_bwd(du, sv["h0"], W["norm_mix"][l], dh, "rms_mix_bwd")
        G["norm_mix"][l] = dgn.reshape(D)
        if dist:
            gs = [dw[n] for n in BIG]
            pairs = [_add_half(g, r, hax, cidx, "rs_add_half_" + n)
                     for n, g, r, hax in zip(BIG, gs, _swap_halves(gs, HALF_AXES, "rs_swap_halves"), HALF_AXES)]
            ss, sbs = [list(t) for t in zip(*pairs)]
            ss[0], sbs[0] = _kernel_to_gathered_cols(ss[0]), _kernel_to_gathered_cols(sbs[0])
            pend = (l, ss, sbs)
        else:
            for n in BIG:
                GW[n][l] = dw[n]
    grads = {n: (jnp.stack(v) if isinstance(v, list) else v) for n, v in G.items()}
    grads["rel_bias_table"] = _bias_bwd(dbias, bucket, "bias_bwd")
    grads["meta_tokens"] = dh[NPAD:BLK]
    if dist:
        mine = _place(_pack(grads, SMALL + TINY)[None], 8, 4 * x_ + 2 * y_ + c_)
        recv, grads["small_gathered"] = _scatter_to_chips(pend[2], GATHER_AXES, mine, "rs_scatter")
        finish_layer(pend[0], pend[1], recv)
    grads.update({n: jnp.stack(GW[n]) for n in BIG})
    return loss, dh[BLK:], grads


NAMES = ("meta_tokens", "rel_bias_table", "norm_mix", "w_in", "swa_sinks", "fox_forget_bias", "conv_w", "conv_b",
         "lru_w_r", "lru_b_r", "lru_w_i", "lru_b_i", "lru_lambda", "w_branch", "w_out", "norm_ffn", "w_ffn_in",
         "w_ffn_out", "norm_final")


def _three_d(n, a):
    return a.reshape(DEPTH, 3 * LW, -1) if n == "w_branch" else a


GATHER_AXES = [BIG_AXIS[n] - 1 for n in BIG]
HALF_AXES = [1 - a for a in GATHER_AXES]


def _gather_weights(P):
    x, y, c = _here()
    mine, me = 2 * x + y, 4 * x + 2 * y + c
    placed = []
    for l in range(DEPTH):
        bufs = []
        for n in BIG:
            shard = _three_d(n, P[n])[l].astype(BF16)
            if n == "w_in":
                shard = jnp.pad(shard, ((0, 0), (0, IN_SLOT - IN_SHARD)))
            zero = jnp.zeros_like(shard)
            bufs.append(jnp.concatenate([jnp.where(mine == j, shard, zero) for j in range(N_SHARD)], axis=BIG_AXIS[n] - 1))
        placed.append(bufs)
    full = {n: _layer_layout(n, a) for n, a in zip(BIG, _all_gather_weights(placed[0], GATHER_AXES, "ag_weights"))}
    tiny = _all_gather_small(_place(_pack(P, TINY)[None], 8, me), "ag_tiny_weights")
    parts = [_unpack(tiny[2 * j], TINY, {n: P[n].shape for n in TINY}) for j in range(N_SHARD)]
    for n in TINY:
        full[n] = jnp.concatenate([p[n] for p in parts], axis=SHARD_AXIS[n])
    for n in SMALL:
        full[n] = P[n]
    full["lru_w_r"] = jnp.stack([_pair_blocks(P["lru_w_r"][l]) for l in range(DEPTH)]).astype(BF16)
    full["lru_w_i"] = jnp.stack([_pair_blocks(P["lru_w_i"][l]) for l in range(DEPTH)]).astype(BF16)
    return full, placed


def _reduce_grads(grads, P):
    x, y, c = _here()
    mine, me = 2 * x + y, 4 * x + 2 * y + c
    out = {n: grads[n].reshape(P[n].shape) for n in BIG if n != "w_in"}
    out["w_in"] = grads["w_in"][:, :, :IN_SHARD]
    names = SMALL + TINY
    small = _unpack(_sum_lead(grads["small_gathered"], "sum_small_grads"), names, {n: grads[n].shape for n in names})
    for n in SMALL:
        out[n] = small[n]
    for n in TINY:
        w = P[n].shape[SHARD_AXIS[n]]
        out[n] = lax.dynamic_slice_in_dim(small[n], mine * w, w, axis=SHARD_AXIS[n])
    return out


def _update(P, Gd, M, V):
    delta, new_m, new_v = {}, {}, {}
    for n in BIG + TINY:
        shp = P[n].shape
        two = (int(np.prod(shp[:-1])), shp[-1])
        d, m, v = _adamw(P[n].reshape(two), Gd[n].reshape(two), M[n].reshape(two), V[n].reshape(two), "adamw_" + n)
        delta[n], new_m[n], new_v[n] = d.reshape(shp), m.reshape(shp), v.reshape(shp)
    shapes = {n: P[n].shape for n in SMALL}
    d, m, v = _adamw(_pack(P, SMALL), _pack(Gd, SMALL), _pack(M, SMALL), _pack(V, SMALL), "adamw_small")
    for dst, buf in ((delta, d), (new_m, m), (new_v, v)):
        dst.update(_unpack(buf, SMALL, shapes))
    return delta, new_m, new_v


def kernel(x, meta_tokens, rel_bias_table, norm_mix, w_in, swa_sinks, fox_forget_bias, conv_w, conv_b, lru_w_r, lru_b_r, lru_w_i, lru_b_i, lru_lambda, w_branch, w_out, norm_ffn, w_ffn_in, w_ffn_out, norm_final, loss_target, m_meta_tokens, m_rel_bias_table, m_norm_mix, m_w_in, m_swa_sinks, m_fox_forget_bias, m_conv_w, m_conv_b, m_lru_w_r, m_lru_b_r, m_lru_w_i, m_lru_b_i, m_lru_lambda, m_w_branch, m_w_out, m_norm_ffn, m_w_ffn_in, m_w_ffn_out, m_norm_final, v_meta_tokens, v_rel_bias_table, v_norm_mix, v_w_in, v_swa_sinks, v_fox_forget_bias, v_conv_w, v_conv_b, v_lru_w_r, v_lru_b_r, v_lru_w_i, v_lru_b_i, v_lru_lambda, v_w_branch, v_w_out, v_norm_ffn, v_w_ffn_in, v_w_ffn_out, v_norm_final):
    P = dict(zip(NAMES, (meta_tokens, rel_bias_table, norm_mix, w_in, swa_sinks, fox_forget_bias, conv_w, conv_b, lru_w_r,
                         lru_b_r, lru_w_i, lru_b_i, lru_lambda, w_branch, w_out, norm_ffn, w_ffn_in, w_ffn_out, norm_final)))
    M = dict(zip(NAMES, (m_meta_tokens, m_rel_bias_table, m_norm_mix, m_w_in, m_swa_sinks, m_fox_forget_bias, m_conv_w,
                         m_conv_b, m_lru_w_r, m_lru_b_r, m_lru_w_i, m_lru_b_i, m_lru_lambda, m_w_branch, m_w_out, m_norm_ffn,
                         m_w_ffn_in, m_w_ffn_out, m_norm_final)))
    V = dict(zip(NAMES, (v_meta_tokens, v_rel_bias_table, v_norm_mix, v_w_in, v_swa_sinks, v_fox_forget_bias, v_conv_w,
                         v_conv_b, v_lru_w_r, v_lru_b_r, v_lru_w_i, v_lru_b_i, v_lru_lambda, v_w_branch, v_w_out, v_norm_ffn,
                         v_w_ffn_in, v_w_ffn_out, v_norm_final)))
    W, placed = _gather_weights(P)
    loss_local, grad_x, grads = _local_step(x[0], loss_target[0], W, placed)
    loss = lax.psum(loss_local, ("x", "y", "c"))
    Gd = _reduce_grads(grads, P)
    delta, new_m, new_v = _update(P, Gd, M, V)
    return (loss, grad_x[None], *[Gd[n] for n in NAMES], *[delta[n] for n in NAMES],
            *[new_m[n] for n in NAMES], *[new_v[n] for n in NAMES])
```
